```python
import math
import jax, jax.numpy as jnp
from jax import lax
import numpy as np

D_MODEL = 1024
BATCH = 8
SEQ = 2048
DEPTH = 2

GRID_W = 64
CTX_LEN = 256

D_MIX = D_MODEL
D_CONV = D_MODEL // 4
D_CONF = D_MODEL // 4
NA_HEAD_DIM = 64
D_NA = D_MIX - D_CONV - D_CONF
N_NA_HEADS = D_NA // NA_HEAD_DIM
SHORT_CONV_W = 3
CONF_CONV_W = 31
NA_WIN_ROWS_MAX = 8
NA_WIN_COLS = 16
N_EXPERTS = 16
EC_CAPACITY_FACTOR = 2
D_EXPERT = 1024
LN_EPS = 1e-5
DEEPNORM_ALPHA = (2.0 * DEPTH) ** 0.25
DEEPNORM_BETA = (8.0 * DEPTH) ** -0.25
NEG_INF = -1e30

OFF_A = 0
OFF_B = OFF_A + 3 * D_CONV
OFF_Q = OFF_B + 2 * D_CONF
OFF_K = OFF_Q + D_NA
OFF_V = OFF_K + D_NA
D_IN = OFF_V + D_NA

kernel_name = "hybrid_conv_conformer_natten_ecmoe_deepnorm"


def layer_norm(x, gain=None, bias=None):
    xf = x.astype(jnp.float32)
    mu = jnp.mean(xf, axis=-1, keepdims=True)
    var = jnp.mean(jnp.square(xf - mu), axis=-1, keepdims=True)
    y = (xf - mu) * lax.rsqrt(var + LN_EPS)
    if gain is not None:
        y = y * gain.astype(jnp.float32) + bias.astype(jnp.float32)
    return y.astype(x.dtype)


def modulation(cond, w_mod, b_mod):
    m = jax.nn.silu(cond) @ w_mod + b_mod
    return jnp.split(m, 6, axis=-1)


def modulate(h, shift, scale):
    return h * (1 + scale) + shift


def depthwise_conv(x, w):
    pad = w.shape[0] // 2
    return lax.conv_general_dilated(
        x, w[:, None, :].astype(x.dtype), window_strides=(1,), padding=[(pad, pad)],
        dimension_numbers=('NWC', 'WIO', 'NWC'), feature_group_count=x.shape[-1])


def short_conv_mixer(u, w_short):
    bg, cg, xv = jnp.split(u, 3, axis=-1)
    return bg * depthwise_conv(cg * xv, w_short)


def conformer_conv_mixer(u, w_dw, b_dw, g_ln, b_ln):
    a, g = jnp.split(u, 2, axis=-1)
    h = a * jax.nn.sigmoid(g)
    h = depthwise_conv(h, w_dw) + b_dw
    return jax.nn.silu(layer_norm(h, g_ln, b_ln))


def split_heads(t):
    return t.reshape(*t.shape[:-1], N_NA_HEADS, NA_HEAD_DIM)


def na_tables(rows):
    wr = min(NA_WIN_ROWS_MAX, rows)
    r = np.arange(rows)
    row_start = np.clip(r - wr // 2, 0, rows - wr)
    row_idx = row_start[:, None] + np.arange(wr)[None, :]
    d_row = row_idx - r[:, None] + (NA_WIN_ROWS_MAX - 1)
    j = np.arange(GRID_W)
    col_start = np.clip(j - NA_WIN_COLS // 2, 0, GRID_W - NA_WIN_COLS)
    col_in = (j[None, :] >= col_start[:, None]) & (j[None, :] < col_start[:, None] + NA_WIN_COLS)
    d_col = np.clip(j[None, :] - j[:, None] + NA_WIN_COLS - 1, 0, 2 * NA_WIN_COLS - 2)
    return wr, row_idx, d_row, col_in, d_col


def neighbourhood_attention(q, k, v, k_ctx, v_ctx, rpb, tables):
    wr, row_idx, d_row, col_in, d_col = tables
    bsz, seq = q.shape[0], q.shape[1]
    rows = seq // GRID_W
    grid = lambda t: t.reshape(bsz, rows, GRID_W, N_NA_HEADS, NA_HEAD_DIM)
    qg, kg, vg = grid(q), grid(k), grid(v)
    k_band = kg[:, row_idx]
    v_band = vg[:, row_idx]
    s_band = jnp.einsum('brqhd,brwkhd->bhrqwk', qg, k_band).astype(jnp.float32)
    bias = rpb.astype(jnp.float32)[:, d_row[:, None, :, None], d_col[None, :, None, :]]
    bias = jnp.where(col_in[None, None, :, None, :], bias, NEG_INF)
    s_band = (s_band + bias[None]).reshape(bsz, N_NA_HEADS, rows, GRID_W, wr * GRID_W)
    s_ctx = jnp.einsum('brqhd,bchd->bhrqc', qg, k_ctx).astype(jnp.float32)
    p = jax.nn.softmax(jnp.concatenate([s_band, s_ctx], axis=-1), axis=-1).astype(v.dtype)
    p_band = p[..., :wr * GRID_W].reshape(bsz, N_NA_HEADS, rows, GRID_W, wr, GRID_W)
    p_ctx = p[..., wr * GRID_W:]
    o = (jnp.einsum('bhrqwk,brwkhd->brqhd', p_band, v_band)
         + jnp.einsum('bhrqc,bchd->brqhd', p_ctx, v_ctx))
    return o.reshape(bsz, seq, D_NA)


def context_attention(q, k, v):
    s = jnp.einsum('bqhd,bkhd->bhqk', q, k).astype(jnp.float32)
    p = jax.nn.softmax(s, axis=-1).astype(v.dtype)
    o = jnp.einsum('bhqk,bkhd->bqhd', p, v)
    return o.reshape(q.shape[0], q.shape[1], D_NA)


def expert_choice_ffn(h, w_router, w_gate, w_up, w_down):
    bsz, n_tok, d = h.shape
    cap = EC_CAPACITY_FACTOR * n_tok // N_EXPERTS
    aff = jax.nn.softmax((h @ w_router).astype(jnp.float32), axis=-1)
    g, idx = lax.top_k(jnp.swapaxes(aff, 1, 2), cap)
    b_idx = jnp.arange(bsz)[:, None, None]
    xs = h[b_idx, idx]
    a = jnp.einsum('becd,edf->becf', xs, w_gate)
    u = jnp.einsum('becd,edf->becf', xs, w_up)
    y = jnp.einsum('becf,efd->becd', jax.nn.silu(a) * u, w_down)
    y = y * g[..., None].astype(y.dtype)
    flat = (b_idx * n_tok + idx).reshape(-1)
    out = jax.ops.segment_sum(y.reshape(-1, d), flat, num_segments=bsz * n_tok)
    return out.reshape(bsz, n_tok, d)


def post_norm(x, y, gate, g, b):
    return layer_norm(DEEPNORM_ALPHA * x + (1 + gate) * y, g, b)


def setup_inputs(seed: int = 0) -> dict:
    key = jax.random.key(seed)
    ks = jax.random.split(key, 32)
    f32 = jnp.float32
    L = DEPTH

    def nrm(k, shape, s):
        return jax.random.normal(k, shape, f32) * s

    return {
        "x": nrm(ks[0], (BATCH, SEQ, D_MODEL), 1.0),
        "c": nrm(ks[1], (BATCH, D_MODEL), 1.0),
        "ctx": nrm(ks[2], (BATCH, CTX_LEN, D_MODEL), 1.0),
        "c_ctx": nrm(ks[3], (D_MODEL,), 1.0),
        "w_mod": nrm(ks[4], (L, D_MODEL, 6 * D_MODEL), 0.1 * D_MODEL ** -0.5),
        "b_mod": nrm(ks[5], (L, 6 * D_MODEL), 0.01),
        "w_in": nrm(ks[6], (L, D_MODEL, D_IN), D_MODEL ** -0.5),
        "b_in": nrm(ks[7], (L, D_IN), 0.01),
        "w_short": nrm(ks[8], (L, SHORT_CONV_W, D_CONV), SHORT_CONV_W ** -0.5),
        "w_conf_dw": nrm(ks[9], (L, CONF_CONV_W, D_CONF), CONF_CONV_W ** -0.5),
        "b_conf_dw": nrm(ks[10], (L, D_CONF), 0.01),
        "g_conf_ln": 1.0 + nrm(ks[11], (L, D_CONF), 0.01),
        "b_conf_ln": nrm(ks[12], (L, D_CONF), 0.01),
        "na_rpb": nrm(ks[13], (L, N_NA_HEADS, 2 * NA_WIN_ROWS_MAX - 1, 2 * NA_WIN_COLS - 1), 0.1),
        "w_out": nrm(ks[14], (L, D_MIX, D_MODEL), DEEPNORM_BETA * D_MIX ** -0.5),
        "b_out": nrm(ks[15], (L, D_MODEL), 0.01),
        "g_post1": 1.0 + nrm(ks[16], (L, D_MODEL), 0.01),
        "b_post1": nrm(ks[17], (L, D_MODEL), 0.01),
        "w_router": nrm(ks[18], (L, D_MODEL, N_EXPERTS), D_MODEL ** -0.5),
        "w_gate": nrm(ks[19], (L, N_EXPERTS, D_MODEL, D_EXPERT), D_MODEL ** -0.5),
        "w_up": nrm(ks[20], (L, N_EXPERTS, D_MODEL, D_EXPERT), D_MODEL ** -0.5),
        "w_down": nrm(ks[21], (L, N_EXPERTS, D_EXPERT, D_MODEL), DEEPNORM_BETA * D_EXPERT ** -0.5),
        "g_post2": 1.0 + nrm(ks[22], (L, D_MODEL), 0.01),
        "b_post2": nrm(ks[23], (L, D_MODEL), 0.01),
    }


def reference(x, c, ctx, c_ctx, w_mod, b_mod, w_in, b_in, w_short, w_conf_dw, b_conf_dw, g_conf_ln, b_conf_ln,
              na_rpb, w_out, b_out, g_post1, b_post1, w_router, w_gate, w_up, w_down, g_post2, b_post2):
    rows = x.shape[1] // GRID_W
    tables = na_tables(rows)
    q_scale = NA_HEAD_DIM ** -0.5
    xc = ctx
    for l in range(DEPTH):
        last = l == DEPTH - 1
        sh1, sc1, gt1, sh2, sc2, gt2 = [m[:, None, :] for m in modulation(c, w_mod[l], b_mod[l])]
        csh1, csc1, cgt1, csh2, csc2, cgt2 = modulation(c_ctx, w_mod[l], b_mod[l])

        h = modulate(layer_norm(x), sh1, sc1)
        hc = modulate(layer_norm(xc), csh1, csc1)
        u = h @ w_in[l] + b_in[l]
        if last:
            uc_kv = hc @ w_in[l][:, OFF_K:] + b_in[l][OFF_K:]
            k_c, v_c = split_heads(uc_kv[..., :D_NA]), split_heads(uc_kv[..., D_NA:])
        else:
            uc = hc @ w_in[l] + b_in[l]
            k_c, v_c = split_heads(uc[..., OFF_K:OFF_V]), split_heads(uc[..., OFF_V:])

        ya = short_conv_mixer(u[..., OFF_A:OFF_B], w_short[l])
        yb = conformer_conv_mixer(u[..., OFF_B:OFF_Q], w_conf_dw[l], b_conf_dw[l], g_conf_ln[l], b_conf_ln[l])
        yc = neighbourhood_attention(split_heads(u[..., OFF_Q:OFF_K]) * q_scale, split_heads(u[..., OFF_K:OFF_V]),
                                     split_heads(u[..., OFF_V:]), k_c, v_c, na_rpb[l], tables)
        y = jnp.concatenate([ya, yb, yc], axis=-1) @ w_out[l] + b_out[l]
        x_mid = post_norm(x, y, gt1, g_post1[l], b_post1[l])

        if not last:
            yac = short_conv_mixer(uc[..., OFF_A:OFF_B], w_short[l])
            ybc = conformer_conv_mixer(uc[..., OFF_B:OFF_Q], w_conf_dw[l], b_conf_dw[l], g_conf_ln[l], b_conf_ln[l])
            ycc = context_attention(split_heads(uc[..., OFF_Q:OFF_K]) * q_scale, k_c, v_c)
            yctx = jnp.concatenate([yac, ybc, ycc], axis=-1) @ w_out[l] + b_out[l]
            xc_mid = post_norm(xc, yctx, cgt1, g_post1[l], b_post1[l])

        hm = modulate(layer_norm(x_mid), sh2, sc2)
        ym = expert_choice_ffn(hm, w_router[l], w_gate[l], w_up[l], w_down[l])
        x = post_norm(x_mid, ym, gt2, g_post2[l], b_post2[l])

        if not last:
            hmc = modulate(layer_norm(xc_mid), csh2, csc2)
            ymc = expert_choice_ffn(hmc, w_router[l], w_gate[l], w_up[l], w_down[l])
            xc = post_norm(xc_mid, ymc, cgt2, g_post2[l], b_post2[l])
    return x
```

```python
import functools
import math

import numpy as np
import jax
import jax.numpy as jnp
from jax import lax
from jax.experimental import pallas as pl
from jax.experimental.pallas import tpu as pltpu

F32 = jnp.float32
BF16 = jnp.bfloat16

D_MODEL = 1024
DEPTH = 2
GRID_W = 64
D_CONV = D_MODEL // 4
D_CONF = D_MODEL // 4
NA_HEAD_DIM = 64
D_NA = D_MODEL - D_CONV - D_CONF
N_NA_HEADS = D_NA // NA_HEAD_DIM
SHORT_CONV_W = 3
CONF_CONV_W = 31
NA_WIN_ROWS_MAX = 8
NA_WIN_COLS = 16
N_EXPERTS = 16
EC_CAPACITY_FACTOR = 2
D_EXPERT = 1024
LN_EPS = 1e-5
DEEPNORM_ALPHA = (2.0 * DEPTH) ** 0.25
NEG_INF = -1e30
LOG2E = math.log2(math.e)

OFF_A = 0
OFF_B = OFF_A + 3 * D_CONV
OFF_Q = OFF_B + 2 * D_CONF
OFF_K = OFF_Q + D_NA
OFF_V = OFF_K + D_NA
D_IN = OFF_V + D_NA

LANES = 128
MOD_ROWS = 16
VMEM_LIMIT = 56 * 1024 * 1024
ATTN_ROWS = 4
N_PAIRS = D_NA // LANES
HI = lax.Precision.HIGHEST


def _cparams(sem):
    return pltpu.CompilerParams(dimension_semantics=sem, vmem_limit_bytes=VMEM_LIMIT)


def _ln(x):
    mu = jnp.mean(x, axis=-1, keepdims=True)
    xc = x - mu
    var = jnp.mean(xc * xc, axis=-1, keepdims=True)
    return xc * lax.rsqrt(var + LN_EPS)


def _sigmoid(x):
    return 1.0 / (1.0 + jnp.exp(-x))


def _mod_kernel(cond_ref, w_ref, b_ref, o_ref):
    s = cond_ref[...]
    s = s * _sigmoid(s)
    o_ref[0] = jnp.dot(s, w_ref[0], preferred_element_type=F32, precision=HI) + b_ref[0]


def _modulation(cond, w_mod, b_mod):
    n_l, d, n = w_mod.shape
    tn = 1536
    return pl.pallas_call(
        _mod_kernel,
        out_shape=jax.ShapeDtypeStruct((n_l, MOD_ROWS, n), F32),
        grid=(n_l, n // tn),
        in_specs=[
            pl.BlockSpec((MOD_ROWS, d), lambda l, j: (0, 0)),
            pl.BlockSpec((1, d, tn), lambda l, j: (l, 0, j)),
            pl.BlockSpec((1, 1, tn), lambda l, j: (l, 0, j)),
        ],
        out_specs=pl.BlockSpec((1, MOD_ROWS, tn), lambda l, j: (l, 0, j)),
        compiler_params=_cparams(("arbitrary", "arbitrary")),
        name="modulation",
    )(cond, w_mod, b_mod.reshape(n_l, 1, n))


def _inproj_kernel(x_ref, sh_ref, sc_ref, w_ref, b_ref, *o_refs, splits):
    h = _ln(x_ref[0]) * (1.0 + sc_ref[0]) + sh_ref[0]
    u = jnp.dot(h.astype(BF16), w_ref[...], preferred_element_type=F32) + b_ref[...]
    off = 0
    for o_ref, (width, scale) in zip(o_refs, splits):
        part = u[:, off:off + width]
        if scale != 1.0:
            part = part * scale
        o_ref[0] = part.astype(o_ref.dtype)
        off += width


def _inproj(x, sh, sc, w, b, splits, dtypes, tm):
    nb, s, d = x.shape
    n = w.shape[1]
    per_sample = sh.shape[0] > 1
    mod_map = (lambda bi, i: (bi, 0, 0)) if per_sample else (lambda bi, i: (0, 0, 0))
    return pl.pallas_call(
        functools.partial(_inproj_kernel, splits=splits),
        out_shape=[jax.ShapeDtypeStruct((nb, s, wd), dt) for (wd, _), dt in zip(splits, dtypes)],
        grid=(nb, s // tm),
        in_specs=[
            pl.BlockSpec((1, tm, d), lambda bi, i: (bi, i, 0)),
            pl.BlockSpec((1, 1, d), mod_map),
            pl.BlockSpec((1, 1, d), mod_map),
            pl.BlockSpec((d, n), lambda bi, i: (0, 0)),
            pl.BlockSpec((1, n), lambda bi, i: (0, 0)),
        ],
        out_specs=[pl.BlockSpec((1, tm, wd), lambda bi, i: (bi, i, 0)) for wd, _ in splits],
        compiler_params=_cparams(("arbitrary", "arbitrary")),
        name="inproj",
    )(x, sh, sc, w, b)


CONV_CHUNK = 128
Z_PAD = 8
H_PAD = 16


def _conv_kernel(ua_ref, ub_ref, ws_ref, wd_ref, bd_ref, g_ref, b_ref, o_ref, z_scr, h_scr, *, seq):
    c = D_CONV
    z_scr[0:Z_PAD, :] = jnp.zeros((Z_PAD, c), F32)
    z_scr[Z_PAD + seq:2 * Z_PAD + seq, :] = jnp.zeros((Z_PAD, c), F32)
    h_scr[0:H_PAD, :] = jnp.zeros((H_PAD, c), F32)
    h_scr[H_PAD + seq:2 * H_PAD + seq, :] = jnp.zeros((H_PAD, c), F32)
    z_scr[Z_PAD:Z_PAD + seq, :] = ua_ref[0, :, c:2 * c] * ua_ref[0, :, 2 * c:3 * c]
    h_scr[H_PAD:H_PAD + seq, :] = ub_ref[0, :, 0:c] * _sigmoid(ub_ref[0, :, c:2 * c])
    tc = min(CONV_CHUNK, seq)
    for ci in range(seq // tc):
        t0 = ci * tc
        acc = ws_ref[0:1, :] * z_scr[t0 + Z_PAD - 1:t0 + Z_PAD - 1 + tc, :]
        for j in range(1, SHORT_CONV_W):
            s0 = t0 + Z_PAD - 1 + j
            acc = acc + ws_ref[j:j + 1, :] * z_scr[s0:s0 + tc, :]
        ya = ua_ref[0, t0:t0 + tc, 0:c] * acc
        hb = bd_ref[...] + wd_ref[0:1, :] * h_scr[t0 + H_PAD - 15:t0 + H_PAD - 15 + tc, :]
        for j in range(1, CONF_CONV_W):
            s0 = t0 + H_PAD - CONF_CONV_W // 2 + j
            hb = hb + wd_ref[j:j + 1, :] * h_scr[s0:s0 + tc, :]
        hn = _ln(hb) * g_ref[...] + b_ref[...]
        yb = hn * _sigmoid(hn)
        o_ref[0, t0:t0 + tc, 0:c] = ya.astype(o_ref.dtype)
        o_ref[0, t0:t0 + tc, c:2 * c] = yb.astype(o_ref.dtype)


def _conv_mixers(ua, ub, w_short, w_dw, b_dw, g_ln, b_ln):
    nb, s, _ = ua.shape
    c = D_CONV
    full = lambda shape: pl.BlockSpec(shape, lambda bi: (0,) * len(shape))
    return pl.pallas_call(
        functools.partial(_conv_kernel, seq=s),
        out_shape=jax.ShapeDtypeStruct((nb, s, 2 * c), BF16),
        grid=(nb,),
        in_specs=[
            pl.BlockSpec((1, s, 3 * c), lambda bi: (bi, 0, 0)),
            pl.BlockSpec((1, s, 2 * c), lambda bi: (bi, 0, 0)),
            full((SHORT_CONV_W, c)), full((CONF_CONV_W, c)), full((1, c)), full((1, c)), full((1, c)),
        ],
        out_specs=pl.BlockSpec((1, s, 2 * c), lambda bi: (bi, 0, 0)),
        scratch_shapes=[pltpu.VMEM((s + 2 * Z_PAD, c), F32), pltpu.VMEM((s + 2 * H_PAD, c), F32)],
        compiler_params=_cparams(("arbitrary",)),
        name="conv_mixers",
    )(ua, ub, w_short, w_dw, b_dw.reshape(1, c), g_ln.reshape(1, c), b_ln.reshape(1, c))


SUB_ROWS = 2
WIN_ROWS = SUB_ROWS + NA_WIN_ROWS_MAX - 1
BAND_KEYS = WIN_ROWS * GRID_W
SUB_Q = SUB_ROWS * GRID_W
N_DROW = 2 * NA_WIN_ROWS_MAX - 1
N_DCOL = 2 * NA_WIN_COLS - 1


def _bias_kernel(rpb_ref, o_ref):
    n_rows, n_cols = o_ref.shape
    col = lax.broadcasted_iota(jnp.int32, (LANES, n_cols), 1)
    qc = col >> 6
    kc = col & (GRID_W - 1)
    d_col = jnp.clip(kc - qc + (NA_WIN_COLS - 1), 0, N_DCOL - 1)
    onehot = (lax.broadcasted_iota(jnp.int32, (LANES, n_cols), 0) == d_col).astype(F32)
    vals = jnp.dot(rpb_ref[...], onehot, preferred_element_type=F32, precision=HI)
    col_r = lax.broadcasted_iota(jnp.int32, (n_rows, n_cols), 1)
    qc_r = col_r >> 6
    kc_r = col_r & (GRID_W - 1)
    c0 = jnp.clip(qc_r - NA_WIN_COLS // 2, 0, GRID_W - NA_WIN_COLS)
    inside = (kc_r >= c0) & (kc_r < c0 + NA_WIN_COLS)
    o_ref[...] = jnp.where(inside, vals * LOG2E, NEG_INF)


def _na_plan(rows):
    wr = min(NA_WIN_ROWS_MAX, rows)
    assert wr == NA_WIN_ROWS_MAX and rows % SUB_ROWS == 0 and rows >= WIN_ROWS
    row_start = np.clip(np.arange(rows) - wr // 2, 0, rows - wr)
    w0s, pats, patterns = [], [], []
    for r0 in range(0, rows, SUB_ROWS):
        w0 = int(np.clip(r0 - wr // 2, 0, rows - WIN_ROWS))
        pattern = []
        for iq in range(SUB_ROWS):
            r = r0 + iq
            assert row_start[r] >= w0 and row_start[r] + wr <= w0 + WIN_ROWS
            for w in range(WIN_ROWS):
                ok = row_start[r] <= w0 + w < row_start[r] + wr
                pattern.append(w0 + w - r + NA_WIN_ROWS_MAX - 1 if ok else -1)
        pattern = tuple(pattern)
        if pattern not in patterns:
            patterns.append(pattern)
        w0s.append(w0)
        pats.append(patterns.index(pattern))
    return np.array(w0s, np.int32), np.array(pats, np.int32), patterns


def _na_bias(rpb, patterns):
    assert GRID_W == 64 and N_NA_HEADS * N_DROW <= LANES and N_DCOL <= LANES
    n_rows = N_NA_HEADS * N_DROW
    rpb2 = jnp.pad(rpb.astype(F32).reshape(n_rows, N_DCOL), ((0, LANES - n_rows), (0, LANES - N_DCOL)))
    table = pl.pallas_call(
        _bias_kernel,
        out_shape=jax.ShapeDtypeStruct((LANES, GRID_W * GRID_W), F32),
        name="na_bias",
    )(rpb2)
    table = table[:n_rows].reshape(N_NA_HEADS, N_DROW, GRID_W, GRID_W)
    masked = jnp.full((N_NA_HEADS, GRID_W, GRID_W), NEG_INF, F32)
    out = []
    for pattern in patterns:
        per_q = []
        for iq in range(SUB_ROWS):
            blocks = [masked if d < 0 else table[:, d] for d in pattern[iq * WIN_ROWS:(iq + 1) * WIN_ROWS]]
            per_q.append(jnp.stack(blocks, axis=2))
        full = jnp.stack(per_q, axis=1)
        out.append(full.reshape(N_PAIRS, 2 * SUB_Q, BAND_KEYS))
    return jnp.stack(out)


def _lane_reduce(xs, combine, reduce, neutral):
    chunks = []
    for x in xs:
        rows, n = x.shape
        n_full = n // LANES
        chunks += [x[:, j * LANES:(j + 1) * LANES] for j in range(n_full)]
        if n % LANES:
            fill = jnp.full((rows, LANES - n % LANES), neutral, x.dtype)
            chunks.append(jnp.concatenate([x[:, n_full * LANES:], fill], axis=1))
    return reduce(functools.reduce(combine, chunks), axis=-1, keepdims=True)


def _attn_kernel(w0_ref, pat_ref, q_ref, k_ref, v_ref, kc_ref, vc_ref, *rest, banded, n_sub, sub_q):
    if banded:
        bias_ref, o_ref = rest
    else:
        (o_ref,) = rest
    lane = lax.broadcasted_iota(jnp.int32, (sub_q, LANES), 1)
    first = lane < NA_HEAD_DIM
    nt = (((1,), (1,)), ((), ()))
    stages = [(si, p) for si in range(n_sub) for p in range(N_PAIRS)]

    def window(si):
        blk = pl.program_id(1) * n_sub + si
        return pl.multiple_of(w0_ref[blk] * GRID_W, GRID_W), pat_ref[blk]

    def scores(si, p):
        cols = slice(p * LANES, (p + 1) * LANES)
        q_p = q_ref[0, si * sub_q:(si + 1) * sub_q, cols]
        zero = jnp.zeros_like(q_p)
        qq = jnp.concatenate([jnp.where(first, q_p, zero), jnp.where(first, zero, q_p)], axis=0)
        parts = [lax.dot_general(qq, kc_ref[0, :, cols], nt, preferred_element_type=F32)]
        if banded:
            start, pat = window(si)
            parts.append(lax.dot_general(qq, k_ref[0, pl.ds(start, BAND_KEYS), cols], nt,
                                         preferred_element_type=F32) + bias_ref[pat, p])
        return parts

    def finish(si, p, parts):
        cols = slice(p * LANES, (p + 1) * LANES)
        m = _lane_reduce(parts, jnp.maximum, jnp.max, NEG_INF)
        es = [jnp.exp2(s - m) for s in parts]
        den = _lane_reduce(es, jnp.add, jnp.sum, 0.0)
        o = jnp.dot(es[0].astype(BF16), vc_ref[0, :, cols], preferred_element_type=F32)
        if banded:
            start, _ = window(si)
            o = o + jnp.dot(es[1].astype(BF16), v_ref[0, pl.ds(start, BAND_KEYS), cols],
                            preferred_element_type=F32)
        o = o * (1.0 / den)
        out = jnp.where(first, o[:sub_q], o[sub_q:])
        o_ref[0, si * sub_q:(si + 1) * sub_q, cols] = out.astype(o_ref.dtype)

    nxt = scores(*stages[0])
    for i, (si, p) in enumerate(stages):
        cur = nxt
        if i + 1 < len(stages):
            nxt = scores(*stages[i + 1])
        finish(si, p, cur)


def _neighbourhood_attention(q, k, v, kc, vc, rpb):
    nb, s, dn = q.shape
    rows = s // GRID_W
    nctx = kc.shape[1]
    w0s, pats, patterns = _na_plan(rows)
    bias = _na_bias(rpb, patterns)
    n_sub = ATTN_ROWS // SUB_ROWS
    m_rows = ATTN_ROWS * GRID_W
    grid_spec = pltpu.PrefetchScalarGridSpec(
        num_scalar_prefetch=2,
        grid=(nb, rows // ATTN_ROWS),
        in_specs=[
            pl.BlockSpec((1, m_rows, dn), lambda bi, i, w0, pat: (bi, i, 0)),
            pl.BlockSpec((1, s, dn), lambda bi, i, w0, pat: (bi, 0, 0)),
            pl.BlockSpec((1, s, dn), lambda bi, i, w0, pat: (bi, 0, 0)),
            pl.BlockSpec((1, nctx, dn), lambda bi, i, w0, pat: (bi, 0, 0)),
            pl.BlockSpec((1, nctx, dn), lambda bi, i, w0, pat: (bi, 0, 0)),
            pl.BlockSpec(bias.shape, lambda bi, i, w0, pat: (0, 0, 0, 0), pipeline_mode=pl.Buffered(1)),
        ],
        out_specs=pl.BlockSpec((1, m_rows, dn), lambda bi, i, w0, pat: (bi, i, 0)),
    )
    return pl.pallas_call(
        functools.partial(_attn_kernel, banded=True, n_sub=n_sub, sub_q=SUB_Q),
        out_shape=jax.ShapeDtypeStruct((nb, s, dn), BF16),
        grid_spec=grid_spec,
        compiler_params=_cparams(("arbitrary", "arbitrary")),
        name="neighbourhood_attention",
    )(jnp.asarray(w0s), jnp.asarray(pats), q, k, v, kc, vc, bias)


def _context_attention(q, kc, vc):
    nb, s, dn = q.shape
    spec = pl.BlockSpec((1, s, dn), lambda bi, i, w0, pat: (bi, 0, 0))
    grid_spec = pltpu.PrefetchScalarGridSpec(
        num_scalar_prefetch=2, grid=(nb, 1), in_specs=[spec] * 5, out_specs=spec)
    dummy = jnp.zeros((1,), jnp.int32)
    return pl.pallas_call(
        functools.partial(_attn_kernel, banded=False, n_sub=1, sub_q=s),
        out_shape=jax.ShapeDtypeStruct((nb, s, dn), BF16),
        grid_spec=grid_spec,
        compiler_params=_cparams(("arbitrary", "arbitrary")),
        name="context_attention",
    )(dummy, dummy, q, kc, vc, kc, vc)


def _outproj_kernel(yab_ref, yc_ref, x_ref, w_ref, bo_ref, gt_ref, g_ref, b_ref, sh_ref, sc_ref,
                    wrh_ref, wrl_ref, xmid_ref, hm_ref, lg_ref):
    half = yab_ref.shape[2]
    y = (jnp.dot(yab_ref[0], w_ref[0:half, :], preferred_element_type=F32)
         + jnp.dot(yc_ref[0], w_ref[half:, :], preferred_element_type=F32) + bo_ref[...])
    xm = _ln(DEEPNORM_ALPHA * x_ref[0] + (1.0 + gt_ref[0]) * y) * g_ref[...] + b_ref[...]
    xmid_ref[0] = xm
    hm = _ln(xm) * (1.0 + sc_ref[0]) + sh_ref[0]
    hm_hi = hm.astype(BF16)
    hm_ref[0] = hm_hi
    hm_lo = (hm - hm_hi.astype(F32)).astype(BF16)
    lg_ref[0] = (jnp.dot(hm_hi, wrh_ref[...], preferred_element_type=F32)
                 + jnp.dot(hm_lo, wrh_ref[...], preferred_element_type=F32)
                 + jnp.dot(hm_hi, wrl_ref[...], preferred_element_type=F32))


def _outproj(yab, yc, x, w, bo, gt, g, b, sh, sc, wr_hi, wr_lo, tm):
    nb, s, d = x.shape
    half = yab.shape[2]
    per_sample = gt.shape[0] > 1
    mod_map = (lambda bi, i: (bi, 0, 0)) if per_sample else (lambda bi, i: (0, 0, 0))
    vec = pl.BlockSpec((1, d), lambda bi, i: (0, 0))
    mod = pl.BlockSpec((1, 1, d), mod_map)
    tok = lambda width: pl.BlockSpec((1, tm, width), lambda bi, i: (bi, i, 0))
    return pl.pallas_call(
        _outproj_kernel,
        out_shape=[jax.ShapeDtypeStruct((nb, s, d), F32), jax.ShapeDtypeStruct((nb, s, d), BF16),
                   jax.ShapeDtypeStruct((nb, s, LANES), F32)],
        grid=(nb, s // tm),
        in_specs=[tok(half), tok(half), tok(d), pl.BlockSpec((d, d), lambda bi, i: (0, 0)), vec, mod, vec, vec,
                  mod, mod, pl.BlockSpec((d, LANES), lambda bi, i: (0, 0)),
                  pl.BlockSpec((d, LANES), lambda bi, i: (0, 0))],
        out_specs=[tok(d), tok(d), tok(LANES)],
        compiler_params=_cparams(("arbitrary", "arbitrary")),
        name="outproj_postnorm",
    )(yab, yc, x, w, bo, gt, g, b, sh, sc, wr_hi, wr_lo)


CUM_CHUNK = 256
F32_EXP_BIAS = 127
F32_MANT_BITS = 23


def _prefix_count(mask_f32, tri):
    rows, n = mask_f32.shape
    tc = min(CUM_CHUNK, n)
    base = jnp.zeros((rows, 1), F32)
    parts = []
    for ci in range(n // tc):
        blk = mask_f32[:, ci * tc:(ci + 1) * tc]
        parts.append(jnp.dot(blk.astype(BF16), tri[:tc, :tc], preferred_element_type=F32) + base)
        base = base + jnp.sum(blk, axis=-1, keepdims=True)
    return jnp.concatenate(parts, axis=-1)


def _pow2(k):
    return pltpu.bitcast((k + F32_EXP_BIAS) << F32_MANT_BITS, F32)


def _route_kernel(lg_ref, slot_t_ref, gate_t_ref, slot_c_ref, *, cap, slot_stride):
    nb = lg_ref.shape[0]
    assert nb * N_EXPERTS == LANES
    rows = []
    for b in range(nb):
        lg = lg_ref[b]
        lane = lax.broadcasted_iota(jnp.int32, lg.shape, 1)
        lgm = jnp.where(lane < N_EXPERTS, lg, NEG_INF)
        ex = jnp.exp(lgm - jnp.max(lgm, axis=-1, keepdims=True))
        aff = ex / jnp.sum(ex, axis=-1, keepdims=True)
        rows.append(aff.T[0:N_EXPERTS, :])
    a = jnp.concatenate(rows, axis=0)
    capf = float(cap)

    def enough(t):
        return jnp.sum((a >= t).astype(F32), axis=-1, keepdims=True) >= capf

    def exp_step(_, carry):
        lo, hi = carry
        mid = lo + ((hi - lo + 1) >> 1)
        ok = enough(_pow2(mid))
        return jnp.where(ok, mid, lo), jnp.where(ok, hi, mid - 1)

    k_lo = jnp.full((LANES, 1), -F32_EXP_BIAS, jnp.int32)
    k_hi = jnp.zeros((LANES, 1), jnp.int32)
    k_lo, _ = lax.fori_loop(0, 7, exp_step, (k_lo, k_hi))
    base = _pow2(k_lo)

    def mant_step(_, carry):
        t, step = carry
        step = step * 0.5
        cand = t + step
        return jnp.where(enough(cand), cand, t), step

    thr, _ = lax.fori_loop(0, F32_MANT_BITS, mant_step, (base, base))

    r_i = lax.broadcasted_iota(jnp.int32, (CUM_CHUNK, CUM_CHUNK), 0)
    c_i = lax.broadcasted_iota(jnp.int32, (CUM_CHUNK, CUM_CHUNK), 1)
    tri = (r_i < c_i).astype(BF16)
    gt = (a > thr).astype(F32)
    eq = (a == thr).astype(F32)
    need = capf - jnp.sum(gt, axis=-1, keepdims=True)
    sel = gt + eq * (_prefix_count(eq, tri) < need).astype(F32)
    pos = _prefix_count(sel, tri)
    sample = lax.broadcasted_iota(jnp.int32, (LANES, 1), 0) >> (N_EXPERTS.bit_length() - 1)
    slot = jnp.where(sel > 0.0, pos + (sample * slot_stride).astype(F32), -1.0)
    for b in range(nb):
        lo = b * N_EXPERTS
        slot_t_ref[b] = slot[lo:lo + N_EXPERTS, :]
        gate_t_ref[b] = a[lo:lo + N_EXPERTS, :]
        rolled = slot if b == 0 else jnp.concatenate([slot[lo:, :], slot[:lo, :]], axis=0)
        slot_c_ref[b] = rolled.T


def _route(logits, cap, slot_stride):
    nb, s, _ = logits.shape
    whole = lambda shape: pl.BlockSpec(shape, lambda i: (0,) * len(shape))
    return pl.pallas_call(
        functools.partial(_route_kernel, cap=cap, slot_stride=slot_stride),
        out_shape=[jax.ShapeDtypeStruct((nb, N_EXPERTS, s), F32), jax.ShapeDtypeStruct((nb, N_EXPERTS, s), F32),
                   jax.ShapeDtypeStruct((nb, s, LANES), F32)],
        grid=(1,),
        in_specs=[whole((nb, s, LANES))],
        out_specs=[whole((nb, N_EXPERTS, s)), whole((nb, N_EXPERTS, s)), whole((nb, s, LANES))],
        compiler_params=_cparams(("arbitrary",)),
        name="route",
    )(logits)


def _expert_kernel(hm_ref, slot_ref, gate_ref, wg_ref, wu_ref, wd_ref, y_ref, *, n_slots):
    e = pl.program_id(0)
    s = hm_ref.shape[1]
    row = lax.broadcasted_iota(jnp.int32, (N_EXPERTS, s), 0)
    pick = row == e
    slot_e = jnp.sum(jnp.where(pick, slot_ref[0], 0.0), axis=0, keepdims=True)
    gate_e = jnp.sum(jnp.where(pick, gate_ref[0], 0.0), axis=0, keepdims=True)
    slot_id = lax.broadcasted_iota(jnp.int32, (n_slots, s), 0).astype(F32)
    hit = slot_e == slot_id
    g_slot = jnp.sum(jnp.where(hit, gate_e, 0.0), axis=-1, keepdims=True)
    x_e = jnp.dot(hit.astype(BF16), hm_ref[0], preferred_element_type=F32).astype(BF16)
    a = jnp.dot(x_e, wg_ref[0], preferred_element_type=F32)
    u = jnp.dot(x_e, wu_ref[0], preferred_element_type=F32)
    h = (a * _sigmoid(a) * u).astype(BF16)
    y = jnp.dot(h, wd_ref[0], preferred_element_type=F32) * g_slot
    y_ref[0, 0] = y.astype(y_ref.dtype)


def _experts(hm, slot_t, gate_t, wg, wu, wd, n_slots):
    nb, s, d = hm.shape
    f = wg.shape[2]
    return pl.pallas_call(
        functools.partial(_expert_kernel, n_slots=n_slots),
        out_shape=jax.ShapeDtypeStruct((N_EXPERTS, nb, n_slots, d), BF16),
        grid=(N_EXPERTS, nb),
        in_specs=[
            pl.BlockSpec((1, s, d), lambda e, bi: (bi, 0, 0)),
            pl.BlockSpec((1, N_EXPERTS, s), lambda e, bi: (bi, 0, 0)),
            pl.BlockSpec((1, N_EXPERTS, s), lambda e, bi: (bi, 0, 0)),
            pl.BlockSpec((1, d, f), lambda e, bi: (e, 0, 0)),
            pl.BlockSpec((1, d, f), lambda e, bi: (e, 0, 0)),
            pl.BlockSpec((1, f, d), lambda e, bi: (e, 0, 0)),
        ],
        out_specs=pl.BlockSpec((1, 1, n_slots, d), lambda e, bi: (e, bi, 0, 0)),
        compiler_params=_cparams(("arbitrary", "arbitrary")),
        name="experts",
    )(hm, slot_t, gate_t, wg, wu, wd)


def _combine_kernel(y_ref, slot_ref, x_ref, gt_ref, g_ref, b_ref, o_ref, *, n_slots):
    tm = x_ref.shape[1]
    slot_id = lax.broadcasted_iota(jnp.int32, (tm, n_slots), 1).astype(F32)
    acc = jnp.zeros((tm, x_ref.shape[2]), F32)
    for e in range(N_EXPERTS):
        hit = slot_ref[0, :, e:e + 1] == slot_id
        acc = acc + jnp.dot(hit.astype(BF16), y_ref[e, 0], preferred_element_type=F32)
    z = DEEPNORM_ALPHA * x_ref[0] + (1.0 + gt_ref[0]) * acc
    o_ref[0] = _ln(z) * g_ref[...] + b_ref[...]


def _combine(y, slot_c, x_mid, gt, g, b, tm):
    nb, s, d = x_mid.shape
    n_slots = y.shape[2]
    per_sample = gt.shape[0] > 1
    mod_map = (lambda bi, i: (bi, 0, 0)) if per_sample else (lambda bi, i: (0, 0, 0))
    vec = pl.BlockSpec((1, d), lambda bi, i: (0, 0))
    return pl.pallas_call(
        functools.partial(_combine_kernel, n_slots=n_slots),
        out_shape=jax.ShapeDtypeStruct((nb, s, d), F32),
        grid=(nb, s // tm),
        in_specs=[
            pl.BlockSpec((N_EXPERTS, 1, n_slots, d), lambda bi, i: (0, bi, 0, 0)),
            pl.BlockSpec((1, tm, LANES), lambda bi, i: (bi, i, 0)),
            pl.BlockSpec((1, tm, d), lambda bi, i: (bi, i, 0)),
            pl.BlockSpec((1, 1, d), mod_map), vec, vec,
        ],
        out_specs=pl.BlockSpec((1, tm, d), lambda bi, i: (bi, i, 0)),
        compiler_params=_cparams(("arbitrary", "arbitrary")),
        name="combine_postnorm",
    )(y, slot_c, x_mid, gt, g, b)


def kernel(x, c, ctx, c_ctx, w_mod, b_mod, w_in, b_in, w_short, w_conf_dw, b_conf_dw, g_conf_ln, b_conf_ln,
           na_rpb, w_out, b_out, g_post1, b_post1, w_router, w_gate, w_up, w_down, g_post2, b_post2):
    bsz, seq, d = x.shape
    nctx = ctx.shape[1]
    cap = EC_CAPACITY_FACTOR * seq // N_EXPERTS
    cap_ctx = EC_CAPACITY_FACTOR * nctx // N_EXPERTS
    q_scale = NA_HEAD_DIM ** -0.5 * LOG2E

    cond = jnp.concatenate([c, c_ctx[None, :], jnp.zeros((MOD_ROWS - bsz - 1, d), F32)], axis=0)
    mods = _modulation(cond, w_mod, b_mod)

    lat_splits = ((3 * D_CONV, 1.0), (2 * D_CONF, 1.0), (D_NA, q_scale), (D_NA, 1.0), (D_NA, 1.0))
    lat_dtypes = (F32, F32, BF16, BF16, BF16)
    kv_splits = ((D_NA, 1.0), (D_NA, 1.0))

    xc = ctx
    for l in range(DEPTH):
        last = l == DEPTH - 1
        m_lat = [mods[l, :bsz, i * d:(i + 1) * d][:, None, :] for i in range(6)]
        m_ctx = [mods[l, bsz:bsz + 1, i * d:(i + 1) * d][:, None, :] for i in range(6)]
        w_in_bf = w_in[l].astype(BF16)
        b_in_l = b_in[l][None, :]
        w_out_bf = w_out[l].astype(BF16)
        b_out_l = b_out[l][None, :]
        wr = jnp.pad(w_router[l], ((0, 0), (0, LANES - N_EXPERTS)))
        wr_hi = wr.astype(BF16)
        wr_lo = (wr - wr_hi.astype(F32)).astype(BF16)
        wg_bf, wu_bf, wd_bf = w_gate[l].astype(BF16), w_up[l].astype(BF16), w_down[l].astype(BF16)
        g1, b1 = g_post1[l][None, :], b_post1[l][None, :]
        g2, b2 = g_post2[l][None, :], b_post2[l][None, :]
        conv_w = (w_short[l], w_conf_dw[l], b_conf_dw[l], g_conf_ln[l], b_conf_ln[l])

        if last:
            k_c, v_c = _inproj(xc, m_ctx[0], m_ctx[1], w_in_bf[:, OFF_K:], b_in_l[:, OFF_K:],
                               kv_splits, (BF16, BF16), tm=nctx)
        else:
            uac, ubc, q_c, k_c, v_c = _inproj(xc, m_ctx[0], m_ctx[1], w_in_bf, b_in_l,
                                              lat_splits, lat_dtypes, tm=nctx)

        ua, ub, q, k, v = _inproj(x, m_lat[0], m_lat[1], w_in_bf, b_in_l, lat_splits, lat_dtypes, tm=512)
        yab = _conv_mixers(ua, ub, *conv_w)
        yc = _neighbourhood_attention(q, k, v, k_c, v_c, na_rpb[l])
        x_mid, hm, logits = _outproj(yab, yc, x, w_out_bf, b_out_l, m_lat[2], g1, b1, m_lat[3], m_lat[4],
                                     wr_hi, wr_lo, tm=512)

        slot_t, gate_t, slot_c = _route(logits, cap, 0)
        y_e = _experts(hm, slot_t, gate_t, wg_bf, wu_bf, wd_bf, cap)
        x = _combine(y_e, slot_c, x_mid, m_lat[5], g2, b2, tm=512)

        if not last:
            yabc = _conv_mixers(uac, ubc, *conv_w)
            ycc = _context_attention(q_c, k_c, v_c)
            xc_mid, hmc, logits_c = _outproj(yabc, ycc, xc, w_out_bf, b_out_l, m_ctx[2], g1, b1,
                                             m_ctx[3], m_ctx[4], wr_hi, wr_lo, tm=nctx)
            slot_tc, gate_tc, slot_cc = _route(logits_c, cap_ctx, cap_ctx)
            flat_t = lambda t: jnp.transpose(t, (1, 0, 2)).reshape(1, N_EXPERTS, bsz * nctx)
            y_ec = _experts(hmc.reshape(1, bsz * nctx, d), flat_t(slot_tc), flat_t(gate_tc),
                            wg_bf, wu_bf, wd_bf, bsz * cap_ctx)
            xc = _combine(y_ec, slot_cc.reshape(1, bsz * nctx, LANES), xc_mid.reshape(1, bsz * nctx, d),
                          m_ctx[5], g2, b2, tm=512).reshape(bsz, nctx, d)
    return x
```

```python
import functools
import math

import numpy as np
import jax
import jax.numpy as jnp
from jax import lax
from jax.experimental import pallas as pl
from jax.experimental.pallas import tpu as pltpu

F32 = jnp.float32
BF16 = jnp.bfloat16

D_MODEL = 1024
DEPTH = 2
GRID_W = 64
D_CONV = D_MODEL // 4
D_CONF = D_MODEL // 4
NA_HEAD_DIM = 64
D_NA = D_MODEL - D_CONV - D_CONF
N_NA_HEADS = D_NA // NA_HEAD_DIM
SHORT_CONV_W = 3
CONF_CONV_W = 31
NA_WIN_ROWS_MAX = 8
NA_WIN_COLS = 16
N_EXPERTS = 16
EC_CAPACITY_FACTOR = 2
D_EXPERT = 1024
LN_EPS = 1e-5
DEEPNORM_ALPHA = (2.0 * DEPTH) ** 0.25
NEG_INF = -1e30
LOG2E = math.log2(math.e)

OFF_A = 0
OFF_B = OFF_A + 3 * D_CONV
OFF_Q = OFF_B + 2 * D_CONF
OFF_K = OFF_Q + D_NA
OFF_V = OFF_K + D_NA
D_IN = OFF_V + D_NA

LANES = 128
MOD_ROWS = 16
VMEM_LIMIT = 56 * 1024 * 1024
ATTN_ROWS = 4
N_PAIRS = D_NA // LANES
HI = lax.Precision.HIGHEST


def _cparams(sem):
    return pltpu.CompilerParams(dimension_semantics=sem, vmem_limit_bytes=VMEM_LIMIT)


def _ln(x):
    mu = jnp.mean(x, axis=-1, keepdims=True)
    xc = x - mu
    var = jnp.mean(xc * xc, axis=-1, keepdims=True)
    return xc * lax.rsqrt(var + LN_EPS)


def _sigmoid(x):
    return 1.0 / (1.0 + jnp.exp(-x))


def _mod_kernel(cond_ref, w_ref, b_ref, o_ref):
    s = cond_ref[...]
    s = s * _sigmoid(s)
    o_ref[0] = jnp.dot(s, w_ref[0], preferred_element_type=F32, precision=HI) + b_ref[0]


def _modulation(cond, w_mod, b_mod):
    n_l, d, n = w_mod.shape
    tn = 1536
    return pl.pallas_call(
        _mod_kernel,
        out_shape=jax.ShapeDtypeStruct((n_l, MOD_ROWS, n), F32),
        grid=(n_l, n // tn),
        in_specs=[
            pl.BlockSpec((MOD_ROWS, d), lambda l, j: (0, 0)),
            pl.BlockSpec((1, d, tn), lambda l, j: (l, 0, j)),
            pl.BlockSpec((1, 1, tn), lambda l, j: (l, 0, j)),
        ],
        out_specs=pl.BlockSpec((1, MOD_ROWS, tn), lambda l, j: (l, 0, j)),
        compiler_params=_cparams(("arbitrary", "arbitrary")),
        name="modulation",
    )(cond, w_mod, b_mod.reshape(n_l, 1, n))


def _first_step():
    return (pl.program_id(0) == 0) & (pl.program_id(1) == 0)


def _inproj_kernel(x_ref, sh_ref, sc_ref, w_ref, b_ref, *rest, splits, col0):
    *o_refs, w_bf = rest

    @pl.when(_first_step())
    def _():
        w_bf[...] = w_ref[0].astype(BF16)

    h = _ln(x_ref[0]) * (1.0 + sc_ref[0]) + sh_ref[0]
    u = jnp.dot(h.astype(BF16), w_bf[:, col0:], preferred_element_type=F32) + b_ref[0, :, col0:]
    off = 0
    for o_ref, (width, scale) in zip(o_refs, splits):
        part = u[:, off:off + width]
        if scale != 1.0:
            part = part * scale
        o_ref[0] = part.astype(o_ref.dtype)
        off += width


def _inproj(x, sh, sc, w, b, layer, col0, splits, dtypes, tm):
    nb, s, d = x.shape
    n = w.shape[2]
    per_sample = sh.shape[0] > 1
    mod_map = (lambda bi, i: (bi, 0, 0)) if per_sample else (lambda bi, i: (0, 0, 0))
    return pl.pallas_call(
        functools.partial(_inproj_kernel, splits=splits, col0=col0),
        out_shape=[jax.ShapeDtypeStruct((nb, s, wd), dt) for (wd, _), dt in zip(splits, dtypes)],
        grid=(nb, s // tm),
        in_specs=[
            pl.BlockSpec((1, tm, d), lambda bi, i: (bi, i, 0)),
            pl.BlockSpec((1, 1, d), mod_map),
            pl.BlockSpec((1, 1, d), mod_map),
            pl.BlockSpec((1, d, n), lambda bi, i: (layer, 0, 0), pipeline_mode=pl.Buffered(1)),
            pl.BlockSpec((1, 1, n), lambda bi, i: (layer, 0, 0)),
        ],
        out_specs=[pl.BlockSpec((1, tm, wd), lambda bi, i: (bi, i, 0)) for wd, _ in splits],
        scratch_shapes=[pltpu.VMEM((d, n), BF16)],
        compiler_params=_cparams(("arbitrary", "arbitrary")),
        name="inproj",
    )(x, sh, sc, w, b)


CONV_CHUNK = 128
Z_PAD = 8
H_PAD = 16


def _conv_kernel(ua_ref, ub_ref, ws_ref, wd_ref, bd_ref, g_ref, b_ref, o_ref, z_scr, h_scr, *, seq):
    c = D_CONV
    z_scr[0:Z_PAD, :] = jnp.zeros((Z_PAD, c), F32)
    z_scr[Z_PAD + seq:2 * Z_PAD + seq, :] = jnp.zeros((Z_PAD, c), F32)
    h_scr[0:H_PAD, :] = jnp.zeros((H_PAD, c), F32)
    h_scr[H_PAD + seq:2 * H_PAD + seq, :] = jnp.zeros((H_PAD, c), F32)
    z_scr[Z_PAD:Z_PAD + seq, :] = ua_ref[0, :, c:2 * c] * ua_ref[0, :, 2 * c:3 * c]
    h_scr[H_PAD:H_PAD + seq, :] = ub_ref[0, :, 0:c] * _sigmoid(ub_ref[0, :, c:2 * c])
    tc = min(CONV_CHUNK, seq)
    for ci in range(seq // tc):
        t0 = ci * tc
        acc = ws_ref[0:1, :] * z_scr[t0 + Z_PAD - 1:t0 + Z_PAD - 1 + tc, :]
        for j in range(1, SHORT_CONV_W):
            s0 = t0 + Z_PAD - 1 + j
            acc = acc + ws_ref[j:j + 1, :] * z_scr[s0:s0 + tc, :]
        ya = ua_ref[0, t0:t0 + tc, 0:c] * acc
        hb = bd_ref[...] + wd_ref[0:1, :] * h_scr[t0 + H_PAD - 15:t0 + H_PAD - 15 + tc, :]
        for j in range(1, CONF_CONV_W):
            s0 = t0 + H_PAD - CONF_CONV_W // 2 + j
            hb = hb + wd_ref[j:j + 1, :] * h_scr[s0:s0 + tc, :]
        hn = _ln(hb) * g_ref[...] + b_ref[...]
        yb = hn * _sigmoid(hn)
        o_ref[0, t0:t0 + tc, 0:c] = ya.astype(o_ref.dtype)
        o_ref[0, t0:t0 + tc, c:2 * c] = yb.astype(o_ref.dtype)


def _conv_mixers(ua, ub, w_short, w_dw, b_dw, g_ln, b_ln):
    nb, s, _ = ua.shape
    c = D_CONV
    full = lambda shape: pl.BlockSpec(shape, lambda bi: (0,) * len(shape))
    return pl.pallas_call(
        functools.partial(_conv_kernel, seq=s),
        out_shape=jax.ShapeDtypeStruct((nb, s, 2 * c), BF16),
        grid=(nb,),
        in_specs=[
            pl.BlockSpec((1, s, 3 * c), lambda bi: (bi, 0, 0)),
            pl.BlockSpec((1, s, 2 * c), lambda bi: (bi, 0, 0)),
            full((SHORT_CONV_W, c)), full((CONF_CONV_W, c)), full((1, c)), full((1, c)), full((1, c)),
        ],
        out_specs=pl.BlockSpec((1, s, 2 * c), lambda bi: (bi, 0, 0)),
        scratch_shapes=[pltpu.VMEM((s + 2 * Z_PAD, c), F32), pltpu.VMEM((s + 2 * H_PAD, c), F32)],
        compiler_params=_cparams(("arbitrary",)),
        name="conv_mixers",
    )(ua, ub, w_short, w_dw, b_dw.reshape(1, c), g_ln.reshape(1, c), b_ln.reshape(1, c))


SUB_ROWS = 2
WIN_ROWS = SUB_ROWS + NA_WIN_ROWS_MAX - 1
BAND_KEYS = WIN_ROWS * GRID_W
SUB_Q = SUB_ROWS * GRID_W
N_DROW = 2 * NA_WIN_ROWS_MAX - 1
N_DCOL = 2 * NA_WIN_COLS - 1


def _bias_kernel(rpb_ref, o_ref):
    n_rows, n_cols = o_ref.shape
    col = lax.broadcasted_iota(jnp.int32, (LANES, n_cols), 1)
    qc = col >> 6
    kc = col & (GRID_W - 1)
    d_col = jnp.clip(kc - qc + (NA_WIN_COLS - 1), 0, N_DCOL - 1)
    onehot = (lax.broadcasted_iota(jnp.int32, (LANES, n_cols), 0) == d_col).astype(F32)
    vals = jnp.dot(rpb_ref[...], onehot, preferred_element_type=F32, precision=HI)
    col_r = lax.broadcasted_iota(jnp.int32, (n_rows, n_cols), 1)
    qc_r = col_r >> 6
    kc_r = col_r & (GRID_W - 1)
    c0 = jnp.clip(qc_r - NA_WIN_COLS // 2, 0, GRID_W - NA_WIN_COLS)
    inside = (kc_r >= c0) & (kc_r < c0 + NA_WIN_COLS)
    o_ref[...] = jnp.where(inside, vals * LOG2E, NEG_INF)


def _na_plan(rows):
    wr = min(NA_WIN_ROWS_MAX, rows)
    assert wr == NA_WIN_ROWS_MAX and rows % SUB_ROWS == 0 and rows >= WIN_ROWS
    row_start = np.clip(np.arange(rows) - wr // 2, 0, rows - wr)
    w0s, pats, patterns = [], [], []
    for r0 in range(0, rows, SUB_ROWS):
        w0 = int(np.clip(r0 - wr // 2, 0, rows - WIN_ROWS))
        pattern = []
        for iq in range(SUB_ROWS):
            r = r0 + iq
            assert row_start[r] >= w0 and row_start[r] + wr <= w0 + WIN_ROWS
            for w in range(WIN_ROWS):
                ok = row_start[r] <= w0 + w < row_start[r] + wr
                pattern.append(w0 + w - r + NA_WIN_ROWS_MAX - 1 if ok else -1)
        pattern = tuple(pattern)
        if pattern not in patterns:
            patterns.append(pattern)
        w0s.append(w0)
        pats.append(patterns.index(pattern))
    return np.array(w0s, np.int32), np.array(pats, np.int32), patterns


def _na_bias(rpb, patterns):
    assert GRID_W == 64 and N_NA_HEADS * N_DROW <= LANES and N_DCOL <= LANES
    n_rows = N_NA_HEADS * N_DROW
    rpb2 = jnp.pad(rpb.astype(F32).reshape(n_rows, N_DCOL), ((0, LANES - n_rows), (0, LANES - N_DCOL)))
    table = pl.pallas_call(
        _bias_kernel,
        out_shape=jax.ShapeDtypeStruct((LANES, GRID_W * GRID_W), F32),
        name="na_bias",
    )(rpb2)
    table = table[:n_rows].reshape(N_NA_HEADS, N_DROW, GRID_W, GRID_W)
    masked = jnp.full((N_NA_HEADS, GRID_W, GRID_W), NEG_INF, F32)
    out = []
    for pattern in patterns:
        per_q = []
        for iq in range(SUB_ROWS):
            blocks = [masked if d < 0 else table[:, d] for d in pattern[iq * WIN_ROWS:(iq + 1) * WIN_ROWS]]
            per_q.append(jnp.stack(blocks, axis=2))
        full = jnp.stack(per_q, axis=1)
        out.append(full.reshape(N_PAIRS, 2 * SUB_Q, BAND_KEYS))
    return jnp.stack(out)


def _lane_reduce(xs, combine, reduce, neutral):
    chunks = []
    for x in xs:
        rows, n = x.shape
        n_full = n // LANES
        chunks += [x[:, j * LANES:(j + 1) * LANES] for j in range(n_full)]
        if n % LANES:
            fill = jnp.full((rows, LANES - n % LANES), neutral, x.dtype)
            chunks.append(jnp.concatenate([x[:, n_full * LANES:], fill], axis=1))
    return reduce(functools.reduce(combine, chunks), axis=-1, keepdims=True)


def _attn_kernel(w0_ref, pat_ref, q_ref, k_ref, v_ref, kc_ref, vc_ref, *rest, banded, n_sub, sub_q):
    if banded:
        bias_ref, o_ref = rest
    else:
        (o_ref,) = rest
    lane = lax.broadcasted_iota(jnp.int32, (sub_q, LANES), 1)
    first = lane < NA_HEAD_DIM
    nt = (((1,), (1,)), ((), ()))
    stages = [(si, p) for si in range(n_sub) for p in range(N_PAIRS)]

    def window(si):
        blk = pl.program_id(1) * n_sub + si
        return pl.multiple_of(w0_ref[blk] * GRID_W, GRID_W), pat_ref[blk]

    def scores(si, p):
        cols = slice(p * LANES, (p + 1) * LANES)
        q_p = q_ref[0, si * sub_q:(si + 1) * sub_q, cols]
        zero = jnp.zeros_like(q_p)
        qq = jnp.concatenate([jnp.where(first, q_p, zero), jnp.where(first, zero, q_p)], axis=0)
        parts = [lax.dot_general(qq, kc_ref[0, :, cols], nt, preferred_element_type=F32)]
        if banded:
            start, pat = window(si)
            parts.append(lax.dot_general(qq, k_ref[0, pl.ds(start, BAND_KEYS), cols], nt,
                                         preferred_element_type=F32) + bias_ref[pat, p])
        return parts

    def finish(si, p, parts):
        cols = slice(p * LANES, (p + 1) * LANES)
        m = _lane_reduce(parts, jnp.maximum, jnp.max, NEG_INF)
        es = [jnp.exp2(s - m) for s in parts]
        den = _lane_reduce(es, jnp.add, jnp.sum, 0.0)
        o = jnp.dot(es[0].astype(BF16), vc_ref[0, :, cols], preferred_element_type=F32)
        if banded:
            start, _ = window(si)
            o = o + jnp.dot(es[1].astype(BF16), v_ref[0, pl.ds(start, BAND_KEYS), cols],
                            preferred_element_type=F32)
        o = o * (1.0 / den)
        out = jnp.where(first, o[:sub_q], o[sub_q:])
        o_ref[0, si * sub_q:(si + 1) * sub_q, cols] = out.astype(o_ref.dtype)

    nxt = scores(*stages[0])
    for i, (si, p) in enumerate(stages):
        cur = nxt
        if i + 1 < len(stages):
            nxt = scores(*stages[i + 1])
        finish(si, p, cur)


def _neighbourhood_attention(q, k, v, kc, vc, rpb):
    nb, s, dn = q.shape
    rows = s // GRID_W
    nctx = kc.shape[1]
    w0s, pats, patterns = _na_plan(rows)
    bias = _na_bias(rpb, patterns)
    n_sub = ATTN_ROWS // SUB_ROWS
    m_rows = ATTN_ROWS * GRID_W
    grid_spec = pltpu.PrefetchScalarGridSpec(
        num_scalar_prefetch=2,
        grid=(nb, rows // ATTN_ROWS),
        in_specs=[
            pl.BlockSpec((1, m_rows, dn), lambda bi, i, w0, pat: (bi, i, 0)),
            pl.BlockSpec((1, s, dn), lambda bi, i, w0, pat: (bi, 0, 0)),
            pl.BlockSpec((1, s, dn), lambda bi, i, w0, pat: (bi, 0, 0)),
            pl.BlockSpec((1, nctx, dn), lambda bi, i, w0, pat: (bi, 0, 0)),
            pl.BlockSpec((1, nctx, dn), lambda bi, i, w0, pat: (bi, 0, 0)),
            pl.BlockSpec(bias.shape, lambda bi, i, w0, pat: (0, 0, 0, 0), pipeline_mode=pl.Buffered(1)),
        ],
        out_specs=pl.BlockSpec((1, m_rows, dn), lambda bi, i, w0, pat: (bi, i, 0)),
    )
    return pl.pallas_call(
        functools.partial(_attn_kernel, banded=True, n_sub=n_sub, sub_q=SUB_Q),
        out_shape=jax.ShapeDtypeStruct((nb, s, dn), BF16),
        grid_spec=grid_spec,
        compiler_params=_cparams(("arbitrary", "arbitrary")),
        name="neighbourhood_attention",
    )(jnp.asarray(w0s), jnp.asarray(pats), q, k, v, kc, vc, bias)


def _context_attention(q, kc, vc):
    nb, s, dn = q.shape
    spec = pl.BlockSpec((1, s, dn), lambda bi, i, w0, pat: (bi, 0, 0))
    grid_spec = pltpu.PrefetchScalarGridSpec(
        num_scalar_prefetch=2, grid=(nb, 1), in_specs=[spec] * 5, out_specs=spec)
    dummy = jnp.zeros((1,), jnp.int32)
    return pl.pallas_call(
        functools.partial(_attn_kernel, banded=False, n_sub=1, sub_q=s),
        out_shape=jax.ShapeDtypeStruct((nb, s, dn), BF16),
        grid_spec=grid_spec,
        compiler_params=_cparams(("arbitrary", "arbitrary")),
        name="context_attention",
    )(dummy, dummy, q, kc, vc, kc, vc)


def _outproj_kernel(yab_ref, yc_ref, x_ref, w_ref, bo_ref, gt_ref, g_ref, b_ref, sh_ref, sc_ref,
                    wrh_ref, wrl_ref, xmid_ref, hm_ref, lg_ref, w_bf):
    @pl.when(_first_step())
    def _():
        w_bf[...] = w_ref[0].astype(BF16)

    half = yab_ref.shape[2]
    y = (jnp.dot(yab_ref[0], w_bf[0:half, :], preferred_element_type=F32)
         + jnp.dot(yc_ref[0], w_bf[half:, :], preferred_element_type=F32) + bo_ref[...])
    xm = _ln(DEEPNORM_ALPHA * x_ref[0] + (1.0 + gt_ref[0]) * y) * g_ref[...] + b_ref[...]
    xmid_ref[0] = xm
    hm = _ln(xm) * (1.0 + sc_ref[0]) + sh_ref[0]
    hm_hi = hm.astype(BF16)
    hm_ref[0] = hm_hi
    hm_lo = (hm - hm_hi.astype(F32)).astype(BF16)
    lg_ref[0] = (jnp.dot(hm_hi, wrh_ref[...], preferred_element_type=F32)
                 + jnp.dot(hm_lo, wrh_ref[...], preferred_element_type=F32)
                 + jnp.dot(hm_hi, wrl_ref[...], preferred_element_type=F32))


def _outproj(yab, yc, x, w, layer, bo, gt, g, b, sh, sc, wr_hi, wr_lo, tm):
    nb, s, d = x.shape
    half = yab.shape[2]
    per_sample = gt.shape[0] > 1
    mod_map = (lambda bi, i: (bi, 0, 0)) if per_sample else (lambda bi, i: (0, 0, 0))
    vec = pl.BlockSpec((1, d), lambda bi, i: (0, 0))
    mod = pl.BlockSpec((1, 1, d), mod_map)
    tok = lambda width: pl.BlockSpec((1, tm, width), lambda bi, i: (bi, i, 0))
    return pl.pallas_call(
        _outproj_kernel,
        out_shape=[jax.ShapeDtypeStruct((nb, s, d), F32), jax.ShapeDtypeStruct((nb, s, d), BF16),
                   jax.ShapeDtypeStruct((nb, s, LANES), F32)],
        grid=(nb, s // tm),
        in_specs=[tok(half), tok(half), tok(d),
                  pl.BlockSpec((1, d, d), lambda bi, i: (layer, 0, 0), pipeline_mode=pl.Buffered(1)),
                  vec, mod, vec, vec,
                  mod, mod, pl.BlockSpec((d, LANES), lambda bi, i: (0, 0)),
                  pl.BlockSpec((d, LANES), lambda bi, i: (0, 0))],
        out_specs=[tok(d), tok(d), tok(LANES)],
        scratch_shapes=[pltpu.VMEM((d, d), BF16)],
        compiler_params=_cparams(("arbitrary", "arbitrary")),
        name="outproj_postnorm",
    )(yab, yc, x, w, bo, gt, g, b, sh, sc, wr_hi, wr_lo)


CUM_CHUNK = 256
F32_EXP_BIAS = 127
F32_MANT_BITS = 23


def _prefix_count(mask_f32, tri):
    rows, n = mask_f32.shape
    tc = min(CUM_CHUNK, n)
    base = jnp.zeros((rows, 1), F32)
    parts = []
    for ci in range(n // tc):
        blk = mask_f32[:, ci * tc:(ci + 1) * tc]
        parts.append(jnp.dot(blk.astype(BF16), tri[:tc, :tc], preferred_element_type=F32) + base)
        base = base + jnp.sum(blk, axis=-1, keepdims=True)
    return jnp.concatenate(parts, axis=-1)


def _pow2(k):
    return pltpu.bitcast((k + F32_EXP_BIAS) << F32_MANT_BITS, F32)


def _route_kernel(lg_ref, slot_t_ref, gate_t_ref, slot_c_ref, *, cap, slot_stride):
    nb = lg_ref.shape[0]
    assert nb * N_EXPERTS == LANES
    rows = []
    for b in range(nb):
        lg = lg_ref[b]
        lane = lax.broadcasted_iota(jnp.int32, lg.shape, 1)
        lgm = jnp.where(lane < N_EXPERTS, lg, NEG_INF)
        ex = jnp.exp(lgm - jnp.max(lgm, axis=-1, keepdims=True))
        aff = ex / jnp.sum(ex, axis=-1, keepdims=True)
        rows.append(aff.T[0:N_EXPERTS, :])
    a = jnp.concatenate(rows, axis=0)
    capf = float(cap)

    def enough(t):
        return jnp.sum((a >= t).astype(F32), axis=-1, keepdims=True) >= capf

    def exp_step(_, carry):
        lo, hi = carry
        mid = lo + ((hi - lo + 1) >> 1)
        ok = enough(_pow2(mid))
        return jnp.where(ok, mid, lo), jnp.where(ok, hi, mid - 1)

    k_lo = jnp.full((LANES, 1), -F32_EXP_BIAS, jnp.int32)
    k_hi = jnp.zeros((LANES, 1), jnp.int32)
    k_lo, _ = lax.fori_loop(0, 7, exp_step, (k_lo, k_hi))
    base = _pow2(k_lo)

    def mant_step(_, carry):
        t, step = carry
        step = step * 0.5
        cand = t + step
        return jnp.where(enough(cand), cand, t), step

    thr, _ = lax.fori_loop(0, F32_MANT_BITS, mant_step, (base, base))

    r_i = lax.broadcasted_iota(jnp.int32, (CUM_CHUNK, CUM_CHUNK), 0)
    c_i = lax.broadcasted_iota(jnp.int32, (CUM_CHUNK, CUM_CHUNK), 1)
    tri = (r_i < c_i).astype(BF16)
    gt = (a > thr).astype(F32)
    eq = (a == thr).astype(F32)
    need = capf - jnp.sum(gt, axis=-1, keepdims=True)
    sel = gt + eq * (_prefix_count(eq, tri) < need).astype(F32)
    pos = _prefix_count(sel, tri)
    sample = lax.broadcasted_iota(jnp.int32, (LANES, 1), 0) >> (N_EXPERTS.bit_length() - 1)
    slot = jnp.where(sel > 0.0, pos + (sample * slot_stride).astype(F32), -1.0)
    for b in range(nb):
        lo = b * N_EXPERTS
        slot_t_ref[b] = slot[lo:lo + N_EXPERTS, :]
        gate_t_ref[b] = a[lo:lo + N_EXPERTS, :]
        rolled = slot if b == 0 else jnp.concatenate([slot[lo:, :], slot[:lo, :]], axis=0)
        slot_c_ref[b] = rolled.T


def _route(logits, cap, slot_stride):
    nb, s, _ = logits.shape
    whole = lambda shape: pl.BlockSpec(shape, lambda i: (0,) * len(shape))
    return pl.pallas_call(
        functools.partial(_route_kernel, cap=cap, slot_stride=slot_stride),
        out_shape=[jax.ShapeDtypeStruct((nb, N_EXPERTS, s), F32), jax.ShapeDtypeStruct((nb, N_EXPERTS, s), F32),
                   jax.ShapeDtypeStruct((nb, s, LANES), F32)],
        grid=(1,),
        in_specs=[whole((nb, s, LANES))],
        out_specs=[whole((nb, N_EXPERTS, s)), whole((nb, N_EXPERTS, s)), whole((nb, s, LANES))],
        compiler_params=_cparams(("arbitrary",)),
        name="route",
    )(logits)


def _expert_kernel(hm_ref, slot_ref, gate_ref, wg_ref, wu_ref, wd_ref, y_ref, wg_bf, wu_bf, wd_bf, *, n_slots):
    e = pl.program_id(0)

    @pl.when(pl.program_id(1) == 0)
    def _():
        wg_bf[...] = wg_ref[0, 0].astype(BF16)
        wu_bf[...] = wu_ref[0, 0].astype(BF16)
        wd_bf[...] = wd_ref[0, 0].astype(BF16)

    s = hm_ref.shape[1]
    row = lax.broadcasted_iota(jnp.int32, (N_EXPERTS, s), 0)
    pick = row == e
    slot_e = jnp.sum(jnp.where(pick, slot_ref[0], 0.0), axis=0, keepdims=True)
    gate_e = jnp.sum(jnp.where(pick, gate_ref[0], 0.0), axis=0, keepdims=True)
    slot_id = lax.broadcasted_iota(jnp.int32, (n_slots, s), 0).astype(F32)
    hit = slot_e == slot_id
    g_slot = jnp.sum(jnp.where(hit, gate_e, 0.0), axis=-1, keepdims=True)
    x_e = jnp.dot(hit.astype(BF16), hm_ref[0], preferred_element_type=F32).astype(BF16)
    a = jnp.dot(x_e, wg_bf[...], preferred_element_type=F32)
    u = jnp.dot(x_e, wu_bf[...], preferred_element_type=F32)
    h = (a * _sigmoid(a) * u).astype(BF16)
    y = jnp.dot(h, wd_bf[...], preferred_element_type=F32) * g_slot
    y_ref[0, 0] = y.astype(y_ref.dtype)


def _experts(hm, slot_t, gate_t, wg, wu, wd, layer, n_slots):
    nb, s, d = hm.shape
    f = wg.shape[3]
    return pl.pallas_call(
        functools.partial(_expert_kernel, n_slots=n_slots),
        out_shape=jax.ShapeDtypeStruct((N_EXPERTS, nb, n_slots, d), BF16),
        grid=(N_EXPERTS, nb),
        in_specs=[
            pl.BlockSpec((1, s, d), lambda e, bi: (bi, 0, 0)),
            pl.BlockSpec((1, N_EXPERTS, s), lambda e, bi: (bi, 0, 0)),
            pl.BlockSpec((1, N_EXPERTS, s), lambda e, bi: (bi, 0, 0)),
            pl.BlockSpec((1, 1, d, f), lambda e, bi: (layer, e, 0, 0)),
            pl.BlockSpec((1, 1, d, f), lambda e, bi: (layer, e, 0, 0)),
            pl.BlockSpec((1, 1, f, d), lambda e, bi: (layer, e, 0, 0)),
        ],
        out_specs=pl.BlockSpec((1, 1, n_slots, d), lambda e, bi: (e, bi, 0, 0)),
        scratch_shapes=[pltpu.VMEM((d, f), BF16), pltpu.VMEM((d, f), BF16), pltpu.VMEM((f, d), BF16)],
        compiler_params=_cparams(("arbitrary", "arbitrary")),
        name="experts",
    )(hm, slot_t, gate_t, wg, wu, wd)


def _combine_kernel(y_ref, slot_ref, x_ref, gt_ref, g_ref, b_ref, o_ref, *, n_slots):
    tm = x_ref.shape[1]
    slot_id = lax.broadcasted_iota(jnp.int32, (tm, n_slots), 1).astype(F32)
    acc = jnp.zeros((tm, x_ref.shape[2]), F32)
    for e in range(N_EXPERTS):
        hit = slot_ref[0, :, e:e + 1] == slot_id
        acc = acc + jnp.dot(hit.astype(BF16), y_ref[e, 0], preferred_element_type=F32)
    z = DEEPNORM_ALPHA * x_ref[0] + (1.0 + gt_ref[0]) * acc
    o_ref[0] = _ln(z) * g_ref[...] + b_ref[...]


def _combine(y, slot_c, x_mid, gt, g, b, tm):
    nb, s, d = x_mid.shape
    n_slots = y.shape[2]
    per_sample = gt.shape[0] > 1
    mod_map = (lambda bi, i: (bi, 0, 0)) if per_sample else (lambda bi, i: (0, 0, 0))
    vec = pl.BlockSpec((1, d), lambda bi, i: (0, 0))
    return pl.pallas_call(
        functools.partial(_combine_kernel, n_slots=n_slots),
        out_shape=jax.ShapeDtypeStruct((nb, s, d), F32),
        grid=(nb, s // tm),
        in_specs=[
            pl.BlockSpec((N_EXPERTS, 1, n_slots, d), lambda bi, i: (0, bi, 0, 0)),
            pl.BlockSpec((1, tm, LANES), lambda bi, i: (bi, i, 0)),
            pl.BlockSpec((1, tm, d), lambda bi, i: (bi, i, 0)),
            pl.BlockSpec((1, 1, d), mod_map), vec, vec,
        ],
        out_specs=pl.BlockSpec((1, tm, d), lambda bi, i: (bi, i, 0)),
        compiler_params=_cparams(("arbitrary", "arbitrary")),
        name="combine_postnorm",
    )(y, slot_c, x_mid, gt, g, b)


def kernel(x, c, ctx, c_ctx, w_mod, b_mod, w_in, b_in, w_short, w_conf_dw, b_conf_dw, g_conf_ln, b_conf_ln,
           na_rpb, w_out, b_out, g_post1, b_post1, w_router, w_gate, w_up, w_down, g_post2, b_post2):
    bsz, seq, d = x.shape
    nctx = ctx.shape[1]
    cap = EC_CAPACITY_FACTOR * seq // N_EXPERTS
    cap_ctx = EC_CAPACITY_FACTOR * nctx // N_EXPERTS
    q_scale = NA_HEAD_DIM ** -0.5 * LOG2E

    cond = jnp.concatenate([c, c_ctx[None, :], jnp.zeros((MOD_ROWS - bsz - 1, d), F32)], axis=0)
    mods = _modulation(cond, w_mod, b_mod)

    lat_splits = ((3 * D_CONV, 1.0), (2 * D_CONF, 1.0), (D_NA, q_scale), (D_NA, 1.0), (D_NA, 1.0))
    lat_dtypes = (F32, F32, BF16, BF16, BF16)
    kv_splits = ((D_NA, 1.0), (D_NA, 1.0))
    b_in3 = b_in[:, None, :]

    xc = ctx
    for l in range(DEPTH):
        last = l == DEPTH - 1
        m_lat = [mods[l, :bsz, i * d:(i + 1) * d][:, None, :] for i in range(6)]
        m_ctx = [mods[l, bsz:bsz + 1, i * d:(i + 1) * d][:, None, :] for i in range(6)]
        b_out_l = b_out[l][None, :]
        wr = jnp.pad(w_router[l], ((0, 0), (0, LANES - N_EXPERTS)))
        wr_hi = wr.astype(BF16)
        wr_lo = (wr - wr_hi.astype(F32)).astype(BF16)
        g1, b1 = g_post1[l][None, :], b_post1[l][None, :]
        g2, b2 = g_post2[l][None, :], b_post2[l][None, :]
        conv_w = (w_short[l], w_conf_dw[l], b_conf_dw[l], g_conf_ln[l], b_conf_ln[l])

        if last:
            k_c, v_c = _inproj(xc, m_ctx[0], m_ctx[1], w_in, b_in3, l, OFF_K, kv_splits, (BF16, BF16), tm=nctx)
        else:
            uac, ubc, q_c, k_c, v_c = _inproj(xc, m_ctx[0], m_ctx[1], w_in, b_in3, l, 0,
                                              lat_splits, lat_dtypes, tm=nctx)

        ua, ub, q, k, v = _inproj(x, m_lat[0], m_lat[1], w_in, b_in3, l, 0, lat_splits, lat_dtypes, tm=512)
        yab = _conv_mixers(ua, ub, *conv_w)
        yc = _neighbourhood_attention(q, k, v, k_c, v_c, na_rpb[l])
        x_mid, hm, logits = _outproj(yab, yc, x, w_out, l, b_out_l, m_lat[2], g1, b1, m_lat[3], m_lat[4],
                                     wr_hi, wr_lo, tm=512)

        slot_t, gate_t, slot_c = _route(logits, cap, 0)
        y_e = _experts(hm, slot_t, gate_t, w_gate, w_up, w_down, l, cap)
        x = _combine(y_e, slot_c, x_mid, m_lat[5], g2, b2, tm=512)

        if not last:
            yabc = _conv_mixers(uac, ubc, *conv_w)
            ycc = _context_attention(q_c, k_c, v_c)
            xc_mid, hmc, logits_c = _outproj(yabc, ycc, xc, w_out, l, b_out_l, m_ctx[2], g1, b1,
                                             m_ctx[3], m_ctx[4], wr_hi, wr_lo, tm=nctx)
            slot_tc, gate_tc, slot_cc = _route(logits_c, cap_ctx, cap_ctx)
            flat_t = lambda t: jnp.transpose(t, (1, 0, 2)).reshape(1, N_EXPERTS, bsz * nctx)
            y_ec = _experts(hmc.reshape(1, bsz * nctx, d), flat_t(slot_tc), flat_t(gate_tc),
                            w_gate, w_up, w_down, l, bsz * cap_ctx)
            xc = _combine(y_ec, slot_cc.reshape(1, bsz * nctx, LANES), xc_mid.reshape(1, bsz * nctx, d),
                          m_ctx[5], g2, b2, tm=512).reshape(bsz, nctx, d)
    return x
```

```python
import functools
import math

import numpy as np
import jax
import jax.numpy as jnp
from jax import lax
from jax.experimental import pallas as pl
from jax.experimental.pallas import tpu as pltpu

F32 = jnp.float32
BF16 = jnp.bfloat16

D_MODEL = 1024
DEPTH = 2
GRID_W = 64
D_CONV = D_MODEL // 4
D_CONF = D_MODEL // 4
NA_HEAD_DIM = 64
D_NA = D_MODEL - D_CONV - D_CONF
N_NA_HEADS = D_NA // NA_HEAD_DIM
SHORT_CONV_W = 3
CONF_CONV_W = 31
NA_WIN_ROWS_MAX = 8
NA_WIN_COLS = 16
N_EXPERTS = 16
EC_CAPACITY_FACTOR = 2
D_EXPERT = 1024
LN_EPS = 1e-5
DEEPNORM_ALPHA = (2.0 * DEPTH) ** 0.25
NEG_INF = -1e30
LOG2E = math.log2(math.e)

OFF_A = 0
OFF_B = OFF_A + 3 * D_CONV
OFF_Q = OFF_B + 2 * D_CONF
OFF_K = OFF_Q + D_NA
OFF_V = OFF_K + D_NA
D_IN = OFF_V + D_NA

LANES = 128
SUBLANES = 8
MOD_ROWS = 16
VMEM_LIMIT = 56 * 1024 * 1024
ATTN_ROWS = 4
N_PAIRS = D_NA // LANES
HI = lax.Precision.HIGHEST


def _cparams(sem):
    return pltpu.CompilerParams(dimension_semantics=sem, vmem_limit_bytes=VMEM_LIMIT)


def _ln(x):
    mu = jnp.mean(x, axis=-1, keepdims=True)
    xc = x - mu
    var = jnp.mean(xc * xc, axis=-1, keepdims=True)
    return xc * lax.rsqrt(var + LN_EPS)


def _sigmoid(x):
    return 1.0 / (1.0 + jnp.exp(-x))


def _mod_kernel(cond_ref, w_ref, b_ref, o_ref):
    s = cond_ref[...]
    s = s * _sigmoid(s)
    o_ref[0] = jnp.dot(s, w_ref[0], preferred_element_type=F32, precision=HI) + b_ref[0]


def _modulation(cond, w_mod, b_mod):
    n_l, d, n = w_mod.shape
    tn = 1536
    return pl.pallas_call(
        _mod_kernel,
        out_shape=jax.ShapeDtypeStruct((n_l, MOD_ROWS, n), F32),
        grid=(n_l, n // tn),
        in_specs=[
            pl.BlockSpec((MOD_ROWS, d), lambda l, j: (0, 0)),
            pl.BlockSpec((1, d, tn), lambda l, j: (l, 0, j)),
            pl.BlockSpec((1, 1, tn), lambda l, j: (l, 0, j)),
        ],
        out_specs=pl.BlockSpec((1, MOD_ROWS, tn), lambda l, j: (l, 0, j)),
        compiler_params=_cparams(("arbitrary", "arbitrary")),
        name="modulation",
    )(cond, w_mod, b_mod.reshape(n_l, 1, n))


def _first_step():
    return (pl.program_id(0) == 0) & (pl.program_id(1) == 0)


def _inproj_kernel(x_ref, sh_ref, sc_ref, w_ref, b_ref, *rest, splits, col0):
    *o_refs, w_bf = rest

    @pl.when(_first_step())
    def _():
        w_bf[...] = w_ref[0].astype(BF16)

    h = _ln(x_ref[0]) * (1.0 + sc_ref[0]) + sh_ref[0]
    u = jnp.dot(h.astype(BF16), w_bf[:, col0:], preferred_element_type=F32) + b_ref[0, :, col0:]
    off = 0
    for o_ref, (width, scale) in zip(o_refs, splits):
        part = u[:, off:off + width]
        if scale != 1.0:
            part = part * scale
        o_ref[0] = part.astype(o_ref.dtype)
        off += width


def _inproj(x, sh, sc, w, b, layer, col0, splits, dtypes, tm):
    nb, s, d = x.shape
    n = w.shape[2]
    per_sample = sh.shape[0] > 1
    mod_map = (lambda bi, i: (bi, 0, 0)) if per_sample else (lambda bi, i: (0, 0, 0))
    return pl.pallas_call(
        functools.partial(_inproj_kernel, splits=splits, col0=col0),
        out_shape=[jax.ShapeDtypeStruct((nb, s, wd), dt) for (wd, _), dt in zip(splits, dtypes)],
        grid=(nb, s // tm),
        in_specs=[
            pl.BlockSpec((1, tm, d), lambda bi, i: (bi, i, 0)),
            pl.BlockSpec((1, 1, d), mod_map),
            pl.BlockSpec((1, 1, d), mod_map),
            pl.BlockSpec((1, d, n), lambda bi, i: (layer, 0, 0), pipeline_mode=pl.Buffered(1)),
            pl.BlockSpec((1, 1, n), lambda bi, i: (layer, 0, 0)),
        ],
        out_specs=[pl.BlockSpec((1, tm, wd), lambda bi, i: (bi, i, 0)) for wd, _ in splits],
        scratch_shapes=[pltpu.VMEM((d, n), BF16)],
        compiler_params=_cparams(("arbitrary", "arbitrary")),
        name="inproj",
    )(x, sh, sc, w, b)


CONV_CHUNK = 128
Z_PAD = 8
H_PAD = 16


def _conv_kernel(ua_ref, ub_ref, ws_ref, wd_ref, bd_ref, g_ref, b_ref, o_ref, z_scr, h_scr, *, seq):
    c = D_CONV
    z_scr[0:Z_PAD, :] = jnp.zeros((Z_PAD, c), F32)
    z_scr[Z_PAD + seq:2 * Z_PAD + seq, :] = jnp.zeros((Z_PAD, c), F32)
    h_scr[0, 0:H_PAD, :] = jnp.zeros((H_PAD, c), F32)
    h_scr[0, H_PAD + seq:2 * H_PAD + seq, :] = jnp.zeros((H_PAD, c), F32)
    z_scr[Z_PAD:Z_PAD + seq, :] = ua_ref[0, :, c:2 * c] * ua_ref[0, :, 2 * c:3 * c]
    h_scr[0, H_PAD:H_PAD + seq, :] = ub_ref[0, :, 0:c] * _sigmoid(ub_ref[0, :, c:2 * c])
    n_shift = seq + 2 * H_PAD - SUBLANES
    for r in range(1, SUBLANES):
        h_scr[r, 0:n_shift, :] = h_scr[0, r:r + n_shift, :]
    tc = min(CONV_CHUNK, seq)
    for ci in range(seq // tc):
        t0 = ci * tc
        acc = ws_ref[0:1, :] * z_scr[t0 + Z_PAD - 1:t0 + Z_PAD - 1 + tc, :]
        for j in range(1, SHORT_CONV_W):
            s0 = t0 + Z_PAD - 1 + j
            acc = acc + ws_ref[j:j + 1, :] * z_scr[s0:s0 + tc, :]
        ya = ua_ref[0, t0:t0 + tc, 0:c] * acc
        hb = bd_ref[...]
        for j in range(CONF_CONV_W):
            s0 = t0 + H_PAD - CONF_CONV_W // 2 + j
            a0 = s0 - s0 % SUBLANES
            hb = hb + wd_ref[j:j + 1, :] * h_scr[s0 % SUBLANES, a0:a0 + tc, :]
        hn = _ln(hb) * g_ref[...] + b_ref[...]
        yb = hn * _sigmoid(hn)
        o_ref[0, t0:t0 + tc, 0:c] = ya.astype(o_ref.dtype)
        o_ref[0, t0:t0 + tc, c:2 * c] = yb.astype(o_ref.dtype)


def _conv_mixers(ua, ub, w_short, w_dw, b_dw, g_ln, b_ln):
    nb, s, _ = ua.shape
    c = D_CONV
    full = lambda shape: pl.BlockSpec(shape, lambda bi: (0,) * len(shape))
    return pl.pallas_call(
        functools.partial(_conv_kernel, seq=s),
        out_shape=jax.ShapeDtypeStruct((nb, s, 2 * c), BF16),
        grid=(nb,),
        in_specs=[
            pl.BlockSpec((1, s, 3 * c), lambda bi: (bi, 0, 0)),
            pl.BlockSpec((1, s, 2 * c), lambda bi: (bi, 0, 0)),
            full((SHORT_CONV_W, c)), full((CONF_CONV_W, c)), full((1, c)), full((1, c)), full((1, c)),
        ],
        out_specs=pl.BlockSpec((1, s, 2 * c), lambda bi: (bi, 0, 0)),
        scratch_shapes=[pltpu.VMEM((s + 2 * Z_PAD, c), F32), pltpu.VMEM((SUBLANES, s + 2 * H_PAD, c), F32)],
        compiler_params=_cparams(("arbitrary",)),
        name="conv_mixers",
    )(ua, ub, w_short, w_dw, b_dw.reshape(1, c), g_ln.reshape(1, c), b_ln.reshape(1, c))


SUB_ROWS = 2
WIN_ROWS = SUB_ROWS + NA_WIN_ROWS_MAX - 1
BAND_KEYS = WIN_ROWS * GRID_W
SUB_Q = SUB_ROWS * GRID_W
N_DROW = 2 * NA_WIN_ROWS_MAX - 1
N_DCOL = 2 * NA_WIN_COLS - 1


def _bias_kernel(rpb_ref, o_ref):
    n_rows, n_cols = o_ref.shape
    col = lax.broadcasted_iota(jnp.int32, (LANES, n_cols), 1)
    qc = col >> 6
    kc = col & (GRID_W - 1)
    d_col = jnp.clip(kc - qc + (NA_WIN_COLS - 1), 0, N_DCOL - 1)
    onehot = (lax.broadcasted_iota(jnp.int32, (LANES, n_cols), 0) == d_col).astype(F32)
    vals = jnp.dot(rpb_ref[...], onehot, preferred_element_type=F32, precision=HI)
    col_r = lax.broadcasted_iota(jnp.int32, (n_rows, n_cols), 1)
    qc_r = col_r >> 6
    kc_r = col_r & (GRID_W - 1)
    c0 = jnp.clip(qc_r - NA_WIN_COLS // 2, 0, GRID_W - NA_WIN_COLS)
    inside = (kc_r >= c0) & (kc_r < c0 + NA_WIN_COLS)
    o_ref[...] = jnp.where(inside, vals * LOG2E, NEG_INF)


N_SLABS = -(-WIN_ROWS // 2)
TILES_PER_KIND = N_NA_HEADS * N_DROW
MASKED_TILE = 3 * TILES_PER_KIND


def _na_plan(rows):
    wr = min(NA_WIN_ROWS_MAX, rows)
    assert wr == NA_WIN_ROWS_MAX and rows % SUB_ROWS == 0 and rows >= WIN_ROWS
    row_start = np.clip(np.arange(rows) - wr // 2, 0, rows - wr)
    w0s, tiles = [], []
    for r0 in range(0, rows, SUB_ROWS):
        w0 = int(np.clip(r0 - wr // 2, 0, rows - WIN_ROWS))
        for iq in range(SUB_ROWS):
            r = r0 + iq
            assert row_start[r] >= w0 and row_start[r] + wr <= w0 + WIN_ROWS
            ok = lambda w: w < WIN_ROWS and row_start[r] <= w0 + w < row_start[r] + wr
            d_row = lambda w: w0 + w - r + NA_WIN_ROWS_MAX - 1
            for j in range(N_SLABS):
                lo, hi = ok(2 * j), ok(2 * j + 1)
                if lo and hi:
                    tiles.append(d_row(2 * j))
                elif lo:
                    tiles.append(TILES_PER_KIND + d_row(2 * j))
                elif hi:
                    tiles.append(2 * TILES_PER_KIND + d_row(2 * j + 1))
                else:
                    tiles.append(-1)
        w0s.append(w0)
    return np.array(w0s, np.int32), np.array(tiles, np.int32)


def _na_bias(rpb):
    assert GRID_W == 64 and 2 * GRID_W == LANES and TILES_PER_KIND <= LANES and N_DCOL <= LANES
    rpb2 = jnp.pad(rpb.astype(F32).reshape(TILES_PER_KIND, N_DCOL),
                   ((0, LANES - TILES_PER_KIND), (0, LANES - N_DCOL)))
    table = pl.pallas_call(
        _bias_kernel,
        out_shape=jax.ShapeDtypeStruct((LANES, GRID_W * GRID_W), F32),
        name="na_bias",
    )(rpb2)
    table = table[:TILES_PER_KIND].reshape(N_NA_HEADS, N_DROW, GRID_W, GRID_W)
    masked = jnp.full_like(table, NEG_INF)
    nxt = jnp.concatenate([table[:, 1:], masked[:, :1]], axis=1)
    kinds = [jnp.concatenate(pair, axis=-1).reshape(TILES_PER_KIND, GRID_W, LANES)
             for pair in ((table, nxt), (table, masked), (masked, table))]
    return jnp.concatenate(kinds + [jnp.full((1, GRID_W, LANES), NEG_INF, F32)], axis=0)


def _lane_reduce(xs, combine, reduce, neutral):
    chunks = []
    for x in xs:
        rows, n = x.shape
        n_full = n // LANES
        chunks += [x[:, j * LANES:(j + 1) * LANES] for j in range(n_full)]
        if n % LANES:
            fill = jnp.full((rows, LANES - n % LANES), neutral, x.dtype)
            chunks.append(jnp.concatenate([x[:, n_full * LANES:], fill], axis=1))
    return reduce(functools.reduce(combine, chunks), axis=-1, keepdims=True)


def _attn_kernel(w0_ref, tile_ref, q_ref, k_ref, v_ref, kc_ref, vc_ref, *rest, banded, n_sub, sub_q):
    if banded:
        bias_ref, o_ref = rest
    else:
        (o_ref,) = rest
    lane = lax.broadcasted_iota(jnp.int32, (sub_q, LANES), 1)
    first = lane < NA_HEAD_DIM
    nt = (((1,), (1,)), ((), ()))
    stages = [(si, p) for si in range(n_sub) for p in range(N_PAIRS)]

    def window(si):
        blk = pl.program_id(1) * n_sub + si
        return blk, pl.multiple_of(w0_ref[blk] * GRID_W, GRID_W)

    def bias(blk, head):
        row_blocks = []
        for iq in range(SUB_ROWS):
            slabs = []
            for j in range(N_SLABS):
                t = tile_ref[(blk * SUB_ROWS + iq) * N_SLABS + j]
                tile = bias_ref[jnp.where(t < 0, MASKED_TILE, t + head * N_DROW)]
                width = min(LANES, BAND_KEYS - j * LANES)
                slabs.append(tile[:, :width])
            row_blocks.append(jnp.concatenate(slabs, axis=1))
        return jnp.concatenate(row_blocks, axis=0)

    def scores(si, p):
        cols = slice(p * LANES, (p + 1) * LANES)
        q_p = q_ref[0, si * sub_q:(si + 1) * sub_q, cols]
        zero = jnp.zeros_like(q_p)
        qq = jnp.concatenate([jnp.where(first, q_p, zero), jnp.where(first, zero, q_p)], axis=0)
        parts = [lax.dot_general(qq, kc_ref[0, :, cols], nt, preferred_element_type=F32)]
        if banded:
            blk, start = window(si)
            both = jnp.concatenate([bias(blk, 2 * p), bias(blk, 2 * p + 1)], axis=0)
            parts.append(lax.dot_general(qq, k_ref[0, pl.ds(start, BAND_KEYS), cols], nt,
                                         preferred_element_type=F32) + both)
        return parts

    def finish(si, p, parts):
        cols = slice(p * LANES, (p + 1) * LANES)
        m = _lane_reduce(parts, jnp.maximum, jnp.max, NEG_INF)
        es = [jnp.exp2(s - m) for s in parts]
        den = _lane_reduce(es, jnp.add, jnp.sum, 0.0)
        o = jnp.dot(es[0].astype(BF16), vc_ref[0, :, cols], preferred_element_type=F32)
        if banded:
            _, start = window(si)
            o = o + jnp.dot(es[1].astype(BF16), v_ref[0, pl.ds(start, BAND_KEYS), cols],
                            preferred_element_type=F32)
        o = o * (1.0 / den)
        out = jnp.where(first, o[:sub_q], o[sub_q:])
        o_ref[0, si * sub_q:(si + 1) * sub_q, cols] = out.astype(o_ref.dtype)

    nxt = scores(*stages[0])
    for i, (si, p) in enumerate(stages):
        cur = nxt
        if i + 1 < len(stages):
            nxt = scores(*stages[i + 1])
        finish(si, p, cur)


def _neighbourhood_attention(q, k, v, kc, vc, rpb):
    nb, s, dn = q.shape
    rows = s // GRID_W
    nctx = kc.shape[1]
    w0s, tiles = _na_plan(rows)
    bias = _na_bias(rpb)
    n_sub = ATTN_ROWS // SUB_ROWS
    m_rows = ATTN_ROWS * GRID_W
    grid_spec = pltpu.PrefetchScalarGridSpec(
        num_scalar_prefetch=2,
        grid=(nb, rows // ATTN_ROWS),
        in_specs=[
            pl.BlockSpec((1, m_rows, dn), lambda bi, i, w0, pat: (bi, i, 0)),
            pl.BlockSpec((1, s, dn), lambda bi, i, w0, pat: (bi, 0, 0)),
            pl.BlockSpec((1, s, dn), lambda bi, i, w0, pat: (bi, 0, 0)),
            pl.BlockSpec((1, nctx, dn), lambda bi, i, w0, pat: (bi, 0, 0)),
            pl.BlockSpec((1, nctx, dn), lambda bi, i, w0, pat: (bi, 0, 0)),
            pl.BlockSpec(bias.shape, lambda bi, i, w0, pat: (0, 0, 0), pipeline_mode=pl.Buffered(1)),
        ],
        out_specs=pl.BlockSpec((1, m_rows, dn), lambda bi, i, w0, pat: (bi, i, 0)),
    )
    return pl.pallas_call(
        functools.partial(_attn_kernel, banded=True, n_sub=n_sub, sub_q=SUB_Q),
        out_shape=jax.ShapeDtypeStruct((nb, s, dn), BF16),
        grid_spec=grid_spec,
        compiler_params=_cparams(("arbitrary", "arbitrary")),
        name="neighbourhood_attention",
    )(jnp.asarray(w0s), jnp.asarray(tiles), q, k, v, kc, vc, bias)


def _context_attention(q, kc, vc):
    nb, s, dn = q.shape
    spec = pl.BlockSpec((1, s, dn), lambda bi, i, w0, pat: (bi, 0, 0))
    grid_spec = pltpu.PrefetchScalarGridSpec(
        num_scalar_prefetch=2, grid=(nb, 1), in_specs=[spec] * 5, out_specs=spec)
    dummy = jnp.zeros((1,), jnp.int32)
    return pl.pallas_call(
        functools.partial(_attn_kernel, banded=False, n_sub=1, sub_q=s),
        out_shape=jax.ShapeDtypeStruct((nb, s, dn), BF16),
        grid_spec=grid_spec,
        compiler_params=_cparams(("arbitrary", "arbitrary")),
        name="context_attention",
    )(dummy, dummy, q, kc, vc, kc, vc)


def _outproj_kernel(yab_ref, yc_ref, x_ref, w_ref, bo_ref, gt_ref, g_ref, b_ref, sh_ref, sc_ref,
                    wrh_ref, wrl_ref, xmid_ref, hm_ref, lg_ref, w_bf):
    @pl.when(_first_step())
    def _():
        w_bf[...] = w_ref[0].astype(BF16)

    half = yab_ref.shape[2]
    y = (jnp.dot(yab_ref[0], w_bf[0:half, :], preferred_element_type=F32)
         + jnp.dot(yc_ref[0], w_bf[half:, :], preferred_element_type=F32) + bo_ref[...])
    xm = _ln(DEEPNORM_ALPHA * x_ref[0] + (1.0 + gt_ref[0]) * y) * g_ref[...] + b_ref[...]
    xmid_ref[0] = xm
    hm = _ln(xm) * (1.0 + sc_ref[0]) + sh_ref[0]
    hm_hi = hm.astype(BF16)
    hm_ref[0] = hm_hi
    hm_lo = (hm - hm_hi.astype(F32)).astype(BF16)
    lg_ref[0] = (jnp.dot(hm_hi, wrh_ref[...], preferred_element_type=F32)
                 + jnp.dot(hm_lo, wrh_ref[...], preferred_element_type=F32)
                 + jnp.dot(hm_hi, wrl_ref[...], preferred_element_type=F32))


def _outproj(yab, yc, x, w, layer, bo, gt, g, b, sh, sc, wr_hi, wr_lo, tm):
    nb, s, d = x.shape
    half = yab.shape[2]
    per_sample = gt.shape[0] > 1
    mod_map = (lambda bi, i: (bi, 0, 0)) if per_sample else (lambda bi, i: (0, 0, 0))
    vec = pl.BlockSpec((1, d), lambda bi, i: (0, 0))
    mod = pl.BlockSpec((1, 1, d), mod_map)
    tok = lambda width: pl.BlockSpec((1, tm, width), lambda bi, i: (bi, i, 0))
    return pl.pallas_call(
        _outproj_kernel,
        out_shape=[jax.ShapeDtypeStruct((nb, s, d), F32), jax.ShapeDtypeStruct((nb, s, d), BF16),
                   jax.ShapeDtypeStruct((nb, s, LANES), F32)],
        grid=(nb, s // tm),
        in_specs=[tok(half), tok(half), tok(d),
                  pl.BlockSpec((1, d, d), lambda bi, i: (layer, 0, 0), pipeline_mode=pl.Buffered(1)),
                  vec, mod, vec, vec,
                  mod, mod, pl.BlockSpec((d, LANES), lambda bi, i: (0, 0)),
                  pl.BlockSpec((d, LANES), lambda bi, i: (0, 0))],
        out_specs=[tok(d), tok(d), tok(LANES)],
        scratch_shapes=[pltpu.VMEM((d, d), BF16)],
        compiler_params=_cparams(("arbitrary", "arbitrary")),
        name="outproj_postnorm",
    )(yab, yc, x, w, bo, gt, g, b, sh, sc, wr_hi, wr_lo)


CUM_CHUNK = 256
F32_EXP_BIAS = 127
F32_MANT_BITS = 23


def _prefix_count(mask_f32, tri):
    rows, n = mask_f32.shape
    tc = min(CUM_CHUNK, n)
    base = jnp.zeros((rows, 1), F32)
    parts = []
    for ci in range(n // tc):
        blk = mask_f32[:, ci * tc:(ci + 1) * tc]
        parts.append(jnp.dot(blk.astype(BF16), tri[:tc, :tc], preferred_element_type=F32) + base)
        base = base + jnp.sum(blk, axis=-1, keepdims=True)
    return jnp.concatenate(parts, axis=-1)


def _pow2(k):
    return pltpu.bitcast((k + F32_EXP_BIAS) << F32_MANT_BITS, F32)


def _route_kernel(lg_ref, slot_t_ref, gate_t_ref, slot_c_ref, *, cap, slot_stride):
    nb = lg_ref.shape[0]
    assert nb * N_EXPERTS == LANES
    rows = []
    for b in range(nb):
        lg = lg_ref[b]
        lane = lax.broadcasted_iota(jnp.int32, lg.shape, 1)
        lgm = jnp.where(lane < N_EXPERTS, lg, NEG_INF)
        ex = jnp.exp(lgm - jnp.max(lgm, axis=-1, keepdims=True))
        aff = ex / jnp.sum(ex, axis=-1, keepdims=True)
        rows.append(aff.T[0:N_EXPERTS, :])
    a = jnp.concatenate(rows, axis=0)
    capf = float(cap)

    def enough(t):
        return jnp.sum((a >= t).astype(F32), axis=-1, keepdims=True) >= capf

    def exp_step(_, carry):
        lo, hi = carry
        mid = lo + ((hi - lo + 1) >> 1)
        ok = enough(_pow2(mid))
        return jnp.where(ok, mid, lo), jnp.where(ok, hi, mid - 1)

    k_lo = jnp.full((LANES, 1), -F32_EXP_BIAS, jnp.int32)
    k_hi = jnp.zeros((LANES, 1), jnp.int32)
    k_lo, _ = lax.fori_loop(0, 7, exp_step, (k_lo, k_hi))
    base = _pow2(k_lo)

    def mant_step(_, carry):
        t, step = carry
        step = step * 0.5
        cand = t + step
        return jnp.where(enough(cand), cand, t), step

    thr, _ = lax.fori_loop(0, F32_MANT_BITS, mant_step, (base, base))

    r_i = lax.broadcasted_iota(jnp.int32, (CUM_CHUNK, CUM_CHUNK), 0)
    c_i = lax.broadcasted_iota(jnp.int32, (CUM_CHUNK, CUM_CHUNK), 1)
    tri = (r_i < c_i).astype(BF16)
    gt = (a > thr).astype(F32)
    eq = (a == thr).astype(F32)
    need = capf - jnp.sum(gt, axis=-1, keepdims=True)
    sel = gt + eq * (_prefix_count(eq, tri) < need).astype(F32)
    pos = _prefix_count(sel, tri)
    sample = lax.broadcasted_iota(jnp.int32, (LANES, 1), 0) >> (N_EXPERTS.bit_length() - 1)
    slot = jnp.where(sel > 0.0, pos + (sample * slot_stride).astype(F32), -1.0)
    for b in range(nb):
        lo = b * N_EXPERTS
        slot_t_ref[b] = slot[lo:lo + N_EXPERTS, :]
        gate_t_ref[b] = a[lo:lo + N_EXPERTS, :]
        rolled = slot if b == 0 else jnp.concatenate([slot[lo:, :], slot[:lo, :]], axis=0)
        slot_c_ref[b] = rolled.T


def _route(logits, cap, slot_stride):
    nb, s, _ = logits.shape
    whole = lambda shape: pl.BlockSpec(shape, lambda i: (0,) * len(shape))
    return pl.pallas_call(
        functools.partial(_route_kernel, cap=cap, slot_stride=slot_stride),
        out_shape=[jax.ShapeDtypeStruct((nb, N_EXPERTS, s), F32), jax.ShapeDtypeStruct((nb, N_EXPERTS, s), F32),
                   jax.ShapeDtypeStruct((nb, s, LANES), F32)],
        grid=(1,),
        in_specs=[whole((nb, s, LANES))],
        out_specs=[whole((nb, N_EXPERTS, s)), whole((nb, N_EXPERTS, s)), whole((nb, s, LANES))],
        compiler_params=_cparams(("arbitrary",)),
        name="route",
    )(logits)


def _expert_kernel(hm_ref, slot_ref, gate_ref, wg_ref, wu_ref, wd_ref, y_ref, wg_bf, wu_bf, wd_bf, *, n_slots):
    e = pl.program_id(0)

    @pl.when(pl.program_id(1) == 0)
    def _():
        wg_bf[...] = wg_ref[0, 0].astype(BF16)
        wu_bf[...] = wu_ref[0, 0].astype(BF16)
        wd_bf[...] = wd_ref[0, 0].astype(BF16)

    s = hm_ref.shape[1]
    row = lax.broadcasted_iota(jnp.int32, (N_EXPERTS, s), 0)
    pick = row == e
    slot_e = jnp.sum(jnp.where(pick, slot_ref[0], 0.0), axis=0, keepdims=True)
    gate_e = jnp.sum(jnp.where(pick, gate_ref[0], 0.0), axis=0, keepdims=True)
    slot_id = lax.broadcasted_iota(jnp.int32, (n_slots, s), 0).astype(F32)
    hit = slot_e == slot_id
    g_slot = jnp.sum(jnp.where(hit, gate_e, 0.0), axis=-1, keepdims=True)
    x_e = jnp.dot(hit.astype(BF16), hm_ref[0], preferred_element_type=F32).astype(BF16)
    a = jnp.dot(x_e, wg_bf[...], preferred_element_type=F32)
    u = jnp.dot(x_e, wu_bf[...], preferred_element_type=F32)
    h = (a * _sigmoid(a) * u).astype(BF16)
    y = jnp.dot(h, wd_bf[...], preferred_element_type=F32) * g_slot
    y_ref[0, 0] = y.astype(y_ref.dtype)


def _experts(hm, slot_t, gate_t, wg, wu, wd, layer, n_slots):
    nb, s, d = hm.shape
    f = wg.shape[3]
    return pl.pallas_call(
        functools.partial(_expert_kernel, n_slots=n_slots),
        out_shape=jax.ShapeDtypeStruct((N_EXPERTS, nb, n_slots, d), BF16),
        grid=(N_EXPERTS, nb),
        in_specs=[
            pl.BlockSpec((1, s, d), lambda e, bi: (bi, 0, 0)),
            pl.BlockSpec((1, N_EXPERTS, s), lambda e, bi: (bi, 0, 0)),
            pl.BlockSpec((1, N_EXPERTS, s), lambda e, bi: (bi, 0, 0)),
            pl.BlockSpec((1, 1, d, f), lambda e, bi: (layer, e, 0, 0)),
            pl.BlockSpec((1, 1, d, f), lambda e, bi: (layer, e, 0, 0)),
            pl.BlockSpec((1, 1, f, d), lambda e, bi: (layer, e, 0, 0)),
        ],
        out_specs=pl.BlockSpec((1, 1, n_slots, d), lambda e, bi: (e, bi, 0, 0)),
        scratch_shapes=[pltpu.VMEM((d, f), BF16), pltpu.VMEM((d, f), BF16), pltpu.VMEM((f, d), BF16)],
        compiler_params=_cparams(("arbitrary", "arbitrary")),
        name="experts",
    )(hm, slot_t, gate_t, wg, wu, wd)


def _combine_kernel(y_ref, slot_ref, x_ref, gt_ref, g_ref, b_ref, o_ref, *, n_slots):
    tm = x_ref.shape[1]
    slot_id = lax.broadcasted_iota(jnp.int32, (tm, n_slots), 1).astype(F32)
    acc = jnp.zeros((tm, x_ref.shape[2]), F32)
    for e in range(N_EXPERTS):
        hit = slot_ref[0, :, e:e + 1] == slot_id
        acc = acc + jnp.dot(hit.astype(BF16), y_ref[e, 0], preferred_element_type=F32)
    z = DEEPNORM_ALPHA * x_ref[0] + (1.0 + gt_ref[0]) * acc
    o_ref[0] = _ln(z) * g_ref[...] + b_ref[...]


def _combine(y, slot_c, x_mid, gt, g, b, tm):
    nb, s, d = x_mid.shape
    n_slots = y.shape[2]
    per_sample = gt.shape[0] > 1
    mod_map = (lambda bi, i: (bi, 0, 0)) if per_sample else (lambda bi, i: (0, 0, 0))
    vec = pl.BlockSpec((1, d), lambda bi, i: (0, 0))
    return pl.pallas_call(
        functools.partial(_combine_kernel, n_slots=n_slots),
        out_shape=jax.ShapeDtypeStruct((nb, s, d), F32),
        grid=(nb, s // tm),
        in_specs=[
            pl.BlockSpec((N_EXPERTS, 1, n_slots, d), lambda bi, i: (0, bi, 0, 0)),
            pl.BlockSpec((1, tm, LANES), lambda bi, i: (bi, i, 0)),
            pl.BlockSpec((1, tm, d), lambda bi, i: (bi, i, 0)),
            pl.BlockSpec((1, 1, d), mod_map), vec, vec,
        ],
        out_specs=pl.BlockSpec((1, tm, d), lambda bi, i: (bi, i, 0)),
        compiler_params=_cparams(("arbitrary", "arbitrary")),
        name="combine_postnorm",
    )(y, slot_c, x_mid, gt, g, b)


def kernel(x, c, ctx, c_ctx, w_mod, b_mod, w_in, b_in, w_short, w_conf_dw, b_conf_dw, g_conf_ln, b_conf_ln,
           na_rpb, w_out, b_out, g_post1, b_post1, w_router, w_gate, w_up, w_down, g_post2, b_post2):
    bsz, seq, d = x.shape
    nctx = ctx.shape[1]
    cap = EC_CAPACITY_FACTOR * seq // N_EXPERTS
    cap_ctx = EC_CAPACITY_FACTOR * nctx // N_EXPERTS
    q_scale = NA_HEAD_DIM ** -0.5 * LOG2E

    cond = jnp.concatenate([c, c_ctx[None, :], jnp.zeros((MOD_ROWS - bsz - 1, d), F32)], axis=0)
    mods = _modulation(cond, w_mod, b_mod)

    lat_splits = ((3 * D_CONV, 1.0), (2 * D_CONF, 1.0), (D_NA, q_scale), (D_NA, 1.0), (D_NA, 1.0))
    lat_dtypes = (F32, F32, BF16, BF16, BF16)
    kv_splits = ((D_NA, 1.0), (D_NA, 1.0))
    b_in3 = b_in[:, None, :]

    xc = ctx
    for l in range(DEPTH):
        last = l == DEPTH - 1
        m_lat = [mods[l, :bsz, i * d:(i + 1) * d][:, None, :] for i in range(6)]
        m_ctx = [mods[l, bsz:bsz + 1, i * d:(i + 1) * d][:, None, :] for i in range(6)]
        b_out_l = b_out[l][None, :]
        wr = jnp.pad(w_router[l], ((0, 0), (0, LANES - N_EXPERTS)))
        wr_hi = wr.astype(BF16)
        wr_lo = (wr - wr_hi.astype(F32)).astype(BF16)
        g1, b1 = g_post1[l][None, :], b_post1[l][None, :]
        g2, b2 = g_post2[l][None, :], b_post2[l][None, :]
        conv_w = (w_short[l], w_conf_dw[l], b_conf_dw[l], g_conf_ln[l], b_conf_ln[l])

        if last:
            k_c, v_c = _inproj(xc, m_ctx[0], m_ctx[1], w_in, b_in3, l, OFF_K, kv_splits, (BF16, BF16), tm=nctx)
        else:
            uac, ubc, q_c, k_c, v_c = _inproj(xc, m_ctx[0], m_ctx[1], w_in, b_in3, l, 0,
                                              lat_splits, lat_dtypes, tm=nctx)

        ua, ub, q, k, v = _inproj(x, m_lat[0], m_lat[1], w_in, b_in3, l, 0, lat_splits, lat_dtypes, tm=512)
        yab = _conv_mixers(ua, ub, *conv_w)
        yc = _neighbourhood_attention(q, k, v, k_c, v_c, na_rpb[l])
        x_mid, hm, logits = _outproj(yab, yc, x, w_out, l, b_out_l, m_lat[2], g1, b1, m_lat[3], m_lat[4],
                                     wr_hi, wr_lo, tm=512)

        slot_t, gate_t, slot_c = _route(logits, cap, 0)
        y_e = _experts(hm, slot_t, gate_t, w_gate, w_up, w_down, l, cap)
        x = _combine(y_e, slot_c, x_mid, m_lat[5], g2, b2, tm=512)

        if not last:
            yabc = _conv_mixers(uac, ubc, *conv_w)
            ycc = _context_attention(q_c, k_c, v_c)
            xc_mid, hmc, logits_c = _outproj(yabc, ycc, xc, w_out, l, b_out_l, m_ctx[2], g1, b1,
                                             m_ctx[3], m_ctx[4], wr_hi, wr_lo, tm=nctx)
            slot_tc, gate_tc, slot_cc = _route(logits_c, cap_ctx, cap_ctx)
            flat_t = lambda t: jnp.transpose(t, (1, 0, 2)).reshape(1, N_EXPERTS, bsz * nctx)
            y_ec = _experts(hmc.reshape(1, bsz * nctx, d), flat_t(slot_tc), flat_t(gate_tc),
                            w_gate, w_up, w_down, l, bsz * cap_ctx)
            xc = _combine(y_ec, slot_cc.reshape(1, bsz * nctx, LANES), xc_mid.reshape(1, bsz * nctx, d),
                          m_ctx[5], g2, b2, tm=512).reshape(bsz, nctx, d)
    return x
```

```python
import functools
import math

import numpy as np
import jax
import jax.numpy as jnp
from jax import lax
from jax.experimental import pallas as pl
from jax.experimental.pallas import tpu as pltpu

F32 = jnp.float32
BF16 = jnp.bfloat16

D_MODEL = 1024
DEPTH = 2
GRID_W = 64
D_CONV = D_MODEL // 4
D_CONF = D_MODEL // 4
NA_HEAD_DIM = 64
D_NA = D_MODEL - D_CONV - D_CONF
N_NA_HEADS = D_NA // NA_HEAD_DIM
SHORT_CONV_W = 3
CONF_CONV_W = 31
NA_WIN_ROWS_MAX = 8
NA_WIN_COLS = 16
N_EXPERTS = 16
EC_CAPACITY_FACTOR = 2
D_EXPERT = 1024
LN_EPS = 1e-5
DEEPNORM_ALPHA = (2.0 * DEPTH) ** 0.25
NEG_INF = -1e30
LOG2E = math.log2(math.e)

OFF_A = 0
OFF_B = OFF_A + 3 * D_CONV
OFF_Q = OFF_B + 2 * D_CONF
OFF_K = OFF_Q + D_NA
OFF_V = OFF_K + D_NA
D_IN = OFF_V + D_NA

LANES = 128
SUBLANES = 8
MOD_ROWS = 16
VMEM_LIMIT = 56 * 1024 * 1024
ATTN_ROWS = 4
N_PAIRS = D_NA // LANES
HI = lax.Precision.HIGHEST


def _cparams(sem):
    return pltpu.CompilerParams(dimension_semantics=sem, vmem_limit_bytes=VMEM_LIMIT)


def _ln(x):
    mu = jnp.mean(x, axis=-1, keepdims=True)
    xc = x - mu
    var = jnp.mean(xc * xc, axis=-1, keepdims=True)
    return xc * lax.rsqrt(var + LN_EPS)


def _sigmoid(x):
    return 1.0 / (1.0 + jnp.exp(-x))


def _mod_kernel(cond_ref, w_ref, b_ref, o_ref):
    s = cond_ref[...]
    s = s * _sigmoid(s)
    o_ref[0] = jnp.dot(s, w_ref[0], preferred_element_type=F32, precision=HI) + b_ref[0]


def _modulation(cond, w_mod, b_mod):
    n_l, d, n = w_mod.shape
    tn = 1536
    return pl.pallas_call(
        _mod_kernel,
        out_shape=jax.ShapeDtypeStruct((n_l, MOD_ROWS, n), F32),
        grid=(n_l, n // tn),
        in_specs=[
            pl.BlockSpec((MOD_ROWS, d), lambda l, j: (0, 0)),
            pl.BlockSpec((1, d, tn), lambda l, j: (l, 0, j)),
            pl.BlockSpec((1, 1, tn), lambda l, j: (l, 0, j)),
        ],
        out_specs=pl.BlockSpec((1, MOD_ROWS, tn), lambda l, j: (l, 0, j)),
        compiler_params=_cparams(("arbitrary", "arbitrary")),
        name="modulation",
    )(cond, w_mod, b_mod.reshape(n_l, 1, n))


def _first_step():
    return (pl.program_id(0) == 0) & (pl.program_id(1) == 0)


def _inproj_kernel(x_ref, sh_ref, sc_ref, w_ref, b_ref, *rest, splits, col0):
    *o_refs, w_bf = rest

    @pl.when(_first_step())
    def _():
        w_bf[...] = w_ref[0].astype(BF16)

    h = _ln(x_ref[0]) * (1.0 + sc_ref[0]) + sh_ref[0]
    u = jnp.dot(h.astype(BF16), w_bf[:, col0:], preferred_element_type=F32) + b_ref[0, :, col0:]
    off = 0
    for o_ref, (width, scale) in zip(o_refs, splits):
        part = u[:, off:off + width]
        if scale != 1.0:
            part = part * scale
        o_ref[0] = part.astype(o_ref.dtype)
        off += width


def _inproj(x, sh, sc, w, b, layer, col0, splits, dtypes, tm):
    nb, s, d = x.shape
    n = w.shape[2]
    per_sample = sh.shape[0] > 1
    mod_map = (lambda bi, i: (bi, 0, 0)) if per_sample else (lambda bi, i: (0, 0, 0))
    return pl.pallas_call(
        functools.partial(_inproj_kernel, splits=splits, col0=col0),
        out_shape=[jax.ShapeDtypeStruct((nb, s, wd), dt) for (wd, _), dt in zip(splits, dtypes)],
        grid=(nb, s // tm),
        in_specs=[
            pl.BlockSpec((1, tm, d), lambda bi, i: (bi, i, 0)),
            pl.BlockSpec((1, 1, d), mod_map),
            pl.BlockSpec((1, 1, d), mod_map),
            pl.BlockSpec((1, d, n), lambda bi, i: (layer, 0, 0), pipeline_mode=pl.Buffered(1)),
            pl.BlockSpec((1, 1, n), lambda bi, i: (layer, 0, 0)),
        ],
        out_specs=[pl.BlockSpec((1, tm, wd), lambda bi, i: (bi, i, 0)) for wd, _ in splits],
        scratch_shapes=[pltpu.VMEM((d, n), BF16)],
        compiler_params=_cparams(("arbitrary", "arbitrary")),
        name="inproj",
    )(x, sh, sc, w, b)


CONV_CHUNK = 128
Z_PAD = 8
H_PAD = 16


def _conv_kernel(ua_ref, ub_ref, ws_ref, wd_ref, bd_ref, g_ref, b_ref, o_ref, z_scr, h_scr, *, seq):
    c = D_CONV
    z_scr[0:Z_PAD, :] = jnp.zeros((Z_PAD, c), F32)
    z_scr[Z_PAD + seq:2 * Z_PAD + seq, :] = jnp.zeros((Z_PAD, c), F32)
    h_scr[0, 0:H_PAD, :] = jnp.zeros((H_PAD, c), F32)
    h_scr[0, H_PAD + seq:2 * H_PAD + seq, :] = jnp.zeros((H_PAD, c), F32)
    z_scr[Z_PAD:Z_PAD + seq, :] = ua_ref[0, :, c:2 * c] * ua_ref[0, :, 2 * c:3 * c]
    h_scr[0, H_PAD:H_PAD + seq, :] = ub_ref[0, :, 0:c] * _sigmoid(ub_ref[0, :, c:2 * c])
    n_shift = seq + 2 * H_PAD - SUBLANES
    for r in range(1, SUBLANES):
        h_scr[r, 0:n_shift, :] = h_scr[0, r:r + n_shift, :]
    tc = min(CONV_CHUNK, seq)
    for ci in range(seq // tc):
        t0 = ci * tc
        acc = ws_ref[0:1, :] * z_scr[t0 + Z_PAD - 1:t0 + Z_PAD - 1 + tc, :]
        for j in range(1, SHORT_CONV_W):
            s0 = t0 + Z_PAD - 1 + j
            acc = acc + ws_ref[j:j + 1, :] * z_scr[s0:s0 + tc, :]
        ya = ua_ref[0, t0:t0 + tc, 0:c] * acc
        hb = bd_ref[...]
        for j in range(CONF_CONV_W):
            s0 = t0 + H_PAD - CONF_CONV_W // 2 + j
            a0 = s0 - s0 % SUBLANES
            hb = hb + wd_ref[j:j + 1, :] * h_scr[s0 % SUBLANES, a0:a0 + tc, :]
        hn = _ln(hb) * g_ref[...] + b_ref[...]
        yb = hn * _sigmoid(hn)
        o_ref[0, t0:t0 + tc, 0:c] = ya.astype(o_ref.dtype)
        o_ref[0, t0:t0 + tc, c:2 * c] = yb.astype(o_ref.dtype)


def _conv_mixers(ua, ub, w_short, w_dw, b_dw, g_ln, b_ln):
    nb, s, _ = ua.shape
    c = D_CONV
    full = lambda shape: pl.BlockSpec(shape, lambda bi: (0,) * len(shape))
    return pl.pallas_call(
        functools.partial(_conv_kernel, seq=s),
        out_shape=jax.ShapeDtypeStruct((nb, s, 2 * c), BF16),
        grid=(nb,),
        in_specs=[
            pl.BlockSpec((1, s, 3 * c), lambda bi: (bi, 0, 0)),
            pl.BlockSpec((1, s, 2 * c), lambda bi: (bi, 0, 0)),
            full((SHORT_CONV_W, c)), full((CONF_CONV_W, c)), full((1, c)), full((1, c)), full((1, c)),
        ],
        out_specs=pl.BlockSpec((1, s, 2 * c), lambda bi: (bi, 0, 0)),
        scratch_shapes=[pltpu.VMEM((s + 2 * Z_PAD, c), F32), pltpu.VMEM((SUBLANES, s + 2 * H_PAD, c), F32)],
        compiler_params=_cparams(("arbitrary",)),
        name="conv_mixers",
    )(ua, ub, w_short, w_dw, b_dw.reshape(1, c), g_ln.reshape(1, c), b_ln.reshape(1, c))


SUB_ROWS = 2
WIN_ROWS = SUB_ROWS + NA_WIN_ROWS_MAX - 1
BAND_KEYS = WIN_ROWS * GRID_W
SUB_Q = SUB_ROWS * GRID_W
N_DROW = 2 * NA_WIN_ROWS_MAX - 1
N_DCOL = 2 * NA_WIN_COLS - 1


def _bias_kernel(rpb_ref, o_ref):
    n_rows, n_cols = o_ref.shape
    col = lax.broadcasted_iota(jnp.int32, (LANES, n_cols), 1)
    qc = col >> 6
    kc = col & (GRID_W - 1)
    d_col = jnp.clip(kc - qc + (NA_WIN_COLS - 1), 0, N_DCOL - 1)
    onehot = (lax.broadcasted_iota(jnp.int32, (LANES, n_cols), 0) == d_col).astype(F32)
    vals = jnp.dot(rpb_ref[...], onehot, preferred_element_type=F32, precision=HI)
    col_r = lax.broadcasted_iota(jnp.int32, (n_rows, n_cols), 1)
    qc_r = col_r >> 6
    kc_r = col_r & (GRID_W - 1)
    c0 = jnp.clip(qc_r - NA_WIN_COLS // 2, 0, GRID_W - NA_WIN_COLS)
    inside = (kc_r >= c0) & (kc_r < c0 + NA_WIN_COLS)
    o_ref[...] = jnp.where(inside, vals * LOG2E, NEG_INF)


N_SLABS = -(-WIN_ROWS // 2)
TILES_PER_KIND = N_NA_HEADS * N_DROW
MASKED_TILE = 3 * TILES_PER_KIND


def _na_plan(rows):
    wr = min(NA_WIN_ROWS_MAX, rows)
    assert wr == NA_WIN_ROWS_MAX and rows % SUB_ROWS == 0 and rows >= WIN_ROWS
    row_start = np.clip(np.arange(rows) - wr // 2, 0, rows - wr)
    w0s, tiles = [], []
    for r0 in range(0, rows, SUB_ROWS):
        w0 = int(np.clip(r0 - wr // 2, 0, rows - WIN_ROWS))
        for iq in range(SUB_ROWS):
            r = r0 + iq
            assert row_start[r] >= w0 and row_start[r] + wr <= w0 + WIN_ROWS
            ok = lambda w: w < WIN_ROWS and row_start[r] <= w0 + w < row_start[r] + wr
            d_row = lambda w: w0 + w - r + NA_WIN_ROWS_MAX - 1
            for j in range(N_SLABS):
                lo, hi = ok(2 * j), ok(2 * j + 1)
                if lo and hi:
                    tiles.append(d_row(2 * j))
                elif lo:
                    tiles.append(TILES_PER_KIND + d_row(2 * j))
                elif hi:
                    tiles.append(2 * TILES_PER_KIND + d_row(2 * j + 1))
                else:
                    tiles.append(-1)
        w0s.append(w0)
    return np.array(w0s, np.int32), np.array(tiles, np.int32)


def _na_bias(rpb):
    assert GRID_W == 64 and 2 * GRID_W == LANES and TILES_PER_KIND <= LANES and N_DCOL <= LANES
    rpb2 = jnp.pad(rpb.astype(F32).reshape(TILES_PER_KIND, N_DCOL),
                   ((0, LANES - TILES_PER_KIND), (0, LANES - N_DCOL)))
    table = pl.pallas_call(
        _bias_kernel,
        out_shape=jax.ShapeDtypeStruct((LANES, GRID_W * GRID_W), F32),
        name="na_bias",
    )(rpb2)
    table = table[:TILES_PER_KIND].reshape(N_NA_HEADS, N_DROW, GRID_W, GRID_W)
    masked = jnp.full_like(table, NEG_INF)
    nxt = jnp.concatenate([table[:, 1:], masked[:, :1]], axis=1)
    kinds = [jnp.concatenate(pair, axis=-1).reshape(TILES_PER_KIND, GRID_W, LANES)
             for pair in ((table, nxt), (table, masked), (masked, table))]
    return jnp.concatenate(kinds + [jnp.full((1, GRID_W, LANES), NEG_INF, F32)], axis=0)


def _lane_reduce(xs, combine, reduce, neutral):
    chunks = []
    for x in xs:
        rows, n = x.shape
        n_full = n // LANES
        chunks += [x[:, j * LANES:(j + 1) * LANES] for j in range(n_full)]
        if n % LANES:
            fill = jnp.full((rows, LANES - n % LANES), neutral, x.dtype)
            chunks.append(jnp.concatenate([x[:, n_full * LANES:], fill], axis=1))
    return reduce(functools.reduce(combine, chunks), axis=-1, keepdims=True)


def _attn_kernel(w0_ref, tile_ref, q_ref, k_ref, v_ref, kc_ref, vc_ref, *rest, banded, n_sub, sub_q):
    if banded:
        bias_ref, o_ref = rest
    else:
        (o_ref,) = rest
    lane = lax.broadcasted_iota(jnp.int32, (sub_q, LANES), 1)
    first = lane < NA_HEAD_DIM
    nt = (((1,), (1,)), ((), ()))
    stages = [(si, p) for si in range(n_sub) for p in range(N_PAIRS)]

    def window(si):
        blk = pl.program_id(1) * n_sub + si
        return blk, pl.multiple_of(w0_ref[blk] * GRID_W, GRID_W)

    def bias(blk, head):
        row_blocks = []
        for iq in range(SUB_ROWS):
            slabs = []
            for j in range(N_SLABS):
                t = tile_ref[(blk * SUB_ROWS + iq) * N_SLABS + j]
                tile = bias_ref[jnp.where(t < 0, MASKED_TILE, t + head * N_DROW)]
                width = min(LANES, BAND_KEYS - j * LANES)
                slabs.append(tile[:, :width])
            row_blocks.append(jnp.concatenate(slabs, axis=1))
        return jnp.concatenate(row_blocks, axis=0)

    def scores(si, p):
        cols = slice(p * LANES, (p + 1) * LANES)
        q_p = q_ref[0, si * sub_q:(si + 1) * sub_q, cols]
        zero = jnp.zeros_like(q_p)
        qq = jnp.concatenate([jnp.where(first, q_p, zero), jnp.where(first, zero, q_p)], axis=0)
        parts = [lax.dot_general(qq, kc_ref[0, :, cols], nt, preferred_element_type=F32)]
        if banded:
            blk, start = window(si)
            both = jnp.concatenate([bias(blk, 2 * p), bias(blk, 2 * p + 1)], axis=0)
            parts.append(lax.dot_general(qq, k_ref[0, pl.ds(start, BAND_KEYS), cols], nt,
                                         preferred_element_type=F32) + both)
        return parts

    def finish(si, p, parts):
        cols = slice(p * LANES, (p + 1) * LANES)
        m = _lane_reduce(parts, jnp.maximum, jnp.max, NEG_INF)
        es = [jnp.exp2(s - m) for s in parts]
        den = _lane_reduce(es, jnp.add, jnp.sum, 0.0)
        o = jnp.dot(es[0].astype(BF16), vc_ref[0, :, cols], preferred_element_type=F32)
        if banded:
            _, start = window(si)
            o = o + jnp.dot(es[1].astype(BF16), v_ref[0, pl.ds(start, BAND_KEYS), cols],
                            preferred_element_type=F32)
        o = o * (1.0 / den)
        out = jnp.where(first, o[:sub_q], o[sub_q:])
        o_ref[0, si * sub_q:(si + 1) * sub_q, cols] = out.astype(o_ref.dtype)

    nxt = scores(*stages[0])
    for i, (si, p) in enumerate(stages):
        cur = nxt
        if i + 1 < len(stages):
            nxt = scores(*stages[i + 1])
        finish(si, p, cur)


def _neighbourhood_attention(q, k, v, kc, vc, rpb):
    nb, s, dn = q.shape
    rows = s // GRID_W
    nctx = kc.shape[1]
    w0s, tiles = _na_plan(rows)
    bias = _na_bias(rpb)
    n_sub = ATTN_ROWS // SUB_ROWS
    m_rows = ATTN_ROWS * GRID_W
    grid_spec = pltpu.PrefetchScalarGridSpec(
        num_scalar_prefetch=2,
        grid=(nb, rows // ATTN_ROWS),
        in_specs=[
            pl.BlockSpec((1, m_rows, dn), lambda bi, i, w0, pat: (bi, i, 0)),
            pl.BlockSpec((1, s, dn), lambda bi, i, w0, pat: (bi, 0, 0)),
            pl.BlockSpec((1, s, dn), lambda bi, i, w0, pat: (bi, 0, 0)),
            pl.BlockSpec((1, nctx, dn), lambda bi, i, w0, pat: (bi, 0, 0)),
            pl.BlockSpec((1, nctx, dn), lambda bi, i, w0, pat: (bi, 0, 0)),
            pl.BlockSpec(bias.shape, lambda bi, i, w0, pat: (0, 0, 0), pipeline_mode=pl.Buffered(1)),
        ],
        out_specs=pl.BlockSpec((1, m_rows, dn), lambda bi, i, w0, pat: (bi, i, 0)),
    )
    return pl.pallas_call(
        functools.partial(_attn_kernel, banded=True, n_sub=n_sub, sub_q=SUB_Q),
        out_shape=jax.ShapeDtypeStruct((nb, s, dn), BF16),
        grid_spec=grid_spec,
        compiler_params=_cparams(("arbitrary", "arbitrary")),
        name="neighbourhood_attention",
    )(jnp.asarray(w0s), jnp.asarray(tiles), q, k, v, kc, vc, bias)


def _context_attention(q, kc, vc):
    nb, s, dn = q.shape
    spec = pl.BlockSpec((1, s, dn), lambda bi, i, w0, pat: (bi, 0, 0))
    grid_spec = pltpu.PrefetchScalarGridSpec(
        num_scalar_prefetch=2, grid=(nb, 1), in_specs=[spec] * 5, out_specs=spec)
    dummy = jnp.zeros((1,), jnp.int32)
    return pl.pallas_call(
        functools.partial(_attn_kernel, banded=False, n_sub=1, sub_q=s),
        out_shape=jax.ShapeDtypeStruct((nb, s, dn), BF16),
        grid_spec=grid_spec,
        compiler_params=_cparams(("arbitrary", "arbitrary")),
        name="context_attention",
    )(dummy, dummy, q, kc, vc, kc, vc)


def _store_rowmajor(ref, val, base=0):
    n, width = val.shape
    chunks = width // LANES
    for c in range(chunks):
        ref[pl.ds(base + c, n, stride=chunks), :] = val[:, c * LANES:(c + 1) * LANES]


def _load_rowmajor(ref, base, n, chunks):
    return jnp.concatenate([ref[pl.ds(base + c, n, stride=chunks), :] for c in range(chunks)], axis=1)


def _outproj_kernel(yab_ref, yc_ref, x_ref, w_ref, bo_ref, gt_ref, g_ref, b_ref, sh_ref, sc_ref,
                    wrh_ref, wrl_ref, xmid_ref, hm_ref, lg_ref, w_bf):
    @pl.when(_first_step())
    def _():
        w_bf[...] = w_ref[0].astype(BF16)

    half = yab_ref.shape[2]
    y = (jnp.dot(yab_ref[0], w_bf[0:half, :], preferred_element_type=F32)
         + jnp.dot(yc_ref[0], w_bf[half:, :], preferred_element_type=F32) + bo_ref[...])
    xm = _ln(DEEPNORM_ALPHA * x_ref[0] + (1.0 + gt_ref[0]) * y) * g_ref[...] + b_ref[...]
    xmid_ref[0] = xm
    hm = _ln(xm) * (1.0 + sc_ref[0]) + sh_ref[0]
    _store_rowmajor(hm_ref, hm)
    hm_hi = hm.astype(BF16)
    hm_lo = (hm - hm_hi.astype(F32)).astype(BF16)
    lg_ref[0] = (jnp.dot(hm_hi, wrh_ref[...], preferred_element_type=F32)
                 + jnp.dot(hm_lo, wrh_ref[...], preferred_element_type=F32)
                 + jnp.dot(hm_hi, wrl_ref[...], preferred_element_type=F32))


def _outproj(yab, yc, x, w, layer, bo, gt, g, b, sh, sc, wr_hi, wr_lo, tm):
    nb, s, d = x.shape
    half = yab.shape[2]
    per_sample = gt.shape[0] > 1
    mod_map = (lambda bi, i: (bi, 0, 0)) if per_sample else (lambda bi, i: (0, 0, 0))
    vec = pl.BlockSpec((1, d), lambda bi, i: (0, 0))
    mod = pl.BlockSpec((1, 1, d), mod_map)
    tok = lambda width: pl.BlockSpec((1, tm, width), lambda bi, i: (bi, i, 0))
    n_i = s // tm
    return pl.pallas_call(
        _outproj_kernel,
        out_shape=[jax.ShapeDtypeStruct((nb, s, d), F32),
                   jax.ShapeDtypeStruct((nb * s * (d // LANES), LANES), F32),
                   jax.ShapeDtypeStruct((nb, s, LANES), F32)],
        grid=(nb, n_i),
        in_specs=[tok(half), tok(half), tok(d),
                  pl.BlockSpec((1, d, d), lambda bi, i: (layer, 0, 0), pipeline_mode=pl.Buffered(1)),
                  vec, mod, vec, vec,
                  mod, mod, pl.BlockSpec((d, LANES), lambda bi, i: (0, 0)),
                  pl.BlockSpec((d, LANES), lambda bi, i: (0, 0))],
        out_specs=[tok(d), pl.BlockSpec((tm * (d // LANES), LANES), lambda bi, i: (bi * n_i + i, 0)),
                   tok(LANES)],
        scratch_shapes=[pltpu.VMEM((d, d), BF16)],
        compiler_params=_cparams(("arbitrary", "arbitrary")),
        name="outproj_postnorm",
    )(yab, yc, x, w, bo, gt, g, b, sh, sc, wr_hi, wr_lo)


CUM_CHUNK = 256
F32_EXP_BIAS = 127
F32_MANT_BITS = 23


def _prefix_count(mask_f32, tri):
    rows, n = mask_f32.shape
    tc = min(CUM_CHUNK, n)
    base = jnp.zeros((rows, 1), F32)
    parts = []
    for ci in range(n // tc):
        blk = mask_f32[:, ci * tc:(ci + 1) * tc]
        parts.append(jnp.dot(blk.astype(BF16), tri[:tc, :tc], preferred_element_type=F32) + base)
        base = base + jnp.sum(blk, axis=-1, keepdims=True)
    return jnp.concatenate(parts, axis=-1)


def _pow2(k):
    return pltpu.bitcast((k + F32_EXP_BIAS) << F32_MANT_BITS, F32)


def _route_kernel(lg_ref, slot_c_ref, gate_c_ref, *, cap, slot_stride):
    nb = lg_ref.shape[0]
    assert nb * N_EXPERTS == LANES
    rows = []
    for b in range(nb):
        lg = lg_ref[b]
        lane = lax.broadcasted_iota(jnp.int32, lg.shape, 1)
        lgm = jnp.where(lane < N_EXPERTS, lg, NEG_INF)
        ex = jnp.exp(lgm - jnp.max(lgm, axis=-1, keepdims=True))
        aff = ex / jnp.sum(ex, axis=-1, keepdims=True)
        gate_c_ref[b] = aff
        rows.append(aff.T[0:N_EXPERTS, :])
    a = jnp.concatenate(rows, axis=0)
    capf = float(cap)

    def enough(t):
        return jnp.sum((a >= t).astype(F32), axis=-1, keepdims=True) >= capf

    def exp_step(_, carry):
        lo, hi = carry
        mid = lo + ((hi - lo + 1) >> 1)
        ok = enough(_pow2(mid))
        return jnp.where(ok, mid, lo), jnp.where(ok, hi, mid - 1)

    k_lo = jnp.full((LANES, 1), -F32_EXP_BIAS, jnp.int32)
    k_hi = jnp.zeros((LANES, 1), jnp.int32)
    k_lo, _ = lax.fori_loop(0, 7, exp_step, (k_lo, k_hi))
    base = _pow2(k_lo)

    def mant_step(_, carry):
        t, step = carry
        step = step * 0.5
        cand = t + step
        return jnp.where(enough(cand), cand, t), step

    thr, _ = lax.fori_loop(0, F32_MANT_BITS, mant_step, (base, base))

    r_i = lax.broadcasted_iota(jnp.int32, (CUM_CHUNK, CUM_CHUNK), 0)
    c_i = lax.broadcasted_iota(jnp.int32, (CUM_CHUNK, CUM_CHUNK), 1)
    tri = (r_i < c_i).astype(BF16)
    gt = (a > thr).astype(F32)
    eq = (a == thr).astype(F32)
    need = capf - jnp.sum(gt, axis=-1, keepdims=True)
    sel = gt + eq * (_prefix_count(eq, tri) < need).astype(F32)
    pos = _prefix_count(sel, tri)
    sample = lax.broadcasted_iota(jnp.int32, (LANES, 1), 0) >> (N_EXPERTS.bit_length() - 1)
    slot = jnp.where(sel > 0.0, pos + (sample * slot_stride).astype(F32), -1.0)
    for b in range(nb):
        lo = b * N_EXPERTS
        rolled = slot if b == 0 else jnp.concatenate([slot[lo:, :], slot[:lo, :]], axis=0)
        slot_c_ref[b] = rolled.T


def _route(logits, cap, slot_stride):
    nb, s, _ = logits.shape
    whole = lambda shape: pl.BlockSpec(shape, lambda i: (0,) * len(shape))
    return pl.pallas_call(
        functools.partial(_route_kernel, cap=cap, slot_stride=slot_stride),
        out_shape=[jax.ShapeDtypeStruct((nb, s, LANES), F32), jax.ShapeDtypeStruct((nb, s, LANES), F32)],
        grid=(1,),
        in_specs=[whole((nb, s, LANES))],
        out_specs=[whole((nb, s, LANES)), whole((nb, s, LANES))],
        compiler_params=_cparams(("arbitrary",)),
        name="route",
    )(logits)


def _slot_list_kernel(slot_ref, gate_ref, idx_ref, g_ref, *, n_slots, tok_stride, merge):
    s = slot_ref.shape[1]
    b = pl.program_id(0)
    slot_id = lax.broadcasted_iota(jnp.int32, (s, n_slots), 1).astype(F32)
    tok = (lax.broadcasted_iota(jnp.int32, (s, n_slots), 0) + b * tok_stride).astype(F32)
    idx_rows, g_rows = [], []
    for e in range(N_EXPERTS):
        hit = slot_ref[0, :, e:e + 1] == slot_id
        idx_rows.append(jnp.sum(jnp.where(hit, tok, 0.0), axis=0, keepdims=True))
        g_rows.append(jnp.sum(jnp.where(hit, gate_ref[0, :, e:e + 1], 0.0), axis=0, keepdims=True))
    idx = jnp.concatenate(idx_rows, axis=0).astype(jnp.int32)
    g = jnp.concatenate(g_rows, axis=0)
    if merge:
        @pl.when(b == 0)
        def _():
            idx_ref[0] = idx
            g_ref[0] = g

        @pl.when(b > 0)
        def _():
            idx_ref[0] = idx_ref[0] + idx
            g_ref[0] = g_ref[0] + g
    else:
        idx_ref[0] = idx
        g_ref[0] = g


def _slot_lists(slot_c, gate_c, n_slots, merge):
    nb, s, _ = slot_c.shape
    nbo = 1 if merge else nb
    out_map = (lambda bi: (0, 0, 0)) if merge else (lambda bi: (bi, 0, 0))
    return pl.pallas_call(
        functools.partial(_slot_list_kernel, n_slots=n_slots, tok_stride=s if merge else 0, merge=merge),
        out_shape=[jax.ShapeDtypeStruct((nbo, N_EXPERTS, n_slots), jnp.int32),
                   jax.ShapeDtypeStruct((nbo, N_EXPERTS, n_slots), F32)],
        grid=(nb,),
        in_specs=[pl.BlockSpec((1, s, LANES), lambda bi: (bi, 0, 0)),
                  pl.BlockSpec((1, s, LANES), lambda bi: (bi, 0, 0))],
        out_specs=[pl.BlockSpec((1, N_EXPERTS, n_slots), out_map), pl.BlockSpec((1, N_EXPERTS, n_slots), out_map)],
        compiler_params=_cparams(("arbitrary",)),
        name="slot_lists",
    )(slot_c, gate_c)


ROW_UNROLL = 8


def _dispatch_kernel(idx_ref, hm_ref, x_ref, rows_scr, *, n_slots):
    b = pl.program_id(0)
    chunks = x_ref.shape[2] // LANES
    for e in range(N_EXPERTS):
        base = (b * N_EXPERTS + e) * n_slots

        def move(i, carry):
            for u in range(ROW_UNROLL):
                s = i * ROW_UNROLL + u
                src = pl.multiple_of(idx_ref[base + s] * chunks, chunks)
                dst = pl.multiple_of(s * chunks, chunks)
                rows_scr[pl.ds(dst, chunks), :] = hm_ref[pl.ds(src, chunks), :]
            return carry

        lax.fori_loop(0, n_slots // ROW_UNROLL, move, 0)
        x_ref[e] = _load_rowmajor(rows_scr, 0, n_slots, chunks).astype(x_ref.dtype)


def _dispatch(idx, hm_rm, nb, s, d, n_slots):
    chunks = d // LANES
    assert chunks == SUBLANES and n_slots % ROW_UNROLL == 0
    grid_spec = pltpu.PrefetchScalarGridSpec(
        num_scalar_prefetch=1,
        grid=(nb,),
        in_specs=[pl.BlockSpec((s * chunks, LANES), lambda bi, idx_r: (bi, 0))],
        out_specs=pl.BlockSpec((N_EXPERTS, n_slots, d), lambda bi, idx_r: (0, bi, 0)),
        scratch_shapes=[pltpu.VMEM((n_slots * chunks, LANES), F32)],
    )
    return pl.pallas_call(
        functools.partial(_dispatch_kernel, n_slots=n_slots),
        out_shape=jax.ShapeDtypeStruct((N_EXPERTS, nb * n_slots, d), BF16),
        grid_spec=grid_spec,
        compiler_params=_cparams(("arbitrary",)),
        name="dispatch",
    )(idx, hm_rm)


def _expert_kernel(x_ref, wg_ref, wu_ref, wd_ref, y_ref, wg_bf, wu_bf, wd_bf):
    @pl.when(pl.program_id(1) == 0)
    def _():
        wg_bf[...] = wg_ref[0, 0].astype(BF16)
        wu_bf[...] = wu_ref[0, 0].astype(BF16)
        wd_bf[...] = wd_ref[0, 0].astype(BF16)

    x_e = x_ref[0]
    a = jnp.dot(x_e, wg_bf[...], preferred_element_type=F32)
    u = jnp.dot(x_e, wu_bf[...], preferred_element_type=F32)
    h = (a * _sigmoid(a) * u).astype(BF16)
    _store_rowmajor(y_ref.at[0], jnp.dot(h, wd_bf[...], preferred_element_type=F32))


def _experts(x_e, wg, wu, wd, layer, tm):
    _, rows, d = x_e.shape
    f = wg.shape[3]
    chunks = d // LANES
    return pl.pallas_call(
        _expert_kernel,
        out_shape=jax.ShapeDtypeStruct((N_EXPERTS, rows * chunks, LANES), F32),
        grid=(N_EXPERTS, rows // tm),
        in_specs=[
            pl.BlockSpec((1, tm, d), lambda e, i: (e, i, 0)),
            pl.BlockSpec((1, 1, d, f), lambda e, i: (layer, e, 0, 0)),
            pl.BlockSpec((1, 1, d, f), lambda e, i: (layer, e, 0, 0)),
            pl.BlockSpec((1, 1, f, d), lambda e, i: (layer, e, 0, 0)),
        ],
        out_specs=pl.BlockSpec((1, tm * chunks, LANES), lambda e, i: (e, i, 0)),
        scratch_shapes=[pltpu.VMEM((d, f), BF16), pltpu.VMEM((d, f), BF16), pltpu.VMEM((f, d), BF16)],
        compiler_params=_cparams(("arbitrary", "arbitrary")),
        name="experts",
    )(x_e, wg, wu, wd)


COMBINE_EXPERTS = 4
NORM_CHUNK = 256


def _combine_kernel(idx_ref, gate_ref, y_ref, x_ref, gt_ref, g_ref, b_ref, o_ref, acc, *, n_slots):
    b = pl.program_id(0)
    j = pl.program_id(1)
    s, d = x_ref.shape[1], x_ref.shape[2]
    chunks = d // LANES

    @pl.when(j == 0)
    def _():
        acc[...] = jnp.zeros(acc.shape, F32)

    for eg in range(COMBINE_EXPERTS):
        base = (b * N_EXPERTS + j * COMBINE_EXPERTS + eg) * n_slots

        def add(i, carry):
            new = []
            for u in range(ROW_UNROLL):
                slot = i * ROW_UNROLL + u
                src = pl.multiple_of(slot * chunks, chunks)
                dst = pl.multiple_of(idx_ref[base + slot] * chunks, chunks)
                new.append((dst, acc[pl.ds(dst, chunks), :]
                            + gate_ref[base + slot] * y_ref[eg, pl.ds(src, chunks), :]))
            for dst, val in new:
                acc[pl.ds(dst, chunks), :] = val
            return carry

        lax.fori_loop(0, n_slots // ROW_UNROLL, add, 0)

    @pl.when(j == pl.num_programs(1) - 1)
    def _():
        for ci in range(s // NORM_CHUNK):
            rows = slice(ci * NORM_CHUNK, (ci + 1) * NORM_CHUNK)
            ym = _load_rowmajor(acc, ci * NORM_CHUNK * chunks, NORM_CHUNK, chunks)
            z = DEEPNORM_ALPHA * x_ref[0, rows, :] + (1.0 + gt_ref[0]) * ym
            o_ref[0, rows, :] = _ln(z) * g_ref[...] + b_ref[...]


def _combine(idx, gates, y_rm, x_mid, gt, g, b, n_slots):
    nb, s, d = x_mid.shape
    chunks = d // LANES
    assert chunks == SUBLANES and N_EXPERTS % COMBINE_EXPERTS == 0 and s % NORM_CHUNK == 0
    per_sample = gt.shape[0] > 1
    mod_map = (lambda bi, j, i_r, g_r: (bi, 0, 0)) if per_sample else (lambda bi, j, i_r, g_r: (0, 0, 0))
    vec = pl.BlockSpec((1, d), lambda bi, j, i_r, g_r: (0, 0))
    grid_spec = pltpu.PrefetchScalarGridSpec(
        num_scalar_prefetch=2,
        grid=(nb, N_EXPERTS // COMBINE_EXPERTS),
        in_specs=[
            pl.BlockSpec((COMBINE_EXPERTS, n_slots * chunks, LANES), lambda bi, j, i_r, g_r: (j, bi, 0)),
            pl.BlockSpec((1, s, d), lambda bi, j, i_r, g_r: (bi, 0, 0)),
            pl.BlockSpec((1, 1, d), mod_map), vec, vec,
        ],
        out_specs=pl.BlockSpec((1, s, d), lambda bi, j, i_r, g_r: (bi, 0, 0)),
        scratch_shapes=[pltpu.VMEM((s * chunks, LANES), F32)],
    )
    return pl.pallas_call(
        functools.partial(_combine_kernel, n_slots=n_slots),
        out_shape=jax.ShapeDtypeStruct((nb, s, d), F32),
        grid_spec=grid_spec,
        compiler_params=_cparams(("arbitrary", "arbitrary")),
        name="combine_postnorm",
    )(idx, gates, y_rm, x_mid, gt, g, b)


def kernel(x, c, ctx, c_ctx, w_mod, b_mod, w_in, b_in, w_short, w_conf_dw, b_conf_dw, g_conf_ln, b_conf_ln,
           na_rpb, w_out, b_out, g_post1, b_post1, w_router, w_gate, w_up, w_down, g_post2, b_post2):
    bsz, seq, d = x.shape
    nctx = ctx.shape[1]
    cap = EC_CAPACITY_FACTOR * seq // N_EXPERTS
    cap_ctx = EC_CAPACITY_FACTOR * nctx // N_EXPERTS
    q_scale = NA_HEAD_DIM ** -0.5 * LOG2E

    cond = jnp.concatenate([c, c_ctx[None, :], jnp.zeros((MOD_ROWS - bsz - 1, d), F32)], axis=0)
    mods = _modulation(cond, w_mod, b_mod)

    lat_splits = ((3 * D_CONV, 1.0), (2 * D_CONF, 1.0), (D_NA, q_scale), (D_NA, 1.0), (D_NA, 1.0))
    lat_dtypes = (F32, F32, BF16, BF16, BF16)
    kv_splits = ((D_NA, 1.0), (D_NA, 1.0))
    b_in3 = b_in[:, None, :]

    xc = ctx
    for l in range(DEPTH):
        last = l == DEPTH - 1
        m_lat = [mods[l, :bsz, i * d:(i + 1) * d][:, None, :] for i in range(6)]
        m_ctx = [mods[l, bsz:bsz + 1, i * d:(i + 1) * d][:, None, :] for i in range(6)]
        b_out_l = b_out[l][None, :]
        wr = jnp.pad(w_router[l], ((0, 0), (0, LANES - N_EXPERTS)))
        wr_hi = wr.astype(BF16)
        wr_lo = (wr - wr_hi.astype(F32)).astype(BF16)
        g1, b1 = g_post1[l][None, :], b_post1[l][None, :]
        g2, b2 = g_post2[l][None, :], b_post2[l][None, :]
        conv_w = (w_short[l], w_conf_dw[l], b_conf_dw[l], g_conf_ln[l], b_conf_ln[l])

        if last:
            k_c, v_c = _inproj(xc, m_ctx[0], m_ctx[1], w_in, b_in3, l, OFF_K, kv_splits, (BF16, BF16), tm=nctx)
        else:
            uac, ubc, q_c, k_c, v_c = _inproj(xc, m_ctx[0], m_ctx[1], w_in, b_in3, l, 0,
                                              lat_splits, lat_dtypes, tm=nctx)

        ua, ub, q, k, v = _inproj(x, m_lat[0], m_lat[1], w_in, b_in3, l, 0, lat_splits, lat_dtypes, tm=512)
        yab = _conv_mixers(ua, ub, *conv_w)
        yc = _neighbourhood_attention(q, k, v, k_c, v_c, na_rpb[l])
        x_mid, hm, logits = _outproj(yab, yc, x, w_out, l, b_out_l, m_lat[2], g1, b1, m_lat[3], m_lat[4],
                                     wr_hi, wr_lo, tm=512)

        idx, gates = _slot_lists(*_route(logits, cap, 0), cap, merge=False)
        idx, gates = idx.reshape(-1), gates.reshape(-1)
        y_e = _experts(_dispatch(idx, hm, bsz, seq, d, cap), w_gate, w_up, w_down, l, tm=512)
        x = _combine(idx, gates, y_e, x_mid, m_lat[5], g2, b2, cap)

        if not last:
            yabc = _conv_mixers(uac, ubc, *conv_w)
            ycc = _context_attention(q_c, k_c, v_c)
            xc_mid, hmc, logits_c = _outproj(yabc, ycc, xc, w_out, l, b_out_l, m_ctx[2], g1, b1,
                                             m_ctx[3], m_ctx[4], wr_hi, wr_lo, tm=nctx)
            n_c = bsz * cap_ctx
            idx_c, gates_c = _slot_lists(*_route(logits_c, cap_ctx, cap_ctx), n_c, merge=True)
            idx_c, gates_c = idx_c.reshape(-1), gates_c.reshape(-1)
            y_ec = _experts(_dispatch(idx_c, hmc, 1, bsz * nctx, d, n_c), w_gate, w_up, w_down, l, tm=n_c)
            xc = _combine(idx_c, gates_c, y_ec, xc_mid.reshape(1, bsz * nctx, d),
                          m_ctx[5], g2, b2, n_c).reshape(bsz, nctx, d)
    return x
```

```python
import functools
import math

import numpy as np
import jax
import jax.numpy as jnp
from jax import lax
from jax.experimental import pallas as pl
from jax.experimental.pallas import tpu as pltpu

F32 = jnp.float32
BF16 = jnp.bfloat16

D_MODEL = 1024
DEPTH = 2
GRID_W = 64
D_CONV = D_MODEL // 4
D_CONF = D_MODEL // 4
NA_HEAD_DIM = 64
D_NA = D_MODEL - D_CONV - D_CONF
N_NA_HEADS = D_NA // NA_HEAD_DIM
SHORT_CONV_W = 3
CONF_CONV_W = 31
NA_WIN_ROWS_MAX = 8
NA_WIN_COLS = 16
N_EXPERTS = 16
EC_CAPACITY_FACTOR = 2
D_EXPERT = 1024
LN_EPS = 1e-5
DEEPNORM_ALPHA = (2.0 * DEPTH) ** 0.25
NEG_INF = -1e30
LOG2E = math.log2(math.e)

OFF_A = 0
OFF_B = OFF_A + 3 * D_CONV
OFF_Q = OFF_B + 2 * D_CONF
OFF_K = OFF_Q + D_NA
OFF_V = OFF_K + D_NA
D_IN = OFF_V + D_NA

LANES = 128
SUBLANES = 8
MOD_ROWS = 16
VMEM_LIMIT = 56 * 1024 * 1024
ATTN_ROWS = 8
N_PAIRS = D_NA // LANES
HI = lax.Precision.HIGHEST


def _cparams(sem):
    return pltpu.CompilerParams(dimension_semantics=sem, vmem_limit_bytes=VMEM_LIMIT)


def _ln(x):
    mu = jnp.mean(x, axis=-1, keepdims=True)
    xc = x - mu
    var = jnp.mean(xc * xc, axis=-1, keepdims=True)
    return xc * lax.rsqrt(var + LN_EPS)


def _sigmoid(x):
    return 1.0 / (1.0 + jnp.exp(-x))


def _mod_kernel(cond_ref, w_ref, b_ref, o_ref):
    s = cond_ref[...]
    s = s * _sigmoid(s)
    o_ref[0] = jnp.dot(s, w_ref[0], preferred_element_type=F32, precision=HI) + b_ref[0]


def _modulation(cond, w_mod, b_mod):
    n_l, d, n = w_mod.shape
    tn = 1536
    return pl.pallas_call(
        _mod_kernel,
        out_shape=jax.ShapeDtypeStruct((n_l, MOD_ROWS, n), F32),
        grid=(n_l, n // tn),
        in_specs=[
            pl.BlockSpec((MOD_ROWS, d), lambda l, j: (0, 0)),
            pl.BlockSpec((1, d, tn), lambda l, j: (l, 0, j)),
            pl.BlockSpec((1, 1, tn), lambda l, j: (l, 0, j)),
        ],
        out_specs=pl.BlockSpec((1, MOD_ROWS, tn), lambda l, j: (l, 0, j)),
        compiler_params=_cparams(("arbitrary", "arbitrary")),
        name="modulation",
    )(cond, w_mod, b_mod.reshape(n_l, 1, n))


def _first_step():
    return (pl.program_id(0) == 0) & (pl.program_id(1) == 0)


def _inproj_kernel(x_ref, sh_ref, sc_ref, w_ref, b_ref, *rest, splits, col0):
    *o_refs, w_bf = rest

    @pl.when(_first_step())
    def _():
        w_bf[...] = w_ref[0].astype(BF16)

    h = _ln(x_ref[0]) * (1.0 + sc_ref[0]) + sh_ref[0]
    u = jnp.dot(h.astype(BF16), w_bf[:, col0:], preferred_element_type=F32) + b_ref[0, :, col0:]
    off = 0
    for o_ref, (width, scale) in zip(o_refs, splits):
        part = u[:, off:off + width]
        if scale != 1.0:
            part = part * scale
        o_ref[0] = part.astype(o_ref.dtype)
        off += width


def _inproj(x, sh, sc, w, b, layer, col0, splits, dtypes, tm):
    nb, s, d = x.shape
    n = w.shape[2]
    per_sample = sh.shape[0] > 1
    mod_map = (lambda bi, i: (bi, 0, 0)) if per_sample else (lambda bi, i: (0, 0, 0))
    return pl.pallas_call(
        functools.partial(_inproj_kernel, splits=splits, col0=col0),
        out_shape=[jax.ShapeDtypeStruct((nb, s, wd), dt) for (wd, _), dt in zip(splits, dtypes)],
        grid=(nb, s // tm),
        in_specs=[
            pl.BlockSpec((1, tm, d), lambda bi, i: (bi, i, 0)),
            pl.BlockSpec((1, 1, d), mod_map),
            pl.BlockSpec((1, 1, d), mod_map),
            pl.BlockSpec((1, d, n), lambda bi, i: (layer, 0, 0), pipeline_mode=pl.Buffered(1)),
            pl.BlockSpec((1, 1, n), lambda bi, i: (layer, 0, 0)),
        ],
        out_specs=[pl.BlockSpec((1, tm, wd), lambda bi, i: (bi, i, 0)) for wd, _ in splits],
        scratch_shapes=[pltpu.VMEM((d, n), BF16)],
        compiler_params=_cparams(("arbitrary", "arbitrary")),
        name="inproj",
    )(x, sh, sc, w, b)


CONV_CHUNK = 128
Z_PAD = 8
H_PAD = 16


def _conv_kernel(ua_ref, ub_ref, ws_ref, wd_ref, bd_ref, g_ref, b_ref, o_ref, z_scr, h_scr, *, seq):
    c = D_CONV
    z_scr[0:Z_PAD, :] = jnp.zeros((Z_PAD, c), F32)
    z_scr[Z_PAD + seq:2 * Z_PAD + seq, :] = jnp.zeros((Z_PAD, c), F32)
    h_scr[0, 0:H_PAD, :] = jnp.zeros((H_PAD, c), F32)
    h_scr[0, H_PAD + seq:2 * H_PAD + seq, :] = jnp.zeros((H_PAD, c), F32)
    z_scr[Z_PAD:Z_PAD + seq, :] = ua_ref[0, :, c:2 * c] * ua_ref[0, :, 2 * c:3 * c]
    h_scr[0, H_PAD:H_PAD + seq, :] = ub_ref[0, :, 0:c] * _sigmoid(ub_ref[0, :, c:2 * c])
    n_shift = seq + 2 * H_PAD - SUBLANES
    for r in range(1, SUBLANES):
        h_scr[r, 0:n_shift, :] = h_scr[0, r:r + n_shift, :]
    tc = min(CONV_CHUNK, seq)
    for ci in range(seq // tc):
        t0 = ci * tc
        acc = ws_ref[0:1, :] * z_scr[t0 + Z_PAD - 1:t0 + Z_PAD - 1 + tc, :]
        for j in range(1, SHORT_CONV_W):
            s0 = t0 + Z_PAD - 1 + j
            acc = acc + ws_ref[j:j + 1, :] * z_scr[s0:s0 + tc, :]
        ya = ua_ref[0, t0:t0 + tc, 0:c] * acc
        hb = bd_ref[...]
        for j in range(CONF_CONV_W):
            s0 = t0 + H_PAD - CONF_CONV_W // 2 + j
            a0 = s0 - s0 % SUBLANES
            hb = hb + wd_ref[j:j + 1, :] * h_scr[s0 % SUBLANES, a0:a0 + tc, :]
        hn = _ln(hb) * g_ref[...] + b_ref[...]
        yb = hn * _sigmoid(hn)
        o_ref[0, t0:t0 + tc, 0:c] = ya.astype(o_ref.dtype)
        o_ref[0, t0:t0 + tc, c:2 * c] = yb.astype(o_ref.dtype)


def _conv_mixers(ua, ub, w_short, w_dw, b_dw, g_ln, b_ln):
    nb, s, _ = ua.shape
    c = D_CONV
    full = lambda shape: pl.BlockSpec(shape, lambda bi: (0,) * len(shape))
    return pl.pallas_call(
        functools.partial(_conv_kernel, seq=s),
        out_shape=jax.ShapeDtypeStruct((nb, s, 2 * c), BF16),
        grid=(nb,),
        in_specs=[
            pl.BlockSpec((1, s, 3 * c), lambda bi: (bi, 0, 0)),
            pl.BlockSpec((1, s, 2 * c), lambda bi: (bi, 0, 0)),
            full((SHORT_CONV_W, c)), full((CONF_CONV_W, c)), full((1, c)), full((1, c)), full((1, c)),
        ],
        out_specs=pl.BlockSpec((1, s, 2 * c), lambda bi: (bi, 0, 0)),
        scratch_shapes=[pltpu.VMEM((s + 2 * Z_PAD, c), F32), pltpu.VMEM((SUBLANES, s + 2 * H_PAD, c), F32)],
        compiler_params=_cparams(("arbitrary",)),
        name="conv_mixers",
    )(ua, ub, w_short, w_dw, b_dw.reshape(1, c), g_ln.reshape(1, c), b_ln.reshape(1, c))


SUB_ROWS = 2
WIN_ROWS = SUB_ROWS + NA_WIN_ROWS_MAX - 1
BAND_KEYS = WIN_ROWS * GRID_W
SUB_Q = SUB_ROWS * GRID_W
N_DROW = 2 * NA_WIN_ROWS_MAX - 1
N_DCOL = 2 * NA_WIN_COLS - 1


def _bias_kernel(rpb_ref, o_ref):
    n_rows, n_cols = o_ref.shape
    col = lax.broadcasted_iota(jnp.int32, (LANES, n_cols), 1)
    qc = col >> 6
    kc = col & (GRID_W - 1)
    d_col = jnp.clip(kc - qc + (NA_WIN_COLS - 1), 0, N_DCOL - 1)
    onehot = (lax.broadcasted_iota(jnp.int32, (LANES, n_cols), 0) == d_col).astype(F32)
    vals = jnp.dot(rpb_ref[...], onehot, preferred_element_type=F32, precision=HI)
    col_r = lax.broadcasted_iota(jnp.int32, (n_rows, n_cols), 1)
    qc_r = col_r >> 6
    kc_r = col_r & (GRID_W - 1)
    c0 = jnp.clip(qc_r - NA_WIN_COLS // 2, 0, GRID_W - NA_WIN_COLS)
    inside = (kc_r >= c0) & (kc_r < c0 + NA_WIN_COLS)
    o_ref[...] = jnp.where(inside, vals * LOG2E, NEG_INF)


N_SLABS = -(-WIN_ROWS // 2)
TILES_PER_KIND = N_NA_HEADS * N_DROW
MASKED_TILE = 3 * TILES_PER_KIND


def _na_plan(rows):
    wr = min(NA_WIN_ROWS_MAX, rows)
    assert wr == NA_WIN_ROWS_MAX and rows % SUB_ROWS == 0 and rows >= WIN_ROWS
    row_start = np.clip(np.arange(rows) - wr // 2, 0, rows - wr)
    w0s, tiles = [], []
    for r0 in range(0, rows, SUB_ROWS):
        w0 = int(np.clip(r0 - wr // 2, 0, rows - WIN_ROWS))
        for iq in range(SUB_ROWS):
            r = r0 + iq
            assert row_start[r] >= w0 and row_start[r] + wr <= w0 + WIN_ROWS
            ok = lambda w: w < WIN_ROWS and row_start[r] <= w0 + w < row_start[r] + wr
            d_row = lambda w: w0 + w - r + NA_WIN_ROWS_MAX - 1
            for j in range(N_SLABS):
                lo, hi = ok(2 * j), ok(2 * j + 1)
                if lo and hi:
                    tiles.append(d_row(2 * j))
                elif lo:
                    tiles.append(TILES_PER_KIND + d_row(2 * j))
                elif hi:
                    tiles.append(2 * TILES_PER_KIND + d_row(2 * j + 1))
                else:
                    tiles.append(-1)
        w0s.append(w0)
    return np.array(w0s, np.int32), np.array(tiles, np.int32)


def _na_bias(rpb):
    assert GRID_W == 64 and 2 * GRID_W == LANES and TILES_PER_KIND <= LANES and N_DCOL <= LANES
    rpb2 = jnp.pad(rpb.astype(F32).reshape(TILES_PER_KIND, N_DCOL),
                   ((0, LANES - TILES_PER_KIND), (0, LANES - N_DCOL)))
    table = pl.pallas_call(
        _bias_kernel,
        out_shape=jax.ShapeDtypeStruct((LANES, GRID_W * GRID_W), F32),
        name="na_bias",
    )(rpb2)
    table = table[:TILES_PER_KIND].reshape(N_NA_HEADS, N_DROW, GRID_W, GRID_W)
    masked = jnp.full_like(table, NEG_INF)
    nxt = jnp.concatenate([table[:, 1:], masked[:, :1]], axis=1)
    kinds = [jnp.concatenate(pair, axis=-1).reshape(TILES_PER_KIND, GRID_W, LANES)
             for pair in ((table, nxt), (table, masked), (masked, table))]
    return jnp.concatenate(kinds + [jnp.full((1, GRID_W, LANES), NEG_INF, F32)], axis=0)


def _lane_reduce(xs, combine, reduce, neutral):
    chunks = []
    for x in xs:
        rows, n = x.shape
        n_full = n // LANES
        chunks += [x[:, j * LANES:(j + 1) * LANES] for j in range(n_full)]
        if n % LANES:
            fill = jnp.full((rows, LANES - n % LANES), neutral, x.dtype)
            chunks.append(jnp.concatenate([x[:, n_full * LANES:], fill], axis=1))
    return reduce(functools.reduce(combine, chunks), axis=-1, keepdims=True)


def _attn_kernel(w0_ref, tile_ref, q_ref, k_ref, v_ref, kc_ref, vc_ref, *rest, banded, n_sub, sub_q):
    if banded:
        bias_ref, o_ref = rest
    else:
        (o_ref,) = rest
    lane = lax.broadcasted_iota(jnp.int32, (sub_q, LANES), 1)
    first = lane < NA_HEAD_DIM
    nt = (((1,), (1,)), ((), ()))
    stages = [(si, p) for si in range(n_sub) for p in range(N_PAIRS)]

    def window(si):
        blk = pl.program_id(1) * n_sub + si
        return blk, pl.multiple_of(w0_ref[blk] * GRID_W, GRID_W)

    def bias(blk, head):
        row_blocks = []
        for iq in range(SUB_ROWS):
            slabs = []
            for j in range(N_SLABS):
                t = tile_ref[(blk * SUB_ROWS + iq) * N_SLABS + j]
                tile = bias_ref[jnp.where(t < 0, MASKED_TILE, t + head * N_DROW)]
                width = min(LANES, BAND_KEYS - j * LANES)
                slabs.append(tile[:, :width])
            row_blocks.append(jnp.concatenate(slabs, axis=1))
        return jnp.concatenate(row_blocks, axis=0)

    def scores(si, p):
        cols = slice(p * LANES, (p + 1) * LANES)
        q_p = q_ref[0, si * sub_q:(si + 1) * sub_q, cols]
        zero = jnp.zeros_like(q_p)
        qq = jnp.concatenate([jnp.where(first, q_p, zero), jnp.where(first, zero, q_p)], axis=0)
        parts = [lax.dot_general(qq, kc_ref[0, :, cols], nt, preferred_element_type=F32)]
        if banded:
            blk, start = window(si)
            both = jnp.concatenate([bias(blk, 2 * p), bias(blk, 2 * p + 1)], axis=0)
            parts.append(lax.dot_general(qq, k_ref[0, pl.ds(start, BAND_KEYS), cols], nt,
                                         preferred_element_type=F32) + both)
        return parts

    def finish(si, p, parts):
        cols = slice(p * LANES, (p + 1) * LANES)
        m = _lane_reduce(parts, jnp.maximum, jnp.max, NEG_INF)
        es = [jnp.exp2(s - m) for s in parts]
        den = _lane_reduce(es, jnp.add, jnp.sum, 0.0)
        o = jnp.dot(es[0].astype(BF16), vc_ref[0, :, cols], preferred_element_type=F32)
        if banded:
            _, start = window(si)
            o = o + jnp.dot(es[1].astype(BF16), v_ref[0, pl.ds(start, BAND_KEYS), cols],
                            preferred_element_type=F32)
        o = o * (1.0 / den)
        out = jnp.where(first, o[:sub_q], o[sub_q:])
        o_ref[0, si * sub_q:(si + 1) * sub_q, cols] = out.astype(o_ref.dtype)

    nxt = scores(*stages[0])
    for i, (si, p) in enumerate(stages):
        cur = nxt
        if i + 1 < len(stages):
            nxt = scores(*stages[i + 1])
        finish(si, p, cur)


def _neighbourhood_attention(q, k, v, kc, vc, rpb):
    nb, s, dn = q.shape
    rows = s // GRID_W
    nctx = kc.shape[1]
    w0s, tiles = _na_plan(rows)
    bias = _na_bias(rpb)
    n_sub = ATTN_ROWS // SUB_ROWS
    m_rows = ATTN_ROWS * GRID_W
    grid_spec = pltpu.PrefetchScalarGridSpec(
        num_scalar_prefetch=2,
        grid=(nb, rows // ATTN_ROWS),
        in_specs=[
            pl.BlockSpec((1, m_rows, dn), lambda bi, i, w0, pat: (bi, i, 0)),
            pl.BlockSpec((1, s, dn), lambda bi, i, w0, pat: (bi, 0, 0)),
            pl.BlockSpec((1, s, dn), lambda bi, i, w0, pat: (bi, 0, 0)),
            pl.BlockSpec((1, nctx, dn), lambda bi, i, w0, pat: (bi, 0, 0)),
            pl.BlockSpec((1, nctx, dn), lambda bi, i, w0, pat: (bi, 0, 0)),
            pl.BlockSpec(bias.shape, lambda bi, i, w0, pat: (0, 0, 0), pipeline_mode=pl.Buffered(1)),
        ],
        out_specs=pl.BlockSpec((1, m_rows, dn), lambda bi, i, w0, pat: (bi, i, 0)),
    )
    return pl.pallas_call(
        functools.partial(_attn_kernel, banded=True, n_sub=n_sub, sub_q=SUB_Q),
        out_shape=jax.ShapeDtypeStruct((nb, s, dn), BF16),
        grid_spec=grid_spec,
        compiler_params=_cparams(("arbitrary", "arbitrary")),
        name="neighbourhood_attention",
    )(jnp.asarray(w0s), jnp.asarray(tiles), q, k, v, kc, vc, bias)


def _context_attention(q, kc, vc):
    nb, s, dn = q.shape
    spec = pl.BlockSpec((1, s, dn), lambda bi, i, w0, pat: (bi, 0, 0))
    grid_spec = pltpu.PrefetchScalarGridSpec(
        num_scalar_prefetch=2, grid=(nb, 1), in_specs=[spec] * 5, out_specs=spec)
    dummy = jnp.zeros((1,), jnp.int32)
    return pl.pallas_call(
        functools.partial(_attn_kernel, banded=False, n_sub=1, sub_q=s),
        out_shape=jax.ShapeDtypeStruct((nb, s, dn), BF16),
        grid_spec=grid_spec,
        compiler_params=_cparams(("arbitrary", "arbitrary")),
        name="context_attention",
    )(dummy, dummy, q, kc, vc, kc, vc)


def _store_rowmajor(ref, val, base=0):
    n, width = val.shape
    chunks = width // LANES
    for c in range(chunks):
        ref[pl.ds(base + c, n, stride=chunks), :] = val[:, c * LANES:(c + 1) * LANES]


def _load_rowmajor(ref, base, n, chunks):
    return jnp.concatenate([ref[pl.ds(base + c, n, stride=chunks), :] for c in range(chunks)], axis=1)


def _outproj_kernel(yab_ref, yc_ref, x_ref, w_ref, bo_ref, gt_ref, g_ref, b_ref, sh_ref, sc_ref,
                    wrh_ref, wrl_ref, xmid_ref, hm_ref, lg_ref, w_bf):
    @pl.when(_first_step())
    def _():
        w_bf[...] = w_ref[0].astype(BF16)

    half = yab_ref.shape[2]
    tm = x_ref.shape[1]
    chunks = x_ref.shape[2] // LANES
    n_part = 2 if tm % (2 * SUBLANES * 2) == 0 else 1
    rows_per = tm // n_part
    for part in range(n_part):
        rows = slice(part * rows_per, (part + 1) * rows_per)
        y = (jnp.dot(yab_ref[0, rows, :], w_bf[0:half, :], preferred_element_type=F32)
             + jnp.dot(yc_ref[0, rows, :], w_bf[half:, :], preferred_element_type=F32) + bo_ref[...])
        xm = _ln(DEEPNORM_ALPHA * x_ref[0, rows, :] + (1.0 + gt_ref[0]) * y) * g_ref[...] + b_ref[...]
        xmid_ref[0, rows, :] = xm
        hm = _ln(xm) * (1.0 + sc_ref[0]) + sh_ref[0]
        _store_rowmajor(hm_ref, hm, base=part * rows_per * chunks)
        hm_hi = hm.astype(BF16)
        hm_lo = (hm - hm_hi.astype(F32)).astype(BF16)
        lg_ref[0, rows, :] = (jnp.dot(hm_hi, wrh_ref[...], preferred_element_type=F32)
                              + jnp.dot(hm_lo, wrh_ref[...], preferred_element_type=F32)
                              + jnp.dot(hm_hi, wrl_ref[...], preferred_element_type=F32))


def _outproj(yab, yc, x, w, layer, bo, gt, g, b, sh, sc, wr_hi, wr_lo, tm):
    nb, s, d = x.shape
    half = yab.shape[2]
    per_sample = gt.shape[0] > 1
    mod_map = (lambda bi, i: (bi, 0, 0)) if per_sample else (lambda bi, i: (0, 0, 0))
    vec = pl.BlockSpec((1, d), lambda bi, i: (0, 0))
    mod = pl.BlockSpec((1, 1, d), mod_map)
    tok = lambda width: pl.BlockSpec((1, tm, width), lambda bi, i: (bi, i, 0))
    n_i = s // tm
    return pl.pallas_call(
        _outproj_kernel,
        out_shape=[jax.ShapeDtypeStruct((nb, s, d), F32),
                   jax.ShapeDtypeStruct((nb * s * (d // LANES), LANES), F32),
                   jax.ShapeDtypeStruct((nb, s, LANES), F32)],
        grid=(nb, n_i),
        in_specs=[tok(half), tok(half), tok(d),
                  pl.BlockSpec((1, d, d), lambda bi, i: (layer, 0, 0), pipeline_mode=pl.Buffered(1)),
                  vec, mod, vec, vec,
                  mod, mod, pl.BlockSpec((d, LANES), lambda bi, i: (0, 0)),
                  pl.BlockSpec((d, LANES), lambda bi, i: (0, 0))],
        out_specs=[tok(d), pl.BlockSpec((tm * (d // LANES), LANES), lambda bi, i: (bi * n_i + i, 0)),
                   tok(LANES)],
        scratch_shapes=[pltpu.VMEM((d, d), BF16)],
        compiler_params=_cparams(("arbitrary", "arbitrary")),
        name="outproj_postnorm",
    )(yab, yc, x, w, bo, gt, g, b, sh, sc, wr_hi, wr_lo)


CUM_CHUNK = 256
F32_EXP_BIAS = 127
F32_MANT_BITS = 23


def _prefix_count(mask_f32, tri):
    rows, n = mask_f32.shape
    tc = min(CUM_CHUNK, n)
    base = jnp.zeros((rows, 1), F32)
    parts = []
    for ci in range(n // tc):
        blk = mask_f32[:, ci * tc:(ci + 1) * tc]
        parts.append(jnp.dot(blk.astype(BF16), tri[:tc, :tc], preferred_element_type=F32) + base)
        base = base + jnp.sum(blk, axis=-1, keepdims=True)
    return jnp.concatenate(parts, axis=-1)


def _pow2(k):
    return pltpu.bitcast((k + F32_EXP_BIAS) << F32_MANT_BITS, F32)


def _route_kernel(lg_ref, slot_c_ref, gate_t_ref, *, cap, slot_stride):
    nb = lg_ref.shape[0]
    assert nb * N_EXPERTS == LANES
    rows = []
    for b in range(nb):
        lg = lg_ref[b]
        lane = lax.broadcasted_iota(jnp.int32, lg.shape, 1)
        lgm = jnp.where(lane < N_EXPERTS, lg, NEG_INF)
        ex = jnp.exp(lgm - jnp.max(lgm, axis=-1, keepdims=True))
        aff = ex / jnp.sum(ex, axis=-1, keepdims=True)
        rows.append(aff.T[0:N_EXPERTS, :])
    a = jnp.concatenate(rows, axis=0)
    capf = float(cap)

    def enough(t):
        return jnp.sum((a >= t).astype(F32), axis=-1, keepdims=True) >= capf

    def exp_step(_, carry):
        lo, hi = carry
        mid = lo + ((hi - lo + 1) >> 1)
        ok = enough(_pow2(mid))
        return jnp.where(ok, mid, lo), jnp.where(ok, hi, mid - 1)

    k_lo = jnp.full((LANES, 1), -F32_EXP_BIAS, jnp.int32)
    k_hi = jnp.zeros((LANES, 1), jnp.int32)
    k_lo, _ = lax.fori_loop(0, 7, exp_step, (k_lo, k_hi))
    base = _pow2(k_lo)

    def mant_step(_, carry):
        t, step = carry
        step = step * 0.5
        cand = t + step
        return jnp.where(enough(cand), cand, t), step

    thr, _ = lax.fori_loop(0, F32_MANT_BITS, mant_step, (base, base))

    r_i = lax.broadcasted_iota(jnp.int32, (CUM_CHUNK, CUM_CHUNK), 0)
    c_i = lax.broadcasted_iota(jnp.int32, (CUM_CHUNK, CUM_CHUNK), 1)
    tri = (r_i < c_i).astype(BF16)
    gt = (a > thr).astype(F32)
    eq = (a == thr).astype(F32)
    need = capf - jnp.sum(gt, axis=-1, keepdims=True)
    sel = gt + eq * (_prefix_count(eq, tri) < need).astype(F32)
    pos = _prefix_count(sel, tri)
    sample = lax.broadcasted_iota(jnp.int32, (LANES, 1), 0) >> (N_EXPERTS.bit_length() - 1)
    slot = jnp.where(sel > 0.0, pos + (sample * slot_stride).astype(F32), -1.0)
    for b in range(nb):
        lo = b * N_EXPERTS
        gate_t_ref[b] = a[lo:lo + N_EXPERTS, :]
        rolled = slot if b == 0 else jnp.concatenate([slot[lo:, :], slot[:lo, :]], axis=0)
        slot_c_ref[b] = rolled.T


def _route(logits, cap, slot_stride):
    nb, s, _ = logits.shape
    whole = lambda shape: pl.BlockSpec(shape, lambda i: (0,) * len(shape))
    return pl.pallas_call(
        functools.partial(_route_kernel, cap=cap, slot_stride=slot_stride),
        out_shape=[jax.ShapeDtypeStruct((nb, s, LANES), F32), jax.ShapeDtypeStruct((nb, N_EXPERTS, s), F32)],
        grid=(1,),
        in_specs=[whole((nb, s, LANES))],
        out_specs=[whole((nb, s, LANES)), whole((nb, N_EXPERTS, s))],
        compiler_params=_cparams(("arbitrary",)),
        name="route",
    )(logits)


TOK_SPLIT = 64


def _slot_list_kernel(slot_ref, gate_ref, idx_ref, g_ref, *, n_slots, tok_stride, merge):
    s = slot_ref.shape[1]
    b = pl.program_id(0)
    slot_id = lax.broadcasted_iota(jnp.int32, (s, n_slots), 1).astype(F32)
    tok = lax.broadcasted_iota(jnp.int32, (1, s), 1) + b * tok_stride
    tok_hi = (tok >> (TOK_SPLIT.bit_length() - 1)).astype(F32)
    tok_lo = (tok & (TOK_SPLIT - 1)).astype(F32)
    zeros = jnp.zeros((SUBLANES - 5, s), F32)
    idx_rows, g_rows = [], []
    for e in range(N_EXPERTS):
        hit = jnp.where(slot_ref[0, :, e:e + 1] == slot_id, 1.0, 0.0).astype(BF16)
        g0 = gate_ref[0, e:e + 1, :]
        g_hi = g0.astype(BF16).astype(F32)
        g_mid = (g0 - g_hi).astype(BF16).astype(F32)
        g_lo = g0 - g_hi - g_mid
        lhs = jnp.concatenate([tok_hi, tok_lo, g_hi, g_mid, g_lo, zeros], axis=0).astype(BF16)
        out = jnp.dot(lhs, hit, preferred_element_type=F32)
        idx_rows.append(out[0:1] * float(TOK_SPLIT) + out[1:2])
        g_rows.append(out[2:3] + out[3:4] + out[4:5])
    idx = jnp.concatenate(idx_rows, axis=0).astype(jnp.int32) * SUBLANES
    g = jnp.concatenate(g_rows, axis=0)
    if merge:
        @pl.when(b == 0)
        def _():
            idx_ref[0] = idx
            g_ref[0] = g

        @pl.when(b > 0)
        def _():
            idx_ref[0] = idx_ref[0] + idx
            g_ref[0] = g_ref[0] + g
    else:
        idx_ref[0] = idx
        g_ref[0] = g


def _slot_lists(slot_c, gate_t, n_slots, merge):
    nb, s, _ = slot_c.shape
    nbo = 1 if merge else nb
    out_map = (lambda bi: (0, 0, 0)) if merge else (lambda bi: (bi, 0, 0))
    return pl.pallas_call(
        functools.partial(_slot_list_kernel, n_slots=n_slots, tok_stride=s if merge else 0, merge=merge),
        out_shape=[jax.ShapeDtypeStruct((nbo, N_EXPERTS, n_slots), jnp.int32),
                   jax.ShapeDtypeStruct((nbo, N_EXPERTS, n_slots), F32)],
        grid=(nb,),
        in_specs=[pl.BlockSpec((1, s, LANES), lambda bi: (bi, 0, 0)),
                  pl.BlockSpec((1, N_EXPERTS, s), lambda bi: (bi, 0, 0))],
        out_specs=[pl.BlockSpec((1, N_EXPERTS, n_slots), out_map), pl.BlockSpec((1, N_EXPERTS, n_slots), out_map)],
        compiler_params=_cparams(("arbitrary",)),
        name="slot_lists",
    )(slot_c, gate_t)


ROW_UNROLL = 16


def _dispatch_kernel(idx_ref, hm_ref, x_ref, rows_scr, *, n_slots):
    b = pl.program_id(0)
    chunks = x_ref.shape[2] // LANES
    for e in range(N_EXPERTS):
        base = (b * N_EXPERTS + e) * n_slots

        def move(i, carry):
            for u in range(ROW_UNROLL):
                s = i * ROW_UNROLL + u
                src = pl.multiple_of(idx_ref[base + s], chunks)
                dst = pl.multiple_of(s * chunks, chunks)
                rows_scr[pl.ds(dst, chunks), :] = hm_ref[pl.ds(src, chunks), :]
            return carry

        lax.fori_loop(0, n_slots // ROW_UNROLL, move, 0)
        x_ref[e] = _load_rowmajor(rows_scr, 0, n_slots, chunks).astype(x_ref.dtype)


def _dispatch(idx, hm_rm, nb, s, d, n_slots):
    chunks = d // LANES
    assert chunks == SUBLANES and n_slots % ROW_UNROLL == 0
    grid_spec = pltpu.PrefetchScalarGridSpec(
        num_scalar_prefetch=1,
        grid=(nb,),
        in_specs=[pl.BlockSpec((s * chunks, LANES), lambda bi, idx_r: (bi, 0))],
        out_specs=pl.BlockSpec((N_EXPERTS, n_slots, d), lambda bi, idx_r: (0, bi, 0)),
        scratch_shapes=[pltpu.VMEM((n_slots * chunks, LANES), F32)],
    )
    return pl.pallas_call(
        functools.partial(_dispatch_kernel, n_slots=n_slots),
        out_shape=jax.ShapeDtypeStruct((N_EXPERTS, nb * n_slots, d), BF16),
        grid_spec=grid_spec,
        compiler_params=_cparams(("arbitrary",)),
        name="dispatch",
    )(idx, hm_rm)


def _expert_kernel(*refs, n_main, has_extra):
    if has_extra:
        x_ref, xx_ref, wg_ref, wu_ref, wd_ref, y_ref, yy_ref, wg_bf, wu_bf, wd_bf = refs
    else:
        x_ref, wg_ref, wu_ref, wd_ref, y_ref, wg_bf, wu_bf, wd_bf = refs
    i = pl.program_id(1)

    @pl.when(i == 0)
    def _():
        wg_bf[...] = wg_ref[0, 0].astype(BF16)
        wu_bf[...] = wu_ref[0, 0].astype(BF16)
        wd_bf[...] = wd_ref[0, 0].astype(BF16)

    def ffn(x_e, out_ref):
        a = jnp.dot(x_e, wg_bf[...], preferred_element_type=F32)
        u = jnp.dot(x_e, wu_bf[...], preferred_element_type=F32)
        h = (a * _sigmoid(a) * u).astype(BF16)
        _store_rowmajor(out_ref, jnp.dot(h, wd_bf[...], preferred_element_type=F32))

    if has_extra:
        @pl.when(i < n_main)
        def _():
            ffn(x_ref[0], y_ref.at[0])

        @pl.when(i == n_main)
        def _():
            ffn(xx_ref[0], yy_ref.at[0])
    else:
        ffn(x_ref[0], y_ref.at[0])


def _experts(x_e, x_extra, wg, wu, wd, layer, tm):
    _, rows, d = x_e.shape
    f = wg.shape[3]
    chunks = d // LANES
    n_main = rows // tm
    has_extra = x_extra is not None
    main_map = lambda e, i: (e, jnp.minimum(i, n_main - 1), 0)
    w_spec = lambda shape: pl.BlockSpec((1, 1) + shape, lambda e, i: (layer, e, 0, 0))
    in_specs = [pl.BlockSpec((1, tm, d), main_map)]
    out_specs = [pl.BlockSpec((1, tm * chunks, LANES), main_map)]
    out_shape = [jax.ShapeDtypeStruct((N_EXPERTS, rows * chunks, LANES), F32)]
    args = [x_e]
    if has_extra:
        rows2 = x_extra.shape[1]
        in_specs.append(pl.BlockSpec((1, rows2, d), lambda e, i: (e, 0, 0)))
        out_specs.append(pl.BlockSpec((1, rows2 * chunks, LANES), lambda e, i: (e, 0, 0)))
        out_shape.append(jax.ShapeDtypeStruct((N_EXPERTS, rows2 * chunks, LANES), F32))
        args.append(x_extra)
    outs = pl.pallas_call(
        functools.partial(_expert_kernel, n_main=n_main, has_extra=has_extra),
        out_shape=out_shape,
        grid=(N_EXPERTS, n_main + int(has_extra)),
        in_specs=in_specs + [w_spec((d, f)), w_spec((d, f)), w_spec((f, d))],
        out_specs=out_specs,
        scratch_shapes=[pltpu.VMEM((d, f), BF16), pltpu.VMEM((d, f), BF16), pltpu.VMEM((f, d), BF16)],
        compiler_params=_cparams(("arbitrary", "arbitrary")),
        name="experts",
    )(*args, wg, wu, wd)
    return outs if has_extra else (outs[0], None)


COMBINE_EXPERTS = 4
NORM_CHUNK = 256


def _combine_kernel(idx_ref, gate_ref, y_ref, x_ref, gt_ref, g_ref, b_ref, o_ref, acc, *, n_slots):
    b = pl.program_id(0)
    j = pl.program_id(1)
    s, d = x_ref.shape[1], x_ref.shape[2]
    chunks = d // LANES

    @pl.when(j == 0)
    def _():
        acc[...] = jnp.zeros(acc.shape, F32)

    for eg in range(COMBINE_EXPERTS):
        base = (b * N_EXPERTS + j * COMBINE_EXPERTS + eg) * n_slots

        def add(i, carry):
            new = []
            for u in range(ROW_UNROLL):
                slot = i * ROW_UNROLL + u
                src = pl.multiple_of(slot * chunks, chunks)
                dst = pl.multiple_of(idx_ref[base + slot], chunks)
                new.append((dst, acc[pl.ds(dst, chunks), :]
                            + gate_ref[base + slot] * y_ref[eg, pl.ds(src, chunks), :]))
            for dst, val in new:
                acc[pl.ds(dst, chunks), :] = val
            return carry

        lax.fori_loop(0, n_slots // ROW_UNROLL, add, 0)

    @pl.when(j == pl.num_programs(1) - 1)
    def _():
        for ci in range(s // NORM_CHUNK):
            rows = slice(ci * NORM_CHUNK, (ci + 1) * NORM_CHUNK)
            ym = _load_rowmajor(acc, ci * NORM_CHUNK * chunks, NORM_CHUNK, chunks)
            z = DEEPNORM_ALPHA * x_ref[0, rows, :] + (1.0 + gt_ref[0]) * ym
            o_ref[0, rows, :] = _ln(z) * g_ref[...] + b_ref[...]


def _combine(idx, gates, y_rm, x_mid, gt, g, b, n_slots):
    nb, s, d = x_mid.shape
    chunks = d // LANES
    assert chunks == SUBLANES and N_EXPERTS % COMBINE_EXPERTS == 0 and s % NORM_CHUNK == 0
    per_sample = gt.shape[0] > 1
    mod_map = (lambda bi, j, i_r, g_r: (bi, 0, 0)) if per_sample else (lambda bi, j, i_r, g_r: (0, 0, 0))
    vec = pl.BlockSpec((1, d), lambda bi, j, i_r, g_r: (0, 0))
    grid_spec = pltpu.PrefetchScalarGridSpec(
        num_scalar_prefetch=2,
        grid=(nb, N_EXPERTS // COMBINE_EXPERTS),
        in_specs=[
            pl.BlockSpec((COMBINE_EXPERTS, n_slots * chunks, LANES), lambda bi, j, i_r, g_r: (j, bi, 0)),
            pl.BlockSpec((1, s, d), lambda bi, j, i_r, g_r: (bi, 0, 0)),
            pl.BlockSpec((1, 1, d), mod_map), vec, vec,
        ],
        out_specs=pl.BlockSpec((1, s, d), lambda bi, j, i_r, g_r: (bi, 0, 0)),
        scratch_shapes=[pltpu.VMEM((s * chunks, LANES), F32)],
    )
    return pl.pallas_call(
        functools.partial(_combine_kernel, n_slots=n_slots),
        out_shape=jax.ShapeDtypeStruct((nb, s, d), F32),
        grid_spec=grid_spec,
        compiler_params=_cparams(("arbitrary", "arbitrary")),
        name="combine_postnorm",
    )(idx, gates, y_rm, x_mid, gt, g, b)


def kernel(x, c, ctx, c_ctx, w_mod, b_mod, w_in, b_in, w_short, w_conf_dw, b_conf_dw, g_conf_ln, b_conf_ln,
           na_rpb, w_out, b_out, g_post1, b_post1, w_router, w_gate, w_up, w_down, g_post2, b_post2):
    bsz, seq, d = x.shape
    nctx = ctx.shape[1]
    cap = EC_CAPACITY_FACTOR * seq // N_EXPERTS
    cap_ctx = EC_CAPACITY_FACTOR * nctx // N_EXPERTS
    q_scale = NA_HEAD_DIM ** -0.5 * LOG2E

    cond = jnp.concatenate([c, c_ctx[None, :], jnp.zeros((MOD_ROWS - bsz - 1, d), F32)], axis=0)
    mods = _modulation(cond, w_mod, b_mod)

    lat_splits = ((3 * D_CONV, 1.0), (2 * D_CONF, 1.0), (D_NA, q_scale), (D_NA, 1.0), (D_NA, 1.0))
    lat_dtypes = (F32, F32, BF16, BF16, BF16)
    kv_splits = ((D_NA, 1.0), (D_NA, 1.0))
    b_in3 = b_in[:, None, :]

    xc = ctx
    for l in range(DEPTH):
        last = l == DEPTH - 1
        m_lat = [mods[l, :bsz, i * d:(i + 1) * d][:, None, :] for i in range(6)]
        m_ctx = [mods[l, bsz:bsz + 1, i * d:(i + 1) * d][:, None, :] for i in range(6)]
        b_out_l = b_out[l][None, :]
        wr = jnp.pad(w_router[l], ((0, 0), (0, LANES - N_EXPERTS)))
        wr_hi = wr.astype(BF16)
        wr_lo = (wr - wr_hi.astype(F32)).astype(BF16)
        g1, b1 = g_post1[l][None, :], b_post1[l][None, :]
        g2, b2 = g_post2[l][None, :], b_post2[l][None, :]
        conv_w = (w_short[l], w_conf_dw[l], b_conf_dw[l], g_conf_ln[l], b_conf_ln[l])

        if last:
            k_c, v_c = _inproj(xc, m_ctx[0], m_ctx[1], w_in, b_in3, l, OFF_K, kv_splits, (BF16, BF16), tm=nctx)
        else:
            uac, ubc, q_c, k_c, v_c = _inproj(xc, m_ctx[0], m_ctx[1], w_in, b_in3, l, 0,
                                              lat_splits, lat_dtypes, tm=nctx)

        ua, ub, q, k, v = _inproj(x, m_lat[0], m_lat[1], w_in, b_in3, l, 0, lat_splits, lat_dtypes, tm=512)
        yab = _conv_mixers(ua, ub, *conv_w)
        yc = _neighbourhood_attention(q, k, v, k_c, v_c, na_rpb[l])
        x_mid, hm, logits = _outproj(yab, yc, x, w_out, l, b_out_l, m_lat[2], g1, b1, m_lat[3], m_lat[4],
                                     wr_hi, wr_lo, tm=512)

        idx, gates = _slot_lists(*_route(logits, cap, 0), cap, merge=False)
        idx, gates = idx.reshape(-1), gates.reshape(-1)
        x_e = _dispatch(idx, hm, bsz, seq, d, cap)

        x_ec = None
        if not last:
            yabc = _conv_mixers(uac, ubc, *conv_w)
            ycc = _context_attention(q_c, k_c, v_c)
            xc_mid, hmc, logits_c = _outproj(yabc, ycc, xc, w_out, l, b_out_l, m_ctx[2], g1, b1,
                                             m_ctx[3], m_ctx[4], wr_hi, wr_lo, tm=nctx)
            n_c = bsz * cap_ctx
            idx_c, gates_c = _slot_lists(*_route(logits_c, cap_ctx, cap_ctx), n_c, merge=True)
            idx_c, gates_c = idx_c.reshape(-1), gates_c.reshape(-1)
            x_ec = _dispatch(idx_c, hmc, 1, bsz * nctx, d, n_c)

        y_e, y_ec = _experts(x_e, x_ec, w_gate, w_up, w_down, l, tm=512)
        x = _combine(idx, gates, y_e, x_mid, m_lat[5], g2, b2, cap)
        if not last:
            xc = _combine(idx_c, gates_c, y_ec, xc_mid.reshape(1, bsz * nctx, d),
                          m_ctx[5], g2, b2, n_c).reshape(bsz, nctx, d)
    return x
```

```python
import functools
import math

import numpy as np
import jax
import jax.numpy as jnp
from jax import lax
from jax.experimental import pallas as pl
from jax.experimental.pallas import tpu as pltpu

F32 = jnp.float32
BF16 = jnp.bfloat16

D_MODEL = 1024
DEPTH = 2
GRID_W = 64
D_CONV = D_MODEL // 4
D_CONF = D_MODEL // 4
NA_HEAD_DIM = 64
D_NA = D_MODEL - D_CONV - D_CONF
N_NA_HEADS = D_NA // NA_HEAD_DIM
SHORT_CONV_W = 3
CONF_CONV_W = 31
NA_WIN_ROWS_MAX = 8
NA_WIN_COLS = 16
N_EXPERTS = 16
EC_CAPACITY_FACTOR = 2
D_EXPERT = 1024
LN_EPS = 1e-5
DEEPNORM_ALPHA = (2.0 * DEPTH) ** 0.25
NEG_INF = -1e30
LOG2E = math.log2(math.e)

OFF_A = 0
OFF_B = OFF_A + 3 * D_CONV
OFF_Q = OFF_B + 2 * D_CONF
OFF_K = OFF_Q + D_NA
OFF_V = OFF_K + D_NA
D_IN = OFF_V + D_NA

LANES = 128
SUBLANES = 8
MOD_ROWS = 16
VMEM_LIMIT = 56 * 1024 * 1024
ATTN_ROWS = 8
N_PAIRS = D_NA // LANES
ROW_GROUP = 256
HI = lax.Precision.HIGHEST


def _cparams(sem):
    return pltpu.CompilerParams(dimension_semantics=sem, vmem_limit_bytes=VMEM_LIMIT)


def _ln(x):
    mu = jnp.mean(x, axis=-1, keepdims=True)
    xc = x - mu
    var = jnp.mean(xc * xc, axis=-1, keepdims=True)
    return xc * lax.rsqrt(var + LN_EPS)


def _sigmoid(x):
    return 1.0 / (1.0 + jnp.exp(-x))


def _mod_kernel(cond_ref, w_ref, b_ref, o_ref):
    s = cond_ref[...]
    s = s * _sigmoid(s)
    o_ref[0] = jnp.dot(s, w_ref[0], preferred_element_type=F32, precision=HI) + b_ref[0]


N_MOD = 6


def _modulation(cond, w_mod, b_mod):
    n_l, d, n = w_mod.shape
    out = pl.pallas_call(
        _mod_kernel,
        out_shape=jax.ShapeDtypeStruct((n_l * N_MOD, MOD_ROWS, d), F32),
        grid=(n_l, N_MOD),
        in_specs=[
            pl.BlockSpec((MOD_ROWS, d), lambda l, k: (0, 0)),
            pl.BlockSpec((1, d, d), lambda l, k: (l, 0, k)),
            pl.BlockSpec((1, 1, d), lambda l, k: (l, 0, k)),
        ],
        out_specs=pl.BlockSpec((1, MOD_ROWS, d), lambda l, k: (l * N_MOD + k, 0, 0)),
        compiler_params=_cparams(("arbitrary", "arbitrary")),
        name="modulation",
    )(cond, w_mod, b_mod.reshape(n_l, 1, n))
    return out.reshape(n_l * N_MOD * MOD_ROWS, 1, d)


def _mod_spec(mod, d):
    _, row0, per_sample = mod
    if per_sample:
        return pl.BlockSpec((1, 1, d), lambda bi, *_: (row0 + bi, 0, 0))
    return pl.BlockSpec((1, 1, d), lambda bi, *_: (row0, 0, 0))


def _first_step():
    return (pl.program_id(0) == 0) & (pl.program_id(1) == 0)


def _inproj_kernel(x_ref, sh_ref, sc_ref, w_ref, b_ref, *rest, splits, col0):
    *o_refs, w_bf = rest

    @pl.when(_first_step())
    def _():
        w_bf[...] = w_ref[0].astype(BF16)

    tm = x_ref.shape[1]
    n_part = max(1, tm // ROW_GROUP)
    rows_per = tm // n_part
    for part_i in range(n_part):
        rows = slice(part_i * rows_per, (part_i + 1) * rows_per)
        h = _ln(x_ref[0, rows, :]) * (1.0 + sc_ref[0]) + sh_ref[0]
        u = jnp.dot(h.astype(BF16), w_bf[:, col0:], preferred_element_type=F32) + b_ref[0, :, col0:]
        off = 0
        for o_ref, (width, scale) in zip(o_refs, splits):
            part = u[:, off:off + width]
            if scale != 1.0:
                part = part * scale
            o_ref[0, rows, :] = part.astype(o_ref.dtype)
            off += width


def _inproj(x, sh, sc, w, b, layer, col0, splits, dtypes, tm):
    nb, s, d = x.shape
    n = w.shape[2]
    return pl.pallas_call(
        functools.partial(_inproj_kernel, splits=splits, col0=col0),
        out_shape=[jax.ShapeDtypeStruct((nb, s, wd), dt) for (wd, _), dt in zip(splits, dtypes)],
        grid=(nb, s // tm),
        in_specs=[
            pl.BlockSpec((1, tm, d), lambda bi, i: (bi, i, 0)),
            _mod_spec(sh, d),
            _mod_spec(sc, d),
            pl.BlockSpec((1, d, n), lambda bi, i: (layer, 0, 0), pipeline_mode=pl.Buffered(1)),
            pl.BlockSpec((1, 1, n), lambda bi, i: (layer, 0, 0)),
        ],
        out_specs=[pl.BlockSpec((1, tm, wd), lambda bi, i: (bi, i, 0)) for wd, _ in splits],
        scratch_shapes=[pltpu.VMEM((d, n), BF16)],
        compiler_params=_cparams(("arbitrary", "arbitrary")),
        name="inproj",
    )(x, sh[0], sc[0], w, b)


CONV_CHUNK = 128
Z_PAD = 8
H_PAD = 16


def _conv_kernel(ua_ref, ub_ref, ws_ref, wd_ref, bd_ref, g_ref, b_ref, o_ref, z_scr, h_scr, *, seq):
    c = D_CONV
    z_scr[0:Z_PAD, :] = jnp.zeros((Z_PAD, c), F32)
    z_scr[Z_PAD + seq:2 * Z_PAD + seq, :] = jnp.zeros((Z_PAD, c), F32)
    h_scr[0, 0:H_PAD, :] = jnp.zeros((H_PAD, c), F32)
    h_scr[0, H_PAD + seq:2 * H_PAD + seq, :] = jnp.zeros((H_PAD, c), F32)
    z_scr[Z_PAD:Z_PAD + seq, :] = ua_ref[0, :, c:2 * c] * ua_ref[0, :, 2 * c:3 * c]
    h_scr[0, H_PAD:H_PAD + seq, :] = ub_ref[0, :, 0:c] * _sigmoid(ub_ref[0, :, c:2 * c])
    n_rows = seq + 2 * H_PAD
    h_all = h_scr[0]
    for r in range(1, SUBLANES):
        h_scr[r] = pltpu.roll(h_all, n_rows - r, axis=0)
    tc = min(CONV_CHUNK, seq)
    for ci in range(seq // tc):
        t0 = ci * tc
        acc = ws_ref[0:1, :] * z_scr[t0 + Z_PAD - 1:t0 + Z_PAD - 1 + tc, :]
        for j in range(1, SHORT_CONV_W):
            s0 = t0 + Z_PAD - 1 + j
            acc = acc + ws_ref[j:j + 1, :] * z_scr[s0:s0 + tc, :]
        ya = ua_ref[0, t0:t0 + tc, 0:c] * acc
        hb = bd_ref[...]
        for j in range(CONF_CONV_W):
            s0 = t0 + H_PAD - CONF_CONV_W // 2 + j
            a0 = s0 - s0 % SUBLANES
            hb = hb + wd_ref[j:j + 1, :] * h_scr[s0 % SUBLANES, a0:a0 + tc, :]
        hn = _ln(hb) * g_ref[...] + b_ref[...]
        yb = hn * _sigmoid(hn)
        o_ref[0, t0:t0 + tc, 0:c] = ya.astype(o_ref.dtype)
        o_ref[0, t0:t0 + tc, c:2 * c] = yb.astype(o_ref.dtype)


def _conv_mixers(ua, ub, w_short, w_dw, b_dw, g_ln, b_ln):
    nb, s, _ = ua.shape
    c = D_CONV
    full = lambda shape: pl.BlockSpec(shape, lambda bi: (0,) * len(shape))
    return pl.pallas_call(
        functools.partial(_conv_kernel, seq=s),
        out_shape=jax.ShapeDtypeStruct((nb, s, 2 * c), BF16),
        grid=(nb,),
        in_specs=[
            pl.BlockSpec((1, s, 3 * c), lambda bi: (bi, 0, 0)),
            pl.BlockSpec((1, s, 2 * c), lambda bi: (bi, 0, 0)),
            full((SHORT_CONV_W, c)), full((CONF_CONV_W, c)), full((1, c)), full((1, c)), full((1, c)),
        ],
        out_specs=pl.BlockSpec((1, s, 2 * c), lambda bi: (bi, 0, 0)),
        scratch_shapes=[pltpu.VMEM((s + 2 * Z_PAD, c), F32), pltpu.VMEM((SUBLANES, s + 2 * H_PAD, c), F32)],
        compiler_params=_cparams(("arbitrary",)),
        name="conv_mixers",
    )(ua, ub, w_short, w_dw, b_dw.reshape(1, c), g_ln.reshape(1, c), b_ln.reshape(1, c))


SUB_ROWS = 2
WIN_ROWS = SUB_ROWS + NA_WIN_ROWS_MAX - 1
BAND_KEYS = WIN_ROWS * GRID_W
SUB_Q = SUB_ROWS * GRID_W
N_DROW = 2 * NA_WIN_ROWS_MAX - 1
N_DCOL = 2 * NA_WIN_COLS - 1


def _bias_kernel(rpb_ref, o_ref):
    n_rows, n_cols = o_ref.shape
    col = lax.broadcasted_iota(jnp.int32, (LANES, n_cols), 1)
    qc = col >> 6
    kc = col & (GRID_W - 1)
    d_col = jnp.clip(kc - qc + (NA_WIN_COLS - 1), 0, N_DCOL - 1)
    onehot = (lax.broadcasted_iota(jnp.int32, (LANES, n_cols), 0) == d_col).astype(F32)
    vals = jnp.dot(rpb_ref[...], onehot, preferred_element_type=F32, precision=HI)
    col_r = lax.broadcasted_iota(jnp.int32, (n_rows, n_cols), 1)
    qc_r = col_r >> 6
    kc_r = col_r & (GRID_W - 1)
    c0 = jnp.clip(qc_r - NA_WIN_COLS // 2, 0, GRID_W - NA_WIN_COLS)
    inside = (kc_r >= c0) & (kc_r < c0 + NA_WIN_COLS)
    o_ref[...] = jnp.where(inside, vals * LOG2E, NEG_INF)


N_SLABS = -(-WIN_ROWS // 2)
TILES_PER_KIND = N_NA_HEADS * N_DROW
MASKED_TILE = 3 * TILES_PER_KIND


def _na_plan(rows):
    wr = min(NA_WIN_ROWS_MAX, rows)
    assert wr == NA_WIN_ROWS_MAX and rows % SUB_ROWS == 0 and rows >= WIN_ROWS
    row_start = np.clip(np.arange(rows) - wr // 2, 0, rows - wr)
    w0s, tiles = [], []
    for r0 in range(0, rows, SUB_ROWS):
        w0 = int(np.clip(r0 - wr // 2, 0, rows - WIN_ROWS))
        for iq in range(SUB_ROWS):
            r = r0 + iq
            assert row_start[r] >= w0 and row_start[r] + wr <= w0 + WIN_ROWS
            ok = lambda w: w < WIN_ROWS and row_start[r] <= w0 + w < row_start[r] + wr
            d_row = lambda w: w0 + w - r + NA_WIN_ROWS_MAX - 1
            for j in range(N_SLABS):
                lo, hi = ok(2 * j), ok(2 * j + 1)
                if lo and hi:
                    tiles.append(d_row(2 * j))
                elif lo:
                    tiles.append(TILES_PER_KIND + d_row(2 * j))
                elif hi:
                    tiles.append(2 * TILES_PER_KIND + d_row(2 * j + 1))
                else:
                    tiles.append(-1)
        w0s.append(w0)
    return np.array(w0s, np.int32), np.array(tiles, np.int32)


def _na_bias(rpb):
    assert GRID_W == 64 and 2 * GRID_W == LANES and TILES_PER_KIND <= LANES and N_DCOL <= LANES
    rpb2 = jnp.pad(rpb.astype(F32).reshape(TILES_PER_KIND, N_DCOL),
                   ((0, LANES - TILES_PER_KIND), (0, LANES - N_DCOL)))
    table = pl.pallas_call(
        _bias_kernel,
        out_shape=jax.ShapeDtypeStruct((LANES, GRID_W * GRID_W), F32),
        name="na_bias",
    )(rpb2)
    table = table[:TILES_PER_KIND].reshape(N_NA_HEADS, N_DROW, GRID_W, GRID_W)
    masked = jnp.full_like(table, NEG_INF)
    nxt = jnp.concatenate([table[:, 1:], masked[:, :1]], axis=1)
    kinds = [jnp.concatenate(pair, axis=-1).reshape(TILES_PER_KIND, GRID_W, LANES)
             for pair in ((table, nxt), (table, masked), (masked, table))]
    return jnp.concatenate(kinds + [jnp.full((1, GRID_W, LANES), NEG_INF, F32)], axis=0)


def _lane_reduce(xs, combine, reduce, neutral):
    chunks = []
    for x in xs:
        rows, n = x.shape
        n_full = n // LANES
        chunks += [x[:, j * LANES:(j + 1) * LANES] for j in range(n_full)]
        if n % LANES:
            fill = jnp.full((rows, LANES - n % LANES), neutral, x.dtype)
            chunks.append(jnp.concatenate([x[:, n_full * LANES:], fill], axis=1))
    return reduce(functools.reduce(combine, chunks), axis=-1, keepdims=True)


def _attn_kernel(w0_ref, tile_ref, q_ref, k_ref, v_ref, kc_ref, vc_ref, *rest, banded, n_sub, sub_q):
    if banded:
        bias_ref, o_ref = rest
    else:
        (o_ref,) = rest
    lane = lax.broadcasted_iota(jnp.int32, (sub_q, LANES), 1)
    first = lane < NA_HEAD_DIM
    nt = (((1,), (1,)), ((), ()))
    stages = [(si, p) for si in range(n_sub) for p in range(N_PAIRS)]

    def window(si):
        blk = pl.program_id(1) * n_sub + si
        return blk, pl.multiple_of(w0_ref[blk] * GRID_W, GRID_W)

    def bias(blk, head):
        row_blocks = []
        for iq in range(SUB_ROWS):
            slabs = []
            for j in range(N_SLABS):
                t = tile_ref[(blk * SUB_ROWS + iq) * N_SLABS + j]
                tile = bias_ref[jnp.where(t < 0, MASKED_TILE, t + head * N_DROW)]
                width = min(LANES, BAND_KEYS - j * LANES)
                slabs.append(tile[:, :width])
            row_blocks.append(jnp.concatenate(slabs, axis=1))
        return jnp.concatenate(row_blocks, axis=0)

    def scores(si, p):
        cols = slice(p * LANES, (p + 1) * LANES)
        q_p = q_ref[0, si * sub_q:(si + 1) * sub_q, cols]
        zero = jnp.zeros_like(q_p)
        qq = jnp.concatenate([jnp.where(first, q_p, zero), jnp.where(first, zero, q_p)], axis=0)
        parts = [lax.dot_general(qq, kc_ref[0, :, cols], nt, preferred_element_type=F32)]
        if banded:
            blk, start = window(si)
            both = jnp.concatenate([bias(blk, 2 * p), bias(blk, 2 * p + 1)], axis=0)
            parts.append(lax.dot_general(qq, k_ref[0, pl.ds(start, BAND_KEYS), cols], nt,
                                         preferred_element_type=F32) + both)
        return parts

    def finish(si, p, parts):
        cols = slice(p * LANES, (p + 1) * LANES)
        m = _lane_reduce(parts, jnp.maximum, jnp.max, NEG_INF)
        es = [jnp.exp2(s - m).astype(BF16) for s in parts]

        def with_ones(v):
            return jnp.concatenate([v, jnp.ones_like(v)], axis=1)

        o = jnp.dot(es[0], with_ones(vc_ref[0, :, cols]), preferred_element_type=F32)
        if banded:
            _, start = window(si)
            o = o + jnp.dot(es[1], with_ones(v_ref[0, pl.ds(start, BAND_KEYS), cols]),
                            preferred_element_type=F32)
        o = o[:, :LANES] * (1.0 / o[:, LANES:LANES + 1])
        out = jnp.where(first, o[:sub_q], o[sub_q:])
        o_ref[0, si * sub_q:(si + 1) * sub_q, cols] = out.astype(o_ref.dtype)

    nxt = scores(*stages[0])
    for i, (si, p) in enumerate(stages):
        cur = nxt
        if i + 1 < len(stages):
            nxt = scores(*stages[i + 1])
        finish(si, p, cur)


def _neighbourhood_attention(q, k, v, kc, vc, rpb):
    nb, s, dn = q.shape
    rows = s // GRID_W
    nctx = kc.shape[1]
    w0s, tiles = _na_plan(rows)
    bias = _na_bias(rpb)
    n_sub = ATTN_ROWS // SUB_ROWS
    m_rows = ATTN_ROWS * GRID_W
    grid_spec = pltpu.PrefetchScalarGridSpec(
        num_scalar_prefetch=2,
        grid=(nb, rows // ATTN_ROWS),
        in_specs=[
            pl.BlockSpec((1, m_rows, dn), lambda bi, i, w0, pat: (bi, i, 0)),
            pl.BlockSpec((1, s, dn), lambda bi, i, w0, pat: (bi, 0, 0)),
            pl.BlockSpec((1, s, dn), lambda bi, i, w0, pat: (bi, 0, 0)),
            pl.BlockSpec((1, nctx, dn), lambda bi, i, w0, pat: (bi, 0, 0)),
            pl.BlockSpec((1, nctx, dn), lambda bi, i, w0, pat: (bi, 0, 0)),
            pl.BlockSpec(bias.shape, lambda bi, i, w0, pat: (0, 0, 0), pipeline_mode=pl.Buffered(1)),
        ],
        out_specs=pl.BlockSpec((1, m_rows, dn), lambda bi, i, w0, pat: (bi, i, 0)),
    )
    return pl.pallas_call(
        functools.partial(_attn_kernel, banded=True, n_sub=n_sub, sub_q=SUB_Q),
        out_shape=jax.ShapeDtypeStruct((nb, s, dn), BF16),
        grid_spec=grid_spec,
        compiler_params=_cparams(("arbitrary", "arbitrary")),
        name="neighbourhood_attention",
    )(jnp.asarray(w0s), jnp.asarray(tiles), q, k, v, kc, vc, bias)


def _context_attention(q, kc, vc):
    nb, s, dn = q.shape
    spec = pl.BlockSpec((1, s, dn), lambda bi, i, w0, pat: (bi, 0, 0))
    grid_spec = pltpu.PrefetchScalarGridSpec(
        num_scalar_prefetch=2, grid=(nb, 1), in_specs=[spec] * 5, out_specs=spec)
    dummy = jnp.zeros((1,), jnp.int32)
    return pl.pallas_call(
        functools.partial(_attn_kernel, banded=False, n_sub=1, sub_q=s),
        out_shape=jax.ShapeDtypeStruct((nb, s, dn), BF16),
        grid_spec=grid_spec,
        compiler_params=_cparams(("arbitrary", "arbitrary")),
        name="context_attention",
    )(dummy, dummy, q, kc, vc, kc, vc)


def _store_rowmajor(ref, val, base=0):
    n, width = val.shape
    chunks = width // LANES
    for c in range(chunks):
        ref[pl.ds(base + c, n, stride=chunks), :] = val[:, c * LANES:(c + 1) * LANES]


def _load_rowmajor(ref, base, n, chunks):
    return jnp.concatenate([ref[pl.ds(base + c, n, stride=chunks), :] for c in range(chunks)], axis=1)


def _outproj_kernel(yab_ref, yc_ref, x_ref, w_ref, bo_ref, gt_ref, g_ref, b_ref, sh_ref, sc_ref,
                    wrh_ref, wrl_ref, xmid_ref, hm_ref, lg_ref, w_bf):
    @pl.when(_first_step())
    def _():
        w_bf[...] = w_ref[0].astype(BF16)

    half = yab_ref.shape[2]
    tm = x_ref.shape[1]
    chunks = x_ref.shape[2] // LANES
    n_part = max(1, tm // ROW_GROUP)
    rows_per = tm // n_part
    for part in range(n_part):
        rows = slice(part * rows_per, (part + 1) * rows_per)
        y = (jnp.dot(yab_ref[0, rows, :], w_bf[0:half, :], preferred_element_type=F32)
             + jnp.dot(yc_ref[0, rows, :], w_bf[half:, :], preferred_element_type=F32) + bo_ref[...])
        xm = _ln(DEEPNORM_ALPHA * x_ref[0, rows, :] + (1.0 + gt_ref[0]) * y) * g_ref[...] + b_ref[...]
        xmid_ref[0, rows, :] = xm
        hm = _ln(xm) * (1.0 + sc_ref[0]) + sh_ref[0]
        _store_rowmajor(hm_ref, hm, base=part * rows_per * chunks)
        hm_hi = hm.astype(BF16)
        hm_lo = (hm - hm_hi.astype(F32)).astype(BF16)
        lg_ref[0, rows, :] = (jnp.dot(hm_hi, wrh_ref[...], preferred_element_type=F32)
                              + jnp.dot(hm_lo, wrh_ref[...], preferred_element_type=F32)
                              + jnp.dot(hm_hi, wrl_ref[...], preferred_element_type=F32))


def _outproj(yab, yc, x, w, layer, bo, gt, g, b, sh, sc, wr_hi, wr_lo, tm):
    nb, s, d = x.shape
    half = yab.shape[2]
    vec = pl.BlockSpec((1, d), lambda bi, i: (0, 0))
    tok = lambda width: pl.BlockSpec((1, tm, width), lambda bi, i: (bi, i, 0))
    n_i = s // tm
    return pl.pallas_call(
        _outproj_kernel,
        out_shape=[jax.ShapeDtypeStruct((nb, s, d), F32),
                   jax.ShapeDtypeStruct((nb * s * (d // LANES), LANES), F32),
                   jax.ShapeDtypeStruct((nb, s, LANES), F32)],
        grid=(nb, n_i),
        in_specs=[tok(half), tok(half), tok(d),
                  pl.BlockSpec((1, d, d), lambda bi, i: (layer, 0, 0), pipeline_mode=pl.Buffered(1)),
                  vec, _mod_spec(gt, d), vec, vec,
                  _mod_spec(sh, d), _mod_spec(sc, d), pl.BlockSpec((d, LANES), lambda bi, i: (0, 0)),
                  pl.BlockSpec((d, LANES), lambda bi, i: (0, 0))],
        out_specs=[tok(d), pl.BlockSpec((tm * (d // LANES), LANES), lambda bi, i: (bi * n_i + i, 0)),
                   tok(LANES)],
        scratch_shapes=[pltpu.VMEM((d, d), BF16)],
        compiler_params=_cparams(("arbitrary", "arbitrary")),
        name="outproj_postnorm",
    )(yab, yc, x, w, bo, gt[0], g, b, sh[0], sc[0], wr_hi, wr_lo)


CUM_CHUNK = 256
F32_EXP_BIAS = 127
F32_MANT_BITS = 23


def _prefix_count(mask_f32, tri):
    rows, n = mask_f32.shape
    tc = min(CUM_CHUNK, n)
    base = jnp.zeros((rows, 1), F32)
    parts = []
    for ci in range(n // tc):
        blk = mask_f32[:, ci * tc:(ci + 1) * tc]
        parts.append(jnp.dot(blk.astype(BF16), tri[:tc, :tc], preferred_element_type=F32) + base)
        base = base + jnp.sum(blk, axis=-1, keepdims=True)
    return jnp.concatenate(parts, axis=-1)


def _pow2(k):
    return pltpu.bitcast((k + F32_EXP_BIAS) << F32_MANT_BITS, F32)


def _route_kernel(lg_ref, slot_c_ref, gate_t_ref, *, cap, slot_stride):
    nb = lg_ref.shape[0]
    assert nb * N_EXPERTS == LANES
    rows = []
    for b in range(nb):
        lg = lg_ref[b]
        lane = lax.broadcasted_iota(jnp.int32, lg.shape, 1)
        lgm = jnp.where(lane < N_EXPERTS, lg, NEG_INF)
        ex = jnp.exp(lgm - jnp.max(lgm, axis=-1, keepdims=True))
        aff = ex / jnp.sum(ex, axis=-1, keepdims=True)
        rows.append(aff.T[0:N_EXPERTS, :])
    a = jnp.concatenate(rows, axis=0)
    capf = float(cap)

    def enough(t):
        return jnp.sum((a >= t).astype(F32), axis=-1, keepdims=True) >= capf

    def exp_step(_, carry):
        lo, hi = carry
        mid = lo + ((hi - lo + 1) >> 1)
        ok = enough(_pow2(mid))
        return jnp.where(ok, mid, lo), jnp.where(ok, hi, mid - 1)

    k_lo = jnp.full((LANES, 1), -F32_EXP_BIAS, jnp.int32)
    k_hi = jnp.zeros((LANES, 1), jnp.int32)
    k_lo, _ = lax.fori_loop(0, 7, exp_step, (k_lo, k_hi))
    base = _pow2(k_lo)

    def mant_step(_, carry):
        t, step = carry
        step = step * 0.5
        cand = t + step
        return jnp.where(enough(cand), cand, t), step

    thr, _ = lax.fori_loop(0, F32_MANT_BITS, mant_step, (base, base))

    r_i = lax.broadcasted_iota(jnp.int32, (CUM_CHUNK, CUM_CHUNK), 0)
    c_i = lax.broadcasted_iota(jnp.int32, (CUM_CHUNK, CUM_CHUNK), 1)
    tri = (r_i < c_i).astype(BF16)
    gt = (a > thr).astype(F32)
    eq = (a == thr).astype(F32)
    need = capf - jnp.sum(gt, axis=-1, keepdims=True)
    sel = gt + eq * (_prefix_count(eq, tri) < need).astype(F32)
    pos = _prefix_count(sel, tri)
    sample = lax.broadcasted_iota(jnp.int32, (LANES, 1), 0) >> (N_EXPERTS.bit_length() - 1)
    slot = jnp.where(sel > 0.0, pos + (sample * slot_stride).astype(F32), -1.0)
    for b in range(nb):
        lo = b * N_EXPERTS
        gate_t_ref[b] = a[lo:lo + N_EXPERTS, :]
        rolled = slot if b == 0 else jnp.concatenate([slot[lo:, :], slot[:lo, :]], axis=0)
        slot_c_ref[b] = rolled.T


def _route(logits, cap, slot_stride):
    nb, s, _ = logits.shape
    whole = lambda shape: pl.BlockSpec(shape, lambda i: (0,) * len(shape))
    return pl.pallas_call(
        functools.partial(_route_kernel, cap=cap, slot_stride=slot_stride),
        out_shape=[jax.ShapeDtypeStruct((nb, s, LANES), F32), jax.ShapeDtypeStruct((nb, N_EXPERTS, s), F32)],
        grid=(1,),
        in_specs=[whole((nb, s, LANES))],
        out_specs=[whole((nb, s, LANES)), whole((nb, N_EXPERTS, s))],
        compiler_params=_cparams(("arbitrary",)),
        name="route",
    )(logits)


TOK_SPLIT = 64


def _slot_list_kernel(slot_ref, gate_ref, idx_ref, g_ref, *, n_slots, tok_stride, merge):
    s = slot_ref.shape[1]
    b = pl.program_id(0)
    slot_id = lax.broadcasted_iota(jnp.int32, (s, n_slots), 1).astype(F32)
    tok = lax.broadcasted_iota(jnp.int32, (1, s), 1) + b * tok_stride
    tok_hi = (tok >> (TOK_SPLIT.bit_length() - 1)).astype(F32)
    tok_lo = (tok & (TOK_SPLIT - 1)).astype(F32)
    zeros = jnp.zeros((SUBLANES - 5, s), F32)
    idx_rows, g_rows = [], []
    for e in range(N_EXPERTS):
        hit = jnp.where(slot_ref[0, :, e:e + 1] == slot_id, 1.0, 0.0).astype(BF16)
        g0 = gate_ref[0, e:e + 1, :]
        g_hi = g0.astype(BF16).astype(F32)
        g_mid = (g0 - g_hi).astype(BF16).astype(F32)
        g_lo = g0 - g_hi - g_mid
        lhs = jnp.concatenate([tok_hi, tok_lo, g_hi, g_mid, g_lo, zeros], axis=0).astype(BF16)
        out = jnp.dot(lhs, hit, preferred_element_type=F32)
        idx_rows.append(out[0:1] * float(TOK_SPLIT) + out[1:2])
        g_rows.append(out[2:3] + out[3:4] + out[4:5])
    idx = jnp.concatenate(idx_rows, axis=0).astype(jnp.int32) * SUBLANES
    g = jnp.concatenate(g_rows, axis=0)
    if merge:
        @pl.when(b == 0)
        def _():
            idx_ref[0] = idx
            g_ref[0] = g

        @pl.when(b > 0)
        def _():
            idx_ref[0] = idx_ref[0] + idx
            g_ref[0] = g_ref[0] + g
    else:
        idx_ref[0] = idx
        g_ref[0] = g


def _slot_lists(slot_c, gate_t, n_slots, merge):
    nb, s, _ = slot_c.shape
    nbo = 1 if merge else nb
    out_map = (lambda bi: (0, 0, 0)) if merge else (lambda bi: (bi, 0, 0))
    return pl.pallas_call(
        functools.partial(_slot_list_kernel, n_slots=n_slots, tok_stride=s if merge else 0, merge=merge),
        out_shape=[jax.ShapeDtypeStruct((nbo, N_EXPERTS, n_slots), jnp.int32),
                   jax.ShapeDtypeStruct((nbo, N_EXPERTS, n_slots), F32)],
        grid=(nb,),
        in_specs=[pl.BlockSpec((1, s, LANES), lambda bi: (bi, 0, 0)),
                  pl.BlockSpec((1, N_EXPERTS, s), lambda bi: (bi, 0, 0))],
        out_specs=[pl.BlockSpec((1, N_EXPERTS, n_slots), out_map), pl.BlockSpec((1, N_EXPERTS, n_slots), out_map)],
        compiler_params=_cparams(("arbitrary",)),
        name="slot_lists",
    )(slot_c, gate_t)


ROW_UNROLL = 16


def _dispatch_kernel(idx_ref, hm_ref, x_ref, rows_scr, *, n_slots):
    b = pl.program_id(0)
    chunks = x_ref.shape[2] // LANES
    for e in range(N_EXPERTS):
        base = (b * N_EXPERTS + e) * n_slots

        def move(i, carry):
            for u in range(ROW_UNROLL):
                s = i * ROW_UNROLL + u
                src = pl.multiple_of(idx_ref[base + s], chunks)
                dst = pl.multiple_of(s * chunks, chunks)
                rows_scr[pl.ds(dst, chunks), :] = hm_ref[pl.ds(src, chunks), :]
            return carry

        lax.fori_loop(0, n_slots // ROW_UNROLL, move, 0)
        x_ref[e] = _load_rowmajor(rows_scr, 0, n_slots, chunks).astype(x_ref.dtype)


def _dispatch(idx, hm_rm, nb, s, d, n_slots):
    chunks = d // LANES
    assert chunks == SUBLANES and n_slots % ROW_UNROLL == 0
    grid_spec = pltpu.PrefetchScalarGridSpec(
        num_scalar_prefetch=1,
        grid=(nb,),
        in_specs=[pl.BlockSpec((s * chunks, LANES), lambda bi, idx_r: (bi, 0))],
        out_specs=pl.BlockSpec((N_EXPERTS, n_slots, d), lambda bi, idx_r: (0, bi, 0)),
        scratch_shapes=[pltpu.VMEM((n_slots * chunks, LANES), F32)],
    )
    return pl.pallas_call(
        functools.partial(_dispatch_kernel, n_slots=n_slots),
        out_shape=jax.ShapeDtypeStruct((N_EXPERTS, nb * n_slots, d), BF16),
        grid_spec=grid_spec,
        compiler_params=_cparams(("arbitrary",)),
        name="dispatch",
    )(idx, hm_rm)


def _expert_kernel(*refs, n_main, has_extra):
    if has_extra:
        x_ref, xx_ref, wg_ref, wu_ref, wd_ref, y_ref, yy_ref, wg_bf, wu_bf, wd_bf = refs
    else:
        x_ref, wg_ref, wu_ref, wd_ref, y_ref, wg_bf, wu_bf, wd_bf = refs
    i = pl.program_id(1)

    @pl.when(i == 0)
    def _():
        wg_bf[...] = wg_ref[0, 0].astype(BF16)
        wu_bf[...] = wu_ref[0, 0].astype(BF16)
        wd_bf[...] = wd_ref[0, 0].astype(BF16)

    def ffn(x_e, out_ref):
        a = jnp.dot(x_e, wg_bf[...], preferred_element_type=F32)
        u = jnp.dot(x_e, wu_bf[...], preferred_element_type=F32)
        h = (a * _sigmoid(a) * u).astype(BF16)
        _store_rowmajor(out_ref, jnp.dot(h, wd_bf[...], preferred_element_type=F32))

    if has_extra:
        @pl.when(i < n_main)
        def _():
            ffn(x_ref[0], y_ref.at[0])

        @pl.when(i == n_main)
        def _():
            ffn(xx_ref[0], yy_ref.at[0])
    else:
        ffn(x_ref[0], y_ref.at[0])


def _experts(x_e, x_extra, wg, wu, wd, layer, tm):
    _, rows, d = x_e.shape
    f = wg.shape[3]
    chunks = d // LANES
    n_main = rows // tm
    has_extra = x_extra is not None
    main_map = lambda e, i: (e, jnp.minimum(i, n_main - 1), 0)
    w_spec = lambda shape: pl.BlockSpec((1, 1) + shape, lambda e, i: (layer, e, 0, 0))
    in_specs = [pl.BlockSpec((1, tm, d), main_map)]
    out_specs = [pl.BlockSpec((1, tm * chunks, LANES), main_map)]
    out_shape = [jax.ShapeDtypeStruct((N_EXPERTS, rows * chunks, LANES), F32)]
    args = [x_e]
    if has_extra:
        rows2 = x_extra.shape[1]
        in_specs.append(pl.BlockSpec((1, rows2, d), lambda e, i: (e, 0, 0)))
        out_specs.append(pl.BlockSpec((1, rows2 * chunks, LANES), lambda e, i: (e, 0, 0)))
        out_shape.append(jax.ShapeDtypeStruct((N_EXPERTS, rows2 * chunks, LANES), F32))
        args.append(x_extra)
    outs = pl.pallas_call(
        functools.partial(_expert_kernel, n_main=n_main, has_extra=has_extra),
        out_shape=out_shape,
        grid=(N_EXPERTS, n_main + int(has_extra)),
        in_specs=in_specs + [w_spec((d, f)), w_spec((d, f)), w_spec((f, d))],
        out_specs=out_specs,
        scratch_shapes=[pltpu.VMEM((d, f), BF16), pltpu.VMEM((d, f), BF16), pltpu.VMEM((f, d), BF16)],
        compiler_params=_cparams(("arbitrary", "arbitrary")),
        name="experts",
    )(*args, wg, wu, wd)
    return outs if has_extra else (outs[0], None)


COMBINE_EXPERTS = 4
NORM_CHUNK = 256


def _combine_kernel(idx_ref, gate_ref, y_ref, x_ref, gt_ref, g_ref, b_ref, o_ref, acc, *, n_slots):
    b = pl.program_id(0)
    j = pl.program_id(1)
    s, d = x_ref.shape[1], x_ref.shape[2]
    chunks = d // LANES

    @pl.when(j == 0)
    def _():
        acc[...] = jnp.zeros(acc.shape, F32)

    for eg in range(COMBINE_EXPERTS):
        base = (b * N_EXPERTS + j * COMBINE_EXPERTS + eg) * n_slots

        def add(i, carry):
            new = []
            for u in range(ROW_UNROLL):
                slot = i * ROW_UNROLL + u
                src = pl.multiple_of(slot * chunks, chunks)
                dst = pl.multiple_of(idx_ref[base + slot], chunks)
                new.append((dst, acc[pl.ds(dst, chunks), :]
                            + gate_ref[base + slot] * y_ref[eg, pl.ds(src, chunks), :]))
            for dst, val in new:
                acc[pl.ds(dst, chunks), :] = val
            return carry

        lax.fori_loop(0, n_slots // ROW_UNROLL, add, 0)

    @pl.when(j == pl.num_programs(1) - 1)
    def _():
        for ci in range(s // NORM_CHUNK):
            rows = slice(ci * NORM_CHUNK, (ci + 1) * NORM_CHUNK)
            ym = _load_rowmajor(acc, ci * NORM_CHUNK * chunks, NORM_CHUNK, chunks)
            z = DEEPNORM_ALPHA * x_ref[0, rows, :] + (1.0 + gt_ref[0]) * ym
            o_ref[0, rows, :] = _ln(z) * g_ref[...] + b_ref[...]


def _combine(idx, gates, y_rm, x_mid, gt, g, b, n_slots):
    nb, s, d = x_mid.shape
    chunks = d // LANES
    assert chunks == SUBLANES and N_EXPERTS % COMBINE_EXPERTS == 0 and s % NORM_CHUNK == 0
    vec = pl.BlockSpec((1, d), lambda bi, j, i_r, g_r: (0, 0))
    grid_spec = pltpu.PrefetchScalarGridSpec(
        num_scalar_prefetch=2,
        grid=(nb, N_EXPERTS // COMBINE_EXPERTS),
        in_specs=[
            pl.BlockSpec((COMBINE_EXPERTS, n_slots * chunks, LANES), lambda bi, j, i_r, g_r: (j, bi, 0)),
            pl.BlockSpec((1, s, d), lambda bi, j, i_r, g_r: (bi, 0, 0)),
            _mod_spec(gt, d), vec, vec,
        ],
        out_specs=pl.BlockSpec((1, s, d), lambda bi, j, i_r, g_r: (bi, 0, 0)),
        scratch_shapes=[pltpu.VMEM((s * chunks, LANES), F32)],
    )
    return pl.pallas_call(
        functools.partial(_combine_kernel, n_slots=n_slots),
        out_shape=jax.ShapeDtypeStruct((nb, s, d), F32),
        grid_spec=grid_spec,
        compiler_params=_cparams(("arbitrary", "arbitrary")),
        name="combine_postnorm",
    )(idx, gates, y_rm, x_mid, gt[0], g, b)


def kernel(x, c, ctx, c_ctx, w_mod, b_mod, w_in, b_in, w_short, w_conf_dw, b_conf_dw, g_conf_ln, b_conf_ln,
           na_rpb, w_out, b_out, g_post1, b_post1, w_router, w_gate, w_up, w_down, g_post2, b_post2):
    bsz, seq, d = x.shape
    nctx = ctx.shape[1]
    cap = EC_CAPACITY_FACTOR * seq // N_EXPERTS
    cap_ctx = EC_CAPACITY_FACTOR * nctx // N_EXPERTS
    q_scale = NA_HEAD_DIM ** -0.5 * LOG2E

    cond = jnp.concatenate([c, c_ctx[None, :], jnp.zeros((MOD_ROWS - bsz - 1, d), F32)], axis=0)
    mods = _modulation(cond, w_mod, b_mod)

    lat_splits = ((3 * D_CONV, 1.0), (2 * D_CONF, 1.0), (D_NA, q_scale), (D_NA, 1.0), (D_NA, 1.0))
    lat_dtypes = (F32, F32, BF16, BF16, BF16)
    kv_splits = ((D_NA, 1.0), (D_NA, 1.0))
    b_in3 = b_in[:, None, :]

    xc = ctx
    for l in range(DEPTH):
        last = l == DEPTH - 1
        m_lat = [(mods, (l * N_MOD + k) * MOD_ROWS, True) for k in range(N_MOD)]
        m_ctx = [(mods, (l * N_MOD + k) * MOD_ROWS + bsz, False) for k in range(N_MOD)]
        b_out_l = b_out[l][None, :]
        wr = jnp.pad(w_router[l], ((0, 0), (0, LANES - N_EXPERTS)))
        wr_hi = wr.astype(BF16)
        wr_lo = (wr - wr_hi.astype(F32)).astype(BF16)
        g1, b1 = g_post1[l][None, :], b_post1[l][None, :]
        g2, b2 = g_post2[l][None, :], b_post2[l][None, :]
        conv_w = (w_short[l], w_conf_dw[l], b_conf_dw[l], g_conf_ln[l], b_conf_ln[l])

        if last:
            k_c, v_c = _inproj(xc, m_ctx[0], m_ctx[1], w_in, b_in3, l, OFF_K, kv_splits, (BF16, BF16), tm=nctx)
        else:
            uac, ubc, q_c, k_c, v_c = _inproj(xc, m_ctx[0], m_ctx[1], w_in, b_in3, l, 0,
                                              lat_splits, lat_dtypes, tm=nctx)

        ua, ub, q, k, v = _inproj(x, m_lat[0], m_lat[1], w_in, b_in3, l, 0, lat_splits, lat_dtypes, tm=512)
        yab = _conv_mixers(ua, ub, *conv_w)
        yc = _neighbourhood_attention(q, k, v, k_c, v_c, na_rpb[l])
        x_mid, hm, logits = _outproj(yab, yc, x, w_out, l, b_out_l, m_lat[2], g1, b1, m_lat[3], m_lat[4],
                                     wr_hi, wr_lo, tm=512)

        idx, gates = _slot_lists(*_route(logits, cap, 0), cap, merge=False)
        idx, gates = idx.reshape(-1), gates.reshape(-1)
        x_e = _dispatch(idx, hm, bsz, seq, d, cap)

        x_ec = None
        if not last:
            yabc = _conv_mixers(uac, ubc, *conv_w)
            ycc = _context_attention(q_c, k_c, v_c)
            xc_mid, hmc, logits_c = _outproj(yabc, ycc, xc, w_out, l, b_out_l, m_ctx[2], g1, b1,
                                             m_ctx[3], m_ctx[4], wr_hi, wr_lo, tm=nctx)
            n_c = bsz * cap_ctx
            idx_c, gates_c = _slot_lists(*_route(logits_c, cap_ctx, cap_ctx), n_c, merge=True)
            idx_c, gates_c = idx_c.reshape(-1), gates_c.reshape(-1)
            x_ec = _dispatch(idx_c, hmc, 1, bsz * nctx, d, n_c)

        y_e, y_ec = _experts(x_e, x_ec, w_gate, w_up, w_down, l, tm=512)
        x = _combine(idx, gates, y_e, x_mid, m_lat[5], g2, b2, cap)
        if not last:
            xc = _combine(idx_c, gates_c, y_ec, xc_mid.reshape(1, bsz * nctx, d),
                          m_ctx[5], g2, b2, n_c).reshape(bsz, nctx, d)
    return x
```

```python
import functools
import math

import numpy as np
import jax
import jax.numpy as jnp
from jax import lax
from jax.experimental import pallas as pl
from jax.experimental.pallas import tpu as pltpu

F32 = jnp.float32
BF16 = jnp.bfloat16

D_MODEL = 1024
DEPTH = 2
GRID_W = 64
D_CONV = D_MODEL // 4
D_CONF = D_MODEL // 4
NA_HEAD_DIM = 64
D_NA = D_MODEL - D_CONV - D_CONF
N_NA_HEADS = D_NA // NA_HEAD_DIM
SHORT_CONV_W = 3
CONF_CONV_W = 31
NA_WIN_ROWS_MAX = 8
NA_WIN_COLS = 16
N_EXPERTS = 16
EC_CAPACITY_FACTOR = 2
D_EXPERT = 1024
LN_EPS = 1e-5
DEEPNORM_ALPHA = (2.0 * DEPTH) ** 0.25
NEG_INF = -1e30
LOG2E = math.log2(math.e)

OFF_A = 0
OFF_B = OFF_A + 3 * D_CONV
OFF_Q = OFF_B + 2 * D_CONF
OFF_K = OFF_Q + D_NA
OFF_V = OFF_K + D_NA
D_IN = OFF_V + D_NA

LANES = 128
SUBLANES = 8
MOD_ROWS = 16
VMEM_LIMIT = 56 * 1024 * 1024
ATTN_ROWS = 8
N_PAIRS = D_NA // LANES
ROW_GROUP = 256
HI = lax.Precision.HIGHEST


def _cparams(sem):
    return pltpu.CompilerParams(dimension_semantics=sem, vmem_limit_bytes=VMEM_LIMIT)


def _ln(x):
    mu = jnp.mean(x, axis=-1, keepdims=True)
    xc = x - mu
    var = jnp.mean(xc * xc, axis=-1, keepdims=True)
    return xc * lax.rsqrt(var + LN_EPS)


def _sigmoid(x):
    return 1.0 / (1.0 + jnp.exp(-x))


def _mod_kernel(cond_ref, w_ref, b_ref, o_ref):
    s = cond_ref[...]
    s = s * _sigmoid(s)
    o_ref[0] = jnp.dot(s, w_ref[0], preferred_element_type=F32, precision=HI) + b_ref[0]


N_MOD = 6


def _modulation(cond, w_mod, b_mod):
    n_l, d, n = w_mod.shape
    out = pl.pallas_call(
        _mod_kernel,
        out_shape=jax.ShapeDtypeStruct((n_l * N_MOD, MOD_ROWS, d), F32),
        grid=(n_l, N_MOD),
        in_specs=[
            pl.BlockSpec((MOD_ROWS, d), lambda l, k: (0, 0)),
            pl.BlockSpec((1, d, d), lambda l, k: (l, 0, k)),
            pl.BlockSpec((1, 1, d), lambda l, k: (l, 0, k)),
        ],
        out_specs=pl.BlockSpec((1, MOD_ROWS, d), lambda l, k: (l * N_MOD + k, 0, 0)),
        compiler_params=_cparams(("arbitrary", "arbitrary")),
        name="modulation",
    )(cond, w_mod, b_mod.reshape(n_l, 1, n))
    return out.reshape(n_l * N_MOD * MOD_ROWS, 1, d)


def _mod_spec(mod, d):
    _, row0, per_sample = mod
    if per_sample:
        return pl.BlockSpec((1, 1, d), lambda bi, *_: (row0 + bi, 0, 0))
    return pl.BlockSpec((1, 1, d), lambda bi, *_: (row0, 0, 0))


def _first_step():
    return (pl.program_id(0) == 0) & (pl.program_id(1) == 0)


def _inproj_kernel(x_ref, sh_ref, sc_ref, w_ref, b_ref, *rest, splits, col0):
    *o_refs, w_bf = rest

    @pl.when(_first_step())
    def _():
        w_bf[...] = w_ref[0].astype(BF16)

    tm = x_ref.shape[1]
    n_part = max(1, tm // ROW_GROUP)
    rows_per = tm // n_part
    for part_i in range(n_part):
        rows = slice(part_i * rows_per, (part_i + 1) * rows_per)
        h = _ln(x_ref[0, rows, :]) * (1.0 + sc_ref[0]) + sh_ref[0]
        u = jnp.dot(h.astype(BF16), w_bf[:, col0:], preferred_element_type=F32) + b_ref[0, :, col0:]
        off = 0
        for o_ref, (width, scale) in zip(o_refs, splits):
            part = u[:, off:off + width]
            if scale != 1.0:
                part = part * scale
            o_ref[0, rows, :] = part.astype(o_ref.dtype)
            off += width


def _inproj(x, sh, sc, w, b, layer, col0, splits, dtypes, tm):
    nb, s, d = x.shape
    n = w.shape[2]
    return pl.pallas_call(
        functools.partial(_inproj_kernel, splits=splits, col0=col0),
        out_shape=[jax.ShapeDtypeStruct((nb, s, wd), dt) for (wd, _), dt in zip(splits, dtypes)],
        grid=(nb, s // tm),
        in_specs=[
            pl.BlockSpec((1, tm, d), lambda bi, i: (bi, i, 0)),
            _mod_spec(sh, d),
            _mod_spec(sc, d),
            pl.BlockSpec((1, d, n), lambda bi, i: (layer, 0, 0), pipeline_mode=pl.Buffered(1)),
            pl.BlockSpec((1, 1, n), lambda bi, i: (layer, 0, 0)),
        ],
        out_specs=[pl.BlockSpec((1, tm, wd), lambda bi, i: (bi, i, 0)) for wd, _ in splits],
        scratch_shapes=[pltpu.VMEM((d, n), BF16)],
        compiler_params=_cparams(("arbitrary", "arbitrary")),
        name="inproj",
    )(x, sh[0], sc[0], w, b)


CONV_CHUNK = 128
Z_PAD = 8
H_PAD = 16


def _conv_kernel(ua_ref, ub_ref, ws_ref, wd_ref, bd_ref, g_ref, b_ref, o_ref, z_scr, h_scr, *, seq):
    c = D_CONV
    z_scr[0:Z_PAD, :] = jnp.zeros((Z_PAD, c), F32)
    z_scr[Z_PAD + seq:2 * Z_PAD + seq, :] = jnp.zeros((Z_PAD, c), F32)
    h_scr[0, 0:H_PAD, :] = jnp.zeros((H_PAD, c), F32)
    h_scr[0, H_PAD + seq:2 * H_PAD + seq, :] = jnp.zeros((H_PAD, c), F32)
    z_scr[Z_PAD:Z_PAD + seq, :] = ua_ref[0, :, c:2 * c] * ua_ref[0, :, 2 * c:3 * c]
    h_scr[0, H_PAD:H_PAD + seq, :] = ub_ref[0, :, 0:c] * _sigmoid(ub_ref[0, :, c:2 * c])
    n_rows = seq + 2 * H_PAD
    h_all = h_scr[0]
    for r in range(1, SUBLANES):
        h_scr[r] = pltpu.roll(h_all, n_rows - r, axis=0)
    tc = min(CONV_CHUNK, seq)
    for ci in range(seq // tc):
        t0 = ci * tc
        acc = ws_ref[0:1, :] * z_scr[t0 + Z_PAD - 1:t0 + Z_PAD - 1 + tc, :]
        for j in range(1, SHORT_CONV_W):
            s0 = t0 + Z_PAD - 1 + j
            acc = acc + ws_ref[j:j + 1, :] * z_scr[s0:s0 + tc, :]
        ya = ua_ref[0, t0:t0 + tc, 0:c] * acc
        hb = bd_ref[...]
        for j in range(CONF_CONV_W):
            s0 = t0 + H_PAD - CONF_CONV_W // 2 + j
            a0 = s0 - s0 % SUBLANES
            hb = hb + wd_ref[j:j + 1, :] * h_scr[s0 % SUBLANES, a0:a0 + tc, :]
        hn = _ln(hb) * g_ref[...] + b_ref[...]
        yb = hn * _sigmoid(hn)
        o_ref[0, t0:t0 + tc, 0:c] = ya.astype(o_ref.dtype)
        o_ref[0, t0:t0 + tc, c:2 * c] = yb.astype(o_ref.dtype)


def _conv_mixers(ua, ub, w_short, w_dw, b_dw, g_ln, b_ln):
    nb, s, _ = ua.shape
    c = D_CONV
    full = lambda shape: pl.BlockSpec(shape, lambda bi: (0,) * len(shape))
    return pl.pallas_call(
        functools.partial(_conv_kernel, seq=s),
        out_shape=jax.ShapeDtypeStruct((nb, s, 2 * c), BF16),
        grid=(nb,),
        in_specs=[
            pl.BlockSpec((1, s, 3 * c), lambda bi: (bi, 0, 0)),
            pl.BlockSpec((1, s, 2 * c), lambda bi: (bi, 0, 0)),
            full((SHORT_CONV_W, c)), full((CONF_CONV_W, c)), full((1, c)), full((1, c)), full((1, c)),
        ],
        out_specs=pl.BlockSpec((1, s, 2 * c), lambda bi: (bi, 0, 0)),
        scratch_shapes=[pltpu.VMEM((s + 2 * Z_PAD, c), F32), pltpu.VMEM((SUBLANES, s + 2 * H_PAD, c), F32)],
        compiler_params=_cparams(("arbitrary",)),
        name="conv_mixers",
    )(ua, ub, w_short, w_dw, b_dw.reshape(1, c), g_ln.reshape(1, c), b_ln.reshape(1, c))


SUB_ROWS = 2
WIN_ROWS = SUB_ROWS + NA_WIN_ROWS_MAX - 1
BAND_KEYS = WIN_ROWS * GRID_W
SUB_Q = SUB_ROWS * GRID_W
N_DROW = 2 * NA_WIN_ROWS_MAX - 1
N_DCOL = 2 * NA_WIN_COLS - 1


def _bias_kernel(rpb_ref, o_ref):
    n_rows, n_cols = o_ref.shape
    col = lax.broadcasted_iota(jnp.int32, (LANES, n_cols), 1)
    qc = col >> 6
    kc = col & (GRID_W - 1)
    d_col = jnp.clip(kc - qc + (NA_WIN_COLS - 1), 0, N_DCOL - 1)
    onehot = (lax.broadcasted_iota(jnp.int32, (LANES, n_cols), 0) == d_col).astype(F32)
    vals = jnp.dot(rpb_ref[...], onehot, preferred_element_type=F32, precision=HI)
    col_r = lax.broadcasted_iota(jnp.int32, (n_rows, n_cols), 1)
    qc_r = col_r >> 6
    kc_r = col_r & (GRID_W - 1)
    c0 = jnp.clip(qc_r - NA_WIN_COLS // 2, 0, GRID_W - NA_WIN_COLS)
    inside = (kc_r >= c0) & (kc_r < c0 + NA_WIN_COLS)
    o_ref[...] = jnp.where(inside, vals * LOG2E, NEG_INF)


N_SLABS = -(-WIN_ROWS // 2)
TILES_PER_KIND = N_NA_HEADS * N_DROW
MASKED_TILE = 3 * TILES_PER_KIND


def _na_plan(rows):
    wr = min(NA_WIN_ROWS_MAX, rows)
    assert wr == NA_WIN_ROWS_MAX and rows % SUB_ROWS == 0 and rows >= WIN_ROWS
    row_start = np.clip(np.arange(rows) - wr // 2, 0, rows - wr)
    w0s, tiles = [], []
    for r0 in range(0, rows, SUB_ROWS):
        w0 = int(np.clip(r0 - wr // 2, 0, rows - WIN_ROWS))
        for iq in range(SUB_ROWS):
            r = r0 + iq
            assert row_start[r] >= w0 and row_start[r] + wr <= w0 + WIN_ROWS
            ok = lambda w: w < WIN_ROWS and row_start[r] <= w0 + w < row_start[r] + wr
            d_row = lambda w: w0 + w - r + NA_WIN_ROWS_MAX - 1
            for j in range(N_SLABS):
                lo, hi = ok(2 * j), ok(2 * j + 1)
                if lo and hi:
                    tiles.append(d_row(2 * j))
                elif lo:
                    tiles.append(TILES_PER_KIND + d_row(2 * j))
                elif hi:
                    tiles.append(2 * TILES_PER_KIND + d_row(2 * j + 1))
                else:
                    tiles.append(-1)
        w0s.append(w0)
    return np.array(w0s, np.int32), np.array(tiles, np.int32)


def _na_bias(rpb):
    assert GRID_W == 64 and 2 * GRID_W == LANES and TILES_PER_KIND <= LANES and N_DCOL <= LANES
    rpb2 = jnp.pad(rpb.astype(F32).reshape(TILES_PER_KIND, N_DCOL),
                   ((0, LANES - TILES_PER_KIND), (0, LANES - N_DCOL)))
    table = pl.pallas_call(
        _bias_kernel,
        out_shape=jax.ShapeDtypeStruct((LANES, GRID_W * GRID_W), F32),
        name="na_bias",
    )(rpb2)
    table = table[:TILES_PER_KIND].reshape(N_NA_HEADS, N_DROW, GRID_W, GRID_W)
    masked = jnp.full_like(table, NEG_INF)
    nxt = jnp.concatenate([table[:, 1:], masked[:, :1]], axis=1)
    kinds = [jnp.concatenate(pair, axis=-1).reshape(TILES_PER_KIND, GRID_W, LANES)
             for pair in ((table, nxt), (table, masked), (masked, table))]
    return jnp.concatenate(kinds + [jnp.full((1, GRID_W, LANES), NEG_INF, F32)], axis=0)


def _lane_reduce(xs, combine, reduce, neutral):
    chunks = []
    for x in xs:
        rows, n = x.shape
        n_full = n // LANES
        chunks += [x[:, j * LANES:(j + 1) * LANES] for j in range(n_full)]
        if n % LANES:
            fill = jnp.full((rows, LANES - n % LANES), neutral, x.dtype)
            chunks.append(jnp.concatenate([x[:, n_full * LANES:], fill], axis=1))
    return reduce(functools.reduce(combine, chunks), axis=-1, keepdims=True)


def _attn_kernel(w0_ref, tile_ref, q_ref, k_ref, v_ref, kc_ref, vc_ref, *rest, banded, n_sub, sub_q):
    if banded:
        bias_ref, o_ref = rest
    else:
        (o_ref,) = rest
    lane = lax.broadcasted_iota(jnp.int32, (sub_q, LANES), 1)
    first = lane < NA_HEAD_DIM
    nt = (((1,), (1,)), ((), ()))
    stages = [(si, p) for si in range(n_sub) for p in range(N_PAIRS)]

    def window(si):
        blk = pl.program_id(1) * n_sub + si
        return blk, pl.multiple_of(w0_ref[blk] * GRID_W, GRID_W)

    def bias(blk, head):
        row_blocks = []
        for iq in range(SUB_ROWS):
            slabs = []
            for j in range(N_SLABS):
                t = tile_ref[(blk * SUB_ROWS + iq) * N_SLABS + j]
                tile = bias_ref[jnp.where(t < 0, MASKED_TILE, t + head * N_DROW)]
                width = min(LANES, BAND_KEYS - j * LANES)
                slabs.append(tile[:, :width])
            row_blocks.append(jnp.concatenate(slabs, axis=1))
        return jnp.concatenate(row_blocks, axis=0)

    def scores(si, p):
        cols = slice(p * LANES, (p + 1) * LANES)
        q_p = q_ref[0, si * sub_q:(si + 1) * sub_q, cols]
        zero = jnp.zeros_like(q_p)
        qq = jnp.concatenate([jnp.where(first, q_p, zero), jnp.where(first, zero, q_p)], axis=0)
        parts = [lax.dot_general(qq, kc_ref[0, :, cols], nt, preferred_element_type=F32)]
        if banded:
            blk, start = window(si)
            both = jnp.concatenate([bias(blk, 2 * p), bias(blk, 2 * p + 1)], axis=0)
            parts.append(lax.dot_general(qq, k_ref[0, pl.ds(start, BAND_KEYS), cols], nt,
                                         preferred_element_type=F32) + both)
        return parts

    def finish(si, p, parts):
        cols = slice(p * LANES, (p + 1) * LANES)
        m = _lane_reduce(parts, jnp.maximum, jnp.max, NEG_INF)
        es = [jnp.exp2(s - m) for s in parts]
        den = _lane_reduce(es, jnp.add, jnp.sum, 0.0)
        o = jnp.dot(es[0].astype(BF16), vc_ref[0, :, cols], preferred_element_type=F32)
        if banded:
            _, start = window(si)
            o = o + jnp.dot(es[1].astype(BF16), v_ref[0, pl.ds(start, BAND_KEYS), cols],
                            preferred_element_type=F32)
        o = o * (1.0 / den)
        out = jnp.where(first, o[:sub_q], o[sub_q:])
        o_ref[0, si * sub_q:(si + 1) * sub_q, cols] = out.astype(o_ref.dtype)

    nxt = scores(*stages[0])
    for i, (si, p) in enumerate(stages):
        cur = nxt
        if i + 1 < len(stages):
            nxt = scores(*stages[i + 1])
        finish(si, p, cur)


def _neighbourhood_attention(q, k, v, kc, vc, rpb):
    nb, s, dn = q.shape
    rows = s // GRID_W
    nctx = kc.shape[1]
    w0s, tiles = _na_plan(rows)
    bias = _na_bias(rpb)
    n_sub = ATTN_ROWS // SUB_ROWS
    m_rows = ATTN_ROWS * GRID_W
    grid_spec = pltpu.PrefetchScalarGridSpec(
        num_scalar_prefetch=2,
        grid=(nb, rows // ATTN_ROWS),
        in_specs=[
            pl.BlockSpec((1, m_rows, dn), lambda bi, i, w0, pat: (bi, i, 0)),
            pl.BlockSpec((1, s, dn), lambda bi, i, w0, pat: (bi, 0, 0)),
            pl.BlockSpec((1, s, dn), lambda bi, i, w0, pat: (bi, 0, 0)),
            pl.BlockSpec((1, nctx, dn), lambda bi, i, w0, pat: (bi, 0, 0)),
            pl.BlockSpec((1, nctx, dn), lambda bi, i, w0, pat: (bi, 0, 0)),
            pl.BlockSpec(bias.shape, lambda bi, i, w0, pat: (0, 0, 0), pipeline_mode=pl.Buffered(1)),
        ],
        out_specs=pl.BlockSpec((1, m_rows, dn), lambda bi, i, w0, pat: (bi, i, 0)),
    )
    return pl.pallas_call(
        functools.partial(_attn_kernel, banded=True, n_sub=n_sub, sub_q=SUB_Q),
        out_shape=jax.ShapeDtypeStruct((nb, s, dn), BF16),
        grid_spec=grid_spec,
        compiler_params=_cparams(("arbitrary", "arbitrary")),
        name="neighbourhood_attention",
    )(jnp.asarray(w0s), jnp.asarray(tiles), q, k, v, kc, vc, bias)


def _context_attention(q, kc, vc):
    nb, s, dn = q.shape
    spec = pl.BlockSpec((1, s, dn), lambda bi, i, w0, pat: (bi, 0, 0))
    grid_spec = pltpu.PrefetchScalarGridSpec(
        num_scalar_prefetch=2, grid=(nb, 1), in_specs=[spec] * 5, out_specs=spec)
    dummy = jnp.zeros((1,), jnp.int32)
    return pl.pallas_call(
        functools.partial(_attn_kernel, banded=False, n_sub=1, sub_q=s),
        out_shape=jax.ShapeDtypeStruct((nb, s, dn), BF16),
        grid_spec=grid_spec,
        compiler_params=_cparams(("arbitrary", "arbitrary")),
        name="context_attention",
    )(dummy, dummy, q, kc, vc, kc, vc)


def _store_rowmajor(ref, val, base=0):
    n, width = val.shape
    chunks = width // LANES
    for c in range(chunks):
        ref[pl.ds(base + c, n, stride=chunks), :] = val[:, c * LANES:(c + 1) * LANES]


def _load_rowmajor(ref, base, n, chunks):
    return jnp.concatenate([ref[pl.ds(base + c, n, stride=chunks), :] for c in range(chunks)], axis=1)


def _outproj_kernel(yab_ref, yc_ref, x_ref, w_ref, bo_ref, gt_ref, g_ref, b_ref, sh_ref, sc_ref,
                    wrh_ref, wrl_ref, xmid_ref, hm_ref, lg_ref, w_bf):
    @pl.when(_first_step())
    def _():
        w_bf[...] = w_ref[0].astype(BF16)

    half = yab_ref.shape[2]
    tm = x_ref.shape[1]
    chunks = x_ref.shape[2] // LANES
    n_part = max(1, tm // ROW_GROUP)
    rows_per = tm // n_part
    for part in range(n_part):
        rows = slice(part * rows_per, (part + 1) * rows_per)
        y = (jnp.dot(yab_ref[0, rows, :], w_bf[0:half, :], preferred_element_type=F32)
             + jnp.dot(yc_ref[0, rows, :], w_bf[half:, :], preferred_element_type=F32) + bo_ref[...])
        xm = _ln(DEEPNORM_ALPHA * x_ref[0, rows, :] + (1.0 + gt_ref[0]) * y) * g_ref[...] + b_ref[...]
        xmid_ref[0, rows, :] = xm
        hm = _ln(xm) * (1.0 + sc_ref[0]) + sh_ref[0]
        _store_rowmajor(hm_ref, hm, base=part * rows_per * chunks)
        hm_hi = hm.astype(BF16)
        hm_lo = (hm - hm_hi.astype(F32)).astype(BF16)
        lg_ref[0, rows, :] = (jnp.dot(hm_hi, wrh_ref[...], preferred_element_type=F32)
                              + jnp.dot(hm_lo, wrh_ref[...], preferred_element_type=F32)
                              + jnp.dot(hm_hi, wrl_ref[...], preferred_element_type=F32))


def _outproj(yab, yc, x, w, layer, bo, gt, g, b, sh, sc, wr_hi, wr_lo, tm):
    nb, s, d = x.shape
    half = yab.shape[2]
    vec = pl.BlockSpec((1, d), lambda bi, i: (0, 0))
    tok = lambda width: pl.BlockSpec((1, tm, width), lambda bi, i: (bi, i, 0))
    n_i = s // tm
    return pl.pallas_call(
        _outproj_kernel,
        out_shape=[jax.ShapeDtypeStruct((nb, s, d), F32),
                   jax.ShapeDtypeStruct((nb * s * (d // LANES), LANES), F32),
                   jax.ShapeDtypeStruct((nb, s, LANES), F32)],
        grid=(nb, n_i),
        in_specs=[tok(half), tok(half), tok(d),
                  pl.BlockSpec((1, d, d), lambda bi, i: (layer, 0, 0), pipeline_mode=pl.Buffered(1)),
                  vec, _mod_spec(gt, d), vec, vec,
                  _mod_spec(sh, d), _mod_spec(sc, d), pl.BlockSpec((d, LANES), lambda bi, i: (0, 0)),
                  pl.BlockSpec((d, LANES), lambda bi, i: (0, 0))],
        out_specs=[tok(d), pl.BlockSpec((tm * (d // LANES), LANES), lambda bi, i: (bi * n_i + i, 0)),
                   tok(LANES)],
        scratch_shapes=[pltpu.VMEM((d, d), BF16)],
        compiler_params=_cparams(("arbitrary", "arbitrary")),
        name="outproj_postnorm",
    )(yab, yc, x, w, bo, gt[0], g, b, sh[0], sc[0], wr_hi, wr_lo)


CUM_CHUNK = 256
F32_EXP_BIAS = 127
F32_MANT_BITS = 23


def _prefix_count(mask_f32, tri):
    rows, n = mask_f32.shape
    tc = min(CUM_CHUNK, n)
    base = jnp.zeros((rows, 1), F32)
    parts = []
    for ci in range(n // tc):
        blk = mask_f32[:, ci * tc:(ci + 1) * tc]
        parts.append(jnp.dot(blk.astype(BF16), tri[:tc, :tc], preferred_element_type=F32) + base)
        base = base + jnp.sum(blk, axis=-1, keepdims=True)
    return jnp.concatenate(parts, axis=-1)


def _pow2(k):
    return pltpu.bitcast((k + F32_EXP_BIAS) << F32_MANT_BITS, F32)


def _route_kernel(lg_ref, slot_c_ref, gate_t_ref, *, cap, slot_stride):
    nb = lg_ref.shape[0]
    assert nb * N_EXPERTS == LANES
    rows = []
    for b in range(nb):
        lg = lg_ref[b]
        lane = lax.broadcasted_iota(jnp.int32, lg.shape, 1)
        lgm = jnp.where(lane < N_EXPERTS, lg, NEG_INF)
        ex = jnp.exp(lgm - jnp.max(lgm, axis=-1, keepdims=True))
        aff = ex / jnp.sum(ex, axis=-1, keepdims=True)
        rows.append(aff.T[0:N_EXPERTS, :])
    a = jnp.concatenate(rows, axis=0)
    capf = float(cap)

    def enough(t):
        return jnp.sum((a >= t).astype(F32), axis=-1, keepdims=True) >= capf

    def exp_step(_, carry):
        lo, hi = carry
        mid = lo + ((hi - lo + 1) >> 1)
        ok = enough(_pow2(mid))
        return jnp.where(ok, mid, lo), jnp.where(ok, hi, mid - 1)

    k_lo = jnp.full((LANES, 1), -F32_EXP_BIAS, jnp.int32)
    k_hi = jnp.zeros((LANES, 1), jnp.int32)
    k_lo, _ = lax.fori_loop(0, 7, exp_step, (k_lo, k_hi))
    base = _pow2(k_lo)

    def mant_step(_, carry):
        t, step = carry
        step = step * 0.5
        cand = t + step
        return jnp.where(enough(cand), cand, t), step

    thr, _ = lax.fori_loop(0, F32_MANT_BITS, mant_step, (base, base))

    r_i = lax.broadcasted_iota(jnp.int32, (CUM_CHUNK, CUM_CHUNK), 0)
    c_i = lax.broadcasted_iota(jnp.int32, (CUM_CHUNK, CUM_CHUNK), 1)
    tri = (r_i < c_i).astype(BF16)
    gt = (a > thr).astype(F32)
    eq = (a == thr).astype(F32)
    need = capf - jnp.sum(gt, axis=-1, keepdims=True)
    sel = gt + eq * (_prefix_count(eq, tri) < need).astype(F32)
    pos = _prefix_count(sel, tri)
    sample = lax.broadcasted_iota(jnp.int32, (LANES, 1), 0) >> (N_EXPERTS.bit_length() - 1)
    slot = jnp.where(sel > 0.0, pos + (sample * slot_stride).astype(F32), -1.0)
    for b in range(nb):
        lo = b * N_EXPERTS
        gate_t_ref[b] = a[lo:lo + N_EXPERTS, :]
        rolled = slot if b == 0 else jnp.concatenate([slot[lo:, :], slot[:lo, :]], axis=0)
        slot_c_ref[b] = rolled.T


def _route(logits, cap, slot_stride):
    nb, s, _ = logits.shape
    whole = lambda shape: pl.BlockSpec(shape, lambda i: (0,) * len(shape))
    return pl.pallas_call(
        functools.partial(_route_kernel, cap=cap, slot_stride=slot_stride),
        out_shape=[jax.ShapeDtypeStruct((nb, s, LANES), F32), jax.ShapeDtypeStruct((nb, N_EXPERTS, s), F32)],
        grid=(1,),
        in_specs=[whole((nb, s, LANES))],
        out_specs=[whole((nb, s, LANES)), whole((nb, N_EXPERTS, s))],
        compiler_params=_cparams(("arbitrary",)),
        name="route",
    )(logits)


TOK_SPLIT = 64


def _slot_list_kernel(slot_ref, gate_ref, idx_ref, g_ref, *, n_slots, tok_stride, merge):
    s = slot_ref.shape[1]
    b = pl.program_id(0)
    slot_id = lax.broadcasted_iota(jnp.int32, (s, n_slots), 1).astype(F32)
    tok = lax.broadcasted_iota(jnp.int32, (1, s), 1) + b * tok_stride
    tok_hi = (tok >> (TOK_SPLIT.bit_length() - 1)).astype(F32)
    tok_lo = (tok & (TOK_SPLIT - 1)).astype(F32)
    zeros = jnp.zeros((SUBLANES - 5, s), F32)
    idx_rows, g_rows = [], []
    for e in range(N_EXPERTS):
        hit = jnp.where(slot_ref[0, :, e:e + 1] == slot_id, 1.0, 0.0).astype(BF16)
        g0 = gate_ref[0, e:e + 1, :]
        g_hi = g0.astype(BF16).astype(F32)
        g_mid = (g0 - g_hi).astype(BF16).astype(F32)
        g_lo = g0 - g_hi - g_mid
        lhs = jnp.concatenate([tok_hi, tok_lo, g_hi, g_mid, g_lo, zeros], axis=0).astype(BF16)
        out = jnp.dot(lhs, hit, preferred_element_type=F32)
        idx_rows.append(out[0:1] * float(TOK_SPLIT) + out[1:2])
        g_rows.append(out[2:3] + out[3:4] + out[4:5])
    idx = jnp.concatenate(idx_rows, axis=0).astype(jnp.int32) * SUBLANES
    g = jnp.concatenate(g_rows + [jnp.zeros((LANES - N_EXPERTS, n_slots), F32)], axis=0)
    g_t = g.T
    g_cols = [jnp.broadcast_to(g_t[:, e:e + 1], (n_slots, LANES)) for e in range(N_EXPERTS)]
    if merge:
        @pl.when(b == 0)
        def _():
            idx_ref[0] = idx
            for e in range(N_EXPERTS):
                g_ref[e] = g_cols[e]

        @pl.when(b > 0)
        def _():
            idx_ref[0] = idx_ref[0] + idx
            for e in range(N_EXPERTS):
                g_ref[e] = g_ref[e] + g_cols[e]
    else:
        idx_ref[0] = idx
        for e in range(N_EXPERTS):
            g_ref[e] = g_cols[e]


def _slot_lists(slot_c, gate_t, n_slots, merge):
    nb, s, _ = slot_c.shape
    nbo = 1 if merge else nb
    idx_map = (lambda bi: (0, 0, 0)) if merge else (lambda bi: (bi, 0, 0))
    g_map = (lambda bi: (0, 0, 0)) if merge else (lambda bi: (0, bi, 0))
    return pl.pallas_call(
        functools.partial(_slot_list_kernel, n_slots=n_slots, tok_stride=s if merge else 0, merge=merge),
        out_shape=[jax.ShapeDtypeStruct((nbo, N_EXPERTS, n_slots), jnp.int32),
                   jax.ShapeDtypeStruct((N_EXPERTS, nbo * n_slots, LANES), F32)],
        grid=(nb,),
        in_specs=[pl.BlockSpec((1, s, LANES), lambda bi: (bi, 0, 0)),
                  pl.BlockSpec((1, N_EXPERTS, s), lambda bi: (bi, 0, 0))],
        out_specs=[pl.BlockSpec((1, N_EXPERTS, n_slots), idx_map),
                   pl.BlockSpec((N_EXPERTS, n_slots, LANES), g_map)],
        compiler_params=_cparams(("arbitrary",)),
        name="slot_lists",
    )(slot_c, gate_t)


ROW_UNROLL = 16


def _dispatch_kernel(idx_ref, hm_ref, x_ref, rows_scr, *, n_slots):
    b = pl.program_id(0)
    chunks = x_ref.shape[2] // LANES
    for e in range(N_EXPERTS):
        base = (b * N_EXPERTS + e) * n_slots

        def move(i, carry):
            s0 = i * ROW_UNROLL
            first = base + s0
            tiles = [hm_ref[pl.ds(pl.multiple_of(idx_ref[first + u], chunks), chunks), :]
                     for u in range(ROW_UNROLL)]
            dst = pl.multiple_of(s0 * chunks, ROW_UNROLL * chunks)
            rows_scr[pl.ds(dst, ROW_UNROLL * chunks), :] = jnp.concatenate(tiles, axis=0)
            return carry

        lax.fori_loop(0, n_slots // ROW_UNROLL, move, 0)
        x_ref[e] = _load_rowmajor(rows_scr, 0, n_slots, chunks).astype(x_ref.dtype)


def _dispatch(idx, hm_rm, nb, s, d, n_slots):
    chunks = d // LANES
    assert chunks == SUBLANES and n_slots % ROW_UNROLL == 0
    grid_spec = pltpu.PrefetchScalarGridSpec(
        num_scalar_prefetch=1,
        grid=(nb,),
        in_specs=[pl.BlockSpec((s * chunks, LANES), lambda bi, idx_r: (bi, 0))],
        out_specs=pl.BlockSpec((N_EXPERTS, n_slots, d), lambda bi, idx_r: (0, bi, 0)),
        scratch_shapes=[pltpu.VMEM((n_slots * chunks, LANES), F32)],
    )
    return pl.pallas_call(
        functools.partial(_dispatch_kernel, n_slots=n_slots),
        out_shape=jax.ShapeDtypeStruct((N_EXPERTS, nb * n_slots, d), BF16),
        grid_spec=grid_spec,
        compiler_params=_cparams(("arbitrary",)),
        name="dispatch",
    )(idx, hm_rm)


def _expert_kernel(*refs, n_main, has_extra):
    if has_extra:
        x_ref, g_ref, xx_ref, gg_ref, wg_ref, wu_ref, wd_ref, y_ref, yy_ref, wg_bf, wu_bf, wd_bf = refs
    else:
        x_ref, g_ref, wg_ref, wu_ref, wd_ref, y_ref, wg_bf, wu_bf, wd_bf = refs
    i = pl.program_id(1)

    @pl.when(i == 0)
    def _():
        wg_bf[...] = wg_ref[0, 0].astype(BF16)
        wu_bf[...] = wu_ref[0, 0].astype(BF16)
        wd_bf[...] = wd_ref[0, 0].astype(BF16)

    def ffn(x_e, gate, out_ref):
        a = jnp.dot(x_e, wg_bf[...], preferred_element_type=F32)
        u = jnp.dot(x_e, wu_bf[...], preferred_element_type=F32)
        h = (a * _sigmoid(a) * u).astype(BF16)
        _store_rowmajor(out_ref, jnp.dot(h, wd_bf[...], preferred_element_type=F32) * gate[:, 0:1])

    if has_extra:
        @pl.when(i < n_main)
        def _():
            ffn(x_ref[0], g_ref[0], y_ref.at[0])

        @pl.when(i == n_main)
        def _():
            ffn(xx_ref[0], gg_ref[0], yy_ref.at[0])
    else:
        ffn(x_ref[0], g_ref[0], y_ref.at[0])


def _experts(x_e, g_e, x_extra, g_extra, wg, wu, wd, layer, tm):
    _, rows, d = x_e.shape
    f = wg.shape[3]
    chunks = d // LANES
    n_main = rows // tm
    has_extra = x_extra is not None
    main_map = lambda e, i: (e, jnp.minimum(i, n_main - 1), 0)
    w_spec = lambda shape: pl.BlockSpec((1, 1) + shape, lambda e, i: (layer, e, 0, 0))
    in_specs = [pl.BlockSpec((1, tm, d), main_map), pl.BlockSpec((1, tm, LANES), main_map)]
    out_specs = [pl.BlockSpec((1, tm * chunks, LANES), main_map)]
    out_shape = [jax.ShapeDtypeStruct((N_EXPERTS, rows * chunks, LANES), F32)]
    args = [x_e, g_e]
    if has_extra:
        rows2 = x_extra.shape[1]
        in_specs += [pl.BlockSpec((1, rows2, d), lambda e, i: (e, 0, 0)),
                     pl.BlockSpec((1, rows2, LANES), lambda e, i: (e, 0, 0))]
        out_specs.append(pl.BlockSpec((1, rows2 * chunks, LANES), lambda e, i: (e, 0, 0)))
        out_shape.append(jax.ShapeDtypeStruct((N_EXPERTS, rows2 * chunks, LANES), F32))
        args += [x_extra, g_extra]
    outs = pl.pallas_call(
        functools.partial(_expert_kernel, n_main=n_main, has_extra=has_extra),
        out_shape=out_shape,
        grid=(N_EXPERTS, n_main + int(has_extra)),
        in_specs=in_specs + [w_spec((d, f)), w_spec((d, f)), w_spec((f, d))],
        out_specs=out_specs,
        scratch_shapes=[pltpu.VMEM((d, f), BF16), pltpu.VMEM((d, f), BF16), pltpu.VMEM((f, d), BF16)],
        compiler_params=_cparams(("arbitrary", "arbitrary")),
        name="experts",
    )(*args, wg, wu, wd)
    return outs if has_extra else (outs[0], None)


COMBINE_EXPERTS = 4
NORM_CHUNK = 256


def _combine_kernel(idx_ref, y_ref, x_ref, gt_ref, g_ref, b_ref, o_ref, acc, *, n_slots):
    b = pl.program_id(0)
    j = pl.program_id(1)
    s, d = x_ref.shape[1], x_ref.shape[2]
    chunks = d // LANES

    @pl.when(j == 0)
    def _():
        acc[...] = jnp.zeros(acc.shape, F32)

    for eg in range(COMBINE_EXPERTS):
        base = (b * N_EXPERTS + j * COMBINE_EXPERTS + eg) * n_slots

        def add(i, carry):
            s0 = i * ROW_UNROLL
            first = base + s0
            src = pl.multiple_of(s0 * chunks, ROW_UNROLL * chunks)
            y_rows = y_ref[eg, pl.ds(src, ROW_UNROLL * chunks), :]
            new = []
            for u in range(ROW_UNROLL):
                dst = pl.multiple_of(idx_ref[first + u], chunks)
                new.append((dst, acc[pl.ds(dst, chunks), :] + y_rows[u * chunks:(u + 1) * chunks, :]))
            for dst, val in new:
                acc[pl.ds(dst, chunks), :] = val
            return carry

        lax.fori_loop(0, n_slots // ROW_UNROLL, add, 0)

    @pl.when(j == pl.num_programs(1) - 1)
    def _():
        for ci in range(s // NORM_CHUNK):
            rows = slice(ci * NORM_CHUNK, (ci + 1) * NORM_CHUNK)
            ym = _load_rowmajor(acc, ci * NORM_CHUNK * chunks, NORM_CHUNK, chunks)
            z = DEEPNORM_ALPHA * x_ref[0, rows, :] + (1.0 + gt_ref[0]) * ym
            o_ref[0, rows, :] = _ln(z) * g_ref[...] + b_ref[...]


def _combine(idx, y_rm, x_mid, gt, g, b, n_slots):
    nb, s, d = x_mid.shape
    chunks = d // LANES
    assert chunks == SUBLANES and N_EXPERTS % COMBINE_EXPERTS == 0 and s % NORM_CHUNK == 0
    vec = pl.BlockSpec((1, d), lambda bi, j, i_r: (0, 0))
    grid_spec = pltpu.PrefetchScalarGridSpec(
        num_scalar_prefetch=1,
        grid=(nb, N_EXPERTS // COMBINE_EXPERTS),
        in_specs=[
            pl.BlockSpec((COMBINE_EXPERTS, n_slots * chunks, LANES), lambda bi, j, i_r: (j, bi, 0)),
            pl.BlockSpec((1, s, d), lambda bi, j, i_r: (bi, 0, 0)),
            _mod_spec(gt, d), vec, vec,
        ],
        out_specs=pl.BlockSpec((1, s, d), lambda bi, j, i_r: (bi, 0, 0)),
        scratch_shapes=[pltpu.VMEM((s * chunks, LANES), F32)],
    )
    return pl.pallas_call(
        functools.partial(_combine_kernel, n_slots=n_slots),
        out_shape=jax.ShapeDtypeStruct((nb, s, d), F32),
        grid_spec=grid_spec,
        compiler_params=_cparams(("arbitrary", "arbitrary")),
        name="combine_postnorm",
    )(idx, y_rm, x_mid, gt[0], g, b)


def kernel(x, c, ctx, c_ctx, w_mod, b_mod, w_in, b_in, w_short, w_conf_dw, b_conf_dw, g_conf_ln, b_conf_ln,
           na_rpb, w_out, b_out, g_post1, b_post1, w_router, w_gate, w_up, w_down, g_post2, b_post2):
    bsz, seq, d = x.shape
    nctx = ctx.shape[1]
    cap = EC_CAPACITY_FACTOR * seq // N_EXPERTS
    cap_ctx = EC_CAPACITY_FACTOR * nctx // N_EXPERTS
    q_scale = NA_HEAD_DIM ** -0.5 * LOG2E

    cond = jnp.concatenate([c, c_ctx[None, :], jnp.zeros((MOD_ROWS - bsz - 1, d), F32)], axis=0)
    mods = _modulation(cond, w_mod, b_mod)

    lat_splits = ((3 * D_CONV, 1.0), (2 * D_CONF, 1.0), (D_NA, q_scale), (D_NA, 1.0), (D_NA, 1.0))
    lat_dtypes = (F32, F32, BF16, BF16, BF16)
    kv_splits = ((D_NA, 1.0), (D_NA, 1.0))
    b_in3 = b_in[:, None, :]

    xc = ctx
    for l in range(DEPTH):
        last = l == DEPTH - 1
        m_lat = [(mods, (l * N_MOD + k) * MOD_ROWS, True) for k in range(N_MOD)]
        m_ctx = [(mods, (l * N_MOD + k) * MOD_ROWS + bsz, False) for k in range(N_MOD)]
        b_out_l = b_out[l][None, :]
        wr = jnp.pad(w_router[l], ((0, 0), (0, LANES - N_EXPERTS)))
        wr_hi = wr.astype(BF16)
        wr_lo = (wr - wr_hi.astype(F32)).astype(BF16)
        g1, b1 = g_post1[l][None, :], b_post1[l][None, :]
        g2, b2 = g_post2[l][None, :], b_post2[l][None, :]
        conv_w = (w_short[l], w_conf_dw[l], b_conf_dw[l], g_conf_ln[l], b_conf_ln[l])

        if last:
            k_c, v_c = _inproj(xc, m_ctx[0], m_ctx[1], w_in, b_in3, l, OFF_K, kv_splits, (BF16, BF16), tm=nctx)
        else:
            uac, ubc, q_c, k_c, v_c = _inproj(xc, m_ctx[0], m_ctx[1], w_in, b_in3, l, 0,
                                              lat_splits, lat_dtypes, tm=nctx)

        ua, ub, q, k, v = _inproj(x, m_lat[0], m_lat[1], w_in, b_in3, l, 0, lat_splits, lat_dtypes, tm=512)
        yab = _conv_mixers(ua, ub, *conv_w)
        yc = _neighbourhood_attention(q, k, v, k_c, v_c, na_rpb[l])
        x_mid, hm, logits = _outproj(yab, yc, x, w_out, l, b_out_l, m_lat[2], g1, b1, m_lat[3], m_lat[4],
                                     wr_hi, wr_lo, tm=512)

        idx, gates = _slot_lists(*_route(logits, cap, 0), cap, merge=False)
        idx = idx.reshape(-1)
        x_e = _dispatch(idx, hm, bsz, seq, d, cap)

        x_ec = gates_c = None
        if not last:
            yabc = _conv_mixers(uac, ubc, *conv_w)
            ycc = _context_attention(q_c, k_c, v_c)
            xc_mid, hmc, logits_c = _outproj(yabc, ycc, xc, w_out, l, b_out_l, m_ctx[2], g1, b1,
                                             m_ctx[3], m_ctx[4], wr_hi, wr_lo, tm=nctx)
            n_c = bsz * cap_ctx
            idx_c, gates_c = _slot_lists(*_route(logits_c, cap_ctx, cap_ctx), n_c, merge=True)
            idx_c = idx_c.reshape(-1)
            x_ec = _dispatch(idx_c, hmc, 1, bsz * nctx, d, n_c)

        y_e, y_ec = _experts(x_e, gates, x_ec, gates_c, w_gate, w_up, w_down, l, tm=512)
        x = _combine(idx, y_e, x_mid, m_lat[5], g2, b2, cap)
        if not last:
            xc = _combine(idx_c, y_ec, xc_mid.reshape(1, bsz * nctx, d),
                          m_ctx[5], g2, b2, n_c).reshape(bsz, nctx, d)
    return x
```

```python
import functools
import math

import numpy as np
import jax
import jax.numpy as jnp
from jax import lax
from jax.experimental import pallas as pl
from jax.experimental.pallas import tpu as pltpu

F32 = jnp.float32
BF16 = jnp.bfloat16

D_MODEL = 1024
DEPTH = 2
GRID_W = 64
D_CONV = D_MODEL // 4
D_CONF = D_MODEL // 4
NA_HEAD_DIM = 64
D_NA = D_MODEL - D_CONV - D_CONF
N_NA_HEADS = D_NA // NA_HEAD_DIM
SHORT_CONV_W = 3
CONF_CONV_W = 31
NA_WIN_ROWS_MAX = 8
NA_WIN_COLS = 16
N_EXPERTS = 16
EC_CAPACITY_FACTOR = 2
D_EXPERT = 1024
LN_EPS = 1e-5
DEEPNORM_ALPHA = (2.0 * DEPTH) ** 0.25
NEG_INF = -1e30
LOG2E = math.log2(math.e)

OFF_A = 0
OFF_B = OFF_A + 3 * D_CONV
OFF_Q = OFF_B + 2 * D_CONF
OFF_K = OFF_Q + D_NA
OFF_V = OFF_K + D_NA
D_IN = OFF_V + D_NA

LANES = 128
SUBLANES = 8
MOD_ROWS = 16
VMEM_LIMIT = 56 * 1024 * 1024
ATTN_ROWS = 8
N_PAIRS = D_NA // LANES
ROW_GROUP = 256
OUT_ROW_GROUP = 128
HI = lax.Precision.HIGHEST


def _cparams(sem):
    return pltpu.CompilerParams(dimension_semantics=sem, vmem_limit_bytes=VMEM_LIMIT)


def _ln(x):
    mu = jnp.mean(x, axis=-1, keepdims=True)
    xc = x - mu
    var = jnp.mean(xc * xc, axis=-1, keepdims=True)
    return xc * lax.rsqrt(var + LN_EPS)


def _sigmoid(x):
    return 1.0 / (1.0 + jnp.exp(-x))


def _mod_kernel(cond_ref, w_ref, b_ref, o_ref):
    s = cond_ref[...]
    s = s * _sigmoid(s)
    o_ref[0] = jnp.dot(s, w_ref[0], preferred_element_type=F32, precision=HI) + b_ref[0]


N_MOD = 6


def _modulation(cond, w_mod, b_mod):
    n_l, d, n = w_mod.shape
    out = pl.pallas_call(
        _mod_kernel,
        out_shape=jax.ShapeDtypeStruct((n_l * N_MOD, MOD_ROWS, d), F32),
        grid=(n_l, N_MOD),
        in_specs=[
            pl.BlockSpec((MOD_ROWS, d), lambda l, k: (0, 0)),
            pl.BlockSpec((1, d, d), lambda l, k: (l, 0, k)),
            pl.BlockSpec((1, 1, d), lambda l, k: (l, 0, k)),
        ],
        out_specs=pl.BlockSpec((1, MOD_ROWS, d), lambda l, k: (l * N_MOD + k, 0, 0)),
        compiler_params=_cparams(("arbitrary", "arbitrary")),
        name="modulation",
    )(cond, w_mod, b_mod.reshape(n_l, 1, n))
    return out.reshape(n_l * N_MOD * MOD_ROWS, 1, d)


def _mod_spec(mod, d):
    _, row0, per_sample = mod
    if per_sample:
        return pl.BlockSpec((1, 1, d), lambda bi, *_: (row0 + bi, 0, 0))
    return pl.BlockSpec((1, 1, d), lambda bi, *_: (row0, 0, 0))


def _first_step():
    return (pl.program_id(0) == 0) & (pl.program_id(1) == 0)


def _inproj_kernel(x_ref, sh_ref, sc_ref, w_ref, b_ref, *rest, splits, col0):
    *o_refs, w_bf = rest

    @pl.when(_first_step())
    def _():
        w_bf[...] = w_ref[0].astype(BF16)

    tm = x_ref.shape[1]
    n_part = max(1, tm // ROW_GROUP)
    rows_per = tm // n_part
    groups = [slice(part_i * rows_per, (part_i + 1) * rows_per) for part_i in range(n_part)]
    hs = [(_ln(x_ref[0, rows, :]) * (1.0 + sc_ref[0]) + sh_ref[0]).astype(BF16) for rows in groups]
    for rows, h in zip(groups, hs):
        u = jnp.dot(h, w_bf[:, col0:], preferred_element_type=F32) + b_ref[0, :, col0:]
        off = 0
        for o_ref, (width, scale) in zip(o_refs, splits):
            part = u[:, off:off + width]
            if scale != 1.0:
                part = part * scale
            o_ref[0, rows, :] = part.astype(o_ref.dtype)
            off += width


def _inproj(x, sh, sc, w, b, layer, col0, splits, dtypes, tm):
    nb, s, d = x.shape
    n = w.shape[2]
    return pl.pallas_call(
        functools.partial(_inproj_kernel, splits=splits, col0=col0),
        out_shape=[jax.ShapeDtypeStruct((nb, s, wd), dt) for (wd, _), dt in zip(splits, dtypes)],
        grid=(nb, s // tm),
        in_specs=[
            pl.BlockSpec((1, tm, d), lambda bi, i: (bi, i, 0)),
            _mod_spec(sh, d),
            _mod_spec(sc, d),
            pl.BlockSpec((1, d, n), lambda bi, i: (layer, 0, 0), pipeline_mode=pl.Buffered(1)),
            pl.BlockSpec((1, 1, n), lambda bi, i: (layer, 0, 0)),
        ],
        out_specs=[pl.BlockSpec((1, tm, wd), lambda bi, i: (bi, i, 0)) for wd, _ in splits],
        scratch_shapes=[pltpu.VMEM((d, n), BF16)],
        compiler_params=_cparams(("arbitrary", "arbitrary")),
        name="inproj",
    )(x, sh[0], sc[0], w, b)


CONV_CHUNK = 128
Z_PAD = 8
H_PAD = 16


def _conv_kernel(ua_ref, ub_ref, ws_ref, wd_ref, bd_ref, g_ref, b_ref, o_ref, z_scr, h_scr, *, seq):
    c = D_CONV
    z_scr[0:Z_PAD, :] = jnp.zeros((Z_PAD, c), F32)
    z_scr[Z_PAD + seq:2 * Z_PAD + seq, :] = jnp.zeros((Z_PAD, c), F32)
    h_scr[0, 0:H_PAD, :] = jnp.zeros((H_PAD, c), F32)
    h_scr[0, H_PAD + seq:2 * H_PAD + seq, :] = jnp.zeros((H_PAD, c), F32)
    z_scr[Z_PAD:Z_PAD + seq, :] = ua_ref[0, :, c:2 * c] * ua_ref[0, :, 2 * c:3 * c]
    h_scr[0, H_PAD:H_PAD + seq, :] = ub_ref[0, :, 0:c] * _sigmoid(ub_ref[0, :, c:2 * c])
    n_rows = seq + 2 * H_PAD
    h_all = h_scr[0]
    for r in range(1, SUBLANES):
        h_scr[r] = pltpu.roll(h_all, n_rows - r, axis=0)
    tc = min(CONV_CHUNK, seq)
    for ci in range(seq // tc):
        t0 = ci * tc
        acc = ws_ref[0:1, :] * z_scr[t0 + Z_PAD - 1:t0 + Z_PAD - 1 + tc, :]
        for j in range(1, SHORT_CONV_W):
            s0 = t0 + Z_PAD - 1 + j
            acc = acc + ws_ref[j:j + 1, :] * z_scr[s0:s0 + tc, :]
        ya = ua_ref[0, t0:t0 + tc, 0:c] * acc
        hb = bd_ref[...]
        for j in range(CONF_CONV_W):
            s0 = t0 + H_PAD - CONF_CONV_W // 2 + j
            a0 = s0 - s0 % SUBLANES
            hb = hb + wd_ref[j:j + 1, :] * h_scr[s0 % SUBLANES, a0:a0 + tc, :]
        hn = _ln(hb) * g_ref[...] + b_ref[...]
        yb = hn * _sigmoid(hn)
        o_ref[0, t0:t0 + tc, 0:c] = ya.astype(o_ref.dtype)
        o_ref[0, t0:t0 + tc, c:2 * c] = yb.astype(o_ref.dtype)


def _conv_mixers(ua, ub, w_short, w_dw, b_dw, g_ln, b_ln):
    nb, s, _ = ua.shape
    c = D_CONV
    full = lambda shape: pl.BlockSpec(shape, lambda bi: (0,) * len(shape))
    return pl.pallas_call(
        functools.partial(_conv_kernel, seq=s),
        out_shape=jax.ShapeDtypeStruct((nb, s, 2 * c), BF16),
        grid=(nb,),
        in_specs=[
            pl.BlockSpec((1, s, 3 * c), lambda bi: (bi, 0, 0)),
            pl.BlockSpec((1, s, 2 * c), lambda bi: (bi, 0, 0)),
            full((SHORT_CONV_W, c)), full((CONF_CONV_W, c)), full((1, c)), full((1, c)), full((1, c)),
        ],
        out_specs=pl.BlockSpec((1, s, 2 * c), lambda bi: (bi, 0, 0)),
        scratch_shapes=[pltpu.VMEM((s + 2 * Z_PAD, c), F32), pltpu.VMEM((SUBLANES, s + 2 * H_PAD, c), F32)],
        compiler_params=_cparams(("arbitrary",)),
        name="conv_mixers",
    )(ua, ub, w_short, w_dw, b_dw.reshape(1, c), g_ln.reshape(1, c), b_ln.reshape(1, c))


SUB_ROWS = 2
WIN_ROWS = SUB_ROWS + NA_WIN_ROWS_MAX - 1
BAND_KEYS = WIN_ROWS * GRID_W
SUB_Q = SUB_ROWS * GRID_W
N_DROW = 2 * NA_WIN_ROWS_MAX - 1
N_DCOL = 2 * NA_WIN_COLS - 1


def _bias_kernel(rpb_ref, o_ref):
    n_rows, n_cols = o_ref.shape
    col = lax.broadcasted_iota(jnp.int32, (LANES, n_cols), 1)
    qc = col >> 6
    kc = col & (GRID_W - 1)
    d_col = jnp.clip(kc - qc + (NA_WIN_COLS - 1), 0, N_DCOL - 1)
    onehot = (lax.broadcasted_iota(jnp.int32, (LANES, n_cols), 0) == d_col).astype(F32)
    vals = jnp.dot(rpb_ref[...], onehot, preferred_element_type=F32, precision=HI)
    col_r = lax.broadcasted_iota(jnp.int32, (n_rows, n_cols), 1)
    qc_r = col_r >> 6
    kc_r = col_r & (GRID_W - 1)
    c0 = jnp.clip(qc_r - NA_WIN_COLS // 2, 0, GRID_W - NA_WIN_COLS)
    inside = (kc_r >= c0) & (kc_r < c0 + NA_WIN_COLS)
    o_ref[...] = jnp.where(inside, vals * LOG2E, NEG_INF)


N_SLABS = -(-WIN_ROWS // 2)
TILES_PER_KIND = N_NA_HEADS * N_DROW
MASKED_TILE = 3 * TILES_PER_KIND


def _na_plan(rows):
    wr = min(NA_WIN_ROWS_MAX, rows)
    assert wr == NA_WIN_ROWS_MAX and rows % SUB_ROWS == 0 and rows >= WIN_ROWS
    row_start = np.clip(np.arange(rows) - wr // 2, 0, rows - wr)
    w0s, tiles = [], []
    for r0 in range(0, rows, SUB_ROWS):
        w0 = int(np.clip(r0 - wr // 2, 0, rows - WIN_ROWS))
        for iq in range(SUB_ROWS):
            r = r0 + iq
            assert row_start[r] >= w0 and row_start[r] + wr <= w0 + WIN_ROWS
            ok = lambda w: w < WIN_ROWS and row_start[r] <= w0 + w < row_start[r] + wr
            d_row = lambda w: w0 + w - r + NA_WIN_ROWS_MAX - 1
            for j in range(N_SLABS):
                lo, hi = ok(2 * j), ok(2 * j + 1)
                if lo and hi:
                    tiles.append(d_row(2 * j))
                elif lo:
                    tiles.append(TILES_PER_KIND + d_row(2 * j))
                elif hi:
                    tiles.append(2 * TILES_PER_KIND + d_row(2 * j + 1))
                else:
                    tiles.append(-1)
        w0s.append(w0)
    return np.array(w0s, np.int32), np.array(tiles, np.int32)


def _na_bias(rpb):
    assert GRID_W == 64 and 2 * GRID_W == LANES and TILES_PER_KIND <= LANES and N_DCOL <= LANES
    rpb2 = jnp.pad(rpb.astype(F32).reshape(TILES_PER_KIND, N_DCOL),
                   ((0, LANES - TILES_PER_KIND), (0, LANES - N_DCOL)))
    table = pl.pallas_call(
        _bias_kernel,
        out_shape=jax.ShapeDtypeStruct((LANES, GRID_W * GRID_W), F32),
        name="na_bias",
    )(rpb2)
    table = table[:TILES_PER_KIND].reshape(N_NA_HEADS, N_DROW, GRID_W, GRID_W)
    masked = jnp.full_like(table, NEG_INF)
    nxt = jnp.concatenate([table[:, 1:], masked[:, :1]], axis=1)
    kinds = [jnp.concatenate(pair, axis=-1).reshape(TILES_PER_KIND, GRID_W, LANES)
             for pair in ((table, nxt), (table, masked), (masked, table))]
    return jnp.concatenate(kinds + [jnp.full((1, GRID_W, LANES), NEG_INF, F32)], axis=0)


def _lane_reduce(xs, combine, reduce, neutral):
    chunks = []
    for x in xs:
        rows, n = x.shape
        n_full = n // LANES
        chunks += [x[:, j * LANES:(j + 1) * LANES] for j in range(n_full)]
        if n % LANES:
            fill = jnp.full((rows, LANES - n % LANES), neutral, x.dtype)
            chunks.append(jnp.concatenate([x[:, n_full * LANES:], fill], axis=1))
    return reduce(functools.reduce(combine, chunks), axis=-1, keepdims=True)


def _attn_kernel(w0_ref, tile_ref, q_ref, k_ref, v_ref, kc_ref, vc_ref, *rest, banded, n_sub, sub_q):
    if banded:
        bias_ref, o_ref = rest
    else:
        (o_ref,) = rest
    lane = lax.broadcasted_iota(jnp.int32, (sub_q, LANES), 1)
    first = lane < NA_HEAD_DIM
    nt = (((1,), (1,)), ((), ()))
    stages = [(si, p) for si in range(n_sub) for p in range(N_PAIRS)]

    def window(si):
        blk = pl.program_id(1) * n_sub + si
        return blk, pl.multiple_of(w0_ref[blk] * GRID_W, GRID_W)

    def bias(blk, head):
        row_blocks = []
        for iq in range(SUB_ROWS):
            slabs = []
            for j in range(N_SLABS):
                t = tile_ref[(blk * SUB_ROWS + iq) * N_SLABS + j]
                tile = bias_ref[jnp.where(t < 0, MASKED_TILE, t + head * N_DROW)]
                width = min(LANES, BAND_KEYS - j * LANES)
                slabs.append(tile[:, :width])
            row_blocks.append(jnp.concatenate(slabs, axis=1))
        return jnp.concatenate(row_blocks, axis=0)

    def scores(si, p):
        cols = slice(p * LANES, (p + 1) * LANES)
        q_p = q_ref[0, si * sub_q:(si + 1) * sub_q, cols]
        zero = jnp.zeros_like(q_p)
        qq = jnp.concatenate([jnp.where(first, q_p, zero), jnp.where(first, zero, q_p)], axis=0)
        parts = [lax.dot_general(qq, kc_ref[0, :, cols], nt, preferred_element_type=F32)]
        if banded:
            blk, start = window(si)
            both = jnp.concatenate([bias(blk, 2 * p), bias(blk, 2 * p + 1)], axis=0)
            parts.append(lax.dot_general(qq, k_ref[0, pl.ds(start, BAND_KEYS), cols], nt,
                                         preferred_element_type=F32) + both)
        return parts

    def finish(si, p, parts):
        cols = slice(p * LANES, (p + 1) * LANES)
        m = _lane_reduce(parts, jnp.maximum, jnp.max, NEG_INF)
        es = [jnp.exp2(s - m) for s in parts]
        den = _lane_reduce(es, jnp.add, jnp.sum, 0.0)
        o = jnp.dot(es[0].astype(BF16), vc_ref[0, :, cols], preferred_element_type=F32)
        if banded:
            _, start = window(si)
            o = o + jnp.dot(es[1].astype(BF16), v_ref[0, pl.ds(start, BAND_KEYS), cols],
                            preferred_element_type=F32)
        o = o * (1.0 / den)
        out = jnp.where(first, o[:sub_q], o[sub_q:])
        o_ref[0, si * sub_q:(si + 1) * sub_q, cols] = out.astype(o_ref.dtype)

    nxt = scores(*stages[0])
    for i, (si, p) in enumerate(stages):
        cur = nxt
        if i + 1 < len(stages):
            nxt = scores(*stages[i + 1])
        finish(si, p, cur)


def _neighbourhood_attention(q, k, v, kc, vc, rpb):
    nb, s, dn = q.shape
    rows = s // GRID_W
    nctx = kc.shape[1]
    w0s, tiles = _na_plan(rows)
    bias = _na_bias(rpb)
    n_sub = ATTN_ROWS // SUB_ROWS
    m_rows = ATTN_ROWS * GRID_W
    grid_spec = pltpu.PrefetchScalarGridSpec(
        num_scalar_prefetch=2,
        grid=(nb, rows // ATTN_ROWS),
        in_specs=[
            pl.BlockSpec((1, m_rows, dn), lambda bi, i, w0, pat: (bi, i, 0)),
            pl.BlockSpec((1, s, dn), lambda bi, i, w0, pat: (bi, 0, 0)),
            pl.BlockSpec((1, s, dn), lambda bi, i, w0, pat: (bi, 0, 0)),
            pl.BlockSpec((1, nctx, dn), lambda bi, i, w0, pat: (bi, 0, 0)),
            pl.BlockSpec((1, nctx, dn), lambda bi, i, w0, pat: (bi, 0, 0)),
            pl.BlockSpec(bias.shape, lambda bi, i, w0, pat: (0, 0, 0), pipeline_mode=pl.Buffered(1)),
        ],
        out_specs=pl.BlockSpec((1, m_rows, dn), lambda bi, i, w0, pat: (bi, i, 0)),
    )
    return pl.pallas_call(
        functools.partial(_attn_kernel, banded=True, n_sub=n_sub, sub_q=SUB_Q),
        out_shape=jax.ShapeDtypeStruct((nb, s, dn), BF16),
        grid_spec=grid_spec,
        compiler_params=_cparams(("arbitrary", "arbitrary")),
        name="neighbourhood_attention",
    )(jnp.asarray(w0s), jnp.asarray(tiles), q, k, v, kc, vc, bias)


def _context_attention(q, kc, vc):
    nb, s, dn = q.shape
    spec = pl.BlockSpec((1, s, dn), lambda bi, i, w0, pat: (bi, 0, 0))
    grid_spec = pltpu.PrefetchScalarGridSpec(
        num_scalar_prefetch=2, grid=(nb, 1), in_specs=[spec] * 5, out_specs=spec)
    dummy = jnp.zeros((1,), jnp.int32)
    return pl.pallas_call(
        functools.partial(_attn_kernel, banded=False, n_sub=1, sub_q=s),
        out_shape=jax.ShapeDtypeStruct((nb, s, dn), BF16),
        grid_spec=grid_spec,
        compiler_params=_cparams(("arbitrary", "arbitrary")),
        name="context_attention",
    )(dummy, dummy, q, kc, vc, kc, vc)


def _store_rowmajor(ref, val, base=0):
    n, width = val.shape
    chunks = width // LANES
    for c in range(chunks):
        ref[pl.ds(base + c, n, stride=chunks), :] = val[:, c * LANES:(c + 1) * LANES]


def _load_rowmajor(ref, base, n, chunks):
    return jnp.concatenate([ref[pl.ds(base + c, n, stride=chunks), :] for c in range(chunks)], axis=1)


def _outproj_kernel(yab_ref, yc_ref, x_ref, w_ref, bo_ref, gt_ref, g_ref, b_ref, sh_ref, sc_ref,
                    wrh_ref, wrl_ref, xmid_ref, hm_ref, lg_ref, w_bf):
    @pl.when(_first_step())
    def _():
        w_bf[...] = w_ref[0].astype(BF16)

    half = yab_ref.shape[2]
    tm = x_ref.shape[1]
    chunks = x_ref.shape[2] // LANES
    n_part = max(1, tm // OUT_ROW_GROUP)
    rows_per = tm // n_part
    groups = [slice(part * rows_per, (part + 1) * rows_per) for part in range(n_part)]
    ys = [jnp.dot(yab_ref[0, rows, :], w_bf[0:half, :], preferred_element_type=F32)
          + jnp.dot(yc_ref[0, rows, :], w_bf[half:, :], preferred_element_type=F32) + bo_ref[...]
          for rows in groups]
    for part, (rows, y) in enumerate(zip(groups, ys)):
        xm = _ln(DEEPNORM_ALPHA * x_ref[0, rows, :] + (1.0 + gt_ref[0]) * y) * g_ref[...] + b_ref[...]
        xmid_ref[0, rows, :] = xm
        hm = _ln(xm) * (1.0 + sc_ref[0]) + sh_ref[0]
        _store_rowmajor(hm_ref, hm, base=part * rows_per * chunks)
        hm_hi = hm.astype(BF16)
        hm_lo = (hm - hm_hi.astype(F32)).astype(BF16)
        lg_ref[0, rows, :] = (jnp.dot(hm_hi, wrh_ref[...], preferred_element_type=F32)
                              + jnp.dot(hm_lo, wrh_ref[...], preferred_element_type=F32)
                              + jnp.dot(hm_hi, wrl_ref[...], preferred_element_type=F32))


def _outproj(yab, yc, x, w, layer, bo, gt, g, b, sh, sc, wr_hi, wr_lo, tm):
    nb, s, d = x.shape
    half = yab.shape[2]
    vec = pl.BlockSpec((1, d), lambda bi, i: (0, 0))
    tok = lambda width: pl.BlockSpec((1, tm, width), lambda bi, i: (bi, i, 0))
    n_i = s // tm
    return pl.pallas_call(
        _outproj_kernel,
        out_shape=[jax.ShapeDtypeStruct((nb, s, d), F32),
                   jax.ShapeDtypeStruct((nb * s * (d // LANES), LANES), F32),
                   jax.ShapeDtypeStruct((nb, s, LANES), F32)],
        grid=(nb, n_i),
        in_specs=[tok(half), tok(half), tok(d),
                  pl.BlockSpec((1, d, d), lambda bi, i: (layer, 0, 0), pipeline_mode=pl.Buffered(1)),
                  vec, _mod_spec(gt, d), vec, vec,
                  _mod_spec(sh, d), _mod_spec(sc, d), pl.BlockSpec((d, LANES), lambda bi, i: (0, 0)),
                  pl.BlockSpec((d, LANES), lambda bi, i: (0, 0))],
        out_specs=[tok(d), pl.BlockSpec((tm * (d // LANES), LANES), lambda bi, i: (bi * n_i + i, 0)),
                   tok(LANES)],
        scratch_shapes=[pltpu.VMEM((d, d), BF16)],
        compiler_params=_cparams(("arbitrary", "arbitrary")),
        name="outproj_postnorm",
    )(yab, yc, x, w, bo, gt[0], g, b, sh[0], sc[0], wr_hi, wr_lo)


CUM_CHUNK = 256
F32_EXP_BIAS = 127
F32_MANT_BITS = 23


def _prefix_count(mask_f32, tri):
    rows, n = mask_f32.shape
    tc = min(CUM_CHUNK, n)
    base = jnp.zeros((rows, 1), F32)
    parts = []
    for ci in range(n // tc):
        blk = mask_f32[:, ci * tc:(ci + 1) * tc]
        parts.append(jnp.dot(blk.astype(BF16), tri[:tc, :tc], preferred_element_type=F32) + base)
        base = base + jnp.sum(blk, axis=-1, keepdims=True)
    return jnp.concatenate(parts, axis=-1)


def _pow2(k):
    return pltpu.bitcast((k + F32_EXP_BIAS) << F32_MANT_BITS, F32)


def _route_kernel(lg_ref, slot_c_ref, gate_t_ref, *, cap, slot_stride):
    nb = lg_ref.shape[0]
    assert nb * N_EXPERTS == LANES
    rows = []
    for b in range(nb):
        lg = lg_ref[b]
        lane = lax.broadcasted_iota(jnp.int32, lg.shape, 1)
        lgm = jnp.where(lane < N_EXPERTS, lg, NEG_INF)
        ex = jnp.exp(lgm - jnp.max(lgm, axis=-1, keepdims=True))
        aff = ex / jnp.sum(ex, axis=-1, keepdims=True)
        rows.append(aff.T[0:N_EXPERTS, :])
    a = jnp.concatenate(rows, axis=0)
    capf = float(cap)

    def enough(t):
        return jnp.sum((a >= t).astype(F32), axis=-1, keepdims=True) >= capf

    def exp_step(_, carry):
        lo, hi = carry
        mid = lo + ((hi - lo + 1) >> 1)
        ok = enough(_pow2(mid))
        return jnp.where(ok, mid, lo), jnp.where(ok, hi, mid - 1)

    k_lo = jnp.full((LANES, 1), -F32_EXP_BIAS, jnp.int32)
    k_hi = jnp.zeros((LANES, 1), jnp.int32)
    k_lo, _ = lax.fori_loop(0, 7, exp_step, (k_lo, k_hi))
    base = _pow2(k_lo)

    def mant_step(_, carry):
        t, step = carry
        step = step * 0.5
        cand = t + step
        return jnp.where(enough(cand), cand, t), step

    thr, _ = lax.fori_loop(0, F32_MANT_BITS, mant_step, (base, base))

    r_i = lax.broadcasted_iota(jnp.int32, (CUM_CHUNK, CUM_CHUNK), 0)
    c_i = lax.broadcasted_iota(jnp.int32, (CUM_CHUNK, CUM_CHUNK), 1)
    tri = (r_i < c_i).astype(BF16)
    gt = (a > thr).astype(F32)
    eq = (a == thr).astype(F32)
    need = capf - jnp.sum(gt, axis=-1, keepdims=True)
    sel = gt + eq * (_prefix_count(eq, tri) < need).astype(F32)
    pos = _prefix_count(sel, tri)
    sample = lax.broadcasted_iota(jnp.int32, (LANES, 1), 0) >> (N_EXPERTS.bit_length() - 1)
    slot = jnp.where(sel > 0.0, pos + (sample * slot_stride).astype(F32), -1.0)
    for b in range(nb):
        lo = b * N_EXPERTS
        gate_t_ref[b] = a[lo:lo + N_EXPERTS, :]
        rolled = slot if b == 0 else jnp.concatenate([slot[lo:, :], slot[:lo, :]], axis=0)
        slot_c_ref[b] = rolled.T


def _route(logits, cap, slot_stride):
    nb, s, _ = logits.shape
    whole = lambda shape: pl.BlockSpec(shape, lambda i: (0,) * len(shape))
    return pl.pallas_call(
        functools.partial(_route_kernel, cap=cap, slot_stride=slot_stride),
        out_shape=[jax.ShapeDtypeStruct((nb, s, LANES), F32), jax.ShapeDtypeStruct((nb, N_EXPERTS, s), F32)],
        grid=(1,),
        in_specs=[whole((nb, s, LANES))],
        out_specs=[whole((nb, s, LANES)), whole((nb, N_EXPERTS, s))],
        compiler_params=_cparams(("arbitrary",)),
        name="route",
    )(logits)


TOK_SPLIT = 64


def _slot_list_kernel(slot_ref, gate_ref, idx_ref, g_ref, *, n_slots, tok_stride, merge):
    s = slot_ref.shape[1]
    b = pl.program_id(0)
    slot_id = lax.broadcasted_iota(jnp.int32, (s, n_slots), 1).astype(F32)
    tok = lax.broadcasted_iota(jnp.int32, (1, s), 1) + b * tok_stride
    tok_hi = (tok >> (TOK_SPLIT.bit_length() - 1)).astype(F32)
    tok_lo = (tok & (TOK_SPLIT - 1)).astype(F32)
    zeros = jnp.zeros((SUBLANES - 5, s), F32)
    idx_rows, g_rows = [], []
    for e in range(N_EXPERTS):
        hit = jnp.where(slot_ref[0, :, e:e + 1] == slot_id, 1.0, 0.0).astype(BF16)
        g0 = gate_ref[0, e:e + 1, :]
        g_hi = g0.astype(BF16).astype(F32)
        g_mid = (g0 - g_hi).astype(BF16).astype(F32)
        g_lo = g0 - g_hi - g_mid
        lhs = jnp.concatenate([tok_hi, tok_lo, g_hi, g_mid, g_lo, zeros], axis=0).astype(BF16)
        out = jnp.dot(lhs, hit, preferred_element_type=F32)
        idx_rows.append(out[0:1] * float(TOK_SPLIT) + out[1:2])
        g_rows.append(out[2:3] + out[3:4] + out[4:5])
    idx = jnp.concatenate(idx_rows, axis=0).astype(jnp.int32) * SUBLANES
    g = jnp.concatenate(g_rows + [jnp.zeros((LANES - N_EXPERTS, n_slots), F32)], axis=0)
    g_t = g.T
    g_cols = [jnp.broadcast_to(g_t[:, e:e + 1], (n_slots, LANES)) for e in range(N_EXPERTS)]
    if merge:
        @pl.when(b == 0)
        def _():
            idx_ref[0] = idx
            for e in range(N_EXPERTS):
                g_ref[e] = g_cols[e]

        @pl.when(b > 0)
        def _():
            idx_ref[0] = idx_ref[0] + idx
            for e in range(N_EXPERTS):
                g_ref[e] = g_ref[e] + g_cols[e]
    else:
        idx_ref[0] = idx
        for e in range(N_EXPERTS):
            g_ref[e] = g_cols[e]


def _slot_lists(slot_c, gate_t, n_slots, merge):
    nb, s, _ = slot_c.shape
    nbo = 1 if merge else nb
    idx_map = (lambda bi: (0, 0, 0)) if merge else (lambda bi: (bi, 0, 0))
    g_map = (lambda bi: (0, 0, 0)) if merge else (lambda bi: (0, bi, 0))
    return pl.pallas_call(
        functools.partial(_slot_list_kernel, n_slots=n_slots, tok_stride=s if merge else 0, merge=merge),
        out_shape=[jax.ShapeDtypeStruct((nbo, N_EXPERTS, n_slots), jnp.int32),
                   jax.ShapeDtypeStruct((N_EXPERTS, nbo * n_slots, LANES), F32)],
        grid=(nb,),
        in_specs=[pl.BlockSpec((1, s, LANES), lambda bi: (bi, 0, 0)),
                  pl.BlockSpec((1, N_EXPERTS, s), lambda bi: (bi, 0, 0))],
        out_specs=[pl.BlockSpec((1, N_EXPERTS, n_slots), idx_map),
                   pl.BlockSpec((N_EXPERTS, n_slots, LANES), g_map)],
        compiler_params=_cparams(("arbitrary",)),
        name="slot_lists",
    )(slot_c, gate_t)


ROW_UNROLL = 16


def _dispatch_kernel(idx_ref, hm_ref, x_ref, rows_scr, *, n_slots):
    b = pl.program_id(0)
    chunks = x_ref.shape[2] // LANES
    for e in range(N_EXPERTS):
        base = (b * N_EXPERTS + e) * n_slots

        def move(i, carry):
            s0 = i * ROW_UNROLL
            first = base + s0
            tiles = [hm_ref[pl.ds(pl.multiple_of(idx_ref[first + u], chunks), chunks), :]
                     for u in range(ROW_UNROLL)]
            dst = pl.multiple_of(s0 * chunks, ROW_UNROLL * chunks)
            rows_scr[pl.ds(dst, ROW_UNROLL * chunks), :] = jnp.concatenate(tiles, axis=0)
            return carry

        lax.fori_loop(0, n_slots // ROW_UNROLL, move, 0)
        x_ref[e] = _load_rowmajor(rows_scr, 0, n_slots, chunks).astype(x_ref.dtype)


def _dispatch(idx, hm_rm, nb, s, d, n_slots):
    chunks = d // LANES
    assert chunks == SUBLANES and n_slots % ROW_UNROLL == 0
    grid_spec = pltpu.PrefetchScalarGridSpec(
        num_scalar_prefetch=1,
        grid=(nb,),
        in_specs=[pl.BlockSpec((s * chunks, LANES), lambda bi, idx_r: (bi, 0))],
        out_specs=pl.BlockSpec((N_EXPERTS, n_slots, d), lambda bi, idx_r: (0, bi, 0)),
        scratch_shapes=[pltpu.VMEM((n_slots * chunks, LANES), F32)],
    )
    return pl.pallas_call(
        functools.partial(_dispatch_kernel, n_slots=n_slots),
        out_shape=jax.ShapeDtypeStruct((N_EXPERTS, nb * n_slots, d), BF16),
        grid_spec=grid_spec,
        compiler_params=_cparams(("arbitrary",)),
        name="dispatch",
    )(idx, hm_rm)


def _expert_kernel(*refs, n_main, has_extra):
    if has_extra:
        x_ref, g_ref, xx_ref, gg_ref, wg_ref, wu_ref, wd_ref, y_ref, yy_ref, wg_bf, wu_bf, wd_bf = refs
    else:
        x_ref, g_ref, wg_ref, wu_ref, wd_ref, y_ref, wg_bf, wu_bf, wd_bf = refs
    i = pl.program_id(1)

    @pl.when(i == 0)
    def _():
        wg_bf[...] = wg_ref[0, 0].astype(BF16)
        wu_bf[...] = wu_ref[0, 0].astype(BF16)
        wd_bf[...] = wd_ref[0, 0].astype(BF16)

    def ffn(x_in, gate_in, out_ref):
        rows_all = x_in.shape[1]
        chunks = x_in.shape[2] // LANES
        n_part = max(1, rows_all // ROW_GROUP)
        rows_per = rows_all // n_part
        groups = [slice(p * rows_per, (p + 1) * rows_per) for p in range(n_part)]
        ups = []
        for rows in groups:
            x_e = x_in[0, rows, :]
            ups.append((jnp.dot(x_e, wg_bf[...], preferred_element_type=F32),
                        jnp.dot(x_e, wu_bf[...], preferred_element_type=F32)))
        for p, (rows, (a, u)) in enumerate(zip(groups, ups)):
            h = (a * _sigmoid(a) * u).astype(BF16)
            y = jnp.dot(h, wd_bf[...], preferred_element_type=F32) * gate_in[0, rows, 0:1]
            _store_rowmajor(out_ref, y, base=p * rows_per * chunks)

    if has_extra:
        @pl.when(i < n_main)
        def _():
            ffn(x_ref, g_ref, y_ref.at[0])

        @pl.when(i == n_main)
        def _():
            ffn(xx_ref, gg_ref, yy_ref.at[0])
    else:
        ffn(x_ref, g_ref, y_ref.at[0])


def _experts(x_e, g_e, x_extra, g_extra, wg, wu, wd, layer, tm):
    _, rows, d = x_e.shape
    f = wg.shape[3]
    chunks = d // LANES
    n_main = rows // tm
    has_extra = x_extra is not None
    main_map = lambda e, i: (e, jnp.minimum(i, n_main - 1), 0)
    w_spec = lambda shape: pl.BlockSpec((1, 1) + shape, lambda e, i: (layer, e, 0, 0))
    in_specs = [pl.BlockSpec((1, tm, d), main_map), pl.BlockSpec((1, tm, LANES), main_map)]
    out_specs = [pl.BlockSpec((1, tm * chunks, LANES), main_map)]
    out_shape = [jax.ShapeDtypeStruct((N_EXPERTS, rows * chunks, LANES), F32)]
    args = [x_e, g_e]
    if has_extra:
        rows2 = x_extra.shape[1]
        in_specs += [pl.BlockSpec((1, rows2, d), lambda e, i: (e, 0, 0)),
                     pl.BlockSpec((1, rows2, LANES), lambda e, i: (e, 0, 0))]
        out_specs.append(pl.BlockSpec((1, rows2 * chunks, LANES), lambda e, i: (e, 0, 0)))
        out_shape.append(jax.ShapeDtypeStruct((N_EXPERTS, rows2 * chunks, LANES), F32))
        args += [x_extra, g_extra]
    outs = pl.pallas_call(
        functools.partial(_expert_kernel, n_main=n_main, has_extra=has_extra),
        out_shape=out_shape,
        grid=(N_EXPERTS, n_main + int(has_extra)),
        in_specs=in_specs + [w_spec((d, f)), w_spec((d, f)), w_spec((f, d))],
        out_specs=out_specs,
        scratch_shapes=[pltpu.VMEM((d, f), BF16), pltpu.VMEM((d, f), BF16), pltpu.VMEM((f, d), BF16)],
        compiler_params=_cparams(("arbitrary", "arbitrary")),
        name="experts",
    )(*args, wg, wu, wd)
    return outs if has_extra else (outs[0], None)


COMBINE_EXPERTS = 4
NORM_CHUNK = 256


def _combine_kernel(idx_ref, y_ref, x_ref, gt_ref, g_ref, b_ref, o_ref, acc, *, n_slots):
    b = pl.program_id(0)
    j = pl.program_id(1)
    s, d = x_ref.shape[1], x_ref.shape[2]
    chunks = d // LANES

    @pl.when(j == 0)
    def _():
        acc[...] = jnp.zeros(acc.shape, F32)

    for eg in range(COMBINE_EXPERTS):
        base = (b * N_EXPERTS + j * COMBINE_EXPERTS + eg) * n_slots

        def add(i, carry):
            s0 = i * ROW_UNROLL
            first = base + s0
            src = pl.multiple_of(s0 * chunks, ROW_UNROLL * chunks)
            y_rows = y_ref[eg, pl.ds(src, ROW_UNROLL * chunks), :]
            new = []
            for u in range(ROW_UNROLL):
                dst = pl.multiple_of(idx_ref[first + u], chunks)
                new.append((dst, acc[pl.ds(dst, chunks), :] + y_rows[u * chunks:(u + 1) * chunks, :]))
            for dst, val in new:
                acc[pl.ds(dst, chunks), :] = val
            return carry

        lax.fori_loop(0, n_slots // ROW_UNROLL, add, 0)

    @pl.when(j == pl.num_programs(1) - 1)
    def _():
        for ci in range(s // NORM_CHUNK):
            rows = slice(ci * NORM_CHUNK, (ci + 1) * NORM_CHUNK)
            ym = _load_rowmajor(acc, ci * NORM_CHUNK * chunks, NORM_CHUNK, chunks)
            z = DEEPNORM_ALPHA * x_ref[0, rows, :] + (1.0 + gt_ref[0]) * ym
            o_ref[0, rows, :] = _ln(z) * g_ref[...] + b_ref[...]


def _combine(idx, y_rm, x_mid, gt, g, b, n_slots):
    nb, s, d = x_mid.shape
    chunks = d // LANES
    assert chunks == SUBLANES and N_EXPERTS % COMBINE_EXPERTS == 0 and s % NORM_CHUNK == 0
    vec = pl.BlockSpec((1, d), lambda bi, j, i_r: (0, 0))
    grid_spec = pltpu.PrefetchScalarGridSpec(
        num_scalar_prefetch=1,
        grid=(nb, N_EXPERTS // COMBINE_EXPERTS),
        in_specs=[
            pl.BlockSpec((COMBINE_EXPERTS, n_slots * chunks, LANES), lambda bi, j, i_r: (j, bi, 0)),
            pl.BlockSpec((1, s, d), lambda bi, j, i_r: (bi, 0, 0)),
            _mod_spec(gt, d), vec, vec,
        ],
        out_specs=pl.BlockSpec((1, s, d), lambda bi, j, i_r: (bi, 0, 0)),
        scratch_shapes=[pltpu.VMEM((s * chunks, LANES), F32)],
    )
    return pl.pallas_call(
        functools.partial(_combine_kernel, n_slots=n_slots),
        out_shape=jax.ShapeDtypeStruct((nb, s, d), F32),
        grid_spec=grid_spec,
        compiler_params=_cparams(("arbitrary", "arbitrary")),
        name="combine_postnorm",
    )(idx, y_rm, x_mid, gt[0], g, b)


def kernel(x, c, ctx, c_ctx, w_mod, b_mod, w_in, b_in, w_short, w_conf_dw, b_conf_dw, g_conf_ln, b_conf_ln,
           na_rpb, w_out, b_out, g_post1, b_post1, w_router, w_gate, w_up, w_down, g_post2, b_post2):
    bsz, seq, d = x.shape
    nctx = ctx.shape[1]
    cap = EC_CAPACITY_FACTOR * seq // N_EXPERTS
    cap_ctx = EC_CAPACITY_FACTOR * nctx // N_EXPERTS
    q_scale = NA_HEAD_DIM ** -0.5 * LOG2E

    cond = jnp.concatenate([c, c_ctx[None, :], jnp.zeros((MOD_ROWS - bsz - 1, d), F32)], axis=0)
    mods = _modulation(cond, w_mod, b_mod)

    lat_splits = ((3 * D_CONV, 1.0), (2 * D_CONF, 1.0), (D_NA, q_scale), (D_NA, 1.0), (D_NA, 1.0))
    lat_dtypes = (F32, F32, BF16, BF16, BF16)
    kv_splits = ((D_NA, 1.0), (D_NA, 1.0))
    b_in3 = b_in[:, None, :]

    xc = ctx
    for l in range(DEPTH):
        last = l == DEPTH - 1
        m_lat = [(mods, (l * N_MOD + k) * MOD_ROWS, True) for k in range(N_MOD)]
        m_ctx = [(mods, (l * N_MOD + k) * MOD_ROWS + bsz, False) for k in range(N_MOD)]
        b_out_l = b_out[l][None, :]
        wr = jnp.pad(w_router[l], ((0, 0), (0, LANES - N_EXPERTS)))
        wr_hi = wr.astype(BF16)
        wr_lo = (wr - wr_hi.astype(F32)).astype(BF16)
        g1, b1 = g_post1[l][None, :], b_post1[l][None, :]
        g2, b2 = g_post2[l][None, :], b_post2[l][None, :]
        conv_w = (w_short[l], w_conf_dw[l], b_conf_dw[l], g_conf_ln[l], b_conf_ln[l])

        if last:
            k_c, v_c = _inproj(xc, m_ctx[0], m_ctx[1], w_in, b_in3, l, OFF_K, kv_splits, (BF16, BF16), tm=nctx)
        else:
            uac, ubc, q_c, k_c, v_c = _inproj(xc, m_ctx[0], m_ctx[1], w_in, b_in3, l, 0,
                                              lat_splits, lat_dtypes, tm=nctx)

        ua, ub, q, k, v = _inproj(x, m_lat[0], m_lat[1], w_in, b_in3, l, 0, lat_splits, lat_dtypes, tm=512)
        yab = _conv_mixers(ua, ub, *conv_w)
        yc = _neighbourhood_attention(q, k, v, k_c, v_c, na_rpb[l])
        x_mid, hm, logits = _outproj(yab, yc, x, w_out, l, b_out_l, m_lat[2], g1, b1, m_lat[3], m_lat[4],
                                     wr_hi, wr_lo, tm=512)

        idx, gates = _slot_lists(*_route(logits, cap, 0), cap, merge=False)
        idx = idx.reshape(-1)
        x_e = _dispatch(idx, hm, bsz, seq, d, cap)

        x_ec = gates_c = None
        if not last:
            yabc = _conv_mixers(uac, ubc, *conv_w)
            ycc = _context_attention(q_c, k_c, v_c)
            xc_mid, hmc, logits_c = _outproj(yabc, ycc, xc, w_out, l, b_out_l, m_ctx[2], g1, b1,
                                             m_ctx[3], m_ctx[4], wr_hi, wr_lo, tm=nctx)
            n_c = bsz * cap_ctx
            idx_c, gates_c = _slot_lists(*_route(logits_c, cap_ctx, cap_ctx), n_c, merge=True)
            idx_c = idx_c.reshape(-1)
            x_ec = _dispatch(idx_c, hmc, 1, bsz * nctx, d, n_c)

        y_e, y_ec = _experts(x_e, gates, x_ec, gates_c, w_gate, w_up, w_down, l, tm=512)
        x = _combine(idx, y_e, x_mid, m_lat[5], g2, b2, cap)
        if not last:
            xc = _combine(idx_c, y_ec, xc_mid.reshape(1, bsz * nctx, d),
                          m_ctx[5], g2, b2, n_c).reshape(bsz, nctx, d)
    return x
```

```python
import functools
import math

import numpy as np
import jax
import jax.numpy as jnp
from jax import lax
from jax.experimental import pallas as pl
from jax.experimental.pallas import tpu as pltpu

F32 = jnp.float32
BF16 = jnp.bfloat16

D_MODEL = 1024
DEPTH = 2
GRID_W = 64
D_CONV = D_MODEL // 4
D_CONF = D_MODEL // 4
NA_HEAD_DIM = 64
D_NA = D_MODEL - D_CONV - D_CONF
N_NA_HEADS = D_NA // NA_HEAD_DIM
SHORT_CONV_W = 3
CONF_CONV_W = 31
NA_WIN_ROWS_MAX = 8
NA_WIN_COLS = 16
N_EXPERTS = 16
EC_CAPACITY_FACTOR = 2
D_EXPERT = 1024
LN_EPS = 1e-5
DEEPNORM_ALPHA = (2.0 * DEPTH) ** 0.25
NEG_INF = -1e30
LOG2E = math.log2(math.e)

OFF_A = 0
OFF_B = OFF_A + 3 * D_CONV
OFF_Q = OFF_B + 2 * D_CONF
OFF_K = OFF_Q + D_NA
OFF_V = OFF_K + D_NA
D_IN = OFF_V + D_NA

LANES = 128
SUBLANES = 8
MOD_ROWS = 16
VMEM_LIMIT = 56 * 1024 * 1024
ATTN_ROWS = 8
N_PAIRS = D_NA // LANES
ROW_GROUP = 256
OUT_ROW_GROUP = 128
HI = lax.Precision.HIGHEST


def _cparams(sem):
    return pltpu.CompilerParams(dimension_semantics=sem, vmem_limit_bytes=VMEM_LIMIT)


def _ln(x):
    mu = jnp.mean(x, axis=-1, keepdims=True)
    xc = x - mu
    var = jnp.mean(xc * xc, axis=-1, keepdims=True)
    return xc * lax.rsqrt(var + LN_EPS)


def _sigmoid(x):
    return 1.0 / (1.0 + jnp.exp(-x))


def _mod_kernel(cond_ref, w_ref, b_ref, o_ref):
    s = cond_ref[...]
    s = s * _sigmoid(s)
    o_ref[0] = jnp.dot(s, w_ref[0], preferred_element_type=F32, precision=HI) + b_ref[0]


N_MOD = 6


def _modulation(cond, w_mod, b_mod):
    n_l, d, n = w_mod.shape
    out = pl.pallas_call(
        _mod_kernel,
        out_shape=jax.ShapeDtypeStruct((n_l * N_MOD, MOD_ROWS, d), F32),
        grid=(n_l, N_MOD),
        in_specs=[
            pl.BlockSpec((MOD_ROWS, d), lambda l, k: (0, 0)),
            pl.BlockSpec((1, d, d), lambda l, k: (l, 0, k)),
            pl.BlockSpec((1, 1, d), lambda l, k: (l, 0, k)),
        ],
        out_specs=pl.BlockSpec((1, MOD_ROWS, d), lambda l, k: (l * N_MOD + k, 0, 0)),
        compiler_params=_cparams(("arbitrary", "arbitrary")),
        name="modulation",
    )(cond, w_mod, b_mod.reshape(n_l, 1, n))
    return out.reshape(n_l * N_MOD * MOD_ROWS, 1, d)


def _mod_spec(mod, d):
    _, row0, per_sample = mod
    if per_sample:
        return pl.BlockSpec((1, 1, d), lambda bi, *_: (row0 + bi, 0, 0))
    return pl.BlockSpec((1, 1, d), lambda bi, *_: (row0, 0, 0))


def _first_step():
    return (pl.program_id(0) == 0) & (pl.program_id(1) == 0)


def _inproj_kernel(x_ref, sh_ref, sc_ref, w_ref, b_ref, *rest, splits, col0):
    *o_refs, w_bf = rest

    @pl.when(_first_step())
    def _():
        w_bf[...] = w_ref[0].astype(BF16)

    tm = x_ref.shape[1]
    n_part = max(1, tm // ROW_GROUP)
    rows_per = tm // n_part
    groups = [slice(part_i * rows_per, (part_i + 1) * rows_per) for part_i in range(n_part)]
    hs = [(_ln(x_ref[0, rows, :]) * (1.0 + sc_ref[0]) + sh_ref[0]).astype(BF16) for rows in groups]
    for rows, h in zip(groups, hs):
        u = jnp.dot(h, w_bf[:, col0:], preferred_element_type=F32) + b_ref[0, :, col0:]
        off = 0
        for o_ref, (width, scale) in zip(o_refs, splits):
            part = u[:, off:off + width]
            if scale != 1.0:
                part = part * scale
            o_ref[0, rows, :] = part.astype(o_ref.dtype)
            off += width


def _inproj(x, sh, sc, w, b, layer, col0, splits, dtypes, tm):
    nb, s, d = x.shape
    n = w.shape[2]
    return pl.pallas_call(
        functools.partial(_inproj_kernel, splits=splits, col0=col0),
        out_shape=[jax.ShapeDtypeStruct((nb, s, wd), dt) for (wd, _), dt in zip(splits, dtypes)],
        grid=(nb, s // tm),
        in_specs=[
            pl.BlockSpec((1, tm, d), lambda bi, i: (bi, i, 0)),
            _mod_spec(sh, d),
            _mod_spec(sc, d),
            pl.BlockSpec((1, d, n), lambda bi, i: (layer, 0, 0), pipeline_mode=pl.Buffered(1)),
            pl.BlockSpec((1, 1, n), lambda bi, i: (layer, 0, 0)),
        ],
        out_specs=[pl.BlockSpec((1, tm, wd), lambda bi, i: (bi, i, 0)) for wd, _ in splits],
        scratch_shapes=[pltpu.VMEM((d, n), BF16)],
        compiler_params=_cparams(("arbitrary", "arbitrary")),
        name="inproj",
    )(x, sh[0], sc[0], w, b)


CONV_CHUNK = 128
Z_PAD = 8
H_PAD = 16


def _conv_kernel(ua_ref, ub_ref, ws_ref, wd_ref, bd_ref, g_ref, b_ref, o_ref, z_scr, h_scr, *, seq):
    c = D_CONV
    z_scr[0:Z_PAD, :] = jnp.zeros((Z_PAD, c), F32)
    z_scr[Z_PAD + seq:2 * Z_PAD + seq, :] = jnp.zeros((Z_PAD, c), F32)
    h_scr[0, 0:H_PAD, :] = jnp.zeros((H_PAD, c), F32)
    h_scr[0, H_PAD + seq:2 * H_PAD + seq, :] = jnp.zeros((H_PAD, c), F32)
    z_scr[Z_PAD:Z_PAD + seq, :] = ua_ref[0, :, c:2 * c] * ua_ref[0, :, 2 * c:3 * c]
    h_scr[0, H_PAD:H_PAD + seq, :] = ub_ref[0, :, 0:c] * _sigmoid(ub_ref[0, :, c:2 * c])
    n_rows = seq + 2 * H_PAD
    h_all = h_scr[0]
    for r in range(1, SUBLANES):
        h_scr[r] = pltpu.roll(h_all, n_rows - r, axis=0)
    tc = min(CONV_CHUNK, seq)
    for ci in range(seq // tc):
        t0 = ci * tc
        acc = ws_ref[0:1, :] * z_scr[t0 + Z_PAD - 1:t0 + Z_PAD - 1 + tc, :]
        for j in range(1, SHORT_CONV_W):
            s0 = t0 + Z_PAD - 1 + j
            acc = acc + ws_ref[j:j + 1, :] * z_scr[s0:s0 + tc, :]
        ya = ua_ref[0, t0:t0 + tc, 0:c] * acc
        hb = bd_ref[...]
        for j in range(CONF_CONV_W):
            s0 = t0 + H_PAD - CONF_CONV_W // 2 + j
            a0 = s0 - s0 % SUBLANES
            hb = hb + wd_ref[j:j + 1, :] * h_scr[s0 % SUBLANES, a0:a0 + tc, :]
        hn = _ln(hb) * g_ref[...] + b_ref[...]
        yb = hn * _sigmoid(hn)
        o_ref[0, t0:t0 + tc, 0:c] = ya.astype(o_ref.dtype)
        o_ref[0, t0:t0 + tc, c:2 * c] = yb.astype(o_ref.dtype)


def _conv_mixers(ua, ub, w_short, w_dw, b_dw, g_ln, b_ln):
    nb, s, _ = ua.shape
    c = D_CONV
    full = lambda shape: pl.BlockSpec(shape, lambda bi: (0,) * len(shape))
    return pl.pallas_call(
        functools.partial(_conv_kernel, seq=s),
        out_shape=jax.ShapeDtypeStruct((nb, s, 2 * c), BF16),
        grid=(nb,),
        in_specs=[
            pl.BlockSpec((1, s, 3 * c), lambda bi: (bi, 0, 0)),
            pl.BlockSpec((1, s, 2 * c), lambda bi: (bi, 0, 0)),
            full((SHORT_CONV_W, c)), full((CONF_CONV_W, c)), full((1, c)), full((1, c)), full((1, c)),
        ],
        out_specs=pl.BlockSpec((1, s, 2 * c), lambda bi: (bi, 0, 0)),
        scratch_shapes=[pltpu.VMEM((s + 2 * Z_PAD, c), F32), pltpu.VMEM((SUBLANES, s + 2 * H_PAD, c), F32)],
        compiler_params=_cparams(("arbitrary",)),
        name="conv_mixers",
    )(ua, ub, w_short, w_dw, b_dw.reshape(1, c), g_ln.reshape(1, c), b_ln.reshape(1, c))


SUB_ROWS = 2
WIN_ROWS = SUB_ROWS + NA_WIN_ROWS_MAX - 1
BAND_KEYS = WIN_ROWS * GRID_W
SUB_Q = SUB_ROWS * GRID_W
N_DROW = 2 * NA_WIN_ROWS_MAX - 1
N_DCOL = 2 * NA_WIN_COLS - 1


def _bias_kernel(rpb_ref, o_ref):
    n_rows, n_cols = o_ref.shape
    col = lax.broadcasted_iota(jnp.int32, (LANES, n_cols), 1)
    qc = col >> 6
    kc = col & (GRID_W - 1)
    d_col = jnp.clip(kc - qc + (NA_WIN_COLS - 1), 0, N_DCOL - 1)
    onehot = (lax.broadcasted_iota(jnp.int32, (LANES, n_cols), 0) == d_col).astype(F32)
    vals = jnp.dot(rpb_ref[...], onehot, preferred_element_type=F32, precision=HI)
    col_r = lax.broadcasted_iota(jnp.int32, (n_rows, n_cols), 1)
    qc_r = col_r >> 6
    kc_r = col_r & (GRID_W - 1)
    c0 = jnp.clip(qc_r - NA_WIN_COLS // 2, 0, GRID_W - NA_WIN_COLS)
    inside = (kc_r >= c0) & (kc_r < c0 + NA_WIN_COLS)
    o_ref[...] = jnp.where(inside, vals * LOG2E, NEG_INF)


N_SLABS = -(-WIN_ROWS // 2)
TILES_PER_KIND = N_NA_HEADS * N_DROW
MASKED_TILE = 3 * TILES_PER_KIND


def _na_plan(rows):
    wr = min(NA_WIN_ROWS_MAX, rows)
    assert wr == NA_WIN_ROWS_MAX and rows % SUB_ROWS == 0 and rows >= WIN_ROWS
    row_start = np.clip(np.arange(rows) - wr // 2, 0, rows - wr)
    w0s, tiles = [], []
    for r0 in range(0, rows, SUB_ROWS):
        w0 = int(np.clip(r0 - wr // 2, 0, rows - WIN_ROWS))
        for iq in range(SUB_ROWS):
            r = r0 + iq
            assert row_start[r] >= w0 and row_start[r] + wr <= w0 + WIN_ROWS
            ok = lambda w: w < WIN_ROWS and row_start[r] <= w0 + w < row_start[r] + wr
            d_row = lambda w: w0 + w - r + NA_WIN_ROWS_MAX - 1
            for j in range(N_SLABS):
                lo, hi = ok(2 * j), ok(2 * j + 1)
                if lo and hi:
                    tiles.append(d_row(2 * j))
                elif lo:
                    tiles.append(TILES_PER_KIND + d_row(2 * j))
                elif hi:
                    tiles.append(2 * TILES_PER_KIND + d_row(2 * j + 1))
                else:
                    tiles.append(-1)
        w0s.append(w0)
    return np.array(w0s, np.int32), np.array(tiles, np.int32)


def _na_bias(rpb):
    assert GRID_W == 64 and 2 * GRID_W == LANES and TILES_PER_KIND <= LANES and N_DCOL <= LANES
    rpb2 = jnp.pad(rpb.astype(F32).reshape(TILES_PER_KIND, N_DCOL),
                   ((0, LANES - TILES_PER_KIND), (0, LANES - N_DCOL)))
    table = pl.pallas_call(
        _bias_kernel,
        out_shape=jax.ShapeDtypeStruct((LANES, GRID_W * GRID_W), F32),
        name="na_bias",
    )(rpb2)
    table = table[:TILES_PER_KIND].reshape(N_NA_HEADS, N_DROW, GRID_W, GRID_W)
    masked = jnp.full_like(table, NEG_INF)
    nxt = jnp.concatenate([table[:, 1:], masked[:, :1]], axis=1)
    kinds = [jnp.concatenate(pair, axis=-1).reshape(TILES_PER_KIND, GRID_W, LANES)
             for pair in ((table, nxt), (table, masked), (masked, table))]
    return jnp.concatenate(kinds + [jnp.full((1, GRID_W, LANES), NEG_INF, F32)], axis=0)


def _lane_reduce(xs, combine, reduce, neutral):
    chunks = []
    for x in xs:
        rows, n = x.shape
        n_full = n // LANES
        chunks += [x[:, j * LANES:(j + 1) * LANES] for j in range(n_full)]
        if n % LANES:
            fill = jnp.full((rows, LANES - n % LANES), neutral, x.dtype)
            chunks.append(jnp.concatenate([x[:, n_full * LANES:], fill], axis=1))
    return reduce(functools.reduce(combine, chunks), axis=-1, keepdims=True)


def _attn_kernel(w0_ref, tile_ref, q_ref, k_ref, v_ref, kc_ref, vc_ref, *rest, banded, n_sub, sub_q):
    if banded:
        bias_ref, o_ref = rest
    else:
        (o_ref,) = rest
    lane = lax.broadcasted_iota(jnp.int32, (sub_q, LANES), 1)
    first = lane < NA_HEAD_DIM
    nt = (((1,), (1,)), ((), ()))
    stages = [(si, p) for si in range(n_sub) for p in range(N_PAIRS)]

    def window(si):
        blk = pl.program_id(1) * n_sub + si
        return blk, pl.multiple_of(w0_ref[blk] * GRID_W, GRID_W)

    def bias(blk, head):
        row_blocks = []
        for iq in range(SUB_ROWS):
            slabs = []
            for j in range(N_SLABS):
                t = tile_ref[(blk * SUB_ROWS + iq) * N_SLABS + j]
                tile = bias_ref[jnp.where(t < 0, MASKED_TILE, t + head * N_DROW)]
                width = min(LANES, BAND_KEYS - j * LANES)
                slabs.append(tile[:, :width])
            row_blocks.append(jnp.concatenate(slabs, axis=1))
        return jnp.concatenate(row_blocks, axis=0)

    def scores(si, p):
        cols = slice(p * LANES, (p + 1) * LANES)
        q_p = q_ref[0, si * sub_q:(si + 1) * sub_q, cols]
        zero = jnp.zeros_like(q_p)
        qq = jnp.concatenate([jnp.where(first, q_p, zero), jnp.where(first, zero, q_p)], axis=0)
        parts = [lax.dot_general(qq, kc_ref[0, :, cols], nt, preferred_element_type=F32)]
        if banded:
            blk, start = window(si)
            both = jnp.concatenate([bias(blk, 2 * p), bias(blk, 2 * p + 1)], axis=0)
            parts.append(lax.dot_general(qq, k_ref[0, pl.ds(start, BAND_KEYS), cols], nt,
                                         preferred_element_type=F32) + both)
        return parts

    def finish(si, p, parts):
        cols = slice(p * LANES, (p + 1) * LANES)
        m = _lane_reduce(parts, jnp.maximum, jnp.max, NEG_INF)
        es = [jnp.exp2(s - m) for s in parts]
        den = _lane_reduce(es, jnp.add, jnp.sum, 0.0)
        o = jnp.dot(es[0].astype(BF16), vc_ref[0, :, cols], preferred_element_type=F32)
        if banded:
            _, start = window(si)
            o = o + jnp.dot(es[1].astype(BF16), v_ref[0, pl.ds(start, BAND_KEYS), cols],
                            preferred_element_type=F32)
        o = o * (1.0 / den)
        out = jnp.where(first, o[:sub_q], o[sub_q:])
        o_ref[0, si * sub_q:(si + 1) * sub_q, cols] = out.astype(o_ref.dtype)

    nxt = scores(*stages[0])
    for i, (si, p) in enumerate(stages):
        cur = nxt
        if i + 1 < len(stages):
            nxt = scores(*stages[i + 1])
        finish(si, p, cur)


def _neighbourhood_attention(q, k, v, kc, vc, rpb):
    nb, s, dn = q.shape
    rows = s // GRID_W
    nctx = kc.shape[1]
    w0s, tiles = _na_plan(rows)
    bias = _na_bias(rpb)
    n_sub = ATTN_ROWS // SUB_ROWS
    m_rows = ATTN_ROWS * GRID_W
    grid_spec = pltpu.PrefetchScalarGridSpec(
        num_scalar_prefetch=2,
        grid=(nb, rows // ATTN_ROWS),
        in_specs=[
            pl.BlockSpec((1, m_rows, dn), lambda bi, i, w0, pat: (bi, i, 0)),
            pl.BlockSpec((1, s, dn), lambda bi, i, w0, pat: (bi, 0, 0)),
            pl.BlockSpec((1, s, dn), lambda bi, i, w0, pat: (bi, 0, 0)),
            pl.BlockSpec((1, nctx, dn), lambda bi, i, w0, pat: (bi, 0, 0)),
            pl.BlockSpec((1, nctx, dn), lambda bi, i, w0, pat: (bi, 0, 0)),
            pl.BlockSpec(bias.shape, lambda bi, i, w0, pat: (0, 0, 0), pipeline_mode=pl.Buffered(1)),
        ],
        out_specs=pl.BlockSpec((1, m_rows, dn), lambda bi, i, w0, pat: (bi, i, 0)),
    )
    return pl.pallas_call(
        functools.partial(_attn_kernel, banded=True, n_sub=n_sub, sub_q=SUB_Q),
        out_shape=jax.ShapeDtypeStruct((nb, s, dn), BF16),
        grid_spec=grid_spec,
        compiler_params=_cparams(("arbitrary", "arbitrary")),
        name="neighbourhood_attention",
    )(jnp.asarray(w0s), jnp.asarray(tiles), q, k, v, kc, vc, bias)


def _context_attention(q, kc, vc):
    nb, s, dn = q.shape
    spec = pl.BlockSpec((1, s, dn), lambda bi, i, w0, pat: (bi, 0, 0))
    grid_spec = pltpu.PrefetchScalarGridSpec(
        num_scalar_prefetch=2, grid=(nb, 1), in_specs=[spec] * 5, out_specs=spec)
    dummy = jnp.zeros((1,), jnp.int32)
    return pl.pallas_call(
        functools.partial(_attn_kernel, banded=False, n_sub=1, sub_q=s),
        out_shape=jax.ShapeDtypeStruct((nb, s, dn), BF16),
        grid_spec=grid_spec,
        compiler_params=_cparams(("arbitrary", "arbitrary")),
        name="context_attention",
    )(dummy, dummy, q, kc, vc, kc, vc)


def _store_rowmajor(ref, val, base=0):
    n, width = val.shape
    chunks = width // LANES
    for c in range(chunks):
        ref[pl.ds(base + c, n, stride=chunks), :] = val[:, c * LANES:(c + 1) * LANES]


def _load_rowmajor(ref, base, n, chunks):
    return jnp.concatenate([ref[pl.ds(base + c, n, stride=chunks), :] for c in range(chunks)], axis=1)


def _outproj_kernel(yab_ref, yc_ref, x_ref, w_ref, bo_ref, gt_ref, g_ref, b_ref, sh_ref, sc_ref,
                    wrh_ref, wrl_ref, xmid_ref, hm_ref, lg_ref, w_bf):
    @pl.when(_first_step())
    def _():
        w_bf[...] = w_ref[0].astype(BF16)

    half = yab_ref.shape[2]
    tm = x_ref.shape[1]
    chunks = x_ref.shape[2] // LANES
    n_part = max(1, tm // OUT_ROW_GROUP)
    rows_per = tm // n_part
    groups = [slice(part * rows_per, (part + 1) * rows_per) for part in range(n_part)]
    ys = [jnp.dot(yab_ref[0, rows, :], w_bf[0:half, :], preferred_element_type=F32)
          + jnp.dot(yc_ref[0, rows, :], w_bf[half:, :], preferred_element_type=F32) + bo_ref[...]
          for rows in groups]
    for part, (rows, y) in enumerate(zip(groups, ys)):
        xm = _ln(DEEPNORM_ALPHA * x_ref[0, rows, :] + (1.0 + gt_ref[0]) * y) * g_ref[...] + b_ref[...]
        xmid_ref[0, rows, :] = xm
        hm = _ln(xm) * (1.0 + sc_ref[0]) + sh_ref[0]
        _store_rowmajor(hm_ref, hm, base=part * rows_per * chunks)
        hm_hi = hm.astype(BF16)
        hm_lo = (hm - hm_hi.astype(F32)).astype(BF16)
        lg_ref[0, rows, :] = (jnp.dot(hm_hi, wrh_ref[...], preferred_element_type=F32)
                              + jnp.dot(hm_lo, wrh_ref[...], preferred_element_type=F32)
                              + jnp.dot(hm_hi, wrl_ref[...], preferred_element_type=F32))


def _outproj(yab, yc, x, w, layer, bo, gt, g, b, sh, sc, wr_hi, wr_lo, tm):
    nb, s, d = x.shape
    half = yab.shape[2]
    vec = pl.BlockSpec((1, d), lambda bi, i: (0, 0))
    tok = lambda width: pl.BlockSpec((1, tm, width), lambda bi, i: (bi, i, 0))
    n_i = s // tm
    return pl.pallas_call(
        _outproj_kernel,
        out_shape=[jax.ShapeDtypeStruct((nb, s, d), F32),
                   jax.ShapeDtypeStruct((nb * s * (d // LANES), LANES), F32),
                   jax.ShapeDtypeStruct((nb, s, LANES), F32)],
        grid=(nb, n_i),
        in_specs=[tok(half), tok(half), tok(d),
                  pl.BlockSpec((1, d, d), lambda bi, i: (layer, 0, 0), pipeline_mode=pl.Buffered(1)),
                  vec, _mod_spec(gt, d), vec, vec,
                  _mod_spec(sh, d), _mod_spec(sc, d), pl.BlockSpec((d, LANES), lambda bi, i: (0, 0)),
                  pl.BlockSpec((d, LANES), lambda bi, i: (0, 0))],
        out_specs=[tok(d), pl.BlockSpec((tm * (d // LANES), LANES), lambda bi, i: (bi * n_i + i, 0)),
                   tok(LANES)],
        scratch_shapes=[pltpu.VMEM((d, d), BF16)],
        compiler_params=_cparams(("arbitrary", "arbitrary")),
        name="outproj_postnorm",
    )(yab, yc, x, w, bo, gt[0], g, b, sh[0], sc[0], wr_hi, wr_lo)


CUM_CHUNK = 256
F32_EXP_BIAS = 127
F32_MANT_BITS = 23


def _prefix_count(mask_f32, tri):
    rows, n = mask_f32.shape
    tc = min(CUM_CHUNK, n)
    base = jnp.zeros((rows, 1), F32)
    parts = []
    for ci in range(n // tc):
        blk = mask_f32[:, ci * tc:(ci + 1) * tc]
        parts.append(jnp.dot(blk.astype(BF16), tri[:tc, :tc], preferred_element_type=F32) + base)
        base = base + jnp.sum(blk, axis=-1, keepdims=True)
    return jnp.concatenate(parts, axis=-1)


def _pow2(k):
    return pltpu.bitcast((k + F32_EXP_BIAS) << F32_MANT_BITS, F32)


def _route_kernel(lg_ref, slot_c_ref, gate_t_ref, *, cap, slot_stride):
    nb = lg_ref.shape[0]
    assert nb * N_EXPERTS == LANES
    rows = []
    for b in range(nb):
        lg = lg_ref[b]
        lane = lax.broadcasted_iota(jnp.int32, lg.shape, 1)
        lgm = jnp.where(lane < N_EXPERTS, lg, NEG_INF)
        ex = jnp.exp(lgm - jnp.max(lgm, axis=-1, keepdims=True))
        aff = ex / jnp.sum(ex, axis=-1, keepdims=True)
        rows.append(aff.T[0:N_EXPERTS, :])
    a = jnp.concatenate(rows, axis=0)
    capf = float(cap)

    def enough(t):
        return jnp.sum((a >= t).astype(F32), axis=-1, keepdims=True) >= capf

    def exp_step(_, carry):
        lo, hi = carry
        mid = lo + ((hi - lo + 1) >> 1)
        ok = enough(_pow2(mid))
        return jnp.where(ok, mid, lo), jnp.where(ok, hi, mid - 1)

    k_lo = jnp.full((LANES, 1), -F32_EXP_BIAS, jnp.int32)
    k_hi = jnp.zeros((LANES, 1), jnp.int32)
    k_lo, _ = lax.fori_loop(0, 7, exp_step, (k_lo, k_hi))
    base = _pow2(k_lo)

    def mant_step(_, carry):
        t, step = carry
        step = step * 0.5
        cand = t + step
        return jnp.where(enough(cand), cand, t), step

    thr, _ = lax.fori_loop(0, F32_MANT_BITS, mant_step, (base, base))

    r_i = lax.broadcasted_iota(jnp.int32, (CUM_CHUNK, CUM_CHUNK), 0)
    c_i = lax.broadcasted_iota(jnp.int32, (CUM_CHUNK, CUM_CHUNK), 1)
    tri = (r_i < c_i).astype(BF16)
    gt = (a > thr).astype(F32)
    eq = (a == thr).astype(F32)
    need = capf - jnp.sum(gt, axis=-1, keepdims=True)
    sel = gt + eq * (_prefix_count(eq, tri) < need).astype(F32)
    pos = _prefix_count(sel, tri)
    sample = lax.broadcasted_iota(jnp.int32, (LANES, 1), 0) >> (N_EXPERTS.bit_length() - 1)
    slot = jnp.where(sel > 0.0, pos + (sample * slot_stride).astype(F32), -1.0)
    for b in range(nb):
        lo = b * N_EXPERTS
        gate_t_ref[b] = a[lo:lo + N_EXPERTS, :]
        rolled = slot if b == 0 else jnp.concatenate([slot[lo:, :], slot[:lo, :]], axis=0)
        slot_c_ref[b] = rolled.T


def _route(logits, cap, slot_stride):
    nb, s, _ = logits.shape
    whole = lambda shape: pl.BlockSpec(shape, lambda i: (0,) * len(shape))
    return pl.pallas_call(
        functools.partial(_route_kernel, cap=cap, slot_stride=slot_stride),
        out_shape=[jax.ShapeDtypeStruct((nb, s, LANES), F32), jax.ShapeDtypeStruct((nb, N_EXPERTS, s), F32)],
        grid=(1,),
        in_specs=[whole((nb, s, LANES))],
        out_specs=[whole((nb, s, LANES)), whole((nb, N_EXPERTS, s))],
        compiler_params=_cparams(("arbitrary",)),
        name="route",
    )(logits)


TOK_SPLIT = 64


def _slot_list_kernel(slot_ref, gate_ref, idx_ref, g_ref, *, n_slots, tok_stride, merge):
    s = slot_ref.shape[1]
    b = pl.program_id(0)
    slot_id = lax.broadcasted_iota(jnp.int32, (s, n_slots), 1).astype(F32)
    tok = lax.broadcasted_iota(jnp.int32, (1, s), 1) + b * tok_stride
    tok_hi = (tok >> (TOK_SPLIT.bit_length() - 1)).astype(F32)
    tok_lo = (tok & (TOK_SPLIT - 1)).astype(F32)
    zeros = jnp.zeros((SUBLANES - 5, s), F32)
    idx_rows, g_rows = [], []
    for e in range(N_EXPERTS):
        hit = jnp.where(slot_ref[0, :, e:e + 1] == slot_id, 1.0, 0.0).astype(BF16)
        g0 = gate_ref[0, e:e + 1, :]
        g_hi = g0.astype(BF16).astype(F32)
        g_mid = (g0 - g_hi).astype(BF16).astype(F32)
        g_lo = g0 - g_hi - g_mid
        lhs = jnp.concatenate([tok_hi, tok_lo, g_hi, g_mid, g_lo, zeros], axis=0).astype(BF16)
        out = jnp.dot(lhs, hit, preferred_element_type=F32)
        idx_rows.append(out[0:1] * float(TOK_SPLIT) + out[1:2])
        g_rows.append(out[2:3] + out[3:4] + out[4:5])
    idx = jnp.concatenate(idx_rows, axis=0).astype(jnp.int32) * SUBLANES
    g = jnp.concatenate(g_rows + [jnp.zeros((LANES - N_EXPERTS, n_slots), F32)], axis=0)
    g_t = g.T
    g_cols = [jnp.broadcast_to(g_t[:, e:e + 1], (n_slots, LANES)) for e in range(N_EXPERTS)]
    if merge:
        @pl.when(b == 0)
        def _():
            idx_ref[0] = idx
            for e in range(N_EXPERTS):
                g_ref[e] = g_cols[e]

        @pl.when(b > 0)
        def _():
            idx_ref[0] = idx_ref[0] + idx
            for e in range(N_EXPERTS):
                g_ref[e] = g_ref[e] + g_cols[e]
    else:
        idx_ref[0] = idx
        for e in range(N_EXPERTS):
            g_ref[e] = g_cols[e]


def _slot_lists(slot_c, gate_t, n_slots, merge):
    nb, s, _ = slot_c.shape
    nbo = 1 if merge else nb
    idx_map = (lambda bi: (0, 0, 0)) if merge else (lambda bi: (bi, 0, 0))
    g_map = (lambda bi: (0, 0, 0)) if merge else (lambda bi: (0, bi, 0))
    return pl.pallas_call(
        functools.partial(_slot_list_kernel, n_slots=n_slots, tok_stride=s if merge else 0, merge=merge),
        out_shape=[jax.ShapeDtypeStruct((nbo, N_EXPERTS, n_slots), jnp.int32),
                   jax.ShapeDtypeStruct((N_EXPERTS, nbo * n_slots, LANES), F32)],
        grid=(nb,),
        in_specs=[pl.BlockSpec((1, s, LANES), lambda bi: (bi, 0, 0)),
                  pl.BlockSpec((1, N_EXPERTS, s), lambda bi: (bi, 0, 0))],
        out_specs=[pl.BlockSpec((1, N_EXPERTS, n_slots), idx_map),
                   pl.BlockSpec((N_EXPERTS, n_slots, LANES), g_map)],
        compiler_params=_cparams(("arbitrary",)),
        name="slot_lists",
    )(slot_c, gate_t)


ROW_UNROLL = 16


def _dispatch_kernel(idx_ref, hm_ref, x_ref, rows_scr, *, n_slots):
    b = pl.program_id(0)
    chunks = x_ref.shape[2] // LANES
    for e in range(N_EXPERTS):
        base = (b * N_EXPERTS + e) * n_slots

        def move(i, carry):
            s0 = i * ROW_UNROLL
            first = base + s0
            tiles = [hm_ref[pl.ds(pl.multiple_of(idx_ref[first + u], chunks), chunks), :]
                     for u in range(ROW_UNROLL)]
            dst = pl.multiple_of(s0 * chunks, ROW_UNROLL * chunks)
            rows_scr[pl.ds(dst, ROW_UNROLL * chunks), :] = jnp.concatenate(tiles, axis=0)
            return carry

        lax.fori_loop(0, n_slots // ROW_UNROLL, move, 0)
        x_ref[e] = _load_rowmajor(rows_scr, 0, n_slots, chunks).astype(x_ref.dtype)


def _dispatch(idx, hm_rm, nb, s, d, n_slots):
    chunks = d // LANES
    assert chunks == SUBLANES and n_slots % ROW_UNROLL == 0
    grid_spec = pltpu.PrefetchScalarGridSpec(
        num_scalar_prefetch=1,
        grid=(nb,),
        in_specs=[pl.BlockSpec((s * chunks, LANES), lambda bi, idx_r: (bi, 0))],
        out_specs=pl.BlockSpec((N_EXPERTS, n_slots, d), lambda bi, idx_r: (0, bi, 0)),
        scratch_shapes=[pltpu.VMEM((n_slots * chunks, LANES), F32)],
    )
    return pl.pallas_call(
        functools.partial(_dispatch_kernel, n_slots=n_slots),
        out_shape=jax.ShapeDtypeStruct((N_EXPERTS, nb * n_slots, d), BF16),
        grid_spec=grid_spec,
        compiler_params=_cparams(("arbitrary",)),
        name="dispatch",
    )(idx, hm_rm)


W_PARTS = 4

def _expert_kernel(*refs, n_main, has_extra):
    n_in = 4 if has_extra else 2
    n_out = 2 if has_extra else 1
    acts, w_parts = refs[:n_in], refs[n_in:n_in + 3 * W_PARTS]
    outs = refs[n_in + 3 * W_PARTS:n_in + 3 * W_PARTS + n_out]
    wg_bf, wu_bf, wd_bf = refs[n_in + 3 * W_PARTS + n_out:]
    if has_extra:
        x_ref, g_ref, xx_ref, gg_ref = acts
        y_ref, yy_ref = outs
    else:
        x_ref, g_ref = acts
        (y_ref,) = outs
    i = pl.program_id(1)

    @pl.when(i == 0)
    def _():
        for k, w_bf in enumerate((wg_bf, wu_bf, wd_bf)):
            for j in range(W_PARTS):
                part = w_parts[k * W_PARTS + j]
                rows = part.shape[2]
                w_bf[j * rows:(j + 1) * rows, :] = part[0, 0].astype(BF16)

    def ffn(x_in, gate_in, out_ref):
        rows_all = x_in.shape[1]
        chunks = x_in.shape[2] // LANES
        n_part = max(1, rows_all // ROW_GROUP)
        rows_per = rows_all // n_part
        groups = [slice(p * rows_per, (p + 1) * rows_per) for p in range(n_part)]
        ups = []
        for rows in groups:
            x_e = x_in[0, rows, :]
            ups.append((jnp.dot(x_e, wg_bf[...], preferred_element_type=F32),
                        jnp.dot(x_e, wu_bf[...], preferred_element_type=F32)))
        for p, (rows, (a, u)) in enumerate(zip(groups, ups)):
            h = (a * _sigmoid(a) * u).astype(BF16)
            y = jnp.dot(h, wd_bf[...], preferred_element_type=F32) * gate_in[0, rows, 0:1]
            _store_rowmajor(out_ref, y, base=p * rows_per * chunks)

    if has_extra:
        @pl.when(i < n_main)
        def _():
            ffn(x_ref, g_ref, y_ref.at[0])

        @pl.when(i == n_main)
        def _():
            ffn(xx_ref, gg_ref, yy_ref.at[0])
    else:
        ffn(x_ref, g_ref, y_ref.at[0])


def _experts(x_e, g_e, x_extra, g_extra, wg, wu, wd, layer, tm):
    _, rows, d = x_e.shape
    f = wg.shape[3]
    chunks = d // LANES
    n_main = rows // tm
    has_extra = x_extra is not None
    main_map = lambda e, i: (e, jnp.minimum(i, n_main - 1), 0)

    def w_specs(shape):
        assert shape[0] % W_PARTS == 0 and n_main >= W_PARTS
        blk = (1, 1, shape[0] // W_PARTS, shape[1])
        return [pl.BlockSpec(blk, functools.partial(
            lambda e, i, j: (layer, jnp.minimum(e + (i > j).astype(jnp.int32), N_EXPERTS - 1), j, 0), j=j))
            for j in range(W_PARTS)]

    in_specs = [pl.BlockSpec((1, tm, d), main_map), pl.BlockSpec((1, tm, LANES), main_map)]
    out_specs = [pl.BlockSpec((1, tm * chunks, LANES), main_map)]
    out_shape = [jax.ShapeDtypeStruct((N_EXPERTS, rows * chunks, LANES), F32)]
    args = [x_e, g_e]
    if has_extra:
        rows2 = x_extra.shape[1]
        in_specs += [pl.BlockSpec((1, rows2, d), lambda e, i: (e, 0, 0)),
                     pl.BlockSpec((1, rows2, LANES), lambda e, i: (e, 0, 0))]
        out_specs.append(pl.BlockSpec((1, rows2 * chunks, LANES), lambda e, i: (e, 0, 0)))
        out_shape.append(jax.ShapeDtypeStruct((N_EXPERTS, rows2 * chunks, LANES), F32))
        args += [x_extra, g_extra]
    outs = pl.pallas_call(
        functools.partial(_expert_kernel, n_main=n_main, has_extra=has_extra),
        out_shape=out_shape,
        grid=(N_EXPERTS, n_main + int(has_extra)),
        in_specs=in_specs + w_specs((d, f)) + w_specs((d, f)) + w_specs((f, d)),
        out_specs=out_specs,
        scratch_shapes=[pltpu.VMEM((d, f), BF16), pltpu.VMEM((d, f), BF16), pltpu.VMEM((f, d), BF16)],
        compiler_params=_cparams(("arbitrary", "arbitrary")),
        name="experts",
    )(*args, *([wg] * W_PARTS), *([wu] * W_PARTS), *([wd] * W_PARTS))
    return outs if has_extra else (outs[0], None)


COMBINE_EXPERTS = 4
NORM_CHUNK = 256


def _combine_kernel(idx_ref, y_ref, x_ref, gt_ref, g_ref, b_ref, o_ref, acc, *, n_slots):
    b = pl.program_id(0)
    j = pl.program_id(1)
    s, d = x_ref.shape[1], x_ref.shape[2]
    chunks = d // LANES

    @pl.when(j == 0)
    def _():
        acc[...] = jnp.zeros(acc.shape, F32)

    for eg in range(COMBINE_EXPERTS):
        base = (b * N_EXPERTS + j * COMBINE_EXPERTS + eg) * n_slots

        def add(i, carry):
            s0 = i * ROW_UNROLL
            first = base + s0
            src = pl.multiple_of(s0 * chunks, ROW_UNROLL * chunks)
            y_rows = y_ref[eg, pl.ds(src, ROW_UNROLL * chunks), :]
            new = []
            for u in range(ROW_UNROLL):
                dst = pl.multiple_of(idx_ref[first + u], chunks)
                new.append((dst, acc[pl.ds(dst, chunks), :] + y_rows[u * chunks:(u + 1) * chunks, :]))
            for dst, val in new:
                acc[pl.ds(dst, chunks), :] = val
            return carry

        lax.fori_loop(0, n_slots // ROW_UNROLL, add, 0)

    @pl.when(j == pl.num_programs(1) - 1)
    def _():
        for ci in range(s // NORM_CHUNK):
            rows = slice(ci * NORM_CHUNK, (ci + 1) * NORM_CHUNK)
            ym = _load_rowmajor(acc, ci * NORM_CHUNK * chunks, NORM_CHUNK, chunks)
            z = DEEPNORM_ALPHA * x_ref[0, rows, :] + (1.0 + gt_ref[0]) * ym
            o_ref[0, rows, :] = _ln(z) * g_ref[...] + b_ref[...]


def _combine(idx, y_rm, x_mid, gt, g, b, n_slots):
    nb, s, d = x_mid.shape
    chunks = d // LANES
    assert chunks == SUBLANES and N_EXPERTS % COMBINE_EXPERTS == 0 and s % NORM_CHUNK == 0
    vec = pl.BlockSpec((1, d), lambda bi, j, i_r: (0, 0))
    grid_spec = pltpu.PrefetchScalarGridSpec(
        num_scalar_prefetch=1,
        grid=(nb, N_EXPERTS // COMBINE_EXPERTS),
        in_specs=[
            pl.BlockSpec((COMBINE_EXPERTS, n_slots * chunks, LANES), lambda bi, j, i_r: (j, bi, 0)),
            pl.BlockSpec((1, s, d), lambda bi, j, i_r: (bi, 0, 0)),
            _mod_spec(gt, d), vec, vec,
        ],
        out_specs=pl.BlockSpec((1, s, d), lambda bi, j, i_r: (bi, 0, 0)),
        scratch_shapes=[pltpu.VMEM((s * chunks, LANES), F32)],
    )
    return pl.pallas_call(
        functools.partial(_combine_kernel, n_slots=n_slots),
        out_shape=jax.ShapeDtypeStruct((nb, s, d), F32),
        grid_spec=grid_spec,
        compiler_params=_cparams(("arbitrary", "arbitrary")),
        name="combine_postnorm",
    )(idx, y_rm, x_mid, gt[0], g, b)


def kernel(x, c, ctx, c_ctx, w_mod, b_mod, w_in, b_in, w_short, w_conf_dw, b_conf_dw, g_conf_ln, b_conf_ln,
           na_rpb, w_out, b_out, g_post1, b_post1, w_router, w_gate, w_up, w_down, g_post2, b_post2):
    bsz, seq, d = x.shape
    nctx = ctx.shape[1]
    cap = EC_CAPACITY_FACTOR * seq // N_EXPERTS
    cap_ctx = EC_CAPACITY_FACTOR * nctx // N_EXPERTS
    q_scale = NA_HEAD_DIM ** -0.5 * LOG2E

    cond = jnp.concatenate([c, c_ctx[None, :], jnp.zeros((MOD_ROWS - bsz - 1, d), F32)], axis=0)
    mods = _modulation(cond, w_mod, b_mod)

    lat_splits = ((3 * D_CONV, 1.0), (2 * D_CONF, 1.0), (D_NA, q_scale), (D_NA, 1.0), (D_NA, 1.0))
    lat_dtypes = (F32, F32, BF16, BF16, BF16)
    kv_splits = ((D_NA, 1.0), (D_NA, 1.0))
    b_in3 = b_in[:, None, :]

    xc = ctx
    for l in range(DEPTH):
        last = l == DEPTH - 1
        m_lat = [(mods, (l * N_MOD + k) * MOD_ROWS, True) for k in range(N_MOD)]
        m_ctx = [(mods, (l * N_MOD + k) * MOD_ROWS + bsz, False) for k in range(N_MOD)]
        b_out_l = b_out[l][None, :]
        wr = jnp.pad(w_router[l], ((0, 0), (0, LANES - N_EXPERTS)))
        wr_hi = wr.astype(BF16)
        wr_lo = (wr - wr_hi.astype(F32)).astype(BF16)
        g1, b1 = g_post1[l][None, :], b_post1[l][None, :]
        g2, b2 = g_post2[l][None, :], b_post2[l][None, :]
        conv_w = (w_short[l], w_conf_dw[l], b_conf_dw[l], g_conf_ln[l], b_conf_ln[l])

        if last:
            k_c, v_c = _inproj(xc, m_ctx[0], m_ctx[1], w_in, b_in3, l, OFF_K, kv_splits, (BF16, BF16), tm=nctx)
        else:
            uac, ubc, q_c, k_c, v_c = _inproj(xc, m_ctx[0], m_ctx[1], w_in, b_in3, l, 0,
                                              lat_splits, lat_dtypes, tm=nctx)

        ua, ub, q, k, v = _inproj(x, m_lat[0], m_lat[1], w_in, b_in3, l, 0, lat_splits, lat_dtypes, tm=512)
        yab = _conv_mixers(ua, ub, *conv_w)
        yc = _neighbourhood_attention(q, k, v, k_c, v_c, na_rpb[l])
        x_mid, hm, logits = _outproj(yab, yc, x, w_out, l, b_out_l, m_lat[2], g1, b1, m_lat[3], m_lat[4],
                                     wr_hi, wr_lo, tm=512)

        idx, gates = _slot_lists(*_route(logits, cap, 0), cap, merge=False)
        idx = idx.reshape(-1)
        x_e = _dispatch(idx, hm, bsz, seq, d, cap)

        x_ec = gates_c = None
        if not last:
            yabc = _conv_mixers(uac, ubc, *conv_w)
            ycc = _context_attention(q_c, k_c, v_c)
            xc_mid, hmc, logits_c = _outproj(yabc, ycc, xc, w_out, l, b_out_l, m_ctx[2], g1, b1,
                                             m_ctx[3], m_ctx[4], wr_hi, wr_lo, tm=nctx)
            n_c = bsz * cap_ctx
            idx_c, gates_c = _slot_lists(*_route(logits_c, cap_ctx, cap_ctx), n_c, merge=True)
            idx_c = idx_c.reshape(-1)
            x_ec = _dispatch(idx_c, hmc, 1, bsz * nctx, d, n_c)

        y_e, y_ec = _experts(x_e, gates, x_ec, gates_c, w_gate, w_up, w_down, l, tm=512)
        x = _combine(idx, y_e, x_mid, m_lat[5], g2, b2, cap)
        if not last:
            xc = _combine(idx_c, y_ec, xc_mid.reshape(1, bsz * nctx, d),
                          m_ctx[5], g2, b2, n_c).reshape(bsz, nctx, d)
    return x
```

```python
import functools
import math

import numpy as np
import jax
import jax.numpy as jnp
from jax import lax
from jax.experimental import pallas as pl
from jax.experimental.pallas import tpu as pltpu

F32 = jnp.float32
BF16 = jnp.bfloat16

D_MODEL = 1024
DEPTH = 2
GRID_W = 64
D_CONV = D_MODEL // 4
D_CONF = D_MODEL // 4
NA_HEAD_DIM = 64
D_NA = D_MODEL - D_CONV - D_CONF
N_NA_HEADS = D_NA // NA_HEAD_DIM
SHORT_CONV_W = 3
CONF_CONV_W = 31
NA_WIN_ROWS_MAX = 8
NA_WIN_COLS = 16
N_EXPERTS = 16
EC_CAPACITY_FACTOR = 2
D_EXPERT = 1024
LN_EPS = 1e-5
DEEPNORM_ALPHA = (2.0 * DEPTH) ** 0.25
NEG_INF = -1e30
LOG2E = math.log2(math.e)

OFF_A = 0
OFF_B = OFF_A + 3 * D_CONV
OFF_Q = OFF_B + 2 * D_CONF
OFF_K = OFF_Q + D_NA
OFF_V = OFF_K + D_NA
D_IN = OFF_V + D_NA

LANES = 128
SUBLANES = 8
MOD_ROWS = 16
VMEM_LIMIT = 56 * 1024 * 1024
ATTN_ROWS = 8
N_PAIRS = D_NA // LANES
ROW_GROUP = 256
OUT_ROW_GROUP = 128
HI = lax.Precision.HIGHEST


def _cparams(sem):
    return pltpu.CompilerParams(dimension_semantics=sem, vmem_limit_bytes=VMEM_LIMIT)


def _ln(x):
    mu = jnp.mean(x, axis=-1, keepdims=True)
    xc = x - mu
    var = jnp.mean(xc * xc, axis=-1, keepdims=True)
    return xc * lax.rsqrt(var + LN_EPS)


def _sigmoid(x):
    return 1.0 / (1.0 + jnp.exp(-x))


def _mod_kernel(cond_ref, w_ref, b_ref, o_ref):
    s = cond_ref[...]
    s = s * _sigmoid(s)
    w = w_ref[0]
    s_hi, w_hi = s.astype(BF16), w.astype(BF16)
    s_lo = (s - s_hi.astype(F32)).astype(BF16)
    w_lo = (w - w_hi.astype(F32)).astype(BF16)
    o_ref[0] = (jnp.dot(s_hi, w_hi, preferred_element_type=F32) + jnp.dot(s_lo, w_hi, preferred_element_type=F32)
                + jnp.dot(s_hi, w_lo, preferred_element_type=F32) + b_ref[0])


N_MOD = 6


def _modulation(cond, w_mod, b_mod):
    n_l, d, n = w_mod.shape
    out = pl.pallas_call(
        _mod_kernel,
        out_shape=jax.ShapeDtypeStruct((n_l * N_MOD, MOD_ROWS, d), F32),
        grid=(n_l, N_MOD),
        in_specs=[
            pl.BlockSpec((MOD_ROWS, d), lambda l, k: (0, 0)),
            pl.BlockSpec((1, d, d), lambda l, k: (l, 0, k)),
            pl.BlockSpec((1, 1, d), lambda l, k: (l, 0, k)),
        ],
        out_specs=pl.BlockSpec((1, MOD_ROWS, d), lambda l, k: (l * N_MOD + k, 0, 0)),
        compiler_params=_cparams(("arbitrary", "arbitrary")),
        name="modulation",
    )(cond, w_mod, b_mod.reshape(n_l, 1, n))
    return out.reshape(n_l * N_MOD * MOD_ROWS, 1, d)


def _mod_spec(mod, d):
    _, row0, per_sample = mod
    if per_sample:
        return pl.BlockSpec((1, 1, d), lambda bi, *_: (row0 + bi, 0, 0))
    return pl.BlockSpec((1, 1, d), lambda bi, *_: (row0, 0, 0))


def _first_step():
    return (pl.program_id(0) == 0) & (pl.program_id(1) == 0)


def _inproj_kernel(x_ref, sh_ref, sc_ref, w_ref, b_ref, *rest, splits, col0):
    *o_refs, w_bf = rest

    @pl.when(_first_step())
    def _():
        w_bf[...] = w_ref[0].astype(BF16)

    tm = x_ref.shape[1]
    n_part = max(1, tm // ROW_GROUP)
    rows_per = tm // n_part
    groups = [slice(part_i * rows_per, (part_i + 1) * rows_per) for part_i in range(n_part)]
    hs = [(_ln(x_ref[0, rows, :]) * (1.0 + sc_ref[0]) + sh_ref[0]).astype(BF16) for rows in groups]
    for rows, h in zip(groups, hs):
        u = jnp.dot(h, w_bf[:, col0:], preferred_element_type=F32) + b_ref[0, :, col0:]
        off = 0
        for o_ref, (width, scale) in zip(o_refs, splits):
            part = u[:, off:off + width]
            if scale != 1.0:
                part = part * scale
            o_ref[0, rows, :] = part.astype(o_ref.dtype)
            off += width


def _inproj(x, sh, sc, w, b, layer, col0, splits, dtypes, tm):
    nb, s, d = x.shape
    n = w.shape[2]
    return pl.pallas_call(
        functools.partial(_inproj_kernel, splits=splits, col0=col0),
        out_shape=[jax.ShapeDtypeStruct((nb, s, wd), dt) for (wd, _), dt in zip(splits, dtypes)],
        grid=(nb, s // tm),
        in_specs=[
            pl.BlockSpec((1, tm, d), lambda bi, i: (bi, i, 0)),
            _mod_spec(sh, d),
            _mod_spec(sc, d),
            pl.BlockSpec((1, d, n), lambda bi, i: (layer, 0, 0), pipeline_mode=pl.Buffered(1)),
            pl.BlockSpec((1, 1, n), lambda bi, i: (layer, 0, 0)),
        ],
        out_specs=[pl.BlockSpec((1, tm, wd), lambda bi, i: (bi, i, 0)) for wd, _ in splits],
        scratch_shapes=[pltpu.VMEM((d, n), BF16)],
        compiler_params=_cparams(("arbitrary", "arbitrary")),
        name="inproj",
    )(x, sh[0], sc[0], w, b)


CONV_CHUNK = 128
Z_PAD = 8
H_PAD = 16


def _conv_kernel(ua_ref, ub_ref, ws_ref, wd_ref, bd_ref, g_ref, b_ref, o_ref, z_scr, h_scr, *, seq):
    c = D_CONV
    z_scr[0:Z_PAD, :] = jnp.zeros((Z_PAD, c), F32)
    z_scr[Z_PAD + seq:2 * Z_PAD + seq, :] = jnp.zeros((Z_PAD, c), F32)
    h_scr[0, 0:H_PAD, :] = jnp.zeros((H_PAD, c), F32)
    h_scr[0, H_PAD + seq:2 * H_PAD + seq, :] = jnp.zeros((H_PAD, c), F32)
    z_scr[Z_PAD:Z_PAD + seq, :] = ua_ref[0, :, c:2 * c] * ua_ref[0, :, 2 * c:3 * c]
    h_scr[0, H_PAD:H_PAD + seq, :] = ub_ref[0, :, 0:c] * _sigmoid(ub_ref[0, :, c:2 * c])
    n_rows = seq + 2 * H_PAD
    h_all = h_scr[0]
    for r in range(1, SUBLANES):
        h_scr[r] = pltpu.roll(h_all, n_rows - r, axis=0)
    tc = min(CONV_CHUNK, seq)
    for ci in range(seq // tc):
        t0 = ci * tc
        acc = ws_ref[0:1, :] * z_scr[t0 + Z_PAD - 1:t0 + Z_PAD - 1 + tc, :]
        for j in range(1, SHORT_CONV_W):
            s0 = t0 + Z_PAD - 1 + j
            acc = acc + ws_ref[j:j + 1, :] * z_scr[s0:s0 + tc, :]
        ya = ua_ref[0, t0:t0 + tc, 0:c] * acc
        hb = bd_ref[...]
        for j in range(CONF_CONV_W):
            s0 = t0 + H_PAD - CONF_CONV_W // 2 + j
            a0 = s0 - s0 % SUBLANES
            hb = hb + wd_ref[j:j + 1, :] * h_scr[s0 % SUBLANES, a0:a0 + tc, :]
        hn = _ln(hb) * g_ref[...] + b_ref[...]
        yb = hn * _sigmoid(hn)
        o_ref[0, t0:t0 + tc, 0:c] = ya.astype(o_ref.dtype)
        o_ref[0, t0:t0 + tc, c:2 * c] = yb.astype(o_ref.dtype)


def _conv_mixers(ua, ub, w_short, w_dw, b_dw, g_ln, b_ln):
    nb, s, _ = ua.shape
    c = D_CONV
    full = lambda shape: pl.BlockSpec(shape, lambda bi: (0,) * len(shape))
    return pl.pallas_call(
        functools.partial(_conv_kernel, seq=s),
        out_shape=jax.ShapeDtypeStruct((nb, s, 2 * c), BF16),
        grid=(nb,),
        in_specs=[
            pl.BlockSpec((1, s, 3 * c), lambda bi: (bi, 0, 0)),
            pl.BlockSpec((1, s, 2 * c), lambda bi: (bi, 0, 0)),
            full((SHORT_CONV_W, c)), full((CONF_CONV_W, c)), full((1, c)), full((1, c)), full((1, c)),
        ],
        out_specs=pl.BlockSpec((1, s, 2 * c), lambda bi: (bi, 0, 0)),
        scratch_shapes=[pltpu.VMEM((s + 2 * Z_PAD, c), F32), pltpu.VMEM((SUBLANES, s + 2 * H_PAD, c), F32)],
        compiler_params=_cparams(("arbitrary",)),
        name="conv_mixers",
    )(ua, ub, w_short, w_dw, b_dw.reshape(1, c), g_ln.reshape(1, c), b_ln.reshape(1, c))


SUB_ROWS = 2
WIN_ROWS = SUB_ROWS + NA_WIN_ROWS_MAX - 1
BAND_KEYS = WIN_ROWS * GRID_W
SUB_Q = SUB_ROWS * GRID_W
N_DROW = 2 * NA_WIN_ROWS_MAX - 1
N_DCOL = 2 * NA_WIN_COLS - 1


def _bias_kernel(rpb_ref, o_ref):
    n_rows, n_cols = o_ref.shape
    col = lax.broadcasted_iota(jnp.int32, (LANES, n_cols), 1)
    qc = col >> 6
    kc = col & (GRID_W - 1)
    d_col = jnp.clip(kc - qc + (NA_WIN_COLS - 1), 0, N_DCOL - 1)
    onehot = (lax.broadcasted_iota(jnp.int32, (LANES, n_cols), 0) == d_col).astype(F32)
    vals = jnp.dot(rpb_ref[...], onehot, preferred_element_type=F32, precision=HI)
    col_r = lax.broadcasted_iota(jnp.int32, (n_rows, n_cols), 1)
    qc_r = col_r >> 6
    kc_r = col_r & (GRID_W - 1)
    c0 = jnp.clip(qc_r - NA_WIN_COLS // 2, 0, GRID_W - NA_WIN_COLS)
    inside = (kc_r >= c0) & (kc_r < c0 + NA_WIN_COLS)
    o_ref[...] = jnp.where(inside, vals * LOG2E, NEG_INF)


N_SLABS = -(-WIN_ROWS // 2)
TILES_PER_KIND = N_NA_HEADS * N_DROW
MASKED_TILE = 3 * TILES_PER_KIND


def _na_plan(rows):
    wr = min(NA_WIN_ROWS_MAX, rows)
    assert wr == NA_WIN_ROWS_MAX and rows % SUB_ROWS == 0 and rows >= WIN_ROWS
    row_start = np.clip(np.arange(rows) - wr // 2, 0, rows - wr)
    w0s, tiles = [], []
    for r0 in range(0, rows, SUB_ROWS):
        w0 = int(np.clip(r0 - wr // 2, 0, rows - WIN_ROWS))
        for iq in range(SUB_ROWS):
            r = r0 + iq
            assert row_start[r] >= w0 and row_start[r] + wr <= w0 + WIN_ROWS
            ok = lambda w: w < WIN_ROWS and row_start[r] <= w0 + w < row_start[r] + wr
            d_row = lambda w: w0 + w - r + NA_WIN_ROWS_MAX - 1
            for j in range(N_SLABS):
                lo, hi = ok(2 * j), ok(2 * j + 1)
                if lo and hi:
                    tiles.append(d_row(2 * j))
                elif lo:
                    tiles.append(TILES_PER_KIND + d_row(2 * j))
                elif hi:
                    tiles.append(2 * TILES_PER_KIND + d_row(2 * j + 1))
                else:
                    tiles.append(-1)
        w0s.append(w0)
    return np.array(w0s, np.int32), np.array(tiles, np.int32)


def _na_bias(rpb):
    assert GRID_W == 64 and 2 * GRID_W == LANES and TILES_PER_KIND <= LANES and N_DCOL <= LANES
    rpb2 = jnp.pad(rpb.astype(F32).reshape(TILES_PER_KIND, N_DCOL),
                   ((0, LANES - TILES_PER_KIND), (0, LANES - N_DCOL)))
    table = pl.pallas_call(
        _bias_kernel,
        out_shape=jax.ShapeDtypeStruct((LANES, GRID_W * GRID_W), F32),
        name="na_bias",
    )(rpb2)
    table = table[:TILES_PER_KIND].reshape(N_NA_HEADS, N_DROW, GRID_W, GRID_W)
    masked = jnp.full_like(table, NEG_INF)
    nxt = jnp.concatenate([table[:, 1:], masked[:, :1]], axis=1)
    kinds = [jnp.concatenate(pair, axis=-1).reshape(TILES_PER_KIND, GRID_W, LANES)
             for pair in ((table, nxt), (table, masked), (masked, table))]
    return jnp.concatenate(kinds + [jnp.full((1, GRID_W, LANES), NEG_INF, F32)], axis=0)


def _lane_reduce(xs, combine, reduce, neutral):
    chunks = []
    for x in xs:
        rows, n = x.shape
        n_full = n // LANES
        chunks += [x[:, j * LANES:(j + 1) * LANES] for j in range(n_full)]
        if n % LANES:
            fill = jnp.full((rows, LANES - n % LANES), neutral, x.dtype)
            chunks.append(jnp.concatenate([x[:, n_full * LANES:], fill], axis=1))
    return reduce(functools.reduce(combine, chunks), axis=-1, keepdims=True)


def _attn_kernel(w0_ref, tile_ref, q_ref, k_ref, v_ref, kc_ref, vc_ref, *rest, banded, n_sub, sub_q):
    if banded:
        bias_ref, o_ref = rest
    else:
        (o_ref,) = rest
    lane = lax.broadcasted_iota(jnp.int32, (sub_q, LANES), 1)
    first = lane < NA_HEAD_DIM
    nt = (((1,), (1,)), ((), ()))
    stages = [(si, p) for si in range(n_sub) for p in range(N_PAIRS)]

    def window(si):
        blk = pl.program_id(1) * n_sub + si
        return blk, pl.multiple_of(w0_ref[blk] * GRID_W, GRID_W)

    def bias(blk, head):
        row_blocks = []
        for iq in range(SUB_ROWS):
            slabs = []
            for j in range(N_SLABS):
                t = tile_ref[(blk * SUB_ROWS + iq) * N_SLABS + j]
                tile = bias_ref[jnp.where(t < 0, MASKED_TILE, t + head * N_DROW)]
                width = min(LANES, BAND_KEYS - j * LANES)
                slabs.append(tile[:, :width])
            row_blocks.append(jnp.concatenate(slabs, axis=1))
        return jnp.concatenate(row_blocks, axis=0)

    def scores(si, p):
        cols = slice(p * LANES, (p + 1) * LANES)
        q_p = q_ref[0, si * sub_q:(si + 1) * sub_q, cols]
        zero = jnp.zeros_like(q_p)
        qq = jnp.concatenate([jnp.where(first, q_p, zero), jnp.where(first, zero, q_p)], axis=0)
        parts = [lax.dot_general(qq, kc_ref[0, :, cols], nt, preferred_element_type=F32)]
        if banded:
            blk, start = window(si)
            both = jnp.concatenate([bias(blk, 2 * p), bias(blk, 2 * p + 1)], axis=0)
            parts.append(lax.dot_general(qq, k_ref[0, pl.ds(start, BAND_KEYS), cols], nt,
                                         preferred_element_type=F32) + both)
        return parts

    def finish(si, p, parts):
        cols = slice(p * LANES, (p + 1) * LANES)
        m = _lane_reduce(parts, jnp.maximum, jnp.max, NEG_INF)
        es = [jnp.exp2(s - m) for s in parts]
        den = _lane_reduce(es, jnp.add, jnp.sum, 0.0)
        o = jnp.dot(es[0].astype(BF16), vc_ref[0, :, cols], preferred_element_type=F32)
        if banded:
            _, start = window(si)
            o = o + jnp.dot(es[1].astype(BF16), v_ref[0, pl.ds(start, BAND_KEYS), cols],
                            preferred_element_type=F32)
        o = o * (1.0 / den)
        out = jnp.where(first, o[:sub_q], o[sub_q:])
        o_ref[0, si * sub_q:(si + 1) * sub_q, cols] = out.astype(o_ref.dtype)

    nxt = scores(*stages[0])
    for i, (si, p) in enumerate(stages):
        cur = nxt
        if i + 1 < len(stages):
            nxt = scores(*stages[i + 1])
        finish(si, p, cur)


def _neighbourhood_attention(q, k, v, kc, vc, rpb):
    nb, s, dn = q.shape
    rows = s // GRID_W
    nctx = kc.shape[1]
    w0s, tiles = _na_plan(rows)
    bias = _na_bias(rpb)
    n_sub = ATTN_ROWS // SUB_ROWS
    m_rows = ATTN_ROWS * GRID_W
    grid_spec = pltpu.PrefetchScalarGridSpec(
        num_scalar_prefetch=2,
        grid=(nb, rows // ATTN_ROWS),
        in_specs=[
            pl.BlockSpec((1, m_rows, dn), lambda bi, i, w0, pat: (bi, i, 0)),
            pl.BlockSpec((1, s, dn), lambda bi, i, w0, pat: (bi, 0, 0)),
            pl.BlockSpec((1, s, dn), lambda bi, i, w0, pat: (bi, 0, 0)),
            pl.BlockSpec((1, nctx, dn), lambda bi, i, w0, pat: (bi, 0, 0)),
            pl.BlockSpec((1, nctx, dn), lambda bi, i, w0, pat: (bi, 0, 0)),
            pl.BlockSpec(bias.shape, lambda bi, i, w0, pat: (0, 0, 0), pipeline_mode=pl.Buffered(1)),
        ],
        out_specs=pl.BlockSpec((1, m_rows, dn), lambda bi, i, w0, pat: (bi, i, 0)),
    )
    return pl.pallas_call(
        functools.partial(_attn_kernel, banded=True, n_sub=n_sub, sub_q=SUB_Q),
        out_shape=jax.ShapeDtypeStruct((nb, s, dn), BF16),
        grid_spec=grid_spec,
        compiler_params=_cparams(("arbitrary", "arbitrary")),
        name="neighbourhood_attention",
    )(jnp.asarray(w0s), jnp.asarray(tiles), q, k, v, kc, vc, bias)


def _context_attention(q, kc, vc):
    nb, s, dn = q.shape
    spec = pl.BlockSpec((1, s, dn), lambda bi, i, w0, pat: (bi, 0, 0))
    grid_spec = pltpu.PrefetchScalarGridSpec(
        num_scalar_prefetch=2, grid=(nb, 1), in_specs=[spec] * 5, out_specs=spec)
    dummy = jnp.zeros((1,), jnp.int32)
    return pl.pallas_call(
        functools.partial(_attn_kernel, banded=False, n_sub=1, sub_q=s),
        out_shape=jax.ShapeDtypeStruct((nb, s, dn), BF16),
        grid_spec=grid_spec,
        compiler_params=_cparams(("arbitrary", "arbitrary")),
        name="context_attention",
    )(dummy, dummy, q, kc, vc, kc, vc)


def _store_rowmajor(ref, val, base=0):
    n, width = val.shape
    chunks = width // LANES
    for c in range(chunks):
        ref[pl.ds(base + c, n, stride=chunks), :] = val[:, c * LANES:(c + 1) * LANES]


def _load_rowmajor(ref, base, n, chunks):
    return jnp.concatenate([ref[pl.ds(base + c, n, stride=chunks), :] for c in range(chunks)], axis=1)


def _outproj_kernel(yab_ref, yc_ref, x_ref, w_ref, bo_ref, gt_ref, g_ref, b_ref, sh_ref, sc_ref,
                    wrh_ref, wrl_ref, xmid_ref, hm_ref, lg_ref, w_bf):
    @pl.when(_first_step())
    def _():
        w_bf[...] = w_ref[0].astype(BF16)

    half = yab_ref.shape[2]
    tm = x_ref.shape[1]
    chunks = x_ref.shape[2] // LANES
    n_part = max(1, tm // OUT_ROW_GROUP)
    rows_per = tm // n_part
    groups = [slice(part * rows_per, (part + 1) * rows_per) for part in range(n_part)]
    ys = [jnp.dot(yab_ref[0, rows, :], w_bf[0:half, :], preferred_element_type=F32)
          + jnp.dot(yc_ref[0, rows, :], w_bf[half:, :], preferred_element_type=F32) + bo_ref[...]
          for rows in groups]
    for part, (rows, y) in enumerate(zip(groups, ys)):
        xm = _ln(DEEPNORM_ALPHA * x_ref[0, rows, :] + (1.0 + gt_ref[0]) * y) * g_ref[...] + b_ref[...]
        xmid_ref[0, rows, :] = xm
        hm = _ln(xm) * (1.0 + sc_ref[0]) + sh_ref[0]
        _store_rowmajor(hm_ref, hm, base=part * rows_per * chunks)
        hm_hi = hm.astype(BF16)
        hm_lo = (hm - hm_hi.astype(F32)).astype(BF16)
        lg_ref[0, rows, :] = (jnp.dot(hm_hi, wrh_ref[...], preferred_element_type=F32)
                              + jnp.dot(hm_lo, wrh_ref[...], preferred_element_type=F32)
                              + jnp.dot(hm_hi, wrl_ref[...], preferred_element_type=F32))


def _outproj(yab, yc, x, w, layer, bo, gt, g, b, sh, sc, wr_hi, wr_lo, tm):
    nb, s, d = x.shape
    half = yab.shape[2]
    vec = pl.BlockSpec((1, d), lambda bi, i: (0, 0))
    tok = lambda width: pl.BlockSpec((1, tm, width), lambda bi, i: (bi, i, 0))
    n_i = s // tm
    return pl.pallas_call(
        _outproj_kernel,
        out_shape=[jax.ShapeDtypeStruct((nb, s, d), F32),
                   jax.ShapeDtypeStruct((nb * s * (d // LANES), LANES), F32),
                   jax.ShapeDtypeStruct((nb, s, LANES), F32)],
        grid=(nb, n_i),
        in_specs=[tok(half), tok(half), tok(d),
                  pl.BlockSpec((1, d, d), lambda bi, i: (layer, 0, 0), pipeline_mode=pl.Buffered(1)),
                  vec, _mod_spec(gt, d), vec, vec,
                  _mod_spec(sh, d), _mod_spec(sc, d), pl.BlockSpec((d, LANES), lambda bi, i: (0, 0)),
                  pl.BlockSpec((d, LANES), lambda bi, i: (0, 0))],
        out_specs=[tok(d), pl.BlockSpec((tm * (d // LANES), LANES), lambda bi, i: (bi * n_i + i, 0)),
                   tok(LANES)],
        scratch_shapes=[pltpu.VMEM((d, d), BF16)],
        compiler_params=_cparams(("arbitrary", "arbitrary")),
        name="outproj_postnorm",
    )(yab, yc, x, w, bo, gt[0], g, b, sh[0], sc[0], wr_hi, wr_lo)


CUM_CHUNK = 256
F32_EXP_BIAS = 127
F32_MANT_BITS = 23


def _prefix_count(mask_f32, tri):
    rows, n = mask_f32.shape
    tc = min(CUM_CHUNK, n)
    base = jnp.zeros((rows, 1), F32)
    parts = []
    for ci in range(n // tc):
        blk = mask_f32[:, ci * tc:(ci + 1) * tc]
        parts.append(jnp.dot(blk.astype(BF16), tri[:tc, :tc], preferred_element_type=F32) + base)
        base = base + jnp.sum(blk, axis=-1, keepdims=True)
    return jnp.concatenate(parts, axis=-1)


def _pow2(k):
    return pltpu.bitcast((k + F32_EXP_BIAS) << F32_MANT_BITS, F32)


def _route_kernel(lg_ref, slot_c_ref, gate_t_ref, *, cap, slot_stride):
    nb = lg_ref.shape[0]
    assert nb * N_EXPERTS == LANES
    rows = []
    for b in range(nb):
        lg = lg_ref[b]
        lane = lax.broadcasted_iota(jnp.int32, lg.shape, 1)
        lgm = jnp.where(lane < N_EXPERTS, lg, NEG_INF)
        ex = jnp.exp(lgm - jnp.max(lgm, axis=-1, keepdims=True))
        aff = ex / jnp.sum(ex, axis=-1, keepdims=True)
        rows.append(aff.T[0:N_EXPERTS, :])
    a = jnp.concatenate(rows, axis=0)
    capf = float(cap)

    def enough(t):
        return jnp.sum((a >= t).astype(F32), axis=-1, keepdims=True) >= capf

    def exp_step(_, carry):
        lo, hi = carry
        mid = lo + ((hi - lo + 1) >> 1)
        ok = enough(_pow2(mid))
        return jnp.where(ok, mid, lo), jnp.where(ok, hi, mid - 1)

    k_lo = jnp.full((LANES, 1), -F32_EXP_BIAS, jnp.int32)
    k_hi = jnp.zeros((LANES, 1), jnp.int32)
    k_lo, _ = lax.fori_loop(0, 7, exp_step, (k_lo, k_hi))
    base = _pow2(k_lo)

    def mant_step(_, carry):
        t, step = carry
        step = step * 0.5
        cand = t + step
        return jnp.where(enough(cand), cand, t), step

    thr, _ = lax.fori_loop(0, F32_MANT_BITS, mant_step, (base, base))

    r_i = lax.broadcasted_iota(jnp.int32, (CUM_CHUNK, CUM_CHUNK), 0)
    c_i = lax.broadcasted_iota(jnp.int32, (CUM_CHUNK, CUM_CHUNK), 1)
    tri = (r_i < c_i).astype(BF16)
    gt = (a > thr).astype(F32)
    eq = (a == thr).astype(F32)
    need = capf - jnp.sum(gt, axis=-1, keepdims=True)
    sel = gt + eq * (_prefix_count(eq, tri) < need).astype(F32)
    pos = _prefix_count(sel, tri)
    sample = lax.broadcasted_iota(jnp.int32, (LANES, 1), 0) >> (N_EXPERTS.bit_length() - 1)
    slot = jnp.where(sel > 0.0, pos + (sample * slot_stride).astype(F32), -1.0)
    for b in range(nb):
        lo = b * N_EXPERTS
        gate_t_ref[b] = a[lo:lo + N_EXPERTS, :]
        rolled = slot if b == 0 else jnp.concatenate([slot[lo:, :], slot[:lo, :]], axis=0)
        slot_c_ref[b] = rolled.T


def _route(logits, cap, slot_stride):
    nb, s, _ = logits.shape
    whole = lambda shape: pl.BlockSpec(shape, lambda i: (0,) * len(shape))
    return pl.pallas_call(
        functools.partial(_route_kernel, cap=cap, slot_stride=slot_stride),
        out_shape=[jax.ShapeDtypeStruct((nb, s, LANES), F32), jax.ShapeDtypeStruct((nb, N_EXPERTS, s), F32)],
        grid=(1,),
        in_specs=[whole((nb, s, LANES))],
        out_specs=[whole((nb, s, LANES)), whole((nb, N_EXPERTS, s))],
        compiler_params=_cparams(("arbitrary",)),
        name="route",
    )(logits)


TOK_SPLIT = 64


def _slot_list_kernel(slot_ref, gate_ref, idx_ref, g_ref, *, n_slots, tok_stride, merge):
    s = slot_ref.shape[1]
    b = pl.program_id(0)
    slot_id = lax.broadcasted_iota(jnp.int32, (s, n_slots), 1).astype(F32)
    tok = lax.broadcasted_iota(jnp.int32, (1, s), 1) + b * tok_stride
    tok_hi = (tok >> (TOK_SPLIT.bit_length() - 1)).astype(F32)
    tok_lo = (tok & (TOK_SPLIT - 1)).astype(F32)
    zeros = jnp.zeros((SUBLANES - 5, s), F32)
    idx_rows, g_rows = [], []
    for e in range(N_EXPERTS):
        hit = jnp.where(slot_ref[0, :, e:e + 1] == slot_id, 1.0, 0.0).astype(BF16)
        g0 = gate_ref[0, e:e + 1, :]
        g_hi = g0.astype(BF16).astype(F32)
        g_mid = (g0 - g_hi).astype(BF16).astype(F32)
        g_lo = g0 - g_hi - g_mid
        lhs = jnp.concatenate([tok_hi, tok_lo, g_hi, g_mid, g_lo, zeros], axis=0).astype(BF16)
        out = jnp.dot(lhs, hit, preferred_element_type=F32)
        idx_rows.append(out[0:1] * float(TOK_SPLIT) + out[1:2])
        g_rows.append(out[2:3] + out[3:4] + out[4:5])
    idx = jnp.concatenate(idx_rows, axis=0).astype(jnp.int32) * SUBLANES
    g = jnp.concatenate(g_rows + [jnp.zeros((LANES - N_EXPERTS, n_slots), F32)], axis=0)
    g_t = g.T
    g_cols = [jnp.broadcast_to(g_t[:, e:e + 1], (n_slots, LANES)) for e in range(N_EXPERTS)]
    if merge:
        @pl.when(b == 0)
        def _():
            idx_ref[0] = idx
            for e in range(N_EXPERTS):
                g_ref[e] = g_cols[e]

        @pl.when(b > 0)
        def _():
            idx_ref[0] = idx_ref[0] + idx
            for e in range(N_EXPERTS):
                g_ref[e] = g_ref[e] + g_cols[e]
    else:
        idx_ref[0] = idx
        for e in range(N_EXPERTS):
            g_ref[e] = g_cols[e]


def _slot_lists(slot_c, gate_t, n_slots, merge):
    nb, s, _ = slot_c.shape
    nbo = 1 if merge else nb
    idx_map = (lambda bi: (0, 0, 0)) if merge else (lambda bi: (bi, 0, 0))
    g_map = (lambda bi: (0, 0, 0)) if merge else (lambda bi: (0, bi, 0))
    return pl.pallas_call(
        functools.partial(_slot_list_kernel, n_slots=n_slots, tok_stride=s if merge else 0, merge=merge),
        out_shape=[jax.ShapeDtypeStruct((nbo, N_EXPERTS, n_slots), jnp.int32),
                   jax.ShapeDtypeStruct((N_EXPERTS, nbo * n_slots, LANES), F32)],
        grid=(nb,),
        in_specs=[pl.BlockSpec((1, s, LANES), lambda bi: (bi, 0, 0)),
                  pl.BlockSpec((1, N_EXPERTS, s), lambda bi: (bi, 0, 0))],
        out_specs=[pl.BlockSpec((1, N_EXPERTS, n_slots), idx_map),
                   pl.BlockSpec((N_EXPERTS, n_slots, LANES), g_map)],
        compiler_params=_cparams(("arbitrary",)),
        name="slot_lists",
    )(slot_c, gate_t)


ROW_UNROLL = 16


def _dispatch_kernel(idx_ref, hm_ref, x_ref, rows_scr, *, n_slots):
    b = pl.program_id(0)
    chunks = x_ref.shape[2] // LANES
    for e in range(N_EXPERTS):
        base = (b * N_EXPERTS + e) * n_slots

        def move(i, carry):
            s0 = i * ROW_UNROLL
            first = base + s0
            tiles = [hm_ref[pl.ds(pl.multiple_of(idx_ref[first + u], chunks), chunks), :]
                     for u in range(ROW_UNROLL)]
            dst = pl.multiple_of(s0 * chunks, ROW_UNROLL * chunks)
            rows_scr[pl.ds(dst, ROW_UNROLL * chunks), :] = jnp.concatenate(tiles, axis=0)
            return carry

        lax.fori_loop(0, n_slots // ROW_UNROLL, move, 0)
        x_ref[e] = _load_rowmajor(rows_scr, 0, n_slots, chunks).astype(x_ref.dtype)


def _dispatch(idx, hm_rm, nb, s, d, n_slots):
    chunks = d // LANES
    assert chunks == SUBLANES and n_slots % ROW_UNROLL == 0
    grid_spec = pltpu.PrefetchScalarGridSpec(
        num_scalar_prefetch=1,
        grid=(nb,),
        in_specs=[pl.BlockSpec((s * chunks, LANES), lambda bi, idx_r: (bi, 0))],
        out_specs=pl.BlockSpec((N_EXPERTS, n_slots, d), lambda bi, idx_r: (0, bi, 0)),
        scratch_shapes=[pltpu.VMEM((n_slots * chunks, LANES), F32)],
    )
    return pl.pallas_call(
        functools.partial(_dispatch_kernel, n_slots=n_slots),
        out_shape=jax.ShapeDtypeStruct((N_EXPERTS, nb * n_slots, d), BF16),
        grid_spec=grid_spec,
        compiler_params=_cparams(("arbitrary",)),
        name="dispatch",
    )(idx, hm_rm)


W_PARTS = 4

def _expert_kernel(*refs, n_main, has_extra):
    n_in = 4 if has_extra else 2
    n_out = 2 if has_extra else 1
    acts, w_parts = refs[:n_in], refs[n_in:n_in + 3 * W_PARTS]
    outs = refs[n_in + 3 * W_PARTS:n_in + 3 * W_PARTS + n_out]
    wg_bf, wu_bf, wd_bf = refs[n_in + 3 * W_PARTS + n_out:]
    if has_extra:
        x_ref, g_ref, xx_ref, gg_ref = acts
        y_ref, yy_ref = outs
    else:
        x_ref, g_ref = acts
        (y_ref,) = outs
    i = pl.program_id(1)

    @pl.when(i == 0)
    def _():
        for k, w_bf in enumerate((wg_bf, wu_bf, wd_bf)):
            for j in range(W_PARTS):
                part = w_parts[k * W_PARTS + j]
                rows = part.shape[2]
                w_bf[j * rows:(j + 1) * rows, :] = part[0, 0].astype(BF16)

    def ffn(x_in, gate_in, out_ref):
        rows_all = x_in.shape[1]
        chunks = x_in.shape[2] // LANES
        n_part = max(1, rows_all // ROW_GROUP)
        rows_per = rows_all // n_part
        groups = [slice(p * rows_per, (p + 1) * rows_per) for p in range(n_part)]
        ups = []
        for rows in groups:
            x_e = x_in[0, rows, :]
            ups.append((jnp.dot(x_e, wg_bf[...], preferred_element_type=F32),
                        jnp.dot(x_e, wu_bf[...], preferred_element_type=F32)))
        for p, (rows, (a, u)) in enumerate(zip(groups, ups)):
            h = (a * _sigmoid(a) * u).astype(BF16)
            y = jnp.dot(h, wd_bf[...], preferred_element_type=F32) * gate_in[0, rows, 0:1]
            _store_rowmajor(out_ref, y, base=p * rows_per * chunks)

    if has_extra:
        @pl.when(i < n_main)
        def _():
            ffn(x_ref, g_ref, y_ref.at[0])

        @pl.when(i == n_main)
        def _():
            ffn(xx_ref, gg_ref, yy_ref.at[0])
    else:
        ffn(x_ref, g_ref, y_ref.at[0])


def _experts(x_e, g_e, x_extra, g_extra, wg, wu, wd, layer, tm):
    _, rows, d = x_e.shape
    f = wg.shape[3]
    chunks = d // LANES
    n_main = rows // tm
    has_extra = x_extra is not None
    main_map = lambda e, i: (e, jnp.minimum(i, n_main - 1), 0)

    def w_specs(shape):
        assert shape[0] % W_PARTS == 0 and n_main >= W_PARTS
        blk = (1, 1, shape[0] // W_PARTS, shape[1])
        return [pl.BlockSpec(blk, functools.partial(
            lambda e, i, j: (layer, jnp.minimum(e + (i > j).astype(jnp.int32), N_EXPERTS - 1), j, 0), j=j))
            for j in range(W_PARTS)]

    in_specs = [pl.BlockSpec((1, tm, d), main_map), pl.BlockSpec((1, tm, LANES), main_map)]
    out_specs = [pl.BlockSpec((1, tm * chunks, LANES), main_map)]
    out_shape = [jax.ShapeDtypeStruct((N_EXPERTS, rows * chunks, LANES), F32)]
    args = [x_e, g_e]
    if has_extra:
        rows2 = x_extra.shape[1]
        in_specs += [pl.BlockSpec((1, rows2, d), lambda e, i: (e, 0, 0)),
                     pl.BlockSpec((1, rows2, LANES), lambda e, i: (e, 0, 0))]
        out_specs.append(pl.BlockSpec((1, rows2 * chunks, LANES), lambda e, i: (e, 0, 0)))
        out_shape.append(jax.ShapeDtypeStruct((N_EXPERTS, rows2 * chunks, LANES), F32))
        args += [x_extra, g_extra]
    outs = pl.pallas_call(
        functools.partial(_expert_kernel, n_main=n_main, has_extra=has_extra),
        out_shape=out_shape,
        grid=(N_EXPERTS, n_main + int(has_extra)),
        in_specs=in_specs + w_specs((d, f)) + w_specs((d, f)) + w_specs((f, d)),
        out_specs=out_specs,
        scratch_shapes=[pltpu.VMEM((d, f), BF16), pltpu.VMEM((d, f), BF16), pltpu.VMEM((f, d), BF16)],
        compiler_params=_cparams(("arbitrary", "arbitrary")),
        name="experts",
    )(*args, *([wg] * W_PARTS), *([wu] * W_PARTS), *([wd] * W_PARTS))
    return outs if has_extra else (outs[0], None)


COMBINE_EXPERTS = 4
NORM_CHUNK = 256


def _combine_kernel(idx_ref, y_ref, x_ref, gt_ref, g_ref, b_ref, o_ref, acc, *, n_slots):
    b = pl.program_id(0)
    j = pl.program_id(1)
    s, d = x_ref.shape[1], x_ref.shape[2]
    chunks = d // LANES

    @pl.when(j == 0)
    def _():
        acc[...] = jnp.zeros(acc.shape, F32)

    for eg in range(COMBINE_EXPERTS):
        base = (b * N_EXPERTS + j * COMBINE_EXPERTS + eg) * n_slots

        def add(i, carry):
            s0 = i * ROW_UNROLL
            first = base + s0
            src = pl.multiple_of(s0 * chunks, ROW_UNROLL * chunks)
            y_rows = y_ref[eg, pl.ds(src, ROW_UNROLL * chunks), :]
            new = []
            for u in range(ROW_UNROLL):
                dst = pl.multiple_of(idx_ref[first + u], chunks)
                new.append((dst, acc[pl.ds(dst, chunks), :] + y_rows[u * chunks:(u + 1) * chunks, :]))
            for dst, val in new:
                acc[pl.ds(dst, chunks), :] = val
            return carry

        lax.fori_loop(0, n_slots // ROW_UNROLL, add, 0)

    @pl.when(j == pl.num_programs(1) - 1)
    def _():
        for ci in range(s // NORM_CHUNK):
            rows = slice(ci * NORM_CHUNK, (ci + 1) * NORM_CHUNK)
            ym = _load_rowmajor(acc, ci * NORM_CHUNK * chunks, NORM_CHUNK, chunks)
            z = DEEPNORM_ALPHA * x_ref[0, rows, :] + (1.0 + gt_ref[0]) * ym
            o_ref[0, rows, :] = _ln(z) * g_ref[...] + b_ref[...]


def _combine(idx, y_rm, x_mid, gt, g, b, n_slots):
    nb, s, d = x_mid.shape
    chunks = d // LANES
    assert chunks == SUBLANES and N_EXPERTS % COMBINE_EXPERTS == 0 and s % NORM_CHUNK == 0
    vec = pl.BlockSpec((1, d), lambda bi, j, i_r: (0, 0))
    grid_spec = pltpu.PrefetchScalarGridSpec(
        num_scalar_prefetch=1,
        grid=(nb, N_EXPERTS // COMBINE_EXPERTS),
        in_specs=[
            pl.BlockSpec((COMBINE_EXPERTS, n_slots * chunks, LANES), lambda bi, j, i_r: (j, bi, 0)),
            pl.BlockSpec((1, s, d), lambda bi, j, i_r: (bi, 0, 0)),
            _mod_spec(gt, d), vec, vec,
        ],
        out_specs=pl.BlockSpec((1, s, d), lambda bi, j, i_r: (bi, 0, 0)),
        scratch_shapes=[pltpu.VMEM((s * chunks, LANES), F32)],
    )
    return pl.pallas_call(
        functools.partial(_combine_kernel, n_slots=n_slots),
        out_shape=jax.ShapeDtypeStruct((nb, s, d), F32),
        grid_spec=grid_spec,
        compiler_params=_cparams(("arbitrary", "arbitrary")),
        name="combine_postnorm",
    )(idx, y_rm, x_mid, gt[0], g, b)


def kernel(x, c, ctx, c_ctx, w_mod, b_mod, w_in, b_in, w_short, w_conf_dw, b_conf_dw, g_conf_ln, b_conf_ln,
           na_rpb, w_out, b_out, g_post1, b_post1, w_router, w_gate, w_up, w_down, g_post2, b_post2):
    bsz, seq, d = x.shape
    nctx = ctx.shape[1]
    cap = EC_CAPACITY_FACTOR * seq // N_EXPERTS
    cap_ctx = EC_CAPACITY_FACTOR * nctx // N_EXPERTS
    q_scale = NA_HEAD_DIM ** -0.5 * LOG2E

    cond = jnp.concatenate([c, c_ctx[None, :], jnp.zeros((MOD_ROWS - bsz - 1, d), F32)], axis=0)
    mods = _modulation(cond, w_mod, b_mod)

    lat_splits = ((3 * D_CONV, 1.0), (2 * D_CONF, 1.0), (D_NA, q_scale), (D_NA, 1.0), (D_NA, 1.0))
    lat_dtypes = (F32, F32, BF16, BF16, BF16)
    kv_splits = ((D_NA, 1.0), (D_NA, 1.0))
    b_in3 = b_in[:, None, :]

    xc = ctx
    for l in range(DEPTH):
        last = l == DEPTH - 1
        m_lat = [(mods, (l * N_MOD + k) * MOD_ROWS, True) for k in range(N_MOD)]
        m_ctx = [(mods, (l * N_MOD + k) * MOD_ROWS + bsz, False) for k in range(N_MOD)]
        b_out_l = b_out[l][None, :]
        wr = jnp.pad(w_router[l], ((0, 0), (0, LANES - N_EXPERTS)))
        wr_hi = wr.astype(BF16)
        wr_lo = (wr - wr_hi.astype(F32)).astype(BF16)
        g1, b1 = g_post1[l][None, :], b_post1[l][None, :]
        g2, b2 = g_post2[l][None, :], b_post2[l][None, :]
        conv_w = (w_short[l], w_conf_dw[l], b_conf_dw[l], g_conf_ln[l], b_conf_ln[l])

        flat = lambda t: t.reshape(1, bsz * nctx, t.shape[-1])
        unflat = lambda t: t.reshape(bsz, nctx, t.shape[-1])
        if last:
            k_c, v_c = map(unflat, _inproj(flat(xc), m_ctx[0], m_ctx[1], w_in, b_in3, l, OFF_K,
                                           kv_splits, (BF16, BF16), tm=512))
        else:
            uac, ubc, q_c, k_c, v_c = map(unflat, _inproj(flat(xc), m_ctx[0], m_ctx[1], w_in, b_in3, l, 0,
                                                          lat_splits, lat_dtypes, tm=512))

        ua, ub, q, k, v = _inproj(x, m_lat[0], m_lat[1], w_in, b_in3, l, 0, lat_splits, lat_dtypes, tm=512)
        yab = _conv_mixers(ua, ub, *conv_w)
        yc = _neighbourhood_attention(q, k, v, k_c, v_c, na_rpb[l])
        x_mid, hm, logits = _outproj(yab, yc, x, w_out, l, b_out_l, m_lat[2], g1, b1, m_lat[3], m_lat[4],
                                     wr_hi, wr_lo, tm=512)

        idx, gates = _slot_lists(*_route(logits, cap, 0), cap, merge=False)
        idx = idx.reshape(-1)
        x_e = _dispatch(idx, hm, bsz, seq, d, cap)

        x_ec = gates_c = None
        if not last:
            yabc = _conv_mixers(uac, ubc, *conv_w)
            ycc = _context_attention(q_c, k_c, v_c)
            xc_mid, hmc, logits_c = _outproj(flat(yabc), flat(ycc), flat(xc), w_out, l, b_out_l, m_ctx[2], g1, b1,
                                             m_ctx[3], m_ctx[4], wr_hi, wr_lo, tm=512)
            n_c = bsz * cap_ctx
            idx_c, gates_c = _slot_lists(*_route(unflat(logits_c), cap_ctx, cap_ctx), n_c, merge=True)
            idx_c = idx_c.reshape(-1)
            x_ec = _dispatch(idx_c, hmc, 1, bsz * nctx, d, n_c)

        y_e, y_ec = _experts(x_e, gates, x_ec, gates_c, w_gate, w_up, w_down, l, tm=512)
        x = _combine(idx, y_e, x_mid, m_lat[5], g2, b2, cap)
        if not last:
            xc = unflat(_combine(idx_c, y_ec, xc_mid, m_ctx[5], g2, b2, n_c))
    return x
```

```python
import functools
import math

import numpy as np
import jax
import jax.numpy as jnp
from jax import lax
from jax.experimental import pallas as pl
from jax.experimental.pallas import tpu as pltpu

F32 = jnp.float32
BF16 = jnp.bfloat16

D_MODEL = 1024
DEPTH = 2
GRID_W = 64
D_CONV = D_MODEL // 4
D_CONF = D_MODEL // 4
NA_HEAD_DIM = 64
D_NA = D_MODEL - D_CONV - D_CONF
N_NA_HEADS = D_NA // NA_HEAD_DIM
SHORT_CONV_W = 3
CONF_CONV_W = 31
NA_WIN_ROWS_MAX = 8
NA_WIN_COLS = 16
N_EXPERTS = 16
EC_CAPACITY_FACTOR = 2
D_EXPERT = 1024
LN_EPS = 1e-5
DEEPNORM_ALPHA = (2.0 * DEPTH) ** 0.25
NEG_INF = -1e30
LOG2E = math.log2(math.e)

OFF_A = 0
OFF_B = OFF_A + 3 * D_CONV
OFF_Q = OFF_B + 2 * D_CONF
OFF_K = OFF_Q + D_NA
OFF_V = OFF_K + D_NA
D_IN = OFF_V + D_NA

LANES = 128
SUBLANES = 8
MOD_ROWS = 16
VMEM_LIMIT = 56 * 1024 * 1024
ATTN_ROWS = 8
N_PAIRS = D_NA // LANES
ROW_GROUP = 256
OUT_ROW_GROUP = 128
HI = lax.Precision.HIGHEST


def _cparams(sem):
    return pltpu.CompilerParams(dimension_semantics=sem, vmem_limit_bytes=VMEM_LIMIT)


def _ln(x):
    mu = jnp.mean(x, axis=-1, keepdims=True)
    xc = x - mu
    var = jnp.mean(xc * xc, axis=-1, keepdims=True)
    return xc * lax.rsqrt(var + LN_EPS)


def _sigmoid(x):
    return 1.0 / (1.0 + jnp.exp(-x))


def _mod_kernel(cond_ref, w_ref, b_ref, o_ref):
    s = cond_ref[...]
    s = s * _sigmoid(s)
    w = w_ref[0]
    s_hi, w_hi = s.astype(BF16), w.astype(BF16)
    s_lo = (s - s_hi.astype(F32)).astype(BF16)
    w_lo = (w - w_hi.astype(F32)).astype(BF16)
    o_ref[0] = (jnp.dot(s_hi, w_hi, preferred_element_type=F32) + jnp.dot(s_lo, w_hi, preferred_element_type=F32)
                + jnp.dot(s_hi, w_lo, preferred_element_type=F32) + b_ref[0])


N_MOD = 6


def _modulation(cond, w_mod, b_mod):
    n_l, d, n = w_mod.shape
    out = pl.pallas_call(
        _mod_kernel,
        out_shape=jax.ShapeDtypeStruct((n_l * N_MOD, MOD_ROWS, d), F32),
        grid=(n_l, N_MOD),
        in_specs=[
            pl.BlockSpec((MOD_ROWS, d), lambda l, k: (0, 0)),
            pl.BlockSpec((1, d, d), lambda l, k: (l, 0, k)),
            pl.BlockSpec((1, 1, d), lambda l, k: (l, 0, k)),
        ],
        out_specs=pl.BlockSpec((1, MOD_ROWS, d), lambda l, k: (l * N_MOD + k, 0, 0)),
        compiler_params=_cparams(("arbitrary", "arbitrary")),
        name="modulation",
    )(cond, w_mod, b_mod.reshape(n_l, 1, n))
    return out.reshape(n_l * N_MOD * MOD_ROWS, 1, d)


def _mod_spec(mod, d):
    _, row0, per_sample = mod
    if per_sample:
        return pl.BlockSpec((1, 1, d), lambda bi, *_: (row0 + bi, 0, 0))
    return pl.BlockSpec((1, 1, d), lambda bi, *_: (row0, 0, 0))


def _first_step():
    return (pl.program_id(0) == 0) & (pl.program_id(1) == 0)


def _inproj_kernel(x_ref, sh_ref, sc_ref, w_ref, b_ref, *rest, splits, col0):
    *o_refs, w_bf = rest

    @pl.when(_first_step())
    def _():
        w_bf[...] = w_ref[0].astype(BF16)

    tm = x_ref.shape[1]
    n_part = max(1, tm // ROW_GROUP)
    rows_per = tm // n_part
    groups = [slice(part_i * rows_per, (part_i + 1) * rows_per) for part_i in range(n_part)]
    hs = [(_ln(x_ref[0, rows, :]) * (1.0 + sc_ref[0]) + sh_ref[0]).astype(BF16) for rows in groups]
    for rows, h in zip(groups, hs):
        u = jnp.dot(h, w_bf[:, col0:], preferred_element_type=F32) + b_ref[0, :, col0:]
        off = 0
        for o_ref, (width, scale) in zip(o_refs, splits):
            part = u[:, off:off + width]
            if scale != 1.0:
                part = part * scale
            o_ref[0, rows, :] = part.astype(o_ref.dtype)
            off += width


def _inproj(x, sh, sc, w, b, layer, col0, splits, dtypes, tm):
    nb, s, d = x.shape
    n = w.shape[2]
    return pl.pallas_call(
        functools.partial(_inproj_kernel, splits=splits, col0=col0),
        out_shape=[jax.ShapeDtypeStruct((nb, s, wd), dt) for (wd, _), dt in zip(splits, dtypes)],
        grid=(nb, s // tm),
        in_specs=[
            pl.BlockSpec((1, tm, d), lambda bi, i: (bi, i, 0)),
            _mod_spec(sh, d),
            _mod_spec(sc, d),
            pl.BlockSpec((1, d, n), lambda bi, i: (layer, 0, 0), pipeline_mode=pl.Buffered(1)),
            pl.BlockSpec((1, 1, n), lambda bi, i: (layer, 0, 0)),
        ],
        out_specs=[pl.BlockSpec((1, tm, wd), lambda bi, i: (bi, i, 0)) for wd, _ in splits],
        scratch_shapes=[pltpu.VMEM((d, n), BF16)],
        compiler_params=_cparams(("arbitrary", "arbitrary")),
        name="inproj",
    )(x, sh[0], sc[0], w, b)


CONV_CHUNK = 128
Z_PAD = 8
H_PAD = 16


def _conv_kernel(ua_ref, ub_ref, ws_ref, wd_ref, bd_ref, g_ref, b_ref, o_ref, z_scr, h_scr, *, seq):
    c = D_CONV
    z_scr[0:Z_PAD, :] = jnp.zeros((Z_PAD, c), F32)
    z_scr[Z_PAD + seq:2 * Z_PAD + seq, :] = jnp.zeros((Z_PAD, c), F32)
    h_scr[0, 0:H_PAD, :] = jnp.zeros((H_PAD, c), F32)
    h_scr[0, H_PAD + seq:2 * H_PAD + seq, :] = jnp.zeros((H_PAD, c), F32)
    z_scr[Z_PAD:Z_PAD + seq, :] = ua_ref[0, :, c:2 * c] * ua_ref[0, :, 2 * c:3 * c]
    h_scr[0, H_PAD:H_PAD + seq, :] = ub_ref[0, :, 0:c] * _sigmoid(ub_ref[0, :, c:2 * c])
    n_rows = seq + 2 * H_PAD
    h_all = h_scr[0]
    for r in range(1, SUBLANES):
        h_scr[r] = pltpu.roll(h_all, n_rows - r, axis=0)
    tc = min(CONV_CHUNK, seq)
    for ci in range(seq // tc):
        t0 = ci * tc
        acc = ws_ref[0:1, :] * z_scr[t0 + Z_PAD - 1:t0 + Z_PAD - 1 + tc, :]
        for j in range(1, SHORT_CONV_W):
            s0 = t0 + Z_PAD - 1 + j
            acc = acc + ws_ref[j:j + 1, :] * z_scr[s0:s0 + tc, :]
        ya = ua_ref[0, t0:t0 + tc, 0:c] * acc
        hb = bd_ref[...]
        for j in range(CONF_CONV_W):
            s0 = t0 + H_PAD - CONF_CONV_W // 2 + j
            a0 = s0 - s0 % SUBLANES
            hb = hb + wd_ref[j:j + 1, :] * h_scr[s0 % SUBLANES, a0:a0 + tc, :]
        hn = _ln(hb) * g_ref[...] + b_ref[...]
        yb = hn * _sigmoid(hn)
        o_ref[0, t0:t0 + tc, 0:c] = ya.astype(o_ref.dtype)
        o_ref[0, t0:t0 + tc, c:2 * c] = yb.astype(o_ref.dtype)


def _conv_mixers(ua, ub, w_short, w_dw, b_dw, g_ln, b_ln):
    nb, s, _ = ua.shape
    c = D_CONV
    full = lambda shape: pl.BlockSpec(shape, lambda bi: (0,) * len(shape))
    return pl.pallas_call(
        functools.partial(_conv_kernel, seq=s),
        out_shape=jax.ShapeDtypeStruct((nb, s, 2 * c), BF16),
        grid=(nb,),
        in_specs=[
            pl.BlockSpec((1, s, 3 * c), lambda bi: (bi, 0, 0)),
            pl.BlockSpec((1, s, 2 * c), lambda bi: (bi, 0, 0)),
            full((SHORT_CONV_W, c)), full((CONF_CONV_W, c)), full((1, c)), full((1, c)), full((1, c)),
        ],
        out_specs=pl.BlockSpec((1, s, 2 * c), lambda bi: (bi, 0, 0)),
        scratch_shapes=[pltpu.VMEM((s + 2 * Z_PAD, c), F32), pltpu.VMEM((SUBLANES, s + 2 * H_PAD, c), F32)],
        compiler_params=_cparams(("arbitrary",)),
        name="conv_mixers",
    )(ua, ub, w_short, w_dw, b_dw.reshape(1, c), g_ln.reshape(1, c), b_ln.reshape(1, c))


SUB_ROWS = 2
WIN_ROWS = SUB_ROWS + NA_WIN_ROWS_MAX - 1
BAND_KEYS = WIN_ROWS * GRID_W
SUB_Q = SUB_ROWS * GRID_W
N_DROW = 2 * NA_WIN_ROWS_MAX - 1
N_DCOL = 2 * NA_WIN_COLS - 1


def _bias_kernel(rpb_ref, o_ref):
    n_rows, n_cols = o_ref.shape
    col = lax.broadcasted_iota(jnp.int32, (LANES, n_cols), 1)
    qc = col >> 6
    kc = col & (GRID_W - 1)
    d_col = jnp.clip(kc - qc + (NA_WIN_COLS - 1), 0, N_DCOL - 1)
    onehot = (lax.broadcasted_iota(jnp.int32, (LANES, n_cols), 0) == d_col).astype(F32)
    vals = jnp.dot(rpb_ref[...], onehot, preferred_element_type=F32, precision=HI)
    col_r = lax.broadcasted_iota(jnp.int32, (n_rows, n_cols), 1)
    qc_r = col_r >> 6
    kc_r = col_r & (GRID_W - 1)
    c0 = jnp.clip(qc_r - NA_WIN_COLS // 2, 0, GRID_W - NA_WIN_COLS)
    inside = (kc_r >= c0) & (kc_r < c0 + NA_WIN_COLS)
    o_ref[...] = jnp.where(inside, vals * LOG2E, NEG_INF)


N_SLABS = -(-WIN_ROWS // 2)
TILES_PER_KIND = N_NA_HEADS * N_DROW
MASKED_TILE = 3 * TILES_PER_KIND


def _na_plan(rows):
    wr = min(NA_WIN_ROWS_MAX, rows)
    assert wr == NA_WIN_ROWS_MAX and rows % SUB_ROWS == 0 and rows >= WIN_ROWS
    row_start = np.clip(np.arange(rows) - wr // 2, 0, rows - wr)
    w0s, tiles = [], []
    for r0 in range(0, rows, SUB_ROWS):
        w0 = int(np.clip(r0 - wr // 2, 0, rows - WIN_ROWS))
        for iq in range(SUB_ROWS):
            r = r0 + iq
            assert row_start[r] >= w0 and row_start[r] + wr <= w0 + WIN_ROWS
            ok = lambda w: w < WIN_ROWS and row_start[r] <= w0 + w < row_start[r] + wr
            d_row = lambda w: w0 + w - r + NA_WIN_ROWS_MAX - 1
            for j in range(N_SLABS):
                lo, hi = ok(2 * j), ok(2 * j + 1)
                if lo and hi:
                    tiles.append(d_row(2 * j))
                elif lo:
                    tiles.append(TILES_PER_KIND + d_row(2 * j))
                elif hi:
                    tiles.append(2 * TILES_PER_KIND + d_row(2 * j + 1))
                else:
                    tiles.append(-1)
        w0s.append(w0)
    return np.array(w0s, np.int32), np.array(tiles, np.int32)


def _na_bias(rpb):
    assert GRID_W == 64 and 2 * GRID_W == LANES and TILES_PER_KIND <= LANES and N_DCOL <= LANES
    rpb2 = jnp.pad(rpb.astype(F32).reshape(TILES_PER_KIND, N_DCOL),
                   ((0, LANES - TILES_PER_KIND), (0, LANES - N_DCOL)))
    table = pl.pallas_call(
        _bias_kernel,
        out_shape=jax.ShapeDtypeStruct((LANES, GRID_W * GRID_W), F32),
        name="na_bias",
    )(rpb2)
    table = table[:TILES_PER_KIND].reshape(N_NA_HEADS, N_DROW, GRID_W, GRID_W)
    masked = jnp.full_like(table, NEG_INF)
    nxt = jnp.concatenate([table[:, 1:], masked[:, :1]], axis=1)
    kinds = [jnp.concatenate(pair, axis=-1).reshape(TILES_PER_KIND, GRID_W, LANES)
             for pair in ((table, nxt), (table, masked), (masked, table))]
    return jnp.concatenate(kinds + [jnp.full((1, GRID_W, LANES), NEG_INF, F32)], axis=0)


def _lane_reduce(xs, combine, reduce, neutral):
    chunks = []
    for x in xs:
        rows, n = x.shape
        n_full = n // LANES
        chunks += [x[:, j * LANES:(j + 1) * LANES] for j in range(n_full)]
        if n % LANES:
            fill = jnp.full((rows, LANES - n % LANES), neutral, x.dtype)
            chunks.append(jnp.concatenate([x[:, n_full * LANES:], fill], axis=1))
    return reduce(functools.reduce(combine, chunks), axis=-1, keepdims=True)


def _attn_kernel(w0_ref, tile_ref, q_ref, k_ref, v_ref, kc_ref, vc_ref, *rest, banded, n_sub, sub_q):
    if banded:
        bias_ref, o_ref = rest
    else:
        (o_ref,) = rest
    lane = lax.broadcasted_iota(jnp.int32, (sub_q, LANES), 1)
    first = lane < NA_HEAD_DIM
    nt = (((1,), (1,)), ((), ()))
    stages = [(si, p) for si in range(n_sub) for p in range(N_PAIRS)]

    def window(si):
        blk = pl.program_id(1) * n_sub + si
        return blk, pl.multiple_of(w0_ref[blk] * GRID_W, GRID_W)

    def bias(blk, head):
        row_blocks = []
        for iq in range(SUB_ROWS):
            slabs = []
            for j in range(N_SLABS):
                t = tile_ref[(blk * SUB_ROWS + iq) * N_SLABS + j]
                tile = bias_ref[jnp.where(t < 0, MASKED_TILE, t + head * N_DROW)]
                width = min(LANES, BAND_KEYS - j * LANES)
                slabs.append(tile[:, :width])
            row_blocks.append(jnp.concatenate(slabs, axis=1))
        return jnp.concatenate(row_blocks, axis=0)

    def scores(si, p):
        cols = slice(p * LANES, (p + 1) * LANES)
        q_p = q_ref[0, si * sub_q:(si + 1) * sub_q, cols]
        zero = jnp.zeros_like(q_p)
        qq = jnp.concatenate([jnp.where(first, q_p, zero), jnp.where(first, zero, q_p)], axis=0)
        parts = [lax.dot_general(qq, kc_ref[0, :, cols], nt, preferred_element_type=F32)]
        if banded:
            blk, start = window(si)
            both = jnp.concatenate([bias(blk, 2 * p), bias(blk, 2 * p + 1)], axis=0)
            parts.append(lax.dot_general(qq, k_ref[0, pl.ds(start, BAND_KEYS), cols], nt,
                                         preferred_element_type=F32) + both)
        return parts

    def finish(si, p, parts):
        cols = slice(p * LANES, (p + 1) * LANES)
        m = _lane_reduce(parts, jnp.maximum, jnp.max, NEG_INF)
        es = [jnp.exp2(s - m) for s in parts]
        den = _lane_reduce(es, jnp.add, jnp.sum, 0.0)
        o = jnp.dot(es[0].astype(BF16), vc_ref[0, :, cols], preferred_element_type=F32)
        if banded:
            _, start = window(si)
            o = o + jnp.dot(es[1].astype(BF16), v_ref[0, pl.ds(start, BAND_KEYS), cols],
                            preferred_element_type=F32)
        o = o * (1.0 / den)
        out = jnp.where(first, o[:sub_q], o[sub_q:])
        o_ref[0, si * sub_q:(si + 1) * sub_q, cols] = out.astype(o_ref.dtype)

    nxt = scores(*stages[0])
    for i, (si, p) in enumerate(stages):
        cur = nxt
        if i + 1 < len(stages):
            nxt = scores(*stages[i + 1])
        finish(si, p, cur)


def _neighbourhood_attention(q, k, v, kc, vc, rpb):
    nb, s, dn = q.shape
    rows = s // GRID_W
    nctx = kc.shape[1]
    w0s, tiles = _na_plan(rows)
    bias = _na_bias(rpb)
    n_sub = ATTN_ROWS // SUB_ROWS
    m_rows = ATTN_ROWS * GRID_W
    grid_spec = pltpu.PrefetchScalarGridSpec(
        num_scalar_prefetch=2,
        grid=(nb, rows // ATTN_ROWS),
        in_specs=[
            pl.BlockSpec((1, m_rows, dn), lambda bi, i, w0, pat: (bi, i, 0)),
            pl.BlockSpec((1, s, dn), lambda bi, i, w0, pat: (bi, 0, 0)),
            pl.BlockSpec((1, s, dn), lambda bi, i, w0, pat: (bi, 0, 0)),
            pl.BlockSpec((1, nctx, dn), lambda bi, i, w0, pat: (bi, 0, 0)),
            pl.BlockSpec((1, nctx, dn), lambda bi, i, w0, pat: (bi, 0, 0)),
            pl.BlockSpec(bias.shape, lambda bi, i, w0, pat: (0, 0, 0), pipeline_mode=pl.Buffered(1)),
        ],
        out_specs=pl.BlockSpec((1, m_rows, dn), lambda bi, i, w0, pat: (bi, i, 0)),
    )
    return pl.pallas_call(
        functools.partial(_attn_kernel, banded=True, n_sub=n_sub, sub_q=SUB_Q),
        out_shape=jax.ShapeDtypeStruct((nb, s, dn), BF16),
        grid_spec=grid_spec,
        compiler_params=_cparams(("arbitrary", "arbitrary")),
        name="neighbourhood_attention",
    )(jnp.asarray(w0s), jnp.asarray(tiles), q, k, v, kc, vc, bias)


def _context_attention(q, kc, vc):
    nb, s, dn = q.shape
    spec = pl.BlockSpec((1, s, dn), lambda bi, i, w0, pat: (bi, 0, 0))
    grid_spec = pltpu.PrefetchScalarGridSpec(
        num_scalar_prefetch=2, grid=(nb, 1), in_specs=[spec] * 5, out_specs=spec)
    dummy = jnp.zeros((1,), jnp.int32)
    return pl.pallas_call(
        functools.partial(_attn_kernel, banded=False, n_sub=1, sub_q=s),
        out_shape=jax.ShapeDtypeStruct((nb, s, dn), BF16),
        grid_spec=grid_spec,
        compiler_params=_cparams(("arbitrary", "arbitrary")),
        name="context_attention",
    )(dummy, dummy, q, kc, vc, kc, vc)


def _store_rowmajor(ref, val, base=0):
    n, width = val.shape
    chunks = width // LANES
    for c in range(chunks):
        ref[pl.ds(base + c, n, stride=chunks), :] = val[:, c * LANES:(c + 1) * LANES]


def _load_rowmajor(ref, base, n, chunks):
    return jnp.concatenate([ref[pl.ds(base + c, n, stride=chunks), :] for c in range(chunks)], axis=1)


def _outproj_kernel(yab_ref, yc_ref, x_ref, w_ref, bo_ref, gt_ref, g_ref, b_ref, sh_ref, sc_ref,
                    wrh_ref, wrl_ref, xmid_ref, hm_ref, lg_ref, w_bf):
    @pl.when(_first_step())
    def _():
        w_bf[...] = w_ref[0].astype(BF16)

    half = yab_ref.shape[2]
    tm = x_ref.shape[1]
    chunks = x_ref.shape[2] // LANES
    n_part = max(1, tm // OUT_ROW_GROUP)
    rows_per = tm // n_part
    groups = [slice(part * rows_per, (part + 1) * rows_per) for part in range(n_part)]
    ys = [jnp.dot(yab_ref[0, rows, :], w_bf[0:half, :], preferred_element_type=F32)
          + jnp.dot(yc_ref[0, rows, :], w_bf[half:, :], preferred_element_type=F32) + bo_ref[...]
          for rows in groups]
    for part, (rows, y) in enumerate(zip(groups, ys)):
        xm = _ln(DEEPNORM_ALPHA * x_ref[0, rows, :] + (1.0 + gt_ref[0]) * y) * g_ref[...] + b_ref[...]
        xmid_ref[0, rows, :] = xm
        hm = _ln(xm) * (1.0 + sc_ref[0]) + sh_ref[0]
        _store_rowmajor(hm_ref, hm, base=part * rows_per * chunks)
        hm_hi = hm.astype(BF16)
        hm_lo = (hm - hm_hi.astype(F32)).astype(BF16)
        lg_ref[0, rows, :] = (jnp.dot(hm_hi, wrh_ref[...], preferred_element_type=F32)
                              + jnp.dot(hm_lo, wrh_ref[...], preferred_element_type=F32)
                              + jnp.dot(hm_hi, wrl_ref[...], preferred_element_type=F32))


def _outproj(yab, yc, x, w, layer, bo, gt, g, b, sh, sc, wr_hi, wr_lo, tm):
    nb, s, d = x.shape
    half = yab.shape[2]
    vec = pl.BlockSpec((1, d), lambda bi, i: (0, 0))
    tok = lambda width: pl.BlockSpec((1, tm, width), lambda bi, i: (bi, i, 0))
    n_i = s // tm
    return pl.pallas_call(
        _outproj_kernel,
        out_shape=[jax.ShapeDtypeStruct((nb, s, d), F32),
                   jax.ShapeDtypeStruct((nb * s * (d // LANES), LANES), F32),
                   jax.ShapeDtypeStruct((nb, s, LANES), F32)],
        grid=(nb, n_i),
        in_specs=[tok(half), tok(half), tok(d),
                  pl.BlockSpec((1, d, d), lambda bi, i: (layer, 0, 0), pipeline_mode=pl.Buffered(1)),
                  vec, _mod_spec(gt, d), vec, vec,
                  _mod_spec(sh, d), _mod_spec(sc, d), pl.BlockSpec((d, LANES), lambda bi, i: (0, 0)),
                  pl.BlockSpec((d, LANES), lambda bi, i: (0, 0))],
        out_specs=[tok(d), pl.BlockSpec((tm * (d // LANES), LANES), lambda bi, i: (bi * n_i + i, 0)),
                   tok(LANES)],
        scratch_shapes=[pltpu.VMEM((d, d), BF16)],
        compiler_params=_cparams(("arbitrary", "arbitrary")),
        name="outproj_postnorm",
    )(yab, yc, x, w, bo, gt[0], g, b, sh[0], sc[0], wr_hi, wr_lo)


CUM_CHUNK = 256
F32_EXP_BIAS = 127
F32_MANT_BITS = 23


def _prefix_count(mask_f32, tri):
    rows, n = mask_f32.shape
    tc = min(CUM_CHUNK, n)
    base = jnp.zeros((rows, 1), F32)
    parts = []
    for ci in range(n // tc):
        blk = mask_f32[:, ci * tc:(ci + 1) * tc]
        parts.append(jnp.dot(blk.astype(BF16), tri[:tc, :tc], preferred_element_type=F32) + base)
        base = base + jnp.sum(blk, axis=-1, keepdims=True)
    return jnp.concatenate(parts, axis=-1)


def _pow2(k):
    return pltpu.bitcast((k + F32_EXP_BIAS) << F32_MANT_BITS, F32)


def _route_kernel(lg_ref, slot_c_ref, gate_t_ref, *, cap, slot_stride):
    nb = lg_ref.shape[0]
    assert nb * N_EXPERTS == LANES
    rows = []
    for b in range(nb):
        lg = lg_ref[b]
        lane = lax.broadcasted_iota(jnp.int32, lg.shape, 1)
        lgm = jnp.where(lane < N_EXPERTS, lg, NEG_INF)
        ex = jnp.exp(lgm - jnp.max(lgm, axis=-1, keepdims=True))
        aff = ex / jnp.sum(ex, axis=-1, keepdims=True)
        rows.append(aff.T[0:N_EXPERTS, :])
    a = jnp.concatenate(rows, axis=0)
    capf = float(cap)

    def enough(t):
        return jnp.sum((a >= t).astype(F32), axis=-1, keepdims=True) >= capf

    def exp_step(_, carry):
        lo, hi = carry
        mid = lo + ((hi - lo + 1) >> 1)
        ok = enough(_pow2(mid))
        return jnp.where(ok, mid, lo), jnp.where(ok, hi, mid - 1)

    k_lo = jnp.full((LANES, 1), -F32_EXP_BIAS, jnp.int32)
    k_hi = jnp.zeros((LANES, 1), jnp.int32)
    k_lo, _ = lax.fori_loop(0, 7, exp_step, (k_lo, k_hi))
    base = _pow2(k_lo)

    def mant_step(_, carry):
        t, step = carry
        step = step * 0.5
        cand = t + step
        return jnp.where(enough(cand), cand, t), step

    thr, _ = lax.fori_loop(0, F32_MANT_BITS, mant_step, (base, base))

    r_i = lax.broadcasted_iota(jnp.int32, (CUM_CHUNK, CUM_CHUNK), 0)
    c_i = lax.broadcasted_iota(jnp.int32, (CUM_CHUNK, CUM_CHUNK), 1)
    tri = (r_i < c_i).astype(BF16)
    gt = (a > thr).astype(F32)
    eq = (a == thr).astype(F32)
    need = capf - jnp.sum(gt, axis=-1, keepdims=True)
    sel = gt + eq * (_prefix_count(eq, tri) < need).astype(F32)
    pos = _prefix_count(sel, tri)
    sample = lax.broadcasted_iota(jnp.int32, (LANES, 1), 0) >> (N_EXPERTS.bit_length() - 1)
    slot = jnp.where(sel > 0.0, pos + (sample * slot_stride).astype(F32), -1.0)
    for b in range(nb):
        lo = b * N_EXPERTS
        gate_t_ref[b] = a[lo:lo + N_EXPERTS, :]
        rolled = slot if b == 0 else jnp.concatenate([slot[lo:, :], slot[:lo, :]], axis=0)
        slot_c_ref[b] = rolled.T


def _route(logits, cap, slot_stride):
    nb, s, _ = logits.shape
    whole = lambda shape: pl.BlockSpec(shape, lambda i: (0,) * len(shape))
    return pl.pallas_call(
        functools.partial(_route_kernel, cap=cap, slot_stride=slot_stride),
        out_shape=[jax.ShapeDtypeStruct((nb, s, LANES), F32), jax.ShapeDtypeStruct((nb, N_EXPERTS, s), F32)],
        grid=(1,),
        in_specs=[whole((nb, s, LANES))],
        out_specs=[whole((nb, s, LANES)), whole((nb, N_EXPERTS, s))],
        compiler_params=_cparams(("arbitrary",)),
        name="route",
    )(logits)


TOK_SPLIT = 64


def _slot_list_kernel(slot_ref, gate_ref, idx_ref, g_ref, *, n_slots, tok_stride, merge):
    s = slot_ref.shape[1]
    b = pl.program_id(0)
    assert n_slots <= 256
    slot_id = lax.broadcasted_iota(jnp.int32, (s, n_slots), 1).astype(F32).astype(BF16)
    one, zero = jnp.ones((s, n_slots), BF16), jnp.zeros((s, n_slots), BF16)
    tok = lax.broadcasted_iota(jnp.int32, (1, s), 1) + b * tok_stride
    tok_hi = (tok >> (TOK_SPLIT.bit_length() - 1)).astype(F32)
    tok_lo = (tok & (TOK_SPLIT - 1)).astype(F32)
    zeros = jnp.zeros((SUBLANES - 5, s), F32)
    idx_rows, g_rows = [], []
    for e in range(N_EXPERTS):
        taken = jnp.broadcast_to(slot_ref[0, :, e:e + 1].astype(BF16), (s, n_slots))
        hit = jnp.where(taken == slot_id, one, zero)
        g0 = gate_ref[0, e:e + 1, :]
        g_hi = g0.astype(BF16).astype(F32)
        g_mid = (g0 - g_hi).astype(BF16).astype(F32)
        g_lo = g0 - g_hi - g_mid
        lhs = jnp.concatenate([tok_hi, tok_lo, g_hi, g_mid, g_lo, zeros], axis=0).astype(BF16)
        out = jnp.dot(lhs, hit, preferred_element_type=F32)
        idx_rows.append(out[0:1] * float(TOK_SPLIT) + out[1:2])
        g_rows.append(out[2:3] + out[3:4] + out[4:5])
    idx = jnp.concatenate(idx_rows, axis=0).astype(jnp.int32) * SUBLANES
    g = jnp.concatenate(g_rows + [jnp.zeros((LANES - N_EXPERTS, n_slots), F32)], axis=0)
    g_t = g.T
    g_cols = [jnp.broadcast_to(g_t[:, e:e + 1], (n_slots, LANES)) for e in range(N_EXPERTS)]
    if merge:
        @pl.when(b == 0)
        def _():
            idx_ref[0] = idx
            for e in range(N_EXPERTS):
                g_ref[e] = g_cols[e]

        @pl.when(b > 0)
        def _():
            idx_ref[0] = idx_ref[0] + idx
            for e in range(N_EXPERTS):
                g_ref[e] = g_ref[e] + g_cols[e]
    else:
        idx_ref[0] = idx
        for e in range(N_EXPERTS):
            g_ref[e] = g_cols[e]


def _slot_lists(slot_c, gate_t, n_slots, merge):
    nb, s, _ = slot_c.shape
    nbo = 1 if merge else nb
    idx_map = (lambda bi: (0, 0, 0)) if merge else (lambda bi: (bi, 0, 0))
    g_map = (lambda bi: (0, 0, 0)) if merge else (lambda bi: (0, bi, 0))
    return pl.pallas_call(
        functools.partial(_slot_list_kernel, n_slots=n_slots, tok_stride=s if merge else 0, merge=merge),
        out_shape=[jax.ShapeDtypeStruct((nbo, N_EXPERTS, n_slots), jnp.int32),
                   jax.ShapeDtypeStruct((N_EXPERTS, nbo * n_slots, LANES), F32)],
        grid=(nb,),
        in_specs=[pl.BlockSpec((1, s, LANES), lambda bi: (bi, 0, 0)),
                  pl.BlockSpec((1, N_EXPERTS, s), lambda bi: (bi, 0, 0))],
        out_specs=[pl.BlockSpec((1, N_EXPERTS, n_slots), idx_map),
                   pl.BlockSpec((N_EXPERTS, n_slots, LANES), g_map)],
        compiler_params=_cparams(("arbitrary",)),
        name="slot_lists",
    )(slot_c, gate_t)


ROW_UNROLL = 32


def _dispatch_kernel(idx_ref, hm_ref, x_ref, rows_scr, *, n_slots):
    b = pl.program_id(0)
    chunks = x_ref.shape[2] // LANES
    for e in range(N_EXPERTS):
        base = (b * N_EXPERTS + e) * n_slots

        def move(i, carry):
            s0 = i * ROW_UNROLL
            first = base + s0
            tiles = [hm_ref[pl.ds(pl.multiple_of(idx_ref[first + u], chunks), chunks), :]
                     for u in range(ROW_UNROLL)]
            dst = pl.multiple_of(s0 * chunks, ROW_UNROLL * chunks)
            rows_scr[pl.ds(dst, ROW_UNROLL * chunks), :] = jnp.concatenate(tiles, axis=0)
            return carry

        lax.fori_loop(0, n_slots // ROW_UNROLL, move, 0)
        x_ref[e] = _load_rowmajor(rows_scr, 0, n_slots, chunks).astype(x_ref.dtype)


def _dispatch(idx, hm_rm, nb, s, d, n_slots):
    chunks = d // LANES
    assert chunks == SUBLANES and n_slots % ROW_UNROLL == 0
    grid_spec = pltpu.PrefetchScalarGridSpec(
        num_scalar_prefetch=1,
        grid=(nb,),
        in_specs=[pl.BlockSpec((s * chunks, LANES), lambda bi, idx_r: (bi, 0))],
        out_specs=pl.BlockSpec((N_EXPERTS, n_slots, d), lambda bi, idx_r: (0, bi, 0)),
        scratch_shapes=[pltpu.VMEM((n_slots * chunks, LANES), F32)],
    )
    return pl.pallas_call(
        functools.partial(_dispatch_kernel, n_slots=n_slots),
        out_shape=jax.ShapeDtypeStruct((N_EXPERTS, nb * n_slots, d), BF16),
        grid_spec=grid_spec,
        compiler_params=_cparams(("arbitrary",)),
        name="dispatch",
    )(idx, hm_rm)


W_PARTS = 4

def _expert_kernel(*refs, n_main, has_extra):
    n_in = 4 if has_extra else 2
    n_out = 2 if has_extra else 1
    acts, w_parts = refs[:n_in], refs[n_in:n_in + 3 * W_PARTS]
    outs = refs[n_in + 3 * W_PARTS:n_in + 3 * W_PARTS + n_out]
    wg_bf, wu_bf, wd_bf = refs[n_in + 3 * W_PARTS + n_out:]
    if has_extra:
        x_ref, g_ref, xx_ref, gg_ref = acts
        y_ref, yy_ref = outs
    else:
        x_ref, g_ref = acts
        (y_ref,) = outs
    i = pl.program_id(1)

    @pl.when(i == 0)
    def _():
        for k, w_bf in enumerate((wg_bf, wu_bf, wd_bf)):
            for j in range(W_PARTS):
                part = w_parts[k * W_PARTS + j]
                rows = part.shape[2]
                w_bf[j * rows:(j + 1) * rows, :] = part[0, 0].astype(BF16)

    def ffn(x_in, gate_in, out_ref):
        rows_all = x_in.shape[1]
        chunks = x_in.shape[2] // LANES
        n_part = max(1, rows_all // ROW_GROUP)
        rows_per = rows_all // n_part
        groups = [slice(p * rows_per, (p + 1) * rows_per) for p in range(n_part)]
        ups = []
        for rows in groups:
            x_e = x_in[0, rows, :]
            ups.append((jnp.dot(x_e, wg_bf[...], preferred_element_type=F32),
                        jnp.dot(x_e, wu_bf[...], preferred_element_type=F32)))
        for p, (rows, (a, u)) in enumerate(zip(groups, ups)):
            h = (a * _sigmoid(a) * u).astype(BF16)
            y = jnp.dot(h, wd_bf[...], preferred_element_type=F32) * gate_in[0, rows, 0:1]
            _store_rowmajor(out_ref, y, base=p * rows_per * chunks)

    if has_extra:
        @pl.when(i < n_main)
        def _():
            ffn(x_ref, g_ref, y_ref.at[0])

        @pl.when(i == n_main)
        def _():
            ffn(xx_ref, gg_ref, yy_ref.at[0])
    else:
        ffn(x_ref, g_ref, y_ref.at[0])


def _experts(x_e, g_e, x_extra, g_extra, wg, wu, wd, layer, tm):
    _, rows, d = x_e.shape
    f = wg.shape[3]
    chunks = d // LANES
    n_main = rows // tm
    has_extra = x_extra is not None
    main_map = lambda e, i: (e, jnp.minimum(i, n_main - 1), 0)

    def w_specs(shape):
        assert shape[0] % W_PARTS == 0 and n_main >= W_PARTS
        blk = (1, 1, shape[0] // W_PARTS, shape[1])
        return [pl.BlockSpec(blk, functools.partial(
            lambda e, i, j: (layer, jnp.minimum(e + (i > j).astype(jnp.int32), N_EXPERTS - 1), j, 0), j=j))
            for j in range(W_PARTS)]

    in_specs = [pl.BlockSpec((1, tm, d), main_map), pl.BlockSpec((1, tm, LANES), main_map)]
    out_specs = [pl.BlockSpec((1, tm * chunks, LANES), main_map)]
    out_shape = [jax.ShapeDtypeStruct((N_EXPERTS, rows * chunks, LANES), F32)]
    args = [x_e, g_e]
    if has_extra:
        rows2 = x_extra.shape[1]
        in_specs += [pl.BlockSpec((1, rows2, d), lambda e, i: (e, 0, 0)),
                     pl.BlockSpec((1, rows2, LANES), lambda e, i: (e, 0, 0))]
        out_specs.append(pl.BlockSpec((1, rows2 * chunks, LANES), lambda e, i: (e, 0, 0)))
        out_shape.append(jax.ShapeDtypeStruct((N_EXPERTS, rows2 * chunks, LANES), F32))
        args += [x_extra, g_extra]
    outs = pl.pallas_call(
        functools.partial(_expert_kernel, n_main=n_main, has_extra=has_extra),
        out_shape=out_shape,
        grid=(N_EXPERTS, n_main + int(has_extra)),
        in_specs=in_specs + w_specs((d, f)) + w_specs((d, f)) + w_specs((f, d)),
        out_specs=out_specs,
        scratch_shapes=[pltpu.VMEM((d, f), BF16), pltpu.VMEM((d, f), BF16), pltpu.VMEM((f, d), BF16)],
        compiler_params=_cparams(("arbitrary", "arbitrary")),
        name="experts",
    )(*args, *([wg] * W_PARTS), *([wu] * W_PARTS), *([wd] * W_PARTS))
    return outs if has_extra else (outs[0], None)


COMBINE_EXPERTS = 4
ADD_UNROLL = 16
NORM_CHUNK = 256


def _combine_kernel(idx_ref, y_ref, x_ref, gt_ref, g_ref, b_ref, o_ref, acc, *, n_slots):
    b = pl.program_id(0)
    j = pl.program_id(1)
    s, d = x_ref.shape[1], x_ref.shape[2]
    chunks = d // LANES

    @pl.when(j == 0)
    def _():
        acc[...] = jnp.zeros(acc.shape, F32)

    for eg in range(COMBINE_EXPERTS):
        base = (b * N_EXPERTS + j * COMBINE_EXPERTS + eg) * n_slots

        def add(i, carry):
            s0 = i * ADD_UNROLL
            first = base + s0
            src = pl.multiple_of(s0 * chunks, ADD_UNROLL * chunks)
            y_rows = y_ref[eg, pl.ds(src, ADD_UNROLL * chunks), :]
            new = []
            for u in range(ADD_UNROLL):
                dst = pl.multiple_of(idx_ref[first + u], chunks)
                new.append((dst, acc[pl.ds(dst, chunks), :] + y_rows[u * chunks:(u + 1) * chunks, :]))
            for dst, val in new:
                acc[pl.ds(dst, chunks), :] = val
            return carry

        lax.fori_loop(0, n_slots // ADD_UNROLL, add, 0)

    @pl.when(j == pl.num_programs(1) - 1)
    def _():
        for ci in range(s // NORM_CHUNK):
            rows = slice(ci * NORM_CHUNK, (ci + 1) * NORM_CHUNK)
            ym = _load_rowmajor(acc, ci * NORM_CHUNK * chunks, NORM_CHUNK, chunks)
            z = DEEPNORM_ALPHA * x_ref[0, rows, :] + (1.0 + gt_ref[0]) * ym
            o_ref[0, rows, :] = _ln(z) * g_ref[...] + b_ref[...]


def _combine(idx, y_rm, x_mid, gt, g, b, n_slots):
    nb, s, d = x_mid.shape
    chunks = d // LANES
    assert chunks == SUBLANES and N_EXPERTS % COMBINE_EXPERTS == 0 and s % NORM_CHUNK == 0
    vec = pl.BlockSpec((1, d), lambda bi, j, i_r: (0, 0))
    grid_spec = pltpu.PrefetchScalarGridSpec(
        num_scalar_prefetch=1,
        grid=(nb, N_EXPERTS // COMBINE_EXPERTS),
        in_specs=[
            pl.BlockSpec((COMBINE_EXPERTS, n_slots * chunks, LANES), lambda bi, j, i_r: (j, bi, 0)),
            pl.BlockSpec((1, s, d), lambda bi, j, i_r: (bi, 0, 0)),
            _mod_spec(gt, d), vec, vec,
        ],
        out_specs=pl.BlockSpec((1, s, d), lambda bi, j, i_r: (bi, 0, 0)),
        scratch_shapes=[pltpu.VMEM((s * chunks, LANES), F32)],
    )
    return pl.pallas_call(
        functools.partial(_combine_kernel, n_slots=n_slots),
        out_shape=jax.ShapeDtypeStruct((nb, s, d), F32),
        grid_spec=grid_spec,
        compiler_params=_cparams(("arbitrary", "arbitrary")),
        name="combine_postnorm",
    )(idx, y_rm, x_mid, gt[0], g, b)


def kernel(x, c, ctx, c_ctx, w_mod, b_mod, w_in, b_in, w_short, w_conf_dw, b_conf_dw, g_conf_ln, b_conf_ln,
           na_rpb, w_out, b_out, g_post1, b_post1, w_router, w_gate, w_up, w_down, g_post2, b_post2):
    bsz, seq, d = x.shape
    nctx = ctx.shape[1]
    cap = EC_CAPACITY_FACTOR * seq // N_EXPERTS
    cap_ctx = EC_CAPACITY_FACTOR * nctx // N_EXPERTS
    q_scale = NA_HEAD_DIM ** -0.5 * LOG2E

    cond = jnp.concatenate([c, c_ctx[None, :], jnp.zeros((MOD_ROWS - bsz - 1, d), F32)], axis=0)
    mods = _modulation(cond, w_mod, b_mod)

    lat_splits = ((3 * D_CONV, 1.0), (2 * D_CONF, 1.0), (D_NA, q_scale), (D_NA, 1.0), (D_NA, 1.0))
    lat_dtypes = (F32, F32, BF16, BF16, BF16)
    kv_splits = ((D_NA, 1.0), (D_NA, 1.0))
    b_in3 = b_in[:, None, :]

    xc = ctx
    for l in range(DEPTH):
        last = l == DEPTH - 1
        m_lat = [(mods, (l * N_MOD + k) * MOD_ROWS, True) for k in range(N_MOD)]
        m_ctx = [(mods, (l * N_MOD + k) * MOD_ROWS + bsz, False) for k in range(N_MOD)]
        b_out_l = b_out[l][None, :]
        wr = jnp.pad(w_router[l], ((0, 0), (0, LANES - N_EXPERTS)))
        wr_hi = wr.astype(BF16)
        wr_lo = (wr - wr_hi.astype(F32)).astype(BF16)
        g1, b1 = g_post1[l][None, :], b_post1[l][None, :]
        g2, b2 = g_post2[l][None, :], b_post2[l][None, :]
        conv_w = (w_short[l], w_conf_dw[l], b_conf_dw[l], g_conf_ln[l], b_conf_ln[l])

        flat = lambda t: t.reshape(1, bsz * nctx, t.shape[-1])
        unflat = lambda t: t.reshape(bsz, nctx, t.shape[-1])
        if last:
            k_c, v_c = map(unflat, _inproj(flat(xc), m_ctx[0], m_ctx[1], w_in, b_in3, l, OFF_K,
                                           kv_splits, (BF16, BF16), tm=512))
        else:
            uac, ubc, q_c, k_c, v_c = map(unflat, _inproj(flat(xc), m_ctx[0], m_ctx[1], w_in, b_in3, l, 0,
                                                          lat_splits, lat_dtypes, tm=512))

        ua, ub, q, k, v = _inproj(x, m_lat[0], m_lat[1], w_in, b_in3, l, 0, lat_splits, lat_dtypes, tm=512)
        yab = _conv_mixers(ua, ub, *conv_w)
        yc = _neighbourhood_attention(q, k, v, k_c, v_c, na_rpb[l])
        x_mid, hm, logits = _outproj(yab, yc, x, w_out, l, b_out_l, m_lat[2], g1, b1, m_lat[3], m_lat[4],
                                     wr_hi, wr_lo, tm=512)

        idx, gates = _slot_lists(*_route(logits, cap, 0), cap, merge=False)
        idx = idx.reshape(-1)
        x_e = _dispatch(idx, hm, bsz, seq, d, cap)

        x_ec = gates_c = None
        if not last:
            yabc = _conv_mixers(uac, ubc, *conv_w)
            ycc = _context_attention(q_c, k_c, v_c)
            xc_mid, hmc, logits_c = _outproj(flat(yabc), flat(ycc), flat(xc), w_out, l, b_out_l, m_ctx[2], g1, b1,
                                             m_ctx[3], m_ctx[4], wr_hi, wr_lo, tm=512)
            n_c = bsz * cap_ctx
            idx_c, gates_c = _slot_lists(*_route(unflat(logits_c), cap_ctx, cap_ctx), n_c, merge=True)
            idx_c = idx_c.reshape(-1)
            x_ec = _dispatch(idx_c, hmc, 1, bsz * nctx, d, n_c)

        y_e, y_ec = _experts(x_e, gates, x_ec, gates_c, w_gate, w_up, w_down, l, tm=512)
        x = _combine(idx, y_e, x_mid, m_lat[5], g2, b2, cap)
        if not last:
            xc = unflat(_combine(idx_c, y_ec, xc_mid, m_ctx[5], g2, b2, n_c))
    return x
```

```python
import functools
import math

import numpy as np
import jax
import jax.numpy as jnp
from jax import lax
from jax.experimental import pallas as pl
from jax.experimental.pallas import tpu as pltpu

F32 = jnp.float32
BF16 = jnp.bfloat16

D_MODEL = 1024
DEPTH = 2
GRID_W = 64
D_CONV = D_MODEL // 4
D_CONF = D_MODEL // 4
NA_HEAD_DIM = 64
D_NA = D_MODEL - D_CONV - D_CONF
N_NA_HEADS = D_NA // NA_HEAD_DIM
SHORT_CONV_W = 3
CONF_CONV_W = 31
NA_WIN_ROWS_MAX = 8
NA_WIN_COLS = 16
N_EXPERTS = 16
EC_CAPACITY_FACTOR = 2
D_EXPERT = 1024
LN_EPS = 1e-5
DEEPNORM_ALPHA = (2.0 * DEPTH) ** 0.25
NEG_INF = -1e30
LOG2E = math.log2(math.e)

OFF_A = 0
OFF_B = OFF_A + 3 * D_CONV
OFF_Q = OFF_B + 2 * D_CONF
OFF_K = OFF_Q + D_NA
OFF_V = OFF_K + D_NA
D_IN = OFF_V + D_NA

LANES = 128
SUBLANES = 8
MOD_ROWS = 16
VMEM_LIMIT = 56 * 1024 * 1024
ATTN_ROWS = 8
N_PAIRS = D_NA // LANES
ROW_GROUP = 256
OUT_ROW_GROUP = 128
HI = lax.Precision.HIGHEST


def _cparams(sem):
    return pltpu.CompilerParams(dimension_semantics=sem, vmem_limit_bytes=VMEM_LIMIT)


def _ln(x):
    mu = jnp.mean(x, axis=-1, keepdims=True)
    xc = x - mu
    var = jnp.mean(xc * xc, axis=-1, keepdims=True)
    return xc * lax.rsqrt(var + LN_EPS)


def _sigmoid(x):
    return 1.0 / (1.0 + jnp.exp(-x))


def _mod_kernel(cond_ref, w_ref, b_ref, o_ref):
    s = cond_ref[...]
    s = s * _sigmoid(s)
    w = w_ref[0]
    s_hi, w_hi = s.astype(BF16), w.astype(BF16)
    s_lo = (s - s_hi.astype(F32)).astype(BF16)
    w_lo = (w - w_hi.astype(F32)).astype(BF16)
    o_ref[0] = (jnp.dot(s_hi, w_hi, preferred_element_type=F32) + jnp.dot(s_lo, w_hi, preferred_element_type=F32)
                + jnp.dot(s_hi, w_lo, preferred_element_type=F32) + b_ref[0])


N_MOD = 6


def _modulation(cond, w_mod, b_mod):
    n_l, d, n = w_mod.shape
    out = pl.pallas_call(
        _mod_kernel,
        out_shape=jax.ShapeDtypeStruct((n_l * N_MOD, MOD_ROWS, d), F32),
        grid=(n_l, N_MOD),
        in_specs=[
            pl.BlockSpec((MOD_ROWS, d), lambda l, k: (0, 0)),
            pl.BlockSpec((1, d, d), lambda l, k: (l, 0, k)),
            pl.BlockSpec((1, 1, d), lambda l, k: (l, 0, k)),
        ],
        out_specs=pl.BlockSpec((1, MOD_ROWS, d), lambda l, k: (l * N_MOD + k, 0, 0)),
        compiler_params=_cparams(("arbitrary", "arbitrary")),
        name="modulation",
    )(cond, w_mod, b_mod.reshape(n_l, 1, n))
    return out.reshape(n_l * N_MOD * MOD_ROWS, 1, d)


def _mod_spec(mod, d):
    _, row0, per_sample = mod
    if per_sample:
        return pl.BlockSpec((1, 1, d), lambda bi, *_: (row0 + bi, 0, 0))
    return pl.BlockSpec((1, 1, d), lambda bi, *_: (row0, 0, 0))


def _first_step():
    return (pl.program_id(0) == 0) & (pl.program_id(1) == 0)


def _inproj_kernel(x_ref, sh_ref, sc_ref, w_ref, b_ref, *rest, splits, col0):
    *o_refs, w_bf = rest

    @pl.when(_first_step())
    def _():
        w_bf[...] = w_ref[0].astype(BF16)

    tm = x_ref.shape[1]
    n_part = max(1, tm // ROW_GROUP)
    rows_per = tm // n_part
    groups = [slice(part_i * rows_per, (part_i + 1) * rows_per) for part_i in range(n_part)]
    hs = [(_ln(x_ref[0, rows, :]) * (1.0 + sc_ref[0]) + sh_ref[0]).astype(BF16) for rows in groups]
    for rows, h in zip(groups, hs):
        u = jnp.dot(h, w_bf[:, col0:], preferred_element_type=F32) + b_ref[0, :, col0:]
        off = 0
        for o_ref, (width, scale) in zip(o_refs, splits):
            part = u[:, off:off + width]
            if scale != 1.0:
                part = part * scale
            o_ref[0, rows, :] = part.astype(o_ref.dtype)
            off += width


def _inproj(x, sh, sc, w, b, layer, col0, splits, dtypes, tm):
    nb, s, d = x.shape
    n = w.shape[2]
    return pl.pallas_call(
        functools.partial(_inproj_kernel, splits=splits, col0=col0),
        out_shape=[jax.ShapeDtypeStruct((nb, s, wd), dt) for (wd, _), dt in zip(splits, dtypes)],
        grid=(nb, s // tm),
        in_specs=[
            pl.BlockSpec((1, tm, d), lambda bi, i: (bi, i, 0)),
            _mod_spec(sh, d),
            _mod_spec(sc, d),
            pl.BlockSpec((1, d, n), lambda bi, i: (layer, 0, 0), pipeline_mode=pl.Buffered(1)),
            pl.BlockSpec((1, 1, n), lambda bi, i: (layer, 0, 0)),
        ],
        out_specs=[pl.BlockSpec((1, tm, wd), lambda bi, i: (bi, i, 0)) for wd, _ in splits],
        scratch_shapes=[pltpu.VMEM((d, n), BF16)],
        compiler_params=_cparams(("arbitrary", "arbitrary")),
        name="inproj",
    )(x, sh[0], sc[0], w, b)


CONV_CHUNK = 128
Z_PAD = 8
H_PAD = 16


def _conv_kernel(ua_ref, ub_ref, ws_ref, wd_ref, bd_ref, g_ref, b_ref, o_ref, z_scr, h_scr, *, seq):
    c = D_CONV
    z_scr[0:Z_PAD, :] = jnp.zeros((Z_PAD, c), F32)
    z_scr[Z_PAD + seq:2 * Z_PAD + seq, :] = jnp.zeros((Z_PAD, c), F32)
    h_scr[0, 0:H_PAD, :] = jnp.zeros((H_PAD, c), F32)
    h_scr[0, H_PAD + seq:2 * H_PAD + seq, :] = jnp.zeros((H_PAD, c), F32)
    z_scr[Z_PAD:Z_PAD + seq, :] = ua_ref[0, :, c:2 * c] * ua_ref[0, :, 2 * c:3 * c]
    h_scr[0, H_PAD:H_PAD + seq, :] = ub_ref[0, :, 0:c] * _sigmoid(ub_ref[0, :, c:2 * c])
    n_rows = seq + 2 * H_PAD
    h_all = h_scr[0]
    for r in range(1, SUBLANES):
        h_scr[r] = pltpu.roll(h_all, n_rows - r, axis=0)
    tc = min(CONV_CHUNK, seq)
    for ci in range(seq // tc):
        t0 = ci * tc
        acc = ws_ref[0:1, :] * z_scr[t0 + Z_PAD - 1:t0 + Z_PAD - 1 + tc, :]
        for j in range(1, SHORT_CONV_W):
            s0 = t0 + Z_PAD - 1 + j
            acc = acc + ws_ref[j:j + 1, :] * z_scr[s0:s0 + tc, :]
        ya = ua_ref[0, t0:t0 + tc, 0:c] * acc
        hb = bd_ref[...]
        for j in range(CONF_CONV_W):
            s0 = t0 + H_PAD - CONF_CONV_W // 2 + j
            a0 = s0 - s0 % SUBLANES
            hb = hb + wd_ref[j:j + 1, :] * h_scr[s0 % SUBLANES, a0:a0 + tc, :]
        hn = _ln(hb) * g_ref[...] + b_ref[...]
        yb = hn * _sigmoid(hn)
        o_ref[0, t0:t0 + tc, 0:c] = ya.astype(o_ref.dtype)
        o_ref[0, t0:t0 + tc, c:2 * c] = yb.astype(o_ref.dtype)


def _conv_mixers(ua, ub, w_short, w_dw, b_dw, g_ln, b_ln):
    nb, s, _ = ua.shape
    c = D_CONV
    full = lambda shape: pl.BlockSpec(shape, lambda bi: (0,) * len(shape))
    return pl.pallas_call(
        functools.partial(_conv_kernel, seq=s),
        out_shape=jax.ShapeDtypeStruct((nb, s, 2 * c), BF16),
        grid=(nb,),
        in_specs=[
            pl.BlockSpec((1, s, 3 * c), lambda bi: (bi, 0, 0)),
            pl.BlockSpec((1, s, 2 * c), lambda bi: (bi, 0, 0)),
            full((SHORT_CONV_W, c)), full((CONF_CONV_W, c)), full((1, c)), full((1, c)), full((1, c)),
        ],
        out_specs=pl.BlockSpec((1, s, 2 * c), lambda bi: (bi, 0, 0)),
        scratch_shapes=[pltpu.VMEM((s + 2 * Z_PAD, c), F32), pltpu.VMEM((SUBLANES, s + 2 * H_PAD, c), F32)],
        compiler_params=_cparams(("arbitrary",)),
        name="conv_mixers",
    )(ua, ub, w_short, w_dw, b_dw.reshape(1, c), g_ln.reshape(1, c), b_ln.reshape(1, c))


SUB_ROWS = 2
WIN_ROWS = SUB_ROWS + NA_WIN_ROWS_MAX - 1
BAND_KEYS = WIN_ROWS * GRID_W
SUB_Q = SUB_ROWS * GRID_W
N_DROW = 2 * NA_WIN_ROWS_MAX - 1
N_DCOL = 2 * NA_WIN_COLS - 1


def _bias_kernel(rpb_ref, nxt_ref, o_ref):
    n_cols = o_ref.shape[1]
    col = lax.broadcasted_iota(jnp.int32, (LANES, n_cols), 1)
    qc = col >> 7
    kc = col & (GRID_W - 1)
    d_col = jnp.clip(kc - qc + (NA_WIN_COLS - 1), 0, N_DCOL - 1)
    onehot = (lax.broadcasted_iota(jnp.int32, (LANES, n_cols), 0) == d_col).astype(F32)
    own = jnp.dot(rpb_ref[...], onehot, preferred_element_type=F32, precision=HI) * LOG2E
    nxt = jnp.dot(nxt_ref[...], onehot, preferred_element_type=F32, precision=HI) * LOG2E
    c0 = jnp.clip(qc - NA_WIN_COLS // 2, 0, GRID_W - NA_WIN_COLS)
    inside = (kc >= c0) & (kc < c0 + NA_WIN_COLS)
    second = (col & GRID_W) != 0
    o_ref[0:LANES, :] = jnp.where(inside, jnp.where(second, nxt, own), NEG_INF)
    o_ref[LANES:2 * LANES, :] = jnp.where(inside & ~second, own, NEG_INF)
    o_ref[2 * LANES:3 * LANES, :] = jnp.where(inside & second, own, NEG_INF)
    o_ref[3 * LANES:, :] = jnp.full((o_ref.shape[0] - 3 * LANES, n_cols), NEG_INF, F32)


N_SLABS = -(-WIN_ROWS // 2)
TILES_PER_KIND = LANES
MASKED_TILE = 3 * TILES_PER_KIND


def _na_plan(rows):
    wr = min(NA_WIN_ROWS_MAX, rows)
    assert wr == NA_WIN_ROWS_MAX and rows % SUB_ROWS == 0 and rows >= WIN_ROWS
    row_start = np.clip(np.arange(rows) - wr // 2, 0, rows - wr)
    w0s, tiles = [], []
    for r0 in range(0, rows, SUB_ROWS):
        w0 = int(np.clip(r0 - wr // 2, 0, rows - WIN_ROWS))
        for iq in range(SUB_ROWS):
            r = r0 + iq
            assert row_start[r] >= w0 and row_start[r] + wr <= w0 + WIN_ROWS
            ok = lambda w: w < WIN_ROWS and row_start[r] <= w0 + w < row_start[r] + wr
            d_row = lambda w: w0 + w - r + NA_WIN_ROWS_MAX - 1
            for j in range(N_SLABS):
                lo, hi = ok(2 * j), ok(2 * j + 1)
                if lo and hi:
                    tiles.append(d_row(2 * j))
                elif lo:
                    tiles.append(TILES_PER_KIND + d_row(2 * j))
                elif hi:
                    tiles.append(2 * TILES_PER_KIND + d_row(2 * j + 1))
                else:
                    tiles.append(-1)
        w0s.append(w0)
    return np.array(w0s, np.int32), np.array(tiles, np.int32)


def _na_bias(rpb):
    n_hd = N_NA_HEADS * N_DROW
    assert GRID_W == 64 and 2 * GRID_W == LANES and n_hd <= TILES_PER_KIND and N_DCOL <= LANES
    rpb = rpb.astype(F32)
    pad = lambda t: jnp.pad(t.reshape(n_hd, N_DCOL), ((0, LANES - n_hd), (0, LANES - N_DCOL)))
    nxt = jnp.concatenate([rpb[:, 1:], jnp.zeros_like(rpb[:, :1])], axis=1)
    n_tiles = 3 * TILES_PER_KIND + SUBLANES
    table = pl.pallas_call(
        _bias_kernel,
        out_shape=jax.ShapeDtypeStruct((n_tiles, GRID_W * LANES), F32),
        compiler_params=pltpu.CompilerParams(vmem_limit_bytes=VMEM_LIMIT),
        name="na_bias",
    )(pad(rpb), pad(nxt))
    return table.reshape(n_tiles, GRID_W, LANES)


def _lane_reduce(xs, combine, reduce, neutral):
    chunks = []
    for x in xs:
        rows, n = x.shape
        n_full = n // LANES
        chunks += [x[:, j * LANES:(j + 1) * LANES] for j in range(n_full)]
        if n % LANES:
            fill = jnp.full((rows, LANES - n % LANES), neutral, x.dtype)
            chunks.append(jnp.concatenate([x[:, n_full * LANES:], fill], axis=1))
    return reduce(functools.reduce(combine, chunks), axis=-1, keepdims=True)


def _attn_kernel(w0_ref, tile_ref, q_ref, k_ref, v_ref, kc_ref, vc_ref, *rest, banded, n_sub, sub_q):
    if banded:
        bias_ref, o_ref = rest
    else:
        (o_ref,) = rest
    lane = lax.broadcasted_iota(jnp.int32, (sub_q, LANES), 1)
    first = lane < NA_HEAD_DIM
    nt = (((1,), (1,)), ((), ()))
    stages = [(si, p) for si in range(n_sub) for p in range(N_PAIRS)]

    def window(si):
        blk = pl.program_id(1) * n_sub + si
        return blk, pl.multiple_of(w0_ref[blk] * GRID_W, GRID_W)

    def bias(blk, head):
        row_blocks = []
        for iq in range(SUB_ROWS):
            slabs = []
            for j in range(N_SLABS):
                t = tile_ref[(blk * SUB_ROWS + iq) * N_SLABS + j]
                tile = bias_ref[jnp.where(t < 0, MASKED_TILE, t + head * N_DROW)]
                width = min(LANES, BAND_KEYS - j * LANES)
                slabs.append(tile[:, :width])
            row_blocks.append(jnp.concatenate(slabs, axis=1))
        return jnp.concatenate(row_blocks, axis=0)

    def scores(si, p):
        cols = slice(p * LANES, (p + 1) * LANES)
        q_p = q_ref[0, si * sub_q:(si + 1) * sub_q, cols]
        zero = jnp.zeros_like(q_p)
        qq = jnp.concatenate([jnp.where(first, q_p, zero), jnp.where(first, zero, q_p)], axis=0)
        parts = [lax.dot_general(qq, kc_ref[0, :, cols], nt, preferred_element_type=F32)]
        if banded:
            blk, start = window(si)
            both = jnp.concatenate([bias(blk, 2 * p), bias(blk, 2 * p + 1)], axis=0)
            parts.append(lax.dot_general(qq, k_ref[0, pl.ds(start, BAND_KEYS), cols], nt,
                                         preferred_element_type=F32) + both)
        return parts

    def finish(si, p, parts):
        cols = slice(p * LANES, (p + 1) * LANES)
        m = _lane_reduce(parts, jnp.maximum, jnp.max, NEG_INF)
        es = [jnp.exp2(s - m) for s in parts]
        den = _lane_reduce(es, jnp.add, jnp.sum, 0.0)
        o = jnp.dot(es[0].astype(BF16), vc_ref[0, :, cols], preferred_element_type=F32)
        if banded:
            _, start = window(si)
            o = o + jnp.dot(es[1].astype(BF16), v_ref[0, pl.ds(start, BAND_KEYS), cols],
                            preferred_element_type=F32)
        o = o * (1.0 / den)
        out = jnp.where(first, o[:sub_q], o[sub_q:])
        o_ref[0, si * sub_q:(si + 1) * sub_q, cols] = out.astype(o_ref.dtype)

    nxt = scores(*stages[0])
    for i, (si, p) in enumerate(stages):
        cur = nxt
        if i + 1 < len(stages):
            nxt = scores(*stages[i + 1])
        finish(si, p, cur)


def _neighbourhood_attention(q, k, v, kc, vc, rpb):
    nb, s, dn = q.shape
    rows = s // GRID_W
    nctx = kc.shape[1]
    w0s, tiles = _na_plan(rows)
    bias = _na_bias(rpb)
    n_sub = ATTN_ROWS // SUB_ROWS
    m_rows = ATTN_ROWS * GRID_W
    grid_spec = pltpu.PrefetchScalarGridSpec(
        num_scalar_prefetch=2,
        grid=(nb, rows // ATTN_ROWS),
        in_specs=[
            pl.BlockSpec((1, m_rows, dn), lambda bi, i, w0, pat: (bi, i, 0)),
            pl.BlockSpec((1, s, dn), lambda bi, i, w0, pat: (bi, 0, 0)),
            pl.BlockSpec((1, s, dn), lambda bi, i, w0, pat: (bi, 0, 0)),
            pl.BlockSpec((1, nctx, dn), lambda bi, i, w0, pat: (bi, 0, 0)),
            pl.BlockSpec((1, nctx, dn), lambda bi, i, w0, pat: (bi, 0, 0)),
            pl.BlockSpec(bias.shape, lambda bi, i, w0, pat: (0, 0, 0), pipeline_mode=pl.Buffered(1)),
        ],
        out_specs=pl.BlockSpec((1, m_rows, dn), lambda bi, i, w0, pat: (bi, i, 0)),
    )
    return pl.pallas_call(
        functools.partial(_attn_kernel, banded=True, n_sub=n_sub, sub_q=SUB_Q),
        out_shape=jax.ShapeDtypeStruct((nb, s, dn), BF16),
        grid_spec=grid_spec,
        compiler_params=_cparams(("arbitrary", "arbitrary")),
        name="neighbourhood_attention",
    )(jnp.asarray(w0s), jnp.asarray(tiles), q, k, v, kc, vc, bias)


def _context_attention(q, kc, vc):
    nb, s, dn = q.shape
    spec = pl.BlockSpec((1, s, dn), lambda bi, i, w0, pat: (bi, 0, 0))
    grid_spec = pltpu.PrefetchScalarGridSpec(
        num_scalar_prefetch=2, grid=(nb, 1), in_specs=[spec] * 5, out_specs=spec)
    dummy = jnp.zeros((1,), jnp.int32)
    return pl.pallas_call(
        functools.partial(_attn_kernel, banded=False, n_sub=1, sub_q=s),
        out_shape=jax.ShapeDtypeStruct((nb, s, dn), BF16),
        grid_spec=grid_spec,
        compiler_params=_cparams(("arbitrary", "arbitrary")),
        name="context_attention",
    )(dummy, dummy, q, kc, vc, kc, vc)


def _store_rowmajor(ref, val, base=0):
    n, width = val.shape
    chunks = width // LANES
    for c in range(chunks):
        ref[pl.ds(base + c, n, stride=chunks), :] = val[:, c * LANES:(c + 1) * LANES]


def _load_rowmajor(ref, base, n, chunks):
    return jnp.concatenate([ref[pl.ds(base + c, n, stride=chunks), :] for c in range(chunks)], axis=1)


def _outproj_kernel(yab_ref, yc_ref, x_ref, w_ref, bo_ref, gt_ref, g_ref, b_ref, sh_ref, sc_ref,
                    wrh_ref, wrl_ref, xmid_ref, hm_ref, lg_ref, w_bf):
    @pl.when(_first_step())
    def _():
        w_bf[...] = w_ref[0].astype(BF16)

    half = yab_ref.shape[2]
    tm = x_ref.shape[1]
    chunks = x_ref.shape[2] // LANES
    n_part = max(1, tm // OUT_ROW_GROUP)
    rows_per = tm // n_part
    groups = [slice(part * rows_per, (part + 1) * rows_per) for part in range(n_part)]
    ys = [jnp.dot(yab_ref[0, rows, :], w_bf[0:half, :], preferred_element_type=F32)
          + jnp.dot(yc_ref[0, rows, :], w_bf[half:, :], preferred_element_type=F32) + bo_ref[...]
          for rows in groups]
    for part, (rows, y) in enumerate(zip(groups, ys)):
        xm = _ln(DEEPNORM_ALPHA * x_ref[0, rows, :] + (1.0 + gt_ref[0]) * y) * g_ref[...] + b_ref[...]
        xmid_ref[0, rows, :] = xm
        hm = _ln(xm) * (1.0 + sc_ref[0]) + sh_ref[0]
        _store_rowmajor(hm_ref, hm, base=part * rows_per * chunks)
        hm_hi = hm.astype(BF16)
        hm_lo = (hm - hm_hi.astype(F32)).astype(BF16)
        lg_ref[0, rows, :] = (jnp.dot(hm_hi, wrh_ref[...], preferred_element_type=F32)
                              + jnp.dot(hm_lo, wrh_ref[...], preferred_element_type=F32)
                              + jnp.dot(hm_hi, wrl_ref[...], preferred_element_type=F32))


def _outproj(yab, yc, x, w, layer, bo, gt, g, b, sh, sc, wr_hi, wr_lo, tm):
    nb, s, d = x.shape
    half = yab.shape[2]
    vec = pl.BlockSpec((1, d), lambda bi, i: (0, 0))
    tok = lambda width: pl.BlockSpec((1, tm, width), lambda bi, i: (bi, i, 0))
    n_i = s // tm
    return pl.pallas_call(
        _outproj_kernel,
        out_shape=[jax.ShapeDtypeStruct((nb, s, d), F32),
                   jax.ShapeDtypeStruct((nb * s * (d // LANES), LANES), F32),
                   jax.ShapeDtypeStruct((nb, s, LANES), F32)],
        grid=(nb, n_i),
        in_specs=[tok(half), tok(half), tok(d),
                  pl.BlockSpec((1, d, d), lambda bi, i: (layer, 0, 0), pipeline_mode=pl.Buffered(1)),
                  vec, _mod_spec(gt, d), vec, vec,
                  _mod_spec(sh, d), _mod_spec(sc, d), pl.BlockSpec((d, LANES), lambda bi, i: (0, 0)),
                  pl.BlockSpec((d, LANES), lambda bi, i: (0, 0))],
        out_specs=[tok(d), pl.BlockSpec((tm * (d // LANES), LANES), lambda bi, i: (bi * n_i + i, 0)),
                   tok(LANES)],
        scratch_shapes=[pltpu.VMEM((d, d), BF16)],
        compiler_params=_cparams(("arbitrary", "arbitrary")),
        name="outproj_postnorm",
    )(yab, yc, x, w, bo, gt[0], g, b, sh[0], sc[0], wr_hi, wr_lo)


CUM_CHUNK = 256
F32_EXP_BIAS = 127
F32_MANT_BITS = 23


def _prefix_count(mask_f32, tri):
    rows, n = mask_f32.shape
    tc = min(CUM_CHUNK, n)
    base = jnp.zeros((rows, 1), F32)
    parts = []
    for ci in range(n // tc):
        blk = mask_f32[:, ci * tc:(ci + 1) * tc]
        parts.append(jnp.dot(blk.astype(BF16), tri[:tc, :tc], preferred_element_type=F32) + base)
        base = base + jnp.sum(blk, axis=-1, keepdims=True)
    return jnp.concatenate(parts, axis=-1)


def _pow2(k):
    return pltpu.bitcast((k + F32_EXP_BIAS) << F32_MANT_BITS, F32)


def _route_kernel(lg_ref, slot_c_ref, gate_t_ref, *, cap, slot_stride):
    nb = lg_ref.shape[0]
    assert nb * N_EXPERTS == LANES
    rows = []
    for b in range(nb):
        lg = lg_ref[b]
        lane = lax.broadcasted_iota(jnp.int32, lg.shape, 1)
        lgm = jnp.where(lane < N_EXPERTS, lg, NEG_INF)
        ex = jnp.exp(lgm - jnp.max(lgm, axis=-1, keepdims=True))
        aff = ex / jnp.sum(ex, axis=-1, keepdims=True)
        rows.append(aff.T[0:N_EXPERTS, :])
    a = jnp.concatenate(rows, axis=0)
    capf = float(cap)

    def enough(t):
        return jnp.sum((a >= t).astype(F32), axis=-1, keepdims=True) >= capf

    def exp_step(_, carry):
        lo, hi = carry
        mid = lo + ((hi - lo + 1) >> 1)
        ok = enough(_pow2(mid))
        return jnp.where(ok, mid, lo), jnp.where(ok, hi, mid - 1)

    k_lo = jnp.full((LANES, 1), -F32_EXP_BIAS, jnp.int32)
    k_hi = jnp.zeros((LANES, 1), jnp.int32)
    k_lo, _ = lax.fori_loop(0, 7, exp_step, (k_lo, k_hi))
    base = _pow2(k_lo)

    def mant_step(_, carry):
        t, step = carry
        step = step * 0.5
        cand = t + step
        return jnp.where(enough(cand), cand, t), step

    thr, _ = lax.fori_loop(0, F32_MANT_BITS, mant_step, (base, base))

    r_i = lax.broadcasted_iota(jnp.int32, (CUM_CHUNK, CUM_CHUNK), 0)
    c_i = lax.broadcasted_iota(jnp.int32, (CUM_CHUNK, CUM_CHUNK), 1)
    tri = (r_i < c_i).astype(BF16)
    gt = (a > thr).astype(F32)
    eq = (a == thr).astype(F32)
    need = capf - jnp.sum(gt, axis=-1, keepdims=True)
    sel = gt + eq * (_prefix_count(eq, tri) < need).astype(F32)
    pos = _prefix_count(sel, tri)
    sample = lax.broadcasted_iota(jnp.int32, (LANES, 1), 0) >> (N_EXPERTS.bit_length() - 1)
    slot = jnp.where(sel > 0.0, pos + (sample * slot_stride).astype(F32), -1.0)
    for b in range(nb):
        lo = b * N_EXPERTS
        gate_t_ref[b] = a[lo:lo + N_EXPERTS, :]
        rolled = slot if b == 0 else jnp.concatenate([slot[lo:, :], slot[:lo, :]], axis=0)
        slot_c_ref[b] = rolled.T


def _route(logits, cap, slot_stride):
    nb, s, _ = logits.shape
    whole = lambda shape: pl.BlockSpec(shape, lambda i: (0,) * len(shape))
    return pl.pallas_call(
        functools.partial(_route_kernel, cap=cap, slot_stride=slot_stride),
        out_shape=[jax.ShapeDtypeStruct((nb, s, LANES), F32), jax.ShapeDtypeStruct((nb, N_EXPERTS, s), F32)],
        grid=(1,),
        in_specs=[whole((nb, s, LANES))],
        out_specs=[whole((nb, s, LANES)), whole((nb, N_EXPERTS, s))],
        compiler_params=_cparams(("arbitrary",)),
        name="route",
    )(logits)


TOK_SPLIT = 64


def _slot_list_kernel(slot_ref, gate_ref, idx_ref, g_ref, *, n_slots, tok_stride, merge):
    s = slot_ref.shape[1]
    b = pl.program_id(0)
    assert n_slots <= 256
    slot_id = lax.broadcasted_iota(jnp.int32, (s, n_slots), 1).astype(F32).astype(BF16)
    one, zero = jnp.ones((s, n_slots), BF16), jnp.zeros((s, n_slots), BF16)
    tok = lax.broadcasted_iota(jnp.int32, (1, s), 1) + b * tok_stride
    tok_hi = (tok >> (TOK_SPLIT.bit_length() - 1)).astype(F32)
    tok_lo = (tok & (TOK_SPLIT - 1)).astype(F32)
    zeros = jnp.zeros((SUBLANES - 5, s), F32)
    idx_rows, g_rows = [], []
    for e in range(N_EXPERTS):
        taken = jnp.broadcast_to(slot_ref[0, :, e:e + 1].astype(BF16), (s, n_slots))
        hit = jnp.where(taken == slot_id, one, zero)
        g0 = gate_ref[0, e:e + 1, :]
        g_hi = g0.astype(BF16).astype(F32)
        g_mid = (g0 - g_hi).astype(BF16).astype(F32)
        g_lo = g0 - g_hi - g_mid
        lhs = jnp.concatenate([tok_hi, tok_lo, g_hi, g_mid, g_lo, zeros], axis=0).astype(BF16)
        out = jnp.dot(lhs, hit, preferred_element_type=F32)
        idx_rows.append(out[0:1] * float(TOK_SPLIT) + out[1:2])
        g_rows.append(out[2:3] + out[3:4] + out[4:5])
    idx = jnp.concatenate(idx_rows, axis=0).astype(jnp.int32) * SUBLANES
    g = jnp.concatenate(g_rows + [jnp.zeros((LANES - N_EXPERTS, n_slots), F32)], axis=0)
    g_t = g.T
    g_cols = [jnp.broadcast_to(g_t[:, e:e + 1], (n_slots, LANES)) for e in range(N_EXPERTS)]
    if merge:
        @pl.when(b == 0)
        def _():
            idx_ref[0] = idx
            for e in range(N_EXPERTS):
                g_ref[e] = g_cols[e]

        @pl.when(b > 0)
        def _():
            idx_ref[0] = idx_ref[0] + idx
            for e in range(N_EXPERTS):
                g_ref[e] = g_ref[e] + g_cols[e]
    else:
        idx_ref[0] = idx
        for e in range(N_EXPERTS):
            g_ref[e] = g_cols[e]


def _slot_lists(slot_c, gate_t, n_slots, merge):
    nb, s, _ = slot_c.shape
    nbo = 1 if merge else nb
    idx_map = (lambda bi: (0, 0, 0)) if merge else (lambda bi: (bi, 0, 0))
    g_map = (lambda bi: (0, 0, 0)) if merge else (lambda bi: (0, bi, 0))
    return pl.pallas_call(
        functools.partial(_slot_list_kernel, n_slots=n_slots, tok_stride=s if merge else 0, merge=merge),
        out_shape=[jax.ShapeDtypeStruct((nbo, N_EXPERTS, n_slots), jnp.int32),
                   jax.ShapeDtypeStruct((N_EXPERTS, nbo * n_slots, LANES), F32)],
        grid=(nb,),
        in_specs=[pl.BlockSpec((1, s, LANES), lambda bi: (bi, 0, 0)),
                  pl.BlockSpec((1, N_EXPERTS, s), lambda bi: (bi, 0, 0))],
        out_specs=[pl.BlockSpec((1, N_EXPERTS, n_slots), idx_map),
                   pl.BlockSpec((N_EXPERTS, n_slots, LANES), g_map)],
        compiler_params=_cparams(("arbitrary",)),
        name="slot_lists",
    )(slot_c, gate_t)


ROW_UNROLL = 32


def _dispatch_kernel(idx_ref, hm_ref, x_ref, rows_scr, *, n_slots):
    b = pl.program_id(0)
    chunks = x_ref.shape[2] // LANES
    for e in range(N_EXPERTS):
        base = (b * N_EXPERTS + e) * n_slots

        def move(i, carry):
            s0 = i * ROW_UNROLL
            first = base + s0
            tiles = [hm_ref[pl.ds(pl.multiple_of(idx_ref[first + u], chunks), chunks), :]
                     for u in range(ROW_UNROLL)]
            dst = pl.multiple_of(s0 * chunks, ROW_UNROLL * chunks)
            rows_scr[pl.ds(dst, ROW_UNROLL * chunks), :] = jnp.concatenate(tiles, axis=0)
            return carry

        lax.fori_loop(0, n_slots // ROW_UNROLL, move, 0)
        x_ref[e] = _load_rowmajor(rows_scr, 0, n_slots, chunks).astype(x_ref.dtype)


def _dispatch(idx, hm_rm, nb, s, d, n_slots):
    chunks = d // LANES
    assert chunks == SUBLANES and n_slots % ROW_UNROLL == 0
    grid_spec = pltpu.PrefetchScalarGridSpec(
        num_scalar_prefetch=1,
        grid=(nb,),
        in_specs=[pl.BlockSpec((s * chunks, LANES), lambda bi, idx_r: (bi, 0))],
        out_specs=pl.BlockSpec((N_EXPERTS, n_slots, d), lambda bi, idx_r: (0, bi, 0)),
        scratch_shapes=[pltpu.VMEM((n_slots * chunks, LANES), F32)],
    )
    return pl.pallas_call(
        functools.partial(_dispatch_kernel, n_slots=n_slots),
        out_shape=jax.ShapeDtypeStruct((N_EXPERTS, nb * n_slots, d), BF16),
        grid_spec=grid_spec,
        compiler_params=_cparams(("arbitrary",)),
        name="dispatch",
    )(idx, hm_rm)


W_PARTS = 4

def _expert_kernel(*refs, n_main, has_extra):
    n_in = 4 if has_extra else 2
    n_out = 2 if has_extra else 1
    acts, w_parts = refs[:n_in], refs[n_in:n_in + 3 * W_PARTS]
    outs = refs[n_in + 3 * W_PARTS:n_in + 3 * W_PARTS + n_out]
    wg_bf, wu_bf, wd_bf = refs[n_in + 3 * W_PARTS + n_out:]
    if has_extra:
        x_ref, g_ref, xx_ref, gg_ref = acts
        y_ref, yy_ref = outs
    else:
        x_ref, g_ref = acts
        (y_ref,) = outs
    i = pl.program_id(1)

    @pl.when(i == 0)
    def _():
        for k, w_bf in enumerate((wg_bf, wu_bf, wd_bf)):
            for j in range(W_PARTS):
                part = w_parts[k * W_PARTS + j]
                rows = part.shape[2]
                w_bf[j * rows:(j + 1) * rows, :] = part[0, 0].astype(BF16)

    def ffn(x_in, gate_in, out_ref):
        rows_all = x_in.shape[1]
        chunks = x_in.shape[2] // LANES
        n_part = max(1, rows_all // ROW_GROUP)
        rows_per = rows_all // n_part
        groups = [slice(p * rows_per, (p + 1) * rows_per) for p in range(n_part)]
        ups = []
        for rows in groups:
            x_e = x_in[0, rows, :]
            ups.append((jnp.dot(x_e, wg_bf[...], preferred_element_type=F32),
                        jnp.dot(x_e, wu_bf[...], preferred_element_type=F32)))
        for p, (rows, (a, u)) in enumerate(zip(groups, ups)):
            h = (a * _sigmoid(a) * u).astype(BF16)
            y = jnp.dot(h, wd_bf[...], preferred_element_type=F32) * gate_in[0, rows, 0:1]
            _store_rowmajor(out_ref, y, base=p * rows_per * chunks)

    if has_extra:
        @pl.when(i < n_main)
        def _():
            ffn(x_ref, g_ref, y_ref.at[0])

        @pl.when(i == n_main)
        def _():
            ffn(xx_ref, gg_ref, yy_ref.at[0])
    else:
        ffn(x_ref, g_ref, y_ref.at[0])


def _experts(x_e, g_e, x_extra, g_extra, wg, wu, wd, layer, tm):
    _, rows, d = x_e.shape
    f = wg.shape[3]
    chunks = d // LANES
    n_main = rows // tm
    has_extra = x_extra is not None
    main_map = lambda e, i: (e, jnp.minimum(i, n_main - 1), 0)

    def w_specs(shape):
        assert shape[0] % W_PARTS == 0 and n_main >= W_PARTS
        blk = (1, 1, shape[0] // W_PARTS, shape[1])
        return [pl.BlockSpec(blk, functools.partial(
            lambda e, i, j: (layer, jnp.minimum(e + (i > j).astype(jnp.int32), N_EXPERTS - 1), j, 0), j=j))
            for j in range(W_PARTS)]

    in_specs = [pl.BlockSpec((1, tm, d), main_map), pl.BlockSpec((1, tm, LANES), main_map)]
    out_specs = [pl.BlockSpec((1, tm * chunks, LANES), main_map)]
    out_shape = [jax.ShapeDtypeStruct((N_EXPERTS, rows * chunks, LANES), F32)]
    args = [x_e, g_e]
    if has_extra:
        rows2 = x_extra.shape[1]
        in_specs += [pl.BlockSpec((1, rows2, d), lambda e, i: (e, 0, 0)),
                     pl.BlockSpec((1, rows2, LANES), lambda e, i: (e, 0, 0))]
        out_specs.append(pl.BlockSpec((1, rows2 * chunks, LANES), lambda e, i: (e, 0, 0)))
        out_shape.append(jax.ShapeDtypeStruct((N_EXPERTS, rows2 * chunks, LANES), F32))
        args += [x_extra, g_extra]
    outs = pl.pallas_call(
        functools.partial(_expert_kernel, n_main=n_main, has_extra=has_extra),
        out_shape=out_shape,
        grid=(N_EXPERTS, n_main + int(has_extra)),
        in_specs=in_specs + w_specs((d, f)) + w_specs((d, f)) + w_specs((f, d)),
        out_specs=out_specs,
        scratch_shapes=[pltpu.VMEM((d, f), BF16), pltpu.VMEM((d, f), BF16), pltpu.VMEM((f, d), BF16)],
        compiler_params=_cparams(("arbitrary", "arbitrary")),
        name="experts",
    )(*args, *([wg] * W_PARTS), *([wu] * W_PARTS), *([wd] * W_PARTS))
    return outs if has_extra else (outs[0], None)


COMBINE_EXPERTS = 4
ADD_UNROLL = 16
NORM_CHUNK = 256


def _combine_kernel(idx_ref, y_ref, x_ref, gt_ref, g_ref, b_ref, o_ref, acc, *, n_slots):
    b = pl.program_id(0)
    j = pl.program_id(1)
    s, d = x_ref.shape[1], x_ref.shape[2]
    chunks = d // LANES

    @pl.when(j == 0)
    def _():
        acc[...] = jnp.zeros(acc.shape, F32)

    for eg in range(COMBINE_EXPERTS):
        base = (b * N_EXPERTS + j * COMBINE_EXPERTS + eg) * n_slots

        def add(i, carry):
            s0 = i * ADD_UNROLL
            first = base + s0
            src = pl.multiple_of(s0 * chunks, ADD_UNROLL * chunks)
            y_rows = y_ref[eg, pl.ds(src, ADD_UNROLL * chunks), :]
            new = []
            for u in range(ADD_UNROLL):
                dst = pl.multiple_of(idx_ref[first + u], chunks)
                new.append((dst, acc[pl.ds(dst, chunks), :] + y_rows[u * chunks:(u + 1) * chunks, :]))
            for dst, val in new:
                acc[pl.ds(dst, chunks), :] = val
            return carry

        lax.fori_loop(0, n_slots // ADD_UNROLL, add, 0)

    @pl.when(j == pl.num_programs(1) - 1)
    def _():
        for ci in range(s // NORM_CHUNK):
            rows = slice(ci * NORM_CHUNK, (ci + 1) * NORM_CHUNK)
            ym = _load_rowmajor(acc, ci * NORM_CHUNK * chunks, NORM_CHUNK, chunks)
            z = DEEPNORM_ALPHA * x_ref[0, rows, :] + (1.0 + gt_ref[0]) * ym
            o_ref[0, rows, :] = _ln(z) * g_ref[...] + b_ref[...]


def _combine(idx, y_rm, x_mid, gt, g, b, n_slots):
    nb, s, d = x_mid.shape
    chunks = d // LANES
    assert chunks == SUBLANES and N_EXPERTS % COMBINE_EXPERTS == 0 and s % NORM_CHUNK == 0
    vec = pl.BlockSpec((1, d), lambda bi, j, i_r: (0, 0))
    grid_spec = pltpu.PrefetchScalarGridSpec(
        num_scalar_prefetch=1,
        grid=(nb, N_EXPERTS // COMBINE_EXPERTS),
        in_specs=[
            pl.BlockSpec((COMBINE_EXPERTS, n_slots * chunks, LANES), lambda bi, j, i_r: (j, bi, 0)),
            pl.BlockSpec((1, s, d), lambda bi, j, i_r: (bi, 0, 0)),
            _mod_spec(gt, d), vec, vec,
        ],
        out_specs=pl.BlockSpec((1, s, d), lambda bi, j, i_r: (bi, 0, 0)),
        scratch_shapes=[pltpu.VMEM((s * chunks, LANES), F32)],
    )
    return pl.pallas_call(
        functools.partial(_combine_kernel, n_slots=n_slots),
        out_shape=jax.ShapeDtypeStruct((nb, s, d), F32),
        grid_spec=grid_spec,
        compiler_params=_cparams(("arbitrary", "arbitrary")),
        name="combine_postnorm",
    )(idx, y_rm, x_mid, gt[0], g, b)


def kernel(x, c, ctx, c_ctx, w_mod, b_mod, w_in, b_in, w_short, w_conf_dw, b_conf_dw, g_conf_ln, b_conf_ln,
           na_rpb, w_out, b_out, g_post1, b_post1, w_router, w_gate, w_up, w_down, g_post2, b_post2):
    bsz, seq, d = x.shape
    nctx = ctx.shape[1]
    cap = EC_CAPACITY_FACTOR * seq // N_EXPERTS
    cap_ctx = EC_CAPACITY_FACTOR * nctx // N_EXPERTS
    q_scale = NA_HEAD_DIM ** -0.5 * LOG2E

    cond = jnp.concatenate([c, c_ctx[None, :], jnp.zeros((MOD_ROWS - bsz - 1, d), F32)], axis=0)
    mods = _modulation(cond, w_mod, b_mod)

    lat_splits = ((3 * D_CONV, 1.0), (2 * D_CONF, 1.0), (D_NA, q_scale), (D_NA, 1.0), (D_NA, 1.0))
    lat_dtypes = (F32, F32, BF16, BF16, BF16)
    kv_splits = ((D_NA, 1.0), (D_NA, 1.0))
    b_in3 = b_in[:, None, :]

    xc = ctx
    for l in range(DEPTH):
        last = l == DEPTH - 1
        m_lat = [(mods, (l * N_MOD + k) * MOD_ROWS, True) for k in range(N_MOD)]
        m_ctx = [(mods, (l * N_MOD + k) * MOD_ROWS + bsz, False) for k in range(N_MOD)]
        b_out_l = b_out[l][None, :]
        wr = jnp.pad(w_router[l], ((0, 0), (0, LANES - N_EXPERTS)))
        wr_hi = wr.astype(BF16)
        wr_lo = (wr - wr_hi.astype(F32)).astype(BF16)
        g1, b1 = g_post1[l][None, :], b_post1[l][None, :]
        g2, b2 = g_post2[l][None, :], b_post2[l][None, :]
        conv_w = (w_short[l], w_conf_dw[l], b_conf_dw[l], g_conf_ln[l], b_conf_ln[l])

        flat = lambda t: t.reshape(1, bsz * nctx, t.shape[-1])
        unflat = lambda t: t.reshape(bsz, nctx, t.shape[-1])
        if last:
            k_c, v_c = map(unflat, _inproj(flat(xc), m_ctx[0], m_ctx[1], w_in, b_in3, l, OFF_K,
                                           kv_splits, (BF16, BF16), tm=512))
        else:
            uac, ubc, q_c, k_c, v_c = map(unflat, _inproj(flat(xc), m_ctx[0], m_ctx[1], w_in, b_in3, l, 0,
                                                          lat_splits, lat_dtypes, tm=512))

        ua, ub, q, k, v = _inproj(x, m_lat[0], m_lat[1], w_in, b_in3, l, 0, lat_splits, lat_dtypes, tm=512)
        yab = _conv_mixers(ua, ub, *conv_w)
        yc = _neighbourhood_attention(q, k, v, k_c, v_c, na_rpb[l])
        x_mid, hm, logits = _outproj(yab, yc, x, w_out, l, b_out_l, m_lat[2], g1, b1, m_lat[3], m_lat[4],
                                     wr_hi, wr_lo, tm=512)

        idx, gates = _slot_lists(*_route(logits, cap, 0), cap, merge=False)
        idx = idx.reshape(-1)
        x_e = _dispatch(idx, hm, bsz, seq, d, cap)

        x_ec = gates_c = None
        if not last:
            yabc = _conv_mixers(uac, ubc, *conv_w)
            ycc = _context_attention(q_c, k_c, v_c)
            xc_mid, hmc, logits_c = _outproj(flat(yabc), flat(ycc), flat(xc), w_out, l, b_out_l, m_ctx[2], g1, b1,
                                             m_ctx[3], m_ctx[4], wr_hi, wr_lo, tm=512)
            n_c = bsz * cap_ctx
            idx_c, gates_c = _slot_lists(*_route(unflat(logits_c), cap_ctx, cap_ctx), n_c, merge=True)
            idx_c = idx_c.reshape(-1)
            x_ec = _dispatch(idx_c, hmc, 1, bsz * nctx, d, n_c)

        y_e, y_ec = _experts(x_e, gates, x_ec, gates_c, w_gate, w_up, w_down, l, tm=512)
        x = _combine(idx, y_e, x_mid, m_lat[5], g2, b2, cap)
        if not last:
            xc = unflat(_combine(idx_c, y_ec, xc_mid, m_ctx[5], g2, b2, n_c))
    return x
```

```python
import functools
import math

import numpy as np
import jax
import jax.numpy as jnp
from jax import lax
from jax.experimental import pallas as pl
from jax.experimental.pallas import tpu as pltpu

F32 = jnp.float32
BF16 = jnp.bfloat16

D_MODEL = 1024
DEPTH = 2
GRID_W = 64
D_CONV = D_MODEL // 4
D_CONF = D_MODEL // 4
NA_HEAD_DIM = 64
D_NA = D_MODEL - D_CONV - D_CONF
N_NA_HEADS = D_NA // NA_HEAD_DIM
SHORT_CONV_W = 3
CONF_CONV_W = 31
NA_WIN_ROWS_MAX = 8
NA_WIN_COLS = 16
N_EXPERTS = 16
EC_CAPACITY_FACTOR = 2
D_EXPERT = 1024
LN_EPS = 1e-5
DEEPNORM_ALPHA = (2.0 * DEPTH) ** 0.25
NEG_INF = -1e30
LOG2E = math.log2(math.e)

OFF_A = 0
OFF_B = OFF_A + 3 * D_CONV
OFF_Q = OFF_B + 2 * D_CONF
OFF_K = OFF_Q + D_NA
OFF_V = OFF_K + D_NA
D_IN = OFF_V + D_NA

LANES = 128
SUBLANES = 8
MOD_ROWS = 16
VMEM_LIMIT = 56 * 1024 * 1024
ATTN_ROWS = 8
N_PAIRS = D_NA // LANES
ROW_GROUP = 256
OUT_ROW_GROUP = 128
HI = lax.Precision.HIGHEST


def _cparams(sem):
    return pltpu.CompilerParams(dimension_semantics=sem, vmem_limit_bytes=VMEM_LIMIT)


def _ln(x):
    mu = jnp.mean(x, axis=-1, keepdims=True)
    xc = x - mu
    var = jnp.mean(xc * xc, axis=-1, keepdims=True)
    return xc * lax.rsqrt(var + LN_EPS)


def _sigmoid(x):
    return 1.0 / (1.0 + jnp.exp(-x))


def _mod_kernel(cond_ref, w_ref, b_ref, o_ref):
    s = cond_ref[...]
    s = s * _sigmoid(s)
    w = w_ref[0]
    s_hi, w_hi = s.astype(BF16), w.astype(BF16)
    s_lo = (s - s_hi.astype(F32)).astype(BF16)
    w_lo = (w - w_hi.astype(F32)).astype(BF16)
    o_ref[0] = (jnp.dot(s_hi, w_hi, preferred_element_type=F32) + jnp.dot(s_lo, w_hi, preferred_element_type=F32)
                + jnp.dot(s_hi, w_lo, preferred_element_type=F32) + b_ref[0])


N_MOD = 6


def _modulation(cond, w_mod, b_mod):
    n_l, d, n = w_mod.shape
    out = pl.pallas_call(
        _mod_kernel,
        out_shape=jax.ShapeDtypeStruct((n_l * N_MOD, MOD_ROWS, d), F32),
        grid=(n_l, N_MOD),
        in_specs=[
            pl.BlockSpec((MOD_ROWS, d), lambda l, k: (0, 0)),
            pl.BlockSpec((1, d, d), lambda l, k: (l, 0, k)),
            pl.BlockSpec((1, 1, d), lambda l, k: (l, 0, k)),
        ],
        out_specs=pl.BlockSpec((1, MOD_ROWS, d), lambda l, k: (l * N_MOD + k, 0, 0)),
        compiler_params=_cparams(("arbitrary", "arbitrary")),
        name="modulation",
    )(cond, w_mod, b_mod.reshape(n_l, 1, n))
    return out.reshape(n_l * N_MOD * MOD_ROWS, 1, d)


def _mod_spec(mod, d):
    _, row0, per_sample = mod
    if per_sample:
        return pl.BlockSpec((1, 1, d), lambda bi, *_: (row0 + bi, 0, 0))
    return pl.BlockSpec((1, 1, d), lambda bi, *_: (row0, 0, 0))


def _first_step():
    return (pl.program_id(0) == 0) & (pl.program_id(1) == 0)


def _inproj_kernel(x_ref, sh_ref, sc_ref, w_ref, b_ref, *rest, splits, col0):
    *o_refs, w_bf = rest

    @pl.when(_first_step())
    def _():
        w_bf[...] = w_ref[0].astype(BF16)

    tm = x_ref.shape[1]
    n_part = max(1, tm // ROW_GROUP)
    rows_per = tm // n_part
    groups = [slice(part_i * rows_per, (part_i + 1) * rows_per) for part_i in range(n_part)]
    hs = [(_ln(x_ref[0, rows, :]) * (1.0 + sc_ref[0]) + sh_ref[0]).astype(BF16) for rows in groups]
    for rows, h in zip(groups, hs):
        u = jnp.dot(h, w_bf[:, col0:], preferred_element_type=F32) + b_ref[0, :, col0:]
        off = 0
        for o_ref, (width, scale) in zip(o_refs, splits):
            part = u[:, off:off + width]
            if scale != 1.0:
                part = part * scale
            o_ref[0, rows, :] = part.astype(o_ref.dtype)
            off += width


def _inproj(x, sh, sc, w, b, layer, col0, splits, dtypes, tm):
    nb, s, d = x.shape
    n = w.shape[2]
    return pl.pallas_call(
        functools.partial(_inproj_kernel, splits=splits, col0=col0),
        out_shape=[jax.ShapeDtypeStruct((nb, s, wd), dt) for (wd, _), dt in zip(splits, dtypes)],
        grid=(nb, s // tm),
        in_specs=[
            pl.BlockSpec((1, tm, d), lambda bi, i: (bi, i, 0)),
            _mod_spec(sh, d),
            _mod_spec(sc, d),
            pl.BlockSpec((1, d, n), lambda bi, i: (layer, 0, 0), pipeline_mode=pl.Buffered(1)),
            pl.BlockSpec((1, 1, n), lambda bi, i: (layer, 0, 0)),
        ],
        out_specs=[pl.BlockSpec((1, tm, wd), lambda bi, i: (bi, i, 0)) for wd, _ in splits],
        scratch_shapes=[pltpu.VMEM((d, n), BF16)],
        compiler_params=_cparams(("arbitrary", "arbitrary")),
        name="inproj",
    )(x, sh[0], sc[0], w, b)


CONV_CHUNK = 128
Z_PAD = 8
H_PAD = 16


def _conv_kernel(ua_ref, ub_ref, ws_ref, wd_ref, bd_ref, g_ref, b_ref, o_ref, z_scr, h_scr, *, seq):
    c = D_CONV
    z_scr[0:Z_PAD, :] = jnp.zeros((Z_PAD, c), F32)
    z_scr[Z_PAD + seq:2 * Z_PAD + seq, :] = jnp.zeros((Z_PAD, c), F32)
    h_scr[0, 0:H_PAD, :] = jnp.zeros((H_PAD, c), F32)
    h_scr[0, H_PAD + seq:2 * H_PAD + seq, :] = jnp.zeros((H_PAD, c), F32)
    z_scr[Z_PAD:Z_PAD + seq, :] = ua_ref[0, :, c:2 * c] * ua_ref[0, :, 2 * c:3 * c]
    h_scr[0, H_PAD:H_PAD + seq, :] = ub_ref[0, :, 0:c] * _sigmoid(ub_ref[0, :, c:2 * c])
    n_rows = seq + 2 * H_PAD
    h_all = h_scr[0]
    for r in range(1, SUBLANES):
        h_scr[r] = pltpu.roll(h_all, n_rows - r, axis=0)
    tc = min(CONV_CHUNK, seq)
    for ci in range(seq // tc):
        t0 = ci * tc
        acc = ws_ref[0:1, :] * z_scr[t0 + Z_PAD - 1:t0 + Z_PAD - 1 + tc, :]
        for j in range(1, SHORT_CONV_W):
            s0 = t0 + Z_PAD - 1 + j
            acc = acc + ws_ref[j:j + 1, :] * z_scr[s0:s0 + tc, :]
        ya = ua_ref[0, t0:t0 + tc, 0:c] * acc
        hb = bd_ref[...]
        for j in range(CONF_CONV_W):
            s0 = t0 + H_PAD - CONF_CONV_W // 2 + j
            a0 = s0 - s0 % SUBLANES
            hb = hb + wd_ref[j:j + 1, :] * h_scr[s0 % SUBLANES, a0:a0 + tc, :]
        hn = _ln(hb) * g_ref[...] + b_ref[...]
        yb = hn * _sigmoid(hn)
        o_ref[0, t0:t0 + tc, 0:c] = ya.astype(o_ref.dtype)
        o_ref[0, t0:t0 + tc, c:2 * c] = yb.astype(o_ref.dtype)


def _conv_mixers(ua, ub, w_short, w_dw, b_dw, g_ln, b_ln):
    nb, s, _ = ua.shape
    c = D_CONV
    full = lambda shape: pl.BlockSpec(shape, lambda bi: (0,) * len(shape))
    return pl.pallas_call(
        functools.partial(_conv_kernel, seq=s),
        out_shape=jax.ShapeDtypeStruct((nb, s, 2 * c), BF16),
        grid=(nb,),
        in_specs=[
            pl.BlockSpec((1, s, 3 * c), lambda bi: (bi, 0, 0)),
            pl.BlockSpec((1, s, 2 * c), lambda bi: (bi, 0, 0)),
            full((SHORT_CONV_W, c)), full((CONF_CONV_W, c)), full((1, c)), full((1, c)), full((1, c)),
        ],
        out_specs=pl.BlockSpec((1, s, 2 * c), lambda bi: (bi, 0, 0)),
        scratch_shapes=[pltpu.VMEM((s + 2 * Z_PAD, c), F32), pltpu.VMEM((SUBLANES, s + 2 * H_PAD, c), F32)],
        compiler_params=_cparams(("arbitrary",)),
        name="conv_mixers",
    )(ua, ub, w_short, w_dw, b_dw.reshape(1, c), g_ln.reshape(1, c), b_ln.reshape(1, c))


SUB_ROWS = 2
WIN_ROWS = SUB_ROWS + NA_WIN_ROWS_MAX - 1
BAND_KEYS = WIN_ROWS * GRID_W
SUB_Q = SUB_ROWS * GRID_W
N_DROW = 2 * NA_WIN_ROWS_MAX - 1
N_DCOL = 2 * NA_WIN_COLS - 1


def _bias_kernel(rpb_ref, nxt_ref, o_ref):
    n_cols = o_ref.shape[1]
    col = lax.broadcasted_iota(jnp.int32, (LANES, n_cols), 1)
    qc = col >> (LANES.bit_length() - 1)
    kc = col & (GRID_W - 1)
    d_col = jnp.clip(kc - qc + (NA_WIN_COLS - 1), 0, N_DCOL - 1)
    onehot = (lax.broadcasted_iota(jnp.int32, (LANES, n_cols), 0) == d_col).astype(F32)
    own = jnp.dot(rpb_ref[...], onehot, preferred_element_type=F32, precision=HI) * LOG2E
    nxt = jnp.dot(nxt_ref[...], onehot, preferred_element_type=F32, precision=HI) * LOG2E
    c0 = jnp.clip(qc - NA_WIN_COLS // 2, 0, GRID_W - NA_WIN_COLS)
    inside = (kc >= c0) & (kc < c0 + NA_WIN_COLS)
    second = (col & GRID_W) != 0
    o_ref[0:LANES, :] = jnp.where(inside, jnp.where(second, nxt, own), NEG_INF)
    o_ref[LANES:2 * LANES, :] = jnp.where(inside & ~second, own, NEG_INF)
    o_ref[2 * LANES:3 * LANES, :] = jnp.where(inside & second, own, NEG_INF)
    o_ref[3 * LANES:, :] = jnp.full((o_ref.shape[0] - 3 * LANES, n_cols), NEG_INF, F32)


N_SLABS = -(-WIN_ROWS // 2)
TILES_PER_KIND = LANES
MASKED_TILE = 3 * TILES_PER_KIND


def _na_plan(rows):
    wr = min(NA_WIN_ROWS_MAX, rows)
    assert wr == NA_WIN_ROWS_MAX and rows % SUB_ROWS == 0 and rows >= WIN_ROWS
    row_start = np.clip(np.arange(rows) - wr // 2, 0, rows - wr)
    w0s, tiles = [], []
    for r0 in range(0, rows, SUB_ROWS):
        w0 = int(np.clip(r0 - wr // 2, 0, rows - WIN_ROWS))
        for iq in range(SUB_ROWS):
            r = r0 + iq
            assert row_start[r] >= w0 and row_start[r] + wr <= w0 + WIN_ROWS
            ok = lambda w: w < WIN_ROWS and row_start[r] <= w0 + w < row_start[r] + wr
            d_row = lambda w: w0 + w - r + NA_WIN_ROWS_MAX - 1
            for j in range(N_SLABS):
                lo, hi = ok(2 * j), ok(2 * j + 1)
                if lo and hi:
                    tiles.append(d_row(2 * j))
                elif lo:
                    tiles.append(TILES_PER_KIND + d_row(2 * j))
                elif hi:
                    tiles.append(2 * TILES_PER_KIND + d_row(2 * j + 1))
                else:
                    tiles.append(-1)
        w0s.append(w0)
    return np.array(w0s, np.int32), np.array(tiles, np.int32)


def _na_bias(rpb):
    n_hd = N_NA_HEADS * N_DROW
    assert GRID_W == 64 and 2 * GRID_W == LANES and n_hd <= TILES_PER_KIND and N_DCOL <= LANES
    rpb = rpb.astype(F32)
    pad = lambda t: jnp.pad(t.reshape(n_hd, N_DCOL), ((0, LANES - n_hd), (0, LANES - N_DCOL)))
    nxt = jnp.concatenate([rpb[:, 1:], jnp.zeros_like(rpb[:, :1])], axis=1)
    n_tiles = 3 * TILES_PER_KIND + SUBLANES
    table = pl.pallas_call(
        _bias_kernel,
        out_shape=jax.ShapeDtypeStruct((n_tiles, GRID_W * LANES), F32),
        compiler_params=pltpu.CompilerParams(vmem_limit_bytes=VMEM_LIMIT),
        name="na_bias",
    )(pad(rpb), pad(nxt))
    return table.reshape(n_tiles, GRID_W, LANES)


def _lane_reduce(xs, combine, reduce, neutral):
    chunks = []
    for x in xs:
        rows, n = x.shape
        n_full = n // LANES
        chunks += [x[:, j * LANES:(j + 1) * LANES] for j in range(n_full)]
        if n % LANES:
            fill = jnp.full((rows, LANES - n % LANES), neutral, x.dtype)
            chunks.append(jnp.concatenate([x[:, n_full * LANES:], fill], axis=1))
    return reduce(functools.reduce(combine, chunks), axis=-1, keepdims=True)


def _attn_kernel(w0_ref, tile_ref, q_ref, k_ref, v_ref, kc_ref, vc_ref, *rest, banded, n_sub, sub_q):
    if banded:
        bias_ref, o_ref = rest
    else:
        (o_ref,) = rest
    lane = lax.broadcasted_iota(jnp.int32, (sub_q, LANES), 1)
    first = lane < NA_HEAD_DIM
    nt = (((1,), (1,)), ((), ()))
    stages = [(si, p) for si in range(n_sub) for p in range(N_PAIRS)]

    def window(si):
        blk = pl.program_id(1) * n_sub + si
        return blk, pl.multiple_of(w0_ref[blk] * GRID_W, GRID_W)

    def bias(blk, head):
        row_blocks = []
        for iq in range(SUB_ROWS):
            slabs = []
            for j in range(N_SLABS):
                t = tile_ref[(blk * SUB_ROWS + iq) * N_SLABS + j]
                tile = bias_ref[jnp.where(t < 0, MASKED_TILE, t + head * N_DROW)]
                width = min(LANES, BAND_KEYS - j * LANES)
                slabs.append(tile[:, :width])
            row_blocks.append(jnp.concatenate(slabs, axis=1))
        return jnp.concatenate(row_blocks, axis=0)

    def scores(si, p):
        cols = slice(p * LANES, (p + 1) * LANES)
        q_p = q_ref[0, si * sub_q:(si + 1) * sub_q, cols]
        zero = jnp.zeros_like(q_p)
        qq = jnp.concatenate([jnp.where(first, q_p, zero), jnp.where(first, zero, q_p)], axis=0)
        parts = [lax.dot_general(qq, kc_ref[0, :, cols], nt, preferred_element_type=F32)]
        if banded:
            blk, start = window(si)
            both = jnp.concatenate([bias(blk, 2 * p), bias(blk, 2 * p + 1)], axis=0)
            parts.append(lax.dot_general(qq, k_ref[0, pl.ds(start, BAND_KEYS), cols], nt,
                                         preferred_element_type=F32) + both)
        return parts

    def finish(si, p, parts):
        cols = slice(p * LANES, (p + 1) * LANES)
        m = _lane_reduce(parts, jnp.maximum, jnp.max, NEG_INF)
        es = [jnp.exp2(s - m) for s in parts]
        den = _lane_reduce(es, jnp.add, jnp.sum, 0.0)
        o = jnp.dot(es[0].astype(BF16), vc_ref[0, :, cols], preferred_element_type=F32)
        if banded:
            _, start = window(si)
            o = o + jnp.dot(es[1].astype(BF16), v_ref[0, pl.ds(start, BAND_KEYS), cols],
                            preferred_element_type=F32)
        o = o * (1.0 / den)
        out = jnp.where(first, o[:sub_q], o[sub_q:])
        o_ref[0, si * sub_q:(si + 1) * sub_q, cols] = out.astype(o_ref.dtype)

    nxt = scores(*stages[0])
    for i, (si, p) in enumerate(stages):
        cur = nxt
        if i + 1 < len(stages):
            nxt = scores(*stages[i + 1])
        finish(si, p, cur)


def _neighbourhood_attention(q, k, v, kc, vc, rpb):
    nb, s, dn = q.shape
    rows = s // GRID_W
    nctx = kc.shape[1]
    w0s, tiles = _na_plan(rows)
    bias = _na_bias(rpb)
    n_sub = ATTN_ROWS // SUB_ROWS
    m_rows = ATTN_ROWS * GRID_W
    grid_spec = pltpu.PrefetchScalarGridSpec(
        num_scalar_prefetch=2,
        grid=(nb, rows // ATTN_ROWS),
        in_specs=[
            pl.BlockSpec((1, m_rows, dn), lambda bi, i, w0, pat: (bi, i, 0)),
            pl.BlockSpec((1, s, dn), lambda bi, i, w0, pat: (bi, 0, 0)),
            pl.BlockSpec((1, s, dn), lambda bi, i, w0, pat: (bi, 0, 0)),
            pl.BlockSpec((1, nctx, dn), lambda bi, i, w0, pat: (bi, 0, 0)),
            pl.BlockSpec((1, nctx, dn), lambda bi, i, w0, pat: (bi, 0, 0)),
            pl.BlockSpec(bias.shape, lambda bi, i, w0, pat: (0, 0, 0), pipeline_mode=pl.Buffered(1)),
        ],
        out_specs=pl.BlockSpec((1, m_rows, dn), lambda bi, i, w0, pat: (bi, i, 0)),
    )
    return pl.pallas_call(
        functools.partial(_attn_kernel, banded=True, n_sub=n_sub, sub_q=SUB_Q),
        out_shape=jax.ShapeDtypeStruct((nb, s, dn), BF16),
        grid_spec=grid_spec,
        compiler_params=_cparams(("arbitrary", "arbitrary")),
        name="neighbourhood_attention",
    )(jnp.asarray(w0s), jnp.asarray(tiles), q, k, v, kc, vc, bias)


def _context_attention(q, kc, vc):
    nb, s, dn = q.shape
    spec = pl.BlockSpec((1, s, dn), lambda bi, i, w0, pat: (bi, 0, 0))
    grid_spec = pltpu.PrefetchScalarGridSpec(
        num_scalar_prefetch=2, grid=(nb, 1), in_specs=[spec] * 5, out_specs=spec)
    dummy = jnp.zeros((1,), jnp.int32)
    return pl.pallas_call(
        functools.partial(_attn_kernel, banded=False, n_sub=1, sub_q=s),
        out_shape=jax.ShapeDtypeStruct((nb, s, dn), BF16),
        grid_spec=grid_spec,
        compiler_params=_cparams(("arbitrary", "arbitrary")),
        name="context_attention",
    )(dummy, dummy, q, kc, vc, kc, vc)


def _store_rowmajor(ref, val, base=0):
    n, width = val.shape
    chunks = width // LANES
    for c in range(chunks):
        ref[pl.ds(base + c, n, stride=chunks), :] = val[:, c * LANES:(c + 1) * LANES]


def _load_rowmajor(ref, base, n, chunks):
    return jnp.concatenate([ref[pl.ds(base + c, n, stride=chunks), :] for c in range(chunks)], axis=1)


def _outproj_kernel(yab_ref, yc_ref, x_ref, w_ref, bo_ref, gt_ref, g_ref, b_ref, sh_ref, sc_ref,
                    wrh_ref, wrhl_ref, xmid_ref, hm_ref, lg_ref, w_bf):
    @pl.when(_first_step())
    def _():
        w_bf[...] = w_ref[0].astype(BF16)

    half = yab_ref.shape[2]
    tm = x_ref.shape[1]
    chunks = x_ref.shape[2] // LANES
    n_part = max(1, tm // OUT_ROW_GROUP)
    rows_per = tm // n_part
    groups = [slice(part * rows_per, (part + 1) * rows_per) for part in range(n_part)]
    ys = [jnp.dot(yab_ref[0, rows, :], w_bf[0:half, :], preferred_element_type=F32)
          + jnp.dot(yc_ref[0, rows, :], w_bf[half:, :], preferred_element_type=F32) + bo_ref[...]
          for rows in groups]
    for part, (rows, y) in enumerate(zip(groups, ys)):
        xm = _ln(DEEPNORM_ALPHA * x_ref[0, rows, :] + (1.0 + gt_ref[0]) * y) * g_ref[...] + b_ref[...]
        xmid_ref[0, rows, :] = xm
        hm = _ln(xm) * (1.0 + sc_ref[0]) + sh_ref[0]
        _store_rowmajor(hm_ref, hm, base=part * rows_per * chunks)
        hm_hi = hm.astype(BF16)
        hm_lo = (hm - hm_hi.astype(F32)).astype(BF16)
        both = jnp.dot(hm_hi, wrhl_ref[...], preferred_element_type=F32)
        lg_ref[0, rows, :] = (both[:, :LANES] + both[:, LANES:]
                              + jnp.dot(hm_lo, wrh_ref[...], preferred_element_type=F32))


def _outproj(yab, yc, x, w, layer, bo, gt, g, b, sh, sc, wr_hi, wr_hl, tm):
    nb, s, d = x.shape
    half = yab.shape[2]
    vec = pl.BlockSpec((1, d), lambda bi, i: (0, 0))
    tok = lambda width: pl.BlockSpec((1, tm, width), lambda bi, i: (bi, i, 0))
    n_i = s // tm
    return pl.pallas_call(
        _outproj_kernel,
        out_shape=[jax.ShapeDtypeStruct((nb, s, d), F32),
                   jax.ShapeDtypeStruct((nb * s * (d // LANES), LANES), F32),
                   jax.ShapeDtypeStruct((nb, s, LANES), F32)],
        grid=(nb, n_i),
        in_specs=[tok(half), tok(half), tok(d),
                  pl.BlockSpec((1, d, d), lambda bi, i: (layer, 0, 0), pipeline_mode=pl.Buffered(1)),
                  vec, _mod_spec(gt, d), vec, vec,
                  _mod_spec(sh, d), _mod_spec(sc, d), pl.BlockSpec((d, LANES), lambda bi, i: (0, 0)),
                  pl.BlockSpec((d, 2 * LANES), lambda bi, i: (0, 0))],
        out_specs=[tok(d), pl.BlockSpec((tm * (d // LANES), LANES), lambda bi, i: (bi * n_i + i, 0)),
                   tok(LANES)],
        scratch_shapes=[pltpu.VMEM((d, d), BF16)],
        compiler_params=_cparams(("arbitrary", "arbitrary")),
        name="outproj_postnorm",
    )(yab, yc, x, w, bo, gt[0], g, b, sh[0], sc[0], wr_hi, wr_hl)


CUM_CHUNK = 256
F32_EXP_BIAS = 127
F32_MANT_BITS = 23


def _prefix_count(mask_f32, tri):
    rows, n = mask_f32.shape
    tc = min(CUM_CHUNK, n)
    base = jnp.zeros((rows, 1), F32)
    parts = []
    for ci in range(n // tc):
        blk = mask_f32[:, ci * tc:(ci + 1) * tc]
        parts.append(jnp.dot(blk.astype(BF16), tri[:tc, :tc], preferred_element_type=F32) + base)
        base = base + jnp.sum(blk, axis=-1, keepdims=True)
    return jnp.concatenate(parts, axis=-1)


def _pow2(k):
    return pltpu.bitcast((k + F32_EXP_BIAS) << F32_MANT_BITS, F32)


def _route_kernel(lg_ref, slot_c_ref, gate_t_ref, *, cap, slot_stride):
    nb = lg_ref.shape[0]
    assert nb * N_EXPERTS == LANES
    rows = []
    for b in range(nb):
        lg = lg_ref[b]
        lane = lax.broadcasted_iota(jnp.int32, lg.shape, 1)
        lgm = jnp.where(lane < N_EXPERTS, lg, NEG_INF)
        ex = jnp.exp(lgm - jnp.max(lgm, axis=-1, keepdims=True))
        aff = ex / jnp.sum(ex, axis=-1, keepdims=True)
        rows.append(aff.T[0:N_EXPERTS, :])
    a = jnp.concatenate(rows, axis=0)
    capf = float(cap)

    def enough(t):
        return jnp.sum((a >= t).astype(F32), axis=-1, keepdims=True) >= capf

    def exp_step(_, carry):
        lo, hi = carry
        mid = lo + ((hi - lo + 1) >> 1)
        ok = enough(_pow2(mid))
        return jnp.where(ok, mid, lo), jnp.where(ok, hi, mid - 1)

    k_lo = jnp.full((LANES, 1), -F32_EXP_BIAS, jnp.int32)
    k_hi = jnp.zeros((LANES, 1), jnp.int32)
    k_lo, _ = lax.fori_loop(0, 7, exp_step, (k_lo, k_hi))
    base = _pow2(k_lo)

    def mant_step(_, carry):
        t, step = carry
        step = step * 0.5
        cand = t + step
        return jnp.where(enough(cand), cand, t), step

    thr, _ = lax.fori_loop(0, F32_MANT_BITS, mant_step, (base, base))

    r_i = lax.broadcasted_iota(jnp.int32, (CUM_CHUNK, CUM_CHUNK), 0)
    c_i = lax.broadcasted_iota(jnp.int32, (CUM_CHUNK, CUM_CHUNK), 1)
    tri = (r_i < c_i).astype(BF16)
    gt = (a > thr).astype(F32)
    eq = (a == thr).astype(F32)
    need = capf - jnp.sum(gt, axis=-1, keepdims=True)
    sel = gt + eq * (_prefix_count(eq, tri) < need).astype(F32)
    pos = _prefix_count(sel, tri)
    sample = lax.broadcasted_iota(jnp.int32, (LANES, 1), 0) >> (N_EXPERTS.bit_length() - 1)
    slot = jnp.where(sel > 0.0, pos + (sample * slot_stride).astype(F32), -1.0)
    for b in range(nb):
        lo = b * N_EXPERTS
        gate_t_ref[b] = a[lo:lo + N_EXPERTS, :]
        rolled = slot if b == 0 else jnp.concatenate([slot[lo:, :], slot[:lo, :]], axis=0)
        slot_c_ref[b] = rolled.T


def _route(logits, cap, slot_stride):
    nb, s, _ = logits.shape
    whole = lambda shape: pl.BlockSpec(shape, lambda i: (0,) * len(shape))
    return pl.pallas_call(
        functools.partial(_route_kernel, cap=cap, slot_stride=slot_stride),
        out_shape=[jax.ShapeDtypeStruct((nb, s, LANES), F32), jax.ShapeDtypeStruct((nb, N_EXPERTS, s), F32)],
        grid=(1,),
        in_specs=[whole((nb, s, LANES))],
        out_specs=[whole((nb, s, LANES)), whole((nb, N_EXPERTS, s))],
        compiler_params=_cparams(("arbitrary",)),
        name="route",
    )(logits)


TOK_SPLIT = 64


def _slot_list_kernel(slot_ref, gate_ref, idx_ref, g_ref, *, n_slots, tok_stride, merge):
    s = slot_ref.shape[1]
    b = pl.program_id(0)
    assert n_slots <= 256
    slot_id = lax.broadcasted_iota(jnp.int32, (s, n_slots), 1).astype(F32).astype(BF16)
    one, zero = jnp.ones((s, n_slots), BF16), jnp.zeros((s, n_slots), BF16)
    tok = lax.broadcasted_iota(jnp.int32, (1, s), 1) + b * tok_stride
    tok_hi = (tok >> (TOK_SPLIT.bit_length() - 1)).astype(F32)
    tok_lo = (tok & (TOK_SPLIT - 1)).astype(F32)
    zeros = jnp.zeros((SUBLANES - 5, s), F32)
    idx_rows, g_rows = [], []
    for e in range(N_EXPERTS):
        taken = jnp.broadcast_to(slot_ref[0, :, e:e + 1].astype(BF16), (s, n_slots))
        hit = jnp.where(taken == slot_id, one, zero)
        g0 = gate_ref[0, e:e + 1, :]
        g_hi = g0.astype(BF16).astype(F32)
        g_mid = (g0 - g_hi).astype(BF16).astype(F32)
        g_lo = g0 - g_hi - g_mid
        lhs = jnp.concatenate([tok_hi, tok_lo, g_hi, g_mid, g_lo, zeros], axis=0).astype(BF16)
        out = jnp.dot(lhs, hit, preferred_element_type=F32)
        idx_rows.append(out[0:1] * float(TOK_SPLIT) + out[1:2])
        g_rows.append(out[2:3] + out[3:4] + out[4:5])
    idx = jnp.concatenate(idx_rows, axis=0).astype(jnp.int32) * SUBLANES
    g = jnp.concatenate(g_rows + [jnp.zeros((LANES - N_EXPERTS, n_slots), F32)], axis=0)
    g_t = g.T
    g_cols = [jnp.broadcast_to(g_t[:, e:e + 1], (n_slots, LANES)) for e in range(N_EXPERTS)]
    if merge:
        @pl.when(b == 0)
        def _():
            idx_ref[0] = idx
            for e in range(N_EXPERTS):
                g_ref[e] = g_cols[e]

        @pl.when(b > 0)
        def _():
            idx_ref[0] = idx_ref[0] + idx
            for e in range(N_EXPERTS):
                g_ref[e] = g_ref[e] + g_cols[e]
    else:
        idx_ref[0] = idx
        for e in range(N_EXPERTS):
            g_ref[e] = g_cols[e]


def _slot_lists(slot_c, gate_t, n_slots, merge):
    nb, s, _ = slot_c.shape
    nbo = 1 if merge else nb
    idx_map = (lambda bi: (0, 0, 0)) if merge else (lambda bi: (bi, 0, 0))
    g_map = (lambda bi: (0, 0, 0)) if merge else (lambda bi: (0, bi, 0))
    return pl.pallas_call(
        functools.partial(_slot_list_kernel, n_slots=n_slots, tok_stride=s if merge else 0, merge=merge),
        out_shape=[jax.ShapeDtypeStruct((nbo, N_EXPERTS, n_slots), jnp.int32),
                   jax.ShapeDtypeStruct((N_EXPERTS, nbo * n_slots, LANES), F32)],
        grid=(nb,),
        in_specs=[pl.BlockSpec((1, s, LANES), lambda bi: (bi, 0, 0)),
                  pl.BlockSpec((1, N_EXPERTS, s), lambda bi: (bi, 0, 0))],
        out_specs=[pl.BlockSpec((1, N_EXPERTS, n_slots), idx_map),
                   pl.BlockSpec((N_EXPERTS, n_slots, LANES), g_map)],
        compiler_params=_cparams(("arbitrary",)),
        name="slot_lists",
    )(slot_c, gate_t)


ROW_UNROLL = 32


def _dispatch_kernel(idx_ref, hm_ref, x_ref, rows_scr, *, n_slots):
    b = pl.program_id(0)
    chunks = x_ref.shape[2] // LANES
    for e in range(N_EXPERTS):
        base = (b * N_EXPERTS + e) * n_slots

        def move(i, carry):
            s0 = i * ROW_UNROLL
            first = base + s0
            tiles = [hm_ref[pl.ds(pl.multiple_of(idx_ref[first + u], chunks), chunks), :]
                     for u in range(ROW_UNROLL)]
            dst = pl.multiple_of(s0 * chunks, ROW_UNROLL * chunks)
            rows_scr[pl.ds(dst, ROW_UNROLL * chunks), :] = jnp.concatenate(tiles, axis=0)
            return carry

        lax.fori_loop(0, n_slots // ROW_UNROLL, move, 0)
        x_ref[e] = _load_rowmajor(rows_scr, 0, n_slots, chunks).astype(x_ref.dtype)


def _dispatch(idx, hm_rm, nb, s, d, n_slots):
    chunks = d // LANES
    assert chunks == SUBLANES and n_slots % ROW_UNROLL == 0
    grid_spec = pltpu.PrefetchScalarGridSpec(
        num_scalar_prefetch=1,
        grid=(nb,),
        in_specs=[pl.BlockSpec((s * chunks, LANES), lambda bi, idx_r: (bi, 0))],
        out_specs=pl.BlockSpec((N_EXPERTS, n_slots, d), lambda bi, idx_r: (0, bi, 0)),
        scratch_shapes=[pltpu.VMEM((n_slots * chunks, LANES), F32)],
    )
    return pl.pallas_call(
        functools.partial(_dispatch_kernel, n_slots=n_slots),
        out_shape=jax.ShapeDtypeStruct((N_EXPERTS, nb * n_slots, d), BF16),
        grid_spec=grid_spec,
        compiler_params=_cparams(("arbitrary",)),
        name="dispatch",
    )(idx, hm_rm)


W_PARTS = 4

def _expert_kernel(*refs, n_main, has_extra):
    n_in = 4 if has_extra else 2
    n_out = 2 if has_extra else 1
    acts, w_parts = refs[:n_in], refs[n_in:n_in + 3 * W_PARTS]
    outs = refs[n_in + 3 * W_PARTS:n_in + 3 * W_PARTS + n_out]
    wg_bf, wu_bf, wd_bf = refs[n_in + 3 * W_PARTS + n_out:]
    if has_extra:
        x_ref, g_ref, xx_ref, gg_ref = acts
        y_ref, yy_ref = outs
    else:
        x_ref, g_ref = acts
        (y_ref,) = outs
    i = pl.program_id(1)

    @pl.when(i == 0)
    def _():
        for k, w_bf in enumerate((wg_bf, wu_bf, wd_bf)):
            for j in range(W_PARTS):
                part = w_parts[k * W_PARTS + j]
                rows = part.shape[2]
                w_bf[j * rows:(j + 1) * rows, :] = part[0, 0].astype(BF16)

    def ffn(x_in, gate_in, out_ref):
        rows_all = x_in.shape[1]
        chunks = x_in.shape[2] // LANES
        n_part = max(1, rows_all // ROW_GROUP)
        rows_per = rows_all // n_part
        groups = [slice(p * rows_per, (p + 1) * rows_per) for p in range(n_part)]
        ups = []
        for rows in groups:
            x_e = x_in[0, rows, :]
            ups.append((jnp.dot(x_e, wg_bf[...], preferred_element_type=F32),
                        jnp.dot(x_e, wu_bf[...], preferred_element_type=F32)))
        for p, (rows, (a, u)) in enumerate(zip(groups, ups)):
            h = (a * _sigmoid(a) * u).astype(BF16)
            y = jnp.dot(h, wd_bf[...], preferred_element_type=F32) * gate_in[0, rows, 0:1]
            _store_rowmajor(out_ref, y, base=p * rows_per * chunks)

    if has_extra:
        @pl.when(i < n_main)
        def _():
            ffn(x_ref, g_ref, y_ref.at[0])

        @pl.when(i == n_main)
        def _():
            ffn(xx_ref, gg_ref, yy_ref.at[0])
    else:
        ffn(x_ref, g_ref, y_ref.at[0])


def _experts(x_e, g_e, x_extra, g_extra, wg, wu, wd, layer, tm):
    _, rows, d = x_e.shape
    f = wg.shape[3]
    chunks = d // LANES
    n_main = rows // tm
    has_extra = x_extra is not None
    main_map = lambda e, i: (e, jnp.minimum(i, n_main - 1), 0)

    def w_specs(shape):
        assert shape[0] % W_PARTS == 0 and n_main >= W_PARTS
        blk = (1, 1, shape[0] // W_PARTS, shape[1])
        return [pl.BlockSpec(blk, functools.partial(
            lambda e, i, j: (layer, jnp.minimum(e + (i > j).astype(jnp.int32), N_EXPERTS - 1), j, 0), j=j))
            for j in range(W_PARTS)]

    in_specs = [pl.BlockSpec((1, tm, d), main_map), pl.BlockSpec((1, tm, LANES), main_map)]
    out_specs = [pl.BlockSpec((1, tm * chunks, LANES), main_map)]
    out_shape = [jax.ShapeDtypeStruct((N_EXPERTS, rows * chunks, LANES), F32)]
    args = [x_e, g_e]
    if has_extra:
        rows2 = x_extra.shape[1]
        in_specs += [pl.BlockSpec((1, rows2, d), lambda e, i: (e, 0, 0)),
                     pl.BlockSpec((1, rows2, LANES), lambda e, i: (e, 0, 0))]
        out_specs.append(pl.BlockSpec((1, rows2 * chunks, LANES), lambda e, i: (e, 0, 0)))
        out_shape.append(jax.ShapeDtypeStruct((N_EXPERTS, rows2 * chunks, LANES), F32))
        args += [x_extra, g_extra]
    outs = pl.pallas_call(
        functools.partial(_expert_kernel, n_main=n_main, has_extra=has_extra),
        out_shape=out_shape,
        grid=(N_EXPERTS, n_main + int(has_extra)),
        in_specs=in_specs + w_specs((d, f)) + w_specs((d, f)) + w_specs((f, d)),
        out_specs=out_specs,
        scratch_shapes=[pltpu.VMEM((d, f), BF16), pltpu.VMEM((d, f), BF16), pltpu.VMEM((f, d), BF16)],
        compiler_params=_cparams(("arbitrary", "arbitrary")),
        name="experts",
    )(*args, *([wg] * W_PARTS), *([wu] * W_PARTS), *([wd] * W_PARTS))
    return outs if has_extra else (outs[0], None)


COMBINE_EXPERTS = 4
ADD_UNROLL = 16
NORM_CHUNK = 256


def _combine_kernel(idx_ref, y_ref, x_ref, gt_ref, g_ref, b_ref, o_ref, acc, *, n_slots):
    b = pl.program_id(0)
    j = pl.program_id(1)
    s, d = x_ref.shape[1], x_ref.shape[2]
    chunks = d // LANES

    @pl.when(j == 0)
    def _():
        acc[...] = jnp.zeros(acc.shape, F32)

    for eg in range(COMBINE_EXPERTS):
        base = (b * N_EXPERTS + j * COMBINE_EXPERTS + eg) * n_slots

        def add(i, carry):
            s0 = i * ADD_UNROLL
            first = base + s0
            src = pl.multiple_of(s0 * chunks, ADD_UNROLL * chunks)
            y_rows = y_ref[eg, pl.ds(src, ADD_UNROLL * chunks), :]
            new = []
            for u in range(ADD_UNROLL):
                dst = pl.multiple_of(idx_ref[first + u], chunks)
                new.append((dst, acc[pl.ds(dst, chunks), :] + y_rows[u * chunks:(u + 1) * chunks, :]))
            for dst, val in new:
                acc[pl.ds(dst, chunks), :] = val
            return carry

        lax.fori_loop(0, n_slots // ADD_UNROLL, add, 0)

    @pl.when(j == pl.num_programs(1) - 1)
    def _():
        for ci in range(s // NORM_CHUNK):
            rows = slice(ci * NORM_CHUNK, (ci + 1) * NORM_CHUNK)
            ym = _load_rowmajor(acc, ci * NORM_CHUNK * chunks, NORM_CHUNK, chunks)
            z = DEEPNORM_ALPHA * x_ref[0, rows, :] + (1.0 + gt_ref[0]) * ym
            o_ref[0, rows, :] = _ln(z) * g_ref[...] + b_ref[...]


def _combine(idx, y_rm, x_mid, gt, g, b, n_slots):
    nb, s, d = x_mid.shape
    chunks = d // LANES
    assert chunks == SUBLANES and N_EXPERTS % COMBINE_EXPERTS == 0 and s % NORM_CHUNK == 0
    vec = pl.BlockSpec((1, d), lambda bi, j, i_r: (0, 0))
    grid_spec = pltpu.PrefetchScalarGridSpec(
        num_scalar_prefetch=1,
        grid=(nb, N_EXPERTS // COMBINE_EXPERTS),
        in_specs=[
            pl.BlockSpec((COMBINE_EXPERTS, n_slots * chunks, LANES), lambda bi, j, i_r: (j, bi, 0)),
            pl.BlockSpec((1, s, d), lambda bi, j, i_r: (bi, 0, 0)),
            _mod_spec(gt, d), vec, vec,
        ],
        out_specs=pl.BlockSpec((1, s, d), lambda bi, j, i_r: (bi, 0, 0)),
        scratch_shapes=[pltpu.VMEM((s * chunks, LANES), F32)],
    )
    return pl.pallas_call(
        functools.partial(_combine_kernel, n_slots=n_slots),
        out_shape=jax.ShapeDtypeStruct((nb, s, d), F32),
        grid_spec=grid_spec,
        compiler_params=_cparams(("arbitrary", "arbitrary")),
        name="combine_postnorm",
    )(idx, y_rm, x_mid, gt[0], g, b)


def kernel(x, c, ctx, c_ctx, w_mod, b_mod, w_in, b_in, w_short, w_conf_dw, b_conf_dw, g_conf_ln, b_conf_ln,
           na_rpb, w_out, b_out, g_post1, b_post1, w_router, w_gate, w_up, w_down, g_post2, b_post2):
    bsz, seq, d = x.shape
    nctx = ctx.shape[1]
    cap = EC_CAPACITY_FACTOR * seq // N_EXPERTS
    cap_ctx = EC_CAPACITY_FACTOR * nctx // N_EXPERTS
    q_scale = NA_HEAD_DIM ** -0.5 * LOG2E

    cond = jnp.concatenate([c, c_ctx[None, :], jnp.zeros((MOD_ROWS - bsz - 1, d), F32)], axis=0)
    mods = _modulation(cond, w_mod, b_mod)

    lat_splits = ((3 * D_CONV, 1.0), (2 * D_CONF, 1.0), (D_NA, q_scale), (D_NA, 1.0), (D_NA, 1.0))
    lat_dtypes = (F32, F32, BF16, BF16, BF16)
    kv_splits = ((D_NA, 1.0), (D_NA, 1.0))
    b_in3 = b_in[:, None, :]

    xc = ctx
    for l in range(DEPTH):
        last = l == DEPTH - 1
        m_lat = [(mods, (l * N_MOD + k) * MOD_ROWS, True) for k in range(N_MOD)]
        m_ctx = [(mods, (l * N_MOD + k) * MOD_ROWS + bsz, False) for k in range(N_MOD)]
        b_out_l = b_out[l][None, :]
        wr = jnp.pad(w_router[l], ((0, 0), (0, LANES - N_EXPERTS)))
        wr_hi = wr.astype(BF16)
        wr_hl = jnp.concatenate([wr_hi, (wr - wr_hi.astype(F32)).astype(BF16)], axis=1)
        g1, b1 = g_post1[l][None, :], b_post1[l][None, :]
        g2, b2 = g_post2[l][None, :], b_post2[l][None, :]
        conv_w = (w_short[l], w_conf_dw[l], b_conf_dw[l], g_conf_ln[l], b_conf_ln[l])

        flat = lambda t: t.reshape(1, bsz * nctx, t.shape[-1])
        unflat = lambda t: t.reshape(bsz, nctx, t.shape[-1])
        if last:
            k_c, v_c = map(unflat, _inproj(flat(xc), m_ctx[0], m_ctx[1], w_in, b_in3, l, OFF_K,
                                           kv_splits, (BF16, BF16), tm=512))
        else:
            uac, ubc, q_c, k_c, v_c = map(unflat, _inproj(flat(xc), m_ctx[0], m_ctx[1], w_in, b_in3, l, 0,
                                                          lat_splits, lat_dtypes, tm=512))

        ua, ub, q, k, v = _inproj(x, m_lat[0], m_lat[1], w_in, b_in3, l, 0, lat_splits, lat_dtypes, tm=512)
        yab = _conv_mixers(ua, ub, *conv_w)
        yc = _neighbourhood_attention(q, k, v, k_c, v_c, na_rpb[l])
        x_mid, hm, logits = _outproj(yab, yc, x, w_out, l, b_out_l, m_lat[2], g1, b1, m_lat[3], m_lat[4],
                                     wr_hi, wr_hl, tm=512)

        idx, gates = _slot_lists(*_route(logits, cap, 0), cap, merge=False)
        idx = idx.reshape(-1)
        x_e = _dispatch(idx, hm, bsz, seq, d, cap)

        x_ec = gates_c = None
        if not last:
            yabc = _conv_mixers(uac, ubc, *conv_w)
            ycc = _context_attention(q_c, k_c, v_c)
            xc_mid, hmc, logits_c = _outproj(flat(yabc), flat(ycc), flat(xc), w_out, l, b_out_l, m_ctx[2], g1, b1,
                                             m_ctx[3], m_ctx[4], wr_hi, wr_hl, tm=512)
            n_c = bsz * cap_ctx
            idx_c, gates_c = _slot_lists(*_route(unflat(logits_c), cap_ctx, cap_ctx), n_c, merge=True)
            idx_c = idx_c.reshape(-1)
            x_ec = _dispatch(idx_c, hmc, 1, bsz * nctx, d, n_c)

        y_e, y_ec = _experts(x_e, gates, x_ec, gates_c, w_gate, w_up, w_down, l, tm=512)
        x = _combine(idx, y_e, x_mid, m_lat[5], g2, b2, cap)
        if not last:
            xc = unflat(_combine(idx_c, y_ec, xc_mid, m_ctx[5], g2, b2, n_c))
    return x
```

```python
import functools
import math

import numpy as np
import jax
import jax.numpy as jnp
from jax import lax
from jax.experimental import pallas as pl
from jax.experimental.pallas import tpu as pltpu

F32 = jnp.float32
BF16 = jnp.bfloat16

D_MODEL = 1024
DEPTH = 2
GRID_W = 64
D_CONV = D_MODEL // 4
D_CONF = D_MODEL // 4
NA_HEAD_DIM = 64
D_NA = D_MODEL - D_CONV - D_CONF
N_NA_HEADS = D_NA // NA_HEAD_DIM
SHORT_CONV_W = 3
CONF_CONV_W = 31
NA_WIN_ROWS_MAX = 8
NA_WIN_COLS = 16
N_EXPERTS = 16
EC_CAPACITY_FACTOR = 2
D_EXPERT = 1024
LN_EPS = 1e-5
DEEPNORM_ALPHA = (2.0 * DEPTH) ** 0.25
NEG_INF = -1e30
LOG2E = math.log2(math.e)

OFF_A = 0
OFF_B = OFF_A + 3 * D_CONV
OFF_Q = OFF_B + 2 * D_CONF
OFF_K = OFF_Q + D_NA
OFF_V = OFF_K + D_NA
D_IN = OFF_V + D_NA

LANES = 128
SUBLANES = 8
MOD_ROWS = 16
VMEM_LIMIT = 56 * 1024 * 1024
ATTN_ROWS = 8
N_PAIRS = D_NA // LANES
ROW_GROUP = 256
OUT_ROW_GROUP = 128
HI = lax.Precision.HIGHEST


def _cparams(sem):
    return pltpu.CompilerParams(dimension_semantics=sem, vmem_limit_bytes=VMEM_LIMIT)


def _ln(x):
    mu = jnp.mean(x, axis=-1, keepdims=True)
    xc = x - mu
    var = jnp.mean(xc * xc, axis=-1, keepdims=True)
    return xc * lax.rsqrt(var + LN_EPS)


def _sigmoid(x):
    return 1.0 / (1.0 + jnp.exp(-x))


def _mod_kernel(cond_ref, w_ref, b_ref, o_ref):
    s = cond_ref[...]
    s = s * _sigmoid(s)
    w = w_ref[0]
    s_hi, w_hi = s.astype(BF16), w.astype(BF16)
    s_lo = (s - s_hi.astype(F32)).astype(BF16)
    w_lo = (w - w_hi.astype(F32)).astype(BF16)
    o_ref[0] = (jnp.dot(s_hi, w_hi, preferred_element_type=F32) + jnp.dot(s_lo, w_hi, preferred_element_type=F32)
                + jnp.dot(s_hi, w_lo, preferred_element_type=F32) + b_ref[0])


N_MOD = 6


def _modulation(cond, w_mod, b_mod):
    n_l, d, n = w_mod.shape
    out = pl.pallas_call(
        _mod_kernel,
        out_shape=jax.ShapeDtypeStruct((n_l * N_MOD, MOD_ROWS, d), F32),
        grid=(n_l, N_MOD),
        in_specs=[
            pl.BlockSpec((MOD_ROWS, d), lambda l, k: (0, 0)),
            pl.BlockSpec((1, d, d), lambda l, k: (l, 0, k)),
            pl.BlockSpec((1, 1, d), lambda l, k: (l, 0, k)),
        ],
        out_specs=pl.BlockSpec((1, MOD_ROWS, d), lambda l, k: (l * N_MOD + k, 0, 0)),
        compiler_params=_cparams(("arbitrary", "arbitrary")),
        name="modulation",
    )(cond, w_mod, b_mod.reshape(n_l, 1, n))
    return out.reshape(n_l * N_MOD * MOD_ROWS, 1, d)


def _mod_spec(mod, d):
    _, row0, per_sample = mod
    if per_sample:
        return pl.BlockSpec((1, 1, d), lambda bi, *_: (row0 + bi, 0, 0))
    return pl.BlockSpec((1, 1, d), lambda bi, *_: (row0, 0, 0))


def _first_step():
    return (pl.program_id(0) == 0) & (pl.program_id(1) == 0)


def _inproj_kernel(x_ref, sh_ref, sc_ref, w_ref, b_ref, *rest, splits, col0):
    *o_refs, w_bf = rest

    @pl.when(_first_step())
    def _():
        w_bf[...] = w_ref[0].astype(BF16)

    tm = x_ref.shape[1]
    n_part = max(1, tm // ROW_GROUP)
    rows_per = tm // n_part
    groups = [slice(part_i * rows_per, (part_i + 1) * rows_per) for part_i in range(n_part)]
    hs = [(_ln(x_ref[0, rows, :]) * (1.0 + sc_ref[0]) + sh_ref[0]).astype(BF16) for rows in groups]
    for rows, h in zip(groups, hs):
        u = jnp.dot(h, w_bf[:, col0:], preferred_element_type=F32) + b_ref[0, :, col0:]
        off = 0
        for o_ref, (width, scale) in zip(o_refs, splits):
            part = u[:, off:off + width]
            if scale != 1.0:
                part = part * scale
            o_ref[0, rows, :] = part.astype(o_ref.dtype)
            off += width


def _inproj(x, sh, sc, w, b, layer, col0, splits, dtypes, tm):
    nb, s, d = x.shape
    n = w.shape[2]
    return pl.pallas_call(
        functools.partial(_inproj_kernel, splits=splits, col0=col0),
        out_shape=[jax.ShapeDtypeStruct((nb, s, wd), dt) for (wd, _), dt in zip(splits, dtypes)],
        grid=(nb, s // tm),
        in_specs=[
            pl.BlockSpec((1, tm, d), lambda bi, i: (bi, i, 0)),
            _mod_spec(sh, d),
            _mod_spec(sc, d),
            pl.BlockSpec((1, d, n), lambda bi, i: (layer, 0, 0), pipeline_mode=pl.Buffered(1)),
            pl.BlockSpec((1, 1, n), lambda bi, i: (layer, 0, 0)),
        ],
        out_specs=[pl.BlockSpec((1, tm, wd), lambda bi, i: (bi, i, 0)) for wd, _ in splits],
        scratch_shapes=[pltpu.VMEM((d, n), BF16)],
        compiler_params=_cparams(("arbitrary", "arbitrary")),
        name="inproj",
    )(x, sh[0], sc[0], w, b)


CONV_CHUNK = 128
Z_PAD = 8
H_PAD = 16


def _conv_kernel(ua_ref, ub_ref, ws_ref, wd_ref, bd_ref, g_ref, b_ref, o_ref, z_scr, h_scr, *, seq):
    c = D_CONV
    z_scr[0:Z_PAD, :] = jnp.zeros((Z_PAD, c), F32)
    z_scr[Z_PAD + seq:2 * Z_PAD + seq, :] = jnp.zeros((Z_PAD, c), F32)
    h_scr[0, 0:H_PAD, :] = jnp.zeros((H_PAD, c), F32)
    h_scr[0, H_PAD + seq:2 * H_PAD + seq, :] = jnp.zeros((H_PAD, c), F32)
    z_scr[Z_PAD:Z_PAD + seq, :] = ua_ref[0, :, c:2 * c] * ua_ref[0, :, 2 * c:3 * c]
    h_scr[0, H_PAD:H_PAD + seq, :] = ub_ref[0, :, 0:c] * _sigmoid(ub_ref[0, :, c:2 * c])
    n_rows = seq + 2 * H_PAD
    h_all = h_scr[0]
    for r in range(1, SUBLANES):
        h_scr[r] = pltpu.roll(h_all, n_rows - r, axis=0)
    tc = min(CONV_CHUNK, seq)
    for ci in range(seq // tc):
        t0 = ci * tc
        acc = ws_ref[0:1, :] * z_scr[t0 + Z_PAD - 1:t0 + Z_PAD - 1 + tc, :]
        for j in range(1, SHORT_CONV_W):
            s0 = t0 + Z_PAD - 1 + j
            acc = acc + ws_ref[j:j + 1, :] * z_scr[s0:s0 + tc, :]
        ya = ua_ref[0, t0:t0 + tc, 0:c] * acc
        hb = bd_ref[...]
        for j in range(CONF_CONV_W):
            s0 = t0 + H_PAD - CONF_CONV_W // 2 + j
            a0 = s0 - s0 % SUBLANES
            hb = hb + wd_ref[j:j + 1, :] * h_scr[s0 % SUBLANES, a0:a0 + tc, :]
        hn = _ln(hb) * g_ref[...] + b_ref[...]
        yb = hn * _sigmoid(hn)
        o_ref[0, t0:t0 + tc, 0:c] = ya.astype(o_ref.dtype)
        o_ref[0, t0:t0 + tc, c:2 * c] = yb.astype(o_ref.dtype)


def _conv_mixers(ua, ub, w_short, w_dw, b_dw, g_ln, b_ln):
    nb, s, _ = ua.shape
    c = D_CONV
    full = lambda shape: pl.BlockSpec(shape, lambda bi: (0,) * len(shape))
    return pl.pallas_call(
        functools.partial(_conv_kernel, seq=s),
        out_shape=jax.ShapeDtypeStruct((nb, s, 2 * c), BF16),
        grid=(nb,),
        in_specs=[
            pl.BlockSpec((1, s, 3 * c), lambda bi: (bi, 0, 0)),
            pl.BlockSpec((1, s, 2 * c), lambda bi: (bi, 0, 0)),
            full((SHORT_CONV_W, c)), full((CONF_CONV_W, c)), full((1, c)), full((1, c)), full((1, c)),
        ],
        out_specs=pl.BlockSpec((1, s, 2 * c), lambda bi: (bi, 0, 0)),
        scratch_shapes=[pltpu.VMEM((s + 2 * Z_PAD, c), F32), pltpu.VMEM((SUBLANES, s + 2 * H_PAD, c), F32)],
        compiler_params=_cparams(("arbitrary",)),
        name="conv_mixers",
    )(ua, ub, w_short, w_dw, b_dw.reshape(1, c), g_ln.reshape(1, c), b_ln.reshape(1, c))


SUB_ROWS = 2
WIN_ROWS = SUB_ROWS + NA_WIN_ROWS_MAX - 1
BAND_KEYS = WIN_ROWS * GRID_W
SUB_Q = SUB_ROWS * GRID_W
N_DROW = 2 * NA_WIN_ROWS_MAX - 1
N_DCOL = 2 * NA_WIN_COLS - 1


def _bias_kernel(rpb_ref, nxt_ref, o_ref):
    n_cols = o_ref.shape[1]
    col = lax.broadcasted_iota(jnp.int32, (LANES, n_cols), 1)
    qc = col >> (LANES.bit_length() - 1)
    kc = col & (GRID_W - 1)
    d_col = jnp.clip(kc - qc + (NA_WIN_COLS - 1), 0, N_DCOL - 1)
    onehot = (lax.broadcasted_iota(jnp.int32, (LANES, n_cols), 0) == d_col).astype(F32)
    own = jnp.dot(rpb_ref[...], onehot, preferred_element_type=F32, precision=HI) * LOG2E
    nxt = jnp.dot(nxt_ref[...], onehot, preferred_element_type=F32, precision=HI) * LOG2E
    c0 = jnp.clip(qc - NA_WIN_COLS // 2, 0, GRID_W - NA_WIN_COLS)
    inside = (kc >= c0) & (kc < c0 + NA_WIN_COLS)
    second = (col & GRID_W) != 0
    o_ref[0:LANES, :] = jnp.where(inside, jnp.where(second, nxt, own), NEG_INF)
    o_ref[LANES:2 * LANES, :] = jnp.where(inside & ~second, own, NEG_INF)
    o_ref[2 * LANES:3 * LANES, :] = jnp.where(inside & second, own, NEG_INF)
    o_ref[3 * LANES:, :] = jnp.full((o_ref.shape[0] - 3 * LANES, n_cols), NEG_INF, F32)


N_SLABS = -(-WIN_ROWS // 2)
TILES_PER_KIND = LANES
MASKED_TILE = 3 * TILES_PER_KIND


def _na_plan(rows):
    wr = min(NA_WIN_ROWS_MAX, rows)
    assert wr == NA_WIN_ROWS_MAX and rows % SUB_ROWS == 0 and rows >= WIN_ROWS
    row_start = np.clip(np.arange(rows) - wr // 2, 0, rows - wr)
    w0s, tiles = [], []
    for r0 in range(0, rows, SUB_ROWS):
        w0 = int(np.clip(r0 - wr // 2, 0, rows - WIN_ROWS))
        for iq in range(SUB_ROWS):
            r = r0 + iq
            assert row_start[r] >= w0 and row_start[r] + wr <= w0 + WIN_ROWS
            ok = lambda w: w < WIN_ROWS and row_start[r] <= w0 + w < row_start[r] + wr
            d_row = lambda w: w0 + w - r + NA_WIN_ROWS_MAX - 1
            for j in range(N_SLABS):
                lo, hi = ok(2 * j), ok(2 * j + 1)
                if lo and hi:
                    tiles.append(d_row(2 * j))
                elif lo:
                    tiles.append(TILES_PER_KIND + d_row(2 * j))
                elif hi:
                    tiles.append(2 * TILES_PER_KIND + d_row(2 * j + 1))
                else:
                    tiles.append(-1)
        w0s.append(w0)
    return np.array(w0s, np.int32), np.array(tiles, np.int32)


def _na_bias(rpb):
    n_hd = N_NA_HEADS * N_DROW
    assert GRID_W == 64 and 2 * GRID_W == LANES and n_hd <= TILES_PER_KIND and N_DCOL <= LANES
    rpb = rpb.astype(F32)
    pad = lambda t: jnp.pad(t.reshape(n_hd, N_DCOL), ((0, LANES - n_hd), (0, LANES - N_DCOL)))
    nxt = jnp.concatenate([rpb[:, 1:], jnp.zeros_like(rpb[:, :1])], axis=1)
    n_tiles = 3 * TILES_PER_KIND + SUBLANES
    table = pl.pallas_call(
        _bias_kernel,
        out_shape=jax.ShapeDtypeStruct((n_tiles, GRID_W * LANES), F32),
        compiler_params=pltpu.CompilerParams(vmem_limit_bytes=VMEM_LIMIT),
        name="na_bias",
    )(pad(rpb), pad(nxt))
    return table.reshape(n_tiles, GRID_W, LANES)


def _lane_reduce(xs, combine, reduce, neutral):
    chunks = []
    for x in xs:
        rows, n = x.shape
        n_full = n // LANES
        chunks += [x[:, j * LANES:(j + 1) * LANES] for j in range(n_full)]
        if n % LANES:
            fill = jnp.full((rows, LANES - n % LANES), neutral, x.dtype)
            chunks.append(jnp.concatenate([x[:, n_full * LANES:], fill], axis=1))
    return reduce(functools.reduce(combine, chunks), axis=-1, keepdims=True)


def _attn_kernel(w0_ref, tile_ref, q_ref, k_ref, v_ref, kc_ref, vc_ref, *rest, banded, n_sub, sub_q):
    if banded:
        bias_ref, o_ref = rest
    else:
        (o_ref,) = rest
    lane = lax.broadcasted_iota(jnp.int32, (sub_q, LANES), 1)
    first = lane < NA_HEAD_DIM
    nt = (((1,), (1,)), ((), ()))
    stages = [(si, p) for si in range(n_sub) for p in range(N_PAIRS)]

    def window(si):
        blk = pl.program_id(1) * n_sub + si
        return blk, pl.multiple_of(w0_ref[blk] * GRID_W, GRID_W)

    def bias(blk, head):
        row_blocks = []
        for iq in range(SUB_ROWS):
            slabs = []
            for j in range(N_SLABS):
                t = tile_ref[(blk * SUB_ROWS + iq) * N_SLABS + j]
                tile = bias_ref[jnp.where(t < 0, MASKED_TILE, t + head * N_DROW)]
                width = min(LANES, BAND_KEYS - j * LANES)
                slabs.append(tile[:, :width])
            row_blocks.append(jnp.concatenate(slabs, axis=1))
        return jnp.concatenate(row_blocks, axis=0)

    def scores(si, p):
        cols = slice(p * LANES, (p + 1) * LANES)
        q_p = q_ref[0, si * sub_q:(si + 1) * sub_q, cols]
        zero = jnp.zeros_like(q_p)
        qq = jnp.concatenate([jnp.where(first, q_p, zero), jnp.where(first, zero, q_p)], axis=0)
        parts = [lax.dot_general(qq, kc_ref[0, :, cols], nt, preferred_element_type=F32)]
        if banded:
            blk, start = window(si)
            both = jnp.concatenate([bias(blk, 2 * p), bias(blk, 2 * p + 1)], axis=0)
            parts.append(lax.dot_general(qq, k_ref[0, pl.ds(start, BAND_KEYS), cols], nt,
                                         preferred_element_type=F32) + both)
        return parts

    def weights(parts):
        m = _lane_reduce(parts, jnp.maximum, jnp.max, NEG_INF)
        es = [jnp.exp2(s - m) for s in parts]
        den = _lane_reduce(es, jnp.add, jnp.sum, 0.0)
        return [e.astype(BF16) for e in es], den

    def values(si, p, es, den):
        cols = slice(p * LANES, (p + 1) * LANES)
        o = jnp.dot(es[0], vc_ref[0, :, cols], preferred_element_type=F32)
        if banded:
            _, start = window(si)
            o = o + jnp.dot(es[1], v_ref[0, pl.ds(start, BAND_KEYS), cols], preferred_element_type=F32)
        o = o * (1.0 / den)
        out = jnp.where(first, o[:sub_q], o[sub_q:])
        o_ref[0, si * sub_q:(si + 1) * sub_q, cols] = out.astype(o_ref.dtype)

    nxt = scores(*stages[0])
    pending = None
    for i, (si, p) in enumerate(stages):
        cur = nxt
        if i + 1 < len(stages):
            nxt = scores(*stages[i + 1])
        if pending is not None:
            values(*pending)
        es, den = weights(cur)
        pending = (si, p, es, den)
    values(*pending)


def _neighbourhood_attention(q, k, v, kc, vc, rpb):
    nb, s, dn = q.shape
    rows = s // GRID_W
    nctx = kc.shape[1]
    w0s, tiles = _na_plan(rows)
    bias = _na_bias(rpb)
    n_sub = ATTN_ROWS // SUB_ROWS
    m_rows = ATTN_ROWS * GRID_W
    grid_spec = pltpu.PrefetchScalarGridSpec(
        num_scalar_prefetch=2,
        grid=(nb, rows // ATTN_ROWS),
        in_specs=[
            pl.BlockSpec((1, m_rows, dn), lambda bi, i, w0, pat: (bi, i, 0)),
            pl.BlockSpec((1, s, dn), lambda bi, i, w0, pat: (bi, 0, 0)),
            pl.BlockSpec((1, s, dn), lambda bi, i, w0, pat: (bi, 0, 0)),
            pl.BlockSpec((1, nctx, dn), lambda bi, i, w0, pat: (bi, 0, 0)),
            pl.BlockSpec((1, nctx, dn), lambda bi, i, w0, pat: (bi, 0, 0)),
            pl.BlockSpec(bias.shape, lambda bi, i, w0, pat: (0, 0, 0), pipeline_mode=pl.Buffered(1)),
        ],
        out_specs=pl.BlockSpec((1, m_rows, dn), lambda bi, i, w0, pat: (bi, i, 0)),
    )
    return pl.pallas_call(
        functools.partial(_attn_kernel, banded=True, n_sub=n_sub, sub_q=SUB_Q),
        out_shape=jax.ShapeDtypeStruct((nb, s, dn), BF16),
        grid_spec=grid_spec,
        compiler_params=_cparams(("arbitrary", "arbitrary")),
        name="neighbourhood_attention",
    )(jnp.asarray(w0s), jnp.asarray(tiles), q, k, v, kc, vc, bias)


def _context_attention(q, kc, vc):
    nb, s, dn = q.shape
    spec = pl.BlockSpec((1, s, dn), lambda bi, i, w0, pat: (bi, 0, 0))
    grid_spec = pltpu.PrefetchScalarGridSpec(
        num_scalar_prefetch=2, grid=(nb, 1), in_specs=[spec] * 5, out_specs=spec)
    dummy = jnp.zeros((1,), jnp.int32)
    return pl.pallas_call(
        functools.partial(_attn_kernel, banded=False, n_sub=1, sub_q=s),
        out_shape=jax.ShapeDtypeStruct((nb, s, dn), BF16),
        grid_spec=grid_spec,
        compiler_params=_cparams(("arbitrary", "arbitrary")),
        name="context_attention",
    )(dummy, dummy, q, kc, vc, kc, vc)


def _store_rowmajor(ref, val, base=0):
    n, width = val.shape
    chunks = width // LANES
    for c in range(chunks):
        ref[pl.ds(base + c, n, stride=chunks), :] = val[:, c * LANES:(c + 1) * LANES]


def _load_rowmajor(ref, base, n, chunks):
    return jnp.concatenate([ref[pl.ds(base + c, n, stride=chunks), :] for c in range(chunks)], axis=1)


def _outproj_kernel(yab_ref, yc_ref, x_ref, w_ref, bo_ref, gt_ref, g_ref, b_ref, sh_ref, sc_ref,
                    wrh_ref, wrhl_ref, xmid_ref, hm_ref, lg_ref, w_bf):
    @pl.when(_first_step())
    def _():
        w_bf[...] = w_ref[0].astype(BF16)

    half = yab_ref.shape[2]
    tm = x_ref.shape[1]
    chunks = x_ref.shape[2] // LANES
    n_part = max(1, tm // OUT_ROW_GROUP)
    rows_per = tm // n_part
    groups = [slice(part * rows_per, (part + 1) * rows_per) for part in range(n_part)]
    ys = [jnp.dot(yab_ref[0, rows, :], w_bf[0:half, :], preferred_element_type=F32)
          + jnp.dot(yc_ref[0, rows, :], w_bf[half:, :], preferred_element_type=F32) + bo_ref[...]
          for rows in groups]
    for part, (rows, y) in enumerate(zip(groups, ys)):
        xm = _ln(DEEPNORM_ALPHA * x_ref[0, rows, :] + (1.0 + gt_ref[0]) * y) * g_ref[...] + b_ref[...]
        xmid_ref[0, rows, :] = xm
        hm = _ln(xm) * (1.0 + sc_ref[0]) + sh_ref[0]
        _store_rowmajor(hm_ref, hm, base=part * rows_per * chunks)
        hm_hi = hm.astype(BF16)
        hm_lo = (hm - hm_hi.astype(F32)).astype(BF16)
        both = jnp.dot(hm_hi, wrhl_ref[...], preferred_element_type=F32)
        lg_ref[0, rows, :] = (both[:, :LANES] + both[:, LANES:]
                              + jnp.dot(hm_lo, wrh_ref[...], preferred_element_type=F32))


def _outproj(yab, yc, x, w, layer, bo, gt, g, b, sh, sc, wr_hi, wr_hl, tm):
    nb, s, d = x.shape
    half = yab.shape[2]
    vec = pl.BlockSpec((1, d), lambda bi, i: (0, 0))
    tok = lambda width: pl.BlockSpec((1, tm, width), lambda bi, i: (bi, i, 0))
    n_i = s // tm
    return pl.pallas_call(
        _outproj_kernel,
        out_shape=[jax.ShapeDtypeStruct((nb, s, d), F32),
                   jax.ShapeDtypeStruct((nb * s * (d // LANES), LANES), F32),
                   jax.ShapeDtypeStruct((nb, s, LANES), F32)],
        grid=(nb, n_i),
        in_specs=[tok(half), tok(half), tok(d),
                  pl.BlockSpec((1, d, d), lambda bi, i: (layer, 0, 0), pipeline_mode=pl.Buffered(1)),
                  vec, _mod_spec(gt, d), vec, vec,
                  _mod_spec(sh, d), _mod_spec(sc, d), pl.BlockSpec((d, LANES), lambda bi, i: (0, 0)),
                  pl.BlockSpec((d, 2 * LANES), lambda bi, i: (0, 0))],
        out_specs=[tok(d), pl.BlockSpec((tm * (d // LANES), LANES), lambda bi, i: (bi * n_i + i, 0)),
                   tok(LANES)],
        scratch_shapes=[pltpu.VMEM((d, d), BF16)],
        compiler_params=_cparams(("arbitrary", "arbitrary")),
        name="outproj_postnorm",
    )(yab, yc, x, w, bo, gt[0], g, b, sh[0], sc[0], wr_hi, wr_hl)


CUM_CHUNK = 256
F32_EXP_BIAS = 127
F32_MANT_BITS = 23


def _prefix_count(mask_f32, tri):
    rows, n = mask_f32.shape
    tc = min(CUM_CHUNK, n)
    base = jnp.zeros((rows, 1), F32)
    parts = []
    for ci in range(n // tc):
        blk = mask_f32[:, ci * tc:(ci + 1) * tc]
        parts.append(jnp.dot(blk.astype(BF16), tri[:tc, :tc], preferred_element_type=F32) + base)
        base = base + jnp.sum(blk, axis=-1, keepdims=True)
    return jnp.concatenate(parts, axis=-1)


def _pow2(k):
    return pltpu.bitcast((k + F32_EXP_BIAS) << F32_MANT_BITS, F32)


def _route_kernel(lg_ref, slot_c_ref, gate_t_ref, *, cap, slot_stride):
    nb = lg_ref.shape[0]
    assert nb * N_EXPERTS == LANES
    rows = []
    for b in range(nb):
        lg = lg_ref[b]
        lane = lax.broadcasted_iota(jnp.int32, lg.shape, 1)
        lgm = jnp.where(lane < N_EXPERTS, lg, NEG_INF)
        ex = jnp.exp(lgm - jnp.max(lgm, axis=-1, keepdims=True))
        aff = ex / jnp.sum(ex, axis=-1, keepdims=True)
        rows.append(aff.T[0:N_EXPERTS, :])
    a = jnp.concatenate(rows, axis=0)
    capf = float(cap)

    def enough(t):
        return jnp.sum((a >= t).astype(F32), axis=-1, keepdims=True) >= capf

    def exp_step(_, carry):
        lo, hi = carry
        mid = lo + ((hi - lo + 1) >> 1)
        ok = enough(_pow2(mid))
        return jnp.where(ok, mid, lo), jnp.where(ok, hi, mid - 1)

    k_lo = jnp.full((LANES, 1), -F32_EXP_BIAS, jnp.int32)
    k_hi = jnp.zeros((LANES, 1), jnp.int32)
    k_lo, _ = lax.fori_loop(0, 7, exp_step, (k_lo, k_hi))
    base = _pow2(k_lo)

    def mant_step(_, carry):
        t, step = carry
        step = step * 0.5
        cand = t + step
        return jnp.where(enough(cand), cand, t), step

    thr, _ = lax.fori_loop(0, F32_MANT_BITS, mant_step, (base, base))

    r_i = lax.broadcasted_iota(jnp.int32, (CUM_CHUNK, CUM_CHUNK), 0)
    c_i = lax.broadcasted_iota(jnp.int32, (CUM_CHUNK, CUM_CHUNK), 1)
    tri = (r_i < c_i).astype(BF16)
    gt = (a > thr).astype(F32)
    eq = (a == thr).astype(F32)
    need = capf - jnp.sum(gt, axis=-1, keepdims=True)
    sel = gt + eq * (_prefix_count(eq, tri) < need).astype(F32)
    pos = _prefix_count(sel, tri)
    sample = lax.broadcasted_iota(jnp.int32, (LANES, 1), 0) >> (N_EXPERTS.bit_length() - 1)
    slot = jnp.where(sel > 0.0, pos + (sample * slot_stride).astype(F32), -1.0)
    for b in range(nb):
        lo = b * N_EXPERTS
        gate_t_ref[b] = a[lo:lo + N_EXPERTS, :]
        rolled = slot if b == 0 else jnp.concatenate([slot[lo:, :], slot[:lo, :]], axis=0)
        slot_c_ref[b] = rolled.T


def _route(logits, cap, slot_stride):
    nb, s, _ = logits.shape
    whole = lambda shape: pl.BlockSpec(shape, lambda i: (0,) * len(shape))
    return pl.pallas_call(
        functools.partial(_route_kernel, cap=cap, slot_stride=slot_stride),
        out_shape=[jax.ShapeDtypeStruct((nb, s, LANES), F32), jax.ShapeDtypeStruct((nb, N_EXPERTS, s), F32)],
        grid=(1,),
        in_specs=[whole((nb, s, LANES))],
        out_specs=[whole((nb, s, LANES)), whole((nb, N_EXPERTS, s))],
        compiler_params=_cparams(("arbitrary",)),
        name="route",
    )(logits)


TOK_SPLIT = 64


def _slot_list_kernel(slot_ref, gate_ref, idx_ref, g_ref, *, n_slots, tok_stride, merge):
    s = slot_ref.shape[1]
    b = pl.program_id(0)
    assert n_slots <= 256
    slot_id = lax.broadcasted_iota(jnp.int32, (s, n_slots), 1).astype(F32).astype(BF16)
    one, zero = jnp.ones((s, n_slots), BF16), jnp.zeros((s, n_slots), BF16)
    tok = lax.broadcasted_iota(jnp.int32, (1, s), 1) + b * tok_stride
    tok_hi = (tok >> (TOK_SPLIT.bit_length() - 1)).astype(F32)
    tok_lo = (tok & (TOK_SPLIT - 1)).astype(F32)
    zeros = jnp.zeros((SUBLANES - 5, s), F32)
    idx_rows, g_rows = [], []
    for e in range(N_EXPERTS):
        taken = jnp.broadcast_to(slot_ref[0, :, e:e + 1].astype(BF16), (s, n_slots))
        hit = jnp.where(taken == slot_id, one, zero)
        g0 = gate_ref[0, e:e + 1, :]
        g_hi = g0.astype(BF16).astype(F32)
        g_mid = (g0 - g_hi).astype(BF16).astype(F32)
        g_lo = g0 - g_hi - g_mid
        lhs = jnp.concatenate([tok_hi, tok_lo, g_hi, g_mid, g_lo, zeros], axis=0).astype(BF16)
        out = jnp.dot(lhs, hit, preferred_element_type=F32)
        idx_rows.append(out[0:1] * float(TOK_SPLIT) + out[1:2])
        g_rows.append(out[2:3] + out[3:4] + out[4:5])
    idx = jnp.concatenate(idx_rows, axis=0).astype(jnp.int32) * SUBLANES
    g = jnp.concatenate(g_rows + [jnp.zeros((LANES - N_EXPERTS, n_slots), F32)], axis=0)
    g_t = g.T
    g_cols = [jnp.broadcast_to(g_t[:, e:e + 1], (n_slots, LANES)) for e in range(N_EXPERTS)]
    if merge:
        @pl.when(b == 0)
        def _():
            idx_ref[0] = idx
            for e in range(N_EXPERTS):
                g_ref[e] = g_cols[e]

        @pl.when(b > 0)
        def _():
            idx_ref[0] = idx_ref[0] + idx
            for e in range(N_EXPERTS):
                g_ref[e] = g_ref[e] + g_cols[e]
    else:
        idx_ref[0] = idx
        for e in range(N_EXPERTS):
            g_ref[e] = g_cols[e]


def _slot_lists(slot_c, gate_t, n_slots, merge):
    nb, s, _ = slot_c.shape
    nbo = 1 if merge else nb
    idx_map = (lambda bi: (0, 0, 0)) if merge else (lambda bi: (bi, 0, 0))
    g_map = (lambda bi: (0, 0, 0)) if merge else (lambda bi: (0, bi, 0))
    return pl.pallas_call(
        functools.partial(_slot_list_kernel, n_slots=n_slots, tok_stride=s if merge else 0, merge=merge),
        out_shape=[jax.ShapeDtypeStruct((nbo, N_EXPERTS, n_slots), jnp.int32),
                   jax.ShapeDtypeStruct((N_EXPERTS, nbo * n_slots, LANES), F32)],
        grid=(nb,),
        in_specs=[pl.BlockSpec((1, s, LANES), lambda bi: (bi, 0, 0)),
                  pl.BlockSpec((1, N_EXPERTS, s), lambda bi: (bi, 0, 0))],
        out_specs=[pl.BlockSpec((1, N_EXPERTS, n_slots), idx_map),
                   pl.BlockSpec((N_EXPERTS, n_slots, LANES), g_map)],
        compiler_params=_cparams(("arbitrary",)),
        name="slot_lists",
    )(slot_c, gate_t)


ROW_UNROLL = 32


def _dispatch_kernel(idx_ref, hm_ref, x_ref, rows_scr, *, n_slots):
    b = pl.program_id(0)
    chunks = x_ref.shape[2] // LANES
    for e in range(N_EXPERTS):
        base = (b * N_EXPERTS + e) * n_slots

        def move(i, carry):
            s0 = i * ROW_UNROLL
            first = base + s0
            tiles = [hm_ref[pl.ds(pl.multiple_of(idx_ref[first + u], chunks), chunks), :]
                     for u in range(ROW_UNROLL)]
            dst = pl.multiple_of(s0 * chunks, ROW_UNROLL * chunks)
            rows_scr[pl.ds(dst, ROW_UNROLL * chunks), :] = jnp.concatenate(tiles, axis=0)
            return carry

        lax.fori_loop(0, n_slots // ROW_UNROLL, move, 0)
        x_ref[e] = _load_rowmajor(rows_scr, 0, n_slots, chunks).astype(x_ref.dtype)


def _dispatch(idx, hm_rm, nb, s, d, n_slots):
    chunks = d // LANES
    assert chunks == SUBLANES and n_slots % ROW_UNROLL == 0
    grid_spec = pltpu.PrefetchScalarGridSpec(
        num_scalar_prefetch=1,
        grid=(nb,),
        in_specs=[pl.BlockSpec((s * chunks, LANES), lambda bi, idx_r: (bi, 0))],
        out_specs=pl.BlockSpec((N_EXPERTS, n_slots, d), lambda bi, idx_r: (0, bi, 0)),
        scratch_shapes=[pltpu.VMEM((n_slots * chunks, LANES), F32)],
    )
    return pl.pallas_call(
        functools.partial(_dispatch_kernel, n_slots=n_slots),
        out_shape=jax.ShapeDtypeStruct((N_EXPERTS, nb * n_slots, d), BF16),
        grid_spec=grid_spec,
        compiler_params=_cparams(("arbitrary",)),
        name="dispatch",
    )(idx, hm_rm)


W_PARTS = 4

def _expert_kernel(*refs, n_main, has_extra):
    n_in = 4 if has_extra else 2
    n_out = 2 if has_extra else 1
    acts, w_parts = refs[:n_in], refs[n_in:n_in + 3 * W_PARTS]
    outs = refs[n_in + 3 * W_PARTS:n_in + 3 * W_PARTS + n_out]
    wg_bf, wu_bf, wd_bf = refs[n_in + 3 * W_PARTS + n_out:]
    if has_extra:
        x_ref, g_ref, xx_ref, gg_ref = acts
        y_ref, yy_ref = outs
    else:
        x_ref, g_ref = acts
        (y_ref,) = outs
    i = pl.program_id(1)

    @pl.when(i == 0)
    def _():
        for k, w_bf in enumerate((wg_bf, wu_bf, wd_bf)):
            for j in range(W_PARTS):
                part = w_parts[k * W_PARTS + j]
                rows = part.shape[2]
                w_bf[j * rows:(j + 1) * rows, :] = part[0, 0].astype(BF16)

    def ffn(x_in, gate_in, out_ref):
        rows_all = x_in.shape[1]
        chunks = x_in.shape[2] // LANES
        n_part = max(1, rows_all // ROW_GROUP)
        rows_per = rows_all // n_part
        groups = [slice(p * rows_per, (p + 1) * rows_per) for p in range(n_part)]
        ups = []
        for rows in groups:
            x_e = x_in[0, rows, :]
            ups.append((jnp.dot(x_e, wg_bf[...], preferred_element_type=F32),
                        jnp.dot(x_e, wu_bf[...], preferred_element_type=F32)))
        for p, (rows, (a, u)) in enumerate(zip(groups, ups)):
            h = (a * _sigmoid(a) * u).astype(BF16)
            y = jnp.dot(h, wd_bf[...], preferred_element_type=F32) * gate_in[0, rows, 0:1]
            _store_rowmajor(out_ref, y, base=p * rows_per * chunks)

    if has_extra:
        @pl.when(i < n_main)
        def _():
            ffn(x_ref, g_ref, y_ref.at[0])

        @pl.when(i == n_main)
        def _():
            ffn(xx_ref, gg_ref, yy_ref.at[0])
    else:
        ffn(x_ref, g_ref, y_ref.at[0])


def _experts(x_e, g_e, x_extra, g_extra, wg, wu, wd, layer, tm):
    _, rows, d = x_e.shape
    f = wg.shape[3]
    chunks = d // LANES
    n_main = rows // tm
    has_extra = x_extra is not None
    main_map = lambda e, i: (e, jnp.minimum(i, n_main - 1), 0)

    def w_specs(shape):
        assert shape[0] % W_PARTS == 0 and n_main >= W_PARTS
        blk = (1, 1, shape[0] // W_PARTS, shape[1])
        return [pl.BlockSpec(blk, functools.partial(
            lambda e, i, j: (layer, jnp.minimum(e + (i > j).astype(jnp.int32), N_EXPERTS - 1), j, 0), j=j))
            for j in range(W_PARTS)]

    in_specs = [pl.BlockSpec((1, tm, d), main_map), pl.BlockSpec((1, tm, LANES), main_map)]
    out_specs = [pl.BlockSpec((1, tm * chunks, LANES), main_map)]
    out_shape = [jax.ShapeDtypeStruct((N_EXPERTS, rows * chunks, LANES), F32)]
    args = [x_e, g_e]
    if has_extra:
        rows2 = x_extra.shape[1]
        in_specs += [pl.BlockSpec((1, rows2, d), lambda e, i: (e, 0, 0)),
                     pl.BlockSpec((1, rows2, LANES), lambda e, i: (e, 0, 0))]
        out_specs.append(pl.BlockSpec((1, rows2 * chunks, LANES), lambda e, i: (e, 0, 0)))
        out_shape.append(jax.ShapeDtypeStruct((N_EXPERTS, rows2 * chunks, LANES), F32))
        args += [x_extra, g_extra]
    outs = pl.pallas_call(
        functools.partial(_expert_kernel, n_main=n_main, has_extra=has_extra),
        out_shape=out_shape,
        grid=(N_EXPERTS, n_main + int(has_extra)),
        in_specs=in_specs + w_specs((d, f)) + w_specs((d, f)) + w_specs((f, d)),
        out_specs=out_specs,
        scratch_shapes=[pltpu.VMEM((d, f), BF16), pltpu.VMEM((d, f), BF16), pltpu.VMEM((f, d), BF16)],
        compiler_params=_cparams(("arbitrary", "arbitrary")),
        name="experts",
    )(*args, *([wg] * W_PARTS), *([wu] * W_PARTS), *([wd] * W_PARTS))
    return outs if has_extra else (outs[0], None)


COMBINE_EXPERTS = 4
ADD_UNROLL = 16
NORM_CHUNK = 256


def _combine_kernel(idx_ref, y_ref, x_ref, gt_ref, g_ref, b_ref, o_ref, acc, *, n_slots):
    b = pl.program_id(0)
    j = pl.program_id(1)
    s, d = x_ref.shape[1], x_ref.shape[2]
    chunks = d // LANES

    @pl.when(j == 0)
    def _():
        acc[...] = jnp.zeros(acc.shape, F32)

    for eg in range(COMBINE_EXPERTS):
        base = (b * N_EXPERTS + j * COMBINE_EXPERTS + eg) * n_slots

        def add(i, carry):
            s0 = i * ADD_UNROLL
            first = base + s0
            src = pl.multiple_of(s0 * chunks, ADD_UNROLL * chunks)
            y_rows = y_ref[eg, pl.ds(src, ADD_UNROLL * chunks), :]
            new = []
            for u in range(ADD_UNROLL):
                dst = pl.multiple_of(idx_ref[first + u], chunks)
                new.append((dst, acc[pl.ds(dst, chunks), :] + y_rows[u * chunks:(u + 1) * chunks, :]))
            for dst, val in new:
                acc[pl.ds(dst, chunks), :] = val
            return carry

        lax.fori_loop(0, n_slots // ADD_UNROLL, add, 0)

    @pl.when(j == pl.num_programs(1) - 1)
    def _():
        for ci in range(s // NORM_CHUNK):
            rows = slice(ci * NORM_CHUNK, (ci + 1) * NORM_CHUNK)
            ym = _load_rowmajor(acc, ci * NORM_CHUNK * chunks, NORM_CHUNK, chunks)
            z = DEEPNORM_ALPHA * x_ref[0, rows, :] + (1.0 + gt_ref[0]) * ym
            o_ref[0, rows, :] = _ln(z) * g_ref[...] + b_ref[...]


def _combine(idx, y_rm, x_mid, gt, g, b, n_slots):
    nb, s, d = x_mid.shape
    chunks = d // LANES
    assert chunks == SUBLANES and N_EXPERTS % COMBINE_EXPERTS == 0 and s % NORM_CHUNK == 0
    vec = pl.BlockSpec((1, d), lambda bi, j, i_r: (0, 0))
    grid_spec = pltpu.PrefetchScalarGridSpec(
        num_scalar_prefetch=1,
        grid=(nb, N_EXPERTS // COMBINE_EXPERTS),
        in_specs=[
            pl.BlockSpec((COMBINE_EXPERTS, n_slots * chunks, LANES), lambda bi, j, i_r: (j, bi, 0)),
            pl.BlockSpec((1, s, d), lambda bi, j, i_r: (bi, 0, 0)),
            _mod_spec(gt, d), vec, vec,
        ],
        out_specs=pl.BlockSpec((1, s, d), lambda bi, j, i_r: (bi, 0, 0)),
        scratch_shapes=[pltpu.VMEM((s * chunks, LANES), F32)],
    )
    return pl.pallas_call(
        functools.partial(_combine_kernel, n_slots=n_slots),
        out_shape=jax.ShapeDtypeStruct((nb, s, d), F32),
        grid_spec=grid_spec,
        compiler_params=_cparams(("arbitrary", "arbitrary")),
        name="combine_postnorm",
    )(idx, y_rm, x_mid, gt[0], g, b)


def kernel(x, c, ctx, c_ctx, w_mod, b_mod, w_in, b_in, w_short, w_conf_dw, b_conf_dw, g_conf_ln, b_conf_ln,
           na_rpb, w_out, b_out, g_post1, b_post1, w_router, w_gate, w_up, w_down, g_post2, b_post2):
    bsz, seq, d = x.shape
    nctx = ctx.shape[1]
    cap = EC_CAPACITY_FACTOR * seq // N_EXPERTS
    cap_ctx = EC_CAPACITY_FACTOR * nctx // N_EXPERTS
    q_scale = NA_HEAD_DIM ** -0.5 * LOG2E

    cond = jnp.concatenate([c, c_ctx[None, :], jnp.zeros((MOD_ROWS - bsz - 1, d), F32)], axis=0)
    mods = _modulation(cond, w_mod, b_mod)

    lat_splits = ((3 * D_CONV, 1.0), (2 * D_CONF, 1.0), (D_NA, q_scale), (D_NA, 1.0), (D_NA, 1.0))
    lat_dtypes = (F32, F32, BF16, BF16, BF16)
    kv_splits = ((D_NA, 1.0), (D_NA, 1.0))
    b_in3 = b_in[:, None, :]

    xc = ctx
    for l in range(DEPTH):
        last = l == DEPTH - 1
        m_lat = [(mods, (l * N_MOD + k) * MOD_ROWS, True) for k in range(N_MOD)]
        m_ctx = [(mods, (l * N_MOD + k) * MOD_ROWS + bsz, False) for k in range(N_MOD)]
        b_out_l = b_out[l][None, :]
        wr = jnp.pad(w_router[l], ((0, 0), (0, LANES - N_EXPERTS)))
        wr_hi = wr.astype(BF16)
        wr_hl = jnp.concatenate([wr_hi, (wr - wr_hi.astype(F32)).astype(BF16)], axis=1)
        g1, b1 = g_post1[l][None, :], b_post1[l][None, :]
        g2, b2 = g_post2[l][None, :], b_post2[l][None, :]
        conv_w = (w_short[l], w_conf_dw[l], b_conf_dw[l], g_conf_ln[l], b_conf_ln[l])

        flat = lambda t: t.reshape(1, bsz * nctx, t.shape[-1])
        unflat = lambda t: t.reshape(bsz, nctx, t.shape[-1])
        if last:
            k_c, v_c = map(unflat, _inproj(flat(xc), m_ctx[0], m_ctx[1], w_in, b_in3, l, OFF_K,
                                           kv_splits, (BF16, BF16), tm=512))
        else:
            uac, ubc, q_c, k_c, v_c = map(unflat, _inproj(flat(xc), m_ctx[0], m_ctx[1], w_in, b_in3, l, 0,
                                                          lat_splits, lat_dtypes, tm=512))

        ua, ub, q, k, v = _inproj(x, m_lat[0], m_lat[1], w_in, b_in3, l, 0, lat_splits, lat_dtypes, tm=512)
        yab = _conv_mixers(ua, ub, *conv_w)
        yc = _neighbourhood_attention(q, k, v, k_c, v_c, na_rpb[l])
        x_mid, hm, logits = _outproj(yab, yc, x, w_out, l, b_out_l, m_lat[2], g1, b1, m_lat[3], m_lat[4],
                                     wr_hi, wr_hl, tm=512)

        idx, gates = _slot_lists(*_route(logits, cap, 0), cap, merge=False)
        idx = idx.reshape(-1)
        x_e = _dispatch(idx, hm, bsz, seq, d, cap)

        x_ec = gates_c = None
        if not last:
            yabc = _conv_mixers(uac, ubc, *conv_w)
            ycc = _context_attention(q_c, k_c, v_c)
            xc_mid, hmc, logits_c = _outproj(flat(yabc), flat(ycc), flat(xc), w_out, l, b_out_l, m_ctx[2], g1, b1,
                                             m_ctx[3], m_ctx[4], wr_hi, wr_hl, tm=512)
            n_c = bsz * cap_ctx
            idx_c, gates_c = _slot_lists(*_route(unflat(logits_c), cap_ctx, cap_ctx), n_c, merge=True)
            idx_c = idx_c.reshape(-1)
            x_ec = _dispatch(idx_c, hmc, 1, bsz * nctx, d, n_c)

        y_e, y_ec = _experts(x_e, gates, x_ec, gates_c, w_gate, w_up, w_down, l, tm=512)
        x = _combine(idx, y_e, x_mid, m_lat[5], g2, b2, cap)
        if not last:
            xc = unflat(_combine(idx_c, y_ec, xc_mid, m_ctx[5], g2, b2, n_c))
    return x
```

```python
import functools
import math

import numpy as np
import jax
import jax.numpy as jnp
from jax import lax
from jax.experimental import pallas as pl
from jax.experimental.pallas import tpu as pltpu

F32 = jnp.float32
BF16 = jnp.bfloat16

D_MODEL = 1024
DEPTH = 2
GRID_W = 64
D_CONV = D_MODEL // 4
D_CONF = D_MODEL // 4
NA_HEAD_DIM = 64
D_NA = D_MODEL - D_CONV - D_CONF
N_NA_HEADS = D_NA // NA_HEAD_DIM
SHORT_CONV_W = 3
CONF_CONV_W = 31
NA_WIN_ROWS_MAX = 8
NA_WIN_COLS = 16
N_EXPERTS = 16
EC_CAPACITY_FACTOR = 2
D_EXPERT = 1024
LN_EPS = 1e-5
DEEPNORM_ALPHA = (2.0 * DEPTH) ** 0.25
NEG_INF = -1e30
LOG2E = math.log2(math.e)

OFF_A = 0
OFF_B = OFF_A + 3 * D_CONV
OFF_Q = OFF_B + 2 * D_CONF
OFF_K = OFF_Q + D_NA
OFF_V = OFF_K + D_NA
D_IN = OFF_V + D_NA

LANES = 128
SUBLANES = 8
MOD_ROWS = 16
VMEM_LIMIT = 56 * 1024 * 1024
ATTN_ROWS = 8
N_PAIRS = D_NA // LANES
ROW_GROUP = 256
OUT_ROW_GROUP = 128
HI = lax.Precision.HIGHEST


def _cparams(sem):
    return pltpu.CompilerParams(dimension_semantics=sem, vmem_limit_bytes=VMEM_LIMIT)


def _ln(x):
    mu = jnp.mean(x, axis=-1, keepdims=True)
    xc = x - mu
    var = jnp.mean(xc * xc, axis=-1, keepdims=True)
    return xc * lax.rsqrt(var + LN_EPS)


def _sigmoid(x):
    return 1.0 / (1.0 + jnp.exp(-x))


def _mod_kernel(cond_ref, w_ref, b_ref, o_ref):
    s = cond_ref[...]
    s = s * _sigmoid(s)
    w = w_ref[0]
    s_hi, w_hi = s.astype(BF16), w.astype(BF16)
    s_lo = (s - s_hi.astype(F32)).astype(BF16)
    w_lo = (w - w_hi.astype(F32)).astype(BF16)
    o_ref[0] = (jnp.dot(s_hi, w_hi, preferred_element_type=F32) + jnp.dot(s_lo, w_hi, preferred_element_type=F32)
                + jnp.dot(s_hi, w_lo, preferred_element_type=F32) + b_ref[0])


N_MOD = 6


def _modulation(cond, w_mod, b_mod):
    n_l, d, n = w_mod.shape
    out = pl.pallas_call(
        _mod_kernel,
        out_shape=jax.ShapeDtypeStruct((n_l * N_MOD, MOD_ROWS, d), F32),
        grid=(n_l, N_MOD),
        in_specs=[
            pl.BlockSpec((MOD_ROWS, d), lambda l, k: (0, 0)),
            pl.BlockSpec((1, d, d), lambda l, k: (l, 0, k)),
            pl.BlockSpec((1, 1, d), lambda l, k: (l, 0, k)),
        ],
        out_specs=pl.BlockSpec((1, MOD_ROWS, d), lambda l, k: (l * N_MOD + k, 0, 0)),
        compiler_params=_cparams(("arbitrary", "arbitrary")),
        name="modulation",
    )(cond, w_mod, b_mod.reshape(n_l, 1, n))
    return out.reshape(n_l * N_MOD * MOD_ROWS, 1, d)


def _mod_spec(mod, d):
    _, row0, per_sample = mod
    if per_sample:
        return pl.BlockSpec((1, 1, d), lambda bi, *_: (row0 + bi, 0, 0))
    return pl.BlockSpec((1, 1, d), lambda bi, *_: (row0, 0, 0))


def _first_step():
    return (pl.program_id(0) == 0) & (pl.program_id(1) == 0)


def _inproj_kernel(x_ref, sh_ref, sc_ref, w_ref, b_ref, *rest, splits, col0):
    *o_refs, w_bf = rest

    @pl.when(_first_step())
    def _():
        w_bf[...] = w_ref[0].astype(BF16)

    tm = x_ref.shape[1]
    n_part = max(1, tm // ROW_GROUP)
    rows_per = tm // n_part
    groups = [slice(part_i * rows_per, (part_i + 1) * rows_per) for part_i in range(n_part)]
    hs = [(_ln(x_ref[0, rows, :]) * (1.0 + sc_ref[0]) + sh_ref[0]).astype(BF16) for rows in groups]
    for rows, h in zip(groups, hs):
        u = jnp.dot(h, w_bf[:, col0:], preferred_element_type=F32) + b_ref[0, :, col0:]
        off = 0
        for o_ref, (width, scale) in zip(o_refs, splits):
            part = u[:, off:off + width]
            if scale != 1.0:
                part = part * scale
            o_ref[0, rows, :] = part.astype(o_ref.dtype)
            off += width


def _inproj(x, sh, sc, w, b, layer, col0, splits, dtypes, tm):
    nb, s, d = x.shape
    n = w.shape[2]
    return pl.pallas_call(
        functools.partial(_inproj_kernel, splits=splits, col0=col0),
        out_shape=[jax.ShapeDtypeStruct((nb, s, wd), dt) for (wd, _), dt in zip(splits, dtypes)],
        grid=(nb, s // tm),
        in_specs=[
            pl.BlockSpec((1, tm, d), lambda bi, i: (bi, i, 0)),
            _mod_spec(sh, d),
            _mod_spec(sc, d),
            pl.BlockSpec((1, d, n), lambda bi, i: (layer, 0, 0), pipeline_mode=pl.Buffered(1)),
            pl.BlockSpec((1, 1, n), lambda bi, i: (layer, 0, 0)),
        ],
        out_specs=[pl.BlockSpec((1, tm, wd), lambda bi, i: (bi, i, 0)) for wd, _ in splits],
        scratch_shapes=[pltpu.VMEM((d, n), BF16)],
        compiler_params=_cparams(("arbitrary", "arbitrary")),
        name="inproj",
    )(x, sh[0], sc[0], w, b)


CONV_CHUNK = 128
Z_PAD = 8
H_PAD = 16


def _conv_kernel(ua_ref, ub_ref, ws_ref, wd_ref, bd_ref, g_ref, b_ref, o_ref, z_scr, h_scr, *, seq):
    c = D_CONV
    z_scr[0:Z_PAD, :] = jnp.zeros((Z_PAD, c), F32)
    z_scr[Z_PAD + seq:2 * Z_PAD + seq, :] = jnp.zeros((Z_PAD, c), F32)
    h_scr[0, 0:H_PAD, :] = jnp.zeros((H_PAD, c), F32)
    h_scr[0, H_PAD + seq:2 * H_PAD + seq, :] = jnp.zeros((H_PAD, c), F32)
    z_scr[Z_PAD:Z_PAD + seq, :] = ua_ref[0, :, c:2 * c] * ua_ref[0, :, 2 * c:3 * c]
    h_scr[0, H_PAD:H_PAD + seq, :] = ub_ref[0, :, 0:c] * _sigmoid(ub_ref[0, :, c:2 * c])
    n_rows = seq + 2 * H_PAD
    h_all = h_scr[0]
    for r in range(1, SUBLANES):
        h_scr[r] = pltpu.roll(h_all, n_rows - r, axis=0)
    tc = min(CONV_CHUNK, seq)
    for ci in range(seq // tc):
        t0 = ci * tc
        acc = ws_ref[0:1, :] * z_scr[t0 + Z_PAD - 1:t0 + Z_PAD - 1 + tc, :]
        for j in range(1, SHORT_CONV_W):
            s0 = t0 + Z_PAD - 1 + j
            acc = acc + ws_ref[j:j + 1, :] * z_scr[s0:s0 + tc, :]
        ya = ua_ref[0, t0:t0 + tc, 0:c] * acc
        hb = bd_ref[...]
        for j in range(CONF_CONV_W):
            s0 = t0 + H_PAD - CONF_CONV_W // 2 + j
            a0 = s0 - s0 % SUBLANES
            hb = hb + wd_ref[j:j + 1, :] * h_scr[s0 % SUBLANES, a0:a0 + tc, :]
        hn = _ln(hb) * g_ref[...] + b_ref[...]
        yb = hn * _sigmoid(hn)
        o_ref[0, t0:t0 + tc, 0:c] = ya.astype(o_ref.dtype)
        o_ref[0, t0:t0 + tc, c:2 * c] = yb.astype(o_ref.dtype)


def _conv_mixers(ua, ub, w_short, w_dw, b_dw, g_ln, b_ln):
    nb, s, _ = ua.shape
    c = D_CONV
    full = lambda shape: pl.BlockSpec(shape, lambda bi: (0,) * len(shape))
    return pl.pallas_call(
        functools.partial(_conv_kernel, seq=s),
        out_shape=jax.ShapeDtypeStruct((nb, s, 2 * c), BF16),
        grid=(nb,),
        in_specs=[
            pl.BlockSpec((1, s, 3 * c), lambda bi: (bi, 0, 0)),
            pl.BlockSpec((1, s, 2 * c), lambda bi: (bi, 0, 0)),
            full((SHORT_CONV_W, c)), full((CONF_CONV_W, c)), full((1, c)), full((1, c)), full((1, c)),
        ],
        out_specs=pl.BlockSpec((1, s, 2 * c), lambda bi: (bi, 0, 0)),
        scratch_shapes=[pltpu.VMEM((s + 2 * Z_PAD, c), F32), pltpu.VMEM((SUBLANES, s + 2 * H_PAD, c), F32)],
        compiler_params=_cparams(("arbitrary",)),
        name="conv_mixers",
    )(ua, ub, w_short, w_dw, b_dw.reshape(1, c), g_ln.reshape(1, c), b_ln.reshape(1, c))


SUB_ROWS = 2
WIN_ROWS = SUB_ROWS + NA_WIN_ROWS_MAX - 1
BAND_KEYS = WIN_ROWS * GRID_W
SUB_Q = SUB_ROWS * GRID_W
N_DROW = 2 * NA_WIN_ROWS_MAX - 1
N_DCOL = 2 * NA_WIN_COLS - 1


def _bias_kernel(rpb_ref, nxt_ref, o_ref):
    n_cols = o_ref.shape[1]
    col = lax.broadcasted_iota(jnp.int32, (LANES, n_cols), 1)
    qc = col >> (LANES.bit_length() - 1)
    kc = col & (GRID_W - 1)
    d_col = jnp.clip(kc - qc + (NA_WIN_COLS - 1), 0, N_DCOL - 1)
    onehot = (lax.broadcasted_iota(jnp.int32, (LANES, n_cols), 0) == d_col).astype(F32)
    own = jnp.dot(rpb_ref[...], onehot, preferred_element_type=F32, precision=HI) * LOG2E
    nxt = jnp.dot(nxt_ref[...], onehot, preferred_element_type=F32, precision=HI) * LOG2E
    c0 = jnp.clip(qc - NA_WIN_COLS // 2, 0, GRID_W - NA_WIN_COLS)
    inside = (kc >= c0) & (kc < c0 + NA_WIN_COLS)
    second = (col & GRID_W) != 0
    o_ref[0:LANES, :] = jnp.where(inside, jnp.where(second, nxt, own), NEG_INF)
    o_ref[LANES:2 * LANES, :] = jnp.where(inside & ~second, own, NEG_INF)
    o_ref[2 * LANES:3 * LANES, :] = jnp.where(inside & second, own, NEG_INF)
    o_ref[3 * LANES:, :] = jnp.full((o_ref.shape[0] - 3 * LANES, n_cols), NEG_INF, F32)


N_SLABS = -(-WIN_ROWS // 2)
TILES_PER_KIND = LANES
MASKED_TILE = 3 * TILES_PER_KIND


def _na_plan(rows):
    wr = min(NA_WIN_ROWS_MAX, rows)
    assert wr == NA_WIN_ROWS_MAX and rows % SUB_ROWS == 0 and rows >= WIN_ROWS
    row_start = np.clip(np.arange(rows) - wr // 2, 0, rows - wr)
    w0s, tiles = [], []
    for r0 in range(0, rows, SUB_ROWS):
        w0 = int(np.clip(r0 - wr // 2, 0, rows - WIN_ROWS))
        for iq in range(SUB_ROWS):
            r = r0 + iq
            assert row_start[r] >= w0 and row_start[r] + wr <= w0 + WIN_ROWS
            ok = lambda w: w < WIN_ROWS and row_start[r] <= w0 + w < row_start[r] + wr
            d_row = lambda w: w0 + w - r + NA_WIN_ROWS_MAX - 1
            for j in range(N_SLABS):
                lo, hi = ok(2 * j), ok(2 * j + 1)
                if lo and hi:
                    tiles.append(d_row(2 * j))
                elif lo:
                    tiles.append(TILES_PER_KIND + d_row(2 * j))
                elif hi:
                    tiles.append(2 * TILES_PER_KIND + d_row(2 * j + 1))
                else:
                    tiles.append(-1)
        w0s.append(w0)
    return np.array(w0s, np.int32), np.array(tiles, np.int32)


def _na_bias(rpb):
    n_hd = N_NA_HEADS * N_DROW
    assert GRID_W == 64 and 2 * GRID_W == LANES and n_hd <= TILES_PER_KIND and N_DCOL <= LANES
    rpb = rpb.astype(F32)
    pad = lambda t: jnp.pad(t.reshape(n_hd, N_DCOL), ((0, LANES - n_hd), (0, LANES - N_DCOL)))
    nxt = jnp.concatenate([rpb[:, 1:], jnp.zeros_like(rpb[:, :1])], axis=1)
    n_tiles = 3 * TILES_PER_KIND + SUBLANES
    table = pl.pallas_call(
        _bias_kernel,
        out_shape=jax.ShapeDtypeStruct((n_tiles, GRID_W * LANES), F32),
        compiler_params=pltpu.CompilerParams(vmem_limit_bytes=VMEM_LIMIT),
        name="na_bias",
    )(pad(rpb), pad(nxt))
    return table.reshape(n_tiles, GRID_W, LANES)


def _lane_reduce(xs, combine, reduce, neutral):
    chunks = []
    for x in xs:
        rows, n = x.shape
        n_full = n // LANES
        chunks += [x[:, j * LANES:(j + 1) * LANES] for j in range(n_full)]
        if n % LANES:
            fill = jnp.full((rows, LANES - n % LANES), neutral, x.dtype)
            chunks.append(jnp.concatenate([x[:, n_full * LANES:], fill], axis=1))
    return reduce(functools.reduce(combine, chunks), axis=-1, keepdims=True)


def _attn_kernel(w0_ref, tile_ref, q_ref, k_ref, v_ref, kc_ref, vc_ref, *rest, banded, n_sub, sub_q):
    if banded:
        bias_ref, o_ref = rest
    else:
        (o_ref,) = rest
    lane = lax.broadcasted_iota(jnp.int32, (sub_q, LANES), 1)
    first = lane < NA_HEAD_DIM
    nt = (((1,), (1,)), ((), ()))
    stages = [(si, p) for si in range(n_sub) for p in range(N_PAIRS)]

    def window(si):
        blk = pl.program_id(1) * n_sub + si
        return blk, pl.multiple_of(w0_ref[blk] * GRID_W, GRID_W)

    def bias(blk, head):
        row_blocks = []
        for iq in range(SUB_ROWS):
            slabs = []
            for j in range(N_SLABS):
                t = tile_ref[(blk * SUB_ROWS + iq) * N_SLABS + j]
                tile = bias_ref[jnp.where(t < 0, MASKED_TILE, t + head * N_DROW)]
                width = min(LANES, BAND_KEYS - j * LANES)
                slabs.append(tile[:, :width])
            row_blocks.append(jnp.concatenate(slabs, axis=1))
        return jnp.concatenate(row_blocks, axis=0)

    def scores(si, p):
        cols = slice(p * LANES, (p + 1) * LANES)
        q_p = q_ref[0, si * sub_q:(si + 1) * sub_q, cols]
        zero = jnp.zeros_like(q_p)
        qq = jnp.concatenate([jnp.where(first, q_p, zero), jnp.where(first, zero, q_p)], axis=0)
        parts = [lax.dot_general(qq, kc_ref[0, :, cols], nt, preferred_element_type=F32)]
        if banded:
            blk, start = window(si)
            both = jnp.concatenate([bias(blk, 2 * p), bias(blk, 2 * p + 1)], axis=0)
            parts.append(lax.dot_general(qq, k_ref[0, pl.ds(start, BAND_KEYS), cols], nt,
                                         preferred_element_type=F32) + both)
        return parts

    def weights(parts):
        m = _lane_reduce(parts, jnp.maximum, jnp.max, NEG_INF)
        es = [jnp.exp2(s - m) for s in parts]
        den = _lane_reduce(es, jnp.add, jnp.sum, 0.0)
        return [e.astype(BF16) for e in es], den

    def values(si, p, es, den):
        cols = slice(p * LANES, (p + 1) * LANES)
        o = jnp.dot(es[0], vc_ref[0, :, cols], preferred_element_type=F32)
        if banded:
            _, start = window(si)
            o = o + jnp.dot(es[1], v_ref[0, pl.ds(start, BAND_KEYS), cols], preferred_element_type=F32)
        o = o * (1.0 / den)
        out = jnp.where(first, o[:sub_q], o[sub_q:])
        o_ref[0, si * sub_q:(si + 1) * sub_q, cols] = out.astype(o_ref.dtype)

    nxt = scores(*stages[0])
    pending = None
    for i, (si, p) in enumerate(stages):
        cur = nxt
        if i + 1 < len(stages):
            nxt = scores(*stages[i + 1])
        if pending is not None:
            values(*pending)
        es, den = weights(cur)
        pending = (si, p, es, den)
    values(*pending)


def _neighbourhood_attention(q, k, v, kc, vc, rpb):
    nb, s, dn = q.shape
    rows = s // GRID_W
    nctx = kc.shape[1]
    w0s, tiles = _na_plan(rows)
    bias = _na_bias(rpb)
    n_sub = ATTN_ROWS // SUB_ROWS
    m_rows = ATTN_ROWS * GRID_W
    grid_spec = pltpu.PrefetchScalarGridSpec(
        num_scalar_prefetch=2,
        grid=(nb, rows // ATTN_ROWS),
        in_specs=[
            pl.BlockSpec((1, m_rows, dn), lambda bi, i, w0, pat: (bi, i, 0)),
            pl.BlockSpec((1, s, dn), lambda bi, i, w0, pat: (bi, 0, 0)),
            pl.BlockSpec((1, s, dn), lambda bi, i, w0, pat: (bi, 0, 0)),
            pl.BlockSpec((1, nctx, dn), lambda bi, i, w0, pat: (bi, 0, 0)),
            pl.BlockSpec((1, nctx, dn), lambda bi, i, w0, pat: (bi, 0, 0)),
            pl.BlockSpec(bias.shape, lambda bi, i, w0, pat: (0, 0, 0), pipeline_mode=pl.Buffered(1)),
        ],
        out_specs=pl.BlockSpec((1, m_rows, dn), lambda bi, i, w0, pat: (bi, i, 0)),
    )
    return pl.pallas_call(
        functools.partial(_attn_kernel, banded=True, n_sub=n_sub, sub_q=SUB_Q),
        out_shape=jax.ShapeDtypeStruct((nb, s, dn), BF16),
        grid_spec=grid_spec,
        compiler_params=_cparams(("arbitrary", "arbitrary")),
        name="neighbourhood_attention",
    )(jnp.asarray(w0s), jnp.asarray(tiles), q, k, v, kc, vc, bias)


def _context_attention(q, kc, vc):
    nb, s, dn = q.shape
    spec = pl.BlockSpec((1, s, dn), lambda bi, i, w0, pat: (bi, 0, 0))
    grid_spec = pltpu.PrefetchScalarGridSpec(
        num_scalar_prefetch=2, grid=(nb, 1), in_specs=[spec] * 5, out_specs=spec)
    dummy = jnp.zeros((1,), jnp.int32)
    return pl.pallas_call(
        functools.partial(_attn_kernel, banded=False, n_sub=1, sub_q=s),
        out_shape=jax.ShapeDtypeStruct((nb, s, dn), BF16),
        grid_spec=grid_spec,
        compiler_params=_cparams(("arbitrary", "arbitrary")),
        name="context_attention",
    )(dummy, dummy, q, kc, vc, kc, vc)


def _store_rowmajor(ref, val, base=0):
    n, width = val.shape
    chunks = width // LANES
    for c in range(chunks):
        ref[pl.ds(base + c, n, stride=chunks), :] = val[:, c * LANES:(c + 1) * LANES]


def _load_rowmajor(ref, base, n, chunks):
    return jnp.concatenate([ref[pl.ds(base + c, n, stride=chunks), :] for c in range(chunks)], axis=1)


def _outproj_kernel(yab_ref, yc_ref, x_ref, w_ref, bo_ref, gt_ref, g_ref, b_ref, sh_ref, sc_ref,
                    wrh_ref, wrhl_ref, xmid_ref, hm_ref, lg_ref, w_bf):
    @pl.when(_first_step())
    def _():
        w_bf[...] = w_ref[0].astype(BF16)

    half = yab_ref.shape[2]
    tm = x_ref.shape[1]
    chunks = x_ref.shape[2] // LANES
    n_part = max(1, tm // OUT_ROW_GROUP)
    rows_per = tm // n_part
    groups = [slice(part * rows_per, (part + 1) * rows_per) for part in range(n_part)]
    ys = [jnp.dot(yab_ref[0, rows, :], w_bf[0:half, :], preferred_element_type=F32)
          + jnp.dot(yc_ref[0, rows, :], w_bf[half:, :], preferred_element_type=F32) + bo_ref[...]
          for rows in groups]
    for part, (rows, y) in enumerate(zip(groups, ys)):
        xm = _ln(DEEPNORM_ALPHA * x_ref[0, rows, :] + (1.0 + gt_ref[0]) * y) * g_ref[...] + b_ref[...]
        xmid_ref[0, rows, :] = xm
        hm = _ln(xm) * (1.0 + sc_ref[0]) + sh_ref[0]
        _store_rowmajor(hm_ref, hm, base=part * rows_per * chunks)
        hm_hi = hm.astype(BF16)
        hm_lo = (hm - hm_hi.astype(F32)).astype(BF16)
        both = jnp.dot(hm_hi, wrhl_ref[...], preferred_element_type=F32)
        lg_ref[0, rows, :] = (both[:, :LANES] + both[:, LANES:]
                              + jnp.dot(hm_lo, wrh_ref[...], preferred_element_type=F32))


def _outproj(yab, yc, x, w, layer, bo, gt, g, b, sh, sc, wr_hi, wr_hl, tm):
    nb, s, d = x.shape
    half = yab.shape[2]
    vec = pl.BlockSpec((1, d), lambda bi, i: (0, 0))
    tok = lambda width: pl.BlockSpec((1, tm, width), lambda bi, i: (bi, i, 0))
    n_i = s // tm
    return pl.pallas_call(
        _outproj_kernel,
        out_shape=[jax.ShapeDtypeStruct((nb, s, d), F32),
                   jax.ShapeDtypeStruct((nb * s * (d // LANES), LANES), F32),
                   jax.ShapeDtypeStruct((nb, s, LANES), F32)],
        grid=(nb, n_i),
        in_specs=[tok(half), tok(half), tok(d),
                  pl.BlockSpec((1, d, d), lambda bi, i: (layer, 0, 0), pipeline_mode=pl.Buffered(1)),
                  vec, _mod_spec(gt, d), vec, vec,
                  _mod_spec(sh, d), _mod_spec(sc, d), pl.BlockSpec((d, LANES), lambda bi, i: (0, 0)),
                  pl.BlockSpec((d, 2 * LANES), lambda bi, i: (0, 0))],
        out_specs=[tok(d), pl.BlockSpec((tm * (d // LANES), LANES), lambda bi, i: (bi * n_i + i, 0)),
                   tok(LANES)],
        scratch_shapes=[pltpu.VMEM((d, d), BF16)],
        compiler_params=_cparams(("arbitrary", "arbitrary")),
        name="outproj_postnorm",
    )(yab, yc, x, w, bo, gt[0], g, b, sh[0], sc[0], wr_hi, wr_hl)


CUM_CHUNK = 256
F32_EXP_BIAS = 127
F32_MANT_BITS = 23


def _prefix_count(mask_f32, tri):
    rows, n = mask_f32.shape
    tc = min(CUM_CHUNK, n)
    base = jnp.zeros((rows, 1), F32)
    parts = []
    for ci in range(n // tc):
        blk = mask_f32[:, ci * tc:(ci + 1) * tc]
        parts.append(jnp.dot(blk.astype(BF16), tri[:tc, :tc], preferred_element_type=F32) + base)
        base = base + jnp.sum(blk, axis=-1, keepdims=True)
    return jnp.concatenate(parts, axis=-1)


def _pow2(k):
    return pltpu.bitcast((k + F32_EXP_BIAS) << F32_MANT_BITS, F32)


def _route_kernel(lg_ref, slot_c_ref, gate_t_ref, *, cap, slot_stride):
    nb = lg_ref.shape[0]
    assert nb * N_EXPERTS == LANES
    rows = []
    for b in range(nb):
        lg = lg_ref[b]
        lane = lax.broadcasted_iota(jnp.int32, lg.shape, 1)
        lgm = jnp.where(lane < N_EXPERTS, lg, NEG_INF)
        ex = jnp.exp(lgm - jnp.max(lgm, axis=-1, keepdims=True))
        aff = ex / jnp.sum(ex, axis=-1, keepdims=True)
        rows.append(aff.T[0:N_EXPERTS, :])
    a = jnp.concatenate(rows, axis=0)
    capf = float(cap)

    def enough(t):
        return jnp.sum((a >= t).astype(F32), axis=-1, keepdims=True) >= capf

    def exp_step(_, carry):
        lo, hi = carry
        mid = lo + ((hi - lo + 1) >> 1)
        ok = enough(_pow2(mid))
        return jnp.where(ok, mid, lo), jnp.where(ok, hi, mid - 1)

    k_lo = jnp.full((LANES, 1), -F32_EXP_BIAS, jnp.int32)
    k_hi = jnp.zeros((LANES, 1), jnp.int32)
    k_lo, _ = lax.fori_loop(0, 7, exp_step, (k_lo, k_hi))
    base = _pow2(k_lo)

    def mant_step(_, carry):
        t, step = carry
        step = step * 0.5
        cand = t + step
        return jnp.where(enough(cand), cand, t), step

    thr, _ = lax.fori_loop(0, F32_MANT_BITS, mant_step, (base, base))

    r_i = lax.broadcasted_iota(jnp.int32, (CUM_CHUNK, CUM_CHUNK), 0)
    c_i = lax.broadcasted_iota(jnp.int32, (CUM_CHUNK, CUM_CHUNK), 1)
    tri = (r_i < c_i).astype(BF16)
    gt = (a > thr).astype(F32)
    eq = (a == thr).astype(F32)
    need = capf - jnp.sum(gt, axis=-1, keepdims=True)
    sel = gt + eq * (_prefix_count(eq, tri) < need).astype(F32)
    pos = _prefix_count(sel, tri)
    sample = lax.broadcasted_iota(jnp.int32, (LANES, 1), 0) >> (N_EXPERTS.bit_length() - 1)
    slot = jnp.where(sel > 0.0, pos + (sample * slot_stride).astype(F32), -1.0)
    for b in range(nb):
        lo = b * N_EXPERTS
        gate_t_ref[b] = a[lo:lo + N_EXPERTS, :]
        rolled = slot if b == 0 else jnp.concatenate([slot[lo:, :], slot[:lo, :]], axis=0)
        slot_c_ref[b] = rolled.T


def _route(logits, cap, slot_stride):
    nb, s, _ = logits.shape
    whole = lambda shape: pl.BlockSpec(shape, lambda i: (0,) * len(shape))
    return pl.pallas_call(
        functools.partial(_route_kernel, cap=cap, slot_stride=slot_stride),
        out_shape=[jax.ShapeDtypeStruct((nb, s, LANES), F32), jax.ShapeDtypeStruct((nb, N_EXPERTS, s), F32)],
        grid=(1,),
        in_specs=[whole((nb, s, LANES))],
        out_specs=[whole((nb, s, LANES)), whole((nb, N_EXPERTS, s))],
        compiler_params=_cparams(("arbitrary",)),
        name="route",
    )(logits)


TOK_SPLIT = 64


def _slot_list_kernel(slot_ref, gate_ref, idx_ref, g_ref, *, n_slots, tok_stride, merge):
    s = slot_ref.shape[1]
    b = pl.program_id(0)
    assert n_slots <= 256
    slot_id = lax.broadcasted_iota(jnp.int32, (s, n_slots), 1).astype(F32).astype(BF16)
    one, zero = jnp.ones((s, n_slots), BF16), jnp.zeros((s, n_slots), BF16)
    tok = lax.broadcasted_iota(jnp.int32, (1, s), 1) + b * tok_stride
    tok_hi = (tok >> (TOK_SPLIT.bit_length() - 1)).astype(F32)
    tok_lo = (tok & (TOK_SPLIT - 1)).astype(F32)
    zeros = jnp.zeros((SUBLANES - 5, s), F32)
    idx_rows, g_rows = [], []
    for e in range(N_EXPERTS):
        taken = jnp.broadcast_to(slot_ref[0, :, e:e + 1].astype(BF16), (s, n_slots))
        hit = jnp.where(taken == slot_id, one, zero)
        g0 = gate_ref[0, e:e + 1, :]
        g_hi = g0.astype(BF16).astype(F32)
        g_mid = (g0 - g_hi).astype(BF16).astype(F32)
        g_lo = g0 - g_hi - g_mid
        lhs = jnp.concatenate([tok_hi, tok_lo, g_hi, g_mid, g_lo, zeros], axis=0).astype(BF16)
        out = jnp.dot(lhs, hit, preferred_element_type=F32)
        idx_rows.append(out[0:1] * float(TOK_SPLIT) + out[1:2])
        g_rows.append(out[2:3] + out[3:4] + out[4:5])
    idx = jnp.concatenate(idx_rows, axis=0).astype(jnp.int32) * SUBLANES
    g = jnp.concatenate(g_rows + [jnp.zeros((LANES - N_EXPERTS, n_slots), F32)], axis=0)
    g_t = g.T
    g_cols = [jnp.broadcast_to(g_t[:, e:e + 1], (n_slots, LANES)) for e in range(N_EXPERTS)]
    if merge:
        @pl.when(b == 0)
        def _():
            idx_ref[0] = idx
            for e in range(N_EXPERTS):
                g_ref[e] = g_cols[e]

        @pl.when(b > 0)
        def _():
            idx_ref[0] = idx_ref[0] + idx
            for e in range(N_EXPERTS):
                g_ref[e] = g_ref[e] + g_cols[e]
    else:
        idx_ref[0] = idx
        for e in range(N_EXPERTS):
            g_ref[e] = g_cols[e]


def _slot_lists(slot_c, gate_t, n_slots, merge):
    nb, s, _ = slot_c.shape
    nbo = 1 if merge else nb
    idx_map = (lambda bi: (0, 0, 0)) if merge else (lambda bi: (bi, 0, 0))
    g_map = (lambda bi: (0, 0, 0)) if merge else (lambda bi: (0, bi, 0))
    return pl.pallas_call(
        functools.partial(_slot_list_kernel, n_slots=n_slots, tok_stride=s if merge else 0, merge=merge),
        out_shape=[jax.ShapeDtypeStruct((nbo, N_EXPERTS, n_slots), jnp.int32),
                   jax.ShapeDtypeStruct((N_EXPERTS, nbo * n_slots, LANES), F32)],
        grid=(nb,),
        in_specs=[pl.BlockSpec((1, s, LANES), lambda bi: (bi, 0, 0)),
                  pl.BlockSpec((1, N_EXPERTS, s), lambda bi: (bi, 0, 0))],
        out_specs=[pl.BlockSpec((1, N_EXPERTS, n_slots), idx_map),
                   pl.BlockSpec((N_EXPERTS, n_slots, LANES), g_map)],
        compiler_params=_cparams(("arbitrary",)),
        name="slot_lists",
    )(slot_c, gate_t)


ROW_UNROLL = 32


def _dispatch_kernel(idx_ref, hm_ref, x_ref, rows_scr, *, n_slots):
    b = pl.program_id(0)
    chunks = x_ref.shape[2] // LANES

    def emit(e, s0):
        src = pl.multiple_of(s0 * chunks, ROW_UNROLL * chunks)
        rows = _load_rowmajor(rows_scr.at[e % 2], src, ROW_UNROLL, chunks)
        x_ref[e, pl.ds(pl.multiple_of(s0, ROW_UNROLL), ROW_UNROLL), :] = rows.astype(x_ref.dtype)

    for e in range(N_EXPERTS + 1):
        base = (b * N_EXPERTS + e) * n_slots

        def move(i, carry):
            s0 = i * ROW_UNROLL
            if e < N_EXPERTS:
                first = base + s0
                tiles = [hm_ref[pl.ds(pl.multiple_of(idx_ref[first + u], chunks), chunks), :]
                         for u in range(ROW_UNROLL)]
                dst = pl.multiple_of(s0 * chunks, ROW_UNROLL * chunks)
                rows_scr[e % 2, pl.ds(dst, ROW_UNROLL * chunks), :] = jnp.concatenate(tiles, axis=0)
            if e > 0:
                emit(e - 1, s0)
            return carry

        lax.fori_loop(0, n_slots // ROW_UNROLL, move, 0)


def _dispatch(idx, hm_rm, nb, s, d, n_slots):
    chunks = d // LANES
    assert chunks == SUBLANES and n_slots % ROW_UNROLL == 0
    grid_spec = pltpu.PrefetchScalarGridSpec(
        num_scalar_prefetch=1,
        grid=(nb,),
        in_specs=[pl.BlockSpec((s * chunks, LANES), lambda bi, idx_r: (bi, 0))],
        out_specs=pl.BlockSpec((N_EXPERTS, n_slots, d), lambda bi, idx_r: (0, bi, 0)),
        scratch_shapes=[pltpu.VMEM((2, n_slots * chunks, LANES), F32)],
    )
    return pl.pallas_call(
        functools.partial(_dispatch_kernel, n_slots=n_slots),
        out_shape=jax.ShapeDtypeStruct((N_EXPERTS, nb * n_slots, d), BF16),
        grid_spec=grid_spec,
        compiler_params=_cparams(("arbitrary",)),
        name="dispatch",
    )(idx, hm_rm)


W_PARTS = 4

def _expert_kernel(*refs, n_main, has_extra):
    n_in = 4 if has_extra else 2
    n_out = 2 if has_extra else 1
    acts, w_parts = refs[:n_in], refs[n_in:n_in + 3 * W_PARTS]
    outs = refs[n_in + 3 * W_PARTS:n_in + 3 * W_PARTS + n_out]
    wg_bf, wu_bf, wd_bf = refs[n_in + 3 * W_PARTS + n_out:]
    if has_extra:
        x_ref, g_ref, xx_ref, gg_ref = acts
        y_ref, yy_ref = outs
    else:
        x_ref, g_ref = acts
        (y_ref,) = outs
    i = pl.program_id(1)

    @pl.when(i == 0)
    def _():
        for k, w_bf in enumerate((wg_bf, wu_bf, wd_bf)):
            for j in range(W_PARTS):
                part = w_parts[k * W_PARTS + j]
                rows = part.shape[2]
                w_bf[j * rows:(j + 1) * rows, :] = part[0, 0].astype(BF16)

    def ffn(x_in, gate_in, out_ref):
        rows_all = x_in.shape[1]
        chunks = x_in.shape[2] // LANES
        n_part = max(1, rows_all // ROW_GROUP)
        rows_per = rows_all // n_part
        groups = [slice(p * rows_per, (p + 1) * rows_per) for p in range(n_part)]
        ups = []
        for rows in groups:
            x_e = x_in[0, rows, :]
            ups.append((jnp.dot(x_e, wg_bf[...], preferred_element_type=F32),
                        jnp.dot(x_e, wu_bf[...], preferred_element_type=F32)))
        for p, (rows, (a, u)) in enumerate(zip(groups, ups)):
            h = (a * _sigmoid(a) * u).astype(BF16)
            y = jnp.dot(h, wd_bf[...], preferred_element_type=F32) * gate_in[0, rows, 0:1]
            _store_rowmajor(out_ref, y, base=p * rows_per * chunks)

    if has_extra:
        @pl.when(i < n_main)
        def _():
            ffn(x_ref, g_ref, y_ref.at[0])

        @pl.when(i == n_main)
        def _():
            ffn(xx_ref, gg_ref, yy_ref.at[0])
    else:
        ffn(x_ref, g_ref, y_ref.at[0])


def _experts(x_e, g_e, x_extra, g_extra, wg, wu, wd, layer, tm):
    _, rows, d = x_e.shape
    f = wg.shape[3]
    chunks = d // LANES
    n_main = rows // tm
    has_extra = x_extra is not None
    main_map = lambda e, i: (e, jnp.minimum(i, n_main - 1), 0)

    def w_specs(shape):
        assert shape[0] % W_PARTS == 0 and n_main >= W_PARTS
        blk = (1, 1, shape[0] // W_PARTS, shape[1])
        return [pl.BlockSpec(blk, functools.partial(
            lambda e, i, j: (layer, jnp.minimum(e + (i > j).astype(jnp.int32), N_EXPERTS - 1), j, 0), j=j))
            for j in range(W_PARTS)]

    in_specs = [pl.BlockSpec((1, tm, d), main_map), pl.BlockSpec((1, tm, LANES), main_map)]
    out_specs = [pl.BlockSpec((1, tm * chunks, LANES), main_map)]
    out_shape = [jax.ShapeDtypeStruct((N_EXPERTS, rows * chunks, LANES), F32)]
    args = [x_e, g_e]
    if has_extra:
        rows2 = x_extra.shape[1]
        in_specs += [pl.BlockSpec((1, rows2, d), lambda e, i: (e, 0, 0)),
                     pl.BlockSpec((1, rows2, LANES), lambda e, i: (e, 0, 0))]
        out_specs.append(pl.BlockSpec((1, rows2 * chunks, LANES), lambda e, i: (e, 0, 0)))
        out_shape.append(jax.ShapeDtypeStruct((N_EXPERTS, rows2 * chunks, LANES), F32))
        args += [x_extra, g_extra]
    outs = pl.pallas_call(
        functools.partial(_expert_kernel, n_main=n_main, has_extra=has_extra),
        out_shape=out_shape,
        grid=(N_EXPERTS, n_main + int(has_extra)),
        in_specs=in_specs + w_specs((d, f)) + w_specs((d, f)) + w_specs((f, d)),
        out_specs=out_specs,
        scratch_shapes=[pltpu.VMEM((d, f), BF16), pltpu.VMEM((d, f), BF16), pltpu.VMEM((f, d), BF16)],
        compiler_params=_cparams(("arbitrary", "arbitrary")),
        name="experts",
    )(*args, *([wg] * W_PARTS), *([wu] * W_PARTS), *([wd] * W_PARTS))
    return outs if has_extra else (outs[0], None)


COMBINE_EXPERTS = 4
ADD_UNROLL = 16
NORM_CHUNK = 256


def _combine_kernel(idx_ref, y_ref, x_ref, gt_ref, g_ref, b_ref, o_ref, acc, *, n_slots):
    b = pl.program_id(0)
    j = pl.program_id(1)
    s, d = x_ref.shape[1], x_ref.shape[2]
    chunks = d // LANES

    @pl.when(j == 0)
    def _():
        acc[...] = jnp.zeros(acc.shape, F32)

    for eg in range(COMBINE_EXPERTS):
        base = (b * N_EXPERTS + j * COMBINE_EXPERTS + eg) * n_slots

        def add(i, carry):
            s0 = i * ADD_UNROLL
            first = base + s0
            src = pl.multiple_of(s0 * chunks, ADD_UNROLL * chunks)
            y_rows = y_ref[eg, pl.ds(src, ADD_UNROLL * chunks), :]
            new = []
            for u in range(ADD_UNROLL):
                dst = pl.multiple_of(idx_ref[first + u], chunks)
                new.append((dst, acc[pl.ds(dst, chunks), :] + y_rows[u * chunks:(u + 1) * chunks, :]))
            for dst, val in new:
                acc[pl.ds(dst, chunks), :] = val
            return carry

        lax.fori_loop(0, n_slots // ADD_UNROLL, add, 0)

    @pl.when(j == pl.num_programs(1) - 1)
    def _():
        for ci in range(s // NORM_CHUNK):
            rows = slice(ci * NORM_CHUNK, (ci + 1) * NORM_CHUNK)
            ym = _load_rowmajor(acc, ci * NORM_CHUNK * chunks, NORM_CHUNK, chunks)
            z = DEEPNORM_ALPHA * x_ref[0, rows, :] + (1.0 + gt_ref[0]) * ym
            o_ref[0, rows, :] = _ln(z) * g_ref[...] + b_ref[...]


def _combine(idx, y_rm, x_mid, gt, g, b, n_slots):
    nb, s, d = x_mid.shape
    chunks = d // LANES
    assert chunks == SUBLANES and N_EXPERTS % COMBINE_EXPERTS == 0 and s % NORM_CHUNK == 0
    vec = pl.BlockSpec((1, d), lambda bi, j, i_r: (0, 0))
    grid_spec = pltpu.PrefetchScalarGridSpec(
        num_scalar_prefetch=1,
        grid=(nb, N_EXPERTS // COMBINE_EXPERTS),
        in_specs=[
            pl.BlockSpec((COMBINE_EXPERTS, n_slots * chunks, LANES), lambda bi, j, i_r: (j, bi, 0)),
            pl.BlockSpec((1, s, d), lambda bi, j, i_r: (bi, 0, 0)),
            _mod_spec(gt, d), vec, vec,
        ],
        out_specs=pl.BlockSpec((1, s, d), lambda bi, j, i_r: (bi, 0, 0)),
        scratch_shapes=[pltpu.VMEM((s * chunks, LANES), F32)],
    )
    return pl.pallas_call(
        functools.partial(_combine_kernel, n_slots=n_slots),
        out_shape=jax.ShapeDtypeStruct((nb, s, d), F32),
        grid_spec=grid_spec,
        compiler_params=_cparams(("arbitrary", "arbitrary")),
        name="combine_postnorm",
    )(idx, y_rm, x_mid, gt[0], g, b)


def kernel(x, c, ctx, c_ctx, w_mod, b_mod, w_in, b_in, w_short, w_conf_dw, b_conf_dw, g_conf_ln, b_conf_ln,
           na_rpb, w_out, b_out, g_post1, b_post1, w_router, w_gate, w_up, w_down, g_post2, b_post2):
    bsz, seq, d = x.shape
    nctx = ctx.shape[1]
    cap = EC_CAPACITY_FACTOR * seq // N_EXPERTS
    cap_ctx = EC_CAPACITY_FACTOR * nctx // N_EXPERTS
    q_scale = NA_HEAD_DIM ** -0.5 * LOG2E

    cond = jnp.concatenate([c, c_ctx[None, :], jnp.zeros((MOD_ROWS - bsz - 1, d), F32)], axis=0)
    mods = _modulation(cond, w_mod, b_mod)

    lat_splits = ((3 * D_CONV, 1.0), (2 * D_CONF, 1.0), (D_NA, q_scale), (D_NA, 1.0), (D_NA, 1.0))
    lat_dtypes = (F32, F32, BF16, BF16, BF16)
    kv_splits = ((D_NA, 1.0), (D_NA, 1.0))
    b_in3 = b_in[:, None, :]

    xc = ctx
    for l in range(DEPTH):
        last = l == DEPTH - 1
        m_lat = [(mods, (l * N_MOD + k) * MOD_ROWS, True) for k in range(N_MOD)]
        m_ctx = [(mods, (l * N_MOD + k) * MOD_ROWS + bsz, False) for k in range(N_MOD)]
        b_out_l = b_out[l][None, :]
        wr = jnp.pad(w_router[l], ((0, 0), (0, LANES - N_EXPERTS)))
        wr_hi = wr.astype(BF16)
        wr_hl = jnp.concatenate([wr_hi, (wr - wr_hi.astype(F32)).astype(BF16)], axis=1)
        g1, b1 = g_post1[l][None, :], b_post1[l][None, :]
        g2, b2 = g_post2[l][None, :], b_post2[l][None, :]
        conv_w = (w_short[l], w_conf_dw[l], b_conf_dw[l], g_conf_ln[l], b_conf_ln[l])

        flat = lambda t: t.reshape(1, bsz * nctx, t.shape[-1])
        unflat = lambda t: t.reshape(bsz, nctx, t.shape[-1])
        if last:
            k_c, v_c = map(unflat, _inproj(flat(xc), m_ctx[0], m_ctx[1], w_in, b_in3, l, OFF_K,
                                           kv_splits, (BF16, BF16), tm=512))
        else:
            uac, ubc, q_c, k_c, v_c = map(unflat, _inproj(flat(xc), m_ctx[0], m_ctx[1], w_in, b_in3, l, 0,
                                                          lat_splits, lat_dtypes, tm=512))

        ua, ub, q, k, v = _inproj(x, m_lat[0], m_lat[1], w_in, b_in3, l, 0, lat_splits, lat_dtypes, tm=512)
        yab = _conv_mixers(ua, ub, *conv_w)
        yc = _neighbourhood_attention(q, k, v, k_c, v_c, na_rpb[l])
        x_mid, hm, logits = _outproj(yab, yc, x, w_out, l, b_out_l, m_lat[2], g1, b1, m_lat[3], m_lat[4],
                                     wr_hi, wr_hl, tm=512)

        idx, gates = _slot_lists(*_route(logits, cap, 0), cap, merge=False)
        idx = idx.reshape(-1)
        x_e = _dispatch(idx, hm, bsz, seq, d, cap)

        x_ec = gates_c = None
        if not last:
            yabc = _conv_mixers(uac, ubc, *conv_w)
            ycc = _context_attention(q_c, k_c, v_c)
            xc_mid, hmc, logits_c = _outproj(flat(yabc), flat(ycc), flat(xc), w_out, l, b_out_l, m_ctx[2], g1, b1,
                                             m_ctx[3], m_ctx[4], wr_hi, wr_hl, tm=512)
            n_c = bsz * cap_ctx
            idx_c, gates_c = _slot_lists(*_route(unflat(logits_c), cap_ctx, cap_ctx), n_c, merge=True)
            idx_c = idx_c.reshape(-1)
            x_ec = _dispatch(idx_c, hmc, 1, bsz * nctx, d, n_c)

        y_e, y_ec = _experts(x_e, gates, x_ec, gates_c, w_gate, w_up, w_down, l, tm=512)
        x = _combine(idx, y_e, x_mid, m_lat[5], g2, b2, cap)
        if not last:
            xc = unflat(_combine(idx_c, y_ec, xc_mid, m_ctx[5], g2, b2, n_c))
    return x
```

```python
import functools
import math

import numpy as np
import jax
import jax.numpy as jnp
from jax import lax
from jax.experimental import pallas as pl
from jax.experimental.pallas import tpu as pltpu

F32 = jnp.float32
BF16 = jnp.bfloat16

D_MODEL = 1024
DEPTH = 2
GRID_W = 64
D_CONV = D_MODEL // 4
D_CONF = D_MODEL // 4
NA_HEAD_DIM = 64
D_NA = D_MODEL - D_CONV - D_CONF
N_NA_HEADS = D_NA // NA_HEAD_DIM
SHORT_CONV_W = 3
CONF_CONV_W = 31
NA_WIN_ROWS_MAX = 8
NA_WIN_COLS = 16
N_EXPERTS = 16
EC_CAPACITY_FACTOR = 2
D_EXPERT = 1024
LN_EPS = 1e-5
DEEPNORM_ALPHA = (2.0 * DEPTH) ** 0.25
NEG_INF = -1e30
LOG2E = math.log2(math.e)

OFF_A = 0
OFF_B = OFF_A + 3 * D_CONV
OFF_Q = OFF_B + 2 * D_CONF
OFF_K = OFF_Q + D_NA
OFF_V = OFF_K + D_NA
D_IN = OFF_V + D_NA

LANES = 128
SUBLANES = 8
MOD_ROWS = 16
VMEM_LIMIT = 56 * 1024 * 1024
ATTN_ROWS = 8
N_PAIRS = D_NA // LANES
ROW_GROUP = 256
OUT_ROW_GROUP = 128
HI = lax.Precision.HIGHEST


def _cparams(sem):
    return pltpu.CompilerParams(dimension_semantics=sem, vmem_limit_bytes=VMEM_LIMIT)


def _ln(x):
    mu = jnp.mean(x, axis=-1, keepdims=True)
    xc = x - mu
    var = jnp.mean(xc * xc, axis=-1, keepdims=True)
    return xc * lax.rsqrt(var + LN_EPS)


def _sigmoid(x):
    return 1.0 / (1.0 + jnp.exp(-x))


def _mod_kernel(cond_ref, w_ref, b_ref, o_ref):
    s = cond_ref[...]
    s = s * _sigmoid(s)
    w = w_ref[0]
    s_hi, w_hi = s.astype(BF16), w.astype(BF16)
    s_lo = (s - s_hi.astype(F32)).astype(BF16)
    w_lo = (w - w_hi.astype(F32)).astype(BF16)
    o_ref[0] = (jnp.dot(s_hi, w_hi, preferred_element_type=F32) + jnp.dot(s_lo, w_hi, preferred_element_type=F32)
                + jnp.dot(s_hi, w_lo, preferred_element_type=F32) + b_ref[0])


N_MOD = 6


def _modulation(cond, w_mod, b_mod):
    n_l, d, n = w_mod.shape
    out = pl.pallas_call(
        _mod_kernel,
        out_shape=jax.ShapeDtypeStruct((n_l * N_MOD, MOD_ROWS, d), F32),
        grid=(n_l, N_MOD),
        in_specs=[
            pl.BlockSpec((MOD_ROWS, d), lambda l, k: (0, 0)),
            pl.BlockSpec((1, d, d), lambda l, k: (l, 0, k)),
            pl.BlockSpec((1, 1, d), lambda l, k: (l, 0, k)),
        ],
        out_specs=pl.BlockSpec((1, MOD_ROWS, d), lambda l, k: (l * N_MOD + k, 0, 0)),
        compiler_params=_cparams(("arbitrary", "arbitrary")),
        name="modulation",
    )(cond, w_mod, b_mod.reshape(n_l, 1, n))
    return out.reshape(n_l * N_MOD * MOD_ROWS, 1, d)


def _mod_spec(mod, d):
    _, row0, per_sample = mod
    if per_sample:
        return pl.BlockSpec((1, 1, d), lambda bi, *_: (row0 + bi, 0, 0))
    return pl.BlockSpec((1, 1, d), lambda bi, *_: (row0, 0, 0))


def _first_step():
    return (pl.program_id(0) == 0) & (pl.program_id(1) == 0)


def _inproj_kernel(x_ref, sh_ref, sc_ref, w_ref, b_ref, *rest, splits, col0):
    *o_refs, w_bf = rest

    @pl.when(_first_step())
    def _():
        w_bf[...] = w_ref[0].astype(BF16)

    tm = x_ref.shape[1]
    n_part = max(1, tm // ROW_GROUP)
    rows_per = tm // n_part
    groups = [slice(part_i * rows_per, (part_i + 1) * rows_per) for part_i in range(n_part)]
    hs = [(_ln(x_ref[0, rows, :]) * (1.0 + sc_ref[0]) + sh_ref[0]).astype(BF16) for rows in groups]
    for rows, h in zip(groups, hs):
        u = jnp.dot(h, w_bf[:, col0:], preferred_element_type=F32) + b_ref[0, :, col0:]
        off = 0
        for o_ref, (width, scale) in zip(o_refs, splits):
            part = u[:, off:off + width]
            if scale != 1.0:
                part = part * scale
            o_ref[0, rows, :] = part.astype(o_ref.dtype)
            off += width


def _inproj(x, sh, sc, w, b, layer, col0, splits, dtypes, tm):
    nb, s, d = x.shape
    n = w.shape[2]
    return pl.pallas_call(
        functools.partial(_inproj_kernel, splits=splits, col0=col0),
        out_shape=[jax.ShapeDtypeStruct((nb, s, wd), dt) for (wd, _), dt in zip(splits, dtypes)],
        grid=(nb, s // tm),
        in_specs=[
            pl.BlockSpec((1, tm, d), lambda bi, i: (bi, i, 0)),
            _mod_spec(sh, d),
            _mod_spec(sc, d),
            pl.BlockSpec((1, d, n), lambda bi, i: (layer, 0, 0), pipeline_mode=pl.Buffered(1)),
            pl.BlockSpec((1, 1, n), lambda bi, i: (layer, 0, 0)),
        ],
        out_specs=[pl.BlockSpec((1, tm, wd), lambda bi, i: (bi, i, 0)) for wd, _ in splits],
        scratch_shapes=[pltpu.VMEM((d, n), BF16)],
        compiler_params=_cparams(("arbitrary", "arbitrary")),
        name="inproj",
    )(x, sh[0], sc[0], w, b)


CONV_CHUNK = 128
Z_PAD = 8
H_PAD = 16


def _conv_kernel(ua_ref, ub_ref, ws_ref, wd_ref, bd_ref, g_ref, b_ref, o_ref, z_scr, h_scr, *, seq):
    c = D_CONV
    z_scr[0:Z_PAD, :] = jnp.zeros((Z_PAD, c), F32)
    z_scr[Z_PAD + seq:2 * Z_PAD + seq, :] = jnp.zeros((Z_PAD, c), F32)
    h_scr[0, 0:H_PAD, :] = jnp.zeros((H_PAD, c), F32)
    h_scr[0, H_PAD + seq:2 * H_PAD + seq, :] = jnp.zeros((H_PAD, c), F32)
    z_scr[Z_PAD:Z_PAD + seq, :] = ua_ref[0, :, c:2 * c] * ua_ref[0, :, 2 * c:3 * c]
    h_scr[0, H_PAD:H_PAD + seq, :] = ub_ref[0, :, 0:c] * _sigmoid(ub_ref[0, :, c:2 * c])
    n_rows = seq + 2 * H_PAD
    h_all = h_scr[0]
    for r in range(1, SUBLANES):
        h_scr[r] = pltpu.roll(h_all, n_rows - r, axis=0)
    tc = min(CONV_CHUNK, seq)
    for ci in range(seq // tc):
        t0 = ci * tc
        acc = ws_ref[0:1, :] * z_scr[t0 + Z_PAD - 1:t0 + Z_PAD - 1 + tc, :]
        for j in range(1, SHORT_CONV_W):
            s0 = t0 + Z_PAD - 1 + j
            acc = acc + ws_ref[j:j + 1, :] * z_scr[s0:s0 + tc, :]
        ya = ua_ref[0, t0:t0 + tc, 0:c] * acc
        hb = bd_ref[...]
        for j in range(CONF_CONV_W):
            s0 = t0 + H_PAD - CONF_CONV_W // 2 + j
            a0 = s0 - s0 % SUBLANES
            hb = hb + wd_ref[j:j + 1, :] * h_scr[s0 % SUBLANES, a0:a0 + tc, :]
        hn = _ln(hb) * g_ref[...] + b_ref[...]
        yb = hn * _sigmoid(hn)
        o_ref[0, t0:t0 + tc, 0:c] = ya.astype(o_ref.dtype)
        o_ref[0, t0:t0 + tc, c:2 * c] = yb.astype(o_ref.dtype)


def _conv_mixers(ua, ub, w_short, w_dw, b_dw, g_ln, b_ln):
    nb, s, _ = ua.shape
    c = D_CONV
    full = lambda shape: pl.BlockSpec(shape, lambda bi: (0,) * len(shape))
    return pl.pallas_call(
        functools.partial(_conv_kernel, seq=s),
        out_shape=jax.ShapeDtypeStruct((nb, s, 2 * c), BF16),
        grid=(nb,),
        in_specs=[
            pl.BlockSpec((1, s, 3 * c), lambda bi: (bi, 0, 0)),
            pl.BlockSpec((1, s, 2 * c), lambda bi: (bi, 0, 0)),
            full((SHORT_CONV_W, c)), full((CONF_CONV_W, c)), full((1, c)), full((1, c)), full((1, c)),
        ],
        out_specs=pl.BlockSpec((1, s, 2 * c), lambda bi: (bi, 0, 0)),
        scratch_shapes=[pltpu.VMEM((s + 2 * Z_PAD, c), F32), pltpu.VMEM((SUBLANES, s + 2 * H_PAD, c), F32)],
        compiler_params=_cparams(("arbitrary",)),
        name="conv_mixers",
    )(ua, ub, w_short, w_dw, b_dw.reshape(1, c), g_ln.reshape(1, c), b_ln.reshape(1, c))


SUB_ROWS = 2
WIN_ROWS = SUB_ROWS + NA_WIN_ROWS_MAX - 1
BAND_KEYS = WIN_ROWS * GRID_W
SUB_Q = SUB_ROWS * GRID_W
N_DROW = 2 * NA_WIN_ROWS_MAX - 1
N_DCOL = 2 * NA_WIN_COLS - 1


def _bias_kernel(rpb_ref, nxt_ref, o_ref):
    n_cols = o_ref.shape[1]
    col = lax.broadcasted_iota(jnp.int32, (LANES, n_cols), 1)
    qc = col >> (LANES.bit_length() - 1)
    kc = col & (GRID_W - 1)
    d_col = jnp.clip(kc - qc + (NA_WIN_COLS - 1), 0, N_DCOL - 1)
    onehot = (lax.broadcasted_iota(jnp.int32, (LANES, n_cols), 0) == d_col).astype(F32)
    own = jnp.dot(rpb_ref[...], onehot, preferred_element_type=F32, precision=HI) * LOG2E
    nxt = jnp.dot(nxt_ref[...], onehot, preferred_element_type=F32, precision=HI) * LOG2E
    c0 = jnp.clip(qc - NA_WIN_COLS // 2, 0, GRID_W - NA_WIN_COLS)
    inside = (kc >= c0) & (kc < c0 + NA_WIN_COLS)
    second = (col & GRID_W) != 0
    o_ref[0:LANES, :] = jnp.where(inside, jnp.where(second, nxt, own), NEG_INF)
    o_ref[LANES:2 * LANES, :] = jnp.where(inside & ~second, own, NEG_INF)
    o_ref[2 * LANES:3 * LANES, :] = jnp.where(inside & second, own, NEG_INF)
    o_ref[3 * LANES:, :] = jnp.full((o_ref.shape[0] - 3 * LANES, n_cols), NEG_INF, F32)


N_SLABS = -(-WIN_ROWS // 2)
TILES_PER_KIND = LANES
MASKED_TILE = 3 * TILES_PER_KIND


def _na_plan(rows):
    wr = min(NA_WIN_ROWS_MAX, rows)
    assert wr == NA_WIN_ROWS_MAX and rows % SUB_ROWS == 0 and rows >= WIN_ROWS
    row_start = np.clip(np.arange(rows) - wr // 2, 0, rows - wr)
    w0s, tiles = [], []
    for r0 in range(0, rows, SUB_ROWS):
        w0 = int(np.clip(r0 - wr // 2, 0, rows - WIN_ROWS))
        for iq in range(SUB_ROWS):
            r = r0 + iq
            assert row_start[r] >= w0 and row_start[r] + wr <= w0 + WIN_ROWS
            ok = lambda w: w < WIN_ROWS and row_start[r] <= w0 + w < row_start[r] + wr
            d_row = lambda w: w0 + w - r + NA_WIN_ROWS_MAX - 1
            for j in range(N_SLABS):
                lo, hi = ok(2 * j), ok(2 * j + 1)
                if lo and hi:
                    tiles.append(d_row(2 * j))
                elif lo:
                    tiles.append(TILES_PER_KIND + d_row(2 * j))
                elif hi:
                    tiles.append(2 * TILES_PER_KIND + d_row(2 * j + 1))
                else:
                    tiles.append(-1)
        w0s.append(w0)
    return np.array(w0s, np.int32), np.array(tiles, np.int32)


def _na_bias(rpb):
    n_hd = N_NA_HEADS * N_DROW
    assert GRID_W == 64 and 2 * GRID_W == LANES and n_hd <= TILES_PER_KIND and N_DCOL <= LANES
    rpb = rpb.astype(F32)
    pad = lambda t: jnp.pad(t.reshape(n_hd, N_DCOL), ((0, LANES - n_hd), (0, LANES - N_DCOL)))
    nxt = jnp.concatenate([rpb[:, 1:], jnp.zeros_like(rpb[:, :1])], axis=1)
    n_tiles = 3 * TILES_PER_KIND + SUBLANES
    table = pl.pallas_call(
        _bias_kernel,
        out_shape=jax.ShapeDtypeStruct((n_tiles, GRID_W * LANES), F32),
        compiler_params=pltpu.CompilerParams(vmem_limit_bytes=VMEM_LIMIT),
        name="na_bias",
    )(pad(rpb), pad(nxt))
    return table.reshape(n_tiles, GRID_W, LANES)


def _lane_reduce(xs, combine, reduce, neutral):
    chunks = []
    for x in xs:
        rows, n = x.shape
        n_full = n // LANES
        chunks += [x[:, j * LANES:(j + 1) * LANES] for j in range(n_full)]
        if n % LANES:
            fill = jnp.full((rows, LANES - n % LANES), neutral, x.dtype)
            chunks.append(jnp.concatenate([x[:, n_full * LANES:], fill], axis=1))
    return reduce(functools.reduce(combine, chunks), axis=-1, keepdims=True)


def _attn_kernel(w0_ref, tile_ref, q_ref, k_ref, v_ref, kc_ref, vc_ref, *rest, banded, n_sub, sub_q):
    if banded:
        bias_ref, o_ref = rest
    else:
        (o_ref,) = rest
    lane = lax.broadcasted_iota(jnp.int32, (sub_q, LANES), 1)
    first = lane < NA_HEAD_DIM
    nt = (((1,), (1,)), ((), ()))
    stages = [(si, p) for si in range(n_sub) for p in range(N_PAIRS)]

    def window(si):
        blk = pl.program_id(1) * n_sub + si
        return blk, pl.multiple_of(w0_ref[blk] * GRID_W, GRID_W)

    def bias(blk, head):
        row_blocks = []
        for iq in range(SUB_ROWS):
            slabs = []
            for j in range(N_SLABS):
                t = tile_ref[(blk * SUB_ROWS + iq) * N_SLABS + j]
                tile = bias_ref[jnp.where(t < 0, MASKED_TILE, t + head * N_DROW)]
                width = min(LANES, BAND_KEYS - j * LANES)
                slabs.append(tile[:, :width])
            row_blocks.append(jnp.concatenate(slabs, axis=1))
        return jnp.concatenate(row_blocks, axis=0)

    def scores(si, p):
        cols = slice(p * LANES, (p + 1) * LANES)
        q_p = q_ref[0, si * sub_q:(si + 1) * sub_q, cols]
        zero = jnp.zeros_like(q_p)
        qq = jnp.concatenate([jnp.where(first, q_p, zero), jnp.where(first, zero, q_p)], axis=0)
        parts = [lax.dot_general(qq, kc_ref[0, :, cols], nt, preferred_element_type=F32)]
        if banded:
            blk, start = window(si)
            both = jnp.concatenate([bias(blk, 2 * p), bias(blk, 2 * p + 1)], axis=0)
            parts.append(lax.dot_general(qq, k_ref[0, pl.ds(start, BAND_KEYS), cols], nt,
                                         preferred_element_type=F32) + both)
        return parts

    def weights(parts):
        m = _lane_reduce(parts, jnp.maximum, jnp.max, NEG_INF)
        es = [jnp.exp2(s - m) for s in parts]
        den = _lane_reduce(es, jnp.add, jnp.sum, 0.0)
        return [e.astype(BF16) for e in es], den

    def values(si, p, es, den):
        cols = slice(p * LANES, (p + 1) * LANES)
        o = jnp.dot(es[0], vc_ref[0, :, cols], preferred_element_type=F32)
        if banded:
            _, start = window(si)
            o = o + jnp.dot(es[1], v_ref[0, pl.ds(start, BAND_KEYS), cols], preferred_element_type=F32)
        o = o * (1.0 / den)
        out = jnp.where(first, o[:sub_q], o[sub_q:])
        o_ref[0, si * sub_q:(si + 1) * sub_q, cols] = out.astype(o_ref.dtype)

    nxt = scores(*stages[0])
    pending = None
    for i, (si, p) in enumerate(stages):
        cur = nxt
        if i + 1 < len(stages):
            nxt = scores(*stages[i + 1])
        if pending is not None:
            values(*pending)
        es, den = weights(cur)
        pending = (si, p, es, den)
    values(*pending)


def _neighbourhood_attention(q, k, v, kc, vc, rpb):
    nb, s, dn = q.shape
    rows = s // GRID_W
    nctx = kc.shape[1]
    w0s, tiles = _na_plan(rows)
    bias = _na_bias(rpb)
    n_sub = ATTN_ROWS // SUB_ROWS
    m_rows = ATTN_ROWS * GRID_W
    grid_spec = pltpu.PrefetchScalarGridSpec(
        num_scalar_prefetch=2,
        grid=(nb, rows // ATTN_ROWS),
        in_specs=[
            pl.BlockSpec((1, m_rows, dn), lambda bi, i, w0, pat: (bi, i, 0)),
            pl.BlockSpec((1, s, dn), lambda bi, i, w0, pat: (bi, 0, 0)),
            pl.BlockSpec((1, s, dn), lambda bi, i, w0, pat: (bi, 0, 0)),
            pl.BlockSpec((1, nctx, dn), lambda bi, i, w0, pat: (bi, 0, 0)),
            pl.BlockSpec((1, nctx, dn), lambda bi, i, w0, pat: (bi, 0, 0)),
            pl.BlockSpec(bias.shape, lambda bi, i, w0, pat: (0, 0, 0), pipeline_mode=pl.Buffered(1)),
        ],
        out_specs=pl.BlockSpec((1, m_rows, dn), lambda bi, i, w0, pat: (bi, i, 0)),
    )
    return pl.pallas_call(
        functools.partial(_attn_kernel, banded=True, n_sub=n_sub, sub_q=SUB_Q),
        out_shape=jax.ShapeDtypeStruct((nb, s, dn), BF16),
        grid_spec=grid_spec,
        compiler_params=_cparams(("arbitrary", "arbitrary")),
        name="neighbourhood_attention",
    )(jnp.asarray(w0s), jnp.asarray(tiles), q, k, v, kc, vc, bias)


def _context_attention(q, kc, vc):
    nb, s, dn = q.shape
    spec = pl.BlockSpec((1, s, dn), lambda bi, i, w0, pat: (bi, 0, 0))
    grid_spec = pltpu.PrefetchScalarGridSpec(
        num_scalar_prefetch=2, grid=(nb, 1), in_specs=[spec] * 5, out_specs=spec)
    dummy = jnp.zeros((1,), jnp.int32)
    return pl.pallas_call(
        functools.partial(_attn_kernel, banded=False, n_sub=1, sub_q=s),
        out_shape=jax.ShapeDtypeStruct((nb, s, dn), BF16),
        grid_spec=grid_spec,
        compiler_params=_cparams(("arbitrary", "arbitrary")),
        name="context_attention",
    )(dummy, dummy, q, kc, vc, kc, vc)


def _store_rowmajor(ref, val, base=0):
    n, width = val.shape
    chunks = width // LANES
    for c in range(chunks):
        ref[pl.ds(base + c, n, stride=chunks), :] = val[:, c * LANES:(c + 1) * LANES]


def _load_rowmajor(ref, base, n, chunks):
    return jnp.concatenate([ref[pl.ds(base + c, n, stride=chunks), :] for c in range(chunks)], axis=1)


def _outproj_kernel(yab_ref, yc_ref, x_ref, w_ref, bo_ref, gt_ref, g_ref, b_ref, sh_ref, sc_ref,
                    wrh_ref, wrhl_ref, xmid_ref, hm_ref, lg_ref, w_bf):
    @pl.when(_first_step())
    def _():
        w_bf[...] = w_ref[0].astype(BF16)

    half = yab_ref.shape[2]
    tm = x_ref.shape[1]
    chunks = x_ref.shape[2] // LANES
    n_part = max(1, tm // OUT_ROW_GROUP)
    rows_per = tm // n_part
    groups = [slice(part * rows_per, (part + 1) * rows_per) for part in range(n_part)]
    ys = [jnp.dot(yab_ref[0, rows, :], w_bf[0:half, :], preferred_element_type=F32)
          + jnp.dot(yc_ref[0, rows, :], w_bf[half:, :], preferred_element_type=F32) + bo_ref[...]
          for rows in groups]
    for part, (rows, y) in enumerate(zip(groups, ys)):
        xm = _ln(DEEPNORM_ALPHA * x_ref[0, rows, :] + (1.0 + gt_ref[0]) * y) * g_ref[...] + b_ref[...]
        xmid_ref[0, rows, :] = xm
        hm = _ln(xm) * (1.0 + sc_ref[0]) + sh_ref[0]
        _store_rowmajor(hm_ref, hm, base=part * rows_per * chunks)
        hm_hi = hm.astype(BF16)
        hm_lo = (hm - hm_hi.astype(F32)).astype(BF16)
        both = jnp.dot(hm_hi, wrhl_ref[...], preferred_element_type=F32)
        lg_ref[0, rows, :] = (both[:, :LANES] + both[:, LANES:]
                              + jnp.dot(hm_lo, wrh_ref[...], preferred_element_type=F32))


def _outproj(yab, yc, x, w, layer, bo, gt, g, b, sh, sc, wr_hi, wr_hl, tm):
    nb, s, d = x.shape
    half = yab.shape[2]
    vec = pl.BlockSpec((1, d), lambda bi, i: (0, 0))
    tok = lambda width: pl.BlockSpec((1, tm, width), lambda bi, i: (bi, i, 0))
    n_i = s // tm
    return pl.pallas_call(
        _outproj_kernel,
        out_shape=[jax.ShapeDtypeStruct((nb, s, d), F32),
                   jax.ShapeDtypeStruct((nb * s * (d // LANES), LANES), F32),
                   jax.ShapeDtypeStruct((nb, s, LANES), F32)],
        grid=(nb, n_i),
        in_specs=[tok(half), tok(half), tok(d),
                  pl.BlockSpec((1, d, d), lambda bi, i: (layer, 0, 0), pipeline_mode=pl.Buffered(1)),
                  vec, _mod_spec(gt, d), vec, vec,
                  _mod_spec(sh, d), _mod_spec(sc, d), pl.BlockSpec((d, LANES), lambda bi, i: (0, 0)),
                  pl.BlockSpec((d, 2 * LANES), lambda bi, i: (0, 0))],
        out_specs=[tok(d), pl.BlockSpec((tm * (d // LANES), LANES), lambda bi, i: (bi * n_i + i, 0)),
                   tok(LANES)],
        scratch_shapes=[pltpu.VMEM((d, d), BF16)],
        compiler_params=_cparams(("arbitrary", "arbitrary")),
        name="outproj_postnorm",
    )(yab, yc, x, w, bo, gt[0], g, b, sh[0], sc[0], wr_hi, wr_hl)


CUM_CHUNK = 256
F32_EXP_BIAS = 127
F32_MANT_BITS = 23


def _prefix_count(mask_f32, tri):
    rows, n = mask_f32.shape
    tc = min(CUM_CHUNK, n)
    base = jnp.zeros((rows, 1), F32)
    parts = []
    for ci in range(n // tc):
        blk = mask_f32[:, ci * tc:(ci + 1) * tc]
        parts.append(jnp.dot(blk.astype(BF16), tri[:tc, :tc], preferred_element_type=F32) + base)
        base = base + jnp.sum(blk, axis=-1, keepdims=True)
    return jnp.concatenate(parts, axis=-1)


def _pow2(k):
    return pltpu.bitcast((k + F32_EXP_BIAS) << F32_MANT_BITS, F32)


def _route_kernel(lg_ref, slot_c_ref, gate_t_ref, *, cap, slot_stride):
    nb = lg_ref.shape[0]
    assert nb * N_EXPERTS == LANES
    rows = []
    for b in range(nb):
        lg = lg_ref[b]
        lane = lax.broadcasted_iota(jnp.int32, lg.shape, 1)
        lgm = jnp.where(lane < N_EXPERTS, lg, NEG_INF)
        ex = jnp.exp(lgm - jnp.max(lgm, axis=-1, keepdims=True))
        aff = ex / jnp.sum(ex, axis=-1, keepdims=True)
        rows.append(aff.T[0:N_EXPERTS, :])
    a = jnp.concatenate(rows, axis=0)
    capf = float(cap)

    def enough(t):
        return jnp.sum((a >= t).astype(F32), axis=-1, keepdims=True) >= capf

    def exp_step(_, carry):
        lo, hi = carry
        mid = lo + ((hi - lo + 1) >> 1)
        ok = enough(_pow2(mid))
        return jnp.where(ok, mid, lo), jnp.where(ok, hi, mid - 1)

    k_lo = jnp.full((LANES, 1), -F32_EXP_BIAS, jnp.int32)
    k_hi = jnp.zeros((LANES, 1), jnp.int32)
    k_lo, _ = lax.fori_loop(0, 7, exp_step, (k_lo, k_hi))
    base = _pow2(k_lo)

    def mant_step(_, carry):
        t, step = carry
        step = step * 0.5
        cand = t + step
        return jnp.where(enough(cand), cand, t), step

    thr, _ = lax.fori_loop(0, F32_MANT_BITS, mant_step, (base, base))

    r_i = lax.broadcasted_iota(jnp.int32, (CUM_CHUNK, CUM_CHUNK), 0)
    c_i = lax.broadcasted_iota(jnp.int32, (CUM_CHUNK, CUM_CHUNK), 1)
    tri = (r_i < c_i).astype(BF16)
    gt = (a > thr).astype(F32)
    eq = (a == thr).astype(F32)
    need = capf - jnp.sum(gt, axis=-1, keepdims=True)
    sel = gt + eq * (_prefix_count(eq, tri) < need).astype(F32)
    pos = _prefix_count(sel, tri)
    sample = lax.broadcasted_iota(jnp.int32, (LANES, 1), 0) >> (N_EXPERTS.bit_length() - 1)
    slot = jnp.where(sel > 0.0, pos + (sample * slot_stride).astype(F32), -1.0)
    for b in range(nb):
        lo = b * N_EXPERTS
        gate_t_ref[b] = a[lo:lo + N_EXPERTS, :]
        rolled = slot if b == 0 else jnp.concatenate([slot[lo:, :], slot[:lo, :]], axis=0)
        slot_c_ref[b] = rolled.T


def _route(logits, cap, slot_stride):
    nb, s, _ = logits.shape
    whole = lambda shape: pl.BlockSpec(shape, lambda i: (0,) * len(shape))
    return pl.pallas_call(
        functools.partial(_route_kernel, cap=cap, slot_stride=slot_stride),
        out_shape=[jax.ShapeDtypeStruct((nb, s, LANES), F32), jax.ShapeDtypeStruct((nb, N_EXPERTS, s), F32)],
        grid=(1,),
        in_specs=[whole((nb, s, LANES))],
        out_specs=[whole((nb, s, LANES)), whole((nb, N_EXPERTS, s))],
        compiler_params=_cparams(("arbitrary",)),
        name="route",
    )(logits)


TOK_SPLIT = 64


def _slot_list_kernel(slot_ref, gate_ref, idx_ref, g_ref, *, n_slots, tok_stride, merge):
    s = slot_ref.shape[1]
    b = pl.program_id(0)
    assert n_slots <= 256
    slot_id = lax.broadcasted_iota(jnp.int32, (s, n_slots), 1).astype(F32).astype(BF16)
    one, zero = jnp.ones((s, n_slots), BF16), jnp.zeros((s, n_slots), BF16)
    tok = lax.broadcasted_iota(jnp.int32, (1, s), 1) + b * tok_stride
    tok_hi = (tok >> (TOK_SPLIT.bit_length() - 1)).astype(F32)
    tok_lo = (tok & (TOK_SPLIT - 1)).astype(F32)
    zeros = jnp.zeros((SUBLANES - 5, s), F32)
    idx_rows, g_rows = [], []
    for e in range(N_EXPERTS):
        taken = jnp.broadcast_to(slot_ref[0, :, e:e + 1].astype(BF16), (s, n_slots))
        hit = jnp.where(taken == slot_id, one, zero)
        g0 = gate_ref[0, e:e + 1, :]
        g_hi = g0.astype(BF16).astype(F32)
        g_mid = (g0 - g_hi).astype(BF16).astype(F32)
        g_lo = g0 - g_hi - g_mid
        lhs = jnp.concatenate([tok_hi, tok_lo, g_hi, g_mid, g_lo, zeros], axis=0).astype(BF16)
        out = jnp.dot(lhs, hit, preferred_element_type=F32)
        idx_rows.append(out[0:1] * float(TOK_SPLIT) + out[1:2])
        g_rows.append(out[2:3] + out[3:4] + out[4:5])
    idx = jnp.concatenate(idx_rows, axis=0).astype(jnp.int32) * SUBLANES
    g = jnp.concatenate(g_rows + [jnp.zeros((LANES - N_EXPERTS, n_slots), F32)], axis=0)
    g_t = g.T
    g_cols = [jnp.broadcast_to(g_t[:, e:e + 1], (n_slots, LANES)) for e in range(N_EXPERTS)]
    if merge:
        @pl.when(b == 0)
        def _():
            idx_ref[0] = idx
            for e in range(N_EXPERTS):
                g_ref[e] = g_cols[e]

        @pl.when(b > 0)
        def _():
            idx_ref[0] = idx_ref[0] + idx
            for e in range(N_EXPERTS):
                g_ref[e] = g_ref[e] + g_cols[e]
    else:
        idx_ref[0] = idx
        for e in range(N_EXPERTS):
            g_ref[e] = g_cols[e]


def _slot_lists(slot_c, gate_t, n_slots, merge):
    nb, s, _ = slot_c.shape
    nbo = 1 if merge else nb
    idx_map = (lambda bi: (0, 0, 0)) if merge else (lambda bi: (bi, 0, 0))
    g_map = (lambda bi: (0, 0, 0)) if merge else (lambda bi: (0, bi, 0))
    return pl.pallas_call(
        functools.partial(_slot_list_kernel, n_slots=n_slots, tok_stride=s if merge else 0, merge=merge),
        out_shape=[jax.ShapeDtypeStruct((nbo, N_EXPERTS, n_slots), jnp.int32),
                   jax.ShapeDtypeStruct((N_EXPERTS, nbo * n_slots, LANES), F32)],
        grid=(nb,),
        in_specs=[pl.BlockSpec((1, s, LANES), lambda bi: (bi, 0, 0)),
                  pl.BlockSpec((1, N_EXPERTS, s), lambda bi: (bi, 0, 0))],
        out_specs=[pl.BlockSpec((1, N_EXPERTS, n_slots), idx_map),
                   pl.BlockSpec((N_EXPERTS, n_slots, LANES), g_map)],
        compiler_params=_cparams(("arbitrary",)),
        name="slot_lists",
    )(slot_c, gate_t)


ROW_UNROLL = 32


def _dispatch_kernel(idx_ref, hm_ref, x_ref, rows_scr, *, n_slots):
    b = pl.program_id(0)
    chunks = x_ref.shape[2] // LANES

    def emit(e, s0):
        src = pl.multiple_of(s0 * chunks, ROW_UNROLL * chunks)
        rows = _load_rowmajor(rows_scr.at[e % 2], src, ROW_UNROLL, chunks)
        x_ref[e, pl.ds(pl.multiple_of(s0, ROW_UNROLL), ROW_UNROLL), :] = rows.astype(x_ref.dtype)

    for e in range(N_EXPERTS + 1):
        base = (b * N_EXPERTS + e) * n_slots

        def move(i, carry):
            s0 = i * ROW_UNROLL
            if e < N_EXPERTS:
                first = base + s0
                tiles = [hm_ref[pl.ds(pl.multiple_of(idx_ref[first + u], chunks), chunks), :]
                         for u in range(ROW_UNROLL)]
                dst = pl.multiple_of(s0 * chunks, ROW_UNROLL * chunks)
                rows_scr[e % 2, pl.ds(dst, ROW_UNROLL * chunks), :] = jnp.concatenate(tiles, axis=0)
            if e > 0:
                emit(e - 1, s0)
            return carry

        lax.fori_loop(0, n_slots // ROW_UNROLL, move, 0)


def _dispatch(idx, hm_rm, nb, s, d, n_slots):
    chunks = d // LANES
    assert chunks == SUBLANES and n_slots % ROW_UNROLL == 0
    grid_spec = pltpu.PrefetchScalarGridSpec(
        num_scalar_prefetch=1,
        grid=(nb,),
        in_specs=[pl.BlockSpec((s * chunks, LANES), lambda bi, idx_r: (bi, 0))],
        out_specs=pl.BlockSpec((N_EXPERTS, n_slots, d), lambda bi, idx_r: (0, bi, 0)),
        scratch_shapes=[pltpu.VMEM((2, n_slots * chunks, LANES), F32)],
    )
    return pl.pallas_call(
        functools.partial(_dispatch_kernel, n_slots=n_slots),
        out_shape=jax.ShapeDtypeStruct((N_EXPERTS, nb * n_slots, d), BF16),
        grid_spec=grid_spec,
        compiler_params=_cparams(("arbitrary",)),
        name="dispatch",
    )(idx, hm_rm)


W_PARTS = 4

def _expert_kernel(*refs, n_main, has_extra):
    n_in = 4 if has_extra else 2
    n_out = 2 if has_extra else 1
    acts, w_parts = refs[:n_in], refs[n_in:n_in + 3]
    outs = refs[n_in + 3:n_in + 3 + n_out]
    w_bfs = refs[n_in + 3 + n_out:]
    if has_extra:
        x_ref, g_ref, xx_ref, gg_ref = acts
        y_ref, yy_ref = outs
    else:
        x_ref, g_ref = acts
        (y_ref,) = outs
    s = pl.program_id(0)
    i = pl.program_id(1)
    fill = s % 2
    cur = 1 - fill
    wg_bf, wu_bf, wd_bf = (w_bf.at[cur] for w_bf in w_bfs)

    def cast_part():
        block = jnp.minimum(i, W_PARTS - 1)
        for part, w_bf in zip(w_parts, w_bfs):
            rows = part.shape[2]
            w_bf[fill, pl.ds(pl.multiple_of(block * rows, rows), rows), :] = part[0, 0].astype(BF16)

    def ffn(x_in, gate_in, out_ref):
        rows_all = x_in.shape[1]
        chunks = x_in.shape[2] // LANES
        n_part = max(1, rows_all // ROW_GROUP)
        rows_per = rows_all // n_part
        groups = [slice(p * rows_per, (p + 1) * rows_per) for p in range(n_part)]
        ups = []
        for rows in groups:
            x_e = x_in[0, rows, :]
            ups.append((jnp.dot(x_e, wg_bf[...], preferred_element_type=F32),
                        jnp.dot(x_e, wu_bf[...], preferred_element_type=F32)))
        for p, (rows, (a, u)) in enumerate(zip(groups, ups)):
            h = (a * _sigmoid(a) * u).astype(BF16)
            y = jnp.dot(h, wd_bf[...], preferred_element_type=F32) * gate_in[0, rows, 0:1]
            _store_rowmajor(out_ref, y, base=p * rows_per * chunks)

    @pl.when(s == 0)
    def _():
        cast_part()

    if has_extra:
        @pl.when((s > 0) & (i < n_main))
        def _():
            cast_part()
            ffn(x_ref, g_ref, y_ref.at[0])

        @pl.when((s > 0) & (i == n_main))
        def _():
            cast_part()
            ffn(xx_ref, gg_ref, yy_ref.at[0])
    else:
        @pl.when(s > 0)
        def _():
            cast_part()
            ffn(x_ref, g_ref, y_ref.at[0])


def _experts(x_e, g_e, x_extra, g_extra, wg, wu, wd, layer, tm):
    _, rows, d = x_e.shape
    f = wg.shape[3]
    chunks = d // LANES
    n_main = rows // tm
    has_extra = x_extra is not None
    assert n_main >= W_PARTS and d % W_PARTS == 0 and f % W_PARTS == 0
    expert = lambda s: jnp.maximum(s - 1, 0)
    main_map = lambda s, i: (expert(s), jnp.where(s == 0, 0, jnp.minimum(i, n_main - 1)), 0)
    extra_map = lambda s, i: (expert(s), 0, 0)

    def w_spec(shape):
        blk = (1, 1, shape[0] // W_PARTS, shape[1])
        return pl.BlockSpec(blk, lambda s, i: (layer, jnp.minimum(s, N_EXPERTS - 1),
                                               jnp.minimum(i, W_PARTS - 1), 0))

    in_specs = [pl.BlockSpec((1, tm, d), main_map), pl.BlockSpec((1, tm, LANES), main_map)]
    out_specs = [pl.BlockSpec((1, tm * chunks, LANES), main_map)]
    out_shape = [jax.ShapeDtypeStruct((N_EXPERTS, rows * chunks, LANES), F32)]
    args = [x_e, g_e]
    if has_extra:
        rows2 = x_extra.shape[1]
        in_specs += [pl.BlockSpec((1, rows2, d), extra_map), pl.BlockSpec((1, rows2, LANES), extra_map)]
        out_specs.append(pl.BlockSpec((1, rows2 * chunks, LANES), extra_map))
        out_shape.append(jax.ShapeDtypeStruct((N_EXPERTS, rows2 * chunks, LANES), F32))
        args += [x_extra, g_extra]
    outs = pl.pallas_call(
        functools.partial(_expert_kernel, n_main=n_main, has_extra=has_extra),
        out_shape=out_shape,
        grid=(N_EXPERTS + 1, n_main + int(has_extra)),
        in_specs=in_specs + [w_spec((d, f)), w_spec((d, f)), w_spec((f, d))],
        out_specs=out_specs,
        scratch_shapes=[pltpu.VMEM((2, d, f), BF16), pltpu.VMEM((2, d, f), BF16), pltpu.VMEM((2, f, d), BF16)],
        compiler_params=_cparams(("arbitrary", "arbitrary")),
        name="experts",
    )(*args, wg, wu, wd)
    return outs if has_extra else (outs[0], None)


COMBINE_EXPERTS = 4
ADD_UNROLL = 16
NORM_CHUNK = 256


def _combine_kernel(idx_ref, y_ref, x_ref, gt_ref, g_ref, b_ref, o_ref, acc, *, n_slots):
    b = pl.program_id(0)
    j = pl.program_id(1)
    s, d = x_ref.shape[1], x_ref.shape[2]
    chunks = d // LANES

    @pl.when(j == 0)
    def _():
        acc[...] = jnp.zeros(acc.shape, F32)

    for eg in range(COMBINE_EXPERTS):
        base = (b * N_EXPERTS + j * COMBINE_EXPERTS + eg) * n_slots

        def add(i, carry):
            s0 = i * ADD_UNROLL
            first = base + s0
            src = pl.multiple_of(s0 * chunks, ADD_UNROLL * chunks)
            y_rows = y_ref[eg, pl.ds(src, ADD_UNROLL * chunks), :]
            new = []
            for u in range(ADD_UNROLL):
                dst = pl.multiple_of(idx_ref[first + u], chunks)
                new.append((dst, acc[pl.ds(dst, chunks), :] + y_rows[u * chunks:(u + 1) * chunks, :]))
            for dst, val in new:
                acc[pl.ds(dst, chunks), :] = val
            return carry

        lax.fori_loop(0, n_slots // ADD_UNROLL, add, 0)

    @pl.when(j == pl.num_programs(1) - 1)
    def _():
        for ci in range(s // NORM_CHUNK):
            rows = slice(ci * NORM_CHUNK, (ci + 1) * NORM_CHUNK)
            ym = _load_rowmajor(acc, ci * NORM_CHUNK * chunks, NORM_CHUNK, chunks)
            z = DEEPNORM_ALPHA * x_ref[0, rows, :] + (1.0 + gt_ref[0]) * ym
            o_ref[0, rows, :] = _ln(z) * g_ref[...] + b_ref[...]


def _combine(idx, y_rm, x_mid, gt, g, b, n_slots):
    nb, s, d = x_mid.shape
    chunks = d // LANES
    assert chunks == SUBLANES and N_EXPERTS % COMBINE_EXPERTS == 0 and s % NORM_CHUNK == 0
    vec = pl.BlockSpec((1, d), lambda bi, j, i_r: (0, 0))
    grid_spec = pltpu.PrefetchScalarGridSpec(
        num_scalar_prefetch=1,
        grid=(nb, N_EXPERTS // COMBINE_EXPERTS),
        in_specs=[
            pl.BlockSpec((COMBINE_EXPERTS, n_slots * chunks, LANES), lambda bi, j, i_r: (j, bi, 0)),
            pl.BlockSpec((1, s, d), lambda bi, j, i_r: (bi, 0, 0)),
            _mod_spec(gt, d), vec, vec,
        ],
        out_specs=pl.BlockSpec((1, s, d), lambda bi, j, i_r: (bi, 0, 0)),
        scratch_shapes=[pltpu.VMEM((s * chunks, LANES), F32)],
    )
    return pl.pallas_call(
        functools.partial(_combine_kernel, n_slots=n_slots),
        out_shape=jax.ShapeDtypeStruct((nb, s, d), F32),
        grid_spec=grid_spec,
        compiler_params=_cparams(("arbitrary", "arbitrary")),
        name="combine_postnorm",
    )(idx, y_rm, x_mid, gt[0], g, b)


def kernel(x, c, ctx, c_ctx, w_mod, b_mod, w_in, b_in, w_short, w_conf_dw, b_conf_dw, g_conf_ln, b_conf_ln,
           na_rpb, w_out, b_out, g_post1, b_post1, w_router, w_gate, w_up, w_down, g_post2, b_post2):
    bsz, seq, d = x.shape
    nctx = ctx.shape[1]
    cap = EC_CAPACITY_FACTOR * seq // N_EXPERTS
    cap_ctx = EC_CAPACITY_FACTOR * nctx // N_EXPERTS
    q_scale = NA_HEAD_DIM ** -0.5 * LOG2E

    cond = jnp.concatenate([c, c_ctx[None, :], jnp.zeros((MOD_ROWS - bsz - 1, d), F32)], axis=0)
    mods = _modulation(cond, w_mod, b_mod)

    lat_splits = ((3 * D_CONV, 1.0), (2 * D_CONF, 1.0), (D_NA, q_scale), (D_NA, 1.0), (D_NA, 1.0))
    lat_dtypes = (F32, F32, BF16, BF16, BF16)
    kv_splits = ((D_NA, 1.0), (D_NA, 1.0))
    b_in3 = b_in[:, None, :]

    xc = ctx
    for l in range(DEPTH):
        last = l == DEPTH - 1
        m_lat = [(mods, (l * N_MOD + k) * MOD_ROWS, True) for k in range(N_MOD)]
        m_ctx = [(mods, (l * N_MOD + k) * MOD_ROWS + bsz, False) for k in range(N_MOD)]
        b_out_l = b_out[l][None, :]
        wr = jnp.pad(w_router[l], ((0, 0), (0, LANES - N_EXPERTS)))
        wr_hi = wr.astype(BF16)
        wr_hl = jnp.concatenate([wr_hi, (wr - wr_hi.astype(F32)).astype(BF16)], axis=1)
        g1, b1 = g_post1[l][None, :], b_post1[l][None, :]
        g2, b2 = g_post2[l][None, :], b_post2[l][None, :]
        conv_w = (w_short[l], w_conf_dw[l], b_conf_dw[l], g_conf_ln[l], b_conf_ln[l])

        flat = lambda t: t.reshape(1, bsz * nctx, t.shape[-1])
        unflat = lambda t: t.reshape(bsz, nctx, t.shape[-1])
        if last:
            k_c, v_c = map(unflat, _inproj(flat(xc), m_ctx[0], m_ctx[1], w_in, b_in3, l, OFF_K,
                                           kv_splits, (BF16, BF16), tm=512))
        else:
            uac, ubc, q_c, k_c, v_c = map(unflat, _inproj(flat(xc), m_ctx[0], m_ctx[1], w_in, b_in3, l, 0,
                                                          lat_splits, lat_dtypes, tm=512))

        ua, ub, q, k, v = _inproj(x, m_lat[0], m_lat[1], w_in, b_in3, l, 0, lat_splits, lat_dtypes, tm=512)
        yab = _conv_mixers(ua, ub, *conv_w)
        yc = _neighbourhood_attention(q, k, v, k_c, v_c, na_rpb[l])
        x_mid, hm, logits = _outproj(yab, yc, x, w_out, l, b_out_l, m_lat[2], g1, b1, m_lat[3], m_lat[4],
                                     wr_hi, wr_hl, tm=512)

        idx, gates = _slot_lists(*_route(logits, cap, 0), cap, merge=False)
        idx = idx.reshape(-1)
        x_e = _dispatch(idx, hm, bsz, seq, d, cap)

        x_ec = gates_c = None
        if not last:
            yabc = _conv_mixers(uac, ubc, *conv_w)
            ycc = _context_attention(q_c, k_c, v_c)
            xc_mid, hmc, logits_c = _outproj(flat(yabc), flat(ycc), flat(xc), w_out, l, b_out_l, m_ctx[2], g1, b1,
                                             m_ctx[3], m_ctx[4], wr_hi, wr_hl, tm=512)
            n_c = bsz * cap_ctx
            idx_c, gates_c = _slot_lists(*_route(unflat(logits_c), cap_ctx, cap_ctx), n_c, merge=True)
            idx_c = idx_c.reshape(-1)
            x_ec = _dispatch(idx_c, hmc, 1, bsz * nctx, d, n_c)

        y_e, y_ec = _experts(x_e, gates, x_ec, gates_c, w_gate, w_up, w_down, l, tm=512)
        x = _combine(idx, y_e, x_mid, m_lat[5], g2, b2, cap)
        if not last:
            xc = unflat(_combine(idx_c, y_ec, xc_mid, m_ctx[5], g2, b2, n_c))
    return x
```

```python
import functools
import math

import numpy as np
import jax
import jax.numpy as jnp
from jax import lax
from jax.experimental import pallas as pl
from jax.experimental.pallas import tpu as pltpu

F32 = jnp.float32
BF16 = jnp.bfloat16

D_MODEL = 1024
DEPTH = 2
GRID_W = 64
D_CONV = D_MODEL // 4
D_CONF = D_MODEL // 4
NA_HEAD_DIM = 64
D_NA = D_MODEL - D_CONV - D_CONF
N_NA_HEADS = D_NA // NA_HEAD_DIM
SHORT_CONV_W = 3
CONF_CONV_W = 31
NA_WIN_ROWS_MAX = 8
NA_WIN_COLS = 16
N_EXPERTS = 16
EC_CAPACITY_FACTOR = 2
D_EXPERT = 1024
LN_EPS = 1e-5
DEEPNORM_ALPHA = (2.0 * DEPTH) ** 0.25
NEG_INF = -1e30
LOG2E = math.log2(math.e)

OFF_A = 0
OFF_B = OFF_A + 3 * D_CONV
OFF_Q = OFF_B + 2 * D_CONF
OFF_K = OFF_Q + D_NA
OFF_V = OFF_K + D_NA
D_IN = OFF_V + D_NA

LANES = 128
SUBLANES = 8
MOD_ROWS = 16
VMEM_LIMIT = 56 * 1024 * 1024
ATTN_ROWS = 8
N_PAIRS = D_NA // LANES
ROW_GROUP = 256
OUT_ROW_GROUP = 128
HI = lax.Precision.HIGHEST


def _cparams(sem):
    return pltpu.CompilerParams(dimension_semantics=sem, vmem_limit_bytes=VMEM_LIMIT)


def _ln(x):
    mu = jnp.mean(x, axis=-1, keepdims=True)
    xc = x - mu
    var = jnp.mean(xc * xc, axis=-1, keepdims=True)
    return xc * lax.rsqrt(var + LN_EPS)


def _sigmoid(x):
    return 1.0 / (1.0 + jnp.exp(-x))


def _mod_kernel(cond_ref, w_ref, b_ref, o_ref):
    s = cond_ref[...]
    s = s * _sigmoid(s)
    w = w_ref[0]
    s_hi, w_hi = s.astype(BF16), w.astype(BF16)
    s_lo = (s - s_hi.astype(F32)).astype(BF16)
    w_lo = (w - w_hi.astype(F32)).astype(BF16)
    o_ref[0] = (jnp.dot(s_hi, w_hi, preferred_element_type=F32) + jnp.dot(s_lo, w_hi, preferred_element_type=F32)
                + jnp.dot(s_hi, w_lo, preferred_element_type=F32) + b_ref[0])


N_MOD = 6


def _modulation(cond, w_mod, b_mod):
    n_l, d, n = w_mod.shape
    out = pl.pallas_call(
        _mod_kernel,
        out_shape=jax.ShapeDtypeStruct((n_l * N_MOD, MOD_ROWS, d), F32),
        grid=(n_l, N_MOD),
        in_specs=[
            pl.BlockSpec((MOD_ROWS, d), lambda l, k: (0, 0)),
            pl.BlockSpec((1, d, d), lambda l, k: (l, 0, k)),
            pl.BlockSpec((1, 1, d), lambda l, k: (l, 0, k)),
        ],
        out_specs=pl.BlockSpec((1, MOD_ROWS, d), lambda l, k: (l * N_MOD + k, 0, 0)),
        compiler_params=_cparams(("arbitrary", "arbitrary")),
        name="modulation",
    )(cond, w_mod, b_mod.reshape(n_l, 1, n))
    return out.reshape(n_l * N_MOD * MOD_ROWS, 1, d)


def _mod_spec(mod, d):
    _, row0, per_sample = mod
    if per_sample:
        return pl.BlockSpec((1, 1, d), lambda bi, *_: (row0 + bi, 0, 0))
    return pl.BlockSpec((1, 1, d), lambda bi, *_: (row0, 0, 0))


def _first_step():
    return (pl.program_id(0) == 0) & (pl.program_id(1) == 0)


def _inproj_kernel(x_ref, sh_ref, sc_ref, w_ref, b_ref, *rest, splits, col0):
    *o_refs, w_bf = rest

    @pl.when(_first_step())
    def _():
        w_bf[...] = w_ref[0].astype(BF16)

    tm = x_ref.shape[1]
    n_part = max(1, tm // ROW_GROUP)
    rows_per = tm // n_part
    groups = [slice(part_i * rows_per, (part_i + 1) * rows_per) for part_i in range(n_part)]
    hs = [(_ln(x_ref[0, rows, :]) * (1.0 + sc_ref[0]) + sh_ref[0]).astype(BF16) for rows in groups]
    for rows, h in zip(groups, hs):
        u = jnp.dot(h, w_bf[:, col0:], preferred_element_type=F32) + b_ref[0, :, col0:]
        off = 0
        for o_ref, (width, scale) in zip(o_refs, splits):
            part = u[:, off:off + width]
            if scale != 1.0:
                part = part * scale
            o_ref[0, rows, :] = part.astype(o_ref.dtype)
            off += width


def _inproj(x, sh, sc, w, b, layer, col0, splits, dtypes, tm):
    nb, s, d = x.shape
    n = w.shape[2]
    return pl.pallas_call(
        functools.partial(_inproj_kernel, splits=splits, col0=col0),
        out_shape=[jax.ShapeDtypeStruct((nb, s, wd), dt) for (wd, _), dt in zip(splits, dtypes)],
        grid=(nb, s // tm),
        in_specs=[
            pl.BlockSpec((1, tm, d), lambda bi, i: (bi, i, 0)),
            _mod_spec(sh, d),
            _mod_spec(sc, d),
            pl.BlockSpec((1, d, n), lambda bi, i: (layer, 0, 0), pipeline_mode=pl.Buffered(1)),
            pl.BlockSpec((1, 1, n), lambda bi, i: (layer, 0, 0)),
        ],
        out_specs=[pl.BlockSpec((1, tm, wd), lambda bi, i: (bi, i, 0)) for wd, _ in splits],
        scratch_shapes=[pltpu.VMEM((d, n), BF16)],
        compiler_params=_cparams(("arbitrary", "arbitrary")),
        name="inproj",
    )(x, sh[0], sc[0], w, b)


CONV_CHUNK = 128
Z_PAD = 8
H_PAD = 16


def _conv_kernel(ua_ref, ub_ref, ws_ref, wd_ref, bd_ref, g_ref, b_ref, o_ref, z_scr, h_scr, *, seq):
    c = D_CONV
    z_scr[0:Z_PAD, :] = jnp.zeros((Z_PAD, c), F32)
    z_scr[Z_PAD + seq:2 * Z_PAD + seq, :] = jnp.zeros((Z_PAD, c), F32)
    h_scr[0, 0:H_PAD, :] = jnp.zeros((H_PAD, c), F32)
    h_scr[0, H_PAD + seq:2 * H_PAD + seq, :] = jnp.zeros((H_PAD, c), F32)
    z_scr[Z_PAD:Z_PAD + seq, :] = ua_ref[0, :, c:2 * c] * ua_ref[0, :, 2 * c:3 * c]
    h_scr[0, H_PAD:H_PAD + seq, :] = ub_ref[0, :, 0:c] * _sigmoid(ub_ref[0, :, c:2 * c])
    n_rows = seq + 2 * H_PAD
    h_all = h_scr[0]
    for r in range(1, SUBLANES):
        h_scr[r] = pltpu.roll(h_all, n_rows - r, axis=0)
    tc = min(CONV_CHUNK, seq)
    for ci in range(seq // tc):
        t0 = ci * tc
        acc = ws_ref[0:1, :] * z_scr[t0 + Z_PAD - 1:t0 + Z_PAD - 1 + tc, :]
        for j in range(1, SHORT_CONV_W):
            s0 = t0 + Z_PAD - 1 + j
            acc = acc + ws_ref[j:j + 1, :] * z_scr[s0:s0 + tc, :]
        ya = ua_ref[0, t0:t0 + tc, 0:c] * acc
        hb = bd_ref[...]
        for j in range(CONF_CONV_W):
            s0 = t0 + H_PAD - CONF_CONV_W // 2 + j
            a0 = s0 - s0 % SUBLANES
            hb = hb + wd_ref[j:j + 1, :] * h_scr[s0 % SUBLANES, a0:a0 + tc, :]
        hn = _ln(hb) * g_ref[...] + b_ref[...]
        yb = hn * _sigmoid(hn)
        o_ref[0, t0:t0 + tc, 0:c] = ya.astype(o_ref.dtype)
        o_ref[0, t0:t0 + tc, c:2 * c] = yb.astype(o_ref.dtype)


def _conv_mixers(ua, ub, w_short, w_dw, b_dw, g_ln, b_ln):
    nb, s, _ = ua.shape
    c = D_CONV
    full = lambda shape: pl.BlockSpec(shape, lambda bi: (0,) * len(shape))
    return pl.pallas_call(
        functools.partial(_conv_kernel, seq=s),
        out_shape=jax.ShapeDtypeStruct((nb, s, 2 * c), BF16),
        grid=(nb,),
        in_specs=[
            pl.BlockSpec((1, s, 3 * c), lambda bi: (bi, 0, 0)),
            pl.BlockSpec((1, s, 2 * c), lambda bi: (bi, 0, 0)),
            full((SHORT_CONV_W, c)), full((CONF_CONV_W, c)), full((1, c)), full((1, c)), full((1, c)),
        ],
        out_specs=pl.BlockSpec((1, s, 2 * c), lambda bi: (bi, 0, 0)),
        scratch_shapes=[pltpu.VMEM((s + 2 * Z_PAD, c), F32), pltpu.VMEM((SUBLANES, s + 2 * H_PAD, c), F32)],
        compiler_params=_cparams(("arbitrary",)),
        name="conv_mixers",
    )(ua, ub, w_short, w_dw, b_dw.reshape(1, c), g_ln.reshape(1, c), b_ln.reshape(1, c))


SUB_ROWS = 2
WIN_ROWS = SUB_ROWS + NA_WIN_ROWS_MAX - 1
BAND_KEYS = WIN_ROWS * GRID_W
SUB_Q = SUB_ROWS * GRID_W
N_DROW = 2 * NA_WIN_ROWS_MAX - 1
N_DCOL = 2 * NA_WIN_COLS - 1


def _bias_kernel(rpb_ref, nxt_ref, o_ref):
    n_cols = o_ref.shape[1]
    col = lax.broadcasted_iota(jnp.int32, (LANES, n_cols), 1)
    qc = col >> (LANES.bit_length() - 1)
    kc = col & (GRID_W - 1)
    d_col = jnp.clip(kc - qc + (NA_WIN_COLS - 1), 0, N_DCOL - 1)
    onehot = (lax.broadcasted_iota(jnp.int32, (LANES, n_cols), 0) == d_col).astype(F32)
    own = jnp.dot(rpb_ref[...], onehot, preferred_element_type=F32, precision=HI) * LOG2E
    nxt = jnp.dot(nxt_ref[...], onehot, preferred_element_type=F32, precision=HI) * LOG2E
    c0 = jnp.clip(qc - NA_WIN_COLS // 2, 0, GRID_W - NA_WIN_COLS)
    inside = (kc >= c0) & (kc < c0 + NA_WIN_COLS)
    second = (col & GRID_W) != 0
    o_ref[0:LANES, :] = jnp.where(inside, jnp.where(second, nxt, own), NEG_INF)
    o_ref[LANES:2 * LANES, :] = jnp.where(inside & ~second, own, NEG_INF)
    o_ref[2 * LANES:3 * LANES, :] = jnp.where(inside & second, own, NEG_INF)
    o_ref[3 * LANES:, :] = jnp.full((o_ref.shape[0] - 3 * LANES, n_cols), NEG_INF, F32)


N_SLABS = -(-WIN_ROWS // 2)
TILES_PER_KIND = LANES
MASKED_TILE = 3 * TILES_PER_KIND


def _na_plan(rows):
    wr = min(NA_WIN_ROWS_MAX, rows)
    assert wr == NA_WIN_ROWS_MAX and rows % SUB_ROWS == 0 and rows >= WIN_ROWS
    row_start = np.clip(np.arange(rows) - wr // 2, 0, rows - wr)
    w0s, tiles = [], []
    for r0 in range(0, rows, SUB_ROWS):
        w0 = int(np.clip(r0 - wr // 2, 0, rows - WIN_ROWS))
        for iq in range(SUB_ROWS):
            r = r0 + iq
            assert row_start[r] >= w0 and row_start[r] + wr <= w0 + WIN_ROWS
            ok = lambda w: w < WIN_ROWS and row_start[r] <= w0 + w < row_start[r] + wr
            d_row = lambda w: w0 + w - r + NA_WIN_ROWS_MAX - 1
            for j in range(N_SLABS):
                lo, hi = ok(2 * j), ok(2 * j + 1)
                if lo and hi:
                    tiles.append(d_row(2 * j))
                elif lo:
                    tiles.append(TILES_PER_KIND + d_row(2 * j))
                elif hi:
                    tiles.append(2 * TILES_PER_KIND + d_row(2 * j + 1))
                else:
                    tiles.append(-1)
        w0s.append(w0)
    return np.array(w0s, np.int32), np.array(tiles, np.int32)


def _na_bias(rpb):
    n_hd = N_NA_HEADS * N_DROW
    assert GRID_W == 64 and 2 * GRID_W == LANES and n_hd <= TILES_PER_KIND and N_DCOL <= LANES
    rpb = rpb.astype(F32)
    pad = lambda t: jnp.pad(t.reshape(n_hd, N_DCOL), ((0, LANES - n_hd), (0, LANES - N_DCOL)))
    nxt = jnp.concatenate([rpb[:, 1:], jnp.zeros_like(rpb[:, :1])], axis=1)
    n_tiles = 3 * TILES_PER_KIND + SUBLANES
    table = pl.pallas_call(
        _bias_kernel,
        out_shape=jax.ShapeDtypeStruct((n_tiles, GRID_W * LANES), F32),
        compiler_params=pltpu.CompilerParams(vmem_limit_bytes=VMEM_LIMIT),
        name="na_bias",
    )(pad(rpb), pad(nxt))
    return table.reshape(n_tiles, GRID_W, LANES)


def _lane_reduce(xs, combine, reduce, neutral):
    chunks = []
    for x in xs:
        rows, n = x.shape
        n_full = n // LANES
        chunks += [x[:, j * LANES:(j + 1) * LANES] for j in range(n_full)]
        if n % LANES:
            fill = jnp.full((rows, LANES - n % LANES), neutral, x.dtype)
            chunks.append(jnp.concatenate([x[:, n_full * LANES:], fill], axis=1))
    return reduce(functools.reduce(combine, chunks), axis=-1, keepdims=True)


def _attn_kernel(w0_ref, tile_ref, q_ref, k_ref, v_ref, kc_ref, vc_ref, *rest, banded, n_sub, sub_q):
    if banded:
        bias_ref, o_ref = rest
    else:
        (o_ref,) = rest
    lane = lax.broadcasted_iota(jnp.int32, (sub_q, LANES), 1)
    first = lane < NA_HEAD_DIM
    nt = (((1,), (1,)), ((), ()))
    stages = [(si, p) for si in range(n_sub) for p in range(N_PAIRS)]

    def window(si):
        blk = pl.program_id(1) * n_sub + si
        return blk, pl.multiple_of(w0_ref[blk] * GRID_W, GRID_W)

    def bias(blk, head):
        row_blocks = []
        for iq in range(SUB_ROWS):
            slabs = []
            for j in range(N_SLABS):
                t = tile_ref[(blk * SUB_ROWS + iq) * N_SLABS + j]
                tile = bias_ref[jnp.where(t < 0, MASKED_TILE, t + head * N_DROW)]
                width = min(LANES, BAND_KEYS - j * LANES)
                slabs.append(tile[:, :width])
            row_blocks.append(jnp.concatenate(slabs, axis=1))
        return jnp.concatenate(row_blocks, axis=0)

    def scores(si, p):
        cols = slice(p * LANES, (p + 1) * LANES)
        q_p = q_ref[0, si * sub_q:(si + 1) * sub_q, cols]
        zero = jnp.zeros_like(q_p)
        qq = jnp.concatenate([jnp.where(first, q_p, zero), jnp.where(first, zero, q_p)], axis=0)
        parts = [lax.dot_general(qq, kc_ref[0, :, cols], nt, preferred_element_type=F32)]
        if banded:
            blk, start = window(si)
            both = jnp.concatenate([bias(blk, 2 * p), bias(blk, 2 * p + 1)], axis=0)
            parts.append(lax.dot_general(qq, k_ref[0, pl.ds(start, BAND_KEYS), cols], nt,
                                         preferred_element_type=F32) + both)
        return parts

    def weights(parts):
        m = _lane_reduce(parts, jnp.maximum, jnp.max, NEG_INF)
        es = [jnp.exp2(s - m) for s in parts]
        den = _lane_reduce(es, jnp.add, jnp.sum, 0.0)
        return [e.astype(BF16) for e in es], den

    def values(si, p, es, den):
        cols = slice(p * LANES, (p + 1) * LANES)
        o = jnp.dot(es[0], vc_ref[0, :, cols], preferred_element_type=F32)
        if banded:
            _, start = window(si)
            o = o + jnp.dot(es[1], v_ref[0, pl.ds(start, BAND_KEYS), cols], preferred_element_type=F32)
        o = o * (1.0 / den)
        out = jnp.where(first, o[:sub_q], o[sub_q:])
        o_ref[0, si * sub_q:(si + 1) * sub_q, cols] = out.astype(o_ref.dtype)

    nxt = scores(*stages[0])
    pending = None
    for i, (si, p) in enumerate(stages):
        cur = nxt
        if i + 1 < len(stages):
            nxt = scores(*stages[i + 1])
        if pending is not None:
            values(*pending)
        es, den = weights(cur)
        pending = (si, p, es, den)
    values(*pending)


def _neighbourhood_attention(q, k, v, kc, vc, rpb):
    nb, s, dn = q.shape
    rows = s // GRID_W
    nctx = kc.shape[1]
    w0s, tiles = _na_plan(rows)
    bias = _na_bias(rpb)
    n_sub = ATTN_ROWS // SUB_ROWS
    m_rows = ATTN_ROWS * GRID_W
    grid_spec = pltpu.PrefetchScalarGridSpec(
        num_scalar_prefetch=2,
        grid=(nb, rows // ATTN_ROWS),
        in_specs=[
            pl.BlockSpec((1, m_rows, dn), lambda bi, i, w0, pat: (bi, i, 0)),
            pl.BlockSpec((1, s, dn), lambda bi, i, w0, pat: (bi, 0, 0)),
            pl.BlockSpec((1, s, dn), lambda bi, i, w0, pat: (bi, 0, 0)),
            pl.BlockSpec((1, nctx, dn), lambda bi, i, w0, pat: (bi, 0, 0)),
            pl.BlockSpec((1, nctx, dn), lambda bi, i, w0, pat: (bi, 0, 0)),
            pl.BlockSpec(bias.shape, lambda bi, i, w0, pat: (0, 0, 0), pipeline_mode=pl.Buffered(1)),
        ],
        out_specs=pl.BlockSpec((1, m_rows, dn), lambda bi, i, w0, pat: (bi, i, 0)),
    )
    return pl.pallas_call(
        functools.partial(_attn_kernel, banded=True, n_sub=n_sub, sub_q=SUB_Q),
        out_shape=jax.ShapeDtypeStruct((nb, s, dn), BF16),
        grid_spec=grid_spec,
        compiler_params=_cparams(("arbitrary", "arbitrary")),
        name="neighbourhood_attention",
    )(jnp.asarray(w0s), jnp.asarray(tiles), q, k, v, kc, vc, bias)


def _context_attention(q, kc, vc):
    nb, s, dn = q.shape
    spec = pl.BlockSpec((1, s, dn), lambda bi, i, w0, pat: (bi, 0, 0))
    grid_spec = pltpu.PrefetchScalarGridSpec(
        num_scalar_prefetch=2, grid=(nb, 1), in_specs=[spec] * 5, out_specs=spec)
    dummy = jnp.zeros((1,), jnp.int32)
    return pl.pallas_call(
        functools.partial(_attn_kernel, banded=False, n_sub=1, sub_q=s),
        out_shape=jax.ShapeDtypeStruct((nb, s, dn), BF16),
        grid_spec=grid_spec,
        compiler_params=_cparams(("arbitrary", "arbitrary")),
        name="context_attention",
    )(dummy, dummy, q, kc, vc, kc, vc)


def _store_rowmajor(ref, val, base=0):
    n, width = val.shape
    chunks = width // LANES
    for c in range(chunks):
        ref[pl.ds(base + c, n, stride=chunks), :] = val[:, c * LANES:(c + 1) * LANES]


def _load_rowmajor(ref, base, n, chunks):
    return jnp.concatenate([ref[pl.ds(base + c, n, stride=chunks), :] for c in range(chunks)], axis=1)


def _outproj_kernel(yab_ref, yc_ref, x_ref, w_ref, bo_ref, gt_ref, g_ref, b_ref, sh_ref, sc_ref,
                    wrh_ref, wrhl_ref, xmid_ref, hm_ref, lg_ref, w_bf):
    @pl.when(_first_step())
    def _():
        w_bf[...] = w_ref[0].astype(BF16)

    half = yab_ref.shape[2]
    tm = x_ref.shape[1]
    chunks = x_ref.shape[2] // LANES
    n_part = max(1, tm // OUT_ROW_GROUP)
    rows_per = tm // n_part
    groups = [slice(part * rows_per, (part + 1) * rows_per) for part in range(n_part)]
    ys = [jnp.dot(yab_ref[0, rows, :], w_bf[0:half, :], preferred_element_type=F32)
          + jnp.dot(yc_ref[0, rows, :], w_bf[half:, :], preferred_element_type=F32) + bo_ref[...]
          for rows in groups]
    for part, (rows, y) in enumerate(zip(groups, ys)):
        xm = _ln(DEEPNORM_ALPHA * x_ref[0, rows, :] + (1.0 + gt_ref[0]) * y) * g_ref[...] + b_ref[...]
        xmid_ref[0, rows, :] = xm
        hm = _ln(xm) * (1.0 + sc_ref[0]) + sh_ref[0]
        _store_rowmajor(hm_ref, hm, base=part * rows_per * chunks)
        hm_hi = hm.astype(BF16)
        hm_lo = (hm - hm_hi.astype(F32)).astype(BF16)
        both = jnp.dot(hm_hi, wrhl_ref[...], preferred_element_type=F32)
        lg_ref[0, rows, :] = (both[:, :LANES] + both[:, LANES:]
                              + jnp.dot(hm_lo, wrh_ref[...], preferred_element_type=F32))


def _outproj(yab, yc, x, w, layer, bo, gt, g, b, sh, sc, wr_hi, wr_hl, tm):
    nb, s, d = x.shape
    half = yab.shape[2]
    vec = pl.BlockSpec((1, d), lambda bi, i: (0, 0))
    tok = lambda width: pl.BlockSpec((1, tm, width), lambda bi, i: (bi, i, 0))
    n_i = s // tm
    return pl.pallas_call(
        _outproj_kernel,
        out_shape=[jax.ShapeDtypeStruct((nb, s, d), F32),
                   jax.ShapeDtypeStruct((nb * s * (d // LANES), LANES), F32),
                   jax.ShapeDtypeStruct((nb, s, LANES), F32)],
        grid=(nb, n_i),
        in_specs=[tok(half), tok(half), tok(d),
                  pl.BlockSpec((1, d, d), lambda bi, i: (layer, 0, 0), pipeline_mode=pl.Buffered(1)),
                  vec, _mod_spec(gt, d), vec, vec,
                  _mod_spec(sh, d), _mod_spec(sc, d), pl.BlockSpec((d, LANES), lambda bi, i: (0, 0)),
                  pl.BlockSpec((d, 2 * LANES), lambda bi, i: (0, 0))],
        out_specs=[tok(d), pl.BlockSpec((tm * (d // LANES), LANES), lambda bi, i: (bi * n_i + i, 0)),
                   tok(LANES)],
        scratch_shapes=[pltpu.VMEM((d, d), BF16)],
        compiler_params=_cparams(("arbitrary", "arbitrary")),
        name="outproj_postnorm",
    )(yab, yc, x, w, bo, gt[0], g, b, sh[0], sc[0], wr_hi, wr_hl)


CUM_CHUNK = 256
F32_EXP_BIAS = 127
F32_MANT_BITS = 23


def _prefix_count(mask_f32, tri):
    rows, n = mask_f32.shape
    tc = min(CUM_CHUNK, n)
    base = jnp.zeros((rows, 1), F32)
    parts = []
    for ci in range(n // tc):
        blk = mask_f32[:, ci * tc:(ci + 1) * tc]
        parts.append(jnp.dot(blk.astype(BF16), tri[:tc, :tc], preferred_element_type=F32) + base)
        base = base + jnp.sum(blk, axis=-1, keepdims=True)
    return jnp.concatenate(parts, axis=-1)


def _pow2(k):
    return pltpu.bitcast((k + F32_EXP_BIAS) << F32_MANT_BITS, F32)


def _route_kernel(lg_ref, slot_c_ref, gate_t_ref, *, cap, slot_stride):
    nb = lg_ref.shape[0]
    assert nb * N_EXPERTS == LANES
    rows = []
    for b in range(nb):
        lg = lg_ref[b]
        lane = lax.broadcasted_iota(jnp.int32, lg.shape, 1)
        lgm = jnp.where(lane < N_EXPERTS, lg, NEG_INF)
        ex = jnp.exp(lgm - jnp.max(lgm, axis=-1, keepdims=True))
        aff = ex / jnp.sum(ex, axis=-1, keepdims=True)
        rows.append(aff.T[0:N_EXPERTS, :])
    a = jnp.concatenate(rows, axis=0)
    capf = float(cap)

    def enough(t):
        return jnp.sum((a >= t).astype(F32), axis=-1, keepdims=True) >= capf

    def exp_step(_, carry):
        lo, hi = carry
        mid = lo + ((hi - lo + 1) >> 1)
        ok = enough(_pow2(mid))
        return jnp.where(ok, mid, lo), jnp.where(ok, hi, mid - 1)

    k_lo = jnp.full((LANES, 1), -F32_EXP_BIAS, jnp.int32)
    k_hi = jnp.zeros((LANES, 1), jnp.int32)
    k_lo, _ = lax.fori_loop(0, 7, exp_step, (k_lo, k_hi))
    base = _pow2(k_lo)

    def mant_step(_, carry):
        t, step = carry
        step = step * 0.5
        cand = t + step
        return jnp.where(enough(cand), cand, t), step

    thr, _ = lax.fori_loop(0, F32_MANT_BITS, mant_step, (base, base))

    r_i = lax.broadcasted_iota(jnp.int32, (CUM_CHUNK, CUM_CHUNK), 0)
    c_i = lax.broadcasted_iota(jnp.int32, (CUM_CHUNK, CUM_CHUNK), 1)
    tri = (r_i < c_i).astype(BF16)
    gt = (a > thr).astype(F32)
    eq = (a == thr).astype(F32)
    need = capf - jnp.sum(gt, axis=-1, keepdims=True)
    sel = gt + eq * (_prefix_count(eq, tri) < need).astype(F32)
    pos = _prefix_count(sel, tri)
    sample = lax.broadcasted_iota(jnp.int32, (LANES, 1), 0) >> (N_EXPERTS.bit_length() - 1)
    slot = jnp.where(sel > 0.0, pos + (sample * slot_stride).astype(F32), -1.0)
    for b in range(nb):
        lo = b * N_EXPERTS
        gate_t_ref[b] = a[lo:lo + N_EXPERTS, :]
        rolled = slot if b == 0 else jnp.concatenate([slot[lo:, :], slot[:lo, :]], axis=0)
        slot_c_ref[b] = rolled.T


def _route(logits, cap, slot_stride):
    nb, s, _ = logits.shape
    whole = lambda shape: pl.BlockSpec(shape, lambda i: (0,) * len(shape))
    return pl.pallas_call(
        functools.partial(_route_kernel, cap=cap, slot_stride=slot_stride),
        out_shape=[jax.ShapeDtypeStruct((nb, s, LANES), F32), jax.ShapeDtypeStruct((nb, N_EXPERTS, s), F32)],
        grid=(1,),
        in_specs=[whole((nb, s, LANES))],
        out_specs=[whole((nb, s, LANES)), whole((nb, N_EXPERTS, s))],
        compiler_params=_cparams(("arbitrary",)),
        name="route",
    )(logits)


TOK_SPLIT = 64


def _slot_list_kernel(slot_ref, gate_ref, idx_ref, g_ref, *, n_slots, tok_stride, merge):
    s = slot_ref.shape[1]
    b = pl.program_id(0)
    assert n_slots <= 256
    slot_id = lax.broadcasted_iota(jnp.int32, (s, n_slots), 1).astype(F32).astype(BF16)
    one, zero = jnp.ones((s, n_slots), BF16), jnp.zeros((s, n_slots), BF16)
    tok = lax.broadcasted_iota(jnp.int32, (1, s), 1) + b * tok_stride
    tok_hi = (tok >> (TOK_SPLIT.bit_length() - 1)).astype(F32)
    tok_lo = (tok & (TOK_SPLIT - 1)).astype(F32)
    zeros = jnp.zeros((SUBLANES - 5, s), F32)
    idx_rows, g_rows = [], []
    for e in range(N_EXPERTS):
        taken = jnp.broadcast_to(slot_ref[0, :, e:e + 1].astype(BF16), (s, n_slots))
        hit = jnp.where(taken == slot_id, one, zero)
        g0 = gate_ref[0, e:e + 1, :]
        g_hi = g0.astype(BF16).astype(F32)
        g_mid = (g0 - g_hi).astype(BF16).astype(F32)
        g_lo = g0 - g_hi - g_mid
        lhs = jnp.concatenate([tok_hi, tok_lo, g_hi, g_mid, g_lo, zeros], axis=0).astype(BF16)
        out = jnp.dot(lhs, hit, preferred_element_type=F32)
        idx_rows.append(out[0:1] * float(TOK_SPLIT) + out[1:2])
        g_rows.append(out[2:3] + out[3:4] + out[4:5])
    idx = jnp.concatenate(idx_rows, axis=0).astype(jnp.int32) * SUBLANES
    g = jnp.concatenate(g_rows + [jnp.zeros((LANES - N_EXPERTS, n_slots), F32)], axis=0)
    g_t = g.T
    g_cols = [jnp.broadcast_to(g_t[:, e:e + 1], (n_slots, LANES)) for e in range(N_EXPERTS)]
    if merge:
        @pl.when(b == 0)
        def _():
            idx_ref[0] = idx
            for e in range(N_EXPERTS):
                g_ref[e] = g_cols[e]

        @pl.when(b > 0)
        def _():
            idx_ref[0] = idx_ref[0] + idx
            for e in range(N_EXPERTS):
                g_ref[e] = g_ref[e] + g_cols[e]
    else:
        idx_ref[0] = idx
        for e in range(N_EXPERTS):
            g_ref[e] = g_cols[e]


def _slot_lists(slot_c, gate_t, n_slots, merge):
    nb, s, _ = slot_c.shape
    nbo = 1 if merge else nb
    idx_map = (lambda bi: (0, 0, 0)) if merge else (lambda bi: (bi, 0, 0))
    g_map = (lambda bi: (0, 0, 0)) if merge else (lambda bi: (0, bi, 0))
    return pl.pallas_call(
        functools.partial(_slot_list_kernel, n_slots=n_slots, tok_stride=s if merge else 0, merge=merge),
        out_shape=[jax.ShapeDtypeStruct((nbo, N_EXPERTS, n_slots), jnp.int32),
                   jax.ShapeDtypeStruct((N_EXPERTS, nbo * n_slots, LANES), F32)],
        grid=(nb,),
        in_specs=[pl.BlockSpec((1, s, LANES), lambda bi: (bi, 0, 0)),
                  pl.BlockSpec((1, N_EXPERTS, s), lambda bi: (bi, 0, 0))],
        out_specs=[pl.BlockSpec((1, N_EXPERTS, n_slots), idx_map),
                   pl.BlockSpec((N_EXPERTS, n_slots, LANES), g_map)],
        compiler_params=_cparams(("arbitrary",)),
        name="slot_lists",
    )(slot_c, gate_t)


ROW_UNROLL = 32


def _dispatch_kernel(idx_ref, hm_ref, x_ref, rows_scr, *, n_slots):
    b = pl.program_id(0)
    chunks = x_ref.shape[2] // LANES

    def emit(e, s0):
        src = pl.multiple_of(s0 * chunks, ROW_UNROLL * chunks)
        rows = _load_rowmajor(rows_scr.at[e % 2], src, ROW_UNROLL, chunks)
        x_ref[e, pl.ds(pl.multiple_of(s0, ROW_UNROLL), ROW_UNROLL), :] = rows.astype(x_ref.dtype)

    for e in range(N_EXPERTS + 1):
        base = (b * N_EXPERTS + e) * n_slots

        def move(i, carry):
            s0 = i * ROW_UNROLL
            if e < N_EXPERTS:
                first = base + s0
                tiles = [hm_ref[pl.ds(pl.multiple_of(idx_ref[first + u], chunks), chunks), :]
                         for u in range(ROW_UNROLL)]
                dst = pl.multiple_of(s0 * chunks, ROW_UNROLL * chunks)
                rows_scr[e % 2, pl.ds(dst, ROW_UNROLL * chunks), :] = jnp.concatenate(tiles, axis=0)
            if e > 0:
                emit(e - 1, s0)
            return carry

        lax.fori_loop(0, n_slots // ROW_UNROLL, move, 0)


def _dispatch(idx, hm_rm, nb, s, d, n_slots):
    chunks = d // LANES
    assert chunks == SUBLANES and n_slots % ROW_UNROLL == 0
    grid_spec = pltpu.PrefetchScalarGridSpec(
        num_scalar_prefetch=1,
        grid=(nb,),
        in_specs=[pl.BlockSpec((s * chunks, LANES), lambda bi, idx_r: (bi, 0))],
        out_specs=pl.BlockSpec((N_EXPERTS, n_slots, d), lambda bi, idx_r: (0, bi, 0)),
        scratch_shapes=[pltpu.VMEM((2, n_slots * chunks, LANES), F32)],
    )
    return pl.pallas_call(
        functools.partial(_dispatch_kernel, n_slots=n_slots),
        out_shape=jax.ShapeDtypeStruct((N_EXPERTS, nb * n_slots, d), BF16),
        grid_spec=grid_spec,
        compiler_params=_cparams(("arbitrary",)),
        name="dispatch",
    )(idx, hm_rm)


W_PARTS = 4

def _expert_kernel(*refs, n_main, has_extra):
    n_in = 4 if has_extra else 2
    n_out = 2 if has_extra else 1
    acts, w_parts = refs[:n_in], refs[n_in:n_in + 3]
    outs = refs[n_in + 3:n_in + 3 + n_out]
    w_bfs = refs[n_in + 3 + n_out:]
    if has_extra:
        x_ref, g_ref, xx_ref, gg_ref = acts
        y_ref, yy_ref = outs
    else:
        x_ref, g_ref = acts
        (y_ref,) = outs
    s = pl.program_id(0)
    i = pl.program_id(1)
    fill = s % 2
    cur = 1 - fill
    wg_bf, wu_bf, wd_bf = (w_bf.at[cur] for w_bf in w_bfs)

    def cast_part():
        block = jnp.minimum(i, W_PARTS - 1)
        for part, w_bf in zip(w_parts, w_bfs):
            rows = part.shape[2]
            w_bf[fill, pl.ds(pl.multiple_of(block * rows, rows), rows), :] = part[0, 0].astype(BF16)

    def ffn(x_in, gate_in, out_ref):
        rows_all = x_in.shape[1]
        chunks = x_in.shape[2] // LANES
        n_part = max(1, rows_all // ROW_GROUP)
        rows_per = rows_all // n_part
        groups = [slice(p * rows_per, (p + 1) * rows_per) for p in range(n_part)]
        ups = []
        for rows in groups:
            x_e = x_in[0, rows, :]
            ups.append((jnp.dot(x_e, wg_bf[...], preferred_element_type=F32),
                        jnp.dot(x_e, wu_bf[...], preferred_element_type=F32)))
        for p, (rows, (a, u)) in enumerate(zip(groups, ups)):
            h = (a * _sigmoid(a) * u).astype(BF16)
            y = jnp.dot(h, wd_bf[...], preferred_element_type=F32) * gate_in[0, rows, 0:1]
            _store_rowmajor(out_ref, y, base=p * rows_per * chunks)

    @pl.when(s == 0)
    def _():
        cast_part()

    if has_extra:
        @pl.when((s > 0) & (i < n_main))
        def _():
            cast_part()
            ffn(x_ref, g_ref, y_ref.at[0])

        @pl.when((s > 0) & (i == n_main))
        def _():
            cast_part()
            ffn(xx_ref, gg_ref, yy_ref.at[0])
    else:
        @pl.when(s > 0)
        def _():
            cast_part()
            ffn(x_ref, g_ref, y_ref.at[0])


def _experts(x_e, g_e, x_extra, g_extra, wg, wu, wd, layer, tm):
    _, rows, d = x_e.shape
    f = wg.shape[3]
    chunks = d // LANES
    n_main = rows // tm
    has_extra = x_extra is not None
    assert n_main >= W_PARTS and d % W_PARTS == 0 and f % W_PARTS == 0
    expert = lambda s: jnp.maximum(s - 1, 0)
    main_map = lambda s, i: (expert(s), jnp.where(s == 0, 0, jnp.minimum(i, n_main - 1)), 0)
    extra_map = lambda s, i: (expert(s), 0, 0)

    def w_spec(shape):
        blk = (1, 1, shape[0] // W_PARTS, shape[1])
        return pl.BlockSpec(blk, lambda s, i: (layer, jnp.minimum(s, N_EXPERTS - 1),
                                               jnp.minimum(i, W_PARTS - 1), 0))

    in_specs = [pl.BlockSpec((1, tm, d), main_map), pl.BlockSpec((1, tm, LANES), main_map)]
    out_specs = [pl.BlockSpec((1, tm * chunks, LANES), main_map)]
    out_shape = [jax.ShapeDtypeStruct((N_EXPERTS, rows * chunks, LANES), F32)]
    args = [x_e, g_e]
    if has_extra:
        rows2 = x_extra.shape[1]
        in_specs += [pl.BlockSpec((1, rows2, d), extra_map), pl.BlockSpec((1, rows2, LANES), extra_map)]
        out_specs.append(pl.BlockSpec((1, rows2 * chunks, LANES), extra_map))
        out_shape.append(jax.ShapeDtypeStruct((N_EXPERTS, rows2 * chunks, LANES), F32))
        args += [x_extra, g_extra]
    outs = pl.pallas_call(
        functools.partial(_expert_kernel, n_main=n_main, has_extra=has_extra),
        out_shape=out_shape,
        grid=(N_EXPERTS + 1, n_main + int(has_extra)),
        in_specs=in_specs + [w_spec((d, f)), w_spec((d, f)), w_spec((f, d))],
        out_specs=out_specs,
        scratch_shapes=[pltpu.VMEM((2, d, f), BF16), pltpu.VMEM((2, d, f), BF16), pltpu.VMEM((2, f, d), BF16)],
        compiler_params=_cparams(("arbitrary", "arbitrary")),
        name="experts",
    )(*args, wg, wu, wd)
    return outs if has_extra else (outs[0], None)


COMBINE_EXPERTS = 4
ADD_UNROLL = 16


def _combine_kernel(idx_ref, y_ref, x_ref, gt_ref, g_ref, b_ref, o_ref, acc, *, n_slots, n_samples):
    s = pl.program_id(0)
    j = pl.program_id(1)
    d = x_ref.shape[2]
    chunks = d // LANES
    per_step = x_ref.shape[1]
    fill = s % 2
    done = 1 - fill
    base = (s * N_EXPERTS + j * COMBINE_EXPERTS) * n_slots

    part = per_step // COMBINE_EXPERTS

    def scatter(eg):
        for g in range(n_slots // ADD_UNROLL):
            first = base + eg * n_slots + g * ADD_UNROLL
            y_rows = y_ref[eg, g * ADD_UNROLL * chunks:(g + 1) * ADD_UNROLL * chunks, :]
            new = []
            for u in range(ADD_UNROLL):
                dst = pl.multiple_of(idx_ref[first + u], chunks)
                new.append((dst, acc[fill, pl.ds(dst, chunks), :] + y_rows[u * chunks:(u + 1) * chunks, :]))
            for dst, val in new:
                acc[fill, pl.ds(dst, chunks), :] = val

    def summed(eg):
        row0 = pl.multiple_of((j * per_step + eg * part) * chunks, part * chunks)
        return _load_rowmajor(acc.at[done], row0, part, chunks)

    def post_norm(eg, ym):
        rows = slice(eg * part, (eg + 1) * part)
        z = DEEPNORM_ALPHA * x_ref[0, rows, :] + (1.0 + gt_ref[0]) * ym
        o_ref[0, rows, :] = _ln(z) * g_ref[...] + b_ref[...]

    @pl.when((s < n_samples) & (j == 0))
    def _():
        acc[fill] = jnp.zeros(acc.shape[1:], F32)

    @pl.when(s == 0)
    def _():
        for eg in range(COMBINE_EXPERTS):
            scatter(eg)

    @pl.when((s > 0) & (s < n_samples))
    def _():
        for eg in range(COMBINE_EXPERTS):
            ym = summed(eg)
            scatter(eg)
            post_norm(eg, ym)

    @pl.when(s == n_samples)
    def _():
        for eg in range(COMBINE_EXPERTS):
            post_norm(eg, summed(eg))


def _combine(idx, y_rm, x_mid, gt, g, b, n_slots):
    nb, s, d = x_mid.shape
    chunks = d // LANES
    n_groups = N_EXPERTS // COMBINE_EXPERTS
    per_step = s // n_groups
    assert chunks == SUBLANES and N_EXPERTS % COMBINE_EXPERTS == 0 and s % n_groups == 0
    assert per_step % SUBLANES == 0 and n_slots % ADD_UNROLL == 0
    prev = lambda si: jnp.maximum(si - 1, 0)
    tok_map = lambda si, j, i_r: (prev(si), jnp.where(si == 0, 0, j), 0)
    vec = pl.BlockSpec((1, d), lambda si, j, i_r: (0, 0))
    _, row0, per_sample = gt
    gt_spec = pl.BlockSpec((1, 1, d), (lambda si, j, i_r: (row0 + prev(si), 0, 0)) if per_sample
                           else (lambda si, j, i_r: (row0, 0, 0)))
    grid_spec = pltpu.PrefetchScalarGridSpec(
        num_scalar_prefetch=1,
        grid=(nb + 1, n_groups),
        in_specs=[
            pl.BlockSpec((COMBINE_EXPERTS, n_slots * chunks, LANES),
                         lambda si, j, i_r: (j, jnp.minimum(si, nb - 1), 0)),
            pl.BlockSpec((1, per_step, d), tok_map),
            gt_spec, vec, vec,
        ],
        out_specs=pl.BlockSpec((1, per_step, d), tok_map),
        scratch_shapes=[pltpu.VMEM((2, s * chunks, LANES), F32)],
    )
    return pl.pallas_call(
        functools.partial(_combine_kernel, n_slots=n_slots, n_samples=nb),
        out_shape=jax.ShapeDtypeStruct((nb, s, d), F32),
        grid_spec=grid_spec,
        compiler_params=_cparams(("arbitrary", "arbitrary")),
        name="combine_postnorm",
    )(idx, y_rm, x_mid, gt[0], g, b)


def kernel(x, c, ctx, c_ctx, w_mod, b_mod, w_in, b_in, w_short, w_conf_dw, b_conf_dw, g_conf_ln, b_conf_ln,
           na_rpb, w_out, b_out, g_post1, b_post1, w_router, w_gate, w_up, w_down, g_post2, b_post2):
    bsz, seq, d = x.shape
    nctx = ctx.shape[1]
    cap = EC_CAPACITY_FACTOR * seq // N_EXPERTS
    cap_ctx = EC_CAPACITY_FACTOR * nctx // N_EXPERTS
    q_scale = NA_HEAD_DIM ** -0.5 * LOG2E

    cond = jnp.concatenate([c, c_ctx[None, :], jnp.zeros((MOD_ROWS - bsz - 1, d), F32)], axis=0)
    mods = _modulation(cond, w_mod, b_mod)

    lat_splits = ((3 * D_CONV, 1.0), (2 * D_CONF, 1.0), (D_NA, q_scale), (D_NA, 1.0), (D_NA, 1.0))
    lat_dtypes = (F32, F32, BF16, BF16, BF16)
    kv_splits = ((D_NA, 1.0), (D_NA, 1.0))
    b_in3 = b_in[:, None, :]

    xc = ctx
    for l in range(DEPTH):
        last = l == DEPTH - 1
        m_lat = [(mods, (l * N_MOD + k) * MOD_ROWS, True) for k in range(N_MOD)]
        m_ctx = [(mods, (l * N_MOD + k) * MOD_ROWS + bsz, False) for k in range(N_MOD)]
        b_out_l = b_out[l][None, :]
        wr = jnp.pad(w_router[l], ((0, 0), (0, LANES - N_EXPERTS)))
        wr_hi = wr.astype(BF16)
        wr_hl = jnp.concatenate([wr_hi, (wr - wr_hi.astype(F32)).astype(BF16)], axis=1)
        g1, b1 = g_post1[l][None, :], b_post1[l][None, :]
        g2, b2 = g_post2[l][None, :], b_post2[l][None, :]
        conv_w = (w_short[l], w_conf_dw[l], b_conf_dw[l], g_conf_ln[l], b_conf_ln[l])

        flat = lambda t: t.reshape(1, bsz * nctx, t.shape[-1])
        unflat = lambda t: t.reshape(bsz, nctx, t.shape[-1])
        if last:
            k_c, v_c = map(unflat, _inproj(flat(xc), m_ctx[0], m_ctx[1], w_in, b_in3, l, OFF_K,
                                           kv_splits, (BF16, BF16), tm=512))
        else:
            uac, ubc, q_c, k_c, v_c = map(unflat, _inproj(flat(xc), m_ctx[0], m_ctx[1], w_in, b_in3, l, 0,
                                                          lat_splits, lat_dtypes, tm=512))

        ua, ub, q, k, v = _inproj(x, m_lat[0], m_lat[1], w_in, b_in3, l, 0, lat_splits, lat_dtypes, tm=512)
        yab = _conv_mixers(ua, ub, *conv_w)
        yc = _neighbourhood_attention(q, k, v, k_c, v_c, na_rpb[l])
        x_mid, hm, logits = _outproj(yab, yc, x, w_out, l, b_out_l, m_lat[2], g1, b1, m_lat[3], m_lat[4],
                                     wr_hi, wr_hl, tm=512)

        idx, gates = _slot_lists(*_route(logits, cap, 0), cap, merge=False)
        idx = idx.reshape(-1)
        x_e = _dispatch(idx, hm, bsz, seq, d, cap)

        x_ec = gates_c = None
        if not last:
            yabc = _conv_mixers(uac, ubc, *conv_w)
            ycc = _context_attention(q_c, k_c, v_c)
            xc_mid, hmc, logits_c = _outproj(flat(yabc), flat(ycc), flat(xc), w_out, l, b_out_l, m_ctx[2], g1, b1,
                                             m_ctx[3], m_ctx[4], wr_hi, wr_hl, tm=512)
            n_c = bsz * cap_ctx
            idx_c, gates_c = _slot_lists(*_route(unflat(logits_c), cap_ctx, cap_ctx), n_c, merge=True)
            idx_c = idx_c.reshape(-1)
            x_ec = _dispatch(idx_c, hmc, 1, bsz * nctx, d, n_c)

        y_e, y_ec = _experts(x_e, gates, x_ec, gates_c, w_gate, w_up, w_down, l, tm=512)
        x = _combine(idx, y_e, x_mid, m_lat[5], g2, b2, cap)
        if not last:
            xc = unflat(_combine(idx_c, y_ec, xc_mid, m_ctx[5], g2, b2, n_c))
    return x
```

```python
import functools
import math

import numpy as np
import jax
import jax.numpy as jnp
from jax import lax
from jax.experimental import pallas as pl
from jax.experimental.pallas import tpu as pltpu

F32 = jnp.float32
BF16 = jnp.bfloat16

D_MODEL = 1024
DEPTH = 2
GRID_W = 64
D_CONV = D_MODEL // 4
D_CONF = D_MODEL // 4
NA_HEAD_DIM = 64
D_NA = D_MODEL - D_CONV - D_CONF
N_NA_HEADS = D_NA // NA_HEAD_DIM
SHORT_CONV_W = 3
CONF_CONV_W = 31
NA_WIN_ROWS_MAX = 8
NA_WIN_COLS = 16
N_EXPERTS = 16
EC_CAPACITY_FACTOR = 2
D_EXPERT = 1024
LN_EPS = 1e-5
DEEPNORM_ALPHA = (2.0 * DEPTH) ** 0.25
NEG_INF = -1e30
LOG2E = math.log2(math.e)

OFF_A = 0
OFF_B = OFF_A + 3 * D_CONV
OFF_Q = OFF_B + 2 * D_CONF
OFF_K = OFF_Q + D_NA
OFF_V = OFF_K + D_NA
D_IN = OFF_V + D_NA

LANES = 128
SUBLANES = 8
MOD_ROWS = 16
VMEM_LIMIT = 56 * 1024 * 1024
ATTN_ROWS = 16
N_PAIRS = D_NA // LANES
ROW_GROUP = 256
OUT_ROW_GROUP = 128
HI = lax.Precision.HIGHEST


def _cparams(sem):
    return pltpu.CompilerParams(dimension_semantics=sem, vmem_limit_bytes=VMEM_LIMIT)


def _ln(x):
    mu = jnp.mean(x, axis=-1, keepdims=True)
    xc = x - mu
    var = jnp.mean(xc * xc, axis=-1, keepdims=True)
    return xc * lax.rsqrt(var + LN_EPS)


def _sigmoid(x):
    return 1.0 / (1.0 + jnp.exp(-x))


def _mod_kernel(cond_ref, w_ref, b_ref, o_ref):
    s = cond_ref[...]
    s = s * _sigmoid(s)
    w = w_ref[0]
    s_hi, w_hi = s.astype(BF16), w.astype(BF16)
    s_lo = (s - s_hi.astype(F32)).astype(BF16)
    w_lo = (w - w_hi.astype(F32)).astype(BF16)
    o_ref[0] = (jnp.dot(s_hi, w_hi, preferred_element_type=F32) + jnp.dot(s_lo, w_hi, preferred_element_type=F32)
                + jnp.dot(s_hi, w_lo, preferred_element_type=F32) + b_ref[0])


N_MOD = 6


def _modulation(cond, w_mod, b_mod):
    n_l, d, n = w_mod.shape
    out = pl.pallas_call(
        _mod_kernel,
        out_shape=jax.ShapeDtypeStruct((n_l * N_MOD, MOD_ROWS, d), F32),
        grid=(n_l, N_MOD),
        in_specs=[
            pl.BlockSpec((MOD_ROWS, d), lambda l, k: (0, 0)),
            pl.BlockSpec((1, d, d), lambda l, k: (l, 0, k)),
            pl.BlockSpec((1, 1, d), lambda l, k: (l, 0, k)),
        ],
        out_specs=pl.BlockSpec((1, MOD_ROWS, d), lambda l, k: (l * N_MOD + k, 0, 0)),
        compiler_params=_cparams(("arbitrary", "arbitrary")),
        name="modulation",
    )(cond, w_mod, b_mod.reshape(n_l, 1, n))
    return out.reshape(n_l * N_MOD * MOD_ROWS, 1, d)


def _mod_spec(mod, d):
    _, row0, per_sample = mod
    if per_sample:
        return pl.BlockSpec((1, 1, d), lambda bi, *_: (row0 + bi, 0, 0))
    return pl.BlockSpec((1, 1, d), lambda bi, *_: (row0, 0, 0))


def _first_step():
    return (pl.program_id(0) == 0) & (pl.program_id(1) == 0)


def _inproj_kernel(x_ref, sh_ref, sc_ref, w_ref, b_ref, *rest, splits, col0):
    *o_refs, w_bf = rest

    @pl.when(_first_step())
    def _():
        w_bf[...] = w_ref[0].astype(BF16)

    tm = x_ref.shape[1]
    n_part = max(1, tm // ROW_GROUP)
    rows_per = tm // n_part
    groups = [slice(part_i * rows_per, (part_i + 1) * rows_per) for part_i in range(n_part)]
    hs = [(_ln(x_ref[0, rows, :]) * (1.0 + sc_ref[0]) + sh_ref[0]).astype(BF16) for rows in groups]
    for rows, h in zip(groups, hs):
        u = jnp.dot(h, w_bf[:, col0:], preferred_element_type=F32) + b_ref[0, :, col0:]
        off = 0
        for o_ref, (width, scale) in zip(o_refs, splits):
            part = u[:, off:off + width]
            if scale != 1.0:
                part = part * scale
            o_ref[0, rows, :] = part.astype(o_ref.dtype)
            off += width


def _inproj(x, sh, sc, w, b, layer, col0, splits, dtypes, tm):
    nb, s, d = x.shape
    n = w.shape[2]
    return pl.pallas_call(
        functools.partial(_inproj_kernel, splits=splits, col0=col0),
        out_shape=[jax.ShapeDtypeStruct((nb, s, wd), dt) for (wd, _), dt in zip(splits, dtypes)],
        grid=(nb, s // tm),
        in_specs=[
            pl.BlockSpec((1, tm, d), lambda bi, i: (bi, i, 0)),
            _mod_spec(sh, d),
            _mod_spec(sc, d),
            pl.BlockSpec((1, d, n), lambda bi, i: (layer, 0, 0), pipeline_mode=pl.Buffered(1)),
            pl.BlockSpec((1, 1, n), lambda bi, i: (layer, 0, 0)),
        ],
        out_specs=[pl.BlockSpec((1, tm, wd), lambda bi, i: (bi, i, 0)) for wd, _ in splits],
        scratch_shapes=[pltpu.VMEM((d, n), BF16)],
        compiler_params=_cparams(("arbitrary", "arbitrary")),
        name="inproj",
    )(x, sh[0], sc[0], w, b)


CONV_CHUNK = 128
Z_PAD = 8
H_PAD = 16


def _conv_kernel(ua_ref, ub_ref, ws_ref, wd_ref, bd_ref, g_ref, b_ref, o_ref, z_scr, h_scr, *, seq):
    c = D_CONV
    z_scr[0:Z_PAD, :] = jnp.zeros((Z_PAD, c), F32)
    z_scr[Z_PAD + seq:2 * Z_PAD + seq, :] = jnp.zeros((Z_PAD, c), F32)
    h_scr[0, 0:H_PAD, :] = jnp.zeros((H_PAD, c), F32)
    h_scr[0, H_PAD + seq:2 * H_PAD + seq, :] = jnp.zeros((H_PAD, c), F32)
    z_scr[Z_PAD:Z_PAD + seq, :] = ua_ref[0, :, c:2 * c] * ua_ref[0, :, 2 * c:3 * c]
    h_scr[0, H_PAD:H_PAD + seq, :] = ub_ref[0, :, 0:c] * _sigmoid(ub_ref[0, :, c:2 * c])
    n_rows = seq + 2 * H_PAD
    h_all = h_scr[0]
    for r in range(1, SUBLANES):
        h_scr[r] = pltpu.roll(h_all, n_rows - r, axis=0)
    tc = min(CONV_CHUNK, seq)
    for ci in range(seq // tc):
        t0 = ci * tc
        acc = ws_ref[0:1, :] * z_scr[t0 + Z_PAD - 1:t0 + Z_PAD - 1 + tc, :]
        for j in range(1, SHORT_CONV_W):
            s0 = t0 + Z_PAD - 1 + j
            acc = acc + ws_ref[j:j + 1, :] * z_scr[s0:s0 + tc, :]
        ya = ua_ref[0, t0:t0 + tc, 0:c] * acc
        hb = bd_ref[...]
        for j in range(CONF_CONV_W):
            s0 = t0 + H_PAD - CONF_CONV_W // 2 + j
            a0 = s0 - s0 % SUBLANES
            hb = hb + wd_ref[j:j + 1, :] * h_scr[s0 % SUBLANES, a0:a0 + tc, :]
        hn = _ln(hb) * g_ref[...] + b_ref[...]
        yb = hn * _sigmoid(hn)
        o_ref[0, t0:t0 + tc, 0:c] = ya.astype(o_ref.dtype)
        o_ref[0, t0:t0 + tc, c:2 * c] = yb.astype(o_ref.dtype)


def _conv_mixers(ua, ub, w_short, w_dw, b_dw, g_ln, b_ln):
    nb, s, _ = ua.shape
    c = D_CONV
    full = lambda shape: pl.BlockSpec(shape, lambda bi: (0,) * len(shape))
    return pl.pallas_call(
        functools.partial(_conv_kernel, seq=s),
        out_shape=jax.ShapeDtypeStruct((nb, s, 2 * c), BF16),
        grid=(nb,),
        in_specs=[
            pl.BlockSpec((1, s, 3 * c), lambda bi: (bi, 0, 0)),
            pl.BlockSpec((1, s, 2 * c), lambda bi: (bi, 0, 0)),
            full((SHORT_CONV_W, c)), full((CONF_CONV_W, c)), full((1, c)), full((1, c)), full((1, c)),
        ],
        out_specs=pl.BlockSpec((1, s, 2 * c), lambda bi: (bi, 0, 0)),
        scratch_shapes=[pltpu.VMEM((s + 2 * Z_PAD, c), F32), pltpu.VMEM((SUBLANES, s + 2 * H_PAD, c), F32)],
        compiler_params=_cparams(("arbitrary",)),
        name="conv_mixers",
    )(ua, ub, w_short, w_dw, b_dw.reshape(1, c), g_ln.reshape(1, c), b_ln.reshape(1, c))


SUB_ROWS = 2
WIN_ROWS = SUB_ROWS + NA_WIN_ROWS_MAX - 1
BAND_KEYS = WIN_ROWS * GRID_W
SUB_Q = SUB_ROWS * GRID_W
N_DROW = 2 * NA_WIN_ROWS_MAX - 1
N_DCOL = 2 * NA_WIN_COLS - 1


def _bias_kernel(rpb_ref, nxt_ref, o_ref):
    n_cols = o_ref.shape[1]
    col = lax.broadcasted_iota(jnp.int32, (LANES, n_cols), 1)
    qc = col >> (LANES.bit_length() - 1)
    kc = col & (GRID_W - 1)
    d_col = jnp.clip(kc - qc + (NA_WIN_COLS - 1), 0, N_DCOL - 1)
    onehot = (lax.broadcasted_iota(jnp.int32, (LANES, n_cols), 0) == d_col).astype(F32)
    own = jnp.dot(rpb_ref[...], onehot, preferred_element_type=F32, precision=HI) * LOG2E
    nxt = jnp.dot(nxt_ref[...], onehot, preferred_element_type=F32, precision=HI) * LOG2E
    c0 = jnp.clip(qc - NA_WIN_COLS // 2, 0, GRID_W - NA_WIN_COLS)
    inside = (kc >= c0) & (kc < c0 + NA_WIN_COLS)
    second = (col & GRID_W) != 0
    o_ref[0:LANES, :] = jnp.where(inside, jnp.where(second, nxt, own), NEG_INF)
    o_ref[LANES:2 * LANES, :] = jnp.where(inside & ~second, own, NEG_INF)
    o_ref[2 * LANES:3 * LANES, :] = jnp.where(inside & second, own, NEG_INF)
    o_ref[3 * LANES:, :] = jnp.full((o_ref.shape[0] - 3 * LANES, n_cols), NEG_INF, F32)


N_SLABS = -(-WIN_ROWS // 2)
TILES_PER_KIND = LANES
MASKED_TILE = 3 * TILES_PER_KIND


def _na_plan(rows):
    wr = min(NA_WIN_ROWS_MAX, rows)
    assert wr == NA_WIN_ROWS_MAX and rows % SUB_ROWS == 0 and rows >= WIN_ROWS
    row_start = np.clip(np.arange(rows) - wr // 2, 0, rows - wr)
    w0s, tiles = [], []
    for r0 in range(0, rows, SUB_ROWS):
        w0 = int(np.clip(r0 - wr // 2, 0, rows - WIN_ROWS))
        for iq in range(SUB_ROWS):
            r = r0 + iq
            assert row_start[r] >= w0 and row_start[r] + wr <= w0 + WIN_ROWS
            ok = lambda w: w < WIN_ROWS and row_start[r] <= w0 + w < row_start[r] + wr
            d_row = lambda w: w0 + w - r + NA_WIN_ROWS_MAX - 1
            for j in range(N_SLABS):
                lo, hi = ok(2 * j), ok(2 * j + 1)
                if lo and hi:
                    tiles.append(d_row(2 * j))
                elif lo:
                    tiles.append(TILES_PER_KIND + d_row(2 * j))
                elif hi:
                    tiles.append(2 * TILES_PER_KIND + d_row(2 * j + 1))
                else:
                    tiles.append(-1)
        w0s.append(w0)
    return np.array(w0s, np.int32), np.array(tiles, np.int32)


def _na_bias(rpb):
    n_hd = N_NA_HEADS * N_DROW
    assert GRID_W == 64 and 2 * GRID_W == LANES and n_hd <= TILES_PER_KIND and N_DCOL <= LANES
    rpb = rpb.astype(F32)
    pad = lambda t: jnp.pad(t.reshape(n_hd, N_DCOL), ((0, LANES - n_hd), (0, LANES - N_DCOL)))
    nxt = jnp.concatenate([rpb[:, 1:], jnp.zeros_like(rpb[:, :1])], axis=1)
    n_tiles = 3 * TILES_PER_KIND + SUBLANES
    table = pl.pallas_call(
        _bias_kernel,
        out_shape=jax.ShapeDtypeStruct((n_tiles, GRID_W * LANES), F32),
        compiler_params=pltpu.CompilerParams(vmem_limit_bytes=VMEM_LIMIT),
        name="na_bias",
    )(pad(rpb), pad(nxt))
    return table.reshape(n_tiles, GRID_W, LANES)


def _lane_reduce(xs, combine, reduce, neutral):
    chunks = []
    for x in xs:
        rows, n = x.shape
        n_full = n // LANES
        chunks += [x[:, j * LANES:(j + 1) * LANES] for j in range(n_full)]
        if n % LANES:
            fill = jnp.full((rows, LANES - n % LANES), neutral, x.dtype)
            chunks.append(jnp.concatenate([x[:, n_full * LANES:], fill], axis=1))
    return reduce(functools.reduce(combine, chunks), axis=-1, keepdims=True)


def _attn_kernel(w0_ref, tile_ref, q_ref, k_ref, v_ref, kc_ref, vc_ref, *rest, banded, n_sub, sub_q):
    if banded:
        bias_ref, o_ref = rest
    else:
        (o_ref,) = rest
    lane = lax.broadcasted_iota(jnp.int32, (sub_q, LANES), 1)
    first = lane < NA_HEAD_DIM
    nt = (((1,), (1,)), ((), ()))
    stages = [(si, p) for si in range(n_sub) for p in range(N_PAIRS)]

    def window(si):
        blk = pl.program_id(1) * n_sub + si
        return blk, pl.multiple_of(w0_ref[blk] * GRID_W, GRID_W)

    def bias(blk, head):
        row_blocks = []
        for iq in range(SUB_ROWS):
            slabs = []
            for j in range(N_SLABS):
                t = tile_ref[(blk * SUB_ROWS + iq) * N_SLABS + j]
                tile = bias_ref[jnp.where(t < 0, MASKED_TILE, t + head * N_DROW)]
                width = min(LANES, BAND_KEYS - j * LANES)
                slabs.append(tile[:, :width])
            row_blocks.append(jnp.concatenate(slabs, axis=1))
        return jnp.concatenate(row_blocks, axis=0)

    def scores(si, p):
        cols = slice(p * LANES, (p + 1) * LANES)
        q_p = q_ref[0, si * sub_q:(si + 1) * sub_q, cols]
        zero = jnp.zeros_like(q_p)
        qq = jnp.concatenate([jnp.where(first, q_p, zero), jnp.where(first, zero, q_p)], axis=0)
        parts = [lax.dot_general(qq, kc_ref[0, :, cols], nt, preferred_element_type=F32)]
        if banded:
            blk, start = window(si)
            both = jnp.concatenate([bias(blk, 2 * p), bias(blk, 2 * p + 1)], axis=0)
            parts.append(lax.dot_general(qq, k_ref[0, pl.ds(start, BAND_KEYS), cols], nt,
                                         preferred_element_type=F32) + both)
        return parts

    def weights(parts):
        m = _lane_reduce(parts, jnp.maximum, jnp.max, NEG_INF)
        es = [jnp.exp2(s - m) for s in parts]
        den = _lane_reduce(es, jnp.add, jnp.sum, 0.0)
        return [e.astype(BF16) for e in es], den

    def values(si, p, es, den):
        cols = slice(p * LANES, (p + 1) * LANES)
        o = jnp.dot(es[0], vc_ref[0, :, cols], preferred_element_type=F32)
        if banded:
            _, start = window(si)
            o = o + jnp.dot(es[1], v_ref[0, pl.ds(start, BAND_KEYS), cols], preferred_element_type=F32)
        o = o * (1.0 / den)
        out = jnp.where(first, o[:sub_q], o[sub_q:])
        o_ref[0, si * sub_q:(si + 1) * sub_q, cols] = out.astype(o_ref.dtype)

    nxt = scores(*stages[0])
    pending = None
    for i, (si, p) in enumerate(stages):
        cur = nxt
        if i + 1 < len(stages):
            nxt = scores(*stages[i + 1])
        if pending is not None:
            values(*pending)
        es, den = weights(cur)
        pending = (si, p, es, den)
    values(*pending)


def _neighbourhood_attention(q, k, v, kc, vc, rpb):
    nb, s, dn = q.shape
    rows = s // GRID_W
    nctx = kc.shape[1]
    w0s, tiles = _na_plan(rows)
    bias = _na_bias(rpb)
    n_sub = ATTN_ROWS // SUB_ROWS
    m_rows = ATTN_ROWS * GRID_W
    grid_spec = pltpu.PrefetchScalarGridSpec(
        num_scalar_prefetch=2,
        grid=(nb, rows // ATTN_ROWS),
        in_specs=[
            pl.BlockSpec((1, m_rows, dn), lambda bi, i, w0, pat: (bi, i, 0)),
            pl.BlockSpec((1, s, dn), lambda bi, i, w0, pat: (bi, 0, 0)),
            pl.BlockSpec((1, s, dn), lambda bi, i, w0, pat: (bi, 0, 0)),
            pl.BlockSpec((1, nctx, dn), lambda bi, i, w0, pat: (bi, 0, 0)),
            pl.BlockSpec((1, nctx, dn), lambda bi, i, w0, pat: (bi, 0, 0)),
            pl.BlockSpec(bias.shape, lambda bi, i, w0, pat: (0, 0, 0), pipeline_mode=pl.Buffered(1)),
        ],
        out_specs=pl.BlockSpec((1, m_rows, dn), lambda bi, i, w0, pat: (bi, i, 0)),
    )
    return pl.pallas_call(
        functools.partial(_attn_kernel, banded=True, n_sub=n_sub, sub_q=SUB_Q),
        out_shape=jax.ShapeDtypeStruct((nb, s, dn), BF16),
        grid_spec=grid_spec,
        compiler_params=_cparams(("arbitrary", "arbitrary")),
        name="neighbourhood_attention",
    )(jnp.asarray(w0s), jnp.asarray(tiles), q, k, v, kc, vc, bias)


def _context_attention(q, kc, vc):
    nb, s, dn = q.shape
    spec = pl.BlockSpec((1, s, dn), lambda bi, i, w0, pat: (bi, 0, 0))
    grid_spec = pltpu.PrefetchScalarGridSpec(
        num_scalar_prefetch=2, grid=(nb, 1), in_specs=[spec] * 5, out_specs=spec)
    dummy = jnp.zeros((1,), jnp.int32)
    return pl.pallas_call(
        functools.partial(_attn_kernel, banded=False, n_sub=1, sub_q=s),
        out_shape=jax.ShapeDtypeStruct((nb, s, dn), BF16),
        grid_spec=grid_spec,
        compiler_params=_cparams(("arbitrary", "arbitrary")),
        name="context_attention",
    )(dummy, dummy, q, kc, vc, kc, vc)


def _store_rowmajor(ref, val, base=0):
    n, width = val.shape
    chunks = width // LANES
    for c in range(chunks):
        ref[pl.ds(base + c, n, stride=chunks), :] = val[:, c * LANES:(c + 1) * LANES]


def _load_rowmajor(ref, base, n, chunks):
    return jnp.concatenate([ref[pl.ds(base + c, n, stride=chunks), :] for c in range(chunks)], axis=1)


def _outproj_kernel(yab_ref, yc_ref, x_ref, w_ref, bo_ref, gt_ref, g_ref, b_ref, sh_ref, sc_ref,
                    wrh_ref, wrhl_ref, xmid_ref, hm_ref, lg_ref, w_bf):
    @pl.when(_first_step())
    def _():
        w_bf[...] = w_ref[0].astype(BF16)

    half = yab_ref.shape[2]
    tm = x_ref.shape[1]
    chunks = x_ref.shape[2] // LANES
    n_part = max(1, tm // OUT_ROW_GROUP)
    rows_per = tm // n_part
    groups = [slice(part * rows_per, (part + 1) * rows_per) for part in range(n_part)]
    ys = [jnp.dot(yab_ref[0, rows, :], w_bf[0:half, :], preferred_element_type=F32)
          + jnp.dot(yc_ref[0, rows, :], w_bf[half:, :], preferred_element_type=F32) + bo_ref[...]
          for rows in groups]
    for part, (rows, y) in enumerate(zip(groups, ys)):
        xm = _ln(DEEPNORM_ALPHA * x_ref[0, rows, :] + (1.0 + gt_ref[0]) * y) * g_ref[...] + b_ref[...]
        xmid_ref[0, rows, :] = xm
        hm = _ln(xm) * (1.0 + sc_ref[0]) + sh_ref[0]
        _store_rowmajor(hm_ref, hm, base=part * rows_per * chunks)
        hm_hi = hm.astype(BF16)
        hm_lo = (hm - hm_hi.astype(F32)).astype(BF16)
        both = jnp.dot(hm_hi, wrhl_ref[...], preferred_element_type=F32)
        lg_ref[0, rows, :] = (both[:, :LANES] + both[:, LANES:]
                              + jnp.dot(hm_lo, wrh_ref[...], preferred_element_type=F32))


def _outproj(yab, yc, x, w, layer, bo, gt, g, b, sh, sc, wr_hi, wr_hl, tm):
    nb, s, d = x.shape
    half = yab.shape[2]
    vec = pl.BlockSpec((1, d), lambda bi, i: (0, 0))
    tok = lambda width: pl.BlockSpec((1, tm, width), lambda bi, i: (bi, i, 0))
    n_i = s // tm
    return pl.pallas_call(
        _outproj_kernel,
        out_shape=[jax.ShapeDtypeStruct((nb, s, d), F32),
                   jax.ShapeDtypeStruct((nb * s * (d // LANES), LANES), F32),
                   jax.ShapeDtypeStruct((nb, s, LANES), F32)],
        grid=(nb, n_i),
        in_specs=[tok(half), tok(half), tok(d),
                  pl.BlockSpec((1, d, d), lambda bi, i: (layer, 0, 0), pipeline_mode=pl.Buffered(1)),
                  vec, _mod_spec(gt, d), vec, vec,
                  _mod_spec(sh, d), _mod_spec(sc, d), pl.BlockSpec((d, LANES), lambda bi, i: (0, 0)),
                  pl.BlockSpec((d, 2 * LANES), lambda bi, i: (0, 0))],
        out_specs=[tok(d), pl.BlockSpec((tm * (d // LANES), LANES), lambda bi, i: (bi * n_i + i, 0)),
                   tok(LANES)],
        scratch_shapes=[pltpu.VMEM((d, d), BF16)],
        compiler_params=_cparams(("arbitrary", "arbitrary")),
        name="outproj_postnorm",
    )(yab, yc, x, w, bo, gt[0], g, b, sh[0], sc[0], wr_hi, wr_hl)


CUM_CHUNK = 256
F32_EXP_BIAS = 127
F32_MANT_BITS = 23


def _prefix_count(mask_f32, tri):
    rows, n = mask_f32.shape
    tc = min(CUM_CHUNK, n)
    base = jnp.zeros((rows, 1), F32)
    parts = []
    for ci in range(n // tc):
        blk = mask_f32[:, ci * tc:(ci + 1) * tc]
        parts.append(jnp.dot(blk.astype(BF16), tri[:tc, :tc], preferred_element_type=F32) + base)
        base = base + jnp.sum(blk, axis=-1, keepdims=True)
    return jnp.concatenate(parts, axis=-1)


def _pow2(k):
    return pltpu.bitcast((k + F32_EXP_BIAS) << F32_MANT_BITS, F32)


def _route_kernel(lg_ref, slot_c_ref, gate_t_ref, *, cap, slot_stride):
    nb = lg_ref.shape[0]
    assert nb * N_EXPERTS == LANES
    rows = []
    for b in range(nb):
        lg = lg_ref[b]
        lane = lax.broadcasted_iota(jnp.int32, lg.shape, 1)
        lgm = jnp.where(lane < N_EXPERTS, lg, NEG_INF)
        ex = jnp.exp(lgm - jnp.max(lgm, axis=-1, keepdims=True))
        aff = ex / jnp.sum(ex, axis=-1, keepdims=True)
        rows.append(aff.T[0:N_EXPERTS, :])
    a = jnp.concatenate(rows, axis=0)
    capf = float(cap)

    def enough(t):
        return jnp.sum((a >= t).astype(F32), axis=-1, keepdims=True) >= capf

    def exp_step(_, carry):
        lo, hi = carry
        mid = lo + ((hi - lo + 1) >> 1)
        ok = enough(_pow2(mid))
        return jnp.where(ok, mid, lo), jnp.where(ok, hi, mid - 1)

    k_lo = jnp.full((LANES, 1), -F32_EXP_BIAS, jnp.int32)
    k_hi = jnp.zeros((LANES, 1), jnp.int32)
    k_lo, _ = lax.fori_loop(0, 7, exp_step, (k_lo, k_hi))
    base = _pow2(k_lo)

    def mant_step(_, carry):
        t, step = carry
        step = step * 0.5
        cand = t + step
        return jnp.where(enough(cand), cand, t), step

    thr, _ = lax.fori_loop(0, F32_MANT_BITS, mant_step, (base, base))

    r_i = lax.broadcasted_iota(jnp.int32, (CUM_CHUNK, CUM_CHUNK), 0)
    c_i = lax.broadcasted_iota(jnp.int32, (CUM_CHUNK, CUM_CHUNK), 1)
    tri = (r_i < c_i).astype(BF16)
    gt = (a > thr).astype(F32)
    eq = (a == thr).astype(F32)
    need = capf - jnp.sum(gt, axis=-1, keepdims=True)
    sel = gt + eq * (_prefix_count(eq, tri) < need).astype(F32)
    pos = _prefix_count(sel, tri)
    sample = lax.broadcasted_iota(jnp.int32, (LANES, 1), 0) >> (N_EXPERTS.bit_length() - 1)
    slot = jnp.where(sel > 0.0, pos + (sample * slot_stride).astype(F32), -1.0)
    for b in range(nb):
        lo = b * N_EXPERTS
        gate_t_ref[b] = a[lo:lo + N_EXPERTS, :]
        rolled = slot if b == 0 else jnp.concatenate([slot[lo:, :], slot[:lo, :]], axis=0)
        slot_c_ref[b] = rolled.T


def _route(logits, cap, slot_stride):
    nb, s, _ = logits.shape
    whole = lambda shape: pl.BlockSpec(shape, lambda i: (0,) * len(shape))
    return pl.pallas_call(
        functools.partial(_route_kernel, cap=cap, slot_stride=slot_stride),
        out_shape=[jax.ShapeDtypeStruct((nb, s, LANES), F32), jax.ShapeDtypeStruct((nb, N_EXPERTS, s), F32)],
        grid=(1,),
        in_specs=[whole((nb, s, LANES))],
        out_specs=[whole((nb, s, LANES)), whole((nb, N_EXPERTS, s))],
        compiler_params=_cparams(("arbitrary",)),
        name="route",
    )(logits)


TOK_SPLIT = 64


def _slot_list_kernel(slot_ref, gate_ref, idx_ref, g_ref, *, n_slots, tok_stride, merge):
    s = slot_ref.shape[1]
    b = pl.program_id(0)
    assert n_slots <= 256
    slot_id = lax.broadcasted_iota(jnp.int32, (s, n_slots), 1).astype(F32).astype(BF16)
    one, zero = jnp.ones((s, n_slots), BF16), jnp.zeros((s, n_slots), BF16)
    tok = lax.broadcasted_iota(jnp.int32, (1, s), 1) + b * tok_stride
    tok_hi = (tok >> (TOK_SPLIT.bit_length() - 1)).astype(F32)
    tok_lo = (tok & (TOK_SPLIT - 1)).astype(F32)
    zeros = jnp.zeros((SUBLANES - 5, s), F32)
    idx_rows, g_rows = [], []
    for e in range(N_EXPERTS):
        taken = jnp.broadcast_to(slot_ref[0, :, e:e + 1].astype(BF16), (s, n_slots))
        hit = jnp.where(taken == slot_id, one, zero)
        g0 = gate_ref[0, e:e + 1, :]
        g_hi = g0.astype(BF16).astype(F32)
        g_mid = (g0 - g_hi).astype(BF16).astype(F32)
        g_lo = g0 - g_hi - g_mid
        lhs = jnp.concatenate([tok_hi, tok_lo, g_hi, g_mid, g_lo, zeros], axis=0).astype(BF16)
        out = jnp.dot(lhs, hit, preferred_element_type=F32)
        idx_rows.append(out[0:1] * float(TOK_SPLIT) + out[1:2])
        g_rows.append(out[2:3] + out[3:4] + out[4:5])
    idx = jnp.concatenate(idx_rows, axis=0).astype(jnp.int32) * SUBLANES
    g = jnp.concatenate(g_rows + [jnp.zeros((LANES - N_EXPERTS, n_slots), F32)], axis=0)
    g_t = g.T
    g_cols = [jnp.broadcast_to(g_t[:, e:e + 1], (n_slots, LANES)) for e in range(N_EXPERTS)]
    if merge:
        @pl.when(b == 0)
        def _():
            idx_ref[0] = idx
            for e in range(N_EXPERTS):
                g_ref[e] = g_cols[e]

        @pl.when(b > 0)
        def _():
            idx_ref[0] = idx_ref[0] + idx
            for e in range(N_EXPERTS):
                g_ref[e] = g_ref[e] + g_cols[e]
    else:
        idx_ref[0] = idx
        for e in range(N_EXPERTS):
            g_ref[e] = g_cols[e]


def _slot_lists(slot_c, gate_t, n_slots, merge):
    nb, s, _ = slot_c.shape
    nbo = 1 if merge else nb
    idx_map = (lambda bi: (0, 0, 0)) if merge else (lambda bi: (bi, 0, 0))
    g_map = (lambda bi: (0, 0, 0)) if merge else (lambda bi: (0, bi, 0))
    return pl.pallas_call(
        functools.partial(_slot_list_kernel, n_slots=n_slots, tok_stride=s if merge else 0, merge=merge),
        out_shape=[jax.ShapeDtypeStruct((nbo, N_EXPERTS, n_slots), jnp.int32),
                   jax.ShapeDtypeStruct((N_EXPERTS, nbo * n_slots, LANES), F32)],
        grid=(nb,),
        in_specs=[pl.BlockSpec((1, s, LANES), lambda bi: (bi, 0, 0)),
                  pl.BlockSpec((1, N_EXPERTS, s), lambda bi: (bi, 0, 0))],
        out_specs=[pl.BlockSpec((1, N_EXPERTS, n_slots), idx_map),
                   pl.BlockSpec((N_EXPERTS, n_slots, LANES), g_map)],
        compiler_params=_cparams(("arbitrary",)),
        name="slot_lists",
    )(slot_c, gate_t)


ROW_UNROLL = 32


def _dispatch_kernel(idx_ref, hm_ref, x_ref, rows_scr, *, n_slots):
    b = pl.program_id(0)
    chunks = x_ref.shape[2] // LANES

    def emit(e, s0):
        src = pl.multiple_of(s0 * chunks, ROW_UNROLL * chunks)
        rows = _load_rowmajor(rows_scr.at[e % 2], src, ROW_UNROLL, chunks)
        x_ref[e, pl.ds(pl.multiple_of(s0, ROW_UNROLL), ROW_UNROLL), :] = rows.astype(x_ref.dtype)

    for e in range(N_EXPERTS + 1):
        base = (b * N_EXPERTS + e) * n_slots

        def move(i, carry):
            s0 = i * ROW_UNROLL
            if e < N_EXPERTS:
                first = base + s0
                tiles = [hm_ref[pl.ds(pl.multiple_of(idx_ref[first + u], chunks), chunks), :]
                         for u in range(ROW_UNROLL)]
                dst = pl.multiple_of(s0 * chunks, ROW_UNROLL * chunks)
                rows_scr[e % 2, pl.ds(dst, ROW_UNROLL * chunks), :] = jnp.concatenate(tiles, axis=0)
            if e > 0:
                emit(e - 1, s0)
            return carry

        lax.fori_loop(0, n_slots // ROW_UNROLL, move, 0)


def _dispatch(idx, hm_rm, nb, s, d, n_slots):
    chunks = d // LANES
    assert chunks == SUBLANES and n_slots % ROW_UNROLL == 0
    grid_spec = pltpu.PrefetchScalarGridSpec(
        num_scalar_prefetch=1,
        grid=(nb,),
        in_specs=[pl.BlockSpec((s * chunks, LANES), lambda bi, idx_r: (bi, 0))],
        out_specs=pl.BlockSpec((N_EXPERTS, n_slots, d), lambda bi, idx_r: (0, bi, 0)),
        scratch_shapes=[pltpu.VMEM((2, n_slots * chunks, LANES), F32)],
    )
    return pl.pallas_call(
        functools.partial(_dispatch_kernel, n_slots=n_slots),
        out_shape=jax.ShapeDtypeStruct((N_EXPERTS, nb * n_slots, d), BF16),
        grid_spec=grid_spec,
        compiler_params=_cparams(("arbitrary",)),
        name="dispatch",
    )(idx, hm_rm)


W_PARTS = 4

def _expert_kernel(*refs, n_main, has_extra):
    n_in = 4 if has_extra else 2
    n_out = 2 if has_extra else 1
    acts, w_parts = refs[:n_in], refs[n_in:n_in + 3]
    outs = refs[n_in + 3:n_in + 3 + n_out]
    w_bfs = refs[n_in + 3 + n_out:]
    if has_extra:
        x_ref, g_ref, xx_ref, gg_ref = acts
        y_ref, yy_ref = outs
    else:
        x_ref, g_ref = acts
        (y_ref,) = outs
    s = pl.program_id(0)
    i = pl.program_id(1)
    fill = s % 2
    cur = 1 - fill
    wg_bf, wu_bf, wd_bf = (w_bf.at[cur] for w_bf in w_bfs)

    def cast_part():
        block = jnp.minimum(i, W_PARTS - 1)
        for part, w_bf in zip(w_parts, w_bfs):
            rows = part.shape[2]
            w_bf[fill, pl.ds(pl.multiple_of(block * rows, rows), rows), :] = part[0, 0].astype(BF16)

    def ffn(x_in, gate_in, out_ref):
        rows_all = x_in.shape[1]
        chunks = x_in.shape[2] // LANES
        n_part = max(1, rows_all // ROW_GROUP)
        rows_per = rows_all // n_part
        groups = [slice(p * rows_per, (p + 1) * rows_per) for p in range(n_part)]
        ups = []
        for rows in groups:
            x_e = x_in[0, rows, :]
            ups.append((jnp.dot(x_e, wg_bf[...], preferred_element_type=F32),
                        jnp.dot(x_e, wu_bf[...], preferred_element_type=F32)))
        for p, (rows, (a, u)) in enumerate(zip(groups, ups)):
            h = (a * _sigmoid(a) * u).astype(BF16)
            y = jnp.dot(h, wd_bf[...], preferred_element_type=F32) * gate_in[0, rows, 0:1]
            _store_rowmajor(out_ref, y, base=p * rows_per * chunks)

    @pl.when(s == 0)
    def _():
        cast_part()

    if has_extra:
        @pl.when((s > 0) & (i < n_main))
        def _():
            cast_part()
            ffn(x_ref, g_ref, y_ref.at[0])

        @pl.when((s > 0) & (i == n_main))
        def _():
            cast_part()
            ffn(xx_ref, gg_ref, yy_ref.at[0])
    else:
        @pl.when(s > 0)
        def _():
            cast_part()
            ffn(x_ref, g_ref, y_ref.at[0])


def _experts(x_e, g_e, x_extra, g_extra, wg, wu, wd, layer, tm):
    _, rows, d = x_e.shape
    f = wg.shape[3]
    chunks = d // LANES
    n_main = rows // tm
    has_extra = x_extra is not None
    assert n_main >= W_PARTS and d % W_PARTS == 0 and f % W_PARTS == 0
    expert = lambda s: jnp.maximum(s - 1, 0)
    main_map = lambda s, i: (expert(s), jnp.where(s == 0, 0, jnp.minimum(i, n_main - 1)), 0)
    extra_map = lambda s, i: (expert(s), 0, 0)

    def w_spec(shape):
        blk = (1, 1, shape[0] // W_PARTS, shape[1])
        return pl.BlockSpec(blk, lambda s, i: (layer, jnp.minimum(s, N_EXPERTS - 1),
                                               jnp.minimum(i, W_PARTS - 1), 0))

    in_specs = [pl.BlockSpec((1, tm, d), main_map), pl.BlockSpec((1, tm, LANES), main_map)]
    out_specs = [pl.BlockSpec((1, tm * chunks, LANES), main_map)]
    out_shape = [jax.ShapeDtypeStruct((N_EXPERTS, rows * chunks, LANES), F32)]
    args = [x_e, g_e]
    if has_extra:
        rows2 = x_extra.shape[1]
        in_specs += [pl.BlockSpec((1, rows2, d), extra_map), pl.BlockSpec((1, rows2, LANES), extra_map)]
        out_specs.append(pl.BlockSpec((1, rows2 * chunks, LANES), extra_map))
        out_shape.append(jax.ShapeDtypeStruct((N_EXPERTS, rows2 * chunks, LANES), F32))
        args += [x_extra, g_extra]
    outs = pl.pallas_call(
        functools.partial(_expert_kernel, n_main=n_main, has_extra=has_extra),
        out_shape=out_shape,
        grid=(N_EXPERTS + 1, n_main + int(has_extra)),
        in_specs=in_specs + [w_spec((d, f)), w_spec((d, f)), w_spec((f, d))],
        out_specs=out_specs,
        scratch_shapes=[pltpu.VMEM((2, d, f), BF16), pltpu.VMEM((2, d, f), BF16), pltpu.VMEM((2, f, d), BF16)],
        compiler_params=_cparams(("arbitrary", "arbitrary")),
        name="experts",
    )(*args, wg, wu, wd)
    return outs if has_extra else (outs[0], None)


COMBINE_EXPERTS = 4
ADD_UNROLL = 16


def _combine_kernel(idx_ref, y_ref, x_ref, gt_ref, g_ref, b_ref, o_ref, acc, *, n_slots, n_samples):
    s = pl.program_id(0)
    j = pl.program_id(1)
    d = x_ref.shape[2]
    chunks = d // LANES
    per_step = x_ref.shape[1]
    fill = s % 2
    done = 1 - fill
    base = (s * N_EXPERTS + j * COMBINE_EXPERTS) * n_slots

    part = per_step // COMBINE_EXPERTS

    def scatter(eg):
        for g in range(n_slots // ADD_UNROLL):
            first = base + eg * n_slots + g * ADD_UNROLL
            y_rows = y_ref[eg, g * ADD_UNROLL * chunks:(g + 1) * ADD_UNROLL * chunks, :]
            new = []
            for u in range(ADD_UNROLL):
                dst = pl.multiple_of(idx_ref[first + u], chunks)
                new.append((dst, acc[fill, pl.ds(dst, chunks), :] + y_rows[u * chunks:(u + 1) * chunks, :]))
            for dst, val in new:
                acc[fill, pl.ds(dst, chunks), :] = val

    def summed(eg):
        row0 = pl.multiple_of((j * per_step + eg * part) * chunks, part * chunks)
        return _load_rowmajor(acc.at[done], row0, part, chunks)

    def post_norm(eg, ym):
        rows = slice(eg * part, (eg + 1) * part)
        z = DEEPNORM_ALPHA * x_ref[0, rows, :] + (1.0 + gt_ref[0]) * ym
        o_ref[0, rows, :] = _ln(z) * g_ref[...] + b_ref[...]

    @pl.when((s < n_samples) & (j == 0))
    def _():
        acc[fill] = jnp.zeros(acc.shape[1:], F32)

    @pl.when(s == 0)
    def _():
        for eg in range(COMBINE_EXPERTS):
            scatter(eg)

    @pl.when((s > 0) & (s < n_samples))
    def _():
        for eg in range(COMBINE_EXPERTS):
            ym = summed(eg)
            scatter(eg)
            post_norm(eg, ym)

    @pl.when(s == n_samples)
    def _():
        for eg in range(COMBINE_EXPERTS):
            post_norm(eg, summed(eg))


def _combine(idx, y_rm, x_mid, gt, g, b, n_slots):
    nb, s, d = x_mid.shape
    chunks = d // LANES
    n_groups = N_EXPERTS // COMBINE_EXPERTS
    per_step = s // n_groups
    assert chunks == SUBLANES and N_EXPERTS % COMBINE_EXPERTS == 0 and s % n_groups == 0
    assert per_step % SUBLANES == 0 and n_slots % ADD_UNROLL == 0
    prev = lambda si: jnp.maximum(si - 1, 0)
    tok_map = lambda si, j, i_r: (prev(si), jnp.where(si == 0, 0, j), 0)
    vec = pl.BlockSpec((1, d), lambda si, j, i_r: (0, 0))
    _, row0, per_sample = gt
    gt_spec = pl.BlockSpec((1, 1, d), (lambda si, j, i_r: (row0 + prev(si), 0, 0)) if per_sample
                           else (lambda si, j, i_r: (row0, 0, 0)))
    grid_spec = pltpu.PrefetchScalarGridSpec(
        num_scalar_prefetch=1,
        grid=(nb + 1, n_groups),
        in_specs=[
            pl.BlockSpec((COMBINE_EXPERTS, n_slots * chunks, LANES),
                         lambda si, j, i_r: (j, jnp.minimum(si, nb - 1), 0)),
            pl.BlockSpec((1, per_step, d), tok_map),
            gt_spec, vec, vec,
        ],
        out_specs=pl.BlockSpec((1, per_step, d), tok_map),
        scratch_shapes=[pltpu.VMEM((2, s * chunks, LANES), F32)],
    )
    return pl.pallas_call(
        functools.partial(_combine_kernel, n_slots=n_slots, n_samples=nb),
        out_shape=jax.ShapeDtypeStruct((nb, s, d), F32),
        grid_spec=grid_spec,
        compiler_params=_cparams(("arbitrary", "arbitrary")),
        name="combine_postnorm",
    )(idx, y_rm, x_mid, gt[0], g, b)


def kernel(x, c, ctx, c_ctx, w_mod, b_mod, w_in, b_in, w_short, w_conf_dw, b_conf_dw, g_conf_ln, b_conf_ln,
           na_rpb, w_out, b_out, g_post1, b_post1, w_router, w_gate, w_up, w_down, g_post2, b_post2):
    bsz, seq, d = x.shape
    nctx = ctx.shape[1]
    cap = EC_CAPACITY_FACTOR * seq // N_EXPERTS
    cap_ctx = EC_CAPACITY_FACTOR * nctx // N_EXPERTS
    q_scale = NA_HEAD_DIM ** -0.5 * LOG2E

    cond = jnp.concatenate([c, c_ctx[None, :], jnp.zeros((MOD_ROWS - bsz - 1, d), F32)], axis=0)
    mods = _modulation(cond, w_mod, b_mod)

    lat_splits = ((3 * D_CONV, 1.0), (2 * D_CONF, 1.0), (D_NA, q_scale), (D_NA, 1.0), (D_NA, 1.0))
    lat_dtypes = (F32, F32, BF16, BF16, BF16)
    kv_splits = ((D_NA, 1.0), (D_NA, 1.0))
    b_in3 = b_in[:, None, :]

    xc = ctx
    for l in range(DEPTH):
        last = l == DEPTH - 1
        m_lat = [(mods, (l * N_MOD + k) * MOD_ROWS, True) for k in range(N_MOD)]
        m_ctx = [(mods, (l * N_MOD + k) * MOD_ROWS + bsz, False) for k in range(N_MOD)]
        b_out_l = b_out[l][None, :]
        wr = jnp.pad(w_router[l], ((0, 0), (0, LANES - N_EXPERTS)))
        wr_hi = wr.astype(BF16)
        wr_hl = jnp.concatenate([wr_hi, (wr - wr_hi.astype(F32)).astype(BF16)], axis=1)
        g1, b1 = g_post1[l][None, :], b_post1[l][None, :]
        g2, b2 = g_post2[l][None, :], b_post2[l][None, :]
        conv_w = (w_short[l], w_conf_dw[l], b_conf_dw[l], g_conf_ln[l], b_conf_ln[l])

        flat = lambda t: t.reshape(1, bsz * nctx, t.shape[-1])
        unflat = lambda t: t.reshape(bsz, nctx, t.shape[-1])
        if last:
            k_c, v_c = map(unflat, _inproj(flat(xc), m_ctx[0], m_ctx[1], w_in, b_in3, l, OFF_K,
                                           kv_splits, (BF16, BF16), tm=512))
        else:
            uac, ubc, q_c, k_c, v_c = map(unflat, _inproj(flat(xc), m_ctx[0], m_ctx[1], w_in, b_in3, l, 0,
                                                          lat_splits, lat_dtypes, tm=512))

        ua, ub, q, k, v = _inproj(x, m_lat[0], m_lat[1], w_in, b_in3, l, 0, lat_splits, lat_dtypes, tm=512)
        yab = _conv_mixers(ua, ub, *conv_w)
        yc = _neighbourhood_attention(q, k, v, k_c, v_c, na_rpb[l])
        x_mid, hm, logits = _outproj(yab, yc, x, w_out, l, b_out_l, m_lat[2], g1, b1, m_lat[3], m_lat[4],
                                     wr_hi, wr_hl, tm=512)

        idx, gates = _slot_lists(*_route(logits, cap, 0), cap, merge=False)
        idx = idx.reshape(-1)
        x_e = _dispatch(idx, hm, bsz, seq, d, cap)

        x_ec = gates_c = None
        if not last:
            yabc = _conv_mixers(uac, ubc, *conv_w)
            ycc = _context_attention(q_c, k_c, v_c)
            xc_mid, hmc, logits_c = _outproj(flat(yabc), flat(ycc), flat(xc), w_out, l, b_out_l, m_ctx[2], g1, b1,
                                             m_ctx[3], m_ctx[4], wr_hi, wr_hl, tm=512)
            n_c = bsz * cap_ctx
            idx_c, gates_c = _slot_lists(*_route(unflat(logits_c), cap_ctx, cap_ctx), n_c, merge=True)
            idx_c = idx_c.reshape(-1)
            x_ec = _dispatch(idx_c, hmc, 1, bsz * nctx, d, n_c)

        y_e, y_ec = _experts(x_e, gates, x_ec, gates_c, w_gate, w_up, w_down, l, tm=512)
        x = _combine(idx, y_e, x_mid, m_lat[5], g2, b2, cap)
        if not last:
            xc = unflat(_combine(idx_c, y_ec, xc_mid, m_ctx[5], g2, b2, n_c))
    return x
```

```python
import functools
import math

import numpy as np
import jax
import jax.numpy as jnp
from jax import lax
from jax.experimental import pallas as pl
from jax.experimental.pallas import tpu as pltpu

F32 = jnp.float32
BF16 = jnp.bfloat16

D_MODEL = 1024
DEPTH = 2
GRID_W = 64
D_CONV = D_MODEL // 4
D_CONF = D_MODEL // 4
NA_HEAD_DIM = 64
D_NA = D_MODEL - D_CONV - D_CONF
N_NA_HEADS = D_NA // NA_HEAD_DIM
SHORT_CONV_W = 3
CONF_CONV_W = 31
NA_WIN_ROWS_MAX = 8
NA_WIN_COLS = 16
N_EXPERTS = 16
EC_CAPACITY_FACTOR = 2
D_EXPERT = 1024
LN_EPS = 1e-5
DEEPNORM_ALPHA = (2.0 * DEPTH) ** 0.25
NEG_INF = -1e30
LOG2E = math.log2(math.e)

OFF_A = 0
OFF_B = OFF_A + 3 * D_CONV
OFF_Q = OFF_B + 2 * D_CONF
OFF_K = OFF_Q + D_NA
OFF_V = OFF_K + D_NA
D_IN = OFF_V + D_NA

LANES = 128
SUBLANES = 8
MOD_ROWS = 16
VMEM_LIMIT = 56 * 1024 * 1024
ATTN_ROWS = 16
N_PAIRS = D_NA // LANES
ROW_GROUP = 256
OUT_ROW_GROUP = 128
HI = lax.Precision.HIGHEST


def _cparams(sem):
    return pltpu.CompilerParams(dimension_semantics=sem, vmem_limit_bytes=VMEM_LIMIT)


def _ln(x):
    mu = jnp.mean(x, axis=-1, keepdims=True)
    xc = x - mu
    var = jnp.mean(xc * xc, axis=-1, keepdims=True)
    return xc * lax.rsqrt(var + LN_EPS)


def _sigmoid(x):
    return 1.0 / (1.0 + jnp.exp(-x))


def _mod_kernel(cond_ref, w_ref, b_ref, o_ref):
    s = cond_ref[...]
    s = s * _sigmoid(s)
    w = w_ref[0]
    s_hi, w_hi = s.astype(BF16), w.astype(BF16)
    s_lo = (s - s_hi.astype(F32)).astype(BF16)
    w_lo = (w - w_hi.astype(F32)).astype(BF16)
    o_ref[0] = (jnp.dot(s_hi, w_hi, preferred_element_type=F32) + jnp.dot(s_lo, w_hi, preferred_element_type=F32)
                + jnp.dot(s_hi, w_lo, preferred_element_type=F32) + b_ref[0])


N_MOD = 6


def _modulation(cond, w_mod, b_mod):
    n_l, d, n = w_mod.shape
    out = pl.pallas_call(
        _mod_kernel,
        out_shape=jax.ShapeDtypeStruct((n_l * N_MOD, MOD_ROWS, d), F32),
        grid=(n_l, N_MOD),
        in_specs=[
            pl.BlockSpec((MOD_ROWS, d), lambda l, k: (0, 0)),
            pl.BlockSpec((1, d, d), lambda l, k: (l, 0, k)),
            pl.BlockSpec((1, 1, d), lambda l, k: (l, 0, k)),
        ],
        out_specs=pl.BlockSpec((1, MOD_ROWS, d), lambda l, k: (l * N_MOD + k, 0, 0)),
        compiler_params=_cparams(("arbitrary", "arbitrary")),
        name="modulation",
    )(cond, w_mod, b_mod.reshape(n_l, 1, n))
    return out.reshape(n_l * N_MOD * MOD_ROWS, 1, d)


def _mod_spec(mod, d):
    _, row0, per_sample = mod
    if per_sample:
        return pl.BlockSpec((1, 1, d), lambda bi, *_: (row0 + bi, 0, 0))
    return pl.BlockSpec((1, 1, d), lambda bi, *_: (row0, 0, 0))


def _first_step():
    return (pl.program_id(0) == 0) & (pl.program_id(1) == 0)


def _inproj_kernel(x_ref, sh_ref, sc_ref, w_ref, b_ref, *rest, splits, col0):
    *o_refs, w_bf = rest

    @pl.when(_first_step())
    def _():
        w_bf[...] = w_ref[0].astype(BF16)

    tm = x_ref.shape[1]
    n_part = max(1, tm // ROW_GROUP)
    rows_per = tm // n_part
    groups = [slice(part_i * rows_per, (part_i + 1) * rows_per) for part_i in range(n_part)]
    hs = [(_ln(x_ref[0, rows, :]) * (1.0 + sc_ref[0]) + sh_ref[0]).astype(BF16) for rows in groups]
    for rows, h in zip(groups, hs):
        u = jnp.dot(h, w_bf[:, col0:], preferred_element_type=F32) + b_ref[0, :, col0:]
        off = 0
        for o_ref, (width, scale) in zip(o_refs, splits):
            part = u[:, off:off + width]
            if scale != 1.0:
                part = part * scale
            o_ref[0, rows, :] = part.astype(o_ref.dtype)
            off += width


def _inproj(x, sh, sc, w, b, layer, col0, splits, dtypes, tm):
    nb, s, d = x.shape
    n = w.shape[2]
    return pl.pallas_call(
        functools.partial(_inproj_kernel, splits=splits, col0=col0),
        out_shape=[jax.ShapeDtypeStruct((nb, s, wd), dt) for (wd, _), dt in zip(splits, dtypes)],
        grid=(nb, s // tm),
        in_specs=[
            pl.BlockSpec((1, tm, d), lambda bi, i: (bi, i, 0)),
            _mod_spec(sh, d),
            _mod_spec(sc, d),
            pl.BlockSpec((1, d, n), lambda bi, i: (layer, 0, 0), pipeline_mode=pl.Buffered(1)),
            pl.BlockSpec((1, 1, n), lambda bi, i: (layer, 0, 0)),
        ],
        out_specs=[pl.BlockSpec((1, tm, wd), lambda bi, i: (bi, i, 0)) for wd, _ in splits],
        scratch_shapes=[pltpu.VMEM((d, n), BF16)],
        compiler_params=_cparams(("arbitrary", "arbitrary")),
        name="inproj",
    )(x, sh[0], sc[0], w, b)


CONV_CHUNK = 128
Z_PAD = 8
H_PAD = 16


def _conv_kernel(ua_ref, ub_ref, ws_ref, wd_ref, bd_ref, g_ref, b_ref, o_ref, z_scr, h_scr, *, seq):
    c = D_CONV
    z_scr[0:Z_PAD, :] = jnp.zeros((Z_PAD, c), F32)
    z_scr[Z_PAD + seq:2 * Z_PAD + seq, :] = jnp.zeros((Z_PAD, c), F32)
    h_scr[0, 0:H_PAD, :] = jnp.zeros((H_PAD, c), F32)
    h_scr[0, H_PAD + seq:2 * H_PAD + seq, :] = jnp.zeros((H_PAD, c), F32)
    z_scr[Z_PAD:Z_PAD + seq, :] = ua_ref[0, :, c:2 * c] * ua_ref[0, :, 2 * c:3 * c]
    h_scr[0, H_PAD:H_PAD + seq, :] = ub_ref[0, :, 0:c] * _sigmoid(ub_ref[0, :, c:2 * c])
    n_rows = seq + 2 * H_PAD
    h_all = h_scr[0]
    for r in range(1, SUBLANES):
        h_scr[r] = pltpu.roll(h_all, n_rows - r, axis=0)
    tc = min(CONV_CHUNK, seq)
    for ci in range(seq // tc):
        t0 = ci * tc
        acc = ws_ref[0:1, :] * z_scr[t0 + Z_PAD - 1:t0 + Z_PAD - 1 + tc, :]
        for j in range(1, SHORT_CONV_W):
            s0 = t0 + Z_PAD - 1 + j
            acc = acc + ws_ref[j:j + 1, :] * z_scr[s0:s0 + tc, :]
        ya = ua_ref[0, t0:t0 + tc, 0:c] * acc
        hb = bd_ref[...]
        for j in range(CONF_CONV_W):
            s0 = t0 + H_PAD - CONF_CONV_W // 2 + j
            a0 = s0 - s0 % SUBLANES
            hb = hb + wd_ref[j:j + 1, :] * h_scr[s0 % SUBLANES, a0:a0 + tc, :]
        hn = _ln(hb) * g_ref[...] + b_ref[...]
        yb = hn * _sigmoid(hn)
        o_ref[0, t0:t0 + tc, 0:c] = ya.astype(o_ref.dtype)
        o_ref[0, t0:t0 + tc, c:2 * c] = yb.astype(o_ref.dtype)


def _conv_mixers(ua, ub, w_short, w_dw, b_dw, g_ln, b_ln):
    nb, s, _ = ua.shape
    c = D_CONV
    full = lambda shape: pl.BlockSpec(shape, lambda bi: (0,) * len(shape))
    return pl.pallas_call(
        functools.partial(_conv_kernel, seq=s),
        out_shape=jax.ShapeDtypeStruct((nb, s, 2 * c), BF16),
        grid=(nb,),
        in_specs=[
            pl.BlockSpec((1, s, 3 * c), lambda bi: (bi, 0, 0)),
            pl.BlockSpec((1, s, 2 * c), lambda bi: (bi, 0, 0)),
            full((SHORT_CONV_W, c)), full((CONF_CONV_W, c)), full((1, c)), full((1, c)), full((1, c)),
        ],
        out_specs=pl.BlockSpec((1, s, 2 * c), lambda bi: (bi, 0, 0)),
        scratch_shapes=[pltpu.VMEM((s + 2 * Z_PAD, c), F32), pltpu.VMEM((SUBLANES, s + 2 * H_PAD, c), F32)],
        compiler_params=_cparams(("arbitrary",)),
        name="conv_mixers",
    )(ua, ub, w_short, w_dw, b_dw.reshape(1, c), g_ln.reshape(1, c), b_ln.reshape(1, c))


SUB_ROWS = 2
WIN_ROWS = SUB_ROWS + NA_WIN_ROWS_MAX - 1
BAND_KEYS = WIN_ROWS * GRID_W
SUB_Q = SUB_ROWS * GRID_W
N_DROW = 2 * NA_WIN_ROWS_MAX - 1
N_DCOL = 2 * NA_WIN_COLS - 1


def _bias_kernel(rpb_ref, nxt_ref, o_ref):
    n_cols = o_ref.shape[1]
    col = lax.broadcasted_iota(jnp.int32, (LANES, n_cols), 1)
    qc = col >> (LANES.bit_length() - 1)
    kc = col & (GRID_W - 1)
    d_col = jnp.clip(kc - qc + (NA_WIN_COLS - 1), 0, N_DCOL - 1)
    onehot = (lax.broadcasted_iota(jnp.int32, (LANES, n_cols), 0) == d_col).astype(F32)
    own = jnp.dot(rpb_ref[...], onehot, preferred_element_type=F32, precision=HI) * LOG2E
    nxt = jnp.dot(nxt_ref[...], onehot, preferred_element_type=F32, precision=HI) * LOG2E
    c0 = jnp.clip(qc - NA_WIN_COLS // 2, 0, GRID_W - NA_WIN_COLS)
    inside = (kc >= c0) & (kc < c0 + NA_WIN_COLS)
    second = (col & GRID_W) != 0
    o_ref[0:LANES, :] = jnp.where(inside, jnp.where(second, nxt, own), NEG_INF)
    o_ref[LANES:2 * LANES, :] = jnp.where(inside & ~second, own, NEG_INF)
    o_ref[2 * LANES:3 * LANES, :] = jnp.where(inside & second, own, NEG_INF)
    o_ref[3 * LANES:, :] = jnp.full((o_ref.shape[0] - 3 * LANES, n_cols), NEG_INF, F32)


N_SLABS = -(-WIN_ROWS // 2)
TILES_PER_KIND = LANES
MASKED_TILE = 3 * TILES_PER_KIND


def _na_plan(rows):
    wr = min(NA_WIN_ROWS_MAX, rows)
    assert wr == NA_WIN_ROWS_MAX and rows % SUB_ROWS == 0 and rows >= WIN_ROWS
    row_start = np.clip(np.arange(rows) - wr // 2, 0, rows - wr)
    w0s, tiles = [], []
    for r0 in range(0, rows, SUB_ROWS):
        w0 = int(np.clip(r0 - wr // 2, 0, rows - WIN_ROWS))
        for iq in range(SUB_ROWS):
            r = r0 + iq
            assert row_start[r] >= w0 and row_start[r] + wr <= w0 + WIN_ROWS
            ok = lambda w: w < WIN_ROWS and row_start[r] <= w0 + w < row_start[r] + wr
            d_row = lambda w: w0 + w - r + NA_WIN_ROWS_MAX - 1
            for j in range(N_SLABS):
                lo, hi = ok(2 * j), ok(2 * j + 1)
                if lo and hi:
                    tiles.append(d_row(2 * j))
                elif lo:
                    tiles.append(TILES_PER_KIND + d_row(2 * j))
                elif hi:
                    tiles.append(2 * TILES_PER_KIND + d_row(2 * j + 1))
                else:
                    tiles.append(-1)
        w0s.append(w0)
    return np.array(w0s, np.int32), np.array(tiles, np.int32)


def _na_bias(rpb):
    n_hd = N_NA_HEADS * N_DROW
    assert GRID_W == 64 and 2 * GRID_W == LANES and n_hd <= TILES_PER_KIND and N_DCOL <= LANES
    rpb = rpb.astype(F32)
    pad = lambda t: jnp.pad(t.reshape(n_hd, N_DCOL), ((0, LANES - n_hd), (0, LANES - N_DCOL)))
    nxt = jnp.concatenate([rpb[:, 1:], jnp.zeros_like(rpb[:, :1])], axis=1)
    n_tiles = 3 * TILES_PER_KIND + SUBLANES
    table = pl.pallas_call(
        _bias_kernel,
        out_shape=jax.ShapeDtypeStruct((n_tiles, GRID_W * LANES), F32),
        compiler_params=pltpu.CompilerParams(vmem_limit_bytes=VMEM_LIMIT),
        name="na_bias",
    )(pad(rpb), pad(nxt))
    return table.reshape(n_tiles, GRID_W, LANES)


def _lane_reduce(xs, combine, reduce, neutral):
    chunks = []
    for x in xs:
        rows, n = x.shape
        n_full = n // LANES
        chunks += [x[:, j * LANES:(j + 1) * LANES] for j in range(n_full)]
        if n % LANES:
            fill = jnp.full((rows, LANES - n % LANES), neutral, x.dtype)
            chunks.append(jnp.concatenate([x[:, n_full * LANES:], fill], axis=1))
    return reduce(functools.reduce(combine, chunks), axis=-1, keepdims=True)


def _attn_kernel(w0_ref, tile_ref, q_ref, k_ref, v_ref, kc_ref, vc_ref, *rest, banded, n_sub, sub_q):
    if banded:
        bias_ref, o_ref = rest
    else:
        (o_ref,) = rest
    lane = lax.broadcasted_iota(jnp.int32, (sub_q, LANES), 1)
    first = lane < NA_HEAD_DIM
    nt = (((1,), (1,)), ((), ()))
    stages = [(si, p) for si in range(n_sub) for p in range(N_PAIRS)]

    def window(si):
        blk = pl.program_id(1) * n_sub + si
        return blk, pl.multiple_of(w0_ref[blk] * GRID_W, GRID_W)

    def bias(blk, head):
        row_blocks = []
        for iq in range(SUB_ROWS):
            slabs = []
            for j in range(N_SLABS):
                t = tile_ref[(blk * SUB_ROWS + iq) * N_SLABS + j]
                tile = bias_ref[jnp.where(t < 0, MASKED_TILE, t + head * N_DROW)]
                width = min(LANES, BAND_KEYS - j * LANES)
                slabs.append(tile[:, :width])
            row_blocks.append(jnp.concatenate(slabs, axis=1))
        return jnp.concatenate(row_blocks, axis=0)

    def scores(si, p):
        cols = slice(p * LANES, (p + 1) * LANES)
        q_p = q_ref[0, si * sub_q:(si + 1) * sub_q, cols]
        zero = jnp.zeros_like(q_p)
        qq = jnp.concatenate([jnp.where(first, q_p, zero), jnp.where(first, zero, q_p)], axis=0)
        parts = [lax.dot_general(qq, kc_ref[0, :, cols], nt, preferred_element_type=F32)]
        if banded:
            blk, start = window(si)
            both = jnp.concatenate([bias(blk, 2 * p), bias(blk, 2 * p + 1)], axis=0)
            parts.append(lax.dot_general(qq, k_ref[0, pl.ds(start, BAND_KEYS), cols], nt,
                                         preferred_element_type=F32) + both)
        return parts

    def weights(parts):
        m = _lane_reduce(parts, jnp.maximum, jnp.max, NEG_INF)
        es = [jnp.exp2(s - m) for s in parts]
        den = _lane_reduce(es, jnp.add, jnp.sum, 0.0)
        return [e.astype(BF16) for e in es], den

    def values(si, p, es, den):
        cols = slice(p * LANES, (p + 1) * LANES)
        o = jnp.dot(es[0], vc_ref[0, :, cols], preferred_element_type=F32)
        if banded:
            _, start = window(si)
            o = o + jnp.dot(es[1], v_ref[0, pl.ds(start, BAND_KEYS), cols], preferred_element_type=F32)
        o = o * (1.0 / den)
        out = jnp.where(first, o[:sub_q], o[sub_q:])
        o_ref[0, si * sub_q:(si + 1) * sub_q, cols] = out.astype(o_ref.dtype)

    nxt = scores(*stages[0])
    pending = None
    for i, (si, p) in enumerate(stages):
        cur = nxt
        if i + 1 < len(stages):
            nxt = scores(*stages[i + 1])
        if pending is not None:
            values(*pending)
        es, den = weights(cur)
        pending = (si, p, es, den)
    values(*pending)


def _neighbourhood_attention(q, k, v, kc, vc, rpb):
    nb, s, dn = q.shape
    rows = s // GRID_W
    nctx = kc.shape[1]
    w0s, tiles = _na_plan(rows)
    bias = _na_bias(rpb)
    n_sub = ATTN_ROWS // SUB_ROWS
    m_rows = ATTN_ROWS * GRID_W
    grid_spec = pltpu.PrefetchScalarGridSpec(
        num_scalar_prefetch=2,
        grid=(nb, rows // ATTN_ROWS),
        in_specs=[
            pl.BlockSpec((1, m_rows, dn), lambda bi, i, w0, pat: (bi, i, 0)),
            pl.BlockSpec((1, s, dn), lambda bi, i, w0, pat: (bi, 0, 0)),
            pl.BlockSpec((1, s, dn), lambda bi, i, w0, pat: (bi, 0, 0)),
            pl.BlockSpec((1, nctx, dn), lambda bi, i, w0, pat: (bi, 0, 0)),
            pl.BlockSpec((1, nctx, dn), lambda bi, i, w0, pat: (bi, 0, 0)),
            pl.BlockSpec(bias.shape, lambda bi, i, w0, pat: (0, 0, 0), pipeline_mode=pl.Buffered(1)),
        ],
        out_specs=pl.BlockSpec((1, m_rows, dn), lambda bi, i, w0, pat: (bi, i, 0)),
    )
    return pl.pallas_call(
        functools.partial(_attn_kernel, banded=True, n_sub=n_sub, sub_q=SUB_Q),
        out_shape=jax.ShapeDtypeStruct((nb, s, dn), BF16),
        grid_spec=grid_spec,
        compiler_params=_cparams(("arbitrary", "arbitrary")),
        name="neighbourhood_attention",
    )(jnp.asarray(w0s), jnp.asarray(tiles), q, k, v, kc, vc, bias)


def _context_attention(q, kc, vc):
    nb, s, dn = q.shape
    spec = pl.BlockSpec((1, s, dn), lambda bi, i, w0, pat: (bi, 0, 0))
    grid_spec = pltpu.PrefetchScalarGridSpec(
        num_scalar_prefetch=2, grid=(nb, 1), in_specs=[spec] * 5, out_specs=spec)
    dummy = jnp.zeros((1,), jnp.int32)
    return pl.pallas_call(
        functools.partial(_attn_kernel, banded=False, n_sub=1, sub_q=s),
        out_shape=jax.ShapeDtypeStruct((nb, s, dn), BF16),
        grid_spec=grid_spec,
        compiler_params=_cparams(("arbitrary", "arbitrary")),
        name="context_attention",
    )(dummy, dummy, q, kc, vc, kc, vc)


def _store_rowmajor(ref, val, base=0):
    n, width = val.shape
    chunks = width // LANES
    for c in range(chunks):
        ref[pl.ds(base + c, n, stride=chunks), :] = val[:, c * LANES:(c + 1) * LANES]


def _load_rowmajor(ref, base, n, chunks):
    return jnp.concatenate([ref[pl.ds(base + c, n, stride=chunks), :] for c in range(chunks)], axis=1)


def _outproj_kernel(yab_ref, yc_ref, x_ref, w_ref, bo_ref, gt_ref, g_ref, b_ref, sh_ref, sc_ref,
                    wrh_ref, wrhl_ref, xmid_ref, hm_ref, lg_ref, w_bf):
    @pl.when(_first_step())
    def _():
        w_bf[...] = w_ref[0].astype(BF16)

    half = yab_ref.shape[2]
    tm = x_ref.shape[1]
    chunks = x_ref.shape[2] // LANES
    n_part = max(1, tm // OUT_ROW_GROUP)
    rows_per = tm // n_part
    groups = [slice(part * rows_per, (part + 1) * rows_per) for part in range(n_part)]
    ys = [jnp.dot(yab_ref[0, rows, :], w_bf[0:half, :], preferred_element_type=F32)
          + jnp.dot(yc_ref[0, rows, :], w_bf[half:, :], preferred_element_type=F32) + bo_ref[...]
          for rows in groups]
    for part, (rows, y) in enumerate(zip(groups, ys)):
        xm = _ln(DEEPNORM_ALPHA * x_ref[0, rows, :] + (1.0 + gt_ref[0]) * y) * g_ref[...] + b_ref[...]
        xmid_ref[0, rows, :] = xm
        hm = _ln(xm) * (1.0 + sc_ref[0]) + sh_ref[0]
        _store_rowmajor(hm_ref, hm, base=part * rows_per * chunks)
        hm_hi = hm.astype(BF16)
        hm_lo = (hm - hm_hi.astype(F32)).astype(BF16)
        both = jnp.dot(hm_hi, wrhl_ref[...], preferred_element_type=F32)
        lg_ref[0, rows, :] = (both[:, :LANES] + both[:, LANES:]
                              + jnp.dot(hm_lo, wrh_ref[...], preferred_element_type=F32))


def _outproj(yab, yc, x, w, layer, bo, gt, g, b, sh, sc, wr_hi, wr_hl, tm):
    nb, s, d = x.shape
    half = yab.shape[2]
    vec = pl.BlockSpec((1, d), lambda bi, i: (0, 0))
    tok = lambda width: pl.BlockSpec((1, tm, width), lambda bi, i: (bi, i, 0))
    n_i = s // tm
    return pl.pallas_call(
        _outproj_kernel,
        out_shape=[jax.ShapeDtypeStruct((nb, s, d), F32),
                   jax.ShapeDtypeStruct((nb * s * (d // LANES), LANES), F32),
                   jax.ShapeDtypeStruct((nb, s, LANES), F32)],
        grid=(nb, n_i),
        in_specs=[tok(half), tok(half), tok(d),
                  pl.BlockSpec((1, d, d), lambda bi, i: (layer, 0, 0), pipeline_mode=pl.Buffered(1)),
                  vec, _mod_spec(gt, d), vec, vec,
                  _mod_spec(sh, d), _mod_spec(sc, d), pl.BlockSpec((d, LANES), lambda bi, i: (0, 0)),
                  pl.BlockSpec((d, 2 * LANES), lambda bi, i: (0, 0))],
        out_specs=[tok(d), pl.BlockSpec((tm * (d // LANES), LANES), lambda bi, i: (bi * n_i + i, 0)),
                   tok(LANES)],
        scratch_shapes=[pltpu.VMEM((d, d), BF16)],
        compiler_params=_cparams(("arbitrary", "arbitrary")),
        name="outproj_postnorm",
    )(yab, yc, x, w, bo, gt[0], g, b, sh[0], sc[0], wr_hi, wr_hl)


CUM_CHUNK = 256
F32_EXP_BIAS = 127
F32_MANT_BITS = 23


def _prefix_count(mask_f32, tri):
    rows, n = mask_f32.shape
    tc = min(CUM_CHUNK, n)
    base = jnp.zeros((rows, 1), F32)
    parts = []
    for ci in range(n // tc):
        blk = mask_f32[:, ci * tc:(ci + 1) * tc]
        parts.append(jnp.dot(blk.astype(BF16), tri[:tc, :tc], preferred_element_type=F32) + base)
        base = base + jnp.sum(blk, axis=-1, keepdims=True)
    return jnp.concatenate(parts, axis=-1)


def _pow2(k):
    return pltpu.bitcast((k + F32_EXP_BIAS) << F32_MANT_BITS, F32)


def _route_kernel(lg_ref, slot_c_ref, gate_t_ref, *, cap, slot_stride):
    nb = lg_ref.shape[0]
    assert nb * N_EXPERTS == LANES
    rows = []
    for b in range(nb):
        lg = lg_ref[b]
        lane = lax.broadcasted_iota(jnp.int32, lg.shape, 1)
        lgm = jnp.where(lane < N_EXPERTS, lg, NEG_INF)
        ex = jnp.exp(lgm - jnp.max(lgm, axis=-1, keepdims=True))
        aff = ex / jnp.sum(ex, axis=-1, keepdims=True)
        rows.append(aff.T[0:N_EXPERTS, :])
    a = jnp.concatenate(rows, axis=0)
    capf = float(cap)

    def enough(t):
        return jnp.sum((a >= t).astype(F32), axis=-1, keepdims=True) >= capf

    def exp_step(_, carry):
        lo, hi = carry
        mid = lo + ((hi - lo + 1) >> 1)
        ok = enough(_pow2(mid))
        return jnp.where(ok, mid, lo), jnp.where(ok, hi, mid - 1)

    k_lo = jnp.full((LANES, 1), -F32_EXP_BIAS, jnp.int32)
    k_hi = jnp.zeros((LANES, 1), jnp.int32)
    k_lo, _ = lax.fori_loop(0, 7, exp_step, (k_lo, k_hi))
    base = _pow2(k_lo)

    def mant_step(_, carry):
        t, step = carry
        step = step * 0.5
        cand = t + step
        return jnp.where(enough(cand), cand, t), step

    thr, _ = lax.fori_loop(0, F32_MANT_BITS, mant_step, (base, base))

    r_i = lax.broadcasted_iota(jnp.int32, (CUM_CHUNK, CUM_CHUNK), 0)
    c_i = lax.broadcasted_iota(jnp.int32, (CUM_CHUNK, CUM_CHUNK), 1)
    tri = (r_i < c_i).astype(BF16)
    gt = (a > thr).astype(F32)
    eq = (a == thr).astype(F32)
    need = capf - jnp.sum(gt, axis=-1, keepdims=True)
    sel = gt + eq * (_prefix_count(eq, tri) < need).astype(F32)
    pos = _prefix_count(sel, tri)
    sample = lax.broadcasted_iota(jnp.int32, (LANES, 1), 0) >> (N_EXPERTS.bit_length() - 1)
    slot = jnp.where(sel > 0.0, pos + (sample * slot_stride).astype(F32), -1.0)
    for b in range(nb):
        lo = b * N_EXPERTS
        gate_t_ref[b] = a[lo:lo + N_EXPERTS, :]
        rolled = slot if b == 0 else jnp.concatenate([slot[lo:, :], slot[:lo, :]], axis=0)
        slot_c_ref[b] = rolled.T


def _route(logits, cap, slot_stride):
    nb, s, _ = logits.shape
    whole = lambda shape: pl.BlockSpec(shape, lambda i: (0,) * len(shape))
    return pl.pallas_call(
        functools.partial(_route_kernel, cap=cap, slot_stride=slot_stride),
        out_shape=[jax.ShapeDtypeStruct((nb, s, LANES), F32), jax.ShapeDtypeStruct((nb, N_EXPERTS, s), F32)],
        grid=(1,),
        in_specs=[whole((nb, s, LANES))],
        out_specs=[whole((nb, s, LANES)), whole((nb, N_EXPERTS, s))],
        compiler_params=_cparams(("arbitrary",)),
        name="route",
    )(logits)


TOK_SPLIT = 64


def _slot_list_kernel(slot_ref, gate_ref, idx_ref, g_ref, *, n_slots, tok_stride, merge):
    s = slot_ref.shape[1]
    b = pl.program_id(0)
    assert n_slots <= 256
    slot_id = lax.broadcasted_iota(jnp.int32, (s, n_slots), 1).astype(F32).astype(BF16)
    one, zero = jnp.ones((s, n_slots), BF16), jnp.zeros((s, n_slots), BF16)
    tok = lax.broadcasted_iota(jnp.int32, (1, s), 1) + b * tok_stride
    tok_hi = (tok >> (TOK_SPLIT.bit_length() - 1)).astype(F32)
    tok_lo = (tok & (TOK_SPLIT - 1)).astype(F32)
    zeros = jnp.zeros((SUBLANES - 5, s), F32)
    idx_rows, g_rows = [], []
    for e in range(N_EXPERTS):
        taken = jnp.broadcast_to(slot_ref[0, :, e:e + 1].astype(BF16), (s, n_slots))
        hit = jnp.where(taken == slot_id, one, zero)
        g0 = gate_ref[0, e:e + 1, :]
        g_hi = g0.astype(BF16).astype(F32)
        g_mid = (g0 - g_hi).astype(BF16).astype(F32)
        g_lo = g0 - g_hi - g_mid
        lhs = jnp.concatenate([tok_hi, tok_lo, g_hi, g_mid, g_lo, zeros], axis=0).astype(BF16)
        out = jnp.dot(lhs, hit, preferred_element_type=F32)
        idx_rows.append(out[0:1] * float(TOK_SPLIT) + out[1:2])
        g_rows.append(out[2:3] + out[3:4] + out[4:5])
    idx = jnp.concatenate(idx_rows, axis=0).astype(jnp.int32) * SUBLANES
    g = jnp.concatenate(g_rows + [jnp.zeros((LANES - N_EXPERTS, n_slots), F32)], axis=0)
    g_t = g.T
    g_cols = [jnp.broadcast_to(g_t[:, e:e + 1], (n_slots, LANES)) for e in range(N_EXPERTS)]
    if merge:
        @pl.when(b == 0)
        def _():
            idx_ref[0] = idx
            for e in range(N_EXPERTS):
                g_ref[e] = g_cols[e]

        @pl.when(b > 0)
        def _():
            idx_ref[0] = idx_ref[0] + idx
            for e in range(N_EXPERTS):
                g_ref[e] = g_ref[e] + g_cols[e]
    else:
        idx_ref[0] = idx
        for e in range(N_EXPERTS):
            g_ref[e] = g_cols[e]


def _slot_lists(slot_c, gate_t, n_slots, merge):
    nb, s, _ = slot_c.shape
    nbo = 1 if merge else nb
    idx_map = (lambda bi: (0, 0, 0)) if merge else (lambda bi: (bi, 0, 0))
    g_map = (lambda bi: (0, 0, 0)) if merge else (lambda bi: (0, bi, 0))
    return pl.pallas_call(
        functools.partial(_slot_list_kernel, n_slots=n_slots, tok_stride=s if merge else 0, merge=merge),
        out_shape=[jax.ShapeDtypeStruct((nbo, N_EXPERTS, n_slots), jnp.int32),
                   jax.ShapeDtypeStruct((N_EXPERTS, nbo * n_slots, LANES), F32)],
        grid=(nb,),
        in_specs=[pl.BlockSpec((1, s, LANES), lambda bi: (bi, 0, 0)),
                  pl.BlockSpec((1, N_EXPERTS, s), lambda bi: (bi, 0, 0))],
        out_specs=[pl.BlockSpec((1, N_EXPERTS, n_slots), idx_map),
                   pl.BlockSpec((N_EXPERTS, n_slots, LANES), g_map)],
        compiler_params=_cparams(("arbitrary",)),
        name="slot_lists",
    )(slot_c, gate_t)


ROW_UNROLL = 32


def _dispatch_kernel(idx_ref, hm_ref, x_ref, rows_scr, *, n_slots):
    b = pl.program_id(0)
    chunks = x_ref.shape[2] // LANES

    def emit(e, s0):
        src = pl.multiple_of(s0 * chunks, ROW_UNROLL * chunks)
        rows = _load_rowmajor(rows_scr.at[e % 2], src, ROW_UNROLL, chunks)
        x_ref[e, pl.ds(pl.multiple_of(s0, ROW_UNROLL), ROW_UNROLL), :] = rows.astype(x_ref.dtype)

    for e in range(N_EXPERTS + 1):
        base = (b * N_EXPERTS + e) * n_slots

        def move(i, carry):
            s0 = i * ROW_UNROLL
            if e < N_EXPERTS:
                first = base + s0
                tiles = [hm_ref[pl.ds(pl.multiple_of(idx_ref[first + u], chunks), chunks), :]
                         for u in range(ROW_UNROLL)]
                dst = pl.multiple_of(s0 * chunks, ROW_UNROLL * chunks)
                rows_scr[e % 2, pl.ds(dst, ROW_UNROLL * chunks), :] = jnp.concatenate(tiles, axis=0)
            if e > 0:
                emit(e - 1, s0)
            return carry

        lax.fori_loop(0, n_slots // ROW_UNROLL, move, 0)


def _dispatch(idx, hm_rm, nb, s, d, n_slots):
    chunks = d // LANES
    assert chunks == SUBLANES and n_slots % ROW_UNROLL == 0
    grid_spec = pltpu.PrefetchScalarGridSpec(
        num_scalar_prefetch=1,
        grid=(nb,),
        in_specs=[pl.BlockSpec((s * chunks, LANES), lambda bi, idx_r: (bi, 0))],
        out_specs=pl.BlockSpec((N_EXPERTS, n_slots, d), lambda bi, idx_r: (0, bi, 0)),
        scratch_shapes=[pltpu.VMEM((2, n_slots * chunks, LANES), F32)],
    )
    return pl.pallas_call(
        functools.partial(_dispatch_kernel, n_slots=n_slots),
        out_shape=jax.ShapeDtypeStruct((N_EXPERTS, nb * n_slots, d), BF16),
        grid_spec=grid_spec,
        compiler_params=_cparams(("arbitrary",)),
        name="dispatch",
    )(idx, hm_rm)


W_PARTS = 4

def _expert_kernel(*refs, n_main, has_extra):
    n_in = 4 if has_extra else 2
    n_out = 2 if has_extra else 1
    acts, w_parts = refs[:n_in], refs[n_in:n_in + 3]
    outs = refs[n_in + 3:n_in + 3 + n_out]
    w_bfs = refs[n_in + 3 + n_out:]
    if has_extra:
        x_ref, g_ref, xx_ref, gg_ref = acts
        y_ref, yy_ref = outs
    else:
        x_ref, g_ref = acts
        (y_ref,) = outs
    s = pl.program_id(0)
    i = pl.program_id(1)
    fill = s % 2
    cur = 1 - fill
    wg_bf, wu_bf, wd_bf = (w_bf.at[cur] for w_bf in w_bfs)

    def cast_part():
        block = jnp.minimum(i, W_PARTS - 1)
        for part, w_bf in zip(w_parts, w_bfs):
            rows = part.shape[2]
            w_bf[fill, pl.ds(pl.multiple_of(block * rows, rows), rows), :] = part[0, 0].astype(BF16)

    def ffn(x_in, gate_in, out_ref):
        rows_all = x_in.shape[1]
        chunks = x_in.shape[2] // LANES
        n_part = max(1, rows_all // ROW_GROUP)
        rows_per = rows_all // n_part
        groups = [slice(p * rows_per, (p + 1) * rows_per) for p in range(n_part)]
        ups = []
        for rows in groups:
            x_e = x_in[0, rows, :]
            ups.append((jnp.dot(x_e, wg_bf[...], preferred_element_type=F32),
                        jnp.dot(x_e, wu_bf[...], preferred_element_type=F32)))
        for p, (rows, (a, u)) in enumerate(zip(groups, ups)):
            h = (a * _sigmoid(a) * u).astype(BF16)
            y = jnp.dot(h, wd_bf[...], preferred_element_type=F32) * gate_in[0, rows, 0:1]
            _store_rowmajor(out_ref, y, base=p * rows_per * chunks)

    @pl.when(s == 0)
    def _():
        cast_part()

    if has_extra:
        @pl.when((s > 0) & (i < n_main))
        def _():
            cast_part()
            ffn(x_ref, g_ref, y_ref.at[0])

        @pl.when((s > 0) & (i == n_main))
        def _():
            cast_part()
            ffn(xx_ref, gg_ref, yy_ref.at[0])
    else:
        @pl.when(s > 0)
        def _():
            cast_part()
            ffn(x_ref, g_ref, y_ref.at[0])


def _experts(x_e, g_e, x_extra, g_extra, wg, wu, wd, layer, tm):
    _, rows, d = x_e.shape
    f = wg.shape[3]
    chunks = d // LANES
    n_main = rows // tm
    has_extra = x_extra is not None
    assert n_main >= W_PARTS and d % W_PARTS == 0 and f % W_PARTS == 0
    expert = lambda s: jnp.maximum(s - 1, 0)
    main_map = lambda s, i: (expert(s), jnp.where(s == 0, 0, jnp.minimum(i, n_main - 1)), 0)
    extra_map = lambda s, i: (expert(s), 0, 0)

    def w_spec(shape):
        blk = (1, 1, shape[0] // W_PARTS, shape[1])
        return pl.BlockSpec(blk, lambda s, i: (layer, jnp.minimum(s, N_EXPERTS - 1),
                                               jnp.minimum(i, W_PARTS - 1), 0))

    in_specs = [pl.BlockSpec((1, tm, d), main_map), pl.BlockSpec((1, tm, LANES), main_map)]
    out_specs = [pl.BlockSpec((1, tm * chunks, LANES), main_map)]
    out_shape = [jax.ShapeDtypeStruct((N_EXPERTS, rows * chunks, LANES), F32)]
    args = [x_e, g_e]
    if has_extra:
        rows2 = x_extra.shape[1]
        in_specs += [pl.BlockSpec((1, rows2, d), extra_map), pl.BlockSpec((1, rows2, LANES), extra_map)]
        out_specs.append(pl.BlockSpec((1, rows2 * chunks, LANES), extra_map))
        out_shape.append(jax.ShapeDtypeStruct((N_EXPERTS, rows2 * chunks, LANES), F32))
        args += [x_extra, g_extra]
    outs = pl.pallas_call(
        functools.partial(_expert_kernel, n_main=n_main, has_extra=has_extra),
        out_shape=out_shape,
        grid=(N_EXPERTS + 1, n_main + int(has_extra)),
        in_specs=in_specs + [w_spec((d, f)), w_spec((d, f)), w_spec((f, d))],
        out_specs=out_specs,
        scratch_shapes=[pltpu.VMEM((2, d, f), BF16), pltpu.VMEM((2, d, f), BF16), pltpu.VMEM((2, f, d), BF16)],
        compiler_params=_cparams(("arbitrary", "arbitrary")),
        name="experts",
    )(*args, wg, wu, wd)
    return outs if has_extra else (outs[0], None)


COMBINE_EXPERTS = 4
ADD_UNROLL = 16


def _combine_kernel(idx_ref, y_ref, x_ref, gt_ref, g_ref, b_ref, o_ref, acc, *, n_slots, n_samples):
    s = pl.program_id(0)
    j = pl.program_id(1)
    d = x_ref.shape[2]
    chunks = d // LANES
    per_step = x_ref.shape[1]
    fill = s % 2
    done = 1 - fill
    base = (s * N_EXPERTS + j * COMBINE_EXPERTS) * n_slots

    part = per_step // COMBINE_EXPERTS

    def scatter(eg):
        for g in range(n_slots // ADD_UNROLL):
            first = base + eg * n_slots + g * ADD_UNROLL
            y_rows = y_ref[eg, g * ADD_UNROLL * chunks:(g + 1) * ADD_UNROLL * chunks, :]
            new = []
            for u in range(ADD_UNROLL):
                dst = pl.multiple_of(idx_ref[first + u], chunks)
                new.append((dst, acc[fill, pl.ds(dst, chunks), :] + y_rows[u * chunks:(u + 1) * chunks, :]))
            for dst, val in new:
                acc[fill, pl.ds(dst, chunks), :] = val

    def summed(eg):
        row0 = pl.multiple_of((j * per_step + eg * part) * chunks, part * chunks)
        return _load_rowmajor(acc.at[done], row0, part, chunks)

    def post_norm(eg, ym):
        rows = slice(eg * part, (eg + 1) * part)
        z = DEEPNORM_ALPHA * x_ref[0, rows, :] + (1.0 + gt_ref[0]) * ym
        o_ref[0, rows, :] = _ln(z) * g_ref[...] + b_ref[...]

    @pl.when((s < n_samples) & (j == 0))
    def _():
        acc[fill] = jnp.zeros(acc.shape[1:], F32)

    @pl.when(s == 0)
    def _():
        for eg in range(COMBINE_EXPERTS):
            scatter(eg)

    @pl.when((s > 0) & (s < n_samples))
    def _():
        for eg in range(COMBINE_EXPERTS):
            ym = summed(eg)
            scatter(eg)
            post_norm(eg, ym)

    @pl.when(s == n_samples)
    def _():
        for eg in range(COMBINE_EXPERTS):
            post_norm(eg, summed(eg))


def _combine(idx, y_rm, x_mid, gt, g, b, n_slots):
    nb, s, d = x_mid.shape
    chunks = d // LANES
    n_groups = N_EXPERTS // COMBINE_EXPERTS
    per_step = s // n_groups
    assert chunks == SUBLANES and N_EXPERTS % COMBINE_EXPERTS == 0 and s % n_groups == 0
    assert per_step % SUBLANES == 0 and n_slots % ADD_UNROLL == 0
    prev = lambda si: jnp.maximum(si - 1, 0)
    tok_map = lambda si, j, i_r: (prev(si), jnp.where(si == 0, 0, j), 0)
    vec = pl.BlockSpec((1, d), lambda si, j, i_r: (0, 0))
    _, row0, per_sample = gt
    gt_spec = pl.BlockSpec((1, 1, d), (lambda si, j, i_r: (row0 + prev(si), 0, 0)) if per_sample
                           else (lambda si, j, i_r: (row0, 0, 0)))
    grid_spec = pltpu.PrefetchScalarGridSpec(
        num_scalar_prefetch=1,
        grid=(nb + 1, n_groups),
        in_specs=[
            pl.BlockSpec((COMBINE_EXPERTS, n_slots * chunks, LANES),
                         lambda si, j, i_r: (j, jnp.minimum(si, nb - 1), 0)),
            pl.BlockSpec((1, per_step, d), tok_map),
            gt_spec, vec, vec,
        ],
        out_specs=pl.BlockSpec((1, per_step, d), tok_map),
        scratch_shapes=[pltpu.VMEM((2, s * chunks, LANES), F32)],
    )
    return pl.pallas_call(
        functools.partial(_combine_kernel, n_slots=n_slots, n_samples=nb),
        out_shape=jax.ShapeDtypeStruct((nb, s, d), F32),
        grid_spec=grid_spec,
        compiler_params=_cparams(("arbitrary", "arbitrary")),
        name="combine_postnorm",
    )(idx, y_rm, x_mid, gt[0], g, b)


def kernel(x, c, ctx, c_ctx, w_mod, b_mod, w_in, b_in, w_short, w_conf_dw, b_conf_dw, g_conf_ln, b_conf_ln,
           na_rpb, w_out, b_out, g_post1, b_post1, w_router, w_gate, w_up, w_down, g_post2, b_post2):
    bsz, seq, d = x.shape
    nctx = ctx.shape[1]
    cap = EC_CAPACITY_FACTOR * seq // N_EXPERTS
    cap_ctx = EC_CAPACITY_FACTOR * nctx // N_EXPERTS
    q_scale = NA_HEAD_DIM ** -0.5 * LOG2E

    cond = jnp.concatenate([c, c_ctx[None, :], jnp.zeros((MOD_ROWS - bsz - 1, d), F32)], axis=0)
    mods = _modulation(cond, w_mod, b_mod)

    lat_splits = ((3 * D_CONV, 1.0), (2 * D_CONF, 1.0), (D_NA, q_scale), (D_NA, 1.0), (D_NA, 1.0))
    lat_dtypes = (F32, F32, BF16, BF16, BF16)
    kv_splits = ((D_NA, 1.0), (D_NA, 1.0))
    b_in3 = b_in[:, None, :]

    xc = ctx
    for l in range(DEPTH):
        last = l == DEPTH - 1
        m_lat = [(mods, (l * N_MOD + k) * MOD_ROWS, True) for k in range(N_MOD)]
        m_ctx = [(mods, (l * N_MOD + k) * MOD_ROWS + bsz, False) for k in range(N_MOD)]
        b_out_l = b_out[l][None, :]
        wr = jnp.pad(w_router[l], ((0, 0), (0, LANES - N_EXPERTS)))
        wr_hi = wr.astype(BF16)
        wr_hl = jnp.concatenate([wr_hi, (wr - wr_hi.astype(F32)).astype(BF16)], axis=1)
        g1, b1 = g_post1[l][None, :], b_post1[l][None, :]
        g2, b2 = g_post2[l][None, :], b_post2[l][None, :]
        conv_w = (w_short[l], w_conf_dw[l], b_conf_dw[l], g_conf_ln[l], b_conf_ln[l])

        flat = lambda t: t.reshape(1, bsz * nctx, t.shape[-1])
        unflat = lambda t: t.reshape(bsz, nctx, t.shape[-1])
        if last:
            k_c, v_c = map(unflat, _inproj(flat(xc), m_ctx[0], m_ctx[1], w_in, b_in3, l, OFF_K,
                                           kv_splits, (BF16, BF16), tm=512))
        else:
            uac, ubc, q_c, k_c, v_c = map(unflat, _inproj(flat(xc), m_ctx[0], m_ctx[1], w_in, b_in3, l, 0,
                                                          lat_splits, lat_dtypes, tm=512))

        ua, ub, q, k, v = _inproj(x, m_lat[0], m_lat[1], w_in, b_in3, l, 0, lat_splits, lat_dtypes, tm=512)
        yab = _conv_mixers(ua, ub, *conv_w)
        yc = _neighbourhood_attention(q, k, v, k_c, v_c, na_rpb[l])
        x_mid, hm, logits = _outproj(yab, yc, x, w_out, l, b_out_l, m_lat[2], g1, b1, m_lat[3], m_lat[4],
                                     wr_hi, wr_hl, tm=1024)

        idx, gates = _slot_lists(*_route(logits, cap, 0), cap, merge=False)
        idx = idx.reshape(-1)
        x_e = _dispatch(idx, hm, bsz, seq, d, cap)

        x_ec = gates_c = None
        if not last:
            yabc = _conv_mixers(uac, ubc, *conv_w)
            ycc = _context_attention(q_c, k_c, v_c)
            xc_mid, hmc, logits_c = _outproj(flat(yabc), flat(ycc), flat(xc), w_out, l, b_out_l, m_ctx[2], g1, b1,
                                             m_ctx[3], m_ctx[4], wr_hi, wr_hl, tm=512)
            n_c = bsz * cap_ctx
            idx_c, gates_c = _slot_lists(*_route(unflat(logits_c), cap_ctx, cap_ctx), n_c, merge=True)
            idx_c = idx_c.reshape(-1)
            x_ec = _dispatch(idx_c, hmc, 1, bsz * nctx, d, n_c)

        y_e, y_ec = _experts(x_e, gates, x_ec, gates_c, w_gate, w_up, w_down, l, tm=512)
        x = _combine(idx, y_e, x_mid, m_lat[5], g2, b2, cap)
        if not last:
            xc = unflat(_combine(idx_c, y_ec, xc_mid, m_ctx[5], g2, b2, n_c))
    return x
```

```python
import functools
import math

import numpy as np
import jax
import jax.numpy as jnp
from jax import lax
from jax.experimental import pallas as pl
from jax.experimental.pallas import tpu as pltpu

F32 = jnp.float32
BF16 = jnp.bfloat16

D_MODEL = 1024
DEPTH = 2
GRID_W = 64
D_CONV = D_MODEL // 4
D_CONF = D_MODEL // 4
NA_HEAD_DIM = 64
D_NA = D_MODEL - D_CONV - D_CONF
N_NA_HEADS = D_NA // NA_HEAD_DIM
SHORT_CONV_W = 3
CONF_CONV_W = 31
NA_WIN_ROWS_MAX = 8
NA_WIN_COLS = 16
N_EXPERTS = 16
EC_CAPACITY_FACTOR = 2
D_EXPERT = 1024
LN_EPS = 1e-5
DEEPNORM_ALPHA = (2.0 * DEPTH) ** 0.25
NEG_INF = -1e30
LOG2E = math.log2(math.e)

OFF_A = 0
OFF_B = OFF_A + 3 * D_CONV
OFF_Q = OFF_B + 2 * D_CONF
OFF_K = OFF_Q + D_NA
OFF_V = OFF_K + D_NA
D_IN = OFF_V + D_NA

LANES = 128
SUBLANES = 8
MOD_ROWS = 16
VMEM_LIMIT = 56 * 1024 * 1024
ATTN_ROWS = 16
N_PAIRS = D_NA // LANES
ROW_GROUP = 256
OUT_ROW_GROUP = 128
HI = lax.Precision.HIGHEST


def _cparams(sem):
    return pltpu.CompilerParams(dimension_semantics=sem, vmem_limit_bytes=VMEM_LIMIT)


def _ln(x):
    mu = jnp.mean(x, axis=-1, keepdims=True)
    xc = x - mu
    var = jnp.mean(xc * xc, axis=-1, keepdims=True)
    return xc * lax.rsqrt(var + LN_EPS)


def _sigmoid(x):
    return 1.0 / (1.0 + jnp.exp(-x))


def _mod_kernel(cond_ref, w_ref, b_ref, o_ref):
    s = cond_ref[...]
    s = s * _sigmoid(s)
    w = w_ref[0]
    s_hi, w_hi = s.astype(BF16), w.astype(BF16)
    s_lo = (s - s_hi.astype(F32)).astype(BF16)
    w_lo = (w - w_hi.astype(F32)).astype(BF16)
    o_ref[0] = (jnp.dot(s_hi, w_hi, preferred_element_type=F32) + jnp.dot(s_lo, w_hi, preferred_element_type=F32)
                + jnp.dot(s_hi, w_lo, preferred_element_type=F32) + b_ref[0])


N_MOD = 6


def _modulation(cond, w_mod, b_mod):
    n_l, d, n = w_mod.shape
    out = pl.pallas_call(
        _mod_kernel,
        out_shape=jax.ShapeDtypeStruct((n_l * N_MOD, MOD_ROWS, d), F32),
        grid=(n_l, N_MOD),
        in_specs=[
            pl.BlockSpec((MOD_ROWS, d), lambda l, k: (0, 0)),
            pl.BlockSpec((1, d, d), lambda l, k: (l, 0, k)),
            pl.BlockSpec((1, 1, d), lambda l, k: (l, 0, k)),
        ],
        out_specs=pl.BlockSpec((1, MOD_ROWS, d), lambda l, k: (l * N_MOD + k, 0, 0)),
        compiler_params=_cparams(("arbitrary", "arbitrary")),
        name="modulation",
    )(cond, w_mod, b_mod.reshape(n_l, 1, n))
    return out.reshape(n_l * N_MOD * MOD_ROWS, 1, d)


def _mod_spec(mod, d):
    _, row0, per_sample = mod
    if per_sample:
        return pl.BlockSpec((1, 1, d), lambda bi, *_: (row0 + bi, 0, 0))
    return pl.BlockSpec((1, 1, d), lambda bi, *_: (row0, 0, 0))


def _first_step():
    return (pl.program_id(0) == 0) & (pl.program_id(1) == 0)


def _inproj_kernel(x_ref, sh_ref, sc_ref, w_ref, b_ref, *rest, splits, col0):
    *o_refs, w_bf = rest

    @pl.when(_first_step())
    def _():
        w_bf[...] = w_ref[0].astype(BF16)

    tm = x_ref.shape[1]
    n_part = max(1, tm // ROW_GROUP)
    rows_per = tm // n_part
    groups = [slice(part_i * rows_per, (part_i + 1) * rows_per) for part_i in range(n_part)]
    hs = [(_ln(x_ref[0, rows, :]) * (1.0 + sc_ref[0]) + sh_ref[0]).astype(BF16) for rows in groups]
    for rows, h in zip(groups, hs):
        u = jnp.dot(h, w_bf[:, col0:], preferred_element_type=F32) + b_ref[0, :, col0:]
        off = 0
        for o_ref, (width, scale) in zip(o_refs, splits):
            part = u[:, off:off + width]
            if scale != 1.0:
                part = part * scale
            o_ref[0, rows, :] = part.astype(o_ref.dtype)
            off += width


def _inproj(x, sh, sc, w, b, layer, col0, splits, dtypes, tm):
    nb, s, d = x.shape
    n = w.shape[2]
    return pl.pallas_call(
        functools.partial(_inproj_kernel, splits=splits, col0=col0),
        out_shape=[jax.ShapeDtypeStruct((nb, s, wd), dt) for (wd, _), dt in zip(splits, dtypes)],
        grid=(nb, s // tm),
        in_specs=[
            pl.BlockSpec((1, tm, d), lambda bi, i: (bi, i, 0)),
            _mod_spec(sh, d),
            _mod_spec(sc, d),
            pl.BlockSpec((1, d, n), lambda bi, i: (layer, 0, 0), pipeline_mode=pl.Buffered(1)),
            pl.BlockSpec((1, 1, n), lambda bi, i: (layer, 0, 0)),
        ],
        out_specs=[pl.BlockSpec((1, tm, wd), lambda bi, i: (bi, i, 0)) for wd, _ in splits],
        scratch_shapes=[pltpu.VMEM((d, n), BF16)],
        compiler_params=_cparams(("arbitrary", "arbitrary")),
        name="inproj",
    )(x, sh[0], sc[0], w, b)


CONV_CHUNK = 128
Z_PAD = 8
H_PAD = 16


def _conv_kernel(ua_ref, ub_ref, ws_ref, wd_ref, bd_ref, g_ref, b_ref, o_ref, z_scr, h_scr, *, seq):
    c = D_CONV
    z_scr[0:Z_PAD, :] = jnp.zeros((Z_PAD, c), F32)
    z_scr[Z_PAD + seq:2 * Z_PAD + seq, :] = jnp.zeros((Z_PAD, c), F32)
    h_scr[0, 0:H_PAD, :] = jnp.zeros((H_PAD, c), F32)
    h_scr[0, H_PAD + seq:2 * H_PAD + seq, :] = jnp.zeros((H_PAD, c), F32)
    z_scr[Z_PAD:Z_PAD + seq, :] = ua_ref[0, :, c:2 * c] * ua_ref[0, :, 2 * c:3 * c]
    h_scr[0, H_PAD:H_PAD + seq, :] = ub_ref[0, :, 0:c] * _sigmoid(ub_ref[0, :, c:2 * c])
    n_rows = seq + 2 * H_PAD
    h_all = h_scr[0]
    for r in range(1, SUBLANES):
        h_scr[r] = pltpu.roll(h_all, n_rows - r, axis=0)
    tc = min(CONV_CHUNK, seq)
    for ci in range(seq // tc):
        t0 = ci * tc
        acc = ws_ref[0:1, :] * z_scr[t0 + Z_PAD - 1:t0 + Z_PAD - 1 + tc, :]
        for j in range(1, SHORT_CONV_W):
            s0 = t0 + Z_PAD - 1 + j
            acc = acc + ws_ref[j:j + 1, :] * z_scr[s0:s0 + tc, :]
        ya = ua_ref[0, t0:t0 + tc, 0:c] * acc
        hb = bd_ref[...]
        for j in range(CONF_CONV_W):
            s0 = t0 + H_PAD - CONF_CONV_W // 2 + j
            a0 = s0 - s0 % SUBLANES
            hb = hb + wd_ref[j:j + 1, :] * h_scr[s0 % SUBLANES, a0:a0 + tc, :]
        hn = _ln(hb) * g_ref[...] + b_ref[...]
        yb = hn * _sigmoid(hn)
        o_ref[0, t0:t0 + tc, 0:c] = ya.astype(o_ref.dtype)
        o_ref[0, t0:t0 + tc, c:2 * c] = yb.astype(o_ref.dtype)


def _conv_mixers(ua, ub, w_short, w_dw, b_dw, g_ln, b_ln):
    nb, s, _ = ua.shape
    c = D_CONV
    full = lambda shape: pl.BlockSpec(shape, lambda bi: (0,) * len(shape))
    return pl.pallas_call(
        functools.partial(_conv_kernel, seq=s),
        out_shape=jax.ShapeDtypeStruct((nb, s, 2 * c), BF16),
        grid=(nb,),
        in_specs=[
            pl.BlockSpec((1, s, 3 * c), lambda bi: (bi, 0, 0)),
            pl.BlockSpec((1, s, 2 * c), lambda bi: (bi, 0, 0)),
            full((SHORT_CONV_W, c)), full((CONF_CONV_W, c)), full((1, c)), full((1, c)), full((1, c)),
        ],
        out_specs=pl.BlockSpec((1, s, 2 * c), lambda bi: (bi, 0, 0)),
        scratch_shapes=[pltpu.VMEM((s + 2 * Z_PAD, c), F32), pltpu.VMEM((SUBLANES, s + 2 * H_PAD, c), F32)],
        compiler_params=_cparams(("arbitrary",)),
        name="conv_mixers",
    )(ua, ub, w_short, w_dw, b_dw.reshape(1, c), g_ln.reshape(1, c), b_ln.reshape(1, c))


SUB_ROWS = 2
WIN_ROWS = SUB_ROWS + NA_WIN_ROWS_MAX - 1
BAND_KEYS = WIN_ROWS * GRID_W
SUB_Q = SUB_ROWS * GRID_W
N_DROW = 2 * NA_WIN_ROWS_MAX - 1
N_DCOL = 2 * NA_WIN_COLS - 1


def _bias_kernel(rpb_ref, nxt_ref, o_ref):
    n_cols = o_ref.shape[1]
    col = lax.broadcasted_iota(jnp.int32, (LANES, n_cols), 1)
    qc = col >> (LANES.bit_length() - 1)
    kc = col & (GRID_W - 1)
    d_col = jnp.clip(kc - qc + (NA_WIN_COLS - 1), 0, N_DCOL - 1)
    onehot = (lax.broadcasted_iota(jnp.int32, (LANES, n_cols), 0) == d_col).astype(F32)
    own = jnp.dot(rpb_ref[...], onehot, preferred_element_type=F32, precision=HI) * LOG2E
    nxt = jnp.dot(nxt_ref[...], onehot, preferred_element_type=F32, precision=HI) * LOG2E
    c0 = jnp.clip(qc - NA_WIN_COLS // 2, 0, GRID_W - NA_WIN_COLS)
    inside = (kc >= c0) & (kc < c0 + NA_WIN_COLS)
    second = (col & GRID_W) != 0
    o_ref[0:LANES, :] = jnp.where(inside, jnp.where(second, nxt, own), NEG_INF)
    o_ref[LANES:2 * LANES, :] = jnp.where(inside & ~second, own, NEG_INF)
    o_ref[2 * LANES:3 * LANES, :] = jnp.where(inside & second, own, NEG_INF)
    o_ref[3 * LANES:, :] = jnp.full((o_ref.shape[0] - 3 * LANES, n_cols), NEG_INF, F32)


N_SLABS = -(-WIN_ROWS // 2)
TILES_PER_KIND = LANES
MASKED_TILE = 3 * TILES_PER_KIND


def _na_plan(rows):
    wr = min(NA_WIN_ROWS_MAX, rows)
    assert wr == NA_WIN_ROWS_MAX and rows % SUB_ROWS == 0 and rows >= WIN_ROWS
    row_start = np.clip(np.arange(rows) - wr // 2, 0, rows - wr)
    w0s, tiles = [], []
    for r0 in range(0, rows, SUB_ROWS):
        w0 = int(np.clip(r0 - wr // 2, 0, rows - WIN_ROWS))
        for iq in range(SUB_ROWS):
            r = r0 + iq
            assert row_start[r] >= w0 and row_start[r] + wr <= w0 + WIN_ROWS
            ok = lambda w: w < WIN_ROWS and row_start[r] <= w0 + w < row_start[r] + wr
            d_row = lambda w: w0 + w - r + NA_WIN_ROWS_MAX - 1
            for j in range(N_SLABS):
                lo, hi = ok(2 * j), ok(2 * j + 1)
                if lo and hi:
                    tiles.append(d_row(2 * j))
                elif lo:
                    tiles.append(TILES_PER_KIND + d_row(2 * j))
                elif hi:
                    tiles.append(2 * TILES_PER_KIND + d_row(2 * j + 1))
                else:
                    tiles.append(-1)
        w0s.append(w0)
    return np.array(w0s, np.int32), np.array(tiles, np.int32)


def _na_bias(rpb):
    n_hd = N_NA_HEADS * N_DROW
    assert GRID_W == 64 and 2 * GRID_W == LANES and n_hd <= TILES_PER_KIND and N_DCOL <= LANES
    rpb = rpb.astype(F32)
    pad = lambda t: jnp.pad(t.reshape(n_hd, N_DCOL), ((0, LANES - n_hd), (0, LANES - N_DCOL)))
    nxt = jnp.concatenate([rpb[:, 1:], jnp.zeros_like(rpb[:, :1])], axis=1)
    n_tiles = 3 * TILES_PER_KIND + SUBLANES
    table = pl.pallas_call(
        _bias_kernel,
        out_shape=jax.ShapeDtypeStruct((n_tiles, GRID_W * LANES), F32),
        compiler_params=pltpu.CompilerParams(vmem_limit_bytes=VMEM_LIMIT),
        name="na_bias",
    )(pad(rpb), pad(nxt))
    return table.reshape(n_tiles, GRID_W, LANES)


def _lane_reduce(xs, combine, reduce, neutral):
    chunks = []
    for x in xs:
        rows, n = x.shape
        n_full = n // LANES
        chunks += [x[:, j * LANES:(j + 1) * LANES] for j in range(n_full)]
        if n % LANES:
            fill = jnp.full((rows, LANES - n % LANES), neutral, x.dtype)
            chunks.append(jnp.concatenate([x[:, n_full * LANES:], fill], axis=1))
    return reduce(functools.reduce(combine, chunks), axis=-1, keepdims=True)


def _attn_kernel(w0_ref, tile_ref, q_ref, k_ref, v_ref, kc_ref, vc_ref, *rest, banded, n_sub, sub_q):
    if banded:
        bias_ref, o_ref = rest
    else:
        (o_ref,) = rest
    lane = lax.broadcasted_iota(jnp.int32, (sub_q, LANES), 1)
    first = lane < NA_HEAD_DIM
    nt = (((1,), (1,)), ((), ()))
    stages = [(si, p) for si in range(n_sub) for p in range(N_PAIRS)]

    def window(si):
        blk = pl.program_id(1) * n_sub + si
        return blk, pl.multiple_of(w0_ref[blk] * GRID_W, GRID_W)

    def bias(blk, head):
        row_blocks = []
        for iq in range(SUB_ROWS):
            slabs = []
            for j in range(N_SLABS):
                t = tile_ref[(blk * SUB_ROWS + iq) * N_SLABS + j]
                tile = bias_ref[jnp.where(t < 0, MASKED_TILE, t + head * N_DROW)]
                width = min(LANES, BAND_KEYS - j * LANES)
                slabs.append(tile[:, :width])
            row_blocks.append(jnp.concatenate(slabs, axis=1))
        return jnp.concatenate(row_blocks, axis=0)

    def scores(si, p):
        cols = slice(p * LANES, (p + 1) * LANES)
        q_p = q_ref[0, si * sub_q:(si + 1) * sub_q, cols]
        zero = jnp.zeros_like(q_p)
        qq = jnp.concatenate([jnp.where(first, q_p, zero), jnp.where(first, zero, q_p)], axis=0)
        parts = [lax.dot_general(qq, kc_ref[0, :, cols], nt, preferred_element_type=F32)]
        if banded:
            blk, start = window(si)
            both = jnp.concatenate([bias(blk, 2 * p), bias(blk, 2 * p + 1)], axis=0)
            parts.append(lax.dot_general(qq, k_ref[0, pl.ds(start, BAND_KEYS), cols], nt,
                                         preferred_element_type=F32) + both)
        return parts

    def weights(parts):
        m = _lane_reduce(parts, jnp.maximum, jnp.max, NEG_INF)
        es = [jnp.exp2(s - m) for s in parts]
        den = _lane_reduce(es, jnp.add, jnp.sum, 0.0)
        return [e.astype(BF16) for e in es], den

    def values(si, p, es, den):
        cols = slice(p * LANES, (p + 1) * LANES)
        o = jnp.dot(es[0], vc_ref[0, :, cols], preferred_element_type=F32)
        if banded:
            _, start = window(si)
            o = o + jnp.dot(es[1], v_ref[0, pl.ds(start, BAND_KEYS), cols], preferred_element_type=F32)
        o = o * (1.0 / den)
        out = jnp.where(first, o[:sub_q], o[sub_q:])
        o_ref[0, si * sub_q:(si + 1) * sub_q, cols] = out.astype(o_ref.dtype)

    nxt = scores(*stages[0])
    pending = None
    for i, (si, p) in enumerate(stages):
        cur = nxt
        if i + 1 < len(stages):
            nxt = scores(*stages[i + 1])
        if pending is not None:
            values(*pending)
        es, den = weights(cur)
        pending = (si, p, es, den)
    values(*pending)


def _neighbourhood_attention(q, k, v, kc, vc, rpb):
    nb, s, dn = q.shape
    rows = s // GRID_W
    nctx = kc.shape[1]
    w0s, tiles = _na_plan(rows)
    bias = _na_bias(rpb)
    n_sub = ATTN_ROWS // SUB_ROWS
    m_rows = ATTN_ROWS * GRID_W
    grid_spec = pltpu.PrefetchScalarGridSpec(
        num_scalar_prefetch=2,
        grid=(nb, rows // ATTN_ROWS),
        in_specs=[
            pl.BlockSpec((1, m_rows, dn), lambda bi, i, w0, pat: (bi, i, 0)),
            pl.BlockSpec((1, s, dn), lambda bi, i, w0, pat: (bi, 0, 0)),
            pl.BlockSpec((1, s, dn), lambda bi, i, w0, pat: (bi, 0, 0)),
            pl.BlockSpec((1, nctx, dn), lambda bi, i, w0, pat: (bi, 0, 0)),
            pl.BlockSpec((1, nctx, dn), lambda bi, i, w0, pat: (bi, 0, 0)),
            pl.BlockSpec(bias.shape, lambda bi, i, w0, pat: (0, 0, 0), pipeline_mode=pl.Buffered(1)),
        ],
        out_specs=pl.BlockSpec((1, m_rows, dn), lambda bi, i, w0, pat: (bi, i, 0)),
    )
    return pl.pallas_call(
        functools.partial(_attn_kernel, banded=True, n_sub=n_sub, sub_q=SUB_Q),
        out_shape=jax.ShapeDtypeStruct((nb, s, dn), BF16),
        grid_spec=grid_spec,
        compiler_params=_cparams(("arbitrary", "arbitrary")),
        name="neighbourhood_attention",
    )(jnp.asarray(w0s), jnp.asarray(tiles), q, k, v, kc, vc, bias)


def _context_attention(q, kc, vc):
    nb, s, dn = q.shape
    spec = pl.BlockSpec((1, s, dn), lambda bi, i, w0, pat: (bi, 0, 0))
    grid_spec = pltpu.PrefetchScalarGridSpec(
        num_scalar_prefetch=2, grid=(nb, 1), in_specs=[spec] * 5, out_specs=spec)
    dummy = jnp.zeros((1,), jnp.int32)
    return pl.pallas_call(
        functools.partial(_attn_kernel, banded=False, n_sub=1, sub_q=s),
        out_shape=jax.ShapeDtypeStruct((nb, s, dn), BF16),
        grid_spec=grid_spec,
        compiler_params=_cparams(("arbitrary", "arbitrary")),
        name="context_attention",
    )(dummy, dummy, q, kc, vc, kc, vc)


def _store_rowmajor(ref, val, base=0):
    n, width = val.shape
    chunks = width // LANES
    for c in range(chunks):
        ref[pl.ds(base + c, n, stride=chunks), :] = val[:, c * LANES:(c + 1) * LANES]


def _load_rowmajor(ref, base, n, chunks):
    return jnp.concatenate([ref[pl.ds(base + c, n, stride=chunks), :] for c in range(chunks)], axis=1)


def _outproj_kernel(yab_ref, yc_ref, x_ref, w_ref, bo_ref, gt_ref, g_ref, b_ref, sh_ref, sc_ref,
                    wrh_ref, wrhl_ref, xmid_ref, hm_ref, lg_ref, w_bf):
    @pl.when(_first_step())
    def _():
        w_bf[...] = w_ref[0].astype(BF16)

    half = yab_ref.shape[2]
    tm = x_ref.shape[1]
    chunks = x_ref.shape[2] // LANES
    n_part = max(1, tm // OUT_ROW_GROUP)
    rows_per = tm // n_part
    groups = [slice(part * rows_per, (part + 1) * rows_per) for part in range(n_part)]
    ys = [jnp.dot(yab_ref[0, rows, :], w_bf[0:half, :], preferred_element_type=F32)
          + jnp.dot(yc_ref[0, rows, :], w_bf[half:, :], preferred_element_type=F32) + bo_ref[...]
          for rows in groups]
    for part, (rows, y) in enumerate(zip(groups, ys)):
        xm = _ln(DEEPNORM_ALPHA * x_ref[0, rows, :] + (1.0 + gt_ref[0]) * y) * g_ref[...] + b_ref[...]
        xmid_ref[0, rows, :] = xm
        hm = _ln(xm) * (1.0 + sc_ref[0]) + sh_ref[0]
        _store_rowmajor(hm_ref, hm, base=part * rows_per * chunks)
        hm_hi = hm.astype(BF16)
        hm_lo = (hm - hm_hi.astype(F32)).astype(BF16)
        both = jnp.dot(hm_hi, wrhl_ref[...], preferred_element_type=F32)
        lg_ref[0, rows, :] = (both[:, :LANES] + both[:, LANES:]
                              + jnp.dot(hm_lo, wrh_ref[...], preferred_element_type=F32))


def _outproj(yab, yc, x, w, layer, bo, gt, g, b, sh, sc, wr_hi, wr_hl, tm):
    nb, s, d = x.shape
    half = yab.shape[2]
    vec = pl.BlockSpec((1, d), lambda bi, i: (0, 0))
    tok = lambda width: pl.BlockSpec((1, tm, width), lambda bi, i: (bi, i, 0))
    n_i = s // tm
    return pl.pallas_call(
        _outproj_kernel,
        out_shape=[jax.ShapeDtypeStruct((nb, s, d), F32),
                   jax.ShapeDtypeStruct((nb * s * (d // LANES), LANES), F32),
                   jax.ShapeDtypeStruct((nb, s, LANES), F32)],
        grid=(nb, n_i),
        in_specs=[tok(half), tok(half), tok(d),
                  pl.BlockSpec((1, d, d), lambda bi, i: (layer, 0, 0), pipeline_mode=pl.Buffered(1)),
                  vec, _mod_spec(gt, d), vec, vec,
                  _mod_spec(sh, d), _mod_spec(sc, d), pl.BlockSpec((d, LANES), lambda bi, i: (0, 0)),
                  pl.BlockSpec((d, 2 * LANES), lambda bi, i: (0, 0))],
        out_specs=[tok(d), pl.BlockSpec((tm * (d // LANES), LANES), lambda bi, i: (bi * n_i + i, 0)),
                   tok(LANES)],
        scratch_shapes=[pltpu.VMEM((d, d), BF16)],
        compiler_params=_cparams(("arbitrary", "arbitrary")),
        name="outproj_postnorm",
    )(yab, yc, x, w, bo, gt[0], g, b, sh[0], sc[0], wr_hi, wr_hl)


CUM_CHUNK = 256
F32_EXP_BIAS = 127
F32_MANT_BITS = 23


def _prefix_count(mask_f32, tri):
    rows, n = mask_f32.shape
    tc = min(CUM_CHUNK, n)
    base = jnp.zeros((rows, 1), F32)
    parts = []
    for ci in range(n // tc):
        blk = mask_f32[:, ci * tc:(ci + 1) * tc]
        parts.append(jnp.dot(blk.astype(BF16), tri[:tc, :tc], preferred_element_type=F32) + base)
        base = base + jnp.sum(blk, axis=-1, keepdims=True)
    return jnp.concatenate(parts, axis=-1)


def _pow2(k):
    return pltpu.bitcast((k + F32_EXP_BIAS) << F32_MANT_BITS, F32)


def _route_kernel(lg_ref, slot_c_ref, gate_t_ref, *, cap, slot_stride):
    nb = lg_ref.shape[0]
    assert nb * N_EXPERTS == LANES
    rows = []
    for b in range(nb):
        lg = lg_ref[b]
        lane = lax.broadcasted_iota(jnp.int32, lg.shape, 1)
        lgm = jnp.where(lane < N_EXPERTS, lg, NEG_INF)
        ex = jnp.exp(lgm - jnp.max(lgm, axis=-1, keepdims=True))
        aff = ex / jnp.sum(ex, axis=-1, keepdims=True)
        rows.append(aff.T[0:N_EXPERTS, :])
    a = jnp.concatenate(rows, axis=0)
    capf = float(cap)

    def enough(t):
        return jnp.sum((a >= t).astype(F32), axis=-1, keepdims=True) >= capf

    def exp_step(_, carry):
        lo, hi = carry
        mid = lo + ((hi - lo + 1) >> 1)
        ok = enough(_pow2(mid))
        return jnp.where(ok, mid, lo), jnp.where(ok, hi, mid - 1)

    k_lo = jnp.full((LANES, 1), -F32_EXP_BIAS, jnp.int32)
    k_hi = jnp.zeros((LANES, 1), jnp.int32)
    k_lo, _ = lax.fori_loop(0, 7, exp_step, (k_lo, k_hi))
    base = _pow2(k_lo)

    def mant_step(_, carry):
        t, step = carry
        step = step * 0.5
        cand = t + step
        return jnp.where(enough(cand), cand, t), step

    thr, _ = lax.fori_loop(0, F32_MANT_BITS, mant_step, (base, base))

    r_i = lax.broadcasted_iota(jnp.int32, (CUM_CHUNK, CUM_CHUNK), 0)
    c_i = lax.broadcasted_iota(jnp.int32, (CUM_CHUNK, CUM_CHUNK), 1)
    tri = (r_i < c_i).astype(BF16)
    gt = (a > thr).astype(F32)
    eq = (a == thr).astype(F32)
    need = capf - jnp.sum(gt, axis=-1, keepdims=True)
    sel = gt + eq * (_prefix_count(eq, tri) < need).astype(F32)
    pos = _prefix_count(sel, tri)
    sample = lax.broadcasted_iota(jnp.int32, (LANES, 1), 0) >> (N_EXPERTS.bit_length() - 1)
    slot = jnp.where(sel > 0.0, pos + (sample * slot_stride).astype(F32), -1.0)
    for b in range(nb):
        lo = b * N_EXPERTS
        gate_t_ref[b] = a[lo:lo + N_EXPERTS, :]
        rolled = slot if b == 0 else jnp.concatenate([slot[lo:, :], slot[:lo, :]], axis=0)
        slot_c_ref[b] = rolled.T


def _route(logits, cap, slot_stride):
    nb, s, _ = logits.shape
    whole = lambda shape: pl.BlockSpec(shape, lambda i: (0,) * len(shape))
    return pl.pallas_call(
        functools.partial(_route_kernel, cap=cap, slot_stride=slot_stride),
        out_shape=[jax.ShapeDtypeStruct((nb, s, LANES), F32), jax.ShapeDtypeStruct((nb, N_EXPERTS, s), F32)],
        grid=(1,),
        in_specs=[whole((nb, s, LANES))],
        out_specs=[whole((nb, s, LANES)), whole((nb, N_EXPERTS, s))],
        compiler_params=_cparams(("arbitrary",)),
        name="route",
    )(logits)


TOK_SPLIT = 64


def _slot_list_kernel(slot_ref, gate_ref, idx_ref, g_ref, *, n_slots, tok_stride, merge):
    s = slot_ref.shape[1]
    b = pl.program_id(0)
    assert n_slots <= 256
    slot_id = lax.broadcasted_iota(jnp.int32, (s, n_slots), 1).astype(F32).astype(BF16)
    one, zero = jnp.ones((s, n_slots), BF16), jnp.zeros((s, n_slots), BF16)
    tok = lax.broadcasted_iota(jnp.int32, (1, s), 1) + b * tok_stride
    tok_hi = (tok >> (TOK_SPLIT.bit_length() - 1)).astype(F32)
    tok_lo = (tok & (TOK_SPLIT - 1)).astype(F32)
    zeros = jnp.zeros((SUBLANES - 5, s), F32)
    idx_rows, g_rows = [], []
    for e in range(N_EXPERTS):
        taken = jnp.broadcast_to(slot_ref[0, :, e:e + 1].astype(BF16), (s, n_slots))
        hit = jnp.where(taken == slot_id, one, zero)
        g0 = gate_ref[0, e:e + 1, :]
        g_hi = g0.astype(BF16).astype(F32)
        g_mid = (g0 - g_hi).astype(BF16).astype(F32)
        g_lo = g0 - g_hi - g_mid
        lhs = jnp.concatenate([tok_hi, tok_lo, g_hi, g_mid, g_lo, zeros], axis=0).astype(BF16)
        out = jnp.dot(lhs, hit, preferred_element_type=F32)
        idx_rows.append(out[0:1] * float(TOK_SPLIT) + out[1:2])
        g_rows.append(out[2:3] + out[3:4] + out[4:5])
    idx = jnp.concatenate(idx_rows, axis=0).astype(jnp.int32) * SUBLANES
    g = jnp.concatenate(g_rows + [jnp.zeros((LANES - N_EXPERTS, n_slots), F32)], axis=0)
    g_t = g.T
    g_cols = [jnp.broadcast_to(g_t[:, e:e + 1], (n_slots, LANES)) for e in range(N_EXPERTS)]
    if merge:
        @pl.when(b == 0)
        def _():
            idx_ref[0] = idx
            for e in range(N_EXPERTS):
                g_ref[e] = g_cols[e]

        @pl.when(b > 0)
        def _():
            idx_ref[0] = idx_ref[0] + idx
            for e in range(N_EXPERTS):
                g_ref[e] = g_ref[e] + g_cols[e]
    else:
        idx_ref[0] = idx
        for e in range(N_EXPERTS):
            g_ref[e] = g_cols[e]


def _slot_lists(slot_c, gate_t, n_slots, merge):
    nb, s, _ = slot_c.shape
    nbo = 1 if merge else nb
    idx_map = (lambda bi: (0, 0, 0)) if merge else (lambda bi: (bi, 0, 0))
    g_map = (lambda bi: (0, 0, 0)) if merge else (lambda bi: (0, bi, 0))
    return pl.pallas_call(
        functools.partial(_slot_list_kernel, n_slots=n_slots, tok_stride=s if merge else 0, merge=merge),
        out_shape=[jax.ShapeDtypeStruct((nbo, N_EXPERTS, n_slots), jnp.int32),
                   jax.ShapeDtypeStruct((N_EXPERTS, nbo * n_slots, LANES), F32)],
        grid=(nb,),
        in_specs=[pl.BlockSpec((1, s, LANES), lambda bi: (bi, 0, 0)),
                  pl.BlockSpec((1, N_EXPERTS, s), lambda bi: (bi, 0, 0))],
        out_specs=[pl.BlockSpec((1, N_EXPERTS, n_slots), idx_map),
                   pl.BlockSpec((N_EXPERTS, n_slots, LANES), g_map)],
        compiler_params=_cparams(("arbitrary",)),
        name="slot_lists",
    )(slot_c, gate_t)


ROW_UNROLL = 32


def _dispatch_kernel(idx_ref, hm_ref, x_ref, rows_scr, *, n_slots):
    b = pl.program_id(0)
    chunks = x_ref.shape[2] // LANES

    def emit(e, s0):
        src = pl.multiple_of(s0 * chunks, ROW_UNROLL * chunks)
        rows = _load_rowmajor(rows_scr.at[e % 2], src, ROW_UNROLL, chunks)
        x_ref[e, pl.ds(pl.multiple_of(s0, ROW_UNROLL), ROW_UNROLL), :] = rows.astype(x_ref.dtype)

    for e in range(N_EXPERTS + 1):
        base = (b * N_EXPERTS + e) * n_slots

        def move(i, carry):
            s0 = i * ROW_UNROLL
            if e < N_EXPERTS:
                first = base + s0
                tiles = [hm_ref[pl.ds(pl.multiple_of(idx_ref[first + u], chunks), chunks), :]
                         for u in range(ROW_UNROLL)]
                dst = pl.multiple_of(s0 * chunks, ROW_UNROLL * chunks)
                rows_scr[e % 2, pl.ds(dst, ROW_UNROLL * chunks), :] = jnp.concatenate(tiles, axis=0)
            if e > 0:
                emit(e - 1, s0)
            return carry

        lax.fori_loop(0, n_slots // ROW_UNROLL, move, 0)


def _dispatch(idx, hm_rm, nb, s, d, n_slots):
    chunks = d // LANES
    assert chunks == SUBLANES and n_slots % ROW_UNROLL == 0
    grid_spec = pltpu.PrefetchScalarGridSpec(
        num_scalar_prefetch=1,
        grid=(nb,),
        in_specs=[pl.BlockSpec((s * chunks, LANES), lambda bi, idx_r: (bi, 0))],
        out_specs=pl.BlockSpec((N_EXPERTS, n_slots, d), lambda bi, idx_r: (0, bi, 0)),
        scratch_shapes=[pltpu.VMEM((2, n_slots * chunks, LANES), F32)],
    )
    return pl.pallas_call(
        functools.partial(_dispatch_kernel, n_slots=n_slots),
        out_shape=jax.ShapeDtypeStruct((N_EXPERTS, nb * n_slots, d), BF16),
        grid_spec=grid_spec,
        compiler_params=_cparams(("arbitrary",)),
        name="dispatch",
    )(idx, hm_rm)


W_PARTS = 4

def _expert_kernel(*refs, n_main, has_extra):
    n_in = 4 if has_extra else 2
    n_out = 2 if has_extra else 1
    acts, w_parts = refs[:n_in], refs[n_in:n_in + 3]
    outs = refs[n_in + 3:n_in + 3 + n_out]
    w_bfs = refs[n_in + 3 + n_out:]
    if has_extra:
        x_ref, g_ref, xx_ref, gg_ref = acts
        y_ref, yy_ref = outs
    else:
        x_ref, g_ref = acts
        (y_ref,) = outs
    s = pl.program_id(0)
    i = pl.program_id(1)
    fill = s % 2
    cur = 1 - fill
    wg_bf, wu_bf, wd_bf = (w_bf.at[cur] for w_bf in w_bfs)

    def cast_part():
        block = jnp.minimum(i, W_PARTS - 1)
        for part, w_bf in zip(w_parts, w_bfs):
            rows = part.shape[2]
            w_bf[fill, pl.ds(pl.multiple_of(block * rows, rows), rows), :] = part[0, 0].astype(BF16)

    def ffn(x_in, gate_in, out_ref):
        rows_all = x_in.shape[1]
        chunks = x_in.shape[2] // LANES
        n_part = max(1, rows_all // ROW_GROUP)
        rows_per = rows_all // n_part
        groups = [slice(p * rows_per, (p + 1) * rows_per) for p in range(n_part)]
        ups = []
        for rows in groups:
            x_e = x_in[0, rows, :]
            ups.append((jnp.dot(x_e, wg_bf[...], preferred_element_type=F32),
                        jnp.dot(x_e, wu_bf[...], preferred_element_type=F32)))
        for p, (rows, (a, u)) in enumerate(zip(groups, ups)):
            h = (a * _sigmoid(a) * u).astype(BF16)
            y = jnp.dot(h, wd_bf[...], preferred_element_type=F32) * gate_in[0, rows, 0:1]
            _store_rowmajor(out_ref, y, base=p * rows_per * chunks)

    @pl.when(s == 0)
    def _():
        cast_part()

    if has_extra:
        @pl.when((s > 0) & (i < n_main))
        def _():
            cast_part()
            ffn(x_ref, g_ref, y_ref.at[0])

        @pl.when((s > 0) & (i == n_main))
        def _():
            cast_part()
            ffn(xx_ref, gg_ref, yy_ref.at[0])
    else:
        @pl.when(s > 0)
        def _():
            cast_part()
            ffn(x_ref, g_ref, y_ref.at[0])


def _experts(x_e, g_e, x_extra, g_extra, wg, wu, wd, layer, tm):
    _, rows, d = x_e.shape
    f = wg.shape[3]
    chunks = d // LANES
    n_main = rows // tm
    has_extra = x_extra is not None
    assert n_main >= W_PARTS and d % W_PARTS == 0 and f % W_PARTS == 0
    expert = lambda s: jnp.maximum(s - 1, 0)
    main_map = lambda s, i: (expert(s), jnp.where(s == 0, 0, jnp.minimum(i, n_main - 1)), 0)
    extra_map = lambda s, i: (expert(s), 0, 0)

    def w_spec(shape):
        blk = (1, 1, shape[0] // W_PARTS, shape[1])
        return pl.BlockSpec(blk, lambda s, i: (layer, jnp.minimum(s, N_EXPERTS - 1),
                                               jnp.minimum(i, W_PARTS - 1), 0))

    in_specs = [pl.BlockSpec((1, tm, d), main_map), pl.BlockSpec((1, tm, LANES), main_map)]
    out_specs = [pl.BlockSpec((1, tm * chunks, LANES), main_map)]
    out_shape = [jax.ShapeDtypeStruct((N_EXPERTS, rows * chunks, LANES), F32)]
    args = [x_e, g_e]
    if has_extra:
        rows2 = x_extra.shape[1]
        in_specs += [pl.BlockSpec((1, rows2, d), extra_map), pl.BlockSpec((1, rows2, LANES), extra_map)]
        out_specs.append(pl.BlockSpec((1, rows2 * chunks, LANES), extra_map))
        out_shape.append(jax.ShapeDtypeStruct((N_EXPERTS, rows2 * chunks, LANES), F32))
        args += [x_extra, g_extra]
    outs = pl.pallas_call(
        functools.partial(_expert_kernel, n_main=n_main, has_extra=has_extra),
        out_shape=out_shape,
        grid=(N_EXPERTS + 1, n_main + int(has_extra)),
        in_specs=in_specs + [w_spec((d, f)), w_spec((d, f)), w_spec((f, d))],
        out_specs=out_specs,
        scratch_shapes=[pltpu.VMEM((2, d, f), BF16), pltpu.VMEM((2, d, f), BF16), pltpu.VMEM((2, f, d), BF16)],
        compiler_params=_cparams(("arbitrary", "arbitrary")),
        name="experts",
    )(*args, wg, wu, wd)
    return outs if has_extra else (outs[0], None)


COMBINE_EXPERTS = 4
ADD_UNROLL = 16


def _combine_kernel(idx_ref, y_ref, x_ref, gt_ref, g_ref, b_ref, o_ref, acc, *, n_slots, n_samples):
    s = pl.program_id(0)
    j = pl.program_id(1)
    d = x_ref.shape[2]
    chunks = d // LANES
    per_step = x_ref.shape[1]
    fill = s % 2
    done = 1 - fill
    base = (s * N_EXPERTS + j * COMBINE_EXPERTS) * n_slots

    part = per_step // COMBINE_EXPERTS

    def scatter(eg):
        for g in range(n_slots // ADD_UNROLL):
            first = base + eg * n_slots + g * ADD_UNROLL
            y_rows = y_ref[eg, g * ADD_UNROLL * chunks:(g + 1) * ADD_UNROLL * chunks, :]
            new = []
            for u in range(ADD_UNROLL):
                dst = pl.multiple_of(idx_ref[first + u], chunks)
                new.append((dst, acc[fill, pl.ds(dst, chunks), :] + y_rows[u * chunks:(u + 1) * chunks, :]))
            for dst, val in new:
                acc[fill, pl.ds(dst, chunks), :] = val

    def summed(eg):
        row0 = pl.multiple_of((j * per_step + eg * part) * chunks, part * chunks)
        return _load_rowmajor(acc.at[done], row0, part, chunks)

    def post_norm(eg, ym):
        rows = slice(eg * part, (eg + 1) * part)
        z = DEEPNORM_ALPHA * x_ref[0, rows, :] + (1.0 + gt_ref[0]) * ym
        o_ref[0, rows, :] = _ln(z) * g_ref[...] + b_ref[...]

    @pl.when((s < n_samples) & (j == 0))
    def _():
        acc[fill] = jnp.zeros(acc.shape[1:], F32)

    @pl.when(s == 0)
    def _():
        for eg in range(COMBINE_EXPERTS):
            scatter(eg)

    @pl.when((s > 0) & (s < n_samples))
    def _():
        for eg in range(COMBINE_EXPERTS):
            ym = summed(eg)
            scatter(eg)
            post_norm(eg, ym)

    @pl.when(s == n_samples)
    def _():
        for eg in range(COMBINE_EXPERTS):
            post_norm(eg, summed(eg))


def _combine(idx, y_rm, x_mid, gt, g, b, n_slots):
    nb, s, d = x_mid.shape
    chunks = d // LANES
    n_groups = N_EXPERTS // COMBINE_EXPERTS
    per_step = s // n_groups
    assert chunks == SUBLANES and N_EXPERTS % COMBINE_EXPERTS == 0 and s % n_groups == 0
    assert per_step % SUBLANES == 0 and n_slots % ADD_UNROLL == 0
    prev = lambda si: jnp.maximum(si - 1, 0)
    tok_map = lambda si, j, i_r: (prev(si), jnp.where(si == 0, 0, j), 0)
    vec = pl.BlockSpec((1, d), lambda si, j, i_r: (0, 0))
    _, row0, per_sample = gt
    gt_spec = pl.BlockSpec((1, 1, d), (lambda si, j, i_r: (row0 + prev(si), 0, 0)) if per_sample
                           else (lambda si, j, i_r: (row0, 0, 0)))
    grid_spec = pltpu.PrefetchScalarGridSpec(
        num_scalar_prefetch=1,
        grid=(nb + 1, n_groups),
        in_specs=[
            pl.BlockSpec((COMBINE_EXPERTS, n_slots * chunks, LANES),
                         lambda si, j, i_r: (j, jnp.minimum(si, nb - 1), 0)),
            pl.BlockSpec((1, per_step, d), tok_map),
            gt_spec, vec, vec,
        ],
        out_specs=pl.BlockSpec((1, per_step, d), tok_map),
        scratch_shapes=[pltpu.VMEM((2, s * chunks, LANES), F32)],
    )
    return pl.pallas_call(
        functools.partial(_combine_kernel, n_slots=n_slots, n_samples=nb),
        out_shape=jax.ShapeDtypeStruct((nb, s, d), F32),
        grid_spec=grid_spec,
        compiler_params=_cparams(("arbitrary", "arbitrary")),
        name="combine_postnorm",
    )(idx, y_rm, x_mid, gt[0], g, b)


def kernel(x, c, ctx, c_ctx, w_mod, b_mod, w_in, b_in, w_short, w_conf_dw, b_conf_dw, g_conf_ln, b_conf_ln,
           na_rpb, w_out, b_out, g_post1, b_post1, w_router, w_gate, w_up, w_down, g_post2, b_post2):
    bsz, seq, d = x.shape
    nctx = ctx.shape[1]
    cap = EC_CAPACITY_FACTOR * seq // N_EXPERTS
    cap_ctx = EC_CAPACITY_FACTOR * nctx // N_EXPERTS
    q_scale = NA_HEAD_DIM ** -0.5 * LOG2E

    cond = jnp.concatenate([c, c_ctx[None, :], jnp.zeros((MOD_ROWS - bsz - 1, d), F32)], axis=0)
    mods = _modulation(cond, w_mod, b_mod)

    lat_splits = ((3 * D_CONV, 1.0), (2 * D_CONF, 1.0), (D_NA, q_scale), (D_NA, 1.0), (D_NA, 1.0))
    lat_dtypes = (F32, F32, BF16, BF16, BF16)
    kv_splits = ((D_NA, 1.0), (D_NA, 1.0))
    b_in3 = b_in[:, None, :]

    xc = ctx
    for l in range(DEPTH):
        last = l == DEPTH - 1
        m_lat = [(mods, (l * N_MOD + k) * MOD_ROWS, True) for k in range(N_MOD)]
        m_ctx = [(mods, (l * N_MOD + k) * MOD_ROWS + bsz, False) for k in range(N_MOD)]
        b_out_l = b_out[l][None, :]
        wr = jnp.pad(w_router[l], ((0, 0), (0, LANES - N_EXPERTS)))
        wr_hi = wr.astype(BF16)
        wr_hl = jnp.concatenate([wr_hi, (wr - wr_hi.astype(F32)).astype(BF16)], axis=1)
        g1, b1 = g_post1[l][None, :], b_post1[l][None, :]
        g2, b2 = g_post2[l][None, :], b_post2[l][None, :]
        conv_w = (w_short[l], w_conf_dw[l], b_conf_dw[l], g_conf_ln[l], b_conf_ln[l])

        flat = lambda t: t.reshape(1, bsz * nctx, t.shape[-1])
        unflat = lambda t: t.reshape(bsz, nctx, t.shape[-1])
        if last:
            k_c, v_c = map(unflat, _inproj(flat(xc), m_ctx[0], m_ctx[1], w_in, b_in3, l, OFF_K,
                                           kv_splits, (BF16, BF16), tm=512))
        else:
            uac, ubc, q_c, k_c, v_c = map(unflat, _inproj(flat(xc), m_ctx[0], m_ctx[1], w_in, b_in3, l, 0,
                                                          lat_splits, lat_dtypes, tm=512))

        ua, ub, q, k, v = _inproj(x, m_lat[0], m_lat[1], w_in, b_in3, l, 0, lat_splits, lat_dtypes, tm=1024)
        yab = _conv_mixers(ua, ub, *conv_w)
        yc = _neighbourhood_attention(q, k, v, k_c, v_c, na_rpb[l])
        x_mid, hm, logits = _outproj(yab, yc, x, w_out, l, b_out_l, m_lat[2], g1, b1, m_lat[3], m_lat[4],
                                     wr_hi, wr_hl, tm=1024)

        idx, gates = _slot_lists(*_route(logits, cap, 0), cap, merge=False)
        idx = idx.reshape(-1)
        x_e = _dispatch(idx, hm, bsz, seq, d, cap)

        x_ec = gates_c = None
        if not last:
            yabc = _conv_mixers(uac, ubc, *conv_w)
            ycc = _context_attention(q_c, k_c, v_c)
            xc_mid, hmc, logits_c = _outproj(flat(yabc), flat(ycc), flat(xc), w_out, l, b_out_l, m_ctx[2], g1, b1,
                                             m_ctx[3], m_ctx[4], wr_hi, wr_hl, tm=512)
            n_c = bsz * cap_ctx
            idx_c, gates_c = _slot_lists(*_route(unflat(logits_c), cap_ctx, cap_ctx), n_c, merge=True)
            idx_c = idx_c.reshape(-1)
            x_ec = _dispatch(idx_c, hmc, 1, bsz * nctx, d, n_c)

        y_e, y_ec = _experts(x_e, gates, x_ec, gates_c, w_gate, w_up, w_down, l, tm=512)
        x = _combine(idx, y_e, x_mid, m_lat[5], g2, b2, cap)
        if not last:
            xc = unflat(_combine(idx_c, y_ec, xc_mid, m_ctx[5], g2, b2, n_c))
    return x
```
